```python
import math
import jax, jax.numpy as jnp
from jax import lax
import numpy as np

D_MODEL = 1024
BATCH = 8
SEQ = 4096
DEPTH = 2

CHUNK = 64
HEAD_DIM = 64
D_CONV = D_MODEL // 2
N_SB_HEADS = 8
D_SB = N_SB_HEADS * HEAD_DIM
D_MIX = D_CONV + D_SB
CONV_WIDTH = 3
PLE_DIM = 256
Q_BLOCK = 128
N_IN = 4 * D_CONV + 4 * D_SB
EPS = 1e-6

kernel_name = "hybrid_shortconv_stickbreaking_ple"


def rmsnorm(x, g):
    xf = x.astype(jnp.float32)
    y = xf * lax.rsqrt(jnp.mean(xf * xf, axis=-1, keepdims=True) + EPS)
    return (y * g.astype(jnp.float32)).astype(x.dtype)


def group_rmsnorm(y, g, group):
    shp = y.shape
    yf = y.astype(jnp.float32).reshape(shp[:-1] + (shp[-1] // group, group))
    yf = yf * lax.rsqrt(jnp.mean(yf * yf, axis=-1, keepdims=True) + EPS)
    return (yf.reshape(shp) * g.astype(jnp.float32)).astype(y.dtype)


def causal_dwconv(u, w, b):
    s = u.shape[1]
    up = jnp.pad(u, ((0, 0), (CONV_WIDTH - 1, 0), (0, 0)))
    y = b
    for j in range(CONV_WIDTH):
        y = y + up[:, j:j + s, :] * w[j]
    return y


def stick_breaking_block(q_blk, k_pre, v_pre, t0):
    dh = q_blk.shape[-1]
    z = jnp.einsum('bqhd,bkhd->bhqk', q_blk.astype(jnp.float32), k_pre.astype(jnp.float32)) / math.sqrt(dh)
    qb, kl = q_blk.shape[1], k_pre.shape[1]
    t_idx = t0 + jnp.arange(qb)[:, None]
    s_idx = jnp.arange(kl)[None, :]
    mask = s_idx < t_idx
    log_1m = jnp.where(mask, jax.nn.log_sigmoid(-z), 0.0)
    rem = lax.cumsum(log_1m, axis=3, reverse=True) - log_1m
    a = jnp.where(mask, jnp.exp(jax.nn.log_sigmoid(z) + rem), 0.0)
    out = jnp.einsum('bhqk,bkhd->bqhd', a, v_pre.astype(jnp.float32))
    return out.astype(q_blk.dtype)


def stick_breaking_attention(q, k, v):
    s = q.shape[1]
    outs = []
    for blk in range(s // Q_BLOCK):
        t0 = blk * Q_BLOCK
        kend = t0 + Q_BLOCK
        outs.append(stick_breaking_block(q[:, t0:kend], k[:, :kend], v[:, :kend], t0))
    return jnp.concatenate(outs, axis=1)


def _fwd_setup_inputs(seed: int = 0) -> dict:
    key = jax.random.key(seed)
    ks = jax.random.split(key, 14)
    f32 = jnp.float32
    x = jax.random.normal(ks[0], (BATCH, SEQ, D_MODEL), f32)
    p = jax.random.normal(ks[1], (DEPTH, BATCH, SEQ, PLE_DIM), f32)
    norm_g = 1.0 + 0.02 * jax.random.normal(ks[2], (DEPTH, D_MODEL), f32)
    w_in = jax.random.normal(ks[3], (DEPTH, D_MODEL, N_IN), f32) * D_MODEL ** -0.5
    conv_w = jax.random.normal(ks[4], (DEPTH, CONV_WIDTH, D_CONV), f32) * CONV_WIDTH ** -0.5
    conv_b = 0.02 * jax.random.normal(ks[5], (DEPTH, D_CONV), f32)
    branch_g = 1.0 + 0.02 * jax.random.normal(ks[6], (DEPTH, D_MIX), f32)
    w_out = jax.random.normal(ks[7], (DEPTH, D_MIX, D_MODEL), f32) * D_MIX ** -0.5
    ple_norm_g = 1.0 + 0.02 * jax.random.normal(ks[8], (DEPTH, D_MODEL), f32)
    w_pg = jax.random.normal(ks[9], (DEPTH, D_MODEL, D_MODEL), f32) * D_MODEL ** -0.5
    b_pg = 0.02 * jax.random.normal(ks[10], (DEPTH, D_MODEL), f32)
    w_pe = jax.random.normal(ks[11], (DEPTH, PLE_DIM, D_MODEL), f32) * PLE_DIM ** -0.5
    final_g = 1.0 + 0.02 * jax.random.normal(ks[12], (D_MODEL,), f32)
    return {"x": x, "p": p, "norm_g": norm_g, "w_in": w_in, "conv_w": conv_w,
            "conv_b": conv_b, "branch_g": branch_g, "w_out": w_out,
            "ple_norm_g": ple_norm_g, "w_pg": w_pg, "b_pg": b_pg, "w_pe": w_pe,
            "final_g": final_g}


def _fwd_reference(x, p, norm_g, w_in, conv_w, conv_b, branch_g, w_out,
              ple_norm_g, w_pg, b_pg, w_pe, final_g):
    bsz, s, _ = x.shape
    for i in range(DEPTH):
        h = rmsnorm(x, norm_g[i])
        proj = h @ w_in[i]
        c_b, c_c, c_h, c_z, q, k, v, a_z = jnp.split(
            proj, [D_CONV, 2 * D_CONV, 3 * D_CONV, 4 * D_CONV,
                   4 * D_CONV + D_SB, 4 * D_CONV + 2 * D_SB, 4 * D_CONV + 3 * D_SB], axis=-1)
        y_c = c_b * causal_dwconv(c_c * c_h, conv_w[i], conv_b[i])
        qh = q.reshape(bsz, s, N_SB_HEADS, HEAD_DIM)
        kh = k.reshape(bsz, s, N_SB_HEADS, HEAD_DIM)
        vh = v.reshape(bsz, s, N_SB_HEADS, HEAD_DIM)
        y_a = stick_breaking_attention(qh, kh, vh).reshape(bsz, s, D_SB)
        y = group_rmsnorm(jnp.concatenate([y_c, y_a], axis=-1), branch_g[i], HEAD_DIM)
        y = y * jax.nn.silu(jnp.concatenate([c_z, a_z], axis=-1))
        x = x + y @ w_out[i]
        gate = jax.nn.sigmoid(rmsnorm(x, ple_norm_g[i]) @ w_pg[i] + b_pg[i])
        x = x + gate * (p[i] @ w_pe[i])
    return rmsnorm(x, final_g)


import jax as _jax
import jax.numpy as _jnp

TWIN_FORMAT = 'train_step'
FWD_PARAMS = ['x', 'p', 'norm_g', 'w_in', 'conv_w', 'conv_b', 'branch_g', 'w_out', 'ple_norm_g', 'w_pg', 'b_pg', 'w_pe', 'final_g']
TWIN_WEIGHTS = ['norm_g', 'w_in', 'conv_w', 'conv_b', 'branch_g', 'w_out', 'ple_norm_g', 'w_pg', 'b_pg', 'w_pe', 'final_g']
TWIN_DIFF_INPUT = 'x'
TWIN_INPUTS = ['x', 'p', 'norm_g', 'w_in', 'conv_w', 'conv_b', 'branch_g', 'w_out', 'ple_norm_g', 'w_pg', 'b_pg', 'w_pe', 'final_g', 'loss_target', 'm_norm_g', 'm_w_in', 'm_conv_w', 'm_conv_b', 'm_branch_g', 'm_w_out', 'm_ple_norm_g', 'm_w_pg', 'm_b_pg', 'm_w_pe', 'm_final_g', 'v_norm_g', 'v_w_in', 'v_conv_w', 'v_conv_b', 'v_branch_g', 'v_w_out', 'v_ple_norm_g', 'v_w_pg', 'v_b_pg', 'v_w_pe', 'v_final_g']
TWIN_OUTPUTS = ['loss', 'grad_x', 'grad_norm_g', 'grad_w_in', 'grad_conv_w', 'grad_conv_b', 'grad_branch_g', 'grad_w_out', 'grad_ple_norm_g', 'grad_w_pg', 'grad_b_pg', 'grad_w_pe', 'grad_final_g', 'delta_norm_g', 'delta_w_in', 'delta_conv_w', 'delta_conv_b', 'delta_branch_g', 'delta_w_out', 'delta_ple_norm_g', 'delta_w_pg', 'delta_b_pg', 'delta_w_pe', 'delta_final_g', 'new_m_norm_g', 'new_m_w_in', 'new_m_conv_w', 'new_m_conv_b', 'new_m_branch_g', 'new_m_w_out', 'new_m_ple_norm_g', 'new_m_w_pg', 'new_m_b_pg', 'new_m_w_pe', 'new_m_final_g', 'new_v_norm_g', 'new_v_w_in', 'new_v_conv_w', 'new_v_conv_b', 'new_v_branch_g', 'new_v_w_out', 'new_v_ple_norm_g', 'new_v_w_pg', 'new_v_b_pg', 'new_v_w_pe', 'new_v_final_g']
TWIN_LEAF_KINDS = {'loss': 'loss', 'grad_x': 'grad_x', 'grad_norm_g': 'grad_w', 'grad_w_in': 'grad_w', 'grad_conv_w': 'grad_w', 'grad_conv_b': 'grad_w', 'grad_branch_g': 'grad_w', 'grad_w_out': 'grad_w', 'grad_ple_norm_g': 'grad_w', 'grad_w_pg': 'grad_w', 'grad_b_pg': 'grad_w', 'grad_w_pe': 'grad_w', 'grad_final_g': 'grad_w', 'delta_norm_g': 'delta_w', 'delta_w_in': 'delta_w', 'delta_conv_w': 'delta_w', 'delta_conv_b': 'delta_w', 'delta_branch_g': 'delta_w', 'delta_w_out': 'delta_w', 'delta_ple_norm_g': 'delta_w', 'delta_w_pg': 'delta_w', 'delta_b_pg': 'delta_w', 'delta_w_pe': 'delta_w', 'delta_final_g': 'delta_w', 'new_m_norm_g': 'new_m', 'new_m_w_in': 'new_m', 'new_m_conv_w': 'new_m', 'new_m_conv_b': 'new_m', 'new_m_branch_g': 'new_m', 'new_m_w_out': 'new_m', 'new_m_ple_norm_g': 'new_m', 'new_m_w_pg': 'new_m', 'new_m_b_pg': 'new_m', 'new_m_w_pe': 'new_m', 'new_m_final_g': 'new_m', 'new_v_norm_g': 'new_v', 'new_v_w_in': 'new_v', 'new_v_conv_w': 'new_v', 'new_v_conv_b': 'new_v', 'new_v_branch_g': 'new_v', 'new_v_w_out': 'new_v', 'new_v_ple_norm_g': 'new_v', 'new_v_w_pg': 'new_v', 'new_v_b_pg': 'new_v', 'new_v_w_pe': 'new_v', 'new_v_final_g': 'new_v'}


def _forward(args):
    return _fwd_reference(*[args[k] for k in FWD_PARAMS])


def _output_shape():
    def fwd():
        inp = _fwd_setup_inputs(0)
        return _fwd_reference(*[inp[k] for k in FWD_PARAMS])
    out = _jax.eval_shape(fwd)
    return out.shape, out.dtype

N_MICROBATCH = 1
ADAM_LR = 0.001
ADAM_B1 = 0.9
ADAM_B2 = 0.999
ADAM_EPS = 1e-08
ADAM_WD = 0.01
ADAM_STEP = 10
PER_EXAMPLE_BATCH_AXIS = {'x': 0, 'p': 1, 'loss_target': 0}
SHARED_INPUTS = []
_WEIGHT_DTYPES = {'norm_g': _jnp.float32, 'w_in': _jnp.float32, 'conv_w': _jnp.float32, 'conv_b': _jnp.float32, 'branch_g': _jnp.float32, 'w_out': _jnp.float32, 'ple_norm_g': _jnp.float32, 'w_pg': _jnp.float32, 'b_pg': _jnp.float32, 'w_pe': _jnp.float32, 'final_g': _jnp.float32}
MOMENT_SCALE = {'norm_g': 1.544941e-01, 'w_in': 7.741323e-02, 'conv_w': 8.809939e-02, 'conv_b': 9.486679e-02, 'branch_g': 8.270000e-02, 'w_out': 8.337886e-02, 'ple_norm_g': 2.973435e-02, 'w_pg': 2.925760e-02, 'b_pg': 5.090113e-02, 'w_pe': 7.486581e-02, 'final_g': 3.204505e+01}


def _to_microbatches(a, axis):
    t = _jnp.moveaxis(a, axis, 0)
    t = t.reshape((N_MICROBATCH, t.shape[0] // N_MICROBATCH) + t.shape[1:])
    return _jnp.moveaxis(t, 1, axis + 1)


def setup_inputs(seed: int = 0) -> dict:
    inp = _fwd_setup_inputs(seed)
    key = _jax.random.fold_in(_jax.random.key(seed), 7919)
    shape, _ = _output_shape()
    out = dict(inp)
    out["loss_target"] = _jax.random.normal(_jax.random.fold_in(key, 0), shape, _jnp.float32)
    for i, name in enumerate(TWIN_WEIGHTS):
        w = inp[name].astype(_jnp.float32)
        if MOMENT_SCALE is None:
            s = _jnp.sqrt(_jnp.mean(_jnp.square(w)) + 1e-30)
        else:
            s = MOMENT_SCALE[name]
        km, kv = _jax.random.split(_jax.random.fold_in(key, i + 1))
        out[name] = w
        out["m_" + name] = s * _jax.random.normal(km, w.shape, _jnp.float32)
        out["v_" + name] = (s * s) * _jax.random.uniform(kv, w.shape, _jnp.float32, 0.5, 1.5)
    if N_MICROBATCH > 1:
        for name, axis in PER_EXAMPLE_BATCH_AXIS.items():
            out[name] = _to_microbatches(out[name], axis)
    return {'x': out['x'], 'p': out['p'], 'norm_g': out['norm_g'], 'w_in': out['w_in'], 'conv_w': out['conv_w'], 'conv_b': out['conv_b'], 'branch_g': out['branch_g'], 'w_out': out['w_out'], 'ple_norm_g': out['ple_norm_g'], 'w_pg': out['w_pg'], 'b_pg': out['b_pg'], 'w_pe': out['w_pe'], 'final_g': out['final_g'], 'loss_target': out['loss_target'], 'm_norm_g': out['m_norm_g'], 'm_w_in': out['m_w_in'], 'm_conv_w': out['m_conv_w'], 'm_conv_b': out['m_conv_b'], 'm_branch_g': out['m_branch_g'], 'm_w_out': out['m_w_out'], 'm_ple_norm_g': out['m_ple_norm_g'], 'm_w_pg': out['m_w_pg'], 'm_b_pg': out['m_b_pg'], 'm_w_pe': out['m_w_pe'], 'm_final_g': out['m_final_g'], 'v_norm_g': out['v_norm_g'], 'v_w_in': out['v_w_in'], 'v_conv_w': out['v_conv_w'], 'v_conv_b': out['v_conv_b'], 'v_branch_g': out['v_branch_g'], 'v_w_out': out['v_w_out'], 'v_ple_norm_g': out['v_ple_norm_g'], 'v_w_pg': out['v_w_pg'], 'v_b_pg': out['v_b_pg'], 'v_w_pe': out['v_w_pe'], 'v_final_g': out['v_final_g']}


def _loss(weights, diff, rest, loss_target):
    with _jax.named_scope("forward"):
        args = {**rest, TWIN_DIFF_INPUT: diff, **{k: w.astype(_WEIGHT_DTYPES[k]) for k, w in weights.items()}}
        y = _forward(args)
    with _jax.named_scope("loss_head"):
        err = _jnp.square(y.astype(_jnp.float32) - loss_target)
        return 0.5 * _jnp.sum(_jnp.mean(err, axis=-1)) if err.ndim else 0.5 * err


def _adamw(w, g, m, v):
    m = ADAM_B1 * m + (1.0 - ADAM_B1) * g
    v = ADAM_B2 * v + (1.0 - ADAM_B2) * _jnp.square(g)
    m_hat = m / (1.0 - ADAM_B1 ** ADAM_STEP)
    v_hat = v / (1.0 - ADAM_B2 ** ADAM_STEP)
    delta = -ADAM_LR * (m_hat / (_jnp.sqrt(v_hat) + ADAM_EPS) + ADAM_WD * w)
    return delta, m, v


def reference(x, p, norm_g, w_in, conv_w, conv_b, branch_g, w_out, ple_norm_g, w_pg, b_pg, w_pe, final_g, loss_target, m_norm_g, m_w_in, m_conv_w, m_conv_b, m_branch_g, m_w_out, m_ple_norm_g, m_w_pg, m_b_pg, m_w_pe, m_final_g, v_norm_g, v_w_in, v_conv_w, v_conv_b, v_branch_g, v_w_out, v_ple_norm_g, v_w_pg, v_b_pg, v_w_pe, v_final_g):
    given = dict(x=x, p=p, norm_g=norm_g, w_in=w_in, conv_w=conv_w, conv_b=conv_b, branch_g=branch_g, w_out=w_out, ple_norm_g=ple_norm_g, w_pg=w_pg, b_pg=b_pg, w_pe=w_pe, final_g=final_g, loss_target=loss_target, m_norm_g=m_norm_g, m_w_in=m_w_in, m_conv_w=m_conv_w, m_conv_b=m_conv_b, m_branch_g=m_branch_g, m_w_out=m_w_out, m_ple_norm_g=m_ple_norm_g, m_w_pg=m_w_pg, m_b_pg=m_b_pg, m_w_pe=m_w_pe, m_final_g=m_final_g, v_norm_g=v_norm_g, v_w_in=v_w_in, v_conv_w=v_conv_w, v_conv_b=v_conv_b, v_branch_g=v_branch_g, v_w_out=v_w_out, v_ple_norm_g=v_ple_norm_g, v_w_pg=v_w_pg, v_b_pg=v_b_pg, v_w_pe=v_w_pe, v_final_g=v_final_g)
    weights = {n: given[n] for n in TWIN_WEIGHTS}
    shared = {n: given[n] for n in SHARED_INPUTS}
    per_example = {n: given[n] for n in ['x', 'p']}
    grad_fn = _jax.value_and_grad(_loss, argnums=(0, 1))

    def one_microbatch(ex, loss_target):
        ex = dict(ex)
        diff = ex.pop(TWIN_DIFF_INPUT)
        return grad_fn(weights, diff, {**shared, **ex}, loss_target)

    if N_MICROBATCH == 1:
        loss, (grad_w, grad_x) = one_microbatch(per_example, given["loss_target"])
    else:
        def body(carry, xs):
            loss_sum, grad_sum = carry
            l_k, (gw_k, gx_k) = one_microbatch(xs[0], xs[1])
            with _jax.named_scope("update"):
                return (loss_sum + l_k, _jax.tree.map(_jnp.add, grad_sum, gw_k)), gx_k

        init = (_jnp.zeros((), _jnp.float32), _jax.tree.map(_jnp.zeros_like, weights))
        (loss, grad_w), grad_x = _jax.lax.scan(body, init, (per_example, given["loss_target"]))
    with _jax.named_scope("update"):
        delta_w, new_m, new_v = {}, {}, {}
        for n in TWIN_WEIGHTS:
            delta_w[n], new_m[n], new_v[n] = _adamw(weights[n], grad_w[n], given["m_" + n], given["v_" + n])
    return (loss, grad_x, *[grad_w[n] for n in TWIN_WEIGHTS], *[delta_w[n] for n in TWIN_WEIGHTS],
            *[new_m[n] for n in TWIN_WEIGHTS], *[new_v[n] for n in TWIN_WEIGHTS])
```

```python
import math

import jax
import jax.numpy as jnp
from jax import lax
from jax.experimental import pallas as pl
from jax.experimental.pallas import tpu as pltpu

F32 = jnp.float32
BF16 = jnp.bfloat16
EPS = 1e-6
HEAD = 64
LANES = 128
ATT_T = 128
DEPTH = 2
VMEM_LIMIT = 56 * 1024 * 1024
MESH = pl.DeviceIdType.MESH
ANY = pl.BlockSpec(memory_space=pl.ANY)

ADAM_LR = 0.001
ADAM_B1 = 0.9
ADAM_B2 = 0.999
ADAM_EPS = 1e-08
ADAM_WD = 0.01
ADAM_STEP = 10


def _pcall(body, **kw):
    return pl.pallas_call(body, **kw)


def _cp(n_axes):
    return pltpu.CompilerParams(dimension_semantics=("arbitrary",) * n_axes, vmem_limit_bytes=VMEM_LIMIT)


def _tile(n, pref):
    return pref if n % pref == 0 else n


def _split_dot(a, b, passes):
    out = None
    rem = a
    for _ in range(passes):
        hi = rem.astype(BF16)
        t = jnp.dot(hi, b, preferred_element_type=F32)
        out = t if out is None else out + t
        rem = rem - hi.astype(F32)
    return out


def _group_mat():
    r = lax.broadcasted_iota(jnp.int32, (LANES, LANES), 0) // HEAD
    c = lax.broadcasted_iota(jnp.int32, (LANES, LANES), 1) // HEAD
    return jnp.where(r == c, 1.0 / HEAD, 0.0).astype(BF16)


def _group_mean(v, gm):
    return _split_dot(v, gm, 3)


def _sigmoid(z):
    return 1.0 / (1.0 + jnp.exp(-z))


def _dot_nt(a, b):
    return lax.dot_general(a, b, (((1,), (1,)), ((), ())), preferred_element_type=F32)


def _dot_tn(a, b):
    return lax.dot_general(a, b, (((0,), (0,)), ((), ())), preferred_element_type=F32)


def _cast_bf16(w, name):
    shape = w.shape
    w2 = w.reshape(-1, shape[-1])
    r, c = w2.shape
    tr = _tile(r, 512)

    def body(w_ref, o_ref):
        o_ref[...] = w_ref[...].astype(BF16)

    out = _pcall(body, name=name, grid=(r // tr,),
                 in_specs=[pl.BlockSpec((tr, c), lambda i: (i, 0))],
                 out_specs=pl.BlockSpec((tr, c), lambda i: (i, 0)),
                 out_shape=jax.ShapeDtypeStruct((r, c), BF16), compiler_params=_cp(1))(w2)
    return out.reshape(shape)


def _rms_fwd(x, g, name):
    s, d = x.shape
    tm = _tile(s, 512)

    def body(x_ref, g_ref, h_ref):
        xv = x_ref[...]
        r = lax.rsqrt(jnp.mean(xv * xv, axis=-1, keepdims=True) + EPS)
        h_ref[...] = (xv * r * g_ref[...]).astype(BF16)

    return _pcall(body, name=name, grid=(s // tm,),
                  in_specs=[pl.BlockSpec((tm, d), lambda m: (m, 0)), pl.BlockSpec((1, d), lambda m: (0, 0))],
                  out_specs=pl.BlockSpec((tm, d), lambda m: (m, 0)),
                  out_shape=jax.ShapeDtypeStruct((s, d), BF16), compiler_params=_cp(1))(x, g)


def _rms_bwd_rows(dh, xv, g):
    r = lax.rsqrt(jnp.mean(xv * xv, axis=-1, keepdims=True) + EPS)
    xn = xv * r
    dxn = dh * g
    dx = r * (dxn - xn * jnp.mean(dxn * xn, axis=-1, keepdims=True))
    return dx, dh * xn


def _colsum8(v):
    tm, d = v.shape
    return jnp.sum(v.reshape(tm // 8, 8, d), axis=0)


def _inproj(h, w, name):
    s, d = h.shape
    n = w.shape[1]
    sw = d // 2
    ns = n // sw
    tm = _tile(s, 512)

    def body(h_ref, w_ref, o_ref):
        o_ref[0] = jnp.dot(h_ref[...], w_ref[...], preferred_element_type=F32).astype(BF16)

    return _pcall(body, name=name, grid=(ns, s // tm),
                  in_specs=[pl.BlockSpec((tm, d), lambda j, m: (m, 0)), pl.BlockSpec((d, sw), lambda j, m: (0, j))],
                  out_specs=pl.BlockSpec((1, tm, sw), lambda j, m: (j, m, 0)),
                  out_shape=jax.ShapeDtypeStruct((ns, s, sw), BF16), compiler_params=_cp(2))(h, w)


def _attn_fwd(proj, name):
    _, s, sw = proj.shape
    nhp = sw // LANES
    t = ATT_T
    nq = s // t
    scale = 1.0 / math.sqrt(HEAD)

    def body(q_ref, k_ref, v_ref, o_ref, tl_ref):
        i = pl.program_id(1)
        row = lax.broadcasted_iota(jnp.int32, (t, t), 0)
        col = lax.broadcasted_iota(jnp.int32, (t, t), 1)
        tri = (row > col).astype(BF16)
        lane = lax.broadcasted_iota(jnp.int32, (t, LANES), 1)
        q = q_ref[0]
        outs = []
        for h in range(2):
            qm = jnp.where((lane // HEAD) == h, q, jnp.zeros_like(q))

            def step(jj, carry, qm=qm):
                rsum, acc = carry
                j = i - jj
                k0 = pl.multiple_of(j * t, t)
                kj = k_ref[0, pl.ds(k0, t), :]
                vj = v_ref[0, pl.ds(k0, t), :]
                z = _dot_nt(qm, kj) * scale
                lm = -(jnp.maximum(z, 0.0) + jnp.log(1.0 + jnp.exp(-jnp.abs(z))))
                valid = (col - jj * t) < row
                lm = jnp.where(valid, lm, 0.0)
                rem = rsum + _split_dot(lm, tri, 2)
                a = jnp.where(valid, jnp.exp(z + lm + rem), 0.0)
                acc = acc + jnp.dot(a.astype(BF16), vj, preferred_element_type=F32)
                rsum = rsum + jnp.sum(lm, axis=1, keepdims=True)
                return rsum, acc

            rsum, acc = lax.fori_loop(0, i + 1, step, (jnp.zeros((t, 1), F32), jnp.zeros((t, LANES), F32)))
            tl_ref[h] = rsum
            outs.append(acc)
        o_ref[...] = jnp.where(lane < HEAD, outs[0], outs[1]).astype(BF16)

    return _pcall(
        body, name=name, grid=(nhp, nq),
        in_specs=[pl.BlockSpec((1, t, LANES), lambda hp, i: (4, i, hp)),
                  pl.BlockSpec((1, s, LANES), lambda hp, i: (5, 0, hp)),
                  pl.BlockSpec((1, s, LANES), lambda hp, i: (6, 0, hp))],
        out_specs=[pl.BlockSpec((t, LANES), lambda hp, i: (i, hp)),
                   pl.BlockSpec((2, t, 1), lambda hp, i: (hp, i, 0))],
        out_shape=[jax.ShapeDtypeStruct((s, sw), BF16), jax.ShapeDtypeStruct((2 * nhp, s, 1), F32)],
        compiler_params=_cp(2))(proj, proj, proj)


def _conv_rows(cc_ref, ch_ref, w_ref, b_ref, r, tc):
    r0 = pl.multiple_of(r * tc, tc)
    u = cc_ref[0, pl.ds(r0, tc), :].astype(F32) * ch_ref[0, pl.ds(r0, tc), :].astype(F32)
    p0 = pl.multiple_of(jnp.maximum(r0 - 16, 0), 16)
    up = cc_ref[0, pl.ds(p0, 16), :].astype(F32) * ch_ref[0, pl.ds(p0, 16), :].astype(F32)
    up = up * (r > 0).astype(F32)
    prev1 = up[15:16, :]
    prev2 = up[14:15, :]
    rid = lax.broadcasted_iota(jnp.int32, u.shape, 0)
    s1 = jnp.where(rid == 0, prev1, pltpu.roll(u, 1, axis=0))
    s2 = jnp.where(rid == 0, prev2, jnp.where(rid == 1, prev1, pltpu.roll(u, 2, axis=0)))
    cv = b_ref[...] + s2 * w_ref[0:1, :] + s1 * w_ref[1:2, :] + u * w_ref[2:3, :]
    return r0, u, s1, s2, cv


def _mix_fwd(proj, ya, conv_w, conv_b, bg, name):
    _, s, sw = proj.shape
    nh = sw // LANES
    tc = _tile(s, 256)

    def body(cb_ref, cc_ref, ch_ref, cz_ref, ya_ref, az_ref, w_ref, b_ref, g_ref, y_ref):
        c = pl.program_id(0)
        gm = _group_mat()

        def finish(r0, yv, zg):
            n = yv * lax.rsqrt(_group_mean(yv * yv, gm) + EPS)
            y_ref[pl.ds(r0, tc), :] = (n * g_ref[...] * (zg * _sigmoid(zg))).astype(BF16)

        @pl.when(c < nh)
        def _():
            def step(r, carry):
                r0, _, _, _, cv = _conv_rows(cc_ref, ch_ref, w_ref, b_ref, r, tc)
                yc = cb_ref[0, pl.ds(r0, tc), :].astype(F32) * cv
                finish(r0, yc, cz_ref[0, pl.ds(r0, tc), :].astype(F32))
                return carry
            lax.fori_loop(0, s // tc, step, 0)

        @pl.when(c >= nh)
        def _():
            def step(r, carry):
                r0 = pl.multiple_of(r * tc, tc)
                finish(r0, ya_ref[pl.ds(r0, tc), :].astype(F32), az_ref[0, pl.ds(r0, tc), :].astype(F32))
                return carry
            lax.fori_loop(0, s // tc, step, 0)

    def sec(k):
        return pl.BlockSpec((1, s, LANES), lambda c: (k, 0, jnp.minimum(c, nh - 1)))

    return _pcall(
        body, name=name, grid=(2 * nh,),
        in_specs=[sec(0), sec(1), sec(2), sec(3),
                  pl.BlockSpec((s, LANES), lambda c: (0, jnp.maximum(c - nh, 0))),
                  pl.BlockSpec((1, s, LANES), lambda c: (7, 0, jnp.maximum(c - nh, 0))),
                  pl.BlockSpec((3, LANES), lambda c: (0, jnp.minimum(c, nh - 1))),
                  pl.BlockSpec((1, LANES), lambda c: (0, jnp.minimum(c, nh - 1))),
                  pl.BlockSpec((1, LANES), lambda c: (0, c))],
        out_specs=pl.BlockSpec((s, LANES), lambda c: (0, c)),
        out_shape=jax.ShapeDtypeStruct((s, 2 * sw), BF16), compiler_params=_cp(1),
    )(proj, proj, proj, proj, ya, proj, conv_w, conv_b, bg)


def _outproj(y, w, x, g, name):
    s, d = x.shape
    tm = _tile(s, 256)

    def body(y_ref, w_ref, x_ref, g_ref, x1_ref, hn_ref):
        x1 = x_ref[...] + jnp.dot(y_ref[...], w_ref[...], preferred_element_type=F32)
        x1_ref[...] = x1
        r = lax.rsqrt(jnp.mean(x1 * x1, axis=-1, keepdims=True) + EPS)
        hn_ref[...] = (x1 * r * g_ref[...]).astype(BF16)

    row = lambda m: (m, 0)
    fix = lambda m: (0, 0)
    return _pcall(body, name=name, grid=(s // tm,),
                  in_specs=[pl.BlockSpec((tm, d), row), pl.BlockSpec((d, d), fix), pl.BlockSpec((tm, d), row),
                            pl.BlockSpec((1, d), fix)],
                  out_specs=[pl.BlockSpec((tm, d), row), pl.BlockSpec((tm, d), row)],
                  out_shape=[jax.ShapeDtypeStruct((s, d), F32), jax.ShapeDtypeStruct((s, d), BF16)],
                  compiler_params=_cp(1))(y, w, x, g)


def _ple_fwd(hn, w_pg, b_pg, p, w_pe, x1, name):
    s, d = x1.shape
    pd = p.shape[1]
    tm = _tile(s, 256)

    def body(hn_ref, wg_ref, b_ref, p_ref, we_ref, x1_ref, x2_ref, gate_ref, e_ref):
        gate = _sigmoid(jnp.dot(hn_ref[...], wg_ref[...], preferred_element_type=F32) + b_ref[...])
        e = jnp.dot(p_ref[...].astype(BF16), we_ref[...], preferred_element_type=F32)
        x2_ref[...] = x1_ref[...] + gate * e
        gate_ref[...] = gate.astype(BF16)
        e_ref[...] = e.astype(BF16)

    row = lambda m: (m, 0)
    fix = lambda m: (0, 0)
    return _pcall(body, name=name, grid=(s // tm,),
                  in_specs=[pl.BlockSpec((tm, d), row), pl.BlockSpec((d, d), fix), pl.BlockSpec((1, d), fix),
                            pl.BlockSpec((tm, pd), row), pl.BlockSpec((pd, d), fix), pl.BlockSpec((tm, d), row)],
                  out_specs=[pl.BlockSpec((tm, d), row)] * 3,
                  out_shape=[jax.ShapeDtypeStruct((s, d), F32), jax.ShapeDtypeStruct((s, d), BF16),
                             jax.ShapeDtypeStruct((s, d), BF16)],
                  compiler_params=_cp(1))(hn, w_pg, b_pg, p, w_pe, x1)


def _loss_head(x, tgt, g, name):
    s, d = x.shape
    tm = _tile(s, 256)

    def body(x_ref, t_ref, g_ref, l_ref, dx_ref, dg_ref):
        m = pl.program_id(0)

        @pl.when(m == 0)
        def _():
            l_ref[...] = jnp.zeros_like(l_ref)
            dg_ref[...] = jnp.zeros_like(dg_ref)

        xv = x_ref[...]
        gv = g_ref[...]
        r = lax.rsqrt(jnp.mean(xv * xv, axis=-1, keepdims=True) + EPS)
        xn = xv * r
        err = xn * gv - t_ref[...]
        l_ref[...] += jnp.sum(err * err)
        dy = err * (1.0 / d)
        dxn = dy * gv
        dx_ref[...] = r * (dxn - xn * jnp.mean(dxn * xn, axis=-1, keepdims=True))
        dg_ref[...] += _colsum8(dy * xn)

    row = lambda m: (m, 0)
    fix = lambda m: (0, 0)
    return _pcall(body, name=name, grid=(s // tm,),
                  in_specs=[pl.BlockSpec((tm, d), row), pl.BlockSpec((tm, d), row), pl.BlockSpec((1, d), fix)],
                  out_specs=[pl.BlockSpec((8, LANES), fix), pl.BlockSpec((tm, d), row), pl.BlockSpec((8, d), fix)],
                  out_shape=[jax.ShapeDtypeStruct((8, LANES), F32), jax.ShapeDtypeStruct((s, d), F32),
                             jax.ShapeDtypeStruct((8, d), F32)],
                  compiler_params=_cp(1))(x, tgt, g)


def _ple_bwd(dx2, gate, e, x1, w_pg, g_ple, w_out, name):
    s, d = dx2.shape
    tm = _tile(s, 256)

    def body(dx2_ref, gate_ref, e_ref, x1_ref, wg_ref, g_ref, wo_ref,
             du_ref, de_ref, dx1_ref, dy_ref, db_ref, dg_ref):
        m = pl.program_id(0)

        @pl.when(m == 0)
        def _():
            db_ref[...] = jnp.zeros_like(db_ref)
            dg_ref[...] = jnp.zeros_like(dg_ref)

        dx2v = dx2_ref[...]
        gate = gate_ref[...].astype(F32)
        du = dx2v * e_ref[...].astype(F32) * gate * (1.0 - gate)
        de_ref[...] = (dx2v * gate).astype(BF16)
        dub = du.astype(BF16)
        du_ref[...] = dub
        db_ref[...] += _colsum8(du)
        dhn = _dot_nt(dub, wg_ref[...])
        dxr, dgr = _rms_bwd_rows(dhn, x1_ref[...], g_ref[...])
        dx1 = dx2v + dxr
        dx1_ref[...] = dx1
        dg_ref[...] += _colsum8(dgr)
        dy_ref[...] = _dot_nt(dx1.astype(BF16), wo_ref[...]).astype(BF16)

    row = lambda m: (m, 0)
    fix = lambda m: (0, 0)
    t = pl.BlockSpec((tm, d), row)
    return _pcall(body, name=name, grid=(s // tm,),
                  in_specs=[t, t, t, t, pl.BlockSpec((d, d), fix), pl.BlockSpec((1, d), fix), pl.BlockSpec((d, d), fix)],
                  out_specs=[t, t, t, t, pl.BlockSpec((8, d), fix), pl.BlockSpec((8, d), fix)],
                  out_shape=[jax.ShapeDtypeStruct((s, d), BF16), jax.ShapeDtypeStruct((s, d), BF16),
                             jax.ShapeDtypeStruct((s, d), F32), jax.ShapeDtypeStruct((s, d), BF16),
                             jax.ShapeDtypeStruct((8, d), F32), jax.ShapeDtypeStruct((8, d), F32)],
                  compiler_params=_cp(1))(dx2, gate, e, x1, w_pg, g_ple, w_out)


def _mm_tn(a, b, stack, layer, name, b_sections=False):
    s, ka = a.shape
    if b_sections:
        ns, _, tn = b.shape
        n = ns * tn
    else:
        n = b.shape[1]
        tn = _tile(n, 1024)
        ns = n // tn
    tk = _tile(s, 512)
    nk = s // tk

    def body(*refs):
        a_ref, b_ref = refs[0], refs[1]
        o_ref, acc_ref = refs[-2], refs[-1]
        k = pl.program_id(1)

        @pl.when(k == 0)
        def _():
            acc_ref[...] = jnp.zeros_like(acc_ref)

        bv = b_ref[0] if b_sections else b_ref[...]
        acc_ref[...] += _dot_tn(a_ref[...].astype(BF16), bv.astype(BF16))

        @pl.when(k == nk - 1)
        def _():
            o_ref[0] = acc_ref[...]

    b_spec = (pl.BlockSpec((1, tk, tn), lambda j, k: (j, k, 0)) if b_sections
              else pl.BlockSpec((tk, tn), lambda j, k: (k, j)))
    in_specs = [pl.BlockSpec((tk, ka), lambda j, k: (k, 0)), b_spec]
    args = [a, b]
    aliases = {}
    if stack is not None:
        in_specs.append(ANY)
        args.append(stack)
        aliases = {2: 0}
    return _pcall(body, name=name, grid=(ns, nk), in_specs=in_specs,
                  out_specs=pl.BlockSpec((1, ka, tn), lambda j, k: (layer, 0, j)),
                  out_shape=jax.ShapeDtypeStruct((DEPTH, ka, n), F32),
                  scratch_shapes=[pltpu.VMEM((ka, tn), F32)], input_output_aliases=aliases,
                  compiler_params=_cp(2))(*args)


def _norm_gate_bwd(dy, yv, zg, g, gm):
    r = lax.rsqrt(_group_mean(yv * yv, gm) + EPS)
    n = yv * r
    sg = _sigmoid(zg)
    sil = zg * sg
    dzg = dy * n * g * (sg * (1.0 + zg * (1.0 - sg)))
    dn = dy * g * sil
    dyv = r * (dn - n * _group_mean(dn * n, gm))
    return dyv, dzg, dy * n * sil


def _convmix_bwd(dy, proj, conv_w, conv_b, bg, name):
    _, s, sw = proj.shape
    nh = sw // LANES
    tc = _tile(s, 256)
    nr = s // tc

    def body(dy_ref, cb_ref, cc_ref, ch_ref, cz_ref, w_ref, b_ref, g_ref,
             dp_ref, dw_ref, db_ref, dg_ref, dcv_ref):
        gm = _group_mat()
        dcv_ref[pl.ds(s, 8), :] = jnp.zeros((8, LANES), F32)

        def pass1(r, carry):
            dw0, dw1, dw2, db, dg = carry
            r0, u, s1, s2, cv = _conv_rows(cc_ref, ch_ref, w_ref, b_ref, r, tc)
            cb = cb_ref[0, pl.ds(r0, tc), :].astype(F32)
            dyc, dcz, dgr = _norm_gate_bwd(dy_ref[pl.ds(r0, tc), :].astype(F32), cb * cv,
                                           cz_ref[0, pl.ds(r0, tc), :].astype(F32), g_ref[...], gm)
            dp_ref[0, pl.ds(r0, tc), :] = (dyc * cv).astype(BF16)
            dp_ref[3, pl.ds(r0, tc), :] = dcz.astype(BF16)
            dcv = dyc * cb
            dcv_ref[pl.ds(r0, tc), :] = dcv
            return (dw0 + _colsum8(dcv * s2), dw1 + _colsum8(dcv * s1), dw2 + _colsum8(dcv * u),
                    db + _colsum8(dcv), dg + _colsum8(dgr))

        z8 = jnp.zeros((8, LANES), F32)
        dw0, dw1, dw2, db, dg = lax.fori_loop(0, nr, pass1, (z8, z8, z8, z8, z8))
        dw_ref[0] = dw0
        dw_ref[1] = dw1
        dw_ref[2] = dw2
        db_ref[...] = db
        dg_ref[...] = dg

        def pass2(r, carry):
            r0 = pl.multiple_of(r * tc, tc)
            dcv = dcv_ref[pl.ds(r0, tc), :]
            nxt = dcv_ref[pl.ds(pl.multiple_of(r0 + tc, 8), 8), :]
            rid = lax.broadcasted_iota(jnp.int32, dcv.shape, 0)
            n1 = jnp.where(rid == tc - 1, nxt[0:1, :], pltpu.roll(dcv, tc - 1, axis=0))
            n2 = jnp.where(rid == tc - 1, nxt[1:2, :],
                           jnp.where(rid == tc - 2, nxt[0:1, :], pltpu.roll(dcv, tc - 2, axis=0)))
            du = dcv * w_ref[2:3, :] + n1 * w_ref[1:2, :] + n2 * w_ref[0:1, :]
            dp_ref[1, pl.ds(r0, tc), :] = (du * ch_ref[0, pl.ds(r0, tc), :].astype(F32)).astype(BF16)
            dp_ref[2, pl.ds(r0, tc), :] = (du * cc_ref[0, pl.ds(r0, tc), :].astype(F32)).astype(BF16)
            return carry

        lax.fori_loop(0, nr, pass2, 0)

    def sec(k):
        return pl.BlockSpec((1, s, LANES), lambda c: (k, 0, c))

    col = lambda c: (0, c)
    return _pcall(
        body, name=name, grid=(nh,),
        in_specs=[pl.BlockSpec((s, LANES), col), sec(0), sec(1), sec(2), sec(3),
                  pl.BlockSpec((3, LANES), col), pl.BlockSpec((1, LANES), col), pl.BlockSpec((1, LANES), col)],
        out_specs=[pl.BlockSpec((4, s, LANES), lambda c: (0, 0, c)), pl.BlockSpec((3, 8, LANES), lambda c: (0, 0, c)),
                   pl.BlockSpec((8, LANES), col), pl.BlockSpec((8, LANES), col)],
        out_shape=[jax.ShapeDtypeStruct((4, s, sw), BF16), jax.ShapeDtypeStruct((3, 8, sw), F32),
                   jax.ShapeDtypeStruct((8, sw), F32), jax.ShapeDtypeStruct((8, sw), F32)],
        scratch_shapes=[pltpu.VMEM((s + 8, LANES), F32)], compiler_params=_cp(1),
    )(dy, proj, proj, proj, proj, conv_w, conv_b, bg)


def _attnmix_bwd(dy, ya, proj, bg, name):
    _, s, sw = proj.shape
    nh = sw // LANES
    tc = _tile(s, 256)

    def body(dy_ref, ya_ref, az_ref, g_ref, dya_ref, daz_ref, dg_ref):
        gm = _group_mat()

        def step(r, dg):
            r0 = pl.multiple_of(r * tc, tc)
            dyv, dzg, dgr = _norm_gate_bwd(dy_ref[pl.ds(r0, tc), :].astype(F32), ya_ref[pl.ds(r0, tc), :].astype(F32),
                                           az_ref[0, pl.ds(r0, tc), :].astype(F32), g_ref[...], gm)
            dya_ref[pl.ds(r0, tc), :] = dyv.astype(BF16)
            daz_ref[pl.ds(r0, tc), :] = dzg.astype(BF16)
            return dg + _colsum8(dgr)

        dg_ref[...] = lax.fori_loop(0, s // tc, step, jnp.zeros((8, LANES), F32))

    col = lambda c: (0, c)
    return _pcall(
        body, name=name, grid=(nh,),
        in_specs=[pl.BlockSpec((s, LANES), lambda c: (0, c + nh)), pl.BlockSpec((s, LANES), col),
                  pl.BlockSpec((1, s, LANES), lambda c: (7, 0, c)), pl.BlockSpec((1, LANES), lambda c: (0, c + nh))],
        out_specs=[pl.BlockSpec((s, LANES), col), pl.BlockSpec((s, LANES), col), pl.BlockSpec((8, LANES), col)],
        out_shape=[jax.ShapeDtypeStruct((s, sw), BF16), jax.ShapeDtypeStruct((s, sw), BF16),
                   jax.ShapeDtypeStruct((8, sw), F32)],
        compiler_params=_cp(1))(dy, ya, proj, bg)


def _attn_bwd(proj, dya, tl, name):
    _, s, sw = proj.shape
    nhp = sw // LANES
    t = ATT_T
    nq = s // t
    scale = 1.0 / math.sqrt(HEAD)

    def body(q_ref, k_ref, v_ref, do_ref, tl_ref, dq_ref, dk_ref, dv_ref, dka_ref, dva_ref):
        i = pl.program_id(1)

        @pl.when(i == 0)
        def _():
            dka_ref[...] = jnp.zeros_like(dka_ref)
            dva_ref[...] = jnp.zeros_like(dva_ref)

        row = lax.broadcasted_iota(jnp.int32, (t, t), 0)
        col = lax.broadcasted_iota(jnp.int32, (t, t), 1)
        tri = (row <= col).astype(BF16)
        lane = lax.broadcasted_iota(jnp.int32, (t, LANES), 1)
        q = q_ref[0]
        do = do_ref[...]
        dqs = []
        for h in range(2):
            hm = (lane // HEAD) == h
            qm = jnp.where(hm, q, jnp.zeros_like(q))
            dom = jnp.where(hm, do, jnp.zeros_like(do))
            tot = tl_ref[h]

            def step(j, carry, qm=qm, dom=dom, tot=tot):
                psum, gsum, dq = carry
                k0 = pl.multiple_of(j * t, t)
                kj = k_ref[0, pl.ds(k0, t), :]
                vj = v_ref[0, pl.ds(k0, t), :]
                z = _dot_nt(qm, kj) * scale
                lm = -(jnp.maximum(z, 0.0) + jnp.log(1.0 + jnp.exp(-jnp.abs(z))))
                valid = (col + (j - i) * t) < row
                lm = jnp.where(valid, lm, 0.0)
                rem = tot - psum - _split_dot(lm, tri, 2)
                a = jnp.where(valid, jnp.exp(z + lm + rem), 0.0)
                g = a * _dot_nt(dom, vj)
                dlt = gsum + _split_dot(g, tri, 2)
                beta = jnp.exp(z + lm)
                dz = (jnp.where(valid, g - beta * dlt, 0.0) * scale).astype(BF16)
                dq = dq + jnp.dot(dz, kj, preferred_element_type=F32)
                dka_ref[pl.ds(k0, t), :] += _dot_tn(dz, qm)
                dva_ref[pl.ds(k0, t), :] += _dot_tn(a.astype(BF16), dom)
                return psum + jnp.sum(lm, axis=1, keepdims=True), gsum + jnp.sum(g, axis=1, keepdims=True), dq

            z1 = jnp.zeros((t, 1), F32)
            _, _, dq = lax.fori_loop(0, i + 1, step, (z1, z1, jnp.zeros((t, LANES), F32)))
            dqs.append(dq)
        dq_ref[...] = jnp.where(lane < HEAD, dqs[0], dqs[1]).astype(BF16)

        @pl.when(i == nq - 1)
        def _():
            dk_ref[...] = dka_ref[...].astype(BF16)
            dv_ref[...] = dva_ref[...].astype(BF16)

    return _pcall(
        body, name=name, grid=(nhp, nq),
        in_specs=[pl.BlockSpec((1, t, LANES), lambda hp, i: (4, i, hp)),
                  pl.BlockSpec((1, s, LANES), lambda hp, i: (5, 0, hp)),
                  pl.BlockSpec((1, s, LANES), lambda hp, i: (6, 0, hp)),
                  pl.BlockSpec((t, LANES), lambda hp, i: (i, hp)),
                  pl.BlockSpec((2, t, 1), lambda hp, i: (hp, i, 0))],
        out_specs=[pl.BlockSpec((t, LANES), lambda hp, i: (i, hp)),
                   pl.BlockSpec((s, LANES), lambda hp, i: (0, hp)),
                   pl.BlockSpec((s, LANES), lambda hp, i: (0, hp))],
        out_shape=[jax.ShapeDtypeStruct((s, sw), BF16)] * 3,
        scratch_shapes=[pltpu.VMEM((s, LANES), F32), pltpu.VMEM((s, LANES), F32)],
        compiler_params=_cp(2))(proj, proj, proj, dya, tl)


def _inproj_bwd(dproj, w, x, g, dx1, name):
    ns, s, sw = dproj.shape
    d = x.shape[1]
    tm = _tile(s, 256)
    nm = s // tm

    def body(dp_ref, w_ref, x_ref, g_ref, dx1_ref, dx_ref, dg_ref, acc_ref):
        m = pl.program_id(0)
        k = pl.program_id(1)

        @pl.when((m == 0) & (k == 0))
        def _():
            dg_ref[...] = jnp.zeros_like(dg_ref)

        @pl.when(k == 0)
        def _():
            acc_ref[...] = jnp.zeros_like(acc_ref)

        acc_ref[...] += _dot_nt(dp_ref[0], w_ref[...])

        @pl.when(k == ns - 1)
        def _():
            dxr, dgr = _rms_bwd_rows(acc_ref[...], x_ref[...], g_ref[...])
            dx_ref[...] = dx1_ref[...] + dxr
            dg_ref[...] += _colsum8(dgr)

    row = lambda m, k: (m, 0)
    fix = lambda m, k: (0, 0)
    return _pcall(body, name=name, grid=(nm, ns),
                  in_specs=[pl.BlockSpec((1, tm, sw), lambda m, k: (k, m, 0)), pl.BlockSpec((d, sw), lambda m, k: (0, k)),
                            pl.BlockSpec((tm, d), row), pl.BlockSpec((1, d), fix), pl.BlockSpec((tm, d), row)],
                  out_specs=[pl.BlockSpec((tm, d), row), pl.BlockSpec((8, d), fix)],
                  out_shape=[jax.ShapeDtypeStruct((s, d), F32), jax.ShapeDtypeStruct((8, d), F32)],
                  scratch_shapes=[pltpu.VMEM((tm, d), F32)], compiler_params=_cp(2))(dproj, w, x, g, dx1)


def _adamw(w, g, m, v, name):
    r, c = w.shape
    tr = _tile(r, 256)
    c1 = 1.0 - ADAM_B1 ** ADAM_STEP
    c2 = 1.0 - ADAM_B2 ** ADAM_STEP

    def body(w_ref, g_ref, m_ref, v_ref, d_ref, mo_ref, vo_ref):
        gv = g_ref[...]
        mn = ADAM_B1 * m_ref[...] + (1.0 - ADAM_B1) * gv
        vn = ADAM_B2 * v_ref[...] + (1.0 - ADAM_B2) * (gv * gv)
        d_ref[...] = -ADAM_LR * ((mn / c1) / (jnp.sqrt(vn / c2) + ADAM_EPS) + ADAM_WD * w_ref[...])
        mo_ref[...] = mn
        vo_ref[...] = vn

    t = pl.BlockSpec((tr, c), lambda i: (i, 0))
    return _pcall(body, name=name, grid=(r // tr,), in_specs=[t] * 4, out_specs=[t] * 3,
                  out_shape=[jax.ShapeDtypeStruct((r, c), F32)] * 3, compiler_params=_cp(1))(w, g, m, v)


def _add_layer(stack, other, layer, name):
    _, r, c = stack.shape
    tr = _tile(r, 256)

    def body(l_ref, s_ref, o_ref, out_ref):
        out_ref[...] = s_ref[0] + o_ref[...]

    grid_spec = pltpu.PrefetchScalarGridSpec(
        num_scalar_prefetch=1, grid=(r // tr,),
        in_specs=[pl.BlockSpec((1, tr, c), lambda i, l: (l[0], i, 0)), pl.BlockSpec((tr, c), lambda i, l: (i, 0))],
        out_specs=pl.BlockSpec((tr, c), lambda i, l: (i, 0)))
    return _pcall(body, name=name, grid_spec=grid_spec, out_shape=jax.ShapeDtypeStruct((r, c), F32),
                  compiler_params=_cp(1))(layer.reshape(1).astype(jnp.int32), stack, other)


def _sum_shard(full, parts, chip, axis, name):
    _, r, c = parts.shape
    tr = _tile(r, 256)

    def body(k_ref, f_ref, p_ref, out_ref):
        out_ref[...] = ((f_ref[...] + p_ref[0]) + p_ref[1]) + p_ref[2]

    if axis == 1:
        f_spec = pl.BlockSpec((tr, c), lambda i, k: (i, k[0]))
    else:
        nb = r // tr
        f_spec = pl.BlockSpec((tr, c), lambda i, k: (k[0] * nb + i, 0))
    grid_spec = pltpu.PrefetchScalarGridSpec(
        num_scalar_prefetch=1, grid=(r // tr,),
        in_specs=[f_spec, pl.BlockSpec((3, tr, c), lambda i, k: (0, i, 0))],
        out_specs=pl.BlockSpec((tr, c), lambda i, k: (i, 0)))
    return _pcall(body, name=name, grid_spec=grid_spec, out_shape=jax.ShapeDtypeStruct((r, c), F32),
                  compiler_params=_cp(1))(chip.reshape(1).astype(jnp.int32), full, parts)


def _sum_slots(slots, name):
    n = slots.shape[0]

    def body(s_ref, o_ref):
        acc = s_ref[0]
        for i in range(1, n):
            acc = acc + s_ref[i]
        o_ref[...] = acc

    return _pcall(body, name=name, out_shape=jax.ShapeDtypeStruct(slots.shape[1:], F32))(slots)


def _place():
    return lax.axis_index("x"), lax.axis_index("y"), lax.axis_index("c")


def _shard_view(ref, axis, chip, size):
    if axis == 0:
        return ref.at[pl.ds(chip * size, size), :]
    return ref.at[:, pl.ds(chip * size, size)]


SHARD_AXES = (1, 0, 0, 1)


def _gather_weights(shards):
    n = len(shards)
    sizes = [sh.shape[1 + ax] for sh, ax in zip(shards, SHARD_AXES)]
    out_shapes = []
    for sh, ax in zip(shards, SHARD_AXES):
        full = list(sh.shape)
        full[1 + ax] *= 4
        out_shapes.append(jax.ShapeDtypeStruct(tuple(full), sh.dtype))

    def body(*refs):
        srcs, outs = refs[:n], refs[n:2 * n]
        ssem, rsem, lsem = refs[2 * n:]
        x, y, c = _place()
        me = 2 * x + y
        chips = [(1 - x, y), (x, 1 - y), (1 - x, 1 - y)]

        def piece(a, layer, chip):
            return _shard_view(outs[a].at[layer], SHARD_AXES[a], chip, sizes[a])

        local = [pltpu.make_async_copy(srcs[a].at[l], piece(a, l, me), lsem.at[a, l])
                 for a in range(n) for l in range(DEPTH)]
        for cp in local:
            cp.start()
        sends = []
        for a in range(n):
            for j, (cx, cy) in enumerate(chips):
                cp = pltpu.make_async_remote_copy(
                    src_ref=srcs[a].at[c], dst_ref=piece(a, c, me), send_sem=ssem.at[a, j], recv_sem=rsem.at[a, j],
                    device_id=(cx, cy, c), device_id_type=MESH)
                cp.start()
                sends.append(cp)
        for a in range(n):
            for j, (cx, cy) in enumerate(chips):
                got = piece(a, c, 2 * cx + cy)
                pltpu.make_async_remote_copy(
                    src_ref=got, dst_ref=got, send_sem=ssem.at[a, j], recv_sem=rsem.at[a, j],
                    device_id=(cx, cy, c), device_id_type=MESH).wait_recv()
                cp = pltpu.make_async_remote_copy(
                    src_ref=got, dst_ref=got, send_sem=ssem.at[a, 3 + j], recv_sem=rsem.at[a, 3 + j],
                    device_id=(x, y, 1 - c), device_id_type=MESH)
                cp.start()
                sends.append(cp)
        for a in range(n):
            for j, (cx, cy) in enumerate(chips):
                got = piece(a, 1 - c, 2 * cx + cy)
                pltpu.make_async_remote_copy(
                    src_ref=got, dst_ref=got, send_sem=ssem.at[a, 3 + j], recv_sem=rsem.at[a, 3 + j],
                    device_id=(x, y, 1 - c), device_id_type=MESH).wait_recv()
        for cp in sends:
            cp.wait_send()
        for cp in local:
            cp.wait()

    return _pcall(body, name="gather_weights", in_specs=[ANY] * n, out_specs=[ANY] * n, out_shape=out_shapes,
                  scratch_shapes=[pltpu.SemaphoreType.DMA((n, 6)), pltpu.SemaphoreType.DMA((n, 6)),
                                  pltpu.SemaphoreType.DMA((n, DEPTH))])(*shards)


def _swap_layers(stacks):
    n = len(stacks)

    def body(*refs):
        srcs, outs = refs[:n], refs[n:2 * n]
        ssem, rsem = refs[2 * n:]
        x, y, c = _place()
        cps = [pltpu.make_async_remote_copy(src_ref=srcs[a].at[1 - c], dst_ref=outs[a], send_sem=ssem.at[a],
                                            recv_sem=rsem.at[a], device_id=(x, y, 1 - c), device_id_type=MESH)
               for a in range(n)]
        for cp in cps:
            cp.start()
        for cp in cps:
            cp.wait()

    return _pcall(body, name="swap_layers", in_specs=[ANY] * n, out_specs=[ANY] * n,
                  out_shape=[jax.ShapeDtypeStruct(st.shape[1:], st.dtype) for st in stacks],
                  scratch_shapes=[pltpu.SemaphoreType.DMA((n,)), pltpu.SemaphoreType.DMA((n,))])(*stacks)


def _scatter_shards(fulls):
    n = len(fulls)
    shard_shapes = []
    for f, ax in zip(fulls, SHARD_AXES):
        sh = list(f.shape)
        sh[ax] //= 4
        shard_shapes.append(tuple(sh))

    def body(*refs):
        srcs, outs = refs[:n], refs[n:2 * n]
        ssem, rsem = refs[2 * n:]
        x, y, c = _place()
        chips = [(1 - x, y), (x, 1 - y), (1 - x, 1 - y)]
        cps = []
        for a in range(n):
            for j, (cx, cy) in enumerate(chips):
                src = _shard_view(srcs[a], SHARD_AXES[a], 2 * cx + cy, shard_shapes[a][SHARD_AXES[a]])
                cps.append(pltpu.make_async_remote_copy(
                    src_ref=src, dst_ref=outs[a].at[j], send_sem=ssem.at[a, j], recv_sem=rsem.at[a, j],
                    device_id=(cx, cy, c), device_id_type=MESH))
        for cp in cps:
            cp.start()
        for cp in cps:
            cp.wait()

    return _pcall(body, name="scatter_shards", in_specs=[ANY] * n, out_specs=[ANY] * n,
                  out_shape=[jax.ShapeDtypeStruct((3,) + sh, F32) for sh in shard_shapes],
                  scratch_shapes=[pltpu.SemaphoreType.DMA((n, 3)), pltpu.SemaphoreType.DMA((n, 3))])(*fulls)


def _pair_layers(reduced):
    n = len(reduced)

    def body(*refs):
        srcs, outs = refs[:n], refs[n:2 * n]
        ssem, rsem, lsem = refs[2 * n:]
        x, y, c = _place()
        local = [pltpu.make_async_copy(srcs[a], outs[a].at[c], lsem.at[a]) for a in range(n)]
        for cp in local:
            cp.start()
        cps = [pltpu.make_async_remote_copy(src_ref=srcs[a], dst_ref=outs[a].at[c], send_sem=ssem.at[a],
                                            recv_sem=rsem.at[a], device_id=(x, y, 1 - c), device_id_type=MESH)
               for a in range(n)]
        for cp in cps:
            cp.start()
        for a in range(n):
            got = outs[a].at[1 - c]
            pltpu.make_async_remote_copy(src_ref=got, dst_ref=got, send_sem=ssem.at[a], recv_sem=rsem.at[a],
                                         device_id=(x, y, 1 - c), device_id_type=MESH).wait_recv()
        for cp in cps:
            cp.wait_send()
        for cp in local:
            cp.wait()

    return _pcall(body, name="pair_layers", in_specs=[ANY] * n, out_specs=[ANY] * n,
                  out_shape=[jax.ShapeDtypeStruct((DEPTH,) + r.shape, r.dtype) for r in reduced],
                  scratch_shapes=[pltpu.SemaphoreType.DMA((n,)), pltpu.SemaphoreType.DMA((n,)),
                                  pltpu.SemaphoreType.DMA((n,))])(*reduced)


def _exchange_small(pack, name):
    nd = 8

    def body(p_ref, o_ref, ssem, rsem):
        x, y, c = _place()
        me = 4 * x + 2 * y + c
        o_ref[me] = p_ref[...]
        cps = []
        for j in range(1, nd):
            px, py, pc = x ^ (j >> 2), y ^ ((j >> 1) & 1), c ^ (j & 1)
            cps.append(pltpu.make_async_remote_copy(
                src_ref=p_ref, dst_ref=o_ref.at[me], send_sem=ssem.at[j - 1], recv_sem=rsem.at[j - 1],
                device_id=(px, py, pc), device_id_type=MESH))
        for cp in cps:
            cp.start()
        for j in range(1, nd):
            peer = me ^ j
            got = o_ref.at[peer]
            pltpu.make_async_remote_copy(src_ref=got, dst_ref=got, send_sem=ssem.at[j - 1], recv_sem=rsem.at[j - 1],
                                         device_id=(x, y, c), device_id_type=MESH).wait_recv()
        for cp in cps:
            cp.wait_send()

    vm = pl.BlockSpec(memory_space=pltpu.VMEM)
    return _pcall(body, name=name, in_specs=[vm], out_specs=vm,
                  out_shape=jax.ShapeDtypeStruct((nd,) + pack.shape, pack.dtype),
                  scratch_shapes=[pltpu.SemaphoreType.DMA((nd - 1,)), pltpu.SemaphoreType.DMA((nd - 1,))])(pack)


def _row(v):
    return v.reshape(1, -1)


def _local_step(x, p, tgt, norm_g, conv_w, conv_b, branch_g, ple_norm_g, b_pg, final_g, w_in, w_out, w_pg, w_pe):
    saved = []
    xl = x
    for l in range(DEPTH):
        h = _rms_fwd(xl, _row(norm_g[l]), f"rms_fwd_{l}")
        proj = _inproj(h, w_in[l], f"inproj_{l}")
        ya, tl = _attn_fwd(proj, f"attn_fwd_{l}")
        y = _mix_fwd(proj, ya, conv_w[l], _row(conv_b[l]), _row(branch_g[l]), f"mix_fwd_{l}")
        x1, hn = _outproj(y, w_out[l], xl, _row(ple_norm_g[l]), f"outproj_{l}")
        x2, gate, e = _ple_fwd(hn, w_pg[l], _row(b_pg[l]), p[l], w_pe[l], x1, f"ple_fwd_{l}")
        saved.append((xl, h, proj, ya, tl, y, x1, hn, gate, e))
        xl = x2

    sq, dx, d_final = _loss_head(xl, tgt, _row(final_g), "loss_head")

    g_in = g_out = g_pg = g_pe = None
    small = {k: [None] * DEPTH for k in ("norm_g", "conv_w", "conv_b", "branch_g", "ple_norm_g", "b_pg")}
    for l in reversed(range(DEPTH)):
        xl, h, proj, ya, tl, y, x1, hn, gate, e = saved[l]
        du, de, dx1, dy, db_pg, d_ple = _ple_bwd(dx, gate, e, x1, w_pg[l], _row(ple_norm_g[l]), w_out[l], f"ple_bwd_{l}")
        g_pg = _mm_tn(hn, du, g_pg, l, f"grad_w_pg_{l}")
        g_pe = _mm_tn(p[l], de, g_pe, l, f"grad_w_pe_{l}")
        g_out = _mm_tn(y, dx1, g_out, l, f"grad_w_out_{l}")
        dpc, d_cw, d_cb, d_bg_c = _convmix_bwd(dy, proj, conv_w[l], _row(conv_b[l]), _row(branch_g[l]), f"convmix_bwd_{l}")
        dya, daz, d_bg_a = _attnmix_bwd(dy, ya, proj, _row(branch_g[l]), f"attnmix_bwd_{l}")
        dq, dk, dv = _attn_bwd(proj, dya, tl, f"attn_bwd_{l}")
        dproj = jnp.concatenate([dpc, dq[None], dk[None], dv[None], daz[None]], axis=0)
        g_in = _mm_tn(h, dproj, g_in, l, f"grad_w_in_{l}", b_sections=True)
        dx, d_norm = _inproj_bwd(dproj, w_in[l], xl, _row(norm_g[l]), dx1, f"inproj_bwd_{l}")
        small["norm_g"][l] = jnp.sum(d_norm, axis=0)
        small["conv_w"][l] = jnp.sum(d_cw, axis=1)
        small["conv_b"][l] = jnp.sum(d_cb, axis=0)
        small["branch_g"][l] = jnp.concatenate([jnp.sum(d_bg_c, axis=0), jnp.sum(d_bg_a, axis=0)])
        small["ple_norm_g"][l] = jnp.sum(d_ple, axis=0)
        small["b_pg"][l] = jnp.sum(db_pg, axis=0)
    small = {k: jnp.stack(v) for k, v in small.items()}
    small["final_g"] = jnp.sum(d_final, axis=0)
    return sq[0, 0], dx, (g_in, g_out, g_pg, g_pe), small


SMALL_ORDER = ("norm_g", "conv_w", "conv_b", "branch_g", "ple_norm_g", "b_pg", "final_g")


def _pack(parts, width):
    flat = jnp.concatenate([v.reshape(-1) for v in parts])
    rows = -(-flat.shape[0] // width)
    rows = -(-rows // 8) * 8
    return jnp.pad(flat, (0, rows * width - flat.shape[0])).reshape(rows, width)


def _unpack(packed, like):
    flat = packed.reshape(-1)
    out, off = [], 0
    for v in like:
        out.append(flat[off:off + v.size].reshape(v.shape))
        off += v.size
    return out


def kernel(x, p, norm_g, w_in, conv_w, conv_b, branch_g, w_out, ple_norm_g, w_pg, b_pg, w_pe, final_g, loss_target, m_norm_g, m_w_in, m_conv_w, m_conv_b, m_branch_g, m_w_out, m_ple_norm_g, m_w_pg, m_b_pg, m_w_pe, m_final_g, v_norm_g, v_w_in, v_conv_w, v_conv_b, v_branch_g, v_w_out, v_ple_norm_g, v_w_pg, v_b_pg, v_w_pe, v_final_g):
    ix, iy, ic = _place()
    chip = 2 * ix + iy
    d = x.shape[-1]

    big_w = (w_in, w_out, w_pg, w_pe)
    shards = [_cast_bf16(w, f"cast_{i}") for i, w in enumerate(big_w)]
    full_in, full_out, full_pg, full_pe = _gather_weights(shards)
    cw_shard = conv_w.shape[-1]
    cw_slots = _exchange_small(_pack([conv_w], LANES), "exchange_conv_w")
    conv_full = jnp.concatenate([_unpack(cw_slots[2 * k], [conv_w])[0] for k in range(4)], axis=-1)

    sq, dx, big_g, small_g = _local_step(
        x[0], p[:, 0], loss_target[0], norm_g, conv_full, conv_b, branch_g, ple_norm_g, b_pg, final_g,
        full_in, full_out, full_pg, full_pe)

    from_sibling = _swap_layers(big_g)
    chip_sums = [_add_layer(g, o, ic, f"add_layer_{i}") for i, (g, o) in enumerate(zip(big_g, from_sibling))]
    partials = _scatter_shards(chip_sums)
    reduced = [_sum_shard(f, pr, chip, ax, f"sum_shard_{i}")
               for i, (f, pr, ax) in enumerate(zip(chip_sums, partials, SHARD_AXES))]
    g_big = _pair_layers(reduced)

    parts = [small_g[k] for k in SMALL_ORDER] + [sq.reshape(1)]
    slots = _exchange_small(_pack(parts, d), "exchange_small_grads")
    total = _unpack(_sum_slots(slots, "sum_small"), parts)
    g_small = dict(zip(SMALL_ORDER, total[:-1]))
    loss = 0.5 * total[-1][0] / d
    g_small["conv_w"] = lax.dynamic_slice_in_dim(g_small["conv_w"], chip * cw_shard, cw_shard, axis=2)

    grads = dict(g_small)
    grads.update(w_in=g_big[0], w_out=g_big[1], w_pg=g_big[2], w_pe=g_big[3])
    weights = dict(norm_g=norm_g, w_in=w_in, conv_w=conv_w, conv_b=conv_b, branch_g=branch_g, w_out=w_out,
                   ple_norm_g=ple_norm_g, w_pg=w_pg, b_pg=b_pg, w_pe=w_pe, final_g=final_g)
    ms = dict(norm_g=m_norm_g, w_in=m_w_in, conv_w=m_conv_w, conv_b=m_conv_b, branch_g=m_branch_g, w_out=m_w_out,
              ple_norm_g=m_ple_norm_g, w_pg=m_w_pg, b_pg=m_b_pg, w_pe=m_w_pe, final_g=m_final_g)
    vs = dict(norm_g=v_norm_g, w_in=v_w_in, conv_w=v_conv_w, conv_b=v_conv_b, branch_g=v_branch_g, w_out=v_w_out,
              ple_norm_g=v_ple_norm_g, w_pg=v_w_pg, b_pg=v_b_pg, w_pe=v_w_pe, final_g=v_final_g)
    names = ("norm_g", "w_in", "conv_w", "conv_b", "branch_g", "w_out", "ple_norm_g", "w_pg", "b_pg", "w_pe", "final_g")
    delta, new_m, new_v = {}, {}, {}
    for k in ("w_in", "w_out", "w_pg", "w_pe"):
        shp = weights[k].shape
        two = lambda a: a.reshape(-1, shp[-1])
        dl, mn, vn = _adamw(two(weights[k]), two(grads[k]), two(ms[k]), two(vs[k]), f"adamw_{k}")
        delta[k], new_m[k], new_v[k] = dl.reshape(shp), mn.reshape(shp), vn.reshape(shp)
        grads[k] = grads[k].reshape(shp)
    like = [weights[k] for k in SMALL_ORDER]
    packs = [_pack([src[k] for k in SMALL_ORDER], d) for src in (weights, grads, ms, vs)]
    outs = _adamw(*packs, "adamw_small")
    for res, o in zip((delta, new_m, new_v), outs):
        res.update(dict(zip(SMALL_ORDER, _unpack(o, like))))

    return (loss, dx[None], *[grads[k] for k in names], *[delta[k] for k in names],
            *[new_m[k] for k in names], *[new_v[k] for k in names])
```

```python
import math

import jax
import jax.numpy as jnp
from jax import lax
from jax.experimental import pallas as pl
from jax.experimental.pallas import tpu as pltpu

F32 = jnp.float32
BF16 = jnp.bfloat16
EPS = 1e-6
HEAD = 64
LANES = 128
ATT_T = 256
DEPTH = 2
VMEM_LIMIT = 56 * 1024 * 1024
MESH = pl.DeviceIdType.MESH
ANY = pl.BlockSpec(memory_space=pl.ANY)

ADAM_LR = 0.001
ADAM_B1 = 0.9
ADAM_B2 = 0.999
ADAM_EPS = 1e-08
ADAM_WD = 0.01
ADAM_STEP = 10


def _pcall(body, **kw):
    return pl.pallas_call(body, **kw)


def _cp(n_axes):
    return pltpu.CompilerParams(dimension_semantics=("arbitrary",) * n_axes, vmem_limit_bytes=VMEM_LIMIT)


def _tile(n, pref):
    return pref if n % pref == 0 else n


def _split_dot(a, b, passes):
    out = None
    rem = a
    for _ in range(passes):
        hi = rem.astype(BF16)
        t = jnp.dot(hi, b, preferred_element_type=F32)
        out = t if out is None else out + t
        rem = rem - hi.astype(F32)
    return out


def _group_mat():
    r = lax.broadcasted_iota(jnp.int32, (LANES, LANES), 0) // HEAD
    c = lax.broadcasted_iota(jnp.int32, (LANES, LANES), 1) // HEAD
    return jnp.where(r == c, 1.0 / HEAD, 0.0).astype(BF16)


def _group_mean(v, gm):
    return _split_dot(v, gm, 3)


def _sigmoid(z):
    return 1.0 / (1.0 + jnp.exp(-z))


def _dot_nt(a, b):
    return lax.dot_general(a, b, (((1,), (1,)), ((), ())), preferred_element_type=F32)


def _dot_tn(a, b):
    return lax.dot_general(a, b, (((0,), (0,)), ((), ())), preferred_element_type=F32)


def _cast_bf16(w, name):
    shape = w.shape
    w2 = w.reshape(-1, shape[-1])
    r, c = w2.shape
    tr = _tile(r, 512)

    def body(w_ref, o_ref):
        o_ref[...] = w_ref[...].astype(BF16)

    out = _pcall(body, name=name, grid=(r // tr,),
                 in_specs=[pl.BlockSpec((tr, c), lambda i: (i, 0))],
                 out_specs=pl.BlockSpec((tr, c), lambda i: (i, 0)),
                 out_shape=jax.ShapeDtypeStruct((r, c), BF16), compiler_params=_cp(1))(w2)
    return out.reshape(shape)


def _rms_fwd(x, g, name):
    s, d = x.shape
    tm = _tile(s, 512)

    def body(x_ref, g_ref, h_ref):
        xv = x_ref[...]
        r = lax.rsqrt(jnp.mean(xv * xv, axis=-1, keepdims=True) + EPS)
        h_ref[...] = (xv * r * g_ref[...]).astype(BF16)

    return _pcall(body, name=name, grid=(s // tm,),
                  in_specs=[pl.BlockSpec((tm, d), lambda m: (m, 0)), pl.BlockSpec((1, d), lambda m: (0, 0))],
                  out_specs=pl.BlockSpec((tm, d), lambda m: (m, 0)),
                  out_shape=jax.ShapeDtypeStruct((s, d), BF16), compiler_params=_cp(1))(x, g)


def _rms_bwd_rows(dh, xv, g):
    r = lax.rsqrt(jnp.mean(xv * xv, axis=-1, keepdims=True) + EPS)
    xn = xv * r
    dxn = dh * g
    dx = r * (dxn - xn * jnp.mean(dxn * xn, axis=-1, keepdims=True))
    return dx, dh * xn


def _colsum8(v):
    tm, d = v.shape
    return jnp.sum(v.reshape(tm // 8, 8, d), axis=0)


def _inproj(h, w, name):
    s, d = h.shape
    n = w.shape[1]
    sw = d // 2
    ns = n // sw
    tm = _tile(s, 512)

    def body(h_ref, w_ref, o_ref):
        o_ref[0] = jnp.dot(h_ref[...], w_ref[...], preferred_element_type=F32).astype(BF16)

    return _pcall(body, name=name, grid=(ns, s // tm),
                  in_specs=[pl.BlockSpec((tm, d), lambda j, m: (m, 0)), pl.BlockSpec((d, sw), lambda j, m: (0, j))],
                  out_specs=pl.BlockSpec((1, tm, sw), lambda j, m: (j, m, 0)),
                  out_shape=jax.ShapeDtypeStruct((ns, s, sw), BF16), compiler_params=_cp(2))(h, w)


def _softplus_parts(z):
    l1 = jnp.log(1.0 + jnp.exp(-jnp.abs(z)))
    return jnp.minimum(-z, 0.0) - l1, jnp.minimum(z, 0.0) - l1


def _attn_fwd(proj, name):
    _, s, sw = proj.shape
    nhp = sw // LANES
    t = _tile(s, ATT_T)
    nq = s // t
    scale = 1.0 / math.sqrt(HEAD)

    def body(q_ref, k_ref, v_ref, o_ref, tl_ref, acc_ref):
        i = pl.program_id(1)
        row = lax.broadcasted_iota(jnp.int32, (t, t), 0)
        col = lax.broadcasted_iota(jnp.int32, (t, t), 1)
        tri = (row > col).astype(BF16)
        below = col < row
        lane = lax.broadcasted_iota(jnp.int32, (t, LANES), 1)
        q = q_ref[0]
        qms = [jnp.where((lane // HEAD) == h, q, jnp.zeros_like(q)) for h in range(2)]
        acc_ref[...] = jnp.zeros_like(acc_ref)

        def tile(j, rsums, diagonal):
            k0 = pl.multiple_of(j * t, t)
            kj = k_ref[0, pl.ds(k0, t), :]
            vj = v_ref[0, pl.ds(k0, t), :]
            out = []
            for h in range(2):
                z = _dot_nt(qms[h], kj) * scale
                lm, ls = _softplus_parts(z)
                if diagonal:
                    lm = jnp.where(below, lm, 0.0)
                a = jnp.exp(ls + (rsums[h] + _split_dot(lm, tri, 2)))
                if diagonal:
                    a = jnp.where(below, a, 0.0)
                acc_ref[h] += jnp.dot(a.astype(BF16), vj, preferred_element_type=F32)
                out.append(rsums[h] + jnp.sum(lm, axis=1, keepdims=True))
            return tuple(out)

        z1 = jnp.zeros((t, 1), F32)
        rsums = tile(i, (z1, z1), True)
        rsums = lax.fori_loop(0, i, lambda jj, c: tile(i - 1 - jj, c, False), rsums)
        tl_ref[0] = rsums[0]
        tl_ref[1] = rsums[1]
        o_ref[...] = jnp.where(lane < HEAD, acc_ref[0], acc_ref[1]).astype(BF16)

    return _pcall(
        body, name=name, grid=(nhp, nq),
        in_specs=[pl.BlockSpec((1, t, LANES), lambda hp, i: (4, i, hp)),
                  pl.BlockSpec((1, s, LANES), lambda hp, i: (5, 0, hp)),
                  pl.BlockSpec((1, s, LANES), lambda hp, i: (6, 0, hp))],
        out_specs=[pl.BlockSpec((t, LANES), lambda hp, i: (i, hp)),
                   pl.BlockSpec((2, t, 1), lambda hp, i: (hp, i, 0))],
        out_shape=[jax.ShapeDtypeStruct((s, sw), BF16), jax.ShapeDtypeStruct((2 * nhp, s, 1), F32)],
        scratch_shapes=[pltpu.VMEM((2, t, LANES), F32)],
        compiler_params=_cp(2))(proj, proj, proj)


def _conv_rows(cc_ref, ch_ref, w_ref, b_ref, r, tc):
    r0 = pl.multiple_of(r * tc, tc)
    u = cc_ref[0, pl.ds(r0, tc), :].astype(F32) * ch_ref[0, pl.ds(r0, tc), :].astype(F32)
    p0 = pl.multiple_of(jnp.maximum(r0 - 16, 0), 16)
    up = cc_ref[0, pl.ds(p0, 16), :].astype(F32) * ch_ref[0, pl.ds(p0, 16), :].astype(F32)
    up = up * (r > 0).astype(F32)
    prev1 = up[15:16, :]
    prev2 = up[14:15, :]
    rid = lax.broadcasted_iota(jnp.int32, u.shape, 0)
    s1 = jnp.where(rid == 0, prev1, pltpu.roll(u, 1, axis=0))
    s2 = jnp.where(rid == 0, prev2, jnp.where(rid == 1, prev1, pltpu.roll(u, 2, axis=0)))
    cv = b_ref[...] + s2 * w_ref[0:1, :] + s1 * w_ref[1:2, :] + u * w_ref[2:3, :]
    return r0, u, s1, s2, cv


def _mix_fwd(proj, ya, conv_w, conv_b, bg, name):
    _, s, sw = proj.shape
    nh = sw // LANES
    tc = _tile(s, 256)

    def body(cb_ref, cc_ref, ch_ref, cz_ref, ya_ref, az_ref, w_ref, b_ref, g_ref, y_ref):
        c = pl.program_id(0)
        gm = _group_mat()

        def finish(r0, yv, zg):
            n = yv * lax.rsqrt(_group_mean(yv * yv, gm) + EPS)
            y_ref[pl.ds(r0, tc), :] = (n * g_ref[...] * (zg * _sigmoid(zg))).astype(BF16)

        @pl.when(c < nh)
        def _():
            def step(r, carry):
                r0, _, _, _, cv = _conv_rows(cc_ref, ch_ref, w_ref, b_ref, r, tc)
                yc = cb_ref[0, pl.ds(r0, tc), :].astype(F32) * cv
                finish(r0, yc, cz_ref[0, pl.ds(r0, tc), :].astype(F32))
                return carry
            lax.fori_loop(0, s // tc, step, 0)

        @pl.when(c >= nh)
        def _():
            def step(r, carry):
                r0 = pl.multiple_of(r * tc, tc)
                finish(r0, ya_ref[pl.ds(r0, tc), :].astype(F32), az_ref[0, pl.ds(r0, tc), :].astype(F32))
                return carry
            lax.fori_loop(0, s // tc, step, 0)

    def sec(k):
        return pl.BlockSpec((1, s, LANES), lambda c: (k, 0, jnp.minimum(c, nh - 1)))

    return _pcall(
        body, name=name, grid=(2 * nh,),
        in_specs=[sec(0), sec(1), sec(2), sec(3),
                  pl.BlockSpec((s, LANES), lambda c: (0, jnp.maximum(c - nh, 0))),
                  pl.BlockSpec((1, s, LANES), lambda c: (7, 0, jnp.maximum(c - nh, 0))),
                  pl.BlockSpec((3, LANES), lambda c: (0, jnp.minimum(c, nh - 1))),
                  pl.BlockSpec((1, LANES), lambda c: (0, jnp.minimum(c, nh - 1))),
                  pl.BlockSpec((1, LANES), lambda c: (0, c))],
        out_specs=pl.BlockSpec((s, LANES), lambda c: (0, c)),
        out_shape=jax.ShapeDtypeStruct((s, 2 * sw), BF16), compiler_params=_cp(1),
    )(proj, proj, proj, proj, ya, proj, conv_w, conv_b, bg)


def _outproj(y, w, x, g, name):
    s, d = x.shape
    tm = _tile(s, 256)

    def body(y_ref, w_ref, x_ref, g_ref, x1_ref, hn_ref):
        x1 = x_ref[...] + jnp.dot(y_ref[...], w_ref[...], preferred_element_type=F32)
        x1_ref[...] = x1
        r = lax.rsqrt(jnp.mean(x1 * x1, axis=-1, keepdims=True) + EPS)
        hn_ref[...] = (x1 * r * g_ref[...]).astype(BF16)

    row = lambda m: (m, 0)
    fix = lambda m: (0, 0)
    return _pcall(body, name=name, grid=(s // tm,),
                  in_specs=[pl.BlockSpec((tm, d), row), pl.BlockSpec((d, d), fix), pl.BlockSpec((tm, d), row),
                            pl.BlockSpec((1, d), fix)],
                  out_specs=[pl.BlockSpec((tm, d), row), pl.BlockSpec((tm, d), row)],
                  out_shape=[jax.ShapeDtypeStruct((s, d), F32), jax.ShapeDtypeStruct((s, d), BF16)],
                  compiler_params=_cp(1))(y, w, x, g)


def _ple_fwd(hn, w_pg, b_pg, p, w_pe, x1, name):
    s, d = x1.shape
    pd = p.shape[1]
    tm = _tile(s, 256)

    def body(hn_ref, wg_ref, b_ref, p_ref, we_ref, x1_ref, x2_ref, gate_ref, e_ref):
        gate = _sigmoid(jnp.dot(hn_ref[...], wg_ref[...], preferred_element_type=F32) + b_ref[...])
        e = jnp.dot(p_ref[...].astype(BF16), we_ref[...], preferred_element_type=F32)
        x2_ref[...] = x1_ref[...] + gate * e
        gate_ref[...] = gate.astype(BF16)
        e_ref[...] = e.astype(BF16)

    row = lambda m: (m, 0)
    fix = lambda m: (0, 0)
    return _pcall(body, name=name, grid=(s // tm,),
                  in_specs=[pl.BlockSpec((tm, d), row), pl.BlockSpec((d, d), fix), pl.BlockSpec((1, d), fix),
                            pl.BlockSpec((tm, pd), row), pl.BlockSpec((pd, d), fix), pl.BlockSpec((tm, d), row)],
                  out_specs=[pl.BlockSpec((tm, d), row)] * 3,
                  out_shape=[jax.ShapeDtypeStruct((s, d), F32), jax.ShapeDtypeStruct((s, d), BF16),
                             jax.ShapeDtypeStruct((s, d), BF16)],
                  compiler_params=_cp(1))(hn, w_pg, b_pg, p, w_pe, x1)


def _loss_head(x, tgt, g, name):
    s, d = x.shape
    tm = _tile(s, 256)

    def body(x_ref, t_ref, g_ref, l_ref, dx_ref, dg_ref):
        m = pl.program_id(0)

        @pl.when(m == 0)
        def _():
            l_ref[...] = jnp.zeros_like(l_ref)
            dg_ref[...] = jnp.zeros_like(dg_ref)

        xv = x_ref[...]
        gv = g_ref[...]
        r = lax.rsqrt(jnp.mean(xv * xv, axis=-1, keepdims=True) + EPS)
        xn = xv * r
        err = xn * gv - t_ref[...]
        l_ref[...] += jnp.sum(err * err)
        dy = err * (1.0 / d)
        dxn = dy * gv
        dx_ref[...] = r * (dxn - xn * jnp.mean(dxn * xn, axis=-1, keepdims=True))
        dg_ref[...] += _colsum8(dy * xn)

    row = lambda m: (m, 0)
    fix = lambda m: (0, 0)
    return _pcall(body, name=name, grid=(s // tm,),
                  in_specs=[pl.BlockSpec((tm, d), row), pl.BlockSpec((tm, d), row), pl.BlockSpec((1, d), fix)],
                  out_specs=[pl.BlockSpec((8, LANES), fix), pl.BlockSpec((tm, d), row), pl.BlockSpec((8, d), fix)],
                  out_shape=[jax.ShapeDtypeStruct((8, LANES), F32), jax.ShapeDtypeStruct((s, d), F32),
                             jax.ShapeDtypeStruct((8, d), F32)],
                  compiler_params=_cp(1))(x, tgt, g)


def _ple_bwd(dx2, gate, e, x1, w_pg, g_ple, w_out, name):
    s, d = dx2.shape
    tm = _tile(s, 256)

    def body(dx2_ref, gate_ref, e_ref, x1_ref, wg_ref, g_ref, wo_ref,
             du_ref, de_ref, dx1_ref, dy_ref, db_ref, dg_ref):
        m = pl.program_id(0)

        @pl.when(m == 0)
        def _():
            db_ref[...] = jnp.zeros_like(db_ref)
            dg_ref[...] = jnp.zeros_like(dg_ref)

        dx2v = dx2_ref[...]
        gate = gate_ref[...].astype(F32)
        du = dx2v * e_ref[...].astype(F32) * gate * (1.0 - gate)
        de_ref[...] = (dx2v * gate).astype(BF16)
        dub = du.astype(BF16)
        du_ref[...] = dub
        db_ref[...] += _colsum8(du)
        dhn = _dot_nt(dub, wg_ref[...])
        dxr, dgr = _rms_bwd_rows(dhn, x1_ref[...], g_ref[...])
        dx1 = dx2v + dxr
        dx1_ref[...] = dx1
        dg_ref[...] += _colsum8(dgr)
        dy_ref[...] = _dot_nt(dx1.astype(BF16), wo_ref[...]).astype(BF16)

    row = lambda m: (m, 0)
    fix = lambda m: (0, 0)
    t = pl.BlockSpec((tm, d), row)
    return _pcall(body, name=name, grid=(s // tm,),
                  in_specs=[t, t, t, t, pl.BlockSpec((d, d), fix), pl.BlockSpec((1, d), fix), pl.BlockSpec((d, d), fix)],
                  out_specs=[t, t, t, t, pl.BlockSpec((8, d), fix), pl.BlockSpec((8, d), fix)],
                  out_shape=[jax.ShapeDtypeStruct((s, d), BF16), jax.ShapeDtypeStruct((s, d), BF16),
                             jax.ShapeDtypeStruct((s, d), F32), jax.ShapeDtypeStruct((s, d), BF16),
                             jax.ShapeDtypeStruct((8, d), F32), jax.ShapeDtypeStruct((8, d), F32)],
                  compiler_params=_cp(1))(dx2, gate, e, x1, w_pg, g_ple, w_out)


def _mm_tn(a, b, stack, layer, name, b_sections=False):
    s, ka = a.shape
    if b_sections:
        ns, _, tn = b.shape
        n = ns * tn
    else:
        n = b.shape[1]
        tn = _tile(n, 1024)
        ns = n // tn
    tk = _tile(s, 512)
    nk = s // tk

    def body(*refs):
        a_ref, b_ref = refs[0], refs[1]
        o_ref, acc_ref = refs[-2], refs[-1]
        k = pl.program_id(1)

        @pl.when(k == 0)
        def _():
            acc_ref[...] = jnp.zeros_like(acc_ref)

        bv = b_ref[0] if b_sections else b_ref[...]
        acc_ref[...] += _dot_tn(a_ref[...].astype(BF16), bv.astype(BF16))

        @pl.when(k == nk - 1)
        def _():
            o_ref[0] = acc_ref[...]

    b_spec = (pl.BlockSpec((1, tk, tn), lambda j, k: (j, k, 0)) if b_sections
              else pl.BlockSpec((tk, tn), lambda j, k: (k, j)))
    in_specs = [pl.BlockSpec((tk, ka), lambda j, k: (k, 0)), b_spec]
    args = [a, b]
    aliases = {}
    if stack is not None:
        in_specs.append(ANY)
        args.append(stack)
        aliases = {2: 0}
    return _pcall(body, name=name, grid=(ns, nk), in_specs=in_specs,
                  out_specs=pl.BlockSpec((1, ka, tn), lambda j, k: (layer, 0, j)),
                  out_shape=jax.ShapeDtypeStruct((DEPTH, ka, n), F32),
                  scratch_shapes=[pltpu.VMEM((ka, tn), F32)], input_output_aliases=aliases,
                  compiler_params=_cp(2))(*args)


def _norm_gate_bwd(dy, yv, zg, g, gm):
    r = lax.rsqrt(_group_mean(yv * yv, gm) + EPS)
    n = yv * r
    sg = _sigmoid(zg)
    sil = zg * sg
    dzg = dy * n * g * (sg * (1.0 + zg * (1.0 - sg)))
    dn = dy * g * sil
    dyv = r * (dn - n * _group_mean(dn * n, gm))
    return dyv, dzg, dy * n * sil


def _convmix_bwd(dy, proj, conv_w, conv_b, bg, name):
    _, s, sw = proj.shape
    nh = sw // LANES
    tc = _tile(s, 256)
    nr = s // tc

    def body(dy_ref, cb_ref, cc_ref, ch_ref, cz_ref, w_ref, b_ref, g_ref,
             dp_ref, dw_ref, db_ref, dg_ref, dcv_ref):
        gm = _group_mat()
        dcv_ref[pl.ds(s, 8), :] = jnp.zeros((8, LANES), F32)

        def pass1(r, carry):
            dw0, dw1, dw2, db, dg = carry
            r0, u, s1, s2, cv = _conv_rows(cc_ref, ch_ref, w_ref, b_ref, r, tc)
            cb = cb_ref[0, pl.ds(r0, tc), :].astype(F32)
            dyc, dcz, dgr = _norm_gate_bwd(dy_ref[pl.ds(r0, tc), :].astype(F32), cb * cv,
                                           cz_ref[0, pl.ds(r0, tc), :].astype(F32), g_ref[...], gm)
            dp_ref[0, pl.ds(r0, tc), :] = (dyc * cv).astype(BF16)
            dp_ref[3, pl.ds(r0, tc), :] = dcz.astype(BF16)
            dcv = dyc * cb
            dcv_ref[pl.ds(r0, tc), :] = dcv
            return (dw0 + _colsum8(dcv * s2), dw1 + _colsum8(dcv * s1), dw2 + _colsum8(dcv * u),
                    db + _colsum8(dcv), dg + _colsum8(dgr))

        z8 = jnp.zeros((8, LANES), F32)
        dw0, dw1, dw2, db, dg = lax.fori_loop(0, nr, pass1, (z8, z8, z8, z8, z8))
        dw_ref[0] = dw0
        dw_ref[1] = dw1
        dw_ref[2] = dw2
        db_ref[...] = db
        dg_ref[...] = dg

        def pass2(r, carry):
            r0 = pl.multiple_of(r * tc, tc)
            dcv = dcv_ref[pl.ds(r0, tc), :]
            nxt = dcv_ref[pl.ds(pl.multiple_of(r0 + tc, 8), 8), :]
            rid = lax.broadcasted_iota(jnp.int32, dcv.shape, 0)
            n1 = jnp.where(rid == tc - 1, nxt[0:1, :], pltpu.roll(dcv, tc - 1, axis=0))
            n2 = jnp.where(rid == tc - 1, nxt[1:2, :],
                           jnp.where(rid == tc - 2, nxt[0:1, :], pltpu.roll(dcv, tc - 2, axis=0)))
            du = dcv * w_ref[2:3, :] + n1 * w_ref[1:2, :] + n2 * w_ref[0:1, :]
            dp_ref[1, pl.ds(r0, tc), :] = (du * ch_ref[0, pl.ds(r0, tc), :].astype(F32)).astype(BF16)
            dp_ref[2, pl.ds(r0, tc), :] = (du * cc_ref[0, pl.ds(r0, tc), :].astype(F32)).astype(BF16)
            return carry

        lax.fori_loop(0, nr, pass2, 0)

    def sec(k):
        return pl.BlockSpec((1, s, LANES), lambda c: (k, 0, c))

    col = lambda c: (0, c)
    return _pcall(
        body, name=name, grid=(nh,),
        in_specs=[pl.BlockSpec((s, LANES), col), sec(0), sec(1), sec(2), sec(3),
                  pl.BlockSpec((3, LANES), col), pl.BlockSpec((1, LANES), col), pl.BlockSpec((1, LANES), col)],
        out_specs=[pl.BlockSpec((4, s, LANES), lambda c: (0, 0, c)), pl.BlockSpec((3, 8, LANES), lambda c: (0, 0, c)),
                   pl.BlockSpec((8, LANES), col), pl.BlockSpec((8, LANES), col)],
        out_shape=[jax.ShapeDtypeStruct((8, s, sw), BF16), jax.ShapeDtypeStruct((3, 8, sw), F32),
                   jax.ShapeDtypeStruct((8, sw), F32), jax.ShapeDtypeStruct((8, sw), F32)],
        scratch_shapes=[pltpu.VMEM((s + 8, LANES), F32)], compiler_params=_cp(1),
    )(dy, proj, proj, proj, proj, conv_w, conv_b, bg)


def _fill_sections(buf, parts, first, name):
    n = len(parts)

    def body(*refs):
        srcs, out, sem = refs[1:1 + n], refs[1 + n], refs[2 + n]
        cps = [pltpu.make_async_copy(srcs[a], out.at[first + a], sem.at[a]) for a in range(n)]
        for cp in cps:
            cp.start()
        for cp in cps:
            cp.wait()

    return _pcall(body, name=name, in_specs=[ANY] * (n + 1), out_specs=ANY,
                  out_shape=jax.ShapeDtypeStruct(buf.shape, buf.dtype), input_output_aliases={0: 0},
                  scratch_shapes=[pltpu.SemaphoreType.DMA((n,))])(buf, *parts)


def _attnmix_bwd(dy, ya, proj, bg, name):
    _, s, sw = proj.shape
    nh = sw // LANES
    tc = _tile(s, 256)

    def body(dy_ref, ya_ref, az_ref, g_ref, dya_ref, daz_ref, dg_ref):
        gm = _group_mat()

        def step(r, dg):
            r0 = pl.multiple_of(r * tc, tc)
            dyv, dzg, dgr = _norm_gate_bwd(dy_ref[pl.ds(r0, tc), :].astype(F32), ya_ref[pl.ds(r0, tc), :].astype(F32),
                                           az_ref[0, pl.ds(r0, tc), :].astype(F32), g_ref[...], gm)
            dya_ref[pl.ds(r0, tc), :] = dyv.astype(BF16)
            daz_ref[pl.ds(r0, tc), :] = dzg.astype(BF16)
            return dg + _colsum8(dgr)

        dg_ref[...] = lax.fori_loop(0, s // tc, step, jnp.zeros((8, LANES), F32))

    col = lambda c: (0, c)
    return _pcall(
        body, name=name, grid=(nh,),
        in_specs=[pl.BlockSpec((s, LANES), lambda c: (0, c + nh)), pl.BlockSpec((s, LANES), col),
                  pl.BlockSpec((1, s, LANES), lambda c: (7, 0, c)), pl.BlockSpec((1, LANES), lambda c: (0, c + nh))],
        out_specs=[pl.BlockSpec((s, LANES), col), pl.BlockSpec((s, LANES), col), pl.BlockSpec((8, LANES), col)],
        out_shape=[jax.ShapeDtypeStruct((s, sw), BF16), jax.ShapeDtypeStruct((s, sw), BF16),
                   jax.ShapeDtypeStruct((8, sw), F32)],
        compiler_params=_cp(1))(dy, ya, proj, bg)


def _attn_bwd(proj, dya, tl, name):
    _, s, sw = proj.shape
    nhp = sw // LANES
    t = _tile(s, ATT_T)
    nq = s // t
    scale = 1.0 / math.sqrt(HEAD)

    def body(q_ref, k_ref, v_ref, do_ref, tl_ref, dq_ref, dk_ref, dv_ref, dka_ref, dva_ref, dqa_ref):
        i = pl.program_id(1)

        @pl.when(i == 0)
        def _():
            dka_ref[...] = jnp.zeros_like(dka_ref)
            dva_ref[...] = jnp.zeros_like(dva_ref)

        row = lax.broadcasted_iota(jnp.int32, (t, t), 0)
        col = lax.broadcasted_iota(jnp.int32, (t, t), 1)
        tri = (row <= col).astype(BF16)
        below = col < row
        lane = lax.broadcasted_iota(jnp.int32, (t, LANES), 1)
        q = q_ref[0]
        do = do_ref[...]
        qms = [jnp.where((lane // HEAD) == h, q, jnp.zeros_like(q)) for h in range(2)]
        doms = [jnp.where((lane // HEAD) == h, do, jnp.zeros_like(do)) for h in range(2)]
        tots = [tl_ref[0], tl_ref[1]]
        dqa_ref[...] = jnp.zeros_like(dqa_ref)

        def tile(j, carry, diagonal):
            k0 = pl.multiple_of(j * t, t)
            kj = k_ref[0, pl.ds(k0, t), :]
            vj = v_ref[0, pl.ds(k0, t), :]
            out = []
            dk = None
            dv = None
            for h in range(2):
                psum, gsum = carry[h]
                z = _dot_nt(qms[h], kj) * scale
                lm, ls = _softplus_parts(z)
                if diagonal:
                    lm = jnp.where(below, lm, 0.0)
                a = jnp.exp(ls + (tots[h] - psum - _split_dot(lm, tri, 2)))
                if diagonal:
                    a = jnp.where(below, a, 0.0)
                g = a * _dot_nt(doms[h], vj)
                dz = (g - jnp.exp(ls) * (gsum + _split_dot(g, tri, 2))) * scale
                if diagonal:
                    dz = jnp.where(below, dz, 0.0)
                dz = dz.astype(BF16)
                dqa_ref[h] += jnp.dot(dz, kj, preferred_element_type=F32)
                dkh = _dot_tn(dz, qms[h])
                dvh = _dot_tn(a.astype(BF16), doms[h])
                dk = dkh if dk is None else dk + dkh
                dv = dvh if dv is None else dv + dvh
                out.append((psum + jnp.sum(lm, axis=1, keepdims=True), gsum + jnp.sum(g, axis=1, keepdims=True)))
            dka_ref[pl.ds(k0, t), :] += dk
            dva_ref[pl.ds(k0, t), :] += dv
            return tuple(out)

        z1 = jnp.zeros((t, 1), F32)
        carry = lax.fori_loop(0, i, lambda j, c: tile(j, c, False), ((z1, z1), (z1, z1)))
        tile(i, carry, True)
        dq_ref[...] = jnp.where(lane < HEAD, dqa_ref[0], dqa_ref[1]).astype(BF16)

        @pl.when(i == nq - 1)
        def _():
            dk_ref[...] = dka_ref[...].astype(BF16)
            dv_ref[...] = dva_ref[...].astype(BF16)

    return _pcall(
        body, name=name, grid=(nhp, nq),
        in_specs=[pl.BlockSpec((1, t, LANES), lambda hp, i: (4, i, hp)),
                  pl.BlockSpec((1, s, LANES), lambda hp, i: (5, 0, hp)),
                  pl.BlockSpec((1, s, LANES), lambda hp, i: (6, 0, hp)),
                  pl.BlockSpec((t, LANES), lambda hp, i: (i, hp)),
                  pl.BlockSpec((2, t, 1), lambda hp, i: (hp, i, 0))],
        out_specs=[pl.BlockSpec((t, LANES), lambda hp, i: (i, hp)),
                   pl.BlockSpec((s, LANES), lambda hp, i: (0, hp)),
                   pl.BlockSpec((s, LANES), lambda hp, i: (0, hp))],
        out_shape=[jax.ShapeDtypeStruct((s, sw), BF16)] * 3,
        scratch_shapes=[pltpu.VMEM((s, LANES), F32), pltpu.VMEM((s, LANES), F32), pltpu.VMEM((2, t, LANES), F32)],
        compiler_params=_cp(2))(proj, proj, proj, dya, tl)


def _inproj_bwd(dproj, w, x, g, dx1, name):
    ns, s, sw = dproj.shape
    d = x.shape[1]
    tm = _tile(s, 256)
    nm = s // tm

    def body(dp_ref, w_ref, x_ref, g_ref, dx1_ref, dx_ref, dg_ref, acc_ref):
        m = pl.program_id(0)
        k = pl.program_id(1)

        @pl.when((m == 0) & (k == 0))
        def _():
            dg_ref[...] = jnp.zeros_like(dg_ref)

        @pl.when(k == 0)
        def _():
            acc_ref[...] = jnp.zeros_like(acc_ref)

        acc_ref[...] += _dot_nt(dp_ref[0], w_ref[...])

        @pl.when(k == ns - 1)
        def _():
            dxr, dgr = _rms_bwd_rows(acc_ref[...], x_ref[...], g_ref[...])
            dx_ref[...] = dx1_ref[...] + dxr
            dg_ref[...] += _colsum8(dgr)

    row = lambda m, k: (m, 0)
    fix = lambda m, k: (0, 0)
    return _pcall(body, name=name, grid=(nm, ns),
                  in_specs=[pl.BlockSpec((1, tm, sw), lambda m, k: (k, m, 0)), pl.BlockSpec((d, sw), lambda m, k: (0, k)),
                            pl.BlockSpec((tm, d), row), pl.BlockSpec((1, d), fix), pl.BlockSpec((tm, d), row)],
                  out_specs=[pl.BlockSpec((tm, d), row), pl.BlockSpec((8, d), fix)],
                  out_shape=[jax.ShapeDtypeStruct((s, d), F32), jax.ShapeDtypeStruct((8, d), F32)],
                  scratch_shapes=[pltpu.VMEM((tm, d), F32)], compiler_params=_cp(2))(dproj, w, x, g, dx1)


def _adamw(w, g, m, v, name):
    r, c = w.shape
    tr = _tile(r, 256)
    c1 = 1.0 - ADAM_B1 ** ADAM_STEP
    c2 = 1.0 - ADAM_B2 ** ADAM_STEP

    def body(w_ref, g_ref, m_ref, v_ref, d_ref, mo_ref, vo_ref):
        gv = g_ref[...]
        mn = ADAM_B1 * m_ref[...] + (1.0 - ADAM_B1) * gv
        vn = ADAM_B2 * v_ref[...] + (1.0 - ADAM_B2) * (gv * gv)
        d_ref[...] = -ADAM_LR * ((mn / c1) / (jnp.sqrt(vn / c2) + ADAM_EPS) + ADAM_WD * w_ref[...])
        mo_ref[...] = mn
        vo_ref[...] = vn

    t = pl.BlockSpec((tr, c), lambda i: (i, 0))
    return _pcall(body, name=name, grid=(r // tr,), in_specs=[t] * 4, out_specs=[t] * 3,
                  out_shape=[jax.ShapeDtypeStruct((r, c), F32)] * 3, compiler_params=_cp(1))(w, g, m, v)


def _add_layer(stack, other, layer, name):
    _, r, c = stack.shape
    tr = _tile(r, 256)

    def body(l_ref, s_ref, o_ref, out_ref):
        out_ref[...] = s_ref[0] + o_ref[...]

    grid_spec = pltpu.PrefetchScalarGridSpec(
        num_scalar_prefetch=1, grid=(r // tr,),
        in_specs=[pl.BlockSpec((1, tr, c), lambda i, l: (l[0], i, 0)), pl.BlockSpec((tr, c), lambda i, l: (i, 0))],
        out_specs=pl.BlockSpec((tr, c), lambda i, l: (i, 0)))
    return _pcall(body, name=name, grid_spec=grid_spec, out_shape=jax.ShapeDtypeStruct((r, c), F32),
                  compiler_params=_cp(1))(layer.reshape(1).astype(jnp.int32), stack, other)


def _sum_shard(full, parts, chip, axis, name):
    _, r, c = parts.shape
    tr = _tile(r, 256)

    def body(k_ref, f_ref, p_ref, out_ref):
        out_ref[...] = ((f_ref[...] + p_ref[0]) + p_ref[1]) + p_ref[2]

    if axis == 1:
        f_spec = pl.BlockSpec((tr, c), lambda i, k: (i, k[0]))
    else:
        nb = r // tr
        f_spec = pl.BlockSpec((tr, c), lambda i, k: (k[0] * nb + i, 0))
    grid_spec = pltpu.PrefetchScalarGridSpec(
        num_scalar_prefetch=1, grid=(r // tr,),
        in_specs=[f_spec, pl.BlockSpec((3, tr, c), lambda i, k: (0, i, 0))],
        out_specs=pl.BlockSpec((tr, c), lambda i, k: (i, 0)))
    return _pcall(body, name=name, grid_spec=grid_spec, out_shape=jax.ShapeDtypeStruct((r, c), F32),
                  compiler_params=_cp(1))(chip.reshape(1).astype(jnp.int32), full, parts)


def _sum_slots(slots, name):
    n = slots.shape[0]

    def body(s_ref, o_ref):
        acc = s_ref[0]
        for i in range(1, n):
            acc = acc + s_ref[i]
        o_ref[...] = acc

    return _pcall(body, name=name, out_shape=jax.ShapeDtypeStruct(slots.shape[1:], F32))(slots)


def _place():
    return lax.axis_index("x"), lax.axis_index("y"), lax.axis_index("c")


def _shard_view(ref, axis, chip, size):
    if axis == 0:
        return ref.at[pl.ds(chip * size, size), :]
    return ref.at[:, pl.ds(chip * size, size)]


SHARD_AXES = (1, 0, 0, 1)


def _gather_weights(shards):
    n = len(shards)
    sizes = [sh.shape[1 + ax] for sh, ax in zip(shards, SHARD_AXES)]
    out_shapes = []
    for sh, ax in zip(shards, SHARD_AXES):
        full = list(sh.shape)
        full[1 + ax] *= 4
        out_shapes.append(jax.ShapeDtypeStruct(tuple(full), sh.dtype))

    def body(*refs):
        srcs, outs = refs[:n], refs[n:2 * n]
        ssem, rsem, lsem = refs[2 * n:]
        x, y, c = _place()
        me = 2 * x + y
        chips = [(1 - x, y), (x, 1 - y), (1 - x, 1 - y)]

        def piece(a, layer, chip):
            return _shard_view(outs[a].at[layer], SHARD_AXES[a], chip, sizes[a])

        local = [pltpu.make_async_copy(srcs[a].at[l], piece(a, l, me), lsem.at[a, l])
                 for a in range(n) for l in range(DEPTH)]
        for cp in local:
            cp.start()
        sends = []
        for a in range(n):
            for j, (cx, cy) in enumerate(chips):
                cp = pltpu.make_async_remote_copy(
                    src_ref=srcs[a].at[c], dst_ref=piece(a, c, me), send_sem=ssem.at[a, j], recv_sem=rsem.at[a, j],
                    device_id=(cx, cy, c), device_id_type=MESH)
                cp.start()
                sends.append(cp)
        for a in range(n):
            for j, (cx, cy) in enumerate(chips):
                got = piece(a, c, 2 * cx + cy)
                pltpu.make_async_remote_copy(
                    src_ref=got, dst_ref=got, send_sem=ssem.at[a, j], recv_sem=rsem.at[a, j],
                    device_id=(cx, cy, c), device_id_type=MESH).wait_recv()
                cp = pltpu.make_async_remote_copy(
                    src_ref=got, dst_ref=got, send_sem=ssem.at[a, 3 + j], recv_sem=rsem.at[a, 3 + j],
                    device_id=(x, y, 1 - c), device_id_type=MESH)
                cp.start()
                sends.append(cp)
        for a in range(n):
            for j, (cx, cy) in enumerate(chips):
                got = piece(a, 1 - c, 2 * cx + cy)
                pltpu.make_async_remote_copy(
                    src_ref=got, dst_ref=got, send_sem=ssem.at[a, 3 + j], recv_sem=rsem.at[a, 3 + j],
                    device_id=(x, y, 1 - c), device_id_type=MESH).wait_recv()
        for cp in sends:
            cp.wait_send()
        for cp in local:
            cp.wait()

    return _pcall(body, name="gather_weights", in_specs=[ANY] * n, out_specs=[ANY] * n, out_shape=out_shapes,
                  scratch_shapes=[pltpu.SemaphoreType.DMA((n, 6)), pltpu.SemaphoreType.DMA((n, 6)),
                                  pltpu.SemaphoreType.DMA((n, DEPTH))])(*shards)


def _swap_layers(stacks):
    n = len(stacks)

    def body(*refs):
        srcs, outs = refs[:n], refs[n:2 * n]
        ssem, rsem = refs[2 * n:]
        x, y, c = _place()
        cps = [pltpu.make_async_remote_copy(src_ref=srcs[a].at[1 - c], dst_ref=outs[a], send_sem=ssem.at[a],
                                            recv_sem=rsem.at[a], device_id=(x, y, 1 - c), device_id_type=MESH)
               for a in range(n)]
        for cp in cps:
            cp.start()
        for cp in cps:
            cp.wait()

    return _pcall(body, name="swap_layers", in_specs=[ANY] * n, out_specs=[ANY] * n,
                  out_shape=[jax.ShapeDtypeStruct(st.shape[1:], st.dtype) for st in stacks],
                  scratch_shapes=[pltpu.SemaphoreType.DMA((n,)), pltpu.SemaphoreType.DMA((n,))])(*stacks)


def _scatter_shards(fulls):
    n = len(fulls)
    shard_shapes = []
    for f, ax in zip(fulls, SHARD_AXES):
        sh = list(f.shape)
        sh[ax] //= 4
        shard_shapes.append(tuple(sh))

    def body(*refs):
        srcs, outs = refs[:n], refs[n:2 * n]
        ssem, rsem = refs[2 * n:]
        x, y, c = _place()
        chips = [(1 - x, y), (x, 1 - y), (1 - x, 1 - y)]
        cps = []
        for a in range(n):
            for j, (cx, cy) in enumerate(chips):
                src = _shard_view(srcs[a], SHARD_AXES[a], 2 * cx + cy, shard_shapes[a][SHARD_AXES[a]])
                cps.append(pltpu.make_async_remote_copy(
                    src_ref=src, dst_ref=outs[a].at[j], send_sem=ssem.at[a, j], recv_sem=rsem.at[a, j],
                    device_id=(cx, cy, c), device_id_type=MESH))
        for cp in cps:
            cp.start()
        for cp in cps:
            cp.wait()

    return _pcall(body, name="scatter_shards", in_specs=[ANY] * n, out_specs=[ANY] * n,
                  out_shape=[jax.ShapeDtypeStruct((3,) + sh, F32) for sh in shard_shapes],
                  scratch_shapes=[pltpu.SemaphoreType.DMA((n, 3)), pltpu.SemaphoreType.DMA((n, 3))])(*fulls)


def _pair_layers(reduced):
    n = len(reduced)

    def body(*refs):
        srcs, outs = refs[:n], refs[n:2 * n]
        ssem, rsem, lsem = refs[2 * n:]
        x, y, c = _place()
        local = [pltpu.make_async_copy(srcs[a], outs[a].at[c], lsem.at[a]) for a in range(n)]
        for cp in local:
            cp.start()
        cps = [pltpu.make_async_remote_copy(src_ref=srcs[a], dst_ref=outs[a].at[c], send_sem=ssem.at[a],
                                            recv_sem=rsem.at[a], device_id=(x, y, 1 - c), device_id_type=MESH)
               for a in range(n)]
        for cp in cps:
            cp.start()
        for a in range(n):
            got = outs[a].at[1 - c]
            pltpu.make_async_remote_copy(src_ref=got, dst_ref=got, send_sem=ssem.at[a], recv_sem=rsem.at[a],
                                         device_id=(x, y, 1 - c), device_id_type=MESH).wait_recv()
        for cp in cps:
            cp.wait_send()
        for cp in local:
            cp.wait()

    return _pcall(body, name="pair_layers", in_specs=[ANY] * n, out_specs=[ANY] * n,
                  out_shape=[jax.ShapeDtypeStruct((DEPTH,) + r.shape, r.dtype) for r in reduced],
                  scratch_shapes=[pltpu.SemaphoreType.DMA((n,)), pltpu.SemaphoreType.DMA((n,)),
                                  pltpu.SemaphoreType.DMA((n,))])(*reduced)


def _exchange_small(pack, name):
    nd = 8

    def body(p_ref, o_ref, ssem, rsem):
        x, y, c = _place()
        me = 4 * x + 2 * y + c
        o_ref[me] = p_ref[...]
        cps = []
        for j in range(1, nd):
            px, py, pc = x ^ (j >> 2), y ^ ((j >> 1) & 1), c ^ (j & 1)
            cps.append(pltpu.make_async_remote_copy(
                src_ref=p_ref, dst_ref=o_ref.at[me], send_sem=ssem.at[j - 1], recv_sem=rsem.at[j - 1],
                device_id=(px, py, pc), device_id_type=MESH))
        for cp in cps:
            cp.start()
        for j in range(1, nd):
            peer = me ^ j
            got = o_ref.at[peer]
            pltpu.make_async_remote_copy(src_ref=got, dst_ref=got, send_sem=ssem.at[j - 1], recv_sem=rsem.at[j - 1],
                                         device_id=(x, y, c), device_id_type=MESH).wait_recv()
        for cp in cps:
            cp.wait_send()

    vm = pl.BlockSpec(memory_space=pltpu.VMEM)
    return _pcall(body, name=name, in_specs=[vm], out_specs=vm,
                  out_shape=jax.ShapeDtypeStruct((nd,) + pack.shape, pack.dtype),
                  scratch_shapes=[pltpu.SemaphoreType.DMA((nd - 1,)), pltpu.SemaphoreType.DMA((nd - 1,))])(pack)


def _row(v):
    return v.reshape(1, -1)


def _local_step(x, p, tgt, norm_g, conv_w, conv_b, branch_g, ple_norm_g, b_pg, final_g, w_in, w_out, w_pg, w_pe):
    saved = []
    xl = x
    for l in range(DEPTH):
        h = _rms_fwd(xl, _row(norm_g[l]), f"rms_fwd_{l}")
        proj = _inproj(h, w_in[l], f"inproj_{l}")
        ya, tl = _attn_fwd(proj, f"attn_fwd_{l}")
        y = _mix_fwd(proj, ya, conv_w[l], _row(conv_b[l]), _row(branch_g[l]), f"mix_fwd_{l}")
        x1, hn = _outproj(y, w_out[l], xl, _row(ple_norm_g[l]), f"outproj_{l}")
        x2, gate, e = _ple_fwd(hn, w_pg[l], _row(b_pg[l]), p[l], w_pe[l], x1, f"ple_fwd_{l}")
        saved.append((xl, h, proj, ya, tl, y, x1, hn, gate, e))
        xl = x2

    sq, dx, d_final = _loss_head(xl, tgt, _row(final_g), "loss_head")

    g_in = g_out = g_pg = g_pe = None
    small = {k: [None] * DEPTH for k in ("norm_g", "conv_w", "conv_b", "branch_g", "ple_norm_g", "b_pg")}
    for l in reversed(range(DEPTH)):
        xl, h, proj, ya, tl, y, x1, hn, gate, e = saved[l]
        du, de, dx1, dy, db_pg, d_ple = _ple_bwd(dx, gate, e, x1, w_pg[l], _row(ple_norm_g[l]), w_out[l], f"ple_bwd_{l}")
        g_pg = _mm_tn(hn, du, g_pg, l, f"grad_w_pg_{l}")
        g_pe = _mm_tn(p[l], de, g_pe, l, f"grad_w_pe_{l}")
        g_out = _mm_tn(y, dx1, g_out, l, f"grad_w_out_{l}")
        dpc, d_cw, d_cb, d_bg_c = _convmix_bwd(dy, proj, conv_w[l], _row(conv_b[l]), _row(branch_g[l]), f"convmix_bwd_{l}")
        dya, daz, d_bg_a = _attnmix_bwd(dy, ya, proj, _row(branch_g[l]), f"attnmix_bwd_{l}")
        dq, dk, dv = _attn_bwd(proj, dya, tl, f"attn_bwd_{l}")
        dproj = _fill_sections(dpc, (dq, dk, dv, daz), 4, f"fill_sections_{l}")
        g_in = _mm_tn(h, dproj, g_in, l, f"grad_w_in_{l}", b_sections=True)
        dx, d_norm = _inproj_bwd(dproj, w_in[l], xl, _row(norm_g[l]), dx1, f"inproj_bwd_{l}")
        small["norm_g"][l] = jnp.sum(d_norm, axis=0)
        small["conv_w"][l] = jnp.sum(d_cw, axis=1)
        small["conv_b"][l] = jnp.sum(d_cb, axis=0)
        small["branch_g"][l] = jnp.concatenate([jnp.sum(d_bg_c, axis=0), jnp.sum(d_bg_a, axis=0)])
        small["ple_norm_g"][l] = jnp.sum(d_ple, axis=0)
        small["b_pg"][l] = jnp.sum(db_pg, axis=0)
    small = {k: jnp.stack(v) for k, v in small.items()}
    small["final_g"] = jnp.sum(d_final, axis=0)
    return sq[0, 0], dx, (g_in, g_out, g_pg, g_pe), small


SMALL_ORDER = ("norm_g", "conv_w", "conv_b", "branch_g", "ple_norm_g", "b_pg", "final_g")


def _pack(parts, width):
    flat = jnp.concatenate([v.reshape(-1) for v in parts])
    rows = -(-flat.shape[0] // width)
    rows = -(-rows // 8) * 8
    return jnp.pad(flat, (0, rows * width - flat.shape[0])).reshape(rows, width)


def _unpack(packed, like):
    flat = packed.reshape(-1)
    out, off = [], 0
    for v in like:
        out.append(flat[off:off + v.size].reshape(v.shape))
        off += v.size
    return out


def kernel(x, p, norm_g, w_in, conv_w, conv_b, branch_g, w_out, ple_norm_g, w_pg, b_pg, w_pe, final_g, loss_target, m_norm_g, m_w_in, m_conv_w, m_conv_b, m_branch_g, m_w_out, m_ple_norm_g, m_w_pg, m_b_pg, m_w_pe, m_final_g, v_norm_g, v_w_in, v_conv_w, v_conv_b, v_branch_g, v_w_out, v_ple_norm_g, v_w_pg, v_b_pg, v_w_pe, v_final_g):
    ix, iy, ic = _place()
    chip = 2 * ix + iy
    d = x.shape[-1]

    big_w = (w_in, w_out, w_pg, w_pe)
    shards = [_cast_bf16(w, f"cast_{i}") for i, w in enumerate(big_w)]
    full_in, full_out, full_pg, full_pe = _gather_weights(shards)
    cw_shard = conv_w.shape[-1]
    cw_slots = _exchange_small(_pack([conv_w], LANES), "exchange_conv_w")
    conv_full = jnp.concatenate([_unpack(cw_slots[2 * k], [conv_w])[0] for k in range(4)], axis=-1)

    sq, dx, big_g, small_g = _local_step(
        x[0], p[:, 0], loss_target[0], norm_g, conv_full, conv_b, branch_g, ple_norm_g, b_pg, final_g,
        full_in, full_out, full_pg, full_pe)

    from_sibling = _swap_layers(big_g)
    chip_sums = [_add_layer(g, o, ic, f"add_layer_{i}") for i, (g, o) in enumerate(zip(big_g, from_sibling))]
    partials = _scatter_shards(chip_sums)
    reduced = [_sum_shard(f, pr, chip, ax, f"sum_shard_{i}")
               for i, (f, pr, ax) in enumerate(zip(chip_sums, partials, SHARD_AXES))]
    g_big = _pair_layers(reduced)

    parts = [small_g[k] for k in SMALL_ORDER] + [sq.reshape(1)]
    slots = _exchange_small(_pack(parts, d), "exchange_small_grads")
    total = _unpack(_sum_slots(slots, "sum_small"), parts)
    g_small = dict(zip(SMALL_ORDER, total[:-1]))
    loss = 0.5 * total[-1][0] / d
    g_small["conv_w"] = lax.dynamic_slice_in_dim(g_small["conv_w"], chip * cw_shard, cw_shard, axis=2)

    grads = dict(g_small)
    grads.update(w_in=g_big[0], w_out=g_big[1], w_pg=g_big[2], w_pe=g_big[3])
    weights = dict(norm_g=norm_g, w_in=w_in, conv_w=conv_w, conv_b=conv_b, branch_g=branch_g, w_out=w_out,
                   ple_norm_g=ple_norm_g, w_pg=w_pg, b_pg=b_pg, w_pe=w_pe, final_g=final_g)
    ms = dict(norm_g=m_norm_g, w_in=m_w_in, conv_w=m_conv_w, conv_b=m_conv_b, branch_g=m_branch_g, w_out=m_w_out,
              ple_norm_g=m_ple_norm_g, w_pg=m_w_pg, b_pg=m_b_pg, w_pe=m_w_pe, final_g=m_final_g)
    vs = dict(norm_g=v_norm_g, w_in=v_w_in, conv_w=v_conv_w, conv_b=v_conv_b, branch_g=v_branch_g, w_out=v_w_out,
              ple_norm_g=v_ple_norm_g, w_pg=v_w_pg, b_pg=v_b_pg, w_pe=v_w_pe, final_g=v_final_g)
    names = ("norm_g", "w_in", "conv_w", "conv_b", "branch_g", "w_out", "ple_norm_g", "w_pg", "b_pg", "w_pe", "final_g")
    delta, new_m, new_v = {}, {}, {}
    for k in ("w_in", "w_out", "w_pg", "w_pe"):
        shp = weights[k].shape
        two = lambda a: a.reshape(-1, shp[-1])
        dl, mn, vn = _adamw(two(weights[k]), two(grads[k]), two(ms[k]), two(vs[k]), f"adamw_{k}")
        delta[k], new_m[k], new_v[k] = dl.reshape(shp), mn.reshape(shp), vn.reshape(shp)
        grads[k] = grads[k].reshape(shp)
    like = [weights[k] for k in SMALL_ORDER]
    packs = [_pack([src[k] for k in SMALL_ORDER], d) for src in (weights, grads, ms, vs)]
    outs = _adamw(*packs, "adamw_small")
    for res, o in zip((delta, new_m, new_v), outs):
        res.update(dict(zip(SMALL_ORDER, _unpack(o, like))))

    return (loss, dx[None], *[grads[k] for k in names], *[delta[k] for k in names],
            *[new_m[k] for k in names], *[new_v[k] for k in names])
```

```python
import math

import jax
import jax.numpy as jnp
from jax import lax
from jax.experimental import pallas as pl
from jax.experimental.pallas import tpu as pltpu

F32 = jnp.float32
BF16 = jnp.bfloat16
EPS = 1e-6
HEAD = 64
LANES = 128
ATT_T = 256
DEPTH = 2
VMEM_LIMIT = 56 * 1024 * 1024
MESH = pl.DeviceIdType.MESH
ANY = pl.BlockSpec(memory_space=pl.ANY)

ADAM_LR = 0.001
ADAM_B1 = 0.9
ADAM_B2 = 0.999
ADAM_EPS = 1e-08
ADAM_WD = 0.01
ADAM_STEP = 10


def _pcall(body, **kw):
    return pl.pallas_call(body, **kw)


def _cp(n_axes):
    return pltpu.CompilerParams(dimension_semantics=("arbitrary",) * n_axes, vmem_limit_bytes=VMEM_LIMIT)


def _tile(n, pref):
    return pref if n % pref == 0 else n


def _split_dot(a, b, passes):
    out = None
    rem = a
    for _ in range(passes):
        hi = rem.astype(BF16)
        t = jnp.dot(hi, b, preferred_element_type=F32)
        out = t if out is None else out + t
        rem = rem - hi.astype(F32)
    return out


def _group_mat():
    r = lax.broadcasted_iota(jnp.int32, (LANES, LANES), 0) // HEAD
    c = lax.broadcasted_iota(jnp.int32, (LANES, LANES), 1) // HEAD
    return jnp.where(r == c, 1.0 / HEAD, 0.0).astype(BF16)


def _group_mean(v, gm):
    return _split_dot(v, gm, 3)


def _sigmoid(z):
    return 1.0 / (1.0 + jnp.exp(-z))


def _dot_nt(a, b):
    return lax.dot_general(a, b, (((1,), (1,)), ((), ())), preferred_element_type=F32)


def _dot_tn(a, b):
    return lax.dot_general(a, b, (((0,), (0,)), ((), ())), preferred_element_type=F32)


def _cast_into_full(w, chip, axis, name):
    _, r, c = w.shape
    tr = _tile(r, 256)
    nb = r // tr
    full = (DEPTH, 4 * r, c) if axis == 0 else (DEPTH, r, 4 * c)

    def body(k_ref, w_ref, o_ref):
        o_ref[...] = w_ref[...].astype(BF16)

    out_map = (lambda l, i, k: (l, k[0] * nb + i, 0)) if axis == 0 else (lambda l, i, k: (l, i, k[0]))
    grid_spec = pltpu.PrefetchScalarGridSpec(
        num_scalar_prefetch=1, grid=(DEPTH, nb),
        in_specs=[pl.BlockSpec((1, tr, c), lambda l, i, k: (l, i, 0))],
        out_specs=pl.BlockSpec((1, tr, c), out_map))
    return _pcall(body, name=name, grid_spec=grid_spec, out_shape=jax.ShapeDtypeStruct(full, BF16),
                  compiler_params=_cp(2))(chip.reshape(1).astype(jnp.int32), w)


def _rms_fwd(x, g, name):
    s, d = x.shape
    tm = _tile(s, 512)

    def body(x_ref, g_ref, h_ref):
        xv = x_ref[...]
        r = lax.rsqrt(jnp.mean(xv * xv, axis=-1, keepdims=True) + EPS)
        h_ref[...] = (xv * r * g_ref[...]).astype(BF16)

    return _pcall(body, name=name, grid=(s // tm,),
                  in_specs=[pl.BlockSpec((tm, d), lambda m: (m, 0)), pl.BlockSpec((1, d), lambda m: (0, 0))],
                  out_specs=pl.BlockSpec((tm, d), lambda m: (m, 0)),
                  out_shape=jax.ShapeDtypeStruct((s, d), BF16), compiler_params=_cp(1))(x, g)


def _rms_bwd_rows(dh, xv, g):
    r = lax.rsqrt(jnp.mean(xv * xv, axis=-1, keepdims=True) + EPS)
    xn = xv * r
    dxn = dh * g
    dx = r * (dxn - xn * jnp.mean(dxn * xn, axis=-1, keepdims=True))
    return dx, dh * xn


def _colsum8(v):
    tm, d = v.shape
    return jnp.sum(v.reshape(tm // 8, 8, d), axis=0)


def _inproj(h, w, name):
    s, d = h.shape
    n = w.shape[1]
    sw = d // 2
    ns = n // sw
    tm = _tile(s, 512)

    def body(h_ref, w_ref, o_ref):
        o_ref[0] = jnp.dot(h_ref[...], w_ref[...], preferred_element_type=F32).astype(BF16)

    return _pcall(body, name=name, grid=(ns, s // tm),
                  in_specs=[pl.BlockSpec((tm, d), lambda j, m: (m, 0)), pl.BlockSpec((d, sw), lambda j, m: (0, j))],
                  out_specs=pl.BlockSpec((1, tm, sw), lambda j, m: (j, m, 0)),
                  out_shape=jax.ShapeDtypeStruct((ns, s, sw), BF16), compiler_params=_cp(2))(h, w)


def _softplus_parts(z):
    l1 = jnp.log(1.0 + jnp.exp(-jnp.abs(z)))
    return jnp.minimum(-z, 0.0) - l1, jnp.minimum(z, 0.0) - l1


def _attn_fwd(proj, name):
    _, s, sw = proj.shape
    nhp = sw // LANES
    t = _tile(s, ATT_T)
    nq = s // t
    scale = 1.0 / math.sqrt(HEAD)

    def body(q_ref, k_ref, v_ref, o_ref, tl_ref, acc_ref):
        i = pl.program_id(1)
        row = lax.broadcasted_iota(jnp.int32, (t, t), 0)
        col = lax.broadcasted_iota(jnp.int32, (t, t), 1)
        tri = (row > col).astype(BF16)
        below = col < row
        lane = lax.broadcasted_iota(jnp.int32, (t, LANES), 1)
        q = q_ref[0]
        qms = [jnp.where((lane // HEAD) == h, q, jnp.zeros_like(q)) for h in range(2)]
        acc_ref[...] = jnp.zeros_like(acc_ref)

        def tile(j, rsums, diagonal):
            k0 = pl.multiple_of(j * t, t)
            kj = k_ref[0, pl.ds(k0, t), :]
            vj = v_ref[0, pl.ds(k0, t), :]
            out = []
            for h in range(2):
                z = _dot_nt(qms[h], kj) * scale
                lm, ls = _softplus_parts(z)
                if diagonal:
                    lm = jnp.where(below, lm, 0.0)
                a = jnp.exp(ls + (rsums[h] + _split_dot(lm, tri, 2)))
                if diagonal:
                    a = jnp.where(below, a, 0.0)
                acc_ref[h] += jnp.dot(a.astype(BF16), vj, preferred_element_type=F32)
                out.append(rsums[h] + jnp.sum(lm, axis=1, keepdims=True))
            return tuple(out)

        z1 = jnp.zeros((t, 1), F32)
        rsums = tile(i, (z1, z1), True)
        rsums = lax.fori_loop(0, i, lambda jj, c: tile(i - 1 - jj, c, False), rsums)
        tl_ref[0] = rsums[0]
        tl_ref[1] = rsums[1]
        o_ref[...] = jnp.where(lane < HEAD, acc_ref[0], acc_ref[1]).astype(BF16)

    return _pcall(
        body, name=name, grid=(nhp, nq),
        in_specs=[pl.BlockSpec((1, t, LANES), lambda hp, i: (4, i, hp)),
                  pl.BlockSpec((1, s, LANES), lambda hp, i: (5, 0, hp)),
                  pl.BlockSpec((1, s, LANES), lambda hp, i: (6, 0, hp))],
        out_specs=[pl.BlockSpec((t, LANES), lambda hp, i: (i, hp)),
                   pl.BlockSpec((2, t, 1), lambda hp, i: (hp, i, 0))],
        out_shape=[jax.ShapeDtypeStruct((s, sw), BF16), jax.ShapeDtypeStruct((2 * nhp, s, 1), F32)],
        scratch_shapes=[pltpu.VMEM((2, t, LANES), F32)],
        compiler_params=_cp(2))(proj, proj, proj)


def _conv_rows(cc_ref, ch_ref, w_ref, b_ref, r, tc):
    r0 = pl.multiple_of(r * tc, tc)
    u = cc_ref[0, pl.ds(r0, tc), :].astype(F32) * ch_ref[0, pl.ds(r0, tc), :].astype(F32)
    p0 = pl.multiple_of(jnp.maximum(r0 - 16, 0), 16)
    up = cc_ref[0, pl.ds(p0, 16), :].astype(F32) * ch_ref[0, pl.ds(p0, 16), :].astype(F32)
    up = up * (r > 0).astype(F32)
    prev1 = up[15:16, :]
    prev2 = up[14:15, :]
    rid = lax.broadcasted_iota(jnp.int32, u.shape, 0)
    s1 = jnp.where(rid == 0, prev1, pltpu.roll(u, 1, axis=0))
    s2 = jnp.where(rid == 0, prev2, jnp.where(rid == 1, prev1, pltpu.roll(u, 2, axis=0)))
    cv = b_ref[...] + s2 * w_ref[0:1, :] + s1 * w_ref[1:2, :] + u * w_ref[2:3, :]
    return r0, u, s1, s2, cv


def _mix_fwd(proj, ya, conv_w, conv_b, bg, name):
    _, s, sw = proj.shape
    nh = sw // LANES
    tc = _tile(s, 256)

    def body(cb_ref, cc_ref, ch_ref, cz_ref, ya_ref, az_ref, w_ref, b_ref, g_ref, y_ref):
        c = pl.program_id(0)
        gm = _group_mat()

        def finish(r0, yv, zg):
            n = yv * lax.rsqrt(_group_mean(yv * yv, gm) + EPS)
            y_ref[pl.ds(r0, tc), :] = (n * g_ref[...] * (zg * _sigmoid(zg))).astype(BF16)

        @pl.when(c < nh)
        def _():
            def step(r, carry):
                r0, _, _, _, cv = _conv_rows(cc_ref, ch_ref, w_ref, b_ref, r, tc)
                yc = cb_ref[0, pl.ds(r0, tc), :].astype(F32) * cv
                finish(r0, yc, cz_ref[0, pl.ds(r0, tc), :].astype(F32))
                return carry
            lax.fori_loop(0, s // tc, step, 0)

        @pl.when(c >= nh)
        def _():
            def step(r, carry):
                r0 = pl.multiple_of(r * tc, tc)
                finish(r0, ya_ref[pl.ds(r0, tc), :].astype(F32), az_ref[0, pl.ds(r0, tc), :].astype(F32))
                return carry
            lax.fori_loop(0, s // tc, step, 0)

    def sec(k):
        return pl.BlockSpec((1, s, LANES), lambda c: (k, 0, jnp.minimum(c, nh - 1)))

    return _pcall(
        body, name=name, grid=(2 * nh,),
        in_specs=[sec(0), sec(1), sec(2), sec(3),
                  pl.BlockSpec((s, LANES), lambda c: (0, jnp.maximum(c - nh, 0))),
                  pl.BlockSpec((1, s, LANES), lambda c: (7, 0, jnp.maximum(c - nh, 0))),
                  pl.BlockSpec((3, LANES), lambda c: (0, jnp.minimum(c, nh - 1))),
                  pl.BlockSpec((1, LANES), lambda c: (0, jnp.minimum(c, nh - 1))),
                  pl.BlockSpec((1, LANES), lambda c: (0, c))],
        out_specs=pl.BlockSpec((s, LANES), lambda c: (0, c)),
        out_shape=jax.ShapeDtypeStruct((s, 2 * sw), BF16), compiler_params=_cp(1),
    )(proj, proj, proj, proj, ya, proj, conv_w, conv_b, bg)


def _outproj(y, w, x, g, name):
    s, d = x.shape
    tm = _tile(s, 256)

    def body(y_ref, w_ref, x_ref, g_ref, x1_ref, hn_ref):
        x1 = x_ref[...] + jnp.dot(y_ref[...], w_ref[...], preferred_element_type=F32)
        x1_ref[...] = x1
        r = lax.rsqrt(jnp.mean(x1 * x1, axis=-1, keepdims=True) + EPS)
        hn_ref[...] = (x1 * r * g_ref[...]).astype(BF16)

    row = lambda m: (m, 0)
    fix = lambda m: (0, 0)
    return _pcall(body, name=name, grid=(s // tm,),
                  in_specs=[pl.BlockSpec((tm, d), row), pl.BlockSpec((d, d), fix), pl.BlockSpec((tm, d), row),
                            pl.BlockSpec((1, d), fix)],
                  out_specs=[pl.BlockSpec((tm, d), row), pl.BlockSpec((tm, d), row)],
                  out_shape=[jax.ShapeDtypeStruct((s, d), F32), jax.ShapeDtypeStruct((s, d), BF16)],
                  compiler_params=_cp(1))(y, w, x, g)


def _ple_fwd(hn, w_pg, b_pg, p, w_pe, x1, name):
    s, d = x1.shape
    pd = p.shape[1]
    tm = _tile(s, 256)

    def body(hn_ref, wg_ref, b_ref, p_ref, we_ref, x1_ref, x2_ref, gate_ref, e_ref):
        gate = _sigmoid(jnp.dot(hn_ref[...], wg_ref[...], preferred_element_type=F32) + b_ref[...])
        e = jnp.dot(p_ref[...].astype(BF16), we_ref[...], preferred_element_type=F32)
        x2_ref[...] = x1_ref[...] + gate * e
        gate_ref[...] = gate.astype(BF16)
        e_ref[...] = e.astype(BF16)

    row = lambda m: (m, 0)
    fix = lambda m: (0, 0)
    return _pcall(body, name=name, grid=(s // tm,),
                  in_specs=[pl.BlockSpec((tm, d), row), pl.BlockSpec((d, d), fix), pl.BlockSpec((1, d), fix),
                            pl.BlockSpec((tm, pd), row), pl.BlockSpec((pd, d), fix), pl.BlockSpec((tm, d), row)],
                  out_specs=[pl.BlockSpec((tm, d), row)] * 3,
                  out_shape=[jax.ShapeDtypeStruct((s, d), F32), jax.ShapeDtypeStruct((s, d), BF16),
                             jax.ShapeDtypeStruct((s, d), BF16)],
                  compiler_params=_cp(1))(hn, w_pg, b_pg, p, w_pe, x1)


def _loss_head(x, tgt, g, name):
    s, d = x.shape
    tm = _tile(s, 256)

    def body(x_ref, t_ref, g_ref, l_ref, dx_ref, dg_ref):
        m = pl.program_id(0)

        @pl.when(m == 0)
        def _():
            l_ref[...] = jnp.zeros_like(l_ref)
            dg_ref[...] = jnp.zeros_like(dg_ref)

        xv = x_ref[...]
        gv = g_ref[...]
        r = lax.rsqrt(jnp.mean(xv * xv, axis=-1, keepdims=True) + EPS)
        xn = xv * r
        err = xn * gv - t_ref[...]
        l_ref[...] += jnp.sum(err * err)
        dy = err * (1.0 / d)
        dxn = dy * gv
        dx_ref[...] = r * (dxn - xn * jnp.mean(dxn * xn, axis=-1, keepdims=True))
        dg_ref[...] += _colsum8(dy * xn)

    row = lambda m: (m, 0)
    fix = lambda m: (0, 0)
    return _pcall(body, name=name, grid=(s // tm,),
                  in_specs=[pl.BlockSpec((tm, d), row), pl.BlockSpec((tm, d), row), pl.BlockSpec((1, d), fix)],
                  out_specs=[pl.BlockSpec((8, LANES), fix), pl.BlockSpec((tm, d), row), pl.BlockSpec((8, d), fix)],
                  out_shape=[jax.ShapeDtypeStruct((8, LANES), F32), jax.ShapeDtypeStruct((s, d), F32),
                             jax.ShapeDtypeStruct((8, d), F32)],
                  compiler_params=_cp(1))(x, tgt, g)


def _ple_bwd(dx2, gate, e, x1, w_pg, g_ple, w_out, name):
    s, d = dx2.shape
    tm = _tile(s, 256)

    def body(dx2_ref, gate_ref, e_ref, x1_ref, wg_ref, g_ref, wo_ref,
             du_ref, de_ref, dx1_ref, dy_ref, db_ref, dg_ref):
        m = pl.program_id(0)

        @pl.when(m == 0)
        def _():
            db_ref[...] = jnp.zeros_like(db_ref)
            dg_ref[...] = jnp.zeros_like(dg_ref)

        dx2v = dx2_ref[...]
        gate = gate_ref[...].astype(F32)
        du = dx2v * e_ref[...].astype(F32) * gate * (1.0 - gate)
        de_ref[...] = (dx2v * gate).astype(BF16)
        dub = du.astype(BF16)
        du_ref[...] = dub
        db_ref[...] += _colsum8(du)
        dhn = _dot_nt(dub, wg_ref[...])
        dxr, dgr = _rms_bwd_rows(dhn, x1_ref[...], g_ref[...])
        dx1 = dx2v + dxr
        dx1_ref[...] = dx1
        dg_ref[...] += _colsum8(dgr)
        dy_ref[...] = _dot_nt(dx1.astype(BF16), wo_ref[...]).astype(BF16)

    row = lambda m: (m, 0)
    fix = lambda m: (0, 0)
    t = pl.BlockSpec((tm, d), row)
    return _pcall(body, name=name, grid=(s // tm,),
                  in_specs=[t, t, t, t, pl.BlockSpec((d, d), fix), pl.BlockSpec((1, d), fix), pl.BlockSpec((d, d), fix)],
                  out_specs=[t, t, t, t, pl.BlockSpec((8, d), fix), pl.BlockSpec((8, d), fix)],
                  out_shape=[jax.ShapeDtypeStruct((s, d), BF16), jax.ShapeDtypeStruct((s, d), BF16),
                             jax.ShapeDtypeStruct((s, d), F32), jax.ShapeDtypeStruct((s, d), BF16),
                             jax.ShapeDtypeStruct((8, d), F32), jax.ShapeDtypeStruct((8, d), F32)],
                  compiler_params=_cp(1))(dx2, gate, e, x1, w_pg, g_ple, w_out)


def _mm_tn(a, b, stack, layer, name, b_sections=False):
    s, ka = a.shape
    if b_sections:
        ns, _, tn = b.shape
        n = ns * tn
    else:
        n = b.shape[1]
        tn = _tile(n, 1024)
        ns = n // tn
    tk = _tile(s, 512)
    nk = s // tk

    def body(*refs):
        a_ref, b_ref = refs[0], refs[1]
        o_ref, acc_ref = refs[-2], refs[-1]
        k = pl.program_id(1)

        @pl.when(k == 0)
        def _():
            acc_ref[...] = jnp.zeros_like(acc_ref)

        bv = b_ref[0] if b_sections else b_ref[...]
        acc_ref[...] += _dot_tn(a_ref[...].astype(BF16), bv.astype(BF16))

        @pl.when(k == nk - 1)
        def _():
            o_ref[0] = acc_ref[...]

    b_spec = (pl.BlockSpec((1, tk, tn), lambda j, k: (j, k, 0)) if b_sections
              else pl.BlockSpec((tk, tn), lambda j, k: (k, j)))
    in_specs = [pl.BlockSpec((tk, ka), lambda j, k: (k, 0)), b_spec]
    args = [a, b]
    aliases = {}
    if stack is not None:
        in_specs.append(ANY)
        args.append(stack)
        aliases = {2: 0}
    return _pcall(body, name=name, grid=(ns, nk), in_specs=in_specs,
                  out_specs=pl.BlockSpec((1, ka, tn), lambda j, k: (layer, 0, j)),
                  out_shape=jax.ShapeDtypeStruct((DEPTH, ka, n), F32),
                  scratch_shapes=[pltpu.VMEM((ka, tn), F32)], input_output_aliases=aliases,
                  compiler_params=_cp(2))(*args)


def _norm_gate_bwd(dy, yv, zg, g, gm):
    r = lax.rsqrt(_group_mean(yv * yv, gm) + EPS)
    n = yv * r
    sg = _sigmoid(zg)
    sil = zg * sg
    dzg = dy * n * g * (sg * (1.0 + zg * (1.0 - sg)))
    dn = dy * g * sil
    dyv = r * (dn - n * _group_mean(dn * n, gm))
    return dyv, dzg, dy * n * sil


def _convmix_bwd(dy, proj, conv_w, conv_b, bg, name):
    _, s, sw = proj.shape
    nh = sw // LANES
    tc = _tile(s, 256)
    nr = s // tc

    def body(dy_ref, cb_ref, cc_ref, ch_ref, cz_ref, w_ref, b_ref, g_ref,
             dp_ref, dw_ref, db_ref, dg_ref, dcv_ref):
        gm = _group_mat()
        dcv_ref[pl.ds(s, 8), :] = jnp.zeros((8, LANES), F32)

        def pass1(r, carry):
            dw0, dw1, dw2, db, dg = carry
            r0, u, s1, s2, cv = _conv_rows(cc_ref, ch_ref, w_ref, b_ref, r, tc)
            cb = cb_ref[0, pl.ds(r0, tc), :].astype(F32)
            dyc, dcz, dgr = _norm_gate_bwd(dy_ref[pl.ds(r0, tc), :].astype(F32), cb * cv,
                                           cz_ref[0, pl.ds(r0, tc), :].astype(F32), g_ref[...], gm)
            dp_ref[0, pl.ds(r0, tc), :] = (dyc * cv).astype(BF16)
            dp_ref[3, pl.ds(r0, tc), :] = dcz.astype(BF16)
            dcv = dyc * cb
            dcv_ref[pl.ds(r0, tc), :] = dcv
            return (dw0 + _colsum8(dcv * s2), dw1 + _colsum8(dcv * s1), dw2 + _colsum8(dcv * u),
                    db + _colsum8(dcv), dg + _colsum8(dgr))

        z8 = jnp.zeros((8, LANES), F32)
        dw0, dw1, dw2, db, dg = lax.fori_loop(0, nr, pass1, (z8, z8, z8, z8, z8))
        dw_ref[0] = dw0
        dw_ref[1] = dw1
        dw_ref[2] = dw2
        db_ref[...] = db
        dg_ref[...] = dg

        def pass2(r, carry):
            r0 = pl.multiple_of(r * tc, tc)
            dcv = dcv_ref[pl.ds(r0, tc), :]
            nxt = dcv_ref[pl.ds(pl.multiple_of(r0 + tc, 8), 8), :]
            rid = lax.broadcasted_iota(jnp.int32, dcv.shape, 0)
            n1 = jnp.where(rid == tc - 1, nxt[0:1, :], pltpu.roll(dcv, tc - 1, axis=0))
            n2 = jnp.where(rid == tc - 1, nxt[1:2, :],
                           jnp.where(rid == tc - 2, nxt[0:1, :], pltpu.roll(dcv, tc - 2, axis=0)))
            du = dcv * w_ref[2:3, :] + n1 * w_ref[1:2, :] + n2 * w_ref[0:1, :]
            dp_ref[1, pl.ds(r0, tc), :] = (du * ch_ref[0, pl.ds(r0, tc), :].astype(F32)).astype(BF16)
            dp_ref[2, pl.ds(r0, tc), :] = (du * cc_ref[0, pl.ds(r0, tc), :].astype(F32)).astype(BF16)
            return carry

        lax.fori_loop(0, nr, pass2, 0)

    def sec(k):
        return pl.BlockSpec((1, s, LANES), lambda c: (k, 0, c))

    col = lambda c: (0, c)
    return _pcall(
        body, name=name, grid=(nh,),
        in_specs=[pl.BlockSpec((s, LANES), col), sec(0), sec(1), sec(2), sec(3),
                  pl.BlockSpec((3, LANES), col), pl.BlockSpec((1, LANES), col), pl.BlockSpec((1, LANES), col)],
        out_specs=[pl.BlockSpec((4, s, LANES), lambda c: (0, 0, c)), pl.BlockSpec((3, 8, LANES), lambda c: (0, 0, c)),
                   pl.BlockSpec((8, LANES), col), pl.BlockSpec((8, LANES), col)],
        out_shape=[jax.ShapeDtypeStruct((8, s, sw), BF16), jax.ShapeDtypeStruct((3, 8, sw), F32),
                   jax.ShapeDtypeStruct((8, sw), F32), jax.ShapeDtypeStruct((8, sw), F32)],
        scratch_shapes=[pltpu.VMEM((s + 8, LANES), F32)], compiler_params=_cp(1),
    )(dy, proj, proj, proj, proj, conv_w, conv_b, bg)


def _attn_bwd(proj, dy, ya, tl, bg, buf, name):
    _, s, sw = proj.shape
    nhp = sw // LANES
    t = _tile(s, ATT_T)
    nq = s // t
    scale = 1.0 / math.sqrt(HEAD)

    def body(q_ref, k_ref, v_ref, az_ref, dy_ref, ya_ref, tl_ref, g_ref, buf_ref, out_ref, dg_ref,
             dka_ref, dva_ref, dqa_ref):
        step = pl.program_id(1)
        i = nq - 1 - step

        @pl.when(step == 0)
        def _():
            dka_ref[...] = jnp.zeros_like(dka_ref)
            dva_ref[...] = jnp.zeros_like(dva_ref)
            dg_ref[...] = jnp.zeros_like(dg_ref)

        dyv, dzg, dgr = _norm_gate_bwd(dy_ref[...].astype(F32), ya_ref[...].astype(F32), az_ref[0].astype(F32),
                                       g_ref[...], _group_mat())
        out_ref[3] = dzg.astype(BF16)
        dg_ref[...] += _colsum8(dgr)

        row = lax.broadcasted_iota(jnp.int32, (t, t), 0)
        col = lax.broadcasted_iota(jnp.int32, (t, t), 1)
        tri = (row <= col).astype(BF16)
        below = col < row
        lane = lax.broadcasted_iota(jnp.int32, (t, LANES), 1)
        q = q_ref[0]
        do = dyv.astype(BF16)
        qms = [jnp.where((lane // HEAD) == h, q, jnp.zeros_like(q)) for h in range(2)]
        doms = [jnp.where((lane // HEAD) == h, do, jnp.zeros_like(do)) for h in range(2)]
        tots = [tl_ref[0], tl_ref[1]]
        dqa_ref[...] = jnp.zeros_like(dqa_ref)

        def tile(j, carry, diagonal):
            k0 = pl.multiple_of(j * t, t)
            kj = k_ref[0, pl.ds(k0, t), :]
            vj = v_ref[0, pl.ds(k0, t), :]
            out = []
            dk = None
            dv = None
            for h in range(2):
                psum, gsum = carry[h]
                z = _dot_nt(qms[h], kj) * scale
                lm, ls = _softplus_parts(z)
                if diagonal:
                    lm = jnp.where(below, lm, 0.0)
                a = jnp.exp(ls + (tots[h] - psum - _split_dot(lm, tri, 2)))
                if diagonal:
                    a = jnp.where(below, a, 0.0)
                g = a * _dot_nt(doms[h], vj)
                dz = (g - jnp.exp(ls) * (gsum + _split_dot(g, tri, 2))) * scale
                if diagonal:
                    dz = jnp.where(below, dz, 0.0)
                dz = dz.astype(BF16)
                dqa_ref[h] += jnp.dot(dz, kj, preferred_element_type=F32)
                dkh = _dot_tn(dz, qms[h])
                dvh = _dot_tn(a.astype(BF16), doms[h])
                dk = dkh if dk is None else dk + dkh
                dv = dvh if dv is None else dv + dvh
                out.append((psum + jnp.sum(lm, axis=1, keepdims=True), gsum + jnp.sum(g, axis=1, keepdims=True)))
            dka_ref[pl.ds(k0, t), :] += dk
            dva_ref[pl.ds(k0, t), :] += dv
            return tuple(out)

        z1 = jnp.zeros((t, 1), F32)
        carry = lax.fori_loop(0, i, lambda j, c: tile(j, c, False), ((z1, z1), (z1, z1)))
        tile(i, carry, True)
        out_ref[0] = jnp.where(lane < HEAD, dqa_ref[0], dqa_ref[1]).astype(BF16)
        own = pl.multiple_of(i * t, t)
        out_ref[1] = dka_ref[pl.ds(own, t), :].astype(BF16)
        out_ref[2] = dva_ref[pl.ds(own, t), :].astype(BF16)

    def rows(sec):
        return pl.BlockSpec((1, t, LANES), lambda hp, st: (sec, nq - 1 - st, hp))

    def whole(sec):
        return pl.BlockSpec((1, s, LANES), lambda hp, st: (sec, 0, hp))

    return _pcall(
        body, name=name, grid=(nhp, nq),
        in_specs=[rows(4), whole(5), whole(6), rows(7),
                  pl.BlockSpec((t, LANES), lambda hp, st: (nq - 1 - st, hp + nhp)),
                  pl.BlockSpec((t, LANES), lambda hp, st: (nq - 1 - st, hp)),
                  pl.BlockSpec((2, t, 1), lambda hp, st: (hp, nq - 1 - st, 0)),
                  pl.BlockSpec((1, LANES), lambda hp, st: (0, hp + nhp)), ANY],
        out_specs=[pl.BlockSpec((4, t, LANES), lambda hp, st: (1, nq - 1 - st, hp)),
                   pl.BlockSpec((8, LANES), lambda hp, st: (0, hp))],
        out_shape=[jax.ShapeDtypeStruct(buf.shape, buf.dtype), jax.ShapeDtypeStruct((8, sw), F32)],
        input_output_aliases={8: 0},
        scratch_shapes=[pltpu.VMEM((s, LANES), F32), pltpu.VMEM((s, LANES), F32), pltpu.VMEM((2, t, LANES), F32)],
        compiler_params=_cp(2))(proj, proj, proj, proj, dy, ya, tl, bg, buf)


def _inproj_bwd(dproj, w, x, g, dx1, name):
    ns, s, sw = dproj.shape
    d = x.shape[1]
    tm = _tile(s, 256)

    def body(dp_ref, w_ref, x_ref, g_ref, dx1_ref, dx_ref, dg_ref):
        @pl.when(pl.program_id(0) == 0)
        def _():
            dg_ref[...] = jnp.zeros_like(dg_ref)

        dh = _dot_nt(dp_ref[0], w_ref[:, 0:sw])
        for k in range(1, ns):
            dh = dh + _dot_nt(dp_ref[k], w_ref[:, k * sw:(k + 1) * sw])
        dxr, dgr = _rms_bwd_rows(dh, x_ref[...], g_ref[...])
        dx_ref[...] = dx1_ref[...] + dxr
        dg_ref[...] += _colsum8(dgr)

    row = lambda m: (m, 0)
    fix = lambda m: (0, 0)
    return _pcall(body, name=name, grid=(s // tm,),
                  in_specs=[pl.BlockSpec((ns, tm, sw), lambda m: (0, m, 0)), pl.BlockSpec((d, ns * sw), fix),
                            pl.BlockSpec((tm, d), row), pl.BlockSpec((1, d), fix), pl.BlockSpec((tm, d), row)],
                  out_specs=[pl.BlockSpec((tm, d), row), pl.BlockSpec((8, d), fix)],
                  out_shape=[jax.ShapeDtypeStruct((s, d), F32), jax.ShapeDtypeStruct((8, d), F32)],
                  compiler_params=_cp(1))(dproj, w, x, g, dx1)


def _adamw(w, g, m, v, name):
    r, c = w.shape
    tr = _tile(r, 256)
    c1 = 1.0 - ADAM_B1 ** ADAM_STEP
    c2 = 1.0 - ADAM_B2 ** ADAM_STEP

    def body(w_ref, g_ref, m_ref, v_ref, d_ref, mo_ref, vo_ref):
        gv = g_ref[...]
        mn = ADAM_B1 * m_ref[...] + (1.0 - ADAM_B1) * gv
        vn = ADAM_B2 * v_ref[...] + (1.0 - ADAM_B2) * (gv * gv)
        d_ref[...] = -ADAM_LR * ((mn / c1) / (jnp.sqrt(vn / c2) + ADAM_EPS) + ADAM_WD * w_ref[...])
        mo_ref[...] = mn
        vo_ref[...] = vn

    t = pl.BlockSpec((tr, c), lambda i: (i, 0))
    return _pcall(body, name=name, grid=(r // tr,), in_specs=[t] * 4, out_specs=[t] * 3,
                  out_shape=[jax.ShapeDtypeStruct((r, c), F32)] * 3, compiler_params=_cp(1))(w, g, m, v)


def _add_layer(stack, other, layer, name):
    _, r, c = stack.shape
    tr = _tile(r, 256)

    def body(l_ref, s_ref, o_ref, out_ref):
        out_ref[...] = s_ref[0] + o_ref[...]

    grid_spec = pltpu.PrefetchScalarGridSpec(
        num_scalar_prefetch=1, grid=(r // tr,),
        in_specs=[pl.BlockSpec((1, tr, c), lambda i, l: (l[0], i, 0)), pl.BlockSpec((tr, c), lambda i, l: (i, 0))],
        out_specs=pl.BlockSpec((tr, c), lambda i, l: (i, 0)))
    return _pcall(body, name=name, grid_spec=grid_spec, out_shape=jax.ShapeDtypeStruct((r, c), F32),
                  compiler_params=_cp(1))(layer.reshape(1).astype(jnp.int32), stack, other)


def _sum_shard(full, parts, chip, layer, axis, name):
    _, r, c = parts.shape
    tr = _tile(r, 256)

    def body(k_ref, f_ref, p_ref, out_ref):
        out_ref[0] = ((f_ref[...] + p_ref[0]) + p_ref[1]) + p_ref[2]

    if axis == 1:
        f_spec = pl.BlockSpec((tr, c), lambda i, k: (i, k[0]))
    else:
        nb = r // tr
        f_spec = pl.BlockSpec((tr, c), lambda i, k: (k[0] * nb + i, 0))
    grid_spec = pltpu.PrefetchScalarGridSpec(
        num_scalar_prefetch=1, grid=(r // tr,),
        in_specs=[f_spec, pl.BlockSpec((3, tr, c), lambda i, k: (0, i, 0))],
        out_specs=pl.BlockSpec((1, tr, c), lambda i, k: (k[1], i, 0)))
    return _pcall(body, name=name, grid_spec=grid_spec, out_shape=jax.ShapeDtypeStruct((DEPTH, r, c), F32),
                  compiler_params=_cp(1))(jnp.stack([chip, layer]).astype(jnp.int32), full, parts)


def _sum_slots(slots, name):
    n = slots.shape[0]

    def body(s_ref, o_ref):
        acc = s_ref[0]
        for i in range(1, n):
            acc = acc + s_ref[i]
        o_ref[...] = acc

    return _pcall(body, name=name, out_shape=jax.ShapeDtypeStruct(slots.shape[1:], F32))(slots)


def _place():
    return lax.axis_index("x"), lax.axis_index("y"), lax.axis_index("c")


def _shard_view(ref, axis, chip, size):
    if axis == 0:
        return ref.at[pl.ds(chip * size, size), :]
    return ref.at[:, pl.ds(chip * size, size)]


SHARD_AXES = (1, 0, 0, 1)


def _gather_weights(fulls):
    n = len(fulls)
    sizes = [f.shape[1 + ax] // 4 for f, ax in zip(fulls, SHARD_AXES)]

    def body(*refs):
        ins, outs = refs[:n], refs[n:2 * n]
        ssem, rsem = refs[2 * n:]
        x, y, c = _place()
        me = 2 * x + y
        chips = [(1 - x, y), (x, 1 - y), (1 - x, 1 - y)]

        def piece(a, layer, chip, of=outs):
            return _shard_view(of[a].at[layer], SHARD_AXES[a], chip, sizes[a])

        sends = []
        for a in range(n):
            for j, (cx, cy) in enumerate(chips):
                cp = pltpu.make_async_remote_copy(
                    src_ref=piece(a, c, me, ins), dst_ref=piece(a, c, me), send_sem=ssem.at[a, j],
                    recv_sem=rsem.at[a, j], device_id=(cx, cy, c), device_id_type=MESH)
                cp.start()
                sends.append(cp)
        for a in range(n):
            for j, (cx, cy) in enumerate(chips):
                got = piece(a, c, 2 * cx + cy)
                pltpu.make_async_remote_copy(
                    src_ref=got, dst_ref=got, send_sem=ssem.at[a, j], recv_sem=rsem.at[a, j],
                    device_id=(cx, cy, c), device_id_type=MESH).wait_recv()
                cp = pltpu.make_async_remote_copy(
                    src_ref=got, dst_ref=got, send_sem=ssem.at[a, 3 + j], recv_sem=rsem.at[a, 3 + j],
                    device_id=(x, y, 1 - c), device_id_type=MESH)
                cp.start()
                sends.append(cp)
        for a in range(n):
            for j, (cx, cy) in enumerate(chips):
                got = piece(a, 1 - c, 2 * cx + cy)
                pltpu.make_async_remote_copy(
                    src_ref=got, dst_ref=got, send_sem=ssem.at[a, 3 + j], recv_sem=rsem.at[a, 3 + j],
                    device_id=(x, y, 1 - c), device_id_type=MESH).wait_recv()
        for cp in sends:
            cp.wait_send()

    return _pcall(body, name="gather_weights", in_specs=[ANY] * n, out_specs=[ANY] * n,
                  out_shape=[jax.ShapeDtypeStruct(f.shape, f.dtype) for f in fulls],
                  input_output_aliases={a: a for a in range(n)},
                  scratch_shapes=[pltpu.SemaphoreType.DMA((n, 6)), pltpu.SemaphoreType.DMA((n, 6))])(*fulls)


def _swap_layers(stacks):
    n = len(stacks)

    def body(*refs):
        srcs, outs = refs[:n], refs[n:2 * n]
        ssem, rsem = refs[2 * n:]
        x, y, c = _place()
        cps = [pltpu.make_async_remote_copy(src_ref=srcs[a].at[1 - c], dst_ref=outs[a], send_sem=ssem.at[a],
                                            recv_sem=rsem.at[a], device_id=(x, y, 1 - c), device_id_type=MESH)
               for a in range(n)]
        for cp in cps:
            cp.start()
        for cp in cps:
            cp.wait()

    return _pcall(body, name="swap_layers", in_specs=[ANY] * n, out_specs=[ANY] * n,
                  out_shape=[jax.ShapeDtypeStruct(st.shape[1:], st.dtype) for st in stacks],
                  scratch_shapes=[pltpu.SemaphoreType.DMA((n,)), pltpu.SemaphoreType.DMA((n,))])(*stacks)


def _scatter_shards(fulls):
    n = len(fulls)
    shard_shapes = []
    for f, ax in zip(fulls, SHARD_AXES):
        sh = list(f.shape)
        sh[ax] //= 4
        shard_shapes.append(tuple(sh))

    def body(*refs):
        srcs, outs = refs[:n], refs[n:2 * n]
        ssem, rsem = refs[2 * n:]
        x, y, c = _place()
        chips = [(1 - x, y), (x, 1 - y), (1 - x, 1 - y)]
        cps = []
        for a in range(n):
            for j, (cx, cy) in enumerate(chips):
                src = _shard_view(srcs[a], SHARD_AXES[a], 2 * cx + cy, shard_shapes[a][SHARD_AXES[a]])
                cps.append(pltpu.make_async_remote_copy(
                    src_ref=src, dst_ref=outs[a].at[j], send_sem=ssem.at[a, j], recv_sem=rsem.at[a, j],
                    device_id=(cx, cy, c), device_id_type=MESH))
        for cp in cps:
            cp.start()
        for cp in cps:
            cp.wait()

    return _pcall(body, name="scatter_shards", in_specs=[ANY] * n, out_specs=[ANY] * n,
                  out_shape=[jax.ShapeDtypeStruct((3,) + sh, F32) for sh in shard_shapes],
                  scratch_shapes=[pltpu.SemaphoreType.DMA((n, 3)), pltpu.SemaphoreType.DMA((n, 3))])(*fulls)


def _pair_layers(stacks):
    n = len(stacks)

    def body(*refs):
        ins, outs = refs[:n], refs[n:2 * n]
        ssem, rsem = refs[2 * n:]
        x, y, c = _place()
        cps = [pltpu.make_async_remote_copy(src_ref=ins[a].at[c], dst_ref=outs[a].at[c], send_sem=ssem.at[a],
                                            recv_sem=rsem.at[a], device_id=(x, y, 1 - c), device_id_type=MESH)
               for a in range(n)]
        for cp in cps:
            cp.start()
        for a in range(n):
            got = outs[a].at[1 - c]
            pltpu.make_async_remote_copy(src_ref=got, dst_ref=got, send_sem=ssem.at[a], recv_sem=rsem.at[a],
                                         device_id=(x, y, 1 - c), device_id_type=MESH).wait_recv()
        for cp in cps:
            cp.wait_send()

    return _pcall(body, name="pair_layers", in_specs=[ANY] * n, out_specs=[ANY] * n,
                  out_shape=[jax.ShapeDtypeStruct(st.shape, st.dtype) for st in stacks],
                  input_output_aliases={a: a for a in range(n)},
                  scratch_shapes=[pltpu.SemaphoreType.DMA((n,)), pltpu.SemaphoreType.DMA((n,))])(*stacks)


def _exchange_small(pack, name):
    nd = 8

    def body(p_ref, o_ref, ssem, rsem):
        x, y, c = _place()
        me = 4 * x + 2 * y + c
        o_ref[me] = p_ref[...]
        cps = []
        for j in range(1, nd):
            px, py, pc = x ^ (j >> 2), y ^ ((j >> 1) & 1), c ^ (j & 1)
            cps.append(pltpu.make_async_remote_copy(
                src_ref=p_ref, dst_ref=o_ref.at[me], send_sem=ssem.at[j - 1], recv_sem=rsem.at[j - 1],
                device_id=(px, py, pc), device_id_type=MESH))
        for cp in cps:
            cp.start()
        for j in range(1, nd):
            peer = me ^ j
            got = o_ref.at[peer]
            pltpu.make_async_remote_copy(src_ref=got, dst_ref=got, send_sem=ssem.at[j - 1], recv_sem=rsem.at[j - 1],
                                         device_id=(x, y, c), device_id_type=MESH).wait_recv()
        for cp in cps:
            cp.wait_send()

    vm = pl.BlockSpec(memory_space=pltpu.VMEM)
    return _pcall(body, name=name, in_specs=[vm], out_specs=vm,
                  out_shape=jax.ShapeDtypeStruct((nd,) + pack.shape, pack.dtype),
                  scratch_shapes=[pltpu.SemaphoreType.DMA((nd - 1,)), pltpu.SemaphoreType.DMA((nd - 1,))])(pack)


def _row(v):
    return v.reshape(1, -1)


def _local_step(x, p, tgt, norm_g, conv_w, conv_b, branch_g, ple_norm_g, b_pg, final_g, w_in, w_out, w_pg, w_pe):
    saved = []
    xl = x
    for l in range(DEPTH):
        h = _rms_fwd(xl, _row(norm_g[l]), f"rms_fwd_{l}")
        proj = _inproj(h, w_in[l], f"inproj_{l}")
        ya, tl = _attn_fwd(proj, f"attn_fwd_{l}")
        y = _mix_fwd(proj, ya, conv_w[l], _row(conv_b[l]), _row(branch_g[l]), f"mix_fwd_{l}")
        x1, hn = _outproj(y, w_out[l], xl, _row(ple_norm_g[l]), f"outproj_{l}")
        x2, gate, e = _ple_fwd(hn, w_pg[l], _row(b_pg[l]), p[l], w_pe[l], x1, f"ple_fwd_{l}")
        saved.append((xl, h, proj, ya, tl, y, x1, hn, gate, e))
        xl = x2

    sq, dx, d_final = _loss_head(xl, tgt, _row(final_g), "loss_head")

    g_in = g_out = g_pg = g_pe = None
    small = {k: [None] * DEPTH for k in ("norm_g", "conv_w", "conv_b", "branch_g", "ple_norm_g", "b_pg")}
    for l in reversed(range(DEPTH)):
        xl, h, proj, ya, tl, y, x1, hn, gate, e = saved[l]
        du, de, dx1, dy, db_pg, d_ple = _ple_bwd(dx, gate, e, x1, w_pg[l], _row(ple_norm_g[l]), w_out[l], f"ple_bwd_{l}")
        g_pg = _mm_tn(hn, du, g_pg, l, f"grad_w_pg_{l}")
        g_pe = _mm_tn(p[l], de, g_pe, l, f"grad_w_pe_{l}")
        g_out = _mm_tn(y, dx1, g_out, l, f"grad_w_out_{l}")
        dpc, d_cw, d_cb, d_bg_c = _convmix_bwd(dy, proj, conv_w[l], _row(conv_b[l]), _row(branch_g[l]), f"convmix_bwd_{l}")
        dproj, d_bg_a = _attn_bwd(proj, dy, ya, tl, _row(branch_g[l]), dpc, f"attn_bwd_{l}")
        g_in = _mm_tn(h, dproj, g_in, l, f"grad_w_in_{l}", b_sections=True)
        dx, d_norm = _inproj_bwd(dproj, w_in[l], xl, _row(norm_g[l]), dx1, f"inproj_bwd_{l}")
        small["norm_g"][l] = jnp.sum(d_norm, axis=0)
        small["conv_w"][l] = jnp.sum(d_cw, axis=1)
        small["conv_b"][l] = jnp.sum(d_cb, axis=0)
        small["branch_g"][l] = jnp.concatenate([jnp.sum(d_bg_c, axis=0), jnp.sum(d_bg_a, axis=0)])
        small["ple_norm_g"][l] = jnp.sum(d_ple, axis=0)
        small["b_pg"][l] = jnp.sum(db_pg, axis=0)
    small = {k: jnp.stack(v) for k, v in small.items()}
    small["final_g"] = jnp.sum(d_final, axis=0)
    return sq[0, 0], dx, (g_in, g_out, g_pg, g_pe), small


SMALL_ORDER = ("norm_g", "conv_w", "conv_b", "branch_g", "ple_norm_g", "b_pg", "final_g")


def _pack(parts, width):
    flat = jnp.concatenate([v.reshape(-1) for v in parts])
    rows = -(-flat.shape[0] // width)
    rows = -(-rows // 8) * 8
    return jnp.pad(flat, (0, rows * width - flat.shape[0])).reshape(rows, width)


def _unpack(packed, like):
    flat = packed.reshape(-1)
    out, off = [], 0
    for v in like:
        out.append(flat[off:off + v.size].reshape(v.shape))
        off += v.size
    return out


def kernel(x, p, norm_g, w_in, conv_w, conv_b, branch_g, w_out, ple_norm_g, w_pg, b_pg, w_pe, final_g, loss_target, m_norm_g, m_w_in, m_conv_w, m_conv_b, m_branch_g, m_w_out, m_ple_norm_g, m_w_pg, m_b_pg, m_w_pe, m_final_g, v_norm_g, v_w_in, v_conv_w, v_conv_b, v_branch_g, v_w_out, v_ple_norm_g, v_w_pg, v_b_pg, v_w_pe, v_final_g):
    ix, iy, ic = _place()
    chip = 2 * ix + iy
    d = x.shape[-1]

    big_w = (w_in, w_out, w_pg, w_pe)
    own = [_cast_into_full(w, chip, ax, f"cast_{i}") for i, (w, ax) in enumerate(zip(big_w, SHARD_AXES))]
    full_in, full_out, full_pg, full_pe = _gather_weights(own)
    cw_shard = conv_w.shape[-1]
    cw_slots = _exchange_small(_pack([conv_w], LANES), "exchange_conv_w")
    conv_full = jnp.concatenate([_unpack(cw_slots[2 * k], [conv_w])[0] for k in range(4)], axis=-1)

    sq, dx, big_g, small_g = _local_step(
        x[0], p[:, 0], loss_target[0], norm_g, conv_full, conv_b, branch_g, ple_norm_g, b_pg, final_g,
        full_in, full_out, full_pg, full_pe)

    from_sibling = _swap_layers(big_g)
    chip_sums = [_add_layer(g, o, ic, f"add_layer_{i}") for i, (g, o) in enumerate(zip(big_g, from_sibling))]
    partials = _scatter_shards(chip_sums)
    reduced = [_sum_shard(f, pr, chip, ic, ax, f"sum_shard_{i}")
               for i, (f, pr, ax) in enumerate(zip(chip_sums, partials, SHARD_AXES))]
    g_big = _pair_layers(reduced)

    parts = [small_g[k] for k in SMALL_ORDER] + [sq.reshape(1)]
    slots = _exchange_small(_pack(parts, d), "exchange_small_grads")
    total = _unpack(_sum_slots(slots, "sum_small"), parts)
    g_small = dict(zip(SMALL_ORDER, total[:-1]))
    loss = 0.5 * total[-1][0] / d
    g_small["conv_w"] = lax.dynamic_slice_in_dim(g_small["conv_w"], chip * cw_shard, cw_shard, axis=2)

    grads = dict(g_small)
    grads.update(w_in=g_big[0], w_out=g_big[1], w_pg=g_big[2], w_pe=g_big[3])
    weights = dict(norm_g=norm_g, w_in=w_in, conv_w=conv_w, conv_b=conv_b, branch_g=branch_g, w_out=w_out,
                   ple_norm_g=ple_norm_g, w_pg=w_pg, b_pg=b_pg, w_pe=w_pe, final_g=final_g)
    ms = dict(norm_g=m_norm_g, w_in=m_w_in, conv_w=m_conv_w, conv_b=m_conv_b, branch_g=m_branch_g, w_out=m_w_out,
              ple_norm_g=m_ple_norm_g, w_pg=m_w_pg, b_pg=m_b_pg, w_pe=m_w_pe, final_g=m_final_g)
    vs = dict(norm_g=v_norm_g, w_in=v_w_in, conv_w=v_conv_w, conv_b=v_conv_b, branch_g=v_branch_g, w_out=v_w_out,
              ple_norm_g=v_ple_norm_g, w_pg=v_w_pg, b_pg=v_b_pg, w_pe=v_w_pe, final_g=v_final_g)
    names = ("norm_g", "w_in", "conv_w", "conv_b", "branch_g", "w_out", "ple_norm_g", "w_pg", "b_pg", "w_pe", "final_g")
    delta, new_m, new_v = {}, {}, {}
    for k in ("w_in", "w_out", "w_pg", "w_pe"):
        shp = weights[k].shape
        two = lambda a: a.reshape(-1, shp[-1])
        dl, mn, vn = _adamw(two(weights[k]), two(grads[k]), two(ms[k]), two(vs[k]), f"adamw_{k}")
        delta[k], new_m[k], new_v[k] = dl.reshape(shp), mn.reshape(shp), vn.reshape(shp)
        grads[k] = grads[k].reshape(shp)
    like = [weights[k] for k in SMALL_ORDER]
    packs = [_pack([src[k] for k in SMALL_ORDER], d) for src in (weights, grads, ms, vs)]
    outs = _adamw(*packs, "adamw_small")
    for res, o in zip((delta, new_m, new_v), outs):
        res.update(dict(zip(SMALL_ORDER, _unpack(o, like))))

    return (loss, dx[None], *[grads[k] for k in names], *[delta[k] for k in names],
            *[new_m[k] for k in names], *[new_v[k] for k in names])
```

```python
import math

import jax
import jax.numpy as jnp
from jax import lax
from jax.experimental import pallas as pl
from jax.experimental.pallas import tpu as pltpu

F32 = jnp.float32
BF16 = jnp.bfloat16
EPS = 1e-6
HEAD = 64
LANES = 128
ATT_T = 256
DEPTH = 2
VMEM_LIMIT = 56 * 1024 * 1024
MESH = pl.DeviceIdType.MESH
ANY = pl.BlockSpec(memory_space=pl.ANY)

ADAM_LR = 0.001
ADAM_B1 = 0.9
ADAM_B2 = 0.999
ADAM_EPS = 1e-08
ADAM_WD = 0.01
ADAM_STEP = 10


def _pcall(body, **kw):
    return pl.pallas_call(body, **kw)


def _cp(n_axes):
    return pltpu.CompilerParams(dimension_semantics=("arbitrary",) * n_axes, vmem_limit_bytes=VMEM_LIMIT)


def _tile(n, pref):
    return pref if n % pref == 0 else n


def _split_dot(a, b, passes):
    out = None
    rem = a
    for _ in range(passes):
        hi = rem.astype(BF16)
        t = jnp.dot(hi, b, preferred_element_type=F32)
        out = t if out is None else out + t
        rem = rem - hi.astype(F32)
    return out


def _group_mat():
    r = lax.broadcasted_iota(jnp.int32, (LANES, LANES), 0) // HEAD
    c = lax.broadcasted_iota(jnp.int32, (LANES, LANES), 1) // HEAD
    return jnp.where(r == c, 1.0 / HEAD, 0.0).astype(BF16)


def _group_mean(v, gm):
    return _split_dot(v, gm, 3)


def _sigmoid(z):
    return 1.0 / (1.0 + jnp.exp(-z))


def _dot_nt(a, b):
    return lax.dot_general(a, b, (((1,), (1,)), ((), ())), preferred_element_type=F32)


def _dot_tn(a, b):
    return lax.dot_general(a, b, (((0,), (0,)), ((), ())), preferred_element_type=F32)


def _cast_into_full(w, chip, axis, name):
    _, r, c = w.shape
    tr = _tile(r, 256)
    nb = r // tr
    full = (DEPTH, 4 * r, c) if axis == 0 else (DEPTH, r, 4 * c)

    def body(k_ref, w_ref, o_ref):
        o_ref[...] = w_ref[...].astype(BF16)

    out_map = (lambda l, i, k: (l, k[0] * nb + i, 0)) if axis == 0 else (lambda l, i, k: (l, i, k[0]))
    grid_spec = pltpu.PrefetchScalarGridSpec(
        num_scalar_prefetch=1, grid=(DEPTH, nb),
        in_specs=[pl.BlockSpec((1, tr, c), lambda l, i, k: (l, i, 0))],
        out_specs=pl.BlockSpec((1, tr, c), out_map))
    return _pcall(body, name=name, grid_spec=grid_spec, out_shape=jax.ShapeDtypeStruct(full, BF16),
                  compiler_params=_cp(2))(chip.reshape(1).astype(jnp.int32), w)


def _rms_fwd(x, g, name):
    s, d = x.shape
    tm = _tile(s, 512)

    def body(x_ref, g_ref, h_ref):
        xv = x_ref[...]
        r = lax.rsqrt(jnp.mean(xv * xv, axis=-1, keepdims=True) + EPS)
        h_ref[...] = (xv * r * g_ref[...]).astype(BF16)

    return _pcall(body, name=name, grid=(s // tm,),
                  in_specs=[pl.BlockSpec((tm, d), lambda m: (m, 0)), pl.BlockSpec((1, d), lambda m: (0, 0))],
                  out_specs=pl.BlockSpec((tm, d), lambda m: (m, 0)),
                  out_shape=jax.ShapeDtypeStruct((s, d), BF16), compiler_params=_cp(1))(x, g)


def _rms_bwd_rows(dh, xv, g):
    r = lax.rsqrt(jnp.mean(xv * xv, axis=-1, keepdims=True) + EPS)
    xn = xv * r
    dxn = dh * g
    dx = r * (dxn - xn * jnp.mean(dxn * xn, axis=-1, keepdims=True))
    return dx, dh * xn


def _colsum8(v):
    tm, d = v.shape
    return jnp.sum(v.reshape(tm // 8, 8, d), axis=0)


def _inproj(h, w, name):
    s, d = h.shape
    n = w.shape[1]
    sw = d // 2
    ns = n // sw
    tm = _tile(s, 512)

    def body(h_ref, w_ref, o_ref):
        o_ref[0] = jnp.dot(h_ref[...], w_ref[...], preferred_element_type=F32).astype(BF16)

    return _pcall(body, name=name, grid=(ns, s // tm),
                  in_specs=[pl.BlockSpec((tm, d), lambda j, m: (m, 0)), pl.BlockSpec((d, sw), lambda j, m: (0, j))],
                  out_specs=pl.BlockSpec((1, tm, sw), lambda j, m: (j, m, 0)),
                  out_shape=jax.ShapeDtypeStruct((ns, s, sw), BF16), compiler_params=_cp(2))(h, w)


def _softplus_parts(z):
    lm = jnp.minimum(-z, 0.0) - jnp.log(1.0 + jnp.exp(-jnp.abs(z)))
    return lm, lm + z


def _attn_fwd(proj, name):
    _, s, sw = proj.shape
    nhp = sw // LANES
    t = _tile(s, ATT_T)
    nq = s // t
    scale = 1.0 / math.sqrt(HEAD)

    def body(q_ref, k_ref, v_ref, o_ref, tl_ref, acc_ref):
        i = pl.program_id(1)
        row = lax.broadcasted_iota(jnp.int32, (t, t), 0)
        col = lax.broadcasted_iota(jnp.int32, (t, t), 1)
        tri = (row > col).astype(BF16)
        below = col < row
        lane = lax.broadcasted_iota(jnp.int32, (t, LANES), 1)
        q = q_ref[0] * jnp.asarray(scale, BF16)
        qms = [jnp.where((lane // HEAD) == h, q, jnp.zeros_like(q)) for h in range(2)]
        acc_ref[...] = jnp.zeros_like(acc_ref)
        half = t // 2
        chains = [(h, r * half) for h in range(2) for r in range(2)]
        qparts = [qms[h][r0:r0 + half] for h, r0 in chains]

        def tile(j, rsums, diagonal):
            k0 = pl.multiple_of(j * t, t)
            kj = k_ref[0, pl.ds(k0, t), :]
            vj = v_ref[0, pl.ds(k0, t), :]
            zs = [_dot_nt(qp, kj) for qp in qparts]
            lms, lss, css = [], [], []
            for n, (h, r0) in enumerate(chains):
                lm, ls = _softplus_parts(zs[n])
                if diagonal:
                    lm = jnp.where(below[r0:r0 + half], lm, 0.0)
                lms.append(lm)
                lss.append(ls)
                css.append(_split_dot(lm, tri, 2))
            out = []
            for n, (h, r0) in enumerate(chains):
                a = jnp.exp(lss[n] + (rsums[n] + css[n]))
                if diagonal:
                    a = jnp.where(below[r0:r0 + half], a, 0.0)
                acc_ref[h, r0:r0 + half, :] += jnp.dot(a.astype(BF16), vj, preferred_element_type=F32)
                out.append(rsums[n] + jnp.sum(lms[n], axis=1, keepdims=True))
            return tuple(out)

        z1 = jnp.zeros((half, 1), F32)
        rsums = tile(i, (z1,) * 4, True)
        rsums = lax.fori_loop(0, i, lambda jj, c: tile(i - 1 - jj, c, False), rsums)
        for n, (h, r0) in enumerate(chains):
            tl_ref[h, r0:r0 + half, :] = rsums[n]
        o_ref[...] = jnp.where(lane < HEAD, acc_ref[0], acc_ref[1]).astype(BF16)

    return _pcall(
        body, name=name, grid=(nhp, nq),
        in_specs=[pl.BlockSpec((1, t, LANES), lambda hp, i: (4, i, hp)),
                  pl.BlockSpec((1, s, LANES), lambda hp, i: (5, 0, hp)),
                  pl.BlockSpec((1, s, LANES), lambda hp, i: (6, 0, hp))],
        out_specs=[pl.BlockSpec((t, LANES), lambda hp, i: (i, hp)),
                   pl.BlockSpec((2, t, 1), lambda hp, i: (hp, i, 0))],
        out_shape=[jax.ShapeDtypeStruct((s, sw), BF16), jax.ShapeDtypeStruct((2 * nhp, s, 1), F32)],
        scratch_shapes=[pltpu.VMEM((2, t, LANES), F32)],
        compiler_params=_cp(2))(proj, proj, proj)


def _conv_rows(cc_ref, ch_ref, w_ref, b_ref, r, tc):
    r0 = pl.multiple_of(r * tc, tc)
    u = cc_ref[0, pl.ds(r0, tc), :].astype(F32) * ch_ref[0, pl.ds(r0, tc), :].astype(F32)
    p0 = pl.multiple_of(jnp.maximum(r0 - 16, 0), 16)
    up = cc_ref[0, pl.ds(p0, 16), :].astype(F32) * ch_ref[0, pl.ds(p0, 16), :].astype(F32)
    up = up * (r > 0).astype(F32)
    prev1 = up[15:16, :]
    prev2 = up[14:15, :]
    rid = lax.broadcasted_iota(jnp.int32, u.shape, 0)
    s1 = jnp.where(rid == 0, prev1, pltpu.roll(u, 1, axis=0))
    s2 = jnp.where(rid == 0, prev2, jnp.where(rid == 1, prev1, pltpu.roll(u, 2, axis=0)))
    cv = b_ref[...] + s2 * w_ref[0:1, :] + s1 * w_ref[1:2, :] + u * w_ref[2:3, :]
    return r0, u, s1, s2, cv


def _mix_fwd(proj, ya, conv_w, conv_b, bg, name):
    _, s, sw = proj.shape
    nh = sw // LANES
    tc = _tile(s, 256)

    def body(cb_ref, cc_ref, ch_ref, cz_ref, ya_ref, az_ref, w_ref, b_ref, g_ref, y_ref):
        c = pl.program_id(0)
        gm = _group_mat()

        def finish(r0, yv, zg):
            n = yv * lax.rsqrt(_group_mean(yv * yv, gm) + EPS)
            y_ref[pl.ds(r0, tc), :] = (n * g_ref[...] * (zg * _sigmoid(zg))).astype(BF16)

        @pl.when(c < nh)
        def _():
            def step(r, carry):
                r0, _, _, _, cv = _conv_rows(cc_ref, ch_ref, w_ref, b_ref, r, tc)
                yc = cb_ref[0, pl.ds(r0, tc), :].astype(F32) * cv
                finish(r0, yc, cz_ref[0, pl.ds(r0, tc), :].astype(F32))
                return carry
            lax.fori_loop(0, s // tc, step, 0)

        @pl.when(c >= nh)
        def _():
            def step(r, carry):
                r0 = pl.multiple_of(r * tc, tc)
                finish(r0, ya_ref[pl.ds(r0, tc), :].astype(F32), az_ref[0, pl.ds(r0, tc), :].astype(F32))
                return carry
            lax.fori_loop(0, s // tc, step, 0)

    def sec(k):
        return pl.BlockSpec((1, s, LANES), lambda c: (k, 0, jnp.minimum(c, nh - 1)))

    return _pcall(
        body, name=name, grid=(2 * nh,),
        in_specs=[sec(0), sec(1), sec(2), sec(3),
                  pl.BlockSpec((s, LANES), lambda c: (0, jnp.maximum(c - nh, 0))),
                  pl.BlockSpec((1, s, LANES), lambda c: (7, 0, jnp.maximum(c - nh, 0))),
                  pl.BlockSpec((3, LANES), lambda c: (0, jnp.minimum(c, nh - 1))),
                  pl.BlockSpec((1, LANES), lambda c: (0, jnp.minimum(c, nh - 1))),
                  pl.BlockSpec((1, LANES), lambda c: (0, c))],
        out_specs=pl.BlockSpec((s, LANES), lambda c: (0, c)),
        out_shape=jax.ShapeDtypeStruct((s, 2 * sw), BF16), compiler_params=_cp(1),
    )(proj, proj, proj, proj, ya, proj, conv_w, conv_b, bg)


def _outproj(y, w, x, g, name):
    s, d = x.shape
    tm = _tile(s, 256)

    def body(y_ref, w_ref, x_ref, g_ref, x1_ref, hn_ref):
        x1 = x_ref[...] + jnp.dot(y_ref[...], w_ref[...], preferred_element_type=F32)
        x1_ref[...] = x1
        r = lax.rsqrt(jnp.mean(x1 * x1, axis=-1, keepdims=True) + EPS)
        hn_ref[...] = (x1 * r * g_ref[...]).astype(BF16)

    row = lambda m: (m, 0)
    fix = lambda m: (0, 0)
    return _pcall(body, name=name, grid=(s // tm,),
                  in_specs=[pl.BlockSpec((tm, d), row), pl.BlockSpec((d, d), fix), pl.BlockSpec((tm, d), row),
                            pl.BlockSpec((1, d), fix)],
                  out_specs=[pl.BlockSpec((tm, d), row), pl.BlockSpec((tm, d), row)],
                  out_shape=[jax.ShapeDtypeStruct((s, d), F32), jax.ShapeDtypeStruct((s, d), BF16)],
                  compiler_params=_cp(1))(y, w, x, g)


def _ple_fwd(hn, w_pg, b_pg, p, w_pe, x1, name):
    s, d = x1.shape
    pd = p.shape[1]
    tm = _tile(s, 256)

    def body(hn_ref, wg_ref, b_ref, p_ref, we_ref, x1_ref, x2_ref, gate_ref, e_ref):
        gate = _sigmoid(jnp.dot(hn_ref[...], wg_ref[...], preferred_element_type=F32) + b_ref[...])
        e = jnp.dot(p_ref[...].astype(BF16), we_ref[...], preferred_element_type=F32)
        x2_ref[...] = x1_ref[...] + gate * e
        gate_ref[...] = gate.astype(BF16)
        e_ref[...] = e.astype(BF16)

    row = lambda m: (m, 0)
    fix = lambda m: (0, 0)
    return _pcall(body, name=name, grid=(s // tm,),
                  in_specs=[pl.BlockSpec((tm, d), row), pl.BlockSpec((d, d), fix), pl.BlockSpec((1, d), fix),
                            pl.BlockSpec((tm, pd), row), pl.BlockSpec((pd, d), fix), pl.BlockSpec((tm, d), row)],
                  out_specs=[pl.BlockSpec((tm, d), row)] * 3,
                  out_shape=[jax.ShapeDtypeStruct((s, d), F32), jax.ShapeDtypeStruct((s, d), BF16),
                             jax.ShapeDtypeStruct((s, d), BF16)],
                  compiler_params=_cp(1))(hn, w_pg, b_pg, p, w_pe, x1)


def _loss_head(x, tgt, g, name):
    s, d = x.shape
    tm = _tile(s, 256)

    def body(x_ref, t_ref, g_ref, l_ref, dx_ref, dg_ref):
        m = pl.program_id(0)

        @pl.when(m == 0)
        def _():
            l_ref[...] = jnp.zeros_like(l_ref)
            dg_ref[...] = jnp.zeros_like(dg_ref)

        xv = x_ref[...]
        gv = g_ref[...]
        r = lax.rsqrt(jnp.mean(xv * xv, axis=-1, keepdims=True) + EPS)
        xn = xv * r
        err = xn * gv - t_ref[...]
        l_ref[...] += jnp.sum(err * err)
        dy = err * (1.0 / d)
        dxn = dy * gv
        dx_ref[...] = r * (dxn - xn * jnp.mean(dxn * xn, axis=-1, keepdims=True))
        dg_ref[...] += _colsum8(dy * xn)

    row = lambda m: (m, 0)
    fix = lambda m: (0, 0)
    return _pcall(body, name=name, grid=(s // tm,),
                  in_specs=[pl.BlockSpec((tm, d), row), pl.BlockSpec((tm, d), row), pl.BlockSpec((1, d), fix)],
                  out_specs=[pl.BlockSpec((8, LANES), fix), pl.BlockSpec((tm, d), row), pl.BlockSpec((8, d), fix)],
                  out_shape=[jax.ShapeDtypeStruct((8, LANES), F32), jax.ShapeDtypeStruct((s, d), F32),
                             jax.ShapeDtypeStruct((8, d), F32)],
                  compiler_params=_cp(1))(x, tgt, g)


def _ple_bwd(dx2, gate, e, x1, w_pg, g_ple, w_out, name):
    s, d = dx2.shape
    tm = _tile(s, 256)

    def body(dx2_ref, gate_ref, e_ref, x1_ref, wg_ref, g_ref, wo_ref,
             du_ref, de_ref, dx1_ref, dy_ref, db_ref, dg_ref):
        m = pl.program_id(0)

        @pl.when(m == 0)
        def _():
            db_ref[...] = jnp.zeros_like(db_ref)
            dg_ref[...] = jnp.zeros_like(dg_ref)

        dx2v = dx2_ref[...]
        gate = gate_ref[...].astype(F32)
        du = dx2v * e_ref[...].astype(F32) * gate * (1.0 - gate)
        de_ref[...] = (dx2v * gate).astype(BF16)
        dub = du.astype(BF16)
        du_ref[...] = dub
        db_ref[...] += _colsum8(du)
        dhn = _dot_nt(dub, wg_ref[...])
        dxr, dgr = _rms_bwd_rows(dhn, x1_ref[...], g_ref[...])
        dx1 = dx2v + dxr
        dx1_ref[...] = dx1
        dg_ref[...] += _colsum8(dgr)
        dy_ref[...] = _dot_nt(dx1.astype(BF16), wo_ref[...]).astype(BF16)

    row = lambda m: (m, 0)
    fix = lambda m: (0, 0)
    t = pl.BlockSpec((tm, d), row)
    return _pcall(body, name=name, grid=(s // tm,),
                  in_specs=[t, t, t, t, pl.BlockSpec((d, d), fix), pl.BlockSpec((1, d), fix), pl.BlockSpec((d, d), fix)],
                  out_specs=[t, t, t, t, pl.BlockSpec((8, d), fix), pl.BlockSpec((8, d), fix)],
                  out_shape=[jax.ShapeDtypeStruct((s, d), BF16), jax.ShapeDtypeStruct((s, d), BF16),
                             jax.ShapeDtypeStruct((s, d), F32), jax.ShapeDtypeStruct((s, d), BF16),
                             jax.ShapeDtypeStruct((8, d), F32), jax.ShapeDtypeStruct((8, d), F32)],
                  compiler_params=_cp(1))(dx2, gate, e, x1, w_pg, g_ple, w_out)


def _mm_tn(a, b, stack, layer, name, b_sections=False):
    s, ka = a.shape
    if b_sections:
        ns, _, tn = b.shape
        n = ns * tn
    else:
        n = b.shape[1]
        tn = _tile(n, 1024)
        ns = n // tn
    tk = _tile(s, 512)
    nk = s // tk

    def body(*refs):
        a_ref, b_ref = refs[0], refs[1]
        o_ref, acc_ref = refs[-2], refs[-1]
        k = pl.program_id(1)

        @pl.when(k == 0)
        def _():
            acc_ref[...] = jnp.zeros_like(acc_ref)

        bv = b_ref[0] if b_sections else b_ref[...]
        acc_ref[...] += _dot_tn(a_ref[...].astype(BF16), bv.astype(BF16))

        @pl.when(k == nk - 1)
        def _():
            o_ref[0] = acc_ref[...]

    b_spec = (pl.BlockSpec((1, tk, tn), lambda j, k: (j, k, 0)) if b_sections
              else pl.BlockSpec((tk, tn), lambda j, k: (k, j)))
    in_specs = [pl.BlockSpec((tk, ka), lambda j, k: (k, 0)), b_spec]
    args = [a, b]
    aliases = {}
    if stack is not None:
        in_specs.append(ANY)
        args.append(stack)
        aliases = {2: 0}
    return _pcall(body, name=name, grid=(ns, nk), in_specs=in_specs,
                  out_specs=pl.BlockSpec((1, ka, tn), lambda j, k: (layer, 0, j)),
                  out_shape=jax.ShapeDtypeStruct((DEPTH, ka, n), F32),
                  scratch_shapes=[pltpu.VMEM((ka, tn), F32)], input_output_aliases=aliases,
                  compiler_params=_cp(2))(*args)


def _norm_gate_bwd(dy, yv, zg, g, gm):
    r = lax.rsqrt(_group_mean(yv * yv, gm) + EPS)
    n = yv * r
    sg = _sigmoid(zg)
    sil = zg * sg
    dzg = dy * n * g * (sg * (1.0 + zg * (1.0 - sg)))
    dn = dy * g * sil
    dyv = r * (dn - n * _group_mean(dn * n, gm))
    return dyv, dzg, dy * n * sil


def _convmix_bwd(dy, proj, conv_w, conv_b, bg, name):
    _, s, sw = proj.shape
    nh = sw // LANES
    tc = _tile(s, 256)
    nr = s // tc

    def body(dy_ref, cb_ref, cc_ref, ch_ref, cz_ref, w_ref, b_ref, g_ref,
             dp_ref, dw_ref, db_ref, dg_ref, dcv_ref):
        gm = _group_mat()
        dcv_ref[pl.ds(s, 8), :] = jnp.zeros((8, LANES), F32)

        def pass1(r, carry):
            dw0, dw1, dw2, db, dg = carry
            r0, u, s1, s2, cv = _conv_rows(cc_ref, ch_ref, w_ref, b_ref, r, tc)
            cb = cb_ref[0, pl.ds(r0, tc), :].astype(F32)
            dyc, dcz, dgr = _norm_gate_bwd(dy_ref[pl.ds(r0, tc), :].astype(F32), cb * cv,
                                           cz_ref[0, pl.ds(r0, tc), :].astype(F32), g_ref[...], gm)
            dp_ref[0, pl.ds(r0, tc), :] = (dyc * cv).astype(BF16)
            dp_ref[3, pl.ds(r0, tc), :] = dcz.astype(BF16)
            dcv = dyc * cb
            dcv_ref[pl.ds(r0, tc), :] = dcv
            return (dw0 + _colsum8(dcv * s2), dw1 + _colsum8(dcv * s1), dw2 + _colsum8(dcv * u),
                    db + _colsum8(dcv), dg + _colsum8(dgr))

        z8 = jnp.zeros((8, LANES), F32)
        dw0, dw1, dw2, db, dg = lax.fori_loop(0, nr, pass1, (z8, z8, z8, z8, z8))
        dw_ref[0] = dw0
        dw_ref[1] = dw1
        dw_ref[2] = dw2
        db_ref[...] = db
        dg_ref[...] = dg

        def pass2(r, carry):
            r0 = pl.multiple_of(r * tc, tc)
            dcv = dcv_ref[pl.ds(r0, tc), :]
            nxt = dcv_ref[pl.ds(pl.multiple_of(r0 + tc, 8), 8), :]
            rid = lax.broadcasted_iota(jnp.int32, dcv.shape, 0)
            n1 = jnp.where(rid == tc - 1, nxt[0:1, :], pltpu.roll(dcv, tc - 1, axis=0))
            n2 = jnp.where(rid == tc - 1, nxt[1:2, :],
                           jnp.where(rid == tc - 2, nxt[0:1, :], pltpu.roll(dcv, tc - 2, axis=0)))
            du = dcv * w_ref[2:3, :] + n1 * w_ref[1:2, :] + n2 * w_ref[0:1, :]
            dp_ref[1, pl.ds(r0, tc), :] = (du * ch_ref[0, pl.ds(r0, tc), :].astype(F32)).astype(BF16)
            dp_ref[2, pl.ds(r0, tc), :] = (du * cc_ref[0, pl.ds(r0, tc), :].astype(F32)).astype(BF16)
            return carry

        lax.fori_loop(0, nr, pass2, 0)

    def sec(k):
        return pl.BlockSpec((1, s, LANES), lambda c: (k, 0, c))

    col = lambda c: (0, c)
    return _pcall(
        body, name=name, grid=(nh,),
        in_specs=[pl.BlockSpec((s, LANES), col), sec(0), sec(1), sec(2), sec(3),
                  pl.BlockSpec((3, LANES), col), pl.BlockSpec((1, LANES), col), pl.BlockSpec((1, LANES), col)],
        out_specs=[pl.BlockSpec((4, s, LANES), lambda c: (0, 0, c)), pl.BlockSpec((3, 8, LANES), lambda c: (0, 0, c)),
                   pl.BlockSpec((8, LANES), col), pl.BlockSpec((8, LANES), col)],
        out_shape=[jax.ShapeDtypeStruct((8, s, sw), BF16), jax.ShapeDtypeStruct((3, 8, sw), F32),
                   jax.ShapeDtypeStruct((8, sw), F32), jax.ShapeDtypeStruct((8, sw), F32)],
        scratch_shapes=[pltpu.VMEM((s + 8, LANES), F32)], compiler_params=_cp(1),
    )(dy, proj, proj, proj, proj, conv_w, conv_b, bg)


def _attn_bwd(proj, dy, ya, tl, bg, buf, name):
    _, s, sw = proj.shape
    nhp = sw // LANES
    t = _tile(s, ATT_T)
    nq = s // t
    scale = 1.0 / math.sqrt(HEAD)

    def body(q_ref, k_ref, v_ref, az_ref, dy_ref, ya_ref, tl_ref, g_ref, buf_ref, out_ref, dg_ref,
             dka_ref, dva_ref, dqa_ref):
        step = pl.program_id(1)
        i = nq - 1 - step

        @pl.when(step == 0)
        def _():
            dka_ref[...] = jnp.zeros_like(dka_ref)
            dva_ref[...] = jnp.zeros_like(dva_ref)
            dg_ref[...] = jnp.zeros_like(dg_ref)

        dyv, dzg, dgr = _norm_gate_bwd(dy_ref[...].astype(F32), ya_ref[...].astype(F32), az_ref[0].astype(F32),
                                       g_ref[...], _group_mat())
        out_ref[3] = dzg.astype(BF16)
        dg_ref[...] += _colsum8(dgr)

        row = lax.broadcasted_iota(jnp.int32, (t, t), 0)
        col = lax.broadcasted_iota(jnp.int32, (t, t), 1)
        tri = (row <= col).astype(BF16)
        below = col < row
        lane = lax.broadcasted_iota(jnp.int32, (t, LANES), 1)
        q = q_ref[0] * jnp.asarray(scale, BF16)
        do = dyv.astype(BF16)
        qms = [jnp.where((lane // HEAD) == h, q, jnp.zeros_like(q)) for h in range(2)]
        doms = [jnp.where((lane // HEAD) == h, do, jnp.zeros_like(do)) for h in range(2)]
        dqa_ref[...] = jnp.zeros_like(dqa_ref)
        half = t // 2
        chains = [(h, r * half) for h in range(2) for r in range(2)]
        qparts = [qms[h][r0:r0 + half] for h, r0 in chains]
        doparts = [doms[h][r0:r0 + half] for h, r0 in chains]
        tots = [tl_ref[h, r0:r0 + half, :] for h, r0 in chains]

        def tile(j, carry, diagonal):
            k0 = pl.multiple_of(j * t, t)
            kj = k_ref[0, pl.ds(k0, t), :]
            vj = v_ref[0, pl.ds(k0, t), :]
            zs = [_dot_nt(qp, kj) for qp in qparts]
            das = [_dot_nt(dp, vj) for dp in doparts]
            lms, lss, cls = [], [], []
            for n, (h, r0) in enumerate(chains):
                lm, ls = _softplus_parts(zs[n])
                if diagonal:
                    lm = jnp.where(below[r0:r0 + half], lm, 0.0)
                lms.append(lm)
                lss.append(ls)
                cls.append(_split_dot(lm, tri, 2))
            abs_, gs, cgs = [], [], []
            for n, (h, r0) in enumerate(chains):
                a = jnp.exp(lss[n] + (tots[n] - carry[n][0] - cls[n]))
                if diagonal:
                    a = jnp.where(below[r0:r0 + half], a, 0.0)
                g = a * das[n]
                abs_.append(a.astype(BF16))
                gs.append(g)
                cgs.append(_split_dot(g, tri, 2))
            out = []
            dk = None
            dv = None
            for n, (h, r0) in enumerate(chains):
                psum, gsum = carry[n]
                dz = gs[n] - jnp.exp(lss[n]) * (gsum + cgs[n])
                if diagonal:
                    dz = jnp.where(below[r0:r0 + half], dz, 0.0)
                dz = dz.astype(BF16)
                dqa_ref[h, r0:r0 + half, :] += jnp.dot(dz, kj, preferred_element_type=F32)
                dkh = _dot_tn(dz, qparts[n])
                dvh = _dot_tn(abs_[n], doparts[n])
                dk = dkh if dk is None else dk + dkh
                dv = dvh if dv is None else dv + dvh
                out.append((psum + jnp.sum(lms[n], axis=1, keepdims=True), gsum + jnp.sum(gs[n], axis=1, keepdims=True)))
            dka_ref[pl.ds(k0, t), :] += dk
            dva_ref[pl.ds(k0, t), :] += dv
            return tuple(out)

        z1 = jnp.zeros((half, 1), F32)
        carry = lax.fori_loop(0, i, lambda j, c: tile(j, c, False), ((z1, z1),) * 4)
        tile(i, carry, True)
        out_ref[0] = (jnp.where(lane < HEAD, dqa_ref[0], dqa_ref[1]) * scale).astype(BF16)
        own = pl.multiple_of(i * t, t)
        out_ref[1] = dka_ref[pl.ds(own, t), :].astype(BF16)
        out_ref[2] = dva_ref[pl.ds(own, t), :].astype(BF16)

    def rows(sec):
        return pl.BlockSpec((1, t, LANES), lambda hp, st: (sec, nq - 1 - st, hp))

    def whole(sec):
        return pl.BlockSpec((1, s, LANES), lambda hp, st: (sec, 0, hp))

    return _pcall(
        body, name=name, grid=(nhp, nq),
        in_specs=[rows(4), whole(5), whole(6), rows(7),
                  pl.BlockSpec((t, LANES), lambda hp, st: (nq - 1 - st, hp + nhp)),
                  pl.BlockSpec((t, LANES), lambda hp, st: (nq - 1 - st, hp)),
                  pl.BlockSpec((2, t, 1), lambda hp, st: (hp, nq - 1 - st, 0)),
                  pl.BlockSpec((1, LANES), lambda hp, st: (0, hp + nhp)), ANY],
        out_specs=[pl.BlockSpec((4, t, LANES), lambda hp, st: (1, nq - 1 - st, hp)),
                   pl.BlockSpec((8, LANES), lambda hp, st: (0, hp))],
        out_shape=[jax.ShapeDtypeStruct(buf.shape, buf.dtype), jax.ShapeDtypeStruct((8, sw), F32)],
        input_output_aliases={8: 0},
        scratch_shapes=[pltpu.VMEM((s, LANES), F32), pltpu.VMEM((s, LANES), F32), pltpu.VMEM((2, t, LANES), F32)],
        compiler_params=_cp(2))(proj, proj, proj, proj, dy, ya, tl, bg, buf)


def _inproj_bwd(dproj, w, x, g, dx1, name):
    ns, s, sw = dproj.shape
    d = x.shape[1]
    tm = _tile(s, 256)

    def body(dp_ref, w_ref, x_ref, g_ref, dx1_ref, dx_ref, dg_ref):
        @pl.when(pl.program_id(0) == 0)
        def _():
            dg_ref[...] = jnp.zeros_like(dg_ref)

        dh = _dot_nt(dp_ref[0], w_ref[:, 0:sw])
        for k in range(1, ns):
            dh = dh + _dot_nt(dp_ref[k], w_ref[:, k * sw:(k + 1) * sw])
        dxr, dgr = _rms_bwd_rows(dh, x_ref[...], g_ref[...])
        dx_ref[...] = dx1_ref[...] + dxr
        dg_ref[...] += _colsum8(dgr)

    row = lambda m: (m, 0)
    fix = lambda m: (0, 0)
    return _pcall(body, name=name, grid=(s // tm,),
                  in_specs=[pl.BlockSpec((ns, tm, sw), lambda m: (0, m, 0)), pl.BlockSpec((d, ns * sw), fix),
                            pl.BlockSpec((tm, d), row), pl.BlockSpec((1, d), fix), pl.BlockSpec((tm, d), row)],
                  out_specs=[pl.BlockSpec((tm, d), row), pl.BlockSpec((8, d), fix)],
                  out_shape=[jax.ShapeDtypeStruct((s, d), F32), jax.ShapeDtypeStruct((8, d), F32)],
                  compiler_params=_cp(1))(dproj, w, x, g, dx1)


def _adamw(w, g, m, v, name):
    r, c = w.shape
    tr = _tile(r, 256)
    c1 = 1.0 - ADAM_B1 ** ADAM_STEP
    c2 = 1.0 - ADAM_B2 ** ADAM_STEP

    def body(w_ref, g_ref, m_ref, v_ref, d_ref, mo_ref, vo_ref):
        gv = g_ref[...]
        mn = ADAM_B1 * m_ref[...] + (1.0 - ADAM_B1) * gv
        vn = ADAM_B2 * v_ref[...] + (1.0 - ADAM_B2) * (gv * gv)
        d_ref[...] = -ADAM_LR * ((mn / c1) / (jnp.sqrt(vn / c2) + ADAM_EPS) + ADAM_WD * w_ref[...])
        mo_ref[...] = mn
        vo_ref[...] = vn

    t = pl.BlockSpec((tr, c), lambda i: (i, 0))
    return _pcall(body, name=name, grid=(r // tr,), in_specs=[t] * 4, out_specs=[t] * 3,
                  out_shape=[jax.ShapeDtypeStruct((r, c), F32)] * 3, compiler_params=_cp(1))(w, g, m, v)


def _add_layer(stack, other, layer, name):
    _, r, c = stack.shape
    tr = _tile(r, 256)

    def body(l_ref, s_ref, o_ref, out_ref):
        out_ref[...] = s_ref[0] + o_ref[...]

    grid_spec = pltpu.PrefetchScalarGridSpec(
        num_scalar_prefetch=1, grid=(r // tr,),
        in_specs=[pl.BlockSpec((1, tr, c), lambda i, l: (l[0], i, 0)), pl.BlockSpec((tr, c), lambda i, l: (i, 0))],
        out_specs=pl.BlockSpec((tr, c), lambda i, l: (i, 0)))
    return _pcall(body, name=name, grid_spec=grid_spec, out_shape=jax.ShapeDtypeStruct((r, c), F32),
                  compiler_params=_cp(1))(layer.reshape(1).astype(jnp.int32), stack, other)


def _sum_shard(full, parts, chip, layer, axis, name):
    _, r, c = parts.shape
    tr = _tile(r, 256)

    def body(k_ref, f_ref, p_ref, out_ref):
        out_ref[0] = ((f_ref[...] + p_ref[0]) + p_ref[1]) + p_ref[2]

    if axis == 1:
        f_spec = pl.BlockSpec((tr, c), lambda i, k: (i, k[0]))
    else:
        nb = r // tr
        f_spec = pl.BlockSpec((tr, c), lambda i, k: (k[0] * nb + i, 0))
    grid_spec = pltpu.PrefetchScalarGridSpec(
        num_scalar_prefetch=1, grid=(r // tr,),
        in_specs=[f_spec, pl.BlockSpec((3, tr, c), lambda i, k: (0, i, 0))],
        out_specs=pl.BlockSpec((1, tr, c), lambda i, k: (k[1], i, 0)))
    return _pcall(body, name=name, grid_spec=grid_spec, out_shape=jax.ShapeDtypeStruct((DEPTH, r, c), F32),
                  compiler_params=_cp(1))(jnp.stack([chip, layer]).astype(jnp.int32), full, parts)


def _sum_slots(slots, name):
    n = slots.shape[0]

    def body(s_ref, o_ref):
        acc = s_ref[0]
        for i in range(1, n):
            acc = acc + s_ref[i]
        o_ref[...] = acc

    return _pcall(body, name=name, out_shape=jax.ShapeDtypeStruct(slots.shape[1:], F32))(slots)


def _place():
    return lax.axis_index("x"), lax.axis_index("y"), lax.axis_index("c")


def _shard_view(ref, axis, chip, size):
    if axis == 0:
        return ref.at[pl.ds(chip * size, size), :]
    return ref.at[:, pl.ds(chip * size, size)]


SHARD_AXES = (1, 0, 0, 1)


def _gather_weights(fulls):
    n = len(fulls)
    sizes = [f.shape[1 + ax] // 4 for f, ax in zip(fulls, SHARD_AXES)]

    def body(*refs):
        ins, outs = refs[:n], refs[n:2 * n]
        ssem, rsem = refs[2 * n:]
        x, y, c = _place()
        me = 2 * x + y
        chips = [(1 - x, y), (x, 1 - y), (1 - x, 1 - y)]

        def piece(a, layer, chip, of=outs):
            return _shard_view(of[a].at[layer], SHARD_AXES[a], chip, sizes[a])

        sends = []
        for a in range(n):
            for j, (cx, cy) in enumerate(chips):
                cp = pltpu.make_async_remote_copy(
                    src_ref=piece(a, c, me, ins), dst_ref=piece(a, c, me), send_sem=ssem.at[a, j],
                    recv_sem=rsem.at[a, j], device_id=(cx, cy, c), device_id_type=MESH)
                cp.start()
                sends.append(cp)
        for a in range(n):
            for j, (cx, cy) in enumerate(chips):
                got = piece(a, c, 2 * cx + cy)
                pltpu.make_async_remote_copy(
                    src_ref=got, dst_ref=got, send_sem=ssem.at[a, j], recv_sem=rsem.at[a, j],
                    device_id=(cx, cy, c), device_id_type=MESH).wait_recv()
                cp = pltpu.make_async_remote_copy(
                    src_ref=got, dst_ref=got, send_sem=ssem.at[a, 3 + j], recv_sem=rsem.at[a, 3 + j],
                    device_id=(x, y, 1 - c), device_id_type=MESH)
                cp.start()
                sends.append(cp)
        for a in range(n):
            for j, (cx, cy) in enumerate(chips):
                got = piece(a, 1 - c, 2 * cx + cy)
                pltpu.make_async_remote_copy(
                    src_ref=got, dst_ref=got, send_sem=ssem.at[a, 3 + j], recv_sem=rsem.at[a, 3 + j],
                    device_id=(x, y, 1 - c), device_id_type=MESH).wait_recv()
        for cp in sends:
            cp.wait_send()

    return _pcall(body, name="gather_weights", in_specs=[ANY] * n, out_specs=[ANY] * n,
                  out_shape=[jax.ShapeDtypeStruct(f.shape, f.dtype) for f in fulls],
                  input_output_aliases={a: a for a in range(n)},
                  scratch_shapes=[pltpu.SemaphoreType.DMA((n, 6)), pltpu.SemaphoreType.DMA((n, 6))])(*fulls)


def _swap_layers(stacks):
    n = len(stacks)

    def body(*refs):
        srcs, outs = refs[:n], refs[n:2 * n]
        ssem, rsem = refs[2 * n:]
        x, y, c = _place()
        cps = [pltpu.make_async_remote_copy(src_ref=srcs[a].at[1 - c], dst_ref=outs[a], send_sem=ssem.at[a],
                                            recv_sem=rsem.at[a], device_id=(x, y, 1 - c), device_id_type=MESH)
               for a in range(n)]
        for cp in cps:
            cp.start()
        for cp in cps:
            cp.wait()

    return _pcall(body, name="swap_layers", in_specs=[ANY] * n, out_specs=[ANY] * n,
                  out_shape=[jax.ShapeDtypeStruct(st.shape[1:], st.dtype) for st in stacks],
                  scratch_shapes=[pltpu.SemaphoreType.DMA((n,)), pltpu.SemaphoreType.DMA((n,))])(*stacks)


def _scatter_shards(fulls):
    n = len(fulls)
    shard_shapes = []
    for f, ax in zip(fulls, SHARD_AXES):
        sh = list(f.shape)
        sh[ax] //= 4
        shard_shapes.append(tuple(sh))

    def body(*refs):
        srcs, outs = refs[:n], refs[n:2 * n]
        ssem, rsem = refs[2 * n:]
        x, y, c = _place()
        chips = [(1 - x, y), (x, 1 - y), (1 - x, 1 - y)]
        cps = []
        for a in range(n):
            for j, (cx, cy) in enumerate(chips):
                src = _shard_view(srcs[a], SHARD_AXES[a], 2 * cx + cy, shard_shapes[a][SHARD_AXES[a]])
                cps.append(pltpu.make_async_remote_copy(
                    src_ref=src, dst_ref=outs[a].at[j], send_sem=ssem.at[a, j], recv_sem=rsem.at[a, j],
                    device_id=(cx, cy, c), device_id_type=MESH))
        for cp in cps:
            cp.start()
        for cp in cps:
            cp.wait()

    return _pcall(body, name="scatter_shards", in_specs=[ANY] * n, out_specs=[ANY] * n,
                  out_shape=[jax.ShapeDtypeStruct((3,) + sh, F32) for sh in shard_shapes],
                  scratch_shapes=[pltpu.SemaphoreType.DMA((n, 3)), pltpu.SemaphoreType.DMA((n, 3))])(*fulls)


def _pair_layers(stacks):
    n = len(stacks)

    def body(*refs):
        ins, outs = refs[:n], refs[n:2 * n]
        ssem, rsem = refs[2 * n:]
        x, y, c = _place()
        cps = [pltpu.make_async_remote_copy(src_ref=ins[a].at[c], dst_ref=outs[a].at[c], send_sem=ssem.at[a],
                                            recv_sem=rsem.at[a], device_id=(x, y, 1 - c), device_id_type=MESH)
               for a in range(n)]
        for cp in cps:
            cp.start()
        for a in range(n):
            got = outs[a].at[1 - c]
            pltpu.make_async_remote_copy(src_ref=got, dst_ref=got, send_sem=ssem.at[a], recv_sem=rsem.at[a],
                                         device_id=(x, y, 1 - c), device_id_type=MESH).wait_recv()
        for cp in cps:
            cp.wait_send()

    return _pcall(body, name="pair_layers", in_specs=[ANY] * n, out_specs=[ANY] * n,
                  out_shape=[jax.ShapeDtypeStruct(st.shape, st.dtype) for st in stacks],
                  input_output_aliases={a: a for a in range(n)},
                  scratch_shapes=[pltpu.SemaphoreType.DMA((n,)), pltpu.SemaphoreType.DMA((n,))])(*stacks)


def _exchange_small(pack, name):
    nd = 8

    def body(p_ref, o_ref, ssem, rsem):
        x, y, c = _place()
        me = 4 * x + 2 * y + c
        o_ref[me] = p_ref[...]
        cps = []
        for j in range(1, nd):
            px, py, pc = x ^ (j >> 2), y ^ ((j >> 1) & 1), c ^ (j & 1)
            cps.append(pltpu.make_async_remote_copy(
                src_ref=p_ref, dst_ref=o_ref.at[me], send_sem=ssem.at[j - 1], recv_sem=rsem.at[j - 1],
                device_id=(px, py, pc), device_id_type=MESH))
        for cp in cps:
            cp.start()
        for j in range(1, nd):
            peer = me ^ j
            got = o_ref.at[peer]
            pltpu.make_async_remote_copy(src_ref=got, dst_ref=got, send_sem=ssem.at[j - 1], recv_sem=rsem.at[j - 1],
                                         device_id=(x, y, c), device_id_type=MESH).wait_recv()
        for cp in cps:
            cp.wait_send()

    vm = pl.BlockSpec(memory_space=pltpu.VMEM)
    return _pcall(body, name=name, in_specs=[vm], out_specs=vm,
                  out_shape=jax.ShapeDtypeStruct((nd,) + pack.shape, pack.dtype),
                  scratch_shapes=[pltpu.SemaphoreType.DMA((nd - 1,)), pltpu.SemaphoreType.DMA((nd - 1,))])(pack)


def _row(v):
    return v.reshape(1, -1)


def _local_step(x, p, tgt, norm_g, conv_w, conv_b, branch_g, ple_norm_g, b_pg, final_g, w_in, w_out, w_pg, w_pe):
    saved = []
    xl = x
    for l in range(DEPTH):
        h = _rms_fwd(xl, _row(norm_g[l]), f"rms_fwd_{l}")
        proj = _inproj(h, w_in[l], f"inproj_{l}")
        ya, tl = _attn_fwd(proj, f"attn_fwd_{l}")
        y = _mix_fwd(proj, ya, conv_w[l], _row(conv_b[l]), _row(branch_g[l]), f"mix_fwd_{l}")
        x1, hn = _outproj(y, w_out[l], xl, _row(ple_norm_g[l]), f"outproj_{l}")
        x2, gate, e = _ple_fwd(hn, w_pg[l], _row(b_pg[l]), p[l], w_pe[l], x1, f"ple_fwd_{l}")
        saved.append((xl, h, proj, ya, tl, y, x1, hn, gate, e))
        xl = x2

    sq, dx, d_final = _loss_head(xl, tgt, _row(final_g), "loss_head")

    g_in = g_out = g_pg = g_pe = None
    small = {k: [None] * DEPTH for k in ("norm_g", "conv_w", "conv_b", "branch_g", "ple_norm_g", "b_pg")}
    for l in reversed(range(DEPTH)):
        xl, h, proj, ya, tl, y, x1, hn, gate, e = saved[l]
        du, de, dx1, dy, db_pg, d_ple = _ple_bwd(dx, gate, e, x1, w_pg[l], _row(ple_norm_g[l]), w_out[l], f"ple_bwd_{l}")
        g_pg = _mm_tn(hn, du, g_pg, l, f"grad_w_pg_{l}")
        g_pe = _mm_tn(p[l], de, g_pe, l, f"grad_w_pe_{l}")
        g_out = _mm_tn(y, dx1, g_out, l, f"grad_w_out_{l}")
        dpc, d_cw, d_cb, d_bg_c = _convmix_bwd(dy, proj, conv_w[l], _row(conv_b[l]), _row(branch_g[l]), f"convmix_bwd_{l}")
        dproj, d_bg_a = _attn_bwd(proj, dy, ya, tl, _row(branch_g[l]), dpc, f"attn_bwd_{l}")
        g_in = _mm_tn(h, dproj, g_in, l, f"grad_w_in_{l}", b_sections=True)
        dx, d_norm = _inproj_bwd(dproj, w_in[l], xl, _row(norm_g[l]), dx1, f"inproj_bwd_{l}")
        small["norm_g"][l] = jnp.sum(d_norm, axis=0)
        small["conv_w"][l] = jnp.sum(d_cw, axis=1)
        small["conv_b"][l] = jnp.sum(d_cb, axis=0)
        small["branch_g"][l] = jnp.concatenate([jnp.sum(d_bg_c, axis=0), jnp.sum(d_bg_a, axis=0)])
        small["ple_norm_g"][l] = jnp.sum(d_ple, axis=0)
        small["b_pg"][l] = jnp.sum(db_pg, axis=0)
    small = {k: jnp.stack(v) for k, v in small.items()}
    small["final_g"] = jnp.sum(d_final, axis=0)
    return sq[0, 0], dx, (g_in, g_out, g_pg, g_pe), small


SMALL_ORDER = ("norm_g", "conv_w", "conv_b", "branch_g", "ple_norm_g", "b_pg", "final_g")


def _pack(parts, width):
    flat = jnp.concatenate([v.reshape(-1) for v in parts])
    rows = -(-flat.shape[0] // width)
    rows = -(-rows // 8) * 8
    return jnp.pad(flat, (0, rows * width - flat.shape[0])).reshape(rows, width)


def _unpack(packed, like):
    flat = packed.reshape(-1)
    out, off = [], 0
    for v in like:
        out.append(flat[off:off + v.size].reshape(v.shape))
        off += v.size
    return out


def kernel(x, p, norm_g, w_in, conv_w, conv_b, branch_g, w_out, ple_norm_g, w_pg, b_pg, w_pe, final_g, loss_target, m_norm_g, m_w_in, m_conv_w, m_conv_b, m_branch_g, m_w_out, m_ple_norm_g, m_w_pg, m_b_pg, m_w_pe, m_final_g, v_norm_g, v_w_in, v_conv_w, v_conv_b, v_branch_g, v_w_out, v_ple_norm_g, v_w_pg, v_b_pg, v_w_pe, v_final_g):
    ix, iy, ic = _place()
    chip = 2 * ix + iy
    d = x.shape[-1]

    big_w = (w_in, w_out, w_pg, w_pe)
    own = [_cast_into_full(w, chip, ax, f"cast_{i}") for i, (w, ax) in enumerate(zip(big_w, SHARD_AXES))]
    full_in, full_out, full_pg, full_pe = _gather_weights(own)
    cw_shard = conv_w.shape[-1]
    cw_slots = _exchange_small(_pack([conv_w], LANES), "exchange_conv_w")
    conv_full = jnp.concatenate([_unpack(cw_slots[2 * k], [conv_w])[0] for k in range(4)], axis=-1)

    sq, dx, big_g, small_g = _local_step(
        x[0], p[:, 0], loss_target[0], norm_g, conv_full, conv_b, branch_g, ple_norm_g, b_pg, final_g,
        full_in, full_out, full_pg, full_pe)

    from_sibling = _swap_layers(big_g)
    chip_sums = [_add_layer(g, o, ic, f"add_layer_{i}") for i, (g, o) in enumerate(zip(big_g, from_sibling))]
    partials = _scatter_shards(chip_sums)
    reduced = [_sum_shard(f, pr, chip, ic, ax, f"sum_shard_{i}")
               for i, (f, pr, ax) in enumerate(zip(chip_sums, partials, SHARD_AXES))]
    g_big = _pair_layers(reduced)

    parts = [small_g[k] for k in SMALL_ORDER] + [sq.reshape(1)]
    slots = _exchange_small(_pack(parts, d), "exchange_small_grads")
    total = _unpack(_sum_slots(slots, "sum_small"), parts)
    g_small = dict(zip(SMALL_ORDER, total[:-1]))
    loss = 0.5 * total[-1][0] / d
    g_small["conv_w"] = lax.dynamic_slice_in_dim(g_small["conv_w"], chip * cw_shard, cw_shard, axis=2)

    grads = dict(g_small)
    grads.update(w_in=g_big[0], w_out=g_big[1], w_pg=g_big[2], w_pe=g_big[3])
    weights = dict(norm_g=norm_g, w_in=w_in, conv_w=conv_w, conv_b=conv_b, branch_g=branch_g, w_out=w_out,
                   ple_norm_g=ple_norm_g, w_pg=w_pg, b_pg=b_pg, w_pe=w_pe, final_g=final_g)
    ms = dict(norm_g=m_norm_g, w_in=m_w_in, conv_w=m_conv_w, conv_b=m_conv_b, branch_g=m_branch_g, w_out=m_w_out,
              ple_norm_g=m_ple_norm_g, w_pg=m_w_pg, b_pg=m_b_pg, w_pe=m_w_pe, final_g=m_final_g)
    vs = dict(norm_g=v_norm_g, w_in=v_w_in, conv_w=v_conv_w, conv_b=v_conv_b, branch_g=v_branch_g, w_out=v_w_out,
              ple_norm_g=v_ple_norm_g, w_pg=v_w_pg, b_pg=v_b_pg, w_pe=v_w_pe, final_g=v_final_g)
    names = ("norm_g", "w_in", "conv_w", "conv_b", "branch_g", "w_out", "ple_norm_g", "w_pg", "b_pg", "w_pe", "final_g")
    delta, new_m, new_v = {}, {}, {}
    for k in ("w_in", "w_out", "w_pg", "w_pe"):
        shp = weights[k].shape
        two = lambda a: a.reshape(-1, shp[-1])
        dl, mn, vn = _adamw(two(weights[k]), two(grads[k]), two(ms[k]), two(vs[k]), f"adamw_{k}")
        delta[k], new_m[k], new_v[k] = dl.reshape(shp), mn.reshape(shp), vn.reshape(shp)
        grads[k] = grads[k].reshape(shp)
    like = [weights[k] for k in SMALL_ORDER]
    packs = [_pack([src[k] for k in SMALL_ORDER], d) for src in (weights, grads, ms, vs)]
    outs = _adamw(*packs, "adamw_small")
    for res, o in zip((delta, new_m, new_v), outs):
        res.update(dict(zip(SMALL_ORDER, _unpack(o, like))))

    return (loss, dx[None], *[grads[k] for k in names], *[delta[k] for k in names],
            *[new_m[k] for k in names], *[new_v[k] for k in names])
```

```python
import math

import jax
import jax.numpy as jnp
from jax import lax
from jax.experimental import pallas as pl
from jax.experimental.pallas import tpu as pltpu

F32 = jnp.float32
BF16 = jnp.bfloat16
EPS = 1e-6
HEAD = 64
LANES = 128
ATT_TK = 256
ATT_TQ = 512
ATT_ROWS = 128
ALIVE_LOG = -105.0
DEPTH = 2
VMEM_LIMIT = 56 * 1024 * 1024
MESH = pl.DeviceIdType.MESH
ANY = pl.BlockSpec(memory_space=pl.ANY)

ADAM_LR = 0.001
ADAM_B1 = 0.9
ADAM_B2 = 0.999
ADAM_EPS = 1e-08
ADAM_WD = 0.01
ADAM_STEP = 10


def _pcall(body, **kw):
    return pl.pallas_call(body, **kw)


def _cp(n_axes):
    return pltpu.CompilerParams(dimension_semantics=("arbitrary",) * n_axes, vmem_limit_bytes=VMEM_LIMIT)


def _tile(n, pref):
    return pref if n % pref == 0 else n


def _split_dot(a, b, passes):
    out = None
    rem = a
    for _ in range(passes):
        hi = rem.astype(BF16)
        t = jnp.dot(hi, b, preferred_element_type=F32)
        out = t if out is None else out + t
        rem = rem - hi.astype(F32)
    return out


def _group_mat():
    r = lax.broadcasted_iota(jnp.int32, (LANES, LANES), 0) // HEAD
    c = lax.broadcasted_iota(jnp.int32, (LANES, LANES), 1) // HEAD
    return jnp.where(r == c, 1.0 / HEAD, 0.0).astype(BF16)


def _group_mean(v, gm):
    return _split_dot(v, gm, 3)


def _sigmoid(z):
    return 1.0 / (1.0 + jnp.exp(-z))


def _dot_nt(a, b):
    return lax.dot_general(a, b, (((1,), (1,)), ((), ())), preferred_element_type=F32)


def _dot_tn(a, b):
    return lax.dot_general(a, b, (((0,), (0,)), ((), ())), preferred_element_type=F32)


def _cast_into_full(w, chip, axis, name):
    _, r, c = w.shape
    tr = _tile(r, 256)
    nb = r // tr
    full = (DEPTH, 4 * r, c) if axis == 0 else (DEPTH, r, 4 * c)

    def body(k_ref, w_ref, o_ref):
        o_ref[...] = w_ref[...].astype(BF16)

    out_map = (lambda l, i, k: (l, k[0] * nb + i, 0)) if axis == 0 else (lambda l, i, k: (l, i, k[0]))
    grid_spec = pltpu.PrefetchScalarGridSpec(
        num_scalar_prefetch=1, grid=(DEPTH, nb),
        in_specs=[pl.BlockSpec((1, tr, c), lambda l, i, k: (l, i, 0))],
        out_specs=pl.BlockSpec((1, tr, c), out_map))
    return _pcall(body, name=name, grid_spec=grid_spec, out_shape=jax.ShapeDtypeStruct(full, BF16),
                  compiler_params=_cp(2))(chip.reshape(1).astype(jnp.int32), w)


def _rms_fwd(x, g, name):
    s, d = x.shape
    tm = _tile(s, 512)

    def body(x_ref, g_ref, h_ref):
        xv = x_ref[...]
        r = lax.rsqrt(jnp.mean(xv * xv, axis=-1, keepdims=True) + EPS)
        h_ref[...] = (xv * r * g_ref[...]).astype(BF16)

    return _pcall(body, name=name, grid=(s // tm,),
                  in_specs=[pl.BlockSpec((tm, d), lambda m: (m, 0)), pl.BlockSpec((1, d), lambda m: (0, 0))],
                  out_specs=pl.BlockSpec((tm, d), lambda m: (m, 0)),
                  out_shape=jax.ShapeDtypeStruct((s, d), BF16), compiler_params=_cp(1))(x, g)


def _rms_bwd_rows(dh, xv, g):
    r = lax.rsqrt(jnp.mean(xv * xv, axis=-1, keepdims=True) + EPS)
    xn = xv * r
    dxn = dh * g
    dx = r * (dxn - xn * jnp.mean(dxn * xn, axis=-1, keepdims=True))
    return dx, dh * xn


def _colsum8(v):
    tm, d = v.shape
    return jnp.sum(v.reshape(tm // 8, 8, d), axis=0)


def _inproj(h, w, name):
    s, d = h.shape
    n = w.shape[1]
    sw = d // 2
    ns = n // sw
    tm = _tile(s, 512)

    def body(h_ref, w_ref, o_ref):
        o_ref[0] = jnp.dot(h_ref[...], w_ref[...], preferred_element_type=F32).astype(BF16)

    return _pcall(body, name=name, grid=(ns, s // tm),
                  in_specs=[pl.BlockSpec((tm, d), lambda j, m: (m, 0)), pl.BlockSpec((d, sw), lambda j, m: (0, j))],
                  out_specs=pl.BlockSpec((1, tm, sw), lambda j, m: (j, m, 0)),
                  out_shape=jax.ShapeDtypeStruct((ns, s, sw), BF16), compiler_params=_cp(2))(h, w)


def _softplus_parts(z):
    lm = jnp.minimum(-z, 0.0) - jnp.log(1.0 + jnp.exp(-jnp.abs(z)))
    return lm, lm + z


def _attn_tiles(s):
    tk = _tile(s, ATT_TK)
    tq = _tile(s, ATT_TQ)
    return tk, tq, tq // tk, min(ATT_ROWS, tq)


def _chain_live(d, r0, rows, tk):
    return d is None or r0 + rows - 1 > d * tk


def _chain_mask(d, r0, rows, tk):
    if d is None or r0 >= (d + 1) * tk:
        return None
    row = lax.broadcasted_iota(jnp.int32, (rows, tk), 0)
    col = lax.broadcasted_iota(jnp.int32, (rows, tk), 1)
    return col + d * tk < row + r0


def _any_alive(rsums):
    m = rsums[0]
    for r in rsums[1:]:
        m = jnp.maximum(m, r)
    return jnp.max((m > ALIVE_LOG).astype(jnp.int32))


def _attn_fwd(proj, name):
    _, s, sw = proj.shape
    nhp = sw // LANES
    tk, tq, nd, rows = _attn_tiles(s)
    nq = s // tq
    scale = 1.0 / math.sqrt(HEAD)

    def body(q_ref, k_ref, v_ref, o_ref, tl_ref, nw_ref, acc_ref):
        i = pl.program_id(1)
        tri = (lax.broadcasted_iota(jnp.int32, (tk, tk), 0) >
               lax.broadcasted_iota(jnp.int32, (tk, tk), 1)).astype(BF16)
        lane = lax.broadcasted_iota(jnp.int32, (tq, LANES), 1)
        q = q_ref[0] * jnp.asarray(scale, BF16)
        qms = [jnp.where((lane // HEAD) == h, q, jnp.zeros_like(q)) for h in range(2)]
        acc_ref[...] = jnp.zeros_like(acc_ref)
        chains = [(h, r0) for h in range(2) for r0 in range(0, tq, rows)]
        qparts = [qms[h][r0:r0 + rows] for h, r0 in chains]

        def tile(j, rsums, d):
            k0 = pl.multiple_of(j * tk, tk)
            kj = k_ref[0, pl.ds(k0, tk), :]
            vj = v_ref[0, pl.ds(k0, tk), :]
            live = [n for n, (h, r0) in enumerate(chains) if _chain_live(d, r0, rows, tk)]
            masks = {n: _chain_mask(d, chains[n][1], rows, tk) for n in live}
            zs = {n: _dot_nt(qparts[n], kj) for n in live}
            lms, lss, css = {}, {}, {}
            for n in live:
                lm, ls = _softplus_parts(zs[n])
                if masks[n] is not None:
                    lm = jnp.where(masks[n], lm, 0.0)
                lms[n], lss[n] = lm, ls
                css[n] = _split_dot(lm, tri, 2)
            out = list(rsums)
            for n in live:
                h, r0 = chains[n]
                a = jnp.exp(lss[n] + (rsums[n] + css[n]))
                if masks[n] is not None:
                    a = jnp.where(masks[n], a, 0.0)
                acc_ref[h, r0:r0 + rows, :] += jnp.dot(a.astype(BF16), vj, preferred_element_type=F32)
                out[n] = rsums[n] + jnp.sum(lms[n], axis=1, keepdims=True)
            return tuple(out)

        rsums = (jnp.zeros((rows, 1), F32),) * len(chains)
        for d in reversed(range(nd)):
            rsums = tile(i * nd + d, rsums, d)

        def walk(c):
            jj, rs, _ = c
            rs = tile(i * nd - 1 - jj, rs, None)
            return jj + 1, rs, _any_alive(rs)

        walked, rsums, _ = lax.while_loop(lambda c: (c[0] < i * nd) & (c[2] > 0), walk,
                                          (jnp.int32(0), rsums, _any_alive(rsums)))
        for n, (h, r0) in enumerate(chains):
            tl_ref[h, r0:r0 + rows, :] = rsums[n]
        nw_ref[0] = (jnp.zeros((8, LANES), jnp.int32) + walked).astype(F32)
        o_ref[...] = jnp.where(lane < HEAD, acc_ref[0], acc_ref[1]).astype(BF16)

    return _pcall(
        body, name=name, grid=(nhp, nq),
        in_specs=[pl.BlockSpec((1, tq, LANES), lambda hp, i: (4, i, hp)),
                  pl.BlockSpec((1, s, LANES), lambda hp, i: (5, 0, hp)),
                  pl.BlockSpec((1, s, LANES), lambda hp, i: (6, 0, hp))],
        out_specs=[pl.BlockSpec((tq, LANES), lambda hp, i: (i, hp)),
                   pl.BlockSpec((2, tq, 1), lambda hp, i: (hp, i, 0)),
                   pl.BlockSpec((1, 8, LANES), lambda hp, i: (hp * nq + i, 0, 0))],
        out_shape=[jax.ShapeDtypeStruct((s, sw), BF16), jax.ShapeDtypeStruct((2 * nhp, s, 1), F32),
                   jax.ShapeDtypeStruct((nhp * nq, 8, LANES), F32)],
        scratch_shapes=[pltpu.VMEM((2, tq, LANES), F32)],
        compiler_params=_cp(2))(proj, proj, proj)


def _conv_rows(cc_ref, ch_ref, w_ref, b_ref, r, tc):
    r0 = pl.multiple_of(r * tc, tc)
    u = cc_ref[0, pl.ds(r0, tc), :].astype(F32) * ch_ref[0, pl.ds(r0, tc), :].astype(F32)
    p0 = pl.multiple_of(jnp.maximum(r0 - 16, 0), 16)
    up = cc_ref[0, pl.ds(p0, 16), :].astype(F32) * ch_ref[0, pl.ds(p0, 16), :].astype(F32)
    up = up * (r > 0).astype(F32)
    prev1 = up[15:16, :]
    prev2 = up[14:15, :]
    rid = lax.broadcasted_iota(jnp.int32, u.shape, 0)
    s1 = jnp.where(rid == 0, prev1, pltpu.roll(u, 1, axis=0))
    s2 = jnp.where(rid == 0, prev2, jnp.where(rid == 1, prev1, pltpu.roll(u, 2, axis=0)))
    cv = b_ref[...] + s2 * w_ref[0:1, :] + s1 * w_ref[1:2, :] + u * w_ref[2:3, :]
    return r0, u, s1, s2, cv


def _mix_fwd(proj, ya, conv_w, conv_b, bg, name):
    _, s, sw = proj.shape
    nh = sw // LANES
    tc = _tile(s, 256)

    def body(cb_ref, cc_ref, ch_ref, cz_ref, ya_ref, az_ref, w_ref, b_ref, g_ref, y_ref):
        c = pl.program_id(0)
        gm = _group_mat()

        def finish(r0, yv, zg):
            n = yv * lax.rsqrt(_group_mean(yv * yv, gm) + EPS)
            y_ref[pl.ds(r0, tc), :] = (n * g_ref[...] * (zg * _sigmoid(zg))).astype(BF16)

        @pl.when(c < nh)
        def _():
            def step(r, carry):
                r0, _, _, _, cv = _conv_rows(cc_ref, ch_ref, w_ref, b_ref, r, tc)
                yc = cb_ref[0, pl.ds(r0, tc), :].astype(F32) * cv
                finish(r0, yc, cz_ref[0, pl.ds(r0, tc), :].astype(F32))
                return carry
            lax.fori_loop(0, s // tc, step, 0)

        @pl.when(c >= nh)
        def _():
            def step(r, carry):
                r0 = pl.multiple_of(r * tc, tc)
                finish(r0, ya_ref[pl.ds(r0, tc), :].astype(F32), az_ref[0, pl.ds(r0, tc), :].astype(F32))
                return carry
            lax.fori_loop(0, s // tc, step, 0)

    def sec(k):
        return pl.BlockSpec((1, s, LANES), lambda c: (k, 0, jnp.minimum(c, nh - 1)))

    return _pcall(
        body, name=name, grid=(2 * nh,),
        in_specs=[sec(0), sec(1), sec(2), sec(3),
                  pl.BlockSpec((s, LANES), lambda c: (0, jnp.maximum(c - nh, 0))),
                  pl.BlockSpec((1, s, LANES), lambda c: (7, 0, jnp.maximum(c - nh, 0))),
                  pl.BlockSpec((3, LANES), lambda c: (0, jnp.minimum(c, nh - 1))),
                  pl.BlockSpec((1, LANES), lambda c: (0, jnp.minimum(c, nh - 1))),
                  pl.BlockSpec((1, LANES), lambda c: (0, c))],
        out_specs=pl.BlockSpec((s, LANES), lambda c: (0, c)),
        out_shape=jax.ShapeDtypeStruct((s, 2 * sw), BF16), compiler_params=_cp(1),
    )(proj, proj, proj, proj, ya, proj, conv_w, conv_b, bg)


def _outproj(y, w, x, g, name):
    s, d = x.shape
    tm = _tile(s, 256)

    def body(y_ref, w_ref, x_ref, g_ref, x1_ref, hn_ref):
        x1 = x_ref[...] + jnp.dot(y_ref[...], w_ref[...], preferred_element_type=F32)
        x1_ref[...] = x1
        r = lax.rsqrt(jnp.mean(x1 * x1, axis=-1, keepdims=True) + EPS)
        hn_ref[...] = (x1 * r * g_ref[...]).astype(BF16)

    row = lambda m: (m, 0)
    fix = lambda m: (0, 0)
    return _pcall(body, name=name, grid=(s // tm,),
                  in_specs=[pl.BlockSpec((tm, d), row), pl.BlockSpec((d, d), fix), pl.BlockSpec((tm, d), row),
                            pl.BlockSpec((1, d), fix)],
                  out_specs=[pl.BlockSpec((tm, d), row), pl.BlockSpec((tm, d), row)],
                  out_shape=[jax.ShapeDtypeStruct((s, d), F32), jax.ShapeDtypeStruct((s, d), BF16)],
                  compiler_params=_cp(1))(y, w, x, g)


def _ple_fwd(hn, w_pg, b_pg, p, w_pe, x1, name):
    s, d = x1.shape
    pd = p.shape[1]
    tm = _tile(s, 256)

    def body(hn_ref, wg_ref, b_ref, p_ref, we_ref, x1_ref, x2_ref, gate_ref, e_ref):
        gate = _sigmoid(jnp.dot(hn_ref[...], wg_ref[...], preferred_element_type=F32) + b_ref[...])
        e = jnp.dot(p_ref[...].astype(BF16), we_ref[...], preferred_element_type=F32)
        x2_ref[...] = x1_ref[...] + gate * e
        gate_ref[...] = gate.astype(BF16)
        e_ref[...] = e.astype(BF16)

    row = lambda m: (m, 0)
    fix = lambda m: (0, 0)
    return _pcall(body, name=name, grid=(s // tm,),
                  in_specs=[pl.BlockSpec((tm, d), row), pl.BlockSpec((d, d), fix), pl.BlockSpec((1, d), fix),
                            pl.BlockSpec((tm, pd), row), pl.BlockSpec((pd, d), fix), pl.BlockSpec((tm, d), row)],
                  out_specs=[pl.BlockSpec((tm, d), row)] * 3,
                  out_shape=[jax.ShapeDtypeStruct((s, d), F32), jax.ShapeDtypeStruct((s, d), BF16),
                             jax.ShapeDtypeStruct((s, d), BF16)],
                  compiler_params=_cp(1))(hn, w_pg, b_pg, p, w_pe, x1)


def _loss_head(x, tgt, g, name):
    s, d = x.shape
    tm = _tile(s, 256)

    def body(x_ref, t_ref, g_ref, l_ref, dx_ref, dg_ref):
        m = pl.program_id(0)

        @pl.when(m == 0)
        def _():
            l_ref[...] = jnp.zeros_like(l_ref)
            dg_ref[...] = jnp.zeros_like(dg_ref)

        xv = x_ref[...]
        gv = g_ref[...]
        r = lax.rsqrt(jnp.mean(xv * xv, axis=-1, keepdims=True) + EPS)
        xn = xv * r
        err = xn * gv - t_ref[...]
        l_ref[...] += jnp.sum(err * err)
        dy = err * (1.0 / d)
        dxn = dy * gv
        dx_ref[...] = r * (dxn - xn * jnp.mean(dxn * xn, axis=-1, keepdims=True))
        dg_ref[...] += _colsum8(dy * xn)

    row = lambda m: (m, 0)
    fix = lambda m: (0, 0)
    return _pcall(body, name=name, grid=(s // tm,),
                  in_specs=[pl.BlockSpec((tm, d), row), pl.BlockSpec((tm, d), row), pl.BlockSpec((1, d), fix)],
                  out_specs=[pl.BlockSpec((8, LANES), fix), pl.BlockSpec((tm, d), row), pl.BlockSpec((8, d), fix)],
                  out_shape=[jax.ShapeDtypeStruct((8, LANES), F32), jax.ShapeDtypeStruct((s, d), F32),
                             jax.ShapeDtypeStruct((8, d), F32)],
                  compiler_params=_cp(1))(x, tgt, g)


def _ple_bwd(dx2, gate, e, x1, w_pg, g_ple, w_out, name):
    s, d = dx2.shape
    tm = _tile(s, 256)

    def body(dx2_ref, gate_ref, e_ref, x1_ref, wg_ref, g_ref, wo_ref,
             du_ref, de_ref, dx1_ref, dy_ref, db_ref, dg_ref):
        m = pl.program_id(0)

        @pl.when(m == 0)
        def _():
            db_ref[...] = jnp.zeros_like(db_ref)
            dg_ref[...] = jnp.zeros_like(dg_ref)

        dx2v = dx2_ref[...]
        gate = gate_ref[...].astype(F32)
        du = dx2v * e_ref[...].astype(F32) * gate * (1.0 - gate)
        de_ref[...] = (dx2v * gate).astype(BF16)
        dub = du.astype(BF16)
        du_ref[...] = dub
        db_ref[...] += _colsum8(du)
        dhn = _dot_nt(dub, wg_ref[...])
        dxr, dgr = _rms_bwd_rows(dhn, x1_ref[...], g_ref[...])
        dx1 = dx2v + dxr
        dx1_ref[...] = dx1
        dg_ref[...] += _colsum8(dgr)
        dy_ref[...] = _dot_nt(dx1.astype(BF16), wo_ref[...]).astype(BF16)

    row = lambda m: (m, 0)
    fix = lambda m: (0, 0)
    t = pl.BlockSpec((tm, d), row)
    return _pcall(body, name=name, grid=(s // tm,),
                  in_specs=[t, t, t, t, pl.BlockSpec((d, d), fix), pl.BlockSpec((1, d), fix), pl.BlockSpec((d, d), fix)],
                  out_specs=[t, t, t, t, pl.BlockSpec((8, d), fix), pl.BlockSpec((8, d), fix)],
                  out_shape=[jax.ShapeDtypeStruct((s, d), BF16), jax.ShapeDtypeStruct((s, d), BF16),
                             jax.ShapeDtypeStruct((s, d), F32), jax.ShapeDtypeStruct((s, d), BF16),
                             jax.ShapeDtypeStruct((8, d), F32), jax.ShapeDtypeStruct((8, d), F32)],
                  compiler_params=_cp(1))(dx2, gate, e, x1, w_pg, g_ple, w_out)


def _mm_tn(a, b, stack, layer, name, b_sections=False):
    s, ka = a.shape
    if b_sections:
        ns, _, tn = b.shape
        n = ns * tn
    else:
        n = b.shape[1]
        tn = _tile(n, 1024)
        ns = n // tn
    tk = _tile(s, 512)
    nk = s // tk

    def body(*refs):
        a_ref, b_ref = refs[0], refs[1]
        o_ref, acc_ref = refs[-2], refs[-1]
        k = pl.program_id(1)

        @pl.when(k == 0)
        def _():
            acc_ref[...] = jnp.zeros_like(acc_ref)

        bv = b_ref[0] if b_sections else b_ref[...]
        acc_ref[...] += _dot_tn(a_ref[...].astype(BF16), bv.astype(BF16))

        @pl.when(k == nk - 1)
        def _():
            o_ref[0] = acc_ref[...]

    b_spec = (pl.BlockSpec((1, tk, tn), lambda j, k: (j, k, 0)) if b_sections
              else pl.BlockSpec((tk, tn), lambda j, k: (k, j)))
    in_specs = [pl.BlockSpec((tk, ka), lambda j, k: (k, 0)), b_spec]
    args = [a, b]
    aliases = {}
    if stack is not None:
        in_specs.append(ANY)
        args.append(stack)
        aliases = {2: 0}
    return _pcall(body, name=name, grid=(ns, nk), in_specs=in_specs,
                  out_specs=pl.BlockSpec((1, ka, tn), lambda j, k: (layer, 0, j)),
                  out_shape=jax.ShapeDtypeStruct((DEPTH, ka, n), F32),
                  scratch_shapes=[pltpu.VMEM((ka, tn), F32)], input_output_aliases=aliases,
                  compiler_params=_cp(2))(*args)


def _norm_gate_bwd(dy, yv, zg, g, gm):
    r = lax.rsqrt(_group_mean(yv * yv, gm) + EPS)
    n = yv * r
    sg = _sigmoid(zg)
    sil = zg * sg
    dzg = dy * n * g * (sg * (1.0 + zg * (1.0 - sg)))
    dn = dy * g * sil
    dyv = r * (dn - n * _group_mean(dn * n, gm))
    return dyv, dzg, dy * n * sil


def _convmix_bwd(dy, proj, conv_w, conv_b, bg, name):
    _, s, sw = proj.shape
    nh = sw // LANES
    tc = _tile(s, 256)
    nr = s // tc

    def body(dy_ref, cb_ref, cc_ref, ch_ref, cz_ref, w_ref, b_ref, g_ref,
             dp_ref, dw_ref, db_ref, dg_ref, dcv_ref):
        gm = _group_mat()
        dcv_ref[pl.ds(s, 8), :] = jnp.zeros((8, LANES), F32)

        def pass1(r, carry):
            dw0, dw1, dw2, db, dg = carry
            r0, u, s1, s2, cv = _conv_rows(cc_ref, ch_ref, w_ref, b_ref, r, tc)
            cb = cb_ref[0, pl.ds(r0, tc), :].astype(F32)
            dyc, dcz, dgr = _norm_gate_bwd(dy_ref[pl.ds(r0, tc), :].astype(F32), cb * cv,
                                           cz_ref[0, pl.ds(r0, tc), :].astype(F32), g_ref[...], gm)
            dp_ref[0, pl.ds(r0, tc), :] = (dyc * cv).astype(BF16)
            dp_ref[3, pl.ds(r0, tc), :] = dcz.astype(BF16)
            dcv = dyc * cb
            dcv_ref[pl.ds(r0, tc), :] = dcv
            return (dw0 + _colsum8(dcv * s2), dw1 + _colsum8(dcv * s1), dw2 + _colsum8(dcv * u),
                    db + _colsum8(dcv), dg + _colsum8(dgr))

        z8 = jnp.zeros((8, LANES), F32)
        dw0, dw1, dw2, db, dg = lax.fori_loop(0, nr, pass1, (z8, z8, z8, z8, z8))
        dw_ref[0] = dw0
        dw_ref[1] = dw1
        dw_ref[2] = dw2
        db_ref[...] = db
        dg_ref[...] = dg

        def pass2(r, carry):
            r0 = pl.multiple_of(r * tc, tc)
            dcv = dcv_ref[pl.ds(r0, tc), :]
            nxt = dcv_ref[pl.ds(pl.multiple_of(r0 + tc, 8), 8), :]
            rid = lax.broadcasted_iota(jnp.int32, dcv.shape, 0)
            n1 = jnp.where(rid == tc - 1, nxt[0:1, :], pltpu.roll(dcv, tc - 1, axis=0))
            n2 = jnp.where(rid == tc - 1, nxt[1:2, :],
                           jnp.where(rid == tc - 2, nxt[0:1, :], pltpu.roll(dcv, tc - 2, axis=0)))
            du = dcv * w_ref[2:3, :] + n1 * w_ref[1:2, :] + n2 * w_ref[0:1, :]
            dp_ref[1, pl.ds(r0, tc), :] = (du * ch_ref[0, pl.ds(r0, tc), :].astype(F32)).astype(BF16)
            dp_ref[2, pl.ds(r0, tc), :] = (du * cc_ref[0, pl.ds(r0, tc), :].astype(F32)).astype(BF16)
            return carry

        lax.fori_loop(0, nr, pass2, 0)

    def sec(k):
        return pl.BlockSpec((1, s, LANES), lambda c: (k, 0, c))

    col = lambda c: (0, c)
    return _pcall(
        body, name=name, grid=(nh,),
        in_specs=[pl.BlockSpec((s, LANES), col), sec(0), sec(1), sec(2), sec(3),
                  pl.BlockSpec((3, LANES), col), pl.BlockSpec((1, LANES), col), pl.BlockSpec((1, LANES), col)],
        out_specs=[pl.BlockSpec((4, s, LANES), lambda c: (0, 0, c)), pl.BlockSpec((3, 8, LANES), lambda c: (0, 0, c)),
                   pl.BlockSpec((8, LANES), col), pl.BlockSpec((8, LANES), col)],
        out_shape=[jax.ShapeDtypeStruct((8, s, sw), BF16), jax.ShapeDtypeStruct((3, 8, sw), F32),
                   jax.ShapeDtypeStruct((8, sw), F32), jax.ShapeDtypeStruct((8, sw), F32)],
        scratch_shapes=[pltpu.VMEM((s + 8, LANES), F32)], compiler_params=_cp(1),
    )(dy, proj, proj, proj, proj, conv_w, conv_b, bg)


def _attn_bwd(proj, dy, ya, tl, walked, bg, buf, name):
    _, s, sw = proj.shape
    nhp = sw // LANES
    tk, t, nd, rows_c = _attn_tiles(s)
    nq = s // t
    scale = 1.0 / math.sqrt(HEAD)

    def body(q_ref, k_ref, v_ref, az_ref, dy_ref, ya_ref, tl_ref, nw_ref, g_ref, buf_ref, out_ref, dg_ref,
             dka_ref, dva_ref, dqa_ref):
        step = pl.program_id(1)
        i = nq - 1 - step

        @pl.when(step == 0)
        def _():
            dka_ref[...] = jnp.zeros_like(dka_ref)
            dva_ref[...] = jnp.zeros_like(dva_ref)
            dg_ref[...] = jnp.zeros_like(dg_ref)

        dyv, dzg, dgr = _norm_gate_bwd(dy_ref[...].astype(F32), ya_ref[...].astype(F32), az_ref[0].astype(F32),
                                       g_ref[...], _group_mat())
        out_ref[3] = dzg.astype(BF16)
        dg_ref[...] += _colsum8(dgr)

        tri = (lax.broadcasted_iota(jnp.int32, (tk, tk), 0) <=
               lax.broadcasted_iota(jnp.int32, (tk, tk), 1)).astype(BF16)
        lane = lax.broadcasted_iota(jnp.int32, (t, LANES), 1)
        q = q_ref[0] * jnp.asarray(scale, BF16)
        do = dyv.astype(BF16)
        qms = [jnp.where((lane // HEAD) == h, q, jnp.zeros_like(q)) for h in range(2)]
        doms = [jnp.where((lane // HEAD) == h, do, jnp.zeros_like(do)) for h in range(2)]
        dqa_ref[...] = jnp.zeros_like(dqa_ref)
        chains = [(h, r0) for h in range(2) for r0 in range(0, t, rows_c)]
        qparts = [qms[h][r0:r0 + rows_c] for h, r0 in chains]
        doparts = [doms[h][r0:r0 + rows_c] for h, r0 in chains]
        tots = [tl_ref[h, r0:r0 + rows_c, :] for h, r0 in chains]

        def tile(j, carry, d):
            k0 = pl.multiple_of(j * tk, tk)
            kj = k_ref[0, pl.ds(k0, tk), :]
            vj = v_ref[0, pl.ds(k0, tk), :]
            live = [n for n, (h, r0) in enumerate(chains) if _chain_live(d, r0, rows_c, tk)]
            masks = {n: _chain_mask(d, chains[n][1], rows_c, tk) for n in live}
            zs = {n: _dot_nt(qparts[n], kj) for n in live}
            das = {n: _dot_nt(doparts[n], vj) for n in live}
            lms, lss, cls = {}, {}, {}
            for n in live:
                lm, ls = _softplus_parts(zs[n])
                if masks[n] is not None:
                    lm = jnp.where(masks[n], lm, 0.0)
                lms[n], lss[n] = lm, ls
                cls[n] = _split_dot(lm, tri, 2)
            abs_, gs, cgs = {}, {}, {}
            for n in live:
                a = jnp.exp(lss[n] + (tots[n] - carry[n][0] - cls[n]))
                if masks[n] is not None:
                    a = jnp.where(masks[n], a, 0.0)
                gs[n] = a * das[n]
                abs_[n] = a.astype(BF16)
                cgs[n] = _split_dot(gs[n], tri, 2)
            out = list(carry)
            dk = None
            dv = None
            for n in live:
                h, r0 = chains[n]
                psum, gsum = carry[n]
                dz = gs[n] - jnp.exp(lss[n]) * (gsum + cgs[n])
                if masks[n] is not None:
                    dz = jnp.where(masks[n], dz, 0.0)
                dz = dz.astype(BF16)
                dqa_ref[h, r0:r0 + rows_c, :] += jnp.dot(dz, kj, preferred_element_type=F32)
                dkh = _dot_tn(dz, qparts[n])
                dvh = _dot_tn(abs_[n], doparts[n])
                dk = dkh if dk is None else dk + dkh
                dv = dvh if dv is None else dv + dvh
                out[n] = (psum + jnp.sum(lms[n], axis=1, keepdims=True), gsum + jnp.sum(gs[n], axis=1, keepdims=True))
            dka_ref[pl.ds(k0, tk), :] += dk
            dva_ref[pl.ds(k0, tk), :] += dv
            return tuple(out)

        z1 = jnp.zeros((rows_c, 1), F32)
        first = i * nd - jnp.clip(jnp.max(nw_ref[0].astype(jnp.int32)), 0, i * nd)
        carry = lax.fori_loop(first, i * nd, lambda j, c: tile(j, c, None), ((z1, z1),) * len(chains))
        for d in range(nd):
            carry = tile(i * nd + d, carry, d)
        out_ref[0] = (jnp.where(lane < HEAD, dqa_ref[0], dqa_ref[1]) * scale).astype(BF16)
        own = pl.multiple_of(i * t, t)
        out_ref[1] = dka_ref[pl.ds(own, t), :].astype(BF16)
        out_ref[2] = dva_ref[pl.ds(own, t), :].astype(BF16)

    def rows(sec):
        return pl.BlockSpec((1, t, LANES), lambda hp, st: (sec, nq - 1 - st, hp))

    def whole(sec):
        return pl.BlockSpec((1, s, LANES), lambda hp, st: (sec, 0, hp))

    return _pcall(
        body, name=name, grid=(nhp, nq),
        in_specs=[rows(4), whole(5), whole(6), rows(7),
                  pl.BlockSpec((t, LANES), lambda hp, st: (nq - 1 - st, hp + nhp)),
                  pl.BlockSpec((t, LANES), lambda hp, st: (nq - 1 - st, hp)),
                  pl.BlockSpec((2, t, 1), lambda hp, st: (hp, nq - 1 - st, 0)),
                  pl.BlockSpec((1, 8, LANES), lambda hp, st: (hp * nq + nq - 1 - st, 0, 0)),
                  pl.BlockSpec((1, LANES), lambda hp, st: (0, hp + nhp)), ANY],
        out_specs=[pl.BlockSpec((4, t, LANES), lambda hp, st: (1, nq - 1 - st, hp)),
                   pl.BlockSpec((8, LANES), lambda hp, st: (0, hp))],
        out_shape=[jax.ShapeDtypeStruct(buf.shape, buf.dtype), jax.ShapeDtypeStruct((8, sw), F32)],
        input_output_aliases={9: 0},
        scratch_shapes=[pltpu.VMEM((s, LANES), F32), pltpu.VMEM((s, LANES), F32), pltpu.VMEM((2, t, LANES), F32)],
        compiler_params=_cp(2))(proj, proj, proj, proj, dy, ya, tl, walked, bg, buf)


def _inproj_bwd(dproj, w, x, g, dx1, name):
    ns, s, sw = dproj.shape
    d = x.shape[1]
    tm = _tile(s, 256)

    def body(dp_ref, w_ref, x_ref, g_ref, dx1_ref, dx_ref, dg_ref):
        @pl.when(pl.program_id(0) == 0)
        def _():
            dg_ref[...] = jnp.zeros_like(dg_ref)

        dh = _dot_nt(dp_ref[0], w_ref[:, 0:sw])
        for k in range(1, ns):
            dh = dh + _dot_nt(dp_ref[k], w_ref[:, k * sw:(k + 1) * sw])
        dxr, dgr = _rms_bwd_rows(dh, x_ref[...], g_ref[...])
        dx_ref[...] = dx1_ref[...] + dxr
        dg_ref[...] += _colsum8(dgr)

    row = lambda m: (m, 0)
    fix = lambda m: (0, 0)
    return _pcall(body, name=name, grid=(s // tm,),
                  in_specs=[pl.BlockSpec((ns, tm, sw), lambda m: (0, m, 0)), pl.BlockSpec((d, ns * sw), fix),
                            pl.BlockSpec((tm, d), row), pl.BlockSpec((1, d), fix), pl.BlockSpec((tm, d), row)],
                  out_specs=[pl.BlockSpec((tm, d), row), pl.BlockSpec((8, d), fix)],
                  out_shape=[jax.ShapeDtypeStruct((s, d), F32), jax.ShapeDtypeStruct((8, d), F32)],
                  compiler_params=_cp(1))(dproj, w, x, g, dx1)


def _adamw(w, g, m, v, name):
    r, c = w.shape
    tr = _tile(r, 256)
    c1 = 1.0 - ADAM_B1 ** ADAM_STEP
    c2 = 1.0 - ADAM_B2 ** ADAM_STEP

    def body(w_ref, g_ref, m_ref, v_ref, d_ref, mo_ref, vo_ref):
        gv = g_ref[...]
        mn = ADAM_B1 * m_ref[...] + (1.0 - ADAM_B1) * gv
        vn = ADAM_B2 * v_ref[...] + (1.0 - ADAM_B2) * (gv * gv)
        d_ref[...] = -ADAM_LR * ((mn / c1) / (jnp.sqrt(vn / c2) + ADAM_EPS) + ADAM_WD * w_ref[...])
        mo_ref[...] = mn
        vo_ref[...] = vn

    t = pl.BlockSpec((tr, c), lambda i: (i, 0))
    return _pcall(body, name=name, grid=(r // tr,), in_specs=[t] * 4, out_specs=[t] * 3,
                  out_shape=[jax.ShapeDtypeStruct((r, c), F32)] * 3, compiler_params=_cp(1))(w, g, m, v)


def _add_layer(stack, other, layer, name):
    _, r, c = stack.shape
    tr = _tile(r, 256)

    def body(l_ref, s_ref, o_ref, out_ref):
        out_ref[...] = s_ref[0] + o_ref[...]

    grid_spec = pltpu.PrefetchScalarGridSpec(
        num_scalar_prefetch=1, grid=(r // tr,),
        in_specs=[pl.BlockSpec((1, tr, c), lambda i, l: (l[0], i, 0)), pl.BlockSpec((tr, c), lambda i, l: (i, 0))],
        out_specs=pl.BlockSpec((tr, c), lambda i, l: (i, 0)))
    return _pcall(body, name=name, grid_spec=grid_spec, out_shape=jax.ShapeDtypeStruct((r, c), F32),
                  compiler_params=_cp(1))(layer.reshape(1).astype(jnp.int32), stack, other)


def _sum_shard(full, parts, chip, layer, axis, name):
    _, r, c = parts.shape
    tr = _tile(r, 256)

    def body(k_ref, f_ref, p_ref, out_ref):
        out_ref[0] = ((f_ref[...] + p_ref[0]) + p_ref[1]) + p_ref[2]

    if axis == 1:
        f_spec = pl.BlockSpec((tr, c), lambda i, k: (i, k[0]))
    else:
        nb = r // tr
        f_spec = pl.BlockSpec((tr, c), lambda i, k: (k[0] * nb + i, 0))
    grid_spec = pltpu.PrefetchScalarGridSpec(
        num_scalar_prefetch=1, grid=(r // tr,),
        in_specs=[f_spec, pl.BlockSpec((3, tr, c), lambda i, k: (0, i, 0))],
        out_specs=pl.BlockSpec((1, tr, c), lambda i, k: (k[1], i, 0)))
    return _pcall(body, name=name, grid_spec=grid_spec, out_shape=jax.ShapeDtypeStruct((DEPTH, r, c), F32),
                  compiler_params=_cp(1))(jnp.stack([chip, layer]).astype(jnp.int32), full, parts)


def _sum_slots(slots, name):
    n = slots.shape[0]

    def body(s_ref, o_ref):
        acc = s_ref[0]
        for i in range(1, n):
            acc = acc + s_ref[i]
        o_ref[...] = acc

    return _pcall(body, name=name, out_shape=jax.ShapeDtypeStruct(slots.shape[1:], F32))(slots)


def _place():
    return lax.axis_index("x"), lax.axis_index("y"), lax.axis_index("c")


def _shard_view(ref, axis, chip, size):
    if axis == 0:
        return ref.at[pl.ds(chip * size, size), :]
    return ref.at[:, pl.ds(chip * size, size)]


SHARD_AXES = (1, 0, 0, 1)


def _gather_weights(fulls):
    n = len(fulls)
    sizes = [f.shape[1 + ax] // 4 for f, ax in zip(fulls, SHARD_AXES)]

    def body(*refs):
        ins, outs = refs[:n], refs[n:2 * n]
        ssem, rsem = refs[2 * n:]
        x, y, c = _place()
        me = 2 * x + y
        chips = [(1 - x, y), (x, 1 - y), (1 - x, 1 - y)]

        def piece(a, layer, chip, of=outs):
            return _shard_view(of[a].at[layer], SHARD_AXES[a], chip, sizes[a])

        sends = []
        for a in range(n):
            for j, (cx, cy) in enumerate(chips):
                cp = pltpu.make_async_remote_copy(
                    src_ref=piece(a, c, me, ins), dst_ref=piece(a, c, me), send_sem=ssem.at[a, j],
                    recv_sem=rsem.at[a, j], device_id=(cx, cy, c), device_id_type=MESH)
                cp.start()
                sends.append(cp)
        for a in range(n):
            for j, (cx, cy) in enumerate(chips):
                got = piece(a, c, 2 * cx + cy)
                pltpu.make_async_remote_copy(
                    src_ref=got, dst_ref=got, send_sem=ssem.at[a, j], recv_sem=rsem.at[a, j],
                    device_id=(cx, cy, c), device_id_type=MESH).wait_recv()
                cp = pltpu.make_async_remote_copy(
                    src_ref=got, dst_ref=got, send_sem=ssem.at[a, 3 + j], recv_sem=rsem.at[a, 3 + j],
                    device_id=(x, y, 1 - c), device_id_type=MESH)
                cp.start()
                sends.append(cp)
        for a in range(n):
            for j, (cx, cy) in enumerate(chips):
                got = piece(a, 1 - c, 2 * cx + cy)
                pltpu.make_async_remote_copy(
                    src_ref=got, dst_ref=got, send_sem=ssem.at[a, 3 + j], recv_sem=rsem.at[a, 3 + j],
                    device_id=(x, y, 1 - c), device_id_type=MESH).wait_recv()
        for cp in sends:
            cp.wait_send()

    return _pcall(body, name="gather_weights", in_specs=[ANY] * n, out_specs=[ANY] * n,
                  out_shape=[jax.ShapeDtypeStruct(f.shape, f.dtype) for f in fulls],
                  input_output_aliases={a: a for a in range(n)},
                  scratch_shapes=[pltpu.SemaphoreType.DMA((n, 6)), pltpu.SemaphoreType.DMA((n, 6))])(*fulls)


def _swap_layers(stacks):
    n = len(stacks)

    def body(*refs):
        srcs, outs = refs[:n], refs[n:2 * n]
        ssem, rsem = refs[2 * n:]
        x, y, c = _place()
        cps = [pltpu.make_async_remote_copy(src_ref=srcs[a].at[1 - c], dst_ref=outs[a], send_sem=ssem.at[a],
                                            recv_sem=rsem.at[a], device_id=(x, y, 1 - c), device_id_type=MESH)
               for a in range(n)]
        for cp in cps:
            cp.start()
        for cp in cps:
            cp.wait()

    return _pcall(body, name="swap_layers", in_specs=[ANY] * n, out_specs=[ANY] * n,
                  out_shape=[jax.ShapeDtypeStruct(st.shape[1:], st.dtype) for st in stacks],
                  scratch_shapes=[pltpu.SemaphoreType.DMA((n,)), pltpu.SemaphoreType.DMA((n,))])(*stacks)


def _scatter_shards(fulls):
    n = len(fulls)
    shard_shapes = []
    for f, ax in zip(fulls, SHARD_AXES):
        sh = list(f.shape)
        sh[ax] //= 4
        shard_shapes.append(tuple(sh))

    def body(*refs):
        srcs, outs = refs[:n], refs[n:2 * n]
        ssem, rsem = refs[2 * n:]
        x, y, c = _place()
        chips = [(1 - x, y), (x, 1 - y), (1 - x, 1 - y)]
        cps = []
        for a in range(n):
            for j, (cx, cy) in enumerate(chips):
                src = _shard_view(srcs[a], SHARD_AXES[a], 2 * cx + cy, shard_shapes[a][SHARD_AXES[a]])
                cps.append(pltpu.make_async_remote_copy(
                    src_ref=src, dst_ref=outs[a].at[j], send_sem=ssem.at[a, j], recv_sem=rsem.at[a, j],
                    device_id=(cx, cy, c), device_id_type=MESH))
        for cp in cps:
            cp.start()
        for cp in cps:
            cp.wait()

    return _pcall(body, name="scatter_shards", in_specs=[ANY] * n, out_specs=[ANY] * n,
                  out_shape=[jax.ShapeDtypeStruct((3,) + sh, F32) for sh in shard_shapes],
                  scratch_shapes=[pltpu.SemaphoreType.DMA((n, 3)), pltpu.SemaphoreType.DMA((n, 3))])(*fulls)


def _pair_layers(stacks):
    n = len(stacks)

    def body(*refs):
        ins, outs = refs[:n], refs[n:2 * n]
        ssem, rsem = refs[2 * n:]
        x, y, c = _place()
        cps = [pltpu.make_async_remote_copy(src_ref=ins[a].at[c], dst_ref=outs[a].at[c], send_sem=ssem.at[a],
                                            recv_sem=rsem.at[a], device_id=(x, y, 1 - c), device_id_type=MESH)
               for a in range(n)]
        for cp in cps:
            cp.start()
        for a in range(n):
            got = outs[a].at[1 - c]
            pltpu.make_async_remote_copy(src_ref=got, dst_ref=got, send_sem=ssem.at[a], recv_sem=rsem.at[a],
                                         device_id=(x, y, 1 - c), device_id_type=MESH).wait_recv()
        for cp in cps:
            cp.wait_send()

    return _pcall(body, name="pair_layers", in_specs=[ANY] * n, out_specs=[ANY] * n,
                  out_shape=[jax.ShapeDtypeStruct(st.shape, st.dtype) for st in stacks],
                  input_output_aliases={a: a for a in range(n)},
                  scratch_shapes=[pltpu.SemaphoreType.DMA((n,)), pltpu.SemaphoreType.DMA((n,))])(*stacks)


def _exchange_small(pack, name):
    nd = 8

    def body(p_ref, o_ref, ssem, rsem):
        x, y, c = _place()
        me = 4 * x + 2 * y + c
        o_ref[me] = p_ref[...]
        cps = []
        for j in range(1, nd):
            px, py, pc = x ^ (j >> 2), y ^ ((j >> 1) & 1), c ^ (j & 1)
            cps.append(pltpu.make_async_remote_copy(
                src_ref=p_ref, dst_ref=o_ref.at[me], send_sem=ssem.at[j - 1], recv_sem=rsem.at[j - 1],
                device_id=(px, py, pc), device_id_type=MESH))
        for cp in cps:
            cp.start()
        for j in range(1, nd):
            peer = me ^ j
            got = o_ref.at[peer]
            pltpu.make_async_remote_copy(src_ref=got, dst_ref=got, send_sem=ssem.at[j - 1], recv_sem=rsem.at[j - 1],
                                         device_id=(x, y, c), device_id_type=MESH).wait_recv()
        for cp in cps:
            cp.wait_send()

    vm = pl.BlockSpec(memory_space=pltpu.VMEM)
    return _pcall(body, name=name, in_specs=[vm], out_specs=vm,
                  out_shape=jax.ShapeDtypeStruct((nd,) + pack.shape, pack.dtype),
                  scratch_shapes=[pltpu.SemaphoreType.DMA((nd - 1,)), pltpu.SemaphoreType.DMA((nd - 1,))])(pack)


def _row(v):
    return v.reshape(1, -1)


def _local_step(x, p, tgt, norm_g, conv_w, conv_b, branch_g, ple_norm_g, b_pg, final_g, w_in, w_out, w_pg, w_pe):
    saved = []
    xl = x
    for l in range(DEPTH):
        h = _rms_fwd(xl, _row(norm_g[l]), f"rms_fwd_{l}")
        proj = _inproj(h, w_in[l], f"inproj_{l}")
        ya, tl, walked = _attn_fwd(proj, f"attn_fwd_{l}")
        y = _mix_fwd(proj, ya, conv_w[l], _row(conv_b[l]), _row(branch_g[l]), f"mix_fwd_{l}")
        x1, hn = _outproj(y, w_out[l], xl, _row(ple_norm_g[l]), f"outproj_{l}")
        x2, gate, e = _ple_fwd(hn, w_pg[l], _row(b_pg[l]), p[l], w_pe[l], x1, f"ple_fwd_{l}")
        saved.append((xl, h, proj, ya, tl, walked, y, x1, hn, gate, e))
        xl = x2

    sq, dx, d_final = _loss_head(xl, tgt, _row(final_g), "loss_head")

    g_in = g_out = g_pg = g_pe = None
    small = {k: [None] * DEPTH for k in ("norm_g", "conv_w", "conv_b", "branch_g", "ple_norm_g", "b_pg")}
    for l in reversed(range(DEPTH)):
        xl, h, proj, ya, tl, walked, y, x1, hn, gate, e = saved[l]
        du, de, dx1, dy, db_pg, d_ple = _ple_bwd(dx, gate, e, x1, w_pg[l], _row(ple_norm_g[l]), w_out[l], f"ple_bwd_{l}")
        g_pg = _mm_tn(hn, du, g_pg, l, f"grad_w_pg_{l}")
        g_pe = _mm_tn(p[l], de, g_pe, l, f"grad_w_pe_{l}")
        g_out = _mm_tn(y, dx1, g_out, l, f"grad_w_out_{l}")
        dpc, d_cw, d_cb, d_bg_c = _convmix_bwd(dy, proj, conv_w[l], _row(conv_b[l]), _row(branch_g[l]), f"convmix_bwd_{l}")
        dproj, d_bg_a = _attn_bwd(proj, dy, ya, tl, walked, _row(branch_g[l]), dpc, f"attn_bwd_{l}")
        g_in = _mm_tn(h, dproj, g_in, l, f"grad_w_in_{l}", b_sections=True)
        dx, d_norm = _inproj_bwd(dproj, w_in[l], xl, _row(norm_g[l]), dx1, f"inproj_bwd_{l}")
        small["norm_g"][l] = jnp.sum(d_norm, axis=0)
        small["conv_w"][l] = jnp.sum(d_cw, axis=1)
        small["conv_b"][l] = jnp.sum(d_cb, axis=0)
        small["branch_g"][l] = jnp.concatenate([jnp.sum(d_bg_c, axis=0), jnp.sum(d_bg_a, axis=0)])
        small["ple_norm_g"][l] = jnp.sum(d_ple, axis=0)
        small["b_pg"][l] = jnp.sum(db_pg, axis=0)
    small = {k: jnp.stack(v) for k, v in small.items()}
    small["final_g"] = jnp.sum(d_final, axis=0)
    return sq[0, 0], dx, (g_in, g_out, g_pg, g_pe), small


SMALL_ORDER = ("norm_g", "conv_w", "conv_b", "branch_g", "ple_norm_g", "b_pg", "final_g")


def _pack(parts, width):
    flat = jnp.concatenate([v.reshape(-1) for v in parts])
    rows = -(-flat.shape[0] // width)
    rows = -(-rows // 8) * 8
    return jnp.pad(flat, (0, rows * width - flat.shape[0])).reshape(rows, width)


def _unpack(packed, like):
    flat = packed.reshape(-1)
    out, off = [], 0
    for v in like:
        out.append(flat[off:off + v.size].reshape(v.shape))
        off += v.size
    return out


def kernel(x, p, norm_g, w_in, conv_w, conv_b, branch_g, w_out, ple_norm_g, w_pg, b_pg, w_pe, final_g, loss_target, m_norm_g, m_w_in, m_conv_w, m_conv_b, m_branch_g, m_w_out, m_ple_norm_g, m_w_pg, m_b_pg, m_w_pe, m_final_g, v_norm_g, v_w_in, v_conv_w, v_conv_b, v_branch_g, v_w_out, v_ple_norm_g, v_w_pg, v_b_pg, v_w_pe, v_final_g):
    ix, iy, ic = _place()
    chip = 2 * ix + iy
    d = x.shape[-1]

    big_w = (w_in, w_out, w_pg, w_pe)
    own = [_cast_into_full(w, chip, ax, f"cast_{i}") for i, (w, ax) in enumerate(zip(big_w, SHARD_AXES))]
    full_in, full_out, full_pg, full_pe = _gather_weights(own)
    cw_shard = conv_w.shape[-1]
    cw_slots = _exchange_small(_pack([conv_w], LANES), "exchange_conv_w")
    conv_full = jnp.concatenate([_unpack(cw_slots[2 * k], [conv_w])[0] for k in range(4)], axis=-1)

    sq, dx, big_g, small_g = _local_step(
        x[0], p[:, 0], loss_target[0], norm_g, conv_full, conv_b, branch_g, ple_norm_g, b_pg, final_g,
        full_in, full_out, full_pg, full_pe)

    from_sibling = _swap_layers(big_g)
    chip_sums = [_add_layer(g, o, ic, f"add_layer_{i}") for i, (g, o) in enumerate(zip(big_g, from_sibling))]
    partials = _scatter_shards(chip_sums)
    reduced = [_sum_shard(f, pr, chip, ic, ax, f"sum_shard_{i}")
               for i, (f, pr, ax) in enumerate(zip(chip_sums, partials, SHARD_AXES))]
    g_big = _pair_layers(reduced)

    parts = [small_g[k] for k in SMALL_ORDER] + [sq.reshape(1)]
    slots = _exchange_small(_pack(parts, d), "exchange_small_grads")
    total = _unpack(_sum_slots(slots, "sum_small"), parts)
    g_small = dict(zip(SMALL_ORDER, total[:-1]))
    loss = 0.5 * total[-1][0] / d
    g_small["conv_w"] = lax.dynamic_slice_in_dim(g_small["conv_w"], chip * cw_shard, cw_shard, axis=2)

    grads = dict(g_small)
    grads.update(w_in=g_big[0], w_out=g_big[1], w_pg=g_big[2], w_pe=g_big[3])
    weights = dict(norm_g=norm_g, w_in=w_in, conv_w=conv_w, conv_b=conv_b, branch_g=branch_g, w_out=w_out,
                   ple_norm_g=ple_norm_g, w_pg=w_pg, b_pg=b_pg, w_pe=w_pe, final_g=final_g)
    ms = dict(norm_g=m_norm_g, w_in=m_w_in, conv_w=m_conv_w, conv_b=m_conv_b, branch_g=m_branch_g, w_out=m_w_out,
              ple_norm_g=m_ple_norm_g, w_pg=m_w_pg, b_pg=m_b_pg, w_pe=m_w_pe, final_g=m_final_g)
    vs = dict(norm_g=v_norm_g, w_in=v_w_in, conv_w=v_conv_w, conv_b=v_conv_b, branch_g=v_branch_g, w_out=v_w_out,
              ple_norm_g=v_ple_norm_g, w_pg=v_w_pg, b_pg=v_b_pg, w_pe=v_w_pe, final_g=v_final_g)
    names = ("norm_g", "w_in", "conv_w", "conv_b", "branch_g", "w_out", "ple_norm_g", "w_pg", "b_pg", "w_pe", "final_g")
    delta, new_m, new_v = {}, {}, {}
    for k in ("w_in", "w_out", "w_pg", "w_pe"):
        shp = weights[k].shape
        two = lambda a: a.reshape(-1, shp[-1])
        dl, mn, vn = _adamw(two(weights[k]), two(grads[k]), two(ms[k]), two(vs[k]), f"adamw_{k}")
        delta[k], new_m[k], new_v[k] = dl.reshape(shp), mn.reshape(shp), vn.reshape(shp)
        grads[k] = grads[k].reshape(shp)
    like = [weights[k] for k in SMALL_ORDER]
    packs = [_pack([src[k] for k in SMALL_ORDER], d) for src in (weights, grads, ms, vs)]
    outs = _adamw(*packs, "adamw_small")
    for res, o in zip((delta, new_m, new_v), outs):
        res.update(dict(zip(SMALL_ORDER, _unpack(o, like))))

    return (loss, dx[None], *[grads[k] for k in names], *[delta[k] for k in names],
            *[new_m[k] for k in names], *[new_v[k] for k in names])
```

```python
import math

import jax
import jax.numpy as jnp
from jax import lax
from jax.experimental import pallas as pl
from jax.experimental.pallas import tpu as pltpu

F32 = jnp.float32
BF16 = jnp.bfloat16
EPS = 1e-6
HEAD = 64
LANES = 128
ATT_TK = 256
ATT_TQ = 512
ATT_ROWS = 128
ALIVE_LOG = -105.0
DEPTH = 2
VMEM_LIMIT = 56 * 1024 * 1024
MESH = pl.DeviceIdType.MESH
ANY = pl.BlockSpec(memory_space=pl.ANY)

ADAM_LR = 0.001
ADAM_B1 = 0.9
ADAM_B2 = 0.999
ADAM_EPS = 1e-08
ADAM_WD = 0.01
ADAM_STEP = 10


def _pcall(body, **kw):
    return pl.pallas_call(body, **kw)


def _cp(n_axes):
    return pltpu.CompilerParams(dimension_semantics=("arbitrary",) * n_axes, vmem_limit_bytes=VMEM_LIMIT)


def _tile(n, pref):
    return pref if n % pref == 0 else n


def _split_dot(a, b, passes):
    out = None
    rem = a
    for _ in range(passes):
        hi = rem.astype(BF16)
        t = jnp.dot(hi, b, preferred_element_type=F32)
        out = t if out is None else out + t
        rem = rem - hi.astype(F32)
    return out


def _group_mat():
    r = lax.broadcasted_iota(jnp.int32, (LANES, LANES), 0) // HEAD
    c = lax.broadcasted_iota(jnp.int32, (LANES, LANES), 1) // HEAD
    return jnp.where(r == c, 1.0 / HEAD, 0.0).astype(BF16)


def _group_mean(v, gm):
    return _split_dot(v, gm, 3)


def _sigmoid(z):
    return 1.0 / (1.0 + jnp.exp(-z))


def _dot_nt(a, b):
    return lax.dot_general(a, b, (((1,), (1,)), ((), ())), preferred_element_type=F32)


def _dot_tn(a, b):
    return lax.dot_general(a, b, (((0,), (0,)), ((), ())), preferred_element_type=F32)


def _cast_into_full(w, chip, axis, name):
    _, r, c = w.shape
    tr = _tile(r, 256)
    nb = r // tr
    full = (DEPTH, 4 * r, c) if axis == 0 else (DEPTH, r, 4 * c)

    def body(k_ref, w_ref, o_ref):
        o_ref[...] = w_ref[...].astype(BF16)

    out_map = (lambda l, i, k: (l, k[0] * nb + i, 0)) if axis == 0 else (lambda l, i, k: (l, i, k[0]))
    grid_spec = pltpu.PrefetchScalarGridSpec(
        num_scalar_prefetch=1, grid=(DEPTH, nb),
        in_specs=[pl.BlockSpec((1, tr, c), lambda l, i, k: (l, i, 0))],
        out_specs=pl.BlockSpec((1, tr, c), out_map))
    return _pcall(body, name=name, grid_spec=grid_spec, out_shape=jax.ShapeDtypeStruct(full, BF16),
                  compiler_params=_cp(2))(chip.reshape(1).astype(jnp.int32), w)


def _rms_bwd_rows(dh, xv, g):
    r = lax.rsqrt(jnp.mean(xv * xv, axis=-1, keepdims=True) + EPS)
    xn = xv * r
    dxn = dh * g
    dx = r * (dxn - xn * jnp.mean(dxn * xn, axis=-1, keepdims=True))
    return dx, dh * xn


def _colsum8(v):
    tm, d = v.shape
    return jnp.sum(v.reshape(tm // 8, 8, d), axis=0)


def _inproj(x, g, w, layer, name):
    s, d = x.shape
    n = w.shape[2]
    sw = d // 2
    ns = n // sw
    tm = _tile(s, 512)

    def body(x_ref, g_ref, w_ref, h_ref, o_ref):
        xv = x_ref[...]
        r = lax.rsqrt(jnp.mean(xv * xv, axis=-1, keepdims=True) + EPS)
        h = (xv * r * g_ref[...]).astype(BF16)
        h_ref[...] = h
        for k in range(ns):
            o_ref[k] = jnp.dot(h, w_ref[0, :, k * sw:(k + 1) * sw], preferred_element_type=F32).astype(BF16)

    return _pcall(body, name=name, grid=(s // tm,),
                  in_specs=[pl.BlockSpec((tm, d), lambda m: (m, 0)), pl.BlockSpec((1, d), lambda m: (0, 0)),
                            pl.BlockSpec((1, d, n), lambda m: (layer, 0, 0))],
                  out_specs=[pl.BlockSpec((tm, d), lambda m: (m, 0)), pl.BlockSpec((ns, tm, sw), lambda m: (0, m, 0))],
                  out_shape=[jax.ShapeDtypeStruct((s, d), BF16), jax.ShapeDtypeStruct((ns, s, sw), BF16)],
                  compiler_params=_cp(1))(x, g, w)


def _softplus_parts(z):
    lm = jnp.minimum(-z, 0.0) - jnp.log(1.0 + jnp.exp(-jnp.abs(z)))
    return lm, lm + z


def _attn_tiles(s):
    tk = _tile(s, ATT_TK)
    tq = _tile(s, ATT_TQ)
    return tk, tq, tq // tk, min(ATT_ROWS, tq)


def _chain_live(d, r0, rows, tk):
    return d is None or r0 + rows - 1 > d * tk


def _chain_mask(d, r0, rows, tk):
    if d is None or r0 >= (d + 1) * tk:
        return None
    row = lax.broadcasted_iota(jnp.int32, (rows, tk), 0)
    col = lax.broadcasted_iota(jnp.int32, (rows, tk), 1)
    return col + d * tk < row + r0


def _any_alive(rsums):
    m = rsums[0]
    for r in rsums[1:]:
        m = jnp.maximum(m, r)
    return jnp.max((m > ALIVE_LOG).astype(jnp.int32))


def _attn_fwd(proj, name):
    _, s, sw = proj.shape
    nhp = sw // LANES
    tk, tq, nd, rows = _attn_tiles(s)
    nq = s // tq
    scale = 1.0 / math.sqrt(HEAD)

    def body(q_ref, k_ref, v_ref, o_ref, tl_ref, nw_ref, acc_ref):
        i = pl.program_id(1)
        tri = (lax.broadcasted_iota(jnp.int32, (tk, tk), 0) >
               lax.broadcasted_iota(jnp.int32, (tk, tk), 1)).astype(BF16)
        lane = lax.broadcasted_iota(jnp.int32, (tq, LANES), 1)
        q = q_ref[0] * jnp.asarray(scale, BF16)
        qms = [jnp.where((lane // HEAD) == h, q, jnp.zeros_like(q)) for h in range(2)]
        acc_ref[...] = jnp.zeros_like(acc_ref)
        chains = [(h, r0) for h in range(2) for r0 in range(0, tq, rows)]
        qparts = [qms[h][r0:r0 + rows] for h, r0 in chains]

        def tile(j, rsums, d):
            k0 = pl.multiple_of(j * tk, tk)
            kj = k_ref[0, pl.ds(k0, tk), :]
            vj = v_ref[0, pl.ds(k0, tk), :]
            live = [n for n, (h, r0) in enumerate(chains) if _chain_live(d, r0, rows, tk)]
            masks = {n: _chain_mask(d, chains[n][1], rows, tk) for n in live}
            zs = {n: _dot_nt(qparts[n], kj) for n in live}
            lms, lss, css = {}, {}, {}
            for n in live:
                lm, ls = _softplus_parts(zs[n])
                if masks[n] is not None:
                    lm = jnp.where(masks[n], lm, 0.0)
                lms[n], lss[n] = lm, ls
                css[n] = _split_dot(lm, tri, 2)
            out = list(rsums)
            for n in live:
                h, r0 = chains[n]
                a = jnp.exp(lss[n] + (rsums[n] + css[n]))
                if masks[n] is not None:
                    a = jnp.where(masks[n], a, 0.0)
                acc_ref[h, r0:r0 + rows, :] += jnp.dot(a.astype(BF16), vj, preferred_element_type=F32)
                out[n] = rsums[n] + jnp.sum(lms[n], axis=1, keepdims=True)
            return tuple(out)

        rsums = (jnp.zeros((rows, 1), F32),) * len(chains)
        for d in reversed(range(nd)):
            rsums = tile(i * nd + d, rsums, d)

        def walk(c):
            jj, rs, _ = c
            rs = tile(i * nd - 1 - jj, rs, None)
            return jj + 1, rs, _any_alive(rs)

        walked, rsums, _ = lax.while_loop(lambda c: (c[0] < i * nd) & (c[2] > 0), walk,
                                          (jnp.int32(0), rsums, _any_alive(rsums)))
        for n, (h, r0) in enumerate(chains):
            tl_ref[h, r0:r0 + rows, :] = rsums[n]
        nw_ref[0] = (jnp.zeros((8, LANES), jnp.int32) + walked).astype(F32)
        o_ref[...] = jnp.where(lane < HEAD, acc_ref[0], acc_ref[1]).astype(BF16)

    return _pcall(
        body, name=name, grid=(nhp, nq),
        in_specs=[pl.BlockSpec((1, tq, LANES), lambda hp, i: (4, i, hp)),
                  pl.BlockSpec((1, s, LANES), lambda hp, i: (5, 0, hp)),
                  pl.BlockSpec((1, s, LANES), lambda hp, i: (6, 0, hp))],
        out_specs=[pl.BlockSpec((tq, LANES), lambda hp, i: (i, hp)),
                   pl.BlockSpec((2, tq, 1), lambda hp, i: (hp, i, 0)),
                   pl.BlockSpec((1, 8, LANES), lambda hp, i: (hp * nq + i, 0, 0))],
        out_shape=[jax.ShapeDtypeStruct((s, sw), BF16), jax.ShapeDtypeStruct((2 * nhp, s, 1), F32),
                   jax.ShapeDtypeStruct((nhp * nq, 8, LANES), F32)],
        scratch_shapes=[pltpu.VMEM((2, tq, LANES), F32)],
        compiler_params=_cp(2))(proj, proj, proj)


def _conv_rows(cc_ref, ch_ref, w_ref, b_ref, r, tc):
    r0 = pl.multiple_of(r * tc, tc)
    u = cc_ref[0, pl.ds(r0, tc), :].astype(F32) * ch_ref[0, pl.ds(r0, tc), :].astype(F32)
    p0 = pl.multiple_of(jnp.maximum(r0 - 16, 0), 16)
    up = cc_ref[0, pl.ds(p0, 16), :].astype(F32) * ch_ref[0, pl.ds(p0, 16), :].astype(F32)
    up = up * (r > 0).astype(F32)
    prev1 = up[15:16, :]
    prev2 = up[14:15, :]
    rid = lax.broadcasted_iota(jnp.int32, u.shape, 0)
    s1 = jnp.where(rid == 0, prev1, pltpu.roll(u, 1, axis=0))
    s2 = jnp.where(rid == 0, prev2, jnp.where(rid == 1, prev1, pltpu.roll(u, 2, axis=0)))
    cv = b_ref[...] + s2 * w_ref[0:1, :] + s1 * w_ref[1:2, :] + u * w_ref[2:3, :]
    return r0, u, s1, s2, cv


def _mix_fwd(proj, ya, conv_w, conv_b, bg, name):
    _, s, sw = proj.shape
    nh = sw // LANES
    tc = _tile(s, 256)

    def body(cb_ref, cc_ref, ch_ref, cz_ref, ya_ref, az_ref, w_ref, b_ref, g_ref, y_ref):
        c = pl.program_id(0)
        gm = _group_mat()

        def finish(r0, yv, zg):
            n = yv * lax.rsqrt(_group_mean(yv * yv, gm) + EPS)
            y_ref[pl.ds(r0, tc), :] = (n * g_ref[...] * (zg * _sigmoid(zg))).astype(BF16)

        @pl.when(c < nh)
        def _():
            def step(r, carry):
                r0, _, _, _, cv = _conv_rows(cc_ref, ch_ref, w_ref, b_ref, r, tc)
                yc = cb_ref[0, pl.ds(r0, tc), :].astype(F32) * cv
                finish(r0, yc, cz_ref[0, pl.ds(r0, tc), :].astype(F32))
                return carry
            lax.fori_loop(0, s // tc, step, 0)

        @pl.when(c >= nh)
        def _():
            def step(r, carry):
                r0 = pl.multiple_of(r * tc, tc)
                finish(r0, ya_ref[pl.ds(r0, tc), :].astype(F32), az_ref[0, pl.ds(r0, tc), :].astype(F32))
                return carry
            lax.fori_loop(0, s // tc, step, 0)

    def sec(k):
        return pl.BlockSpec((1, s, LANES), lambda c: (k, 0, jnp.minimum(c, nh - 1)))

    return _pcall(
        body, name=name, grid=(2 * nh,),
        in_specs=[sec(0), sec(1), sec(2), sec(3),
                  pl.BlockSpec((s, LANES), lambda c: (0, jnp.maximum(c - nh, 0))),
                  pl.BlockSpec((1, s, LANES), lambda c: (7, 0, jnp.maximum(c - nh, 0))),
                  pl.BlockSpec((3, LANES), lambda c: (0, jnp.minimum(c, nh - 1))),
                  pl.BlockSpec((1, LANES), lambda c: (0, jnp.minimum(c, nh - 1))),
                  pl.BlockSpec((1, LANES), lambda c: (0, c))],
        out_specs=pl.BlockSpec((s, LANES), lambda c: (0, c)),
        out_shape=jax.ShapeDtypeStruct((s, 2 * sw), BF16), compiler_params=_cp(1),
    )(proj, proj, proj, proj, ya, proj, conv_w, conv_b, bg)


def _outproj(y, w, layer, x, g, name):
    s, d = x.shape
    tm = _tile(s, 256)

    def body(y_ref, w_ref, x_ref, g_ref, x1_ref, hn_ref):
        x1 = x_ref[...] + jnp.dot(y_ref[...], w_ref[0], preferred_element_type=F32)
        x1_ref[...] = x1
        r = lax.rsqrt(jnp.mean(x1 * x1, axis=-1, keepdims=True) + EPS)
        hn_ref[...] = (x1 * r * g_ref[...]).astype(BF16)

    row = lambda m: (m, 0)
    fix = lambda m: (0, 0)
    return _pcall(body, name=name, grid=(s // tm,),
                  in_specs=[pl.BlockSpec((tm, d), row), pl.BlockSpec((1, d, d), lambda m: (layer, 0, 0)),
                            pl.BlockSpec((tm, d), row), pl.BlockSpec((1, d), fix)],
                  out_specs=[pl.BlockSpec((tm, d), row), pl.BlockSpec((tm, d), row)],
                  out_shape=[jax.ShapeDtypeStruct((s, d), F32), jax.ShapeDtypeStruct((s, d), BF16)],
                  compiler_params=_cp(1))(y, w, x, g)


def _ple_fwd(hn, w_pg, b_pg, p, w_pe, layer, x1, name):
    s, d = x1.shape
    pd = p.shape[2]
    tm = _tile(s, 256)

    def body(hn_ref, wg_ref, b_ref, p_ref, we_ref, x1_ref, x2_ref, gate_ref, e_ref):
        gate = _sigmoid(jnp.dot(hn_ref[...], wg_ref[0], preferred_element_type=F32) + b_ref[...])
        e = jnp.dot(p_ref[0].astype(BF16), we_ref[0], preferred_element_type=F32)
        x2_ref[...] = x1_ref[...] + gate * e
        gate_ref[...] = gate.astype(BF16)
        e_ref[...] = e.astype(BF16)

    row = lambda m: (m, 0)
    fix = lambda m: (0, 0)
    return _pcall(body, name=name, grid=(s // tm,),
                  in_specs=[pl.BlockSpec((tm, d), row), pl.BlockSpec((1, d, d), lambda m: (layer, 0, 0)),
                            pl.BlockSpec((1, d), fix), pl.BlockSpec((1, tm, pd), lambda m: (layer, m, 0)),
                            pl.BlockSpec((1, pd, d), lambda m: (layer, 0, 0)), pl.BlockSpec((tm, d), row)],
                  out_specs=[pl.BlockSpec((tm, d), row)] * 3,
                  out_shape=[jax.ShapeDtypeStruct((s, d), F32), jax.ShapeDtypeStruct((s, d), BF16),
                             jax.ShapeDtypeStruct((s, d), BF16)],
                  compiler_params=_cp(1))(hn, w_pg, b_pg, p, w_pe, x1)


def _loss_head(x, tgt, g, name):
    s, d = x.shape
    tm = _tile(s, 256)

    def body(x_ref, t_ref, g_ref, l_ref, dx_ref, dg_ref):
        m = pl.program_id(0)

        @pl.when(m == 0)
        def _():
            l_ref[...] = jnp.zeros_like(l_ref)
            dg_ref[...] = jnp.zeros_like(dg_ref)

        xv = x_ref[...]
        gv = g_ref[...]
        r = lax.rsqrt(jnp.mean(xv * xv, axis=-1, keepdims=True) + EPS)
        xn = xv * r
        err = xn * gv - t_ref[...]
        l_ref[...] += jnp.sum(err * err)
        dy = err * (1.0 / d)
        dxn = dy * gv
        dx_ref[...] = r * (dxn - xn * jnp.mean(dxn * xn, axis=-1, keepdims=True))
        dg_ref[...] += _colsum8(dy * xn)

    row = lambda m: (m, 0)
    fix = lambda m: (0, 0)
    return _pcall(body, name=name, grid=(s // tm,),
                  in_specs=[pl.BlockSpec((tm, d), row), pl.BlockSpec((tm, d), row), pl.BlockSpec((1, d), fix)],
                  out_specs=[pl.BlockSpec((8, LANES), fix), pl.BlockSpec((tm, d), row), pl.BlockSpec((8, d), fix)],
                  out_shape=[jax.ShapeDtypeStruct((8, LANES), F32), jax.ShapeDtypeStruct((s, d), F32),
                             jax.ShapeDtypeStruct((8, d), F32)],
                  compiler_params=_cp(1))(x, tgt, g)


def _ple_bwd(dx2, gate, e, x1, w_pg, g_ple, w_out, layer, name):
    s, d = dx2.shape
    tm = _tile(s, 256)

    def body(dx2_ref, gate_ref, e_ref, x1_ref, wg_ref, g_ref, wo_ref,
             du_ref, de_ref, dx1_ref, dy_ref, db_ref, dg_ref):
        m = pl.program_id(0)

        @pl.when(m == 0)
        def _():
            db_ref[...] = jnp.zeros_like(db_ref)
            dg_ref[...] = jnp.zeros_like(dg_ref)

        dx2v = dx2_ref[...]
        gate = gate_ref[...].astype(F32)
        du = dx2v * e_ref[...].astype(F32) * gate * (1.0 - gate)
        de_ref[...] = (dx2v * gate).astype(BF16)
        dub = du.astype(BF16)
        du_ref[...] = dub
        db_ref[...] += _colsum8(du)
        dhn = _dot_nt(dub, wg_ref[0])
        dxr, dgr = _rms_bwd_rows(dhn, x1_ref[...], g_ref[...])
        dx1 = dx2v + dxr
        dx1_ref[...] = dx1
        dg_ref[...] += _colsum8(dgr)
        dy_ref[...] = _dot_nt(dx1.astype(BF16), wo_ref[0]).astype(BF16)

    row = lambda m: (m, 0)
    fix = lambda m: (0, 0)
    t = pl.BlockSpec((tm, d), row)
    return _pcall(body, name=name, grid=(s // tm,),
                  in_specs=[t, t, t, t, pl.BlockSpec((1, d, d), lambda m: (layer, 0, 0)), pl.BlockSpec((1, d), fix),
                            pl.BlockSpec((1, d, d), lambda m: (layer, 0, 0))],
                  out_specs=[t, t, t, t, pl.BlockSpec((8, d), fix), pl.BlockSpec((8, d), fix)],
                  out_shape=[jax.ShapeDtypeStruct((s, d), BF16), jax.ShapeDtypeStruct((s, d), BF16),
                             jax.ShapeDtypeStruct((s, d), F32), jax.ShapeDtypeStruct((s, d), BF16),
                             jax.ShapeDtypeStruct((8, d), F32), jax.ShapeDtypeStruct((8, d), F32)],
                  compiler_params=_cp(1))(dx2, gate, e, x1, w_pg, g_ple, w_out)


def _mm_tn(a, b, stack, layer, name, a_stacked=False):
    s, ka = a.shape[-2:]
    n = b.shape[1]
    tn = _tile(n, 1024)
    ns = n // tn
    tk = _tile(s, 512)
    nk = s // tk

    def body(*refs):
        a_ref, b_ref = refs[0], refs[1]
        o_ref, acc_ref = refs[-2], refs[-1]
        k = pl.program_id(1)

        @pl.when(k == 0)
        def _():
            acc_ref[...] = jnp.zeros_like(acc_ref)

        av = a_ref[0] if a_stacked else a_ref[...]
        acc_ref[...] += _dot_tn(av.astype(BF16), b_ref[...].astype(BF16))

        @pl.when(k == nk - 1)
        def _():
            o_ref[0] = acc_ref[...]

    a_spec = (pl.BlockSpec((1, tk, ka), lambda j, k: (layer, k, 0)) if a_stacked
              else pl.BlockSpec((tk, ka), lambda j, k: (k, 0)))
    in_specs = [a_spec, pl.BlockSpec((tk, tn), lambda j, k: (k, j))]
    args = [a, b]
    aliases = {}
    if stack is not None:
        in_specs.append(ANY)
        args.append(stack)
        aliases = {2: 0}
    return _pcall(body, name=name, grid=(ns, nk), in_specs=in_specs,
                  out_specs=pl.BlockSpec((1, ka, tn), lambda j, k: (layer, 0, j)),
                  out_shape=jax.ShapeDtypeStruct((DEPTH, ka, n), F32),
                  scratch_shapes=[pltpu.VMEM((ka, tn), F32)], input_output_aliases=aliases,
                  compiler_params=_cp(2))(*args)


def _norm_gate_bwd(dy, yv, zg, g, gm):
    r = lax.rsqrt(_group_mean(yv * yv, gm) + EPS)
    n = yv * r
    sg = _sigmoid(zg)
    sil = zg * sg
    dzg = dy * n * g * (sg * (1.0 + zg * (1.0 - sg)))
    dn = dy * g * sil
    dyv = r * (dn - n * _group_mean(dn * n, gm))
    return dyv, dzg, dy * n * sil


def _convmix_bwd(dy, proj, conv_w, conv_b, bg, name):
    _, s, sw = proj.shape
    nh = sw // LANES
    tc = _tile(s, 256)
    nr = s // tc

    def body(dy_ref, cb_ref, cc_ref, ch_ref, cz_ref, w_ref, b_ref, g_ref,
             dp_ref, dw_ref, db_ref, dg_ref, dcv_ref):
        gm = _group_mat()
        dcv_ref[pl.ds(s, 8), :] = jnp.zeros((8, LANES), F32)

        def pass1(r, carry):
            dw0, dw1, dw2, db, dg = carry
            r0, u, s1, s2, cv = _conv_rows(cc_ref, ch_ref, w_ref, b_ref, r, tc)
            cb = cb_ref[0, pl.ds(r0, tc), :].astype(F32)
            dyc, dcz, dgr = _norm_gate_bwd(dy_ref[pl.ds(r0, tc), :].astype(F32), cb * cv,
                                           cz_ref[0, pl.ds(r0, tc), :].astype(F32), g_ref[...], gm)
            dp_ref[0, pl.ds(r0, tc), :] = (dyc * cv).astype(BF16)
            dp_ref[3, pl.ds(r0, tc), :] = dcz.astype(BF16)
            dcv = dyc * cb
            dcv_ref[pl.ds(r0, tc), :] = dcv
            return (dw0 + _colsum8(dcv * s2), dw1 + _colsum8(dcv * s1), dw2 + _colsum8(dcv * u),
                    db + _colsum8(dcv), dg + _colsum8(dgr))

        z8 = jnp.zeros((8, LANES), F32)
        dw0, dw1, dw2, db, dg = lax.fori_loop(0, nr, pass1, (z8, z8, z8, z8, z8))
        dw_ref[0] = dw0
        dw_ref[1] = dw1
        dw_ref[2] = dw2
        db_ref[...] = db
        dg_ref[...] = dg

        def pass2(r, carry):
            r0 = pl.multiple_of(r * tc, tc)
            dcv = dcv_ref[pl.ds(r0, tc), :]
            nxt = dcv_ref[pl.ds(pl.multiple_of(r0 + tc, 8), 8), :]
            rid = lax.broadcasted_iota(jnp.int32, dcv.shape, 0)
            n1 = jnp.where(rid == tc - 1, nxt[0:1, :], pltpu.roll(dcv, tc - 1, axis=0))
            n2 = jnp.where(rid == tc - 1, nxt[1:2, :],
                           jnp.where(rid == tc - 2, nxt[0:1, :], pltpu.roll(dcv, tc - 2, axis=0)))
            du = dcv * w_ref[2:3, :] + n1 * w_ref[1:2, :] + n2 * w_ref[0:1, :]
            dp_ref[1, pl.ds(r0, tc), :] = (du * ch_ref[0, pl.ds(r0, tc), :].astype(F32)).astype(BF16)
            dp_ref[2, pl.ds(r0, tc), :] = (du * cc_ref[0, pl.ds(r0, tc), :].astype(F32)).astype(BF16)
            return carry

        lax.fori_loop(0, nr, pass2, 0)

    def sec(k):
        return pl.BlockSpec((1, s, LANES), lambda c: (k, 0, c))

    col = lambda c: (0, c)
    return _pcall(
        body, name=name, grid=(nh,),
        in_specs=[pl.BlockSpec((s, LANES), col), sec(0), sec(1), sec(2), sec(3),
                  pl.BlockSpec((3, LANES), col), pl.BlockSpec((1, LANES), col), pl.BlockSpec((1, LANES), col)],
        out_specs=[pl.BlockSpec((4, s, LANES), lambda c: (0, 0, c)), pl.BlockSpec((3, 8, LANES), lambda c: (0, 0, c)),
                   pl.BlockSpec((8, LANES), col), pl.BlockSpec((8, LANES), col)],
        out_shape=[jax.ShapeDtypeStruct((8, s, sw), BF16), jax.ShapeDtypeStruct((3, 8, sw), F32),
                   jax.ShapeDtypeStruct((8, sw), F32), jax.ShapeDtypeStruct((8, sw), F32)],
        scratch_shapes=[pltpu.VMEM((s + 8, LANES), F32)], compiler_params=_cp(1),
    )(dy, proj, proj, proj, proj, conv_w, conv_b, bg)


def _attn_bwd(proj, dy, ya, tl, walked, bg, buf, name):
    _, s, sw = proj.shape
    nhp = sw // LANES
    tk, t, nd, rows_c = _attn_tiles(s)
    nq = s // t
    scale = 1.0 / math.sqrt(HEAD)

    def body(q_ref, k_ref, v_ref, az_ref, dy_ref, ya_ref, tl_ref, nw_ref, g_ref, buf_ref, out_ref, dg_ref,
             dka_ref, dva_ref, dqa_ref):
        step = pl.program_id(1)
        i = nq - 1 - step

        @pl.when(step == 0)
        def _():
            dka_ref[...] = jnp.zeros_like(dka_ref)
            dva_ref[...] = jnp.zeros_like(dva_ref)
            dg_ref[...] = jnp.zeros_like(dg_ref)

        dyv, dzg, dgr = _norm_gate_bwd(dy_ref[...].astype(F32), ya_ref[...].astype(F32), az_ref[0].astype(F32),
                                       g_ref[...], _group_mat())
        out_ref[3] = dzg.astype(BF16)
        dg_ref[...] += _colsum8(dgr)

        tri = (lax.broadcasted_iota(jnp.int32, (tk, tk), 0) <=
               lax.broadcasted_iota(jnp.int32, (tk, tk), 1)).astype(BF16)
        lane = lax.broadcasted_iota(jnp.int32, (t, LANES), 1)
        q = q_ref[0] * jnp.asarray(scale, BF16)
        do = dyv.astype(BF16)
        qms = [jnp.where((lane // HEAD) == h, q, jnp.zeros_like(q)) for h in range(2)]
        doms = [jnp.where((lane // HEAD) == h, do, jnp.zeros_like(do)) for h in range(2)]
        dqa_ref[...] = jnp.zeros_like(dqa_ref)
        chains = [(h, r0) for h in range(2) for r0 in range(0, t, rows_c)]
        qparts = [qms[h][r0:r0 + rows_c] for h, r0 in chains]
        doparts = [doms[h][r0:r0 + rows_c] for h, r0 in chains]
        tots = [tl_ref[h, r0:r0 + rows_c, :] for h, r0 in chains]

        def tile(j, carry, d):
            k0 = pl.multiple_of(j * tk, tk)
            kj = k_ref[0, pl.ds(k0, tk), :]
            vj = v_ref[0, pl.ds(k0, tk), :]
            live = [n for n, (h, r0) in enumerate(chains) if _chain_live(d, r0, rows_c, tk)]
            masks = {n: _chain_mask(d, chains[n][1], rows_c, tk) for n in live}
            zs = {n: _dot_nt(qparts[n], kj) for n in live}
            das = {n: _dot_nt(doparts[n], vj) for n in live}
            lms, lss, cls = {}, {}, {}
            for n in live:
                lm, ls = _softplus_parts(zs[n])
                if masks[n] is not None:
                    lm = jnp.where(masks[n], lm, 0.0)
                lms[n], lss[n] = lm, ls
                cls[n] = _split_dot(lm, tri, 2)
            abs_, gs, cgs = {}, {}, {}
            for n in live:
                a = jnp.exp(lss[n] + (tots[n] - carry[n][0] - cls[n]))
                if masks[n] is not None:
                    a = jnp.where(masks[n], a, 0.0)
                gs[n] = a * das[n]
                abs_[n] = a.astype(BF16)
                cgs[n] = _split_dot(gs[n], tri, 2)
            out = list(carry)
            dk = None
            dv = None
            for n in live:
                h, r0 = chains[n]
                psum, gsum = carry[n]
                dz = gs[n] - jnp.exp(lss[n]) * (gsum + cgs[n])
                if masks[n] is not None:
                    dz = jnp.where(masks[n], dz, 0.0)
                dz = dz.astype(BF16)
                dqa_ref[h, r0:r0 + rows_c, :] += jnp.dot(dz, kj, preferred_element_type=F32)
                dkh = _dot_tn(dz, qparts[n])
                dvh = _dot_tn(abs_[n], doparts[n])
                dk = dkh if dk is None else dk + dkh
                dv = dvh if dv is None else dv + dvh
                out[n] = (psum + jnp.sum(lms[n], axis=1, keepdims=True), gsum + jnp.sum(gs[n], axis=1, keepdims=True))
            dka_ref[pl.ds(k0, tk), :] += dk
            dva_ref[pl.ds(k0, tk), :] += dv
            return tuple(out)

        z1 = jnp.zeros((rows_c, 1), F32)
        first = i * nd - jnp.clip(jnp.max(nw_ref[0].astype(jnp.int32)), 0, i * nd)
        carry = lax.fori_loop(first, i * nd, lambda j, c: tile(j, c, None), ((z1, z1),) * len(chains))
        for d in range(nd):
            carry = tile(i * nd + d, carry, d)
        out_ref[0] = (jnp.where(lane < HEAD, dqa_ref[0], dqa_ref[1]) * scale).astype(BF16)
        own = pl.multiple_of(i * t, t)
        out_ref[1] = dka_ref[pl.ds(own, t), :].astype(BF16)
        out_ref[2] = dva_ref[pl.ds(own, t), :].astype(BF16)

    def rows(sec):
        return pl.BlockSpec((1, t, LANES), lambda hp, st: (sec, nq - 1 - st, hp))

    def whole(sec):
        return pl.BlockSpec((1, s, LANES), lambda hp, st: (sec, 0, hp))

    return _pcall(
        body, name=name, grid=(nhp, nq),
        in_specs=[rows(4), whole(5), whole(6), rows(7),
                  pl.BlockSpec((t, LANES), lambda hp, st: (nq - 1 - st, hp + nhp)),
                  pl.BlockSpec((t, LANES), lambda hp, st: (nq - 1 - st, hp)),
                  pl.BlockSpec((2, t, 1), lambda hp, st: (hp, nq - 1 - st, 0)),
                  pl.BlockSpec((1, 8, LANES), lambda hp, st: (hp * nq + nq - 1 - st, 0, 0)),
                  pl.BlockSpec((1, LANES), lambda hp, st: (0, hp + nhp)), ANY],
        out_specs=[pl.BlockSpec((4, t, LANES), lambda hp, st: (1, nq - 1 - st, hp)),
                   pl.BlockSpec((8, LANES), lambda hp, st: (0, hp))],
        out_shape=[jax.ShapeDtypeStruct(buf.shape, buf.dtype), jax.ShapeDtypeStruct((8, sw), F32)],
        input_output_aliases={9: 0},
        scratch_shapes=[pltpu.VMEM((s, LANES), F32), pltpu.VMEM((s, LANES), F32), pltpu.VMEM((2, t, LANES), F32)],
        compiler_params=_cp(2))(proj, proj, proj, proj, dy, ya, tl, walked, bg, buf)


def _grad_w_in(h, dproj, stack, layer, name):
    s, d = h.shape
    ns, _, sw = dproj.shape

    def body(*refs):
        h_ref, b_ref = refs[0], refs[1]
        o_ref, ht_ref = refs[-2], refs[-1]

        @pl.when(pl.program_id(0) == 0)
        def _():
            ht_ref[...] = h_ref[...].T

        o_ref[0] = jnp.dot(ht_ref[...], b_ref[0], preferred_element_type=F32)

    in_specs = [pl.BlockSpec((s, d), lambda j: (0, 0)), pl.BlockSpec((1, s, sw), lambda j: (j, 0, 0))]
    args = [h, dproj]
    aliases = {}
    if stack is not None:
        in_specs.append(ANY)
        args.append(stack)
        aliases = {2: 0}
    return _pcall(body, name=name, grid=(ns,), in_specs=in_specs,
                  out_specs=pl.BlockSpec((1, d, sw), lambda j: (layer, 0, j)),
                  out_shape=jax.ShapeDtypeStruct((DEPTH, d, ns * sw), F32),
                  scratch_shapes=[pltpu.VMEM((d, s), BF16)], input_output_aliases=aliases,
                  compiler_params=_cp(1))(*args)


def _inproj_bwd(dproj, w, layer, x, g, dx1, name):
    ns, s, sw = dproj.shape
    d = x.shape[1]
    tm = _tile(s, 256)

    def body(dp_ref, w_ref, x_ref, g_ref, dx1_ref, dx_ref, dg_ref):
        @pl.when(pl.program_id(0) == 0)
        def _():
            dg_ref[...] = jnp.zeros_like(dg_ref)

        dh = _dot_nt(dp_ref[0], w_ref[0, :, 0:sw])
        for k in range(1, ns):
            dh = dh + _dot_nt(dp_ref[k], w_ref[0, :, k * sw:(k + 1) * sw])
        dxr, dgr = _rms_bwd_rows(dh, x_ref[...], g_ref[...])
        dx_ref[...] = dx1_ref[...] + dxr
        dg_ref[...] += _colsum8(dgr)

    row = lambda m: (m, 0)
    fix = lambda m: (0, 0)
    return _pcall(body, name=name, grid=(s // tm,),
                  in_specs=[pl.BlockSpec((ns, tm, sw), lambda m: (0, m, 0)),
                            pl.BlockSpec((1, d, ns * sw), lambda m: (layer, 0, 0)),
                            pl.BlockSpec((tm, d), row), pl.BlockSpec((1, d), fix), pl.BlockSpec((tm, d), row)],
                  out_specs=[pl.BlockSpec((tm, d), row), pl.BlockSpec((8, d), fix)],
                  out_shape=[jax.ShapeDtypeStruct((s, d), F32), jax.ShapeDtypeStruct((8, d), F32)],
                  compiler_params=_cp(1))(dproj, w, x, g, dx1)


def _adamw(w, g, m, v, name):
    r, c = w.shape
    tr = _tile(r, 256)
    c1 = 1.0 - ADAM_B1 ** ADAM_STEP
    c2 = 1.0 - ADAM_B2 ** ADAM_STEP

    def body(w_ref, g_ref, m_ref, v_ref, d_ref, mo_ref, vo_ref):
        gv = g_ref[...]
        mn = ADAM_B1 * m_ref[...] + (1.0 - ADAM_B1) * gv
        vn = ADAM_B2 * v_ref[...] + (1.0 - ADAM_B2) * (gv * gv)
        d_ref[...] = -ADAM_LR * ((mn / c1) / (jnp.sqrt(vn / c2) + ADAM_EPS) + ADAM_WD * w_ref[...])
        mo_ref[...] = mn
        vo_ref[...] = vn

    t = pl.BlockSpec((tr, c), lambda i: (i, 0))
    return _pcall(body, name=name, grid=(r // tr,), in_specs=[t] * 4, out_specs=[t] * 3,
                  out_shape=[jax.ShapeDtypeStruct((r, c), F32)] * 3, compiler_params=_cp(1))(w, g, m, v)


def _add_layer(stack, other, layer, name):
    _, r, c = stack.shape
    tr = _tile(r, 256)

    def body(l_ref, s_ref, o_ref, out_ref, outb_ref):
        v = s_ref[0] + o_ref[...]
        out_ref[...] = v
        outb_ref[...] = v.astype(BF16)

    t = pl.BlockSpec((tr, c), lambda i, l: (i, 0))
    grid_spec = pltpu.PrefetchScalarGridSpec(
        num_scalar_prefetch=1, grid=(r // tr,),
        in_specs=[pl.BlockSpec((1, tr, c), lambda i, l: (l[0], i, 0)), t], out_specs=[t, t])
    return _pcall(body, name=name, grid_spec=grid_spec,
                  out_shape=[jax.ShapeDtypeStruct((r, c), F32), jax.ShapeDtypeStruct((r, c), BF16)],
                  compiler_params=_cp(1))(layer.reshape(1).astype(jnp.int32), stack, other)


def _sum_shard(full, parts, chip, layer, axis, name):
    _, r, c = parts.shape
    tr = _tile(r, 256)

    def body(k_ref, f_ref, p_ref, out_ref):
        out_ref[0] = ((f_ref[...] + p_ref[0].astype(F32)) + p_ref[1].astype(F32)) + p_ref[2].astype(F32)

    if axis == 1:
        f_spec = pl.BlockSpec((tr, c), lambda i, k: (i, k[0]))
    else:
        nb = r // tr
        f_spec = pl.BlockSpec((tr, c), lambda i, k: (k[0] * nb + i, 0))
    grid_spec = pltpu.PrefetchScalarGridSpec(
        num_scalar_prefetch=1, grid=(r // tr,),
        in_specs=[f_spec, pl.BlockSpec((3, tr, c), lambda i, k: (0, i, 0))],
        out_specs=pl.BlockSpec((1, tr, c), lambda i, k: (k[1], i, 0)))
    return _pcall(body, name=name, grid_spec=grid_spec, out_shape=jax.ShapeDtypeStruct((DEPTH, r, c), F32),
                  compiler_params=_cp(1))(jnp.stack([chip, layer]).astype(jnp.int32), full, parts)


def _sum_slots(slots, name):
    n = slots.shape[0]

    def body(s_ref, o_ref):
        acc = s_ref[0]
        for i in range(1, n):
            acc = acc + s_ref[i]
        o_ref[...] = acc

    return _pcall(body, name=name, out_shape=jax.ShapeDtypeStruct(slots.shape[1:], F32))(slots)


def _place():
    return lax.axis_index("x"), lax.axis_index("y"), lax.axis_index("c")


def _shard_view(ref, axis, chip, size):
    if axis == 0:
        return ref.at[pl.ds(chip * size, size), :]
    return ref.at[:, pl.ds(chip * size, size)]


SHARD_AXES = (1, 0, 0, 1)


def _gather_weights(fulls):
    n = len(fulls)
    sizes = [f.shape[1 + ax] // 4 for f, ax in zip(fulls, SHARD_AXES)]

    def body(*refs):
        ins, outs = refs[:n], refs[n:2 * n]
        ssem, rsem = refs[2 * n:]
        x, y, c = _place()
        me = 2 * x + y
        chips = [(1 - x, y), (x, 1 - y), (1 - x, 1 - y)]

        def piece(a, layer, chip, of=outs):
            return _shard_view(of[a].at[layer], SHARD_AXES[a], chip, sizes[a])

        sends = []
        for a in range(n):
            for j, (cx, cy) in enumerate(chips):
                cp = pltpu.make_async_remote_copy(
                    src_ref=piece(a, c, me, ins), dst_ref=piece(a, c, me), send_sem=ssem.at[a, j],
                    recv_sem=rsem.at[a, j], device_id=(cx, cy, c), device_id_type=MESH)
                cp.start()
                sends.append(cp)
        for a in range(n):
            for j, (cx, cy) in enumerate(chips):
                got = piece(a, c, 2 * cx + cy)
                pltpu.make_async_remote_copy(
                    src_ref=got, dst_ref=got, send_sem=ssem.at[a, j], recv_sem=rsem.at[a, j],
                    device_id=(cx, cy, c), device_id_type=MESH).wait_recv()
                cp = pltpu.make_async_remote_copy(
                    src_ref=got, dst_ref=got, send_sem=ssem.at[a, 3 + j], recv_sem=rsem.at[a, 3 + j],
                    device_id=(x, y, 1 - c), device_id_type=MESH)
                cp.start()
                sends.append(cp)
        for a in range(n):
            for j, (cx, cy) in enumerate(chips):
                got = piece(a, 1 - c, 2 * cx + cy)
                pltpu.make_async_remote_copy(
                    src_ref=got, dst_ref=got, send_sem=ssem.at[a, 3 + j], recv_sem=rsem.at[a, 3 + j],
                    device_id=(x, y, 1 - c), device_id_type=MESH).wait_recv()
        for cp in sends:
            cp.wait_send()

    return _pcall(body, name="gather_weights", in_specs=[ANY] * n, out_specs=[ANY] * n,
                  out_shape=[jax.ShapeDtypeStruct(f.shape, f.dtype) for f in fulls],
                  input_output_aliases={a: a for a in range(n)},
                  scratch_shapes=[pltpu.SemaphoreType.DMA((n, 6)), pltpu.SemaphoreType.DMA((n, 6))])(*fulls)


def _swap_layers(stacks):
    n = len(stacks)

    def body(*refs):
        srcs, outs = refs[:n], refs[n:2 * n]
        ssem, rsem = refs[2 * n:]
        x, y, c = _place()
        cps = [pltpu.make_async_remote_copy(src_ref=srcs[a].at[1 - c], dst_ref=outs[a], send_sem=ssem.at[a],
                                            recv_sem=rsem.at[a], device_id=(x, y, 1 - c), device_id_type=MESH)
               for a in range(n)]
        for cp in cps:
            cp.start()
        for cp in cps:
            cp.wait()

    return _pcall(body, name="swap_layers", in_specs=[ANY] * n, out_specs=[ANY] * n,
                  out_shape=[jax.ShapeDtypeStruct(st.shape[1:], st.dtype) for st in stacks],
                  scratch_shapes=[pltpu.SemaphoreType.DMA((n,)), pltpu.SemaphoreType.DMA((n,))])(*stacks)


def _scatter_shards(fulls):
    n = len(fulls)
    shard_shapes = []
    for f, ax in zip(fulls, SHARD_AXES):
        sh = list(f.shape)
        sh[ax] //= 4
        shard_shapes.append(tuple(sh))

    def body(*refs):
        srcs, outs = refs[:n], refs[n:2 * n]
        ssem, rsem = refs[2 * n:]
        x, y, c = _place()
        chips = [(1 - x, y), (x, 1 - y), (1 - x, 1 - y)]
        cps = []
        for a in range(n):
            for j, (cx, cy) in enumerate(chips):
                src = _shard_view(srcs[a], SHARD_AXES[a], 2 * cx + cy, shard_shapes[a][SHARD_AXES[a]])
                cps.append(pltpu.make_async_remote_copy(
                    src_ref=src, dst_ref=outs[a].at[j], send_sem=ssem.at[a, j], recv_sem=rsem.at[a, j],
                    device_id=(cx, cy, c), device_id_type=MESH))
        for cp in cps:
            cp.start()
        for cp in cps:
            cp.wait()

    return _pcall(body, name="scatter_shards", in_specs=[ANY] * n, out_specs=[ANY] * n,
                  out_shape=[jax.ShapeDtypeStruct((3,) + sh, f.dtype) for sh, f in zip(shard_shapes, fulls)],
                  scratch_shapes=[pltpu.SemaphoreType.DMA((n, 3)), pltpu.SemaphoreType.DMA((n, 3))])(*fulls)


def _pair_layers(stacks):
    n = len(stacks)

    def body(*refs):
        ins, outs = refs[:n], refs[n:2 * n]
        ssem, rsem = refs[2 * n:]
        x, y, c = _place()
        cps = [pltpu.make_async_remote_copy(src_ref=ins[a].at[c], dst_ref=outs[a].at[c], send_sem=ssem.at[a],
                                            recv_sem=rsem.at[a], device_id=(x, y, 1 - c), device_id_type=MESH)
               for a in range(n)]
        for cp in cps:
            cp.start()
        for a in range(n):
            got = outs[a].at[1 - c]
            pltpu.make_async_remote_copy(src_ref=got, dst_ref=got, send_sem=ssem.at[a], recv_sem=rsem.at[a],
                                         device_id=(x, y, 1 - c), device_id_type=MESH).wait_recv()
        for cp in cps:
            cp.wait_send()

    return _pcall(body, name="pair_layers", in_specs=[ANY] * n, out_specs=[ANY] * n,
                  out_shape=[jax.ShapeDtypeStruct(st.shape, st.dtype) for st in stacks],
                  input_output_aliases={a: a for a in range(n)},
                  scratch_shapes=[pltpu.SemaphoreType.DMA((n,)), pltpu.SemaphoreType.DMA((n,))])(*stacks)


def _exchange_small(pack, name):
    nd = 8

    def body(p_ref, o_ref, ssem, rsem):
        x, y, c = _place()
        me = 4 * x + 2 * y + c
        o_ref[me] = p_ref[...]
        cps = []
        for j in range(1, nd):
            px, py, pc = x ^ (j >> 2), y ^ ((j >> 1) & 1), c ^ (j & 1)
            cps.append(pltpu.make_async_remote_copy(
                src_ref=p_ref, dst_ref=o_ref.at[me], send_sem=ssem.at[j - 1], recv_sem=rsem.at[j - 1],
                device_id=(px, py, pc), device_id_type=MESH))
        for cp in cps:
            cp.start()
        for j in range(1, nd):
            peer = me ^ j
            got = o_ref.at[peer]
            pltpu.make_async_remote_copy(src_ref=got, dst_ref=got, send_sem=ssem.at[j - 1], recv_sem=rsem.at[j - 1],
                                         device_id=(x, y, c), device_id_type=MESH).wait_recv()
        for cp in cps:
            cp.wait_send()

    vm = pl.BlockSpec(memory_space=pltpu.VMEM)
    return _pcall(body, name=name, in_specs=[vm], out_specs=vm,
                  out_shape=jax.ShapeDtypeStruct((nd,) + pack.shape, pack.dtype),
                  scratch_shapes=[pltpu.SemaphoreType.DMA((nd - 1,)), pltpu.SemaphoreType.DMA((nd - 1,))])(pack)


def _row(v):
    return v.reshape(1, -1)


def _local_step(x, p, tgt, norm_g, conv_w, conv_b, branch_g, ple_norm_g, b_pg, final_g, w_in, w_out, w_pg, w_pe):
    saved = []
    xl = x
    for l in range(DEPTH):
        h, proj = _inproj(xl, _row(norm_g[l]), w_in, l, f"inproj_{l}")
        ya, tl, walked = _attn_fwd(proj, f"attn_fwd_{l}")
        y = _mix_fwd(proj, ya, conv_w[l], _row(conv_b[l]), _row(branch_g[l]), f"mix_fwd_{l}")
        x1, hn = _outproj(y, w_out, l, xl, _row(ple_norm_g[l]), f"outproj_{l}")
        x2, gate, e = _ple_fwd(hn, w_pg, _row(b_pg[l]), p, w_pe, l, x1, f"ple_fwd_{l}")
        saved.append((xl, h, proj, ya, tl, walked, y, x1, hn, gate, e))
        xl = x2

    sq, dx, d_final = _loss_head(xl, tgt, _row(final_g), "loss_head")

    g_in = g_out = g_pg = g_pe = None
    small = {k: [None] * DEPTH for k in ("norm_g", "conv_w", "conv_b", "branch_g", "ple_norm_g", "b_pg")}
    for l in reversed(range(DEPTH)):
        xl, h, proj, ya, tl, walked, y, x1, hn, gate, e = saved[l]
        du, de, dx1, dy, db_pg, d_ple = _ple_bwd(dx, gate, e, x1, w_pg, _row(ple_norm_g[l]), w_out, l, f"ple_bwd_{l}")
        g_pg = _mm_tn(hn, du, g_pg, l, f"grad_w_pg_{l}")
        g_pe = _mm_tn(p, de, g_pe, l, f"grad_w_pe_{l}", a_stacked=True)
        g_out = _mm_tn(y, dx1, g_out, l, f"grad_w_out_{l}")
        dpc, d_cw, d_cb, d_bg_c = _convmix_bwd(dy, proj, conv_w[l], _row(conv_b[l]), _row(branch_g[l]), f"convmix_bwd_{l}")
        dproj, d_bg_a = _attn_bwd(proj, dy, ya, tl, walked, _row(branch_g[l]), dpc, f"attn_bwd_{l}")
        g_in = _grad_w_in(h, dproj, g_in, l, f"grad_w_in_{l}")
        dx, d_norm = _inproj_bwd(dproj, w_in, l, xl, _row(norm_g[l]), dx1, f"inproj_bwd_{l}")
        small["norm_g"][l] = jnp.sum(d_norm, axis=0)
        small["conv_w"][l] = jnp.sum(d_cw, axis=1)
        small["conv_b"][l] = jnp.sum(d_cb, axis=0)
        small["branch_g"][l] = jnp.concatenate([jnp.sum(d_bg_c, axis=0), jnp.sum(d_bg_a, axis=0)])
        small["ple_norm_g"][l] = jnp.sum(d_ple, axis=0)
        small["b_pg"][l] = jnp.sum(db_pg, axis=0)
    small = {k: jnp.stack(v) for k, v in small.items()}
    small["final_g"] = jnp.sum(d_final, axis=0)
    return sq[0, 0], dx, (g_in, g_out, g_pg, g_pe), small


SMALL_ORDER = ("norm_g", "conv_w", "conv_b", "branch_g", "ple_norm_g", "b_pg", "final_g")


def _pack(parts, width):
    flat = jnp.concatenate([v.reshape(-1) for v in parts])
    rows = -(-flat.shape[0] // width)
    rows = -(-rows // 8) * 8
    return jnp.pad(flat, (0, rows * width - flat.shape[0])).reshape(rows, width)


def _unpack(packed, like):
    flat = packed.reshape(-1)
    out, off = [], 0
    for v in like:
        out.append(flat[off:off + v.size].reshape(v.shape))
        off += v.size
    return out


def kernel(x, p, norm_g, w_in, conv_w, conv_b, branch_g, w_out, ple_norm_g, w_pg, b_pg, w_pe, final_g, loss_target, m_norm_g, m_w_in, m_conv_w, m_conv_b, m_branch_g, m_w_out, m_ple_norm_g, m_w_pg, m_b_pg, m_w_pe, m_final_g, v_norm_g, v_w_in, v_conv_w, v_conv_b, v_branch_g, v_w_out, v_ple_norm_g, v_w_pg, v_b_pg, v_w_pe, v_final_g):
    ix, iy, ic = _place()
    chip = 2 * ix + iy
    d = x.shape[-1]

    big_w = (w_in, w_out, w_pg, w_pe)
    own = [_cast_into_full(w, chip, ax, f"cast_{i}") for i, (w, ax) in enumerate(zip(big_w, SHARD_AXES))]
    full_in, full_out, full_pg, full_pe = _gather_weights(own)
    cw_shard = conv_w.shape[-1]
    cw_slots = _exchange_small(_pack([conv_w], LANES), "exchange_conv_w")
    conv_full = jnp.concatenate([_unpack(cw_slots[2 * k], [conv_w])[0] for k in range(4)], axis=-1)

    sq, dx, big_g, small_g = _local_step(
        x[0], p[:, 0], loss_target[0], norm_g, conv_full, conv_b, branch_g, ple_norm_g, b_pg, final_g,
        full_in, full_out, full_pg, full_pe)

    from_sibling = _swap_layers(big_g)
    chip_sums = [_add_layer(g, o, ic, f"add_layer_{i}") for i, (g, o) in enumerate(zip(big_g, from_sibling))]
    partials = _scatter_shards([narrow for _, narrow in chip_sums])
    reduced = [_sum_shard(wide, pr, chip, ic, ax, f"sum_shard_{i}")
               for i, ((wide, _), pr, ax) in enumerate(zip(chip_sums, partials, SHARD_AXES))]
    g_big = _pair_layers(reduced)

    parts = [small_g[k] for k in SMALL_ORDER] + [sq.reshape(1)]
    slots = _exchange_small(_pack(parts, d), "exchange_small_grads")
    total = _unpack(_sum_slots(slots, "sum_small"), parts)
    g_small = dict(zip(SMALL_ORDER, total[:-1]))
    loss = 0.5 * total[-1][0] / d
    g_small["conv_w"] = lax.dynamic_slice_in_dim(g_small["conv_w"], chip * cw_shard, cw_shard, axis=2)

    grads = dict(g_small)
    grads.update(w_in=g_big[0], w_out=g_big[1], w_pg=g_big[2], w_pe=g_big[3])
    weights = dict(norm_g=norm_g, w_in=w_in, conv_w=conv_w, conv_b=conv_b, branch_g=branch_g, w_out=w_out,
                   ple_norm_g=ple_norm_g, w_pg=w_pg, b_pg=b_pg, w_pe=w_pe, final_g=final_g)
    ms = dict(norm_g=m_norm_g, w_in=m_w_in, conv_w=m_conv_w, conv_b=m_conv_b, branch_g=m_branch_g, w_out=m_w_out,
              ple_norm_g=m_ple_norm_g, w_pg=m_w_pg, b_pg=m_b_pg, w_pe=m_w_pe, final_g=m_final_g)
    vs = dict(norm_g=v_norm_g, w_in=v_w_in, conv_w=v_conv_w, conv_b=v_conv_b, branch_g=v_branch_g, w_out=v_w_out,
              ple_norm_g=v_ple_norm_g, w_pg=v_w_pg, b_pg=v_b_pg, w_pe=v_w_pe, final_g=v_final_g)
    names = ("norm_g", "w_in", "conv_w", "conv_b", "branch_g", "w_out", "ple_norm_g", "w_pg", "b_pg", "w_pe", "final_g")
    delta, new_m, new_v = {}, {}, {}
    for k in ("w_in", "w_out", "w_pg", "w_pe"):
        shp = weights[k].shape
        two = lambda a: a.reshape(-1, shp[-1])
        dl, mn, vn = _adamw(two(weights[k]), two(grads[k]), two(ms[k]), two(vs[k]), f"adamw_{k}")
        delta[k], new_m[k], new_v[k] = dl.reshape(shp), mn.reshape(shp), vn.reshape(shp)
        grads[k] = grads[k].reshape(shp)
    like = [weights[k] for k in SMALL_ORDER]
    packs = [_pack([src[k] for k in SMALL_ORDER], d) for src in (weights, grads, ms, vs)]
    outs = _adamw(*packs, "adamw_small")
    for res, o in zip((delta, new_m, new_v), outs):
        res.update(dict(zip(SMALL_ORDER, _unpack(o, like))))

    return (loss, dx[None], *[grads[k] for k in names], *[delta[k] for k in names],
            *[new_m[k] for k in names], *[new_v[k] for k in names])
```

```python
import math

import jax
import jax.numpy as jnp
from jax import lax
from jax.experimental import pallas as pl
from jax.experimental.pallas import tpu as pltpu

F32 = jnp.float32
BF16 = jnp.bfloat16
EPS = 1e-6
HEAD = 64
LANES = 128
ATT_TK = 256
ATT_TQ = 512
ATT_ROWS = 128
ALIVE_LOG = -105.0
DEPTH = 2
VMEM_LIMIT = 56 * 1024 * 1024
MESH = pl.DeviceIdType.MESH
ANY = pl.BlockSpec(memory_space=pl.ANY)

ADAM_LR = 0.001
ADAM_B1 = 0.9
ADAM_B2 = 0.999
ADAM_EPS = 1e-08
ADAM_WD = 0.01
ADAM_STEP = 10


def _pcall(body, **kw):
    return pl.pallas_call(body, **kw)


def _cp(n_axes):
    return pltpu.CompilerParams(dimension_semantics=("arbitrary",) * n_axes, vmem_limit_bytes=VMEM_LIMIT)


def _tile(n, pref):
    return pref if n % pref == 0 else n


def _split_dot(a, b, passes):
    out = None
    rem = a
    for _ in range(passes):
        hi = rem.astype(BF16)
        t = jnp.dot(hi, b, preferred_element_type=F32)
        out = t if out is None else out + t
        rem = rem - hi.astype(F32)
    return out


def _group_mat():
    r = lax.broadcasted_iota(jnp.int32, (LANES, LANES), 0) // HEAD
    c = lax.broadcasted_iota(jnp.int32, (LANES, LANES), 1) // HEAD
    return jnp.where(r == c, 1.0 / HEAD, 0.0).astype(BF16)


def _group_mean(v, gm):
    return _split_dot(v, gm, 3)


def _sigmoid(z):
    return 1.0 / (1.0 + jnp.exp(-z))


def _dot_nt(a, b):
    return lax.dot_general(a, b, (((1,), (1,)), ((), ())), preferred_element_type=F32)


def _dot_tn(a, b):
    return lax.dot_general(a, b, (((0,), (0,)), ((), ())), preferred_element_type=F32)


def _cast_into_full(w, chip, axis, name):
    _, r, c = w.shape
    tr = _tile(r, 256)
    nb = r // tr
    full = (DEPTH, 4 * r, c) if axis == 0 else (DEPTH, r, 4 * c)

    def body(k_ref, w_ref, o_ref):
        o_ref[...] = w_ref[...].astype(BF16)

    out_map = (lambda l, i, k: (l, k[0] * nb + i, 0)) if axis == 0 else (lambda l, i, k: (l, i, k[0]))
    grid_spec = pltpu.PrefetchScalarGridSpec(
        num_scalar_prefetch=1, grid=(DEPTH, nb),
        in_specs=[pl.BlockSpec((1, tr, c), lambda l, i, k: (l, i, 0))],
        out_specs=pl.BlockSpec((1, tr, c), out_map))
    return _pcall(body, name=name, grid_spec=grid_spec, out_shape=jax.ShapeDtypeStruct(full, BF16),
                  compiler_params=_cp(2))(chip.reshape(1).astype(jnp.int32), w)


def _rms_bwd_rows(dh, xv, g):
    r = lax.rsqrt(jnp.mean(xv * xv, axis=-1, keepdims=True) + EPS)
    xn = xv * r
    dxn = dh * g
    dx = r * (dxn - xn * jnp.mean(dxn * xn, axis=-1, keepdims=True))
    return dx, dh * xn


def _colsum8(v):
    tm, d = v.shape
    return jnp.sum(v.reshape(tm // 8, 8, d), axis=0)


def _inproj(x, g, w, layer, name):
    s, d = x.shape
    n = w.shape[2]
    sw = d // 2
    ns = n // sw
    tm = _tile(s, 512)

    def body(x_ref, g_ref, w_ref, h_ref, o_ref):
        xv = x_ref[...]
        r = lax.rsqrt(jnp.mean(xv * xv, axis=-1, keepdims=True) + EPS)
        h = (xv * r * g_ref[...]).astype(BF16)
        h_ref[...] = h
        for k in range(ns):
            o_ref[k] = jnp.dot(h, w_ref[0, :, k * sw:(k + 1) * sw], preferred_element_type=F32).astype(BF16)

    return _pcall(body, name=name, grid=(s // tm,),
                  in_specs=[pl.BlockSpec((tm, d), lambda m: (m, 0)), pl.BlockSpec((1, d), lambda m: (0, 0)),
                            pl.BlockSpec((1, d, n), lambda m: (layer, 0, 0))],
                  out_specs=[pl.BlockSpec((tm, d), lambda m: (m, 0)), pl.BlockSpec((ns, tm, sw), lambda m: (0, m, 0))],
                  out_shape=[jax.ShapeDtypeStruct((s, d), BF16), jax.ShapeDtypeStruct((ns, s, sw), BF16)],
                  compiler_params=_cp(1))(x, g, w)


def _softplus_parts(z):
    lm = jnp.minimum(-z, 0.0) - jnp.log(1.0 + jnp.exp(-jnp.abs(z)))
    return lm, lm + z


def _attn_tiles(s):
    tk = _tile(s, ATT_TK)
    tq = _tile(s, ATT_TQ)
    return tk, tq, tq // tk, min(ATT_ROWS, tq)


def _diag_work(chains, d, rows, tk):
    work = []
    for n, (_, r0) in enumerate(chains):
        if r0 + rows - 1 <= d * tk:
            continue
        kw = tk // 2 if (tk % 2 == 0 and r0 + rows <= d * tk + tk // 2) else tk
        if r0 >= d * tk + kw:
            mask = None
        else:
            row = lax.broadcasted_iota(jnp.int32, (rows, kw), 0)
            col = lax.broadcasted_iota(jnp.int32, (rows, kw), 1)
            mask = col + d * tk < row + r0
        work.append((n, kw, mask))
    return work


def _walk_work(chains, tk, below=None):
    return [(n, tk, None) for n, (_, r0) in enumerate(chains) if below is None or r0 < below]


def _any_alive(rsums):
    m = rsums[0]
    for r in rsums[1:]:
        m = jnp.maximum(m, r)
    return jnp.max((m > ALIVE_LOG).astype(jnp.int32))


def _attn_fwd(proj, name):
    _, s, sw = proj.shape
    nhp = sw // LANES
    tk, tq, nd, rows = _attn_tiles(s)
    nq = s // tq
    scale = 1.0 / math.sqrt(HEAD)

    def body(q_ref, k_ref, v_ref, o_ref, tl_ref, nw_ref, acc_ref):
        i = pl.program_id(1)
        tri = (lax.broadcasted_iota(jnp.int32, (tk, tk), 0) >
               lax.broadcasted_iota(jnp.int32, (tk, tk), 1)).astype(BF16)
        lane = lax.broadcasted_iota(jnp.int32, (tq, LANES), 1)
        q = q_ref[0] * jnp.asarray(scale, BF16)
        qms = [jnp.where((lane // HEAD) == h, q, jnp.zeros_like(q)) for h in range(2)]
        acc_ref[...] = jnp.zeros_like(acc_ref)
        chains = [(h, r0) for h in range(2) for r0 in range(0, tq, rows)]
        qparts = [qms[h][r0:r0 + rows] for h, r0 in chains]

        def tile(j, rsums, work):
            k0 = pl.multiple_of(j * tk, tk)
            kj = k_ref[0, pl.ds(k0, tk), :]
            vj = v_ref[0, pl.ds(k0, tk), :]
            zs = [_dot_nt(qparts[n], kj[:kw]) for n, kw, _ in work]
            lms, lss, css = [], [], []
            for z, (n, kw, mask) in zip(zs, work):
                lm, ls = _softplus_parts(z)
                if mask is not None:
                    lm = jnp.where(mask, lm, 0.0)
                lms.append(lm)
                lss.append(ls)
                css.append(_split_dot(lm, tri[:kw, :kw], 2))
            out = list(rsums)
            for lm, ls, cs, (n, kw, mask) in zip(lms, lss, css, work):
                h, r0 = chains[n]
                a = jnp.exp(ls + (rsums[n] + cs))
                if mask is not None:
                    a = jnp.where(mask, a, 0.0)
                acc_ref[h, r0:r0 + rows, :] += jnp.dot(a.astype(BF16), vj[:kw], preferred_element_type=F32)
                out[n] = rsums[n] + jnp.sum(lm, axis=1, keepdims=True)
            return tuple(out)

        rsums = (jnp.zeros((rows, 1), F32),) * len(chains)
        for d in reversed(range(nd)):
            rsums = tile(i * nd + d, rsums, _diag_work(chains, d, rows, tk))

        upper = [rs for rs, (_, r0) in zip(rsums, chains) if r0 >= tk]
        lower = [rs for rs, (_, r0) in zip(rsums, chains) if r0 < tk]
        if upper:
            alone = (i > 0) & (_any_alive(upper) == 0) & (_any_alive(lower) > 0)
            rsums = lax.cond(alone, lambda rs: tile(i * nd - 1, rs, _walk_work(chains, tk, below=tk)),
                             lambda rs: rs, rsums)
            alone = alone.astype(jnp.int32)
        else:
            alone = jnp.int32(0)

        def walk(c):
            jj, rs, _ = c
            rs = tile(i * nd - 1 - jj, rs, _walk_work(chains, tk))
            return jj + 1, rs, _any_alive(rs)

        last, rsums, _ = lax.while_loop(lambda c: (c[0] < i * nd) & (c[2] > 0), walk,
                                        (alone, rsums, _any_alive(rsums)))
        for n, (h, r0) in enumerate(chains):
            tl_ref[h, r0:r0 + rows, :] = rsums[n]
        nw_ref[0] = (jnp.zeros((8, LANES), jnp.int32) + (2 * (last - alone) + alone)).astype(F32)
        o_ref[...] = jnp.where(lane < HEAD, acc_ref[0], acc_ref[1]).astype(BF16)

    return _pcall(
        body, name=name, grid=(nhp, nq),
        in_specs=[pl.BlockSpec((1, tq, LANES), lambda hp, i: (4, i, hp)),
                  pl.BlockSpec((1, s, LANES), lambda hp, i: (5, 0, hp)),
                  pl.BlockSpec((1, s, LANES), lambda hp, i: (6, 0, hp))],
        out_specs=[pl.BlockSpec((tq, LANES), lambda hp, i: (i, hp)),
                   pl.BlockSpec((2, tq, 1), lambda hp, i: (hp, i, 0)),
                   pl.BlockSpec((1, 8, LANES), lambda hp, i: (hp * nq + i, 0, 0))],
        out_shape=[jax.ShapeDtypeStruct((s, sw), BF16), jax.ShapeDtypeStruct((2 * nhp, s, 1), F32),
                   jax.ShapeDtypeStruct((nhp * nq, 8, LANES), F32)],
        scratch_shapes=[pltpu.VMEM((2, tq, LANES), F32)],
        compiler_params=_cp(2))(proj, proj, proj)


def _conv_rows(cc_ref, ch_ref, w_ref, b_ref, r, tc):
    r0 = pl.multiple_of(r * tc, tc)
    u = cc_ref[0, pl.ds(r0, tc), :].astype(F32) * ch_ref[0, pl.ds(r0, tc), :].astype(F32)
    p0 = pl.multiple_of(jnp.maximum(r0 - 16, 0), 16)
    up = cc_ref[0, pl.ds(p0, 16), :].astype(F32) * ch_ref[0, pl.ds(p0, 16), :].astype(F32)
    up = up * (r > 0).astype(F32)
    prev1 = up[15:16, :]
    prev2 = up[14:15, :]
    rid = lax.broadcasted_iota(jnp.int32, u.shape, 0)
    s1 = jnp.where(rid == 0, prev1, pltpu.roll(u, 1, axis=0))
    s2 = jnp.where(rid == 0, prev2, jnp.where(rid == 1, prev1, pltpu.roll(u, 2, axis=0)))
    cv = b_ref[...] + s2 * w_ref[0:1, :] + s1 * w_ref[1:2, :] + u * w_ref[2:3, :]
    return r0, u, s1, s2, cv


def _mix_fwd(proj, ya, conv_w, conv_b, bg, name):
    _, s, sw = proj.shape
    nh = sw // LANES
    tc = _tile(s, 256)

    def body(cb_ref, cc_ref, ch_ref, cz_ref, ya_ref, az_ref, w_ref, b_ref, g_ref, y_ref):
        c = pl.program_id(0)
        gm = _group_mat()

        def finish(r0, yv, zg):
            n = yv * lax.rsqrt(_group_mean(yv * yv, gm) + EPS)
            y_ref[pl.ds(r0, tc), :] = (n * g_ref[...] * (zg * _sigmoid(zg))).astype(BF16)

        @pl.when(c < nh)
        def _():
            def step(r, carry):
                r0, _, _, _, cv = _conv_rows(cc_ref, ch_ref, w_ref, b_ref, r, tc)
                yc = cb_ref[0, pl.ds(r0, tc), :].astype(F32) * cv
                finish(r0, yc, cz_ref[0, pl.ds(r0, tc), :].astype(F32))
                return carry
            lax.fori_loop(0, s // tc, step, 0)

        @pl.when(c >= nh)
        def _():
            def step(r, carry):
                r0 = pl.multiple_of(r * tc, tc)
                finish(r0, ya_ref[pl.ds(r0, tc), :].astype(F32), az_ref[0, pl.ds(r0, tc), :].astype(F32))
                return carry
            lax.fori_loop(0, s // tc, step, 0)

    def sec(k):
        return pl.BlockSpec((1, s, LANES), lambda c: (k, 0, jnp.minimum(c, nh - 1)))

    return _pcall(
        body, name=name, grid=(2 * nh,),
        in_specs=[sec(0), sec(1), sec(2), sec(3),
                  pl.BlockSpec((s, LANES), lambda c: (0, jnp.maximum(c - nh, 0))),
                  pl.BlockSpec((1, s, LANES), lambda c: (7, 0, jnp.maximum(c - nh, 0))),
                  pl.BlockSpec((3, LANES), lambda c: (0, jnp.minimum(c, nh - 1))),
                  pl.BlockSpec((1, LANES), lambda c: (0, jnp.minimum(c, nh - 1))),
                  pl.BlockSpec((1, LANES), lambda c: (0, c))],
        out_specs=pl.BlockSpec((s, LANES), lambda c: (0, c)),
        out_shape=jax.ShapeDtypeStruct((s, 2 * sw), BF16), compiler_params=_cp(1),
    )(proj, proj, proj, proj, ya, proj, conv_w, conv_b, bg)


def _outproj(y, w, layer, x, g, name):
    s, d = x.shape
    tm = _tile(s, 256)

    def body(y_ref, w_ref, x_ref, g_ref, x1_ref, hn_ref):
        x1 = x_ref[...] + jnp.dot(y_ref[...], w_ref[0], preferred_element_type=F32)
        x1_ref[...] = x1
        r = lax.rsqrt(jnp.mean(x1 * x1, axis=-1, keepdims=True) + EPS)
        hn_ref[...] = (x1 * r * g_ref[...]).astype(BF16)

    row = lambda m: (m, 0)
    fix = lambda m: (0, 0)
    return _pcall(body, name=name, grid=(s // tm,),
                  in_specs=[pl.BlockSpec((tm, d), row), pl.BlockSpec((1, d, d), lambda m: (layer, 0, 0)),
                            pl.BlockSpec((tm, d), row), pl.BlockSpec((1, d), fix)],
                  out_specs=[pl.BlockSpec((tm, d), row), pl.BlockSpec((tm, d), row)],
                  out_shape=[jax.ShapeDtypeStruct((s, d), F32), jax.ShapeDtypeStruct((s, d), BF16)],
                  compiler_params=_cp(1))(y, w, x, g)


def _ple_fwd(hn, w_pg, b_pg, p, w_pe, layer, x1, name):
    s, d = x1.shape
    pd = p.shape[2]
    tm = _tile(s, 256)

    def body(hn_ref, wg_ref, b_ref, p_ref, we_ref, x1_ref, x2_ref, gate_ref, e_ref):
        gate = _sigmoid(jnp.dot(hn_ref[...], wg_ref[0], preferred_element_type=F32) + b_ref[...])
        e = jnp.dot(p_ref[0].astype(BF16), we_ref[0], preferred_element_type=F32)
        x2_ref[...] = x1_ref[...] + gate * e
        gate_ref[...] = gate.astype(BF16)
        e_ref[...] = e.astype(BF16)

    row = lambda m: (m, 0)
    fix = lambda m: (0, 0)
    return _pcall(body, name=name, grid=(s // tm,),
                  in_specs=[pl.BlockSpec((tm, d), row), pl.BlockSpec((1, d, d), lambda m: (layer, 0, 0)),
                            pl.BlockSpec((1, d), fix), pl.BlockSpec((1, tm, pd), lambda m: (layer, m, 0)),
                            pl.BlockSpec((1, pd, d), lambda m: (layer, 0, 0)), pl.BlockSpec((tm, d), row)],
                  out_specs=[pl.BlockSpec((tm, d), row)] * 3,
                  out_shape=[jax.ShapeDtypeStruct((s, d), F32), jax.ShapeDtypeStruct((s, d), BF16),
                             jax.ShapeDtypeStruct((s, d), BF16)],
                  compiler_params=_cp(1))(hn, w_pg, b_pg, p, w_pe, x1)


def _loss_head(x, tgt, g, name):
    s, d = x.shape
    tm = _tile(s, 256)

    def body(x_ref, t_ref, g_ref, l_ref, dx_ref, dg_ref):
        m = pl.program_id(0)

        @pl.when(m == 0)
        def _():
            l_ref[...] = jnp.zeros_like(l_ref)
            dg_ref[...] = jnp.zeros_like(dg_ref)

        xv = x_ref[...]
        gv = g_ref[...]
        r = lax.rsqrt(jnp.mean(xv * xv, axis=-1, keepdims=True) + EPS)
        xn = xv * r
        err = xn * gv - t_ref[...]
        l_ref[...] += jnp.sum(err * err)
        dy = err * (1.0 / d)
        dxn = dy * gv
        dx_ref[...] = r * (dxn - xn * jnp.mean(dxn * xn, axis=-1, keepdims=True))
        dg_ref[...] += _colsum8(dy * xn)

    row = lambda m: (m, 0)
    fix = lambda m: (0, 0)
    return _pcall(body, name=name, grid=(s // tm,),
                  in_specs=[pl.BlockSpec((tm, d), row), pl.BlockSpec((tm, d), row), pl.BlockSpec((1, d), fix)],
                  out_specs=[pl.BlockSpec((8, LANES), fix), pl.BlockSpec((tm, d), row), pl.BlockSpec((8, d), fix)],
                  out_shape=[jax.ShapeDtypeStruct((8, LANES), F32), jax.ShapeDtypeStruct((s, d), F32),
                             jax.ShapeDtypeStruct((8, d), F32)],
                  compiler_params=_cp(1))(x, tgt, g)


def _ple_bwd(dx2, gate, e, x1, w_pg, g_ple, w_out, layer, name):
    s, d = dx2.shape
    tm = _tile(s, 256)

    def body(dx2_ref, gate_ref, e_ref, x1_ref, wg_ref, g_ref, wo_ref,
             du_ref, de_ref, dx1_ref, dy_ref, db_ref, dg_ref):
        m = pl.program_id(0)

        @pl.when(m == 0)
        def _():
            db_ref[...] = jnp.zeros_like(db_ref)
            dg_ref[...] = jnp.zeros_like(dg_ref)

        dx2v = dx2_ref[...]
        gate = gate_ref[...].astype(F32)
        du = dx2v * e_ref[...].astype(F32) * gate * (1.0 - gate)
        de_ref[...] = (dx2v * gate).astype(BF16)
        dub = du.astype(BF16)
        du_ref[...] = dub
        db_ref[...] += _colsum8(du)
        dhn = _dot_nt(dub, wg_ref[0])
        dxr, dgr = _rms_bwd_rows(dhn, x1_ref[...], g_ref[...])
        dx1 = dx2v + dxr
        dx1_ref[...] = dx1
        dg_ref[...] += _colsum8(dgr)
        dy_ref[...] = _dot_nt(dx1.astype(BF16), wo_ref[0]).astype(BF16)

    row = lambda m: (m, 0)
    fix = lambda m: (0, 0)
    t = pl.BlockSpec((tm, d), row)
    return _pcall(body, name=name, grid=(s // tm,),
                  in_specs=[t, t, t, t, pl.BlockSpec((1, d, d), lambda m: (layer, 0, 0)), pl.BlockSpec((1, d), fix),
                            pl.BlockSpec((1, d, d), lambda m: (layer, 0, 0))],
                  out_specs=[t, t, t, t, pl.BlockSpec((8, d), fix), pl.BlockSpec((8, d), fix)],
                  out_shape=[jax.ShapeDtypeStruct((s, d), BF16), jax.ShapeDtypeStruct((s, d), BF16),
                             jax.ShapeDtypeStruct((s, d), F32), jax.ShapeDtypeStruct((s, d), BF16),
                             jax.ShapeDtypeStruct((8, d), F32), jax.ShapeDtypeStruct((8, d), F32)],
                  compiler_params=_cp(1))(dx2, gate, e, x1, w_pg, g_ple, w_out)


def _mm_tn(a, b, stack, layer, name, a_stacked=False):
    s, ka = a.shape[-2:]
    n = b.shape[1]
    tn = _tile(n, 1024)
    ns = n // tn
    tk = _tile(s, 512)
    nk = s // tk

    def body(*refs):
        a_ref, b_ref = refs[0], refs[1]
        o_ref, acc_ref = refs[-2], refs[-1]
        k = pl.program_id(1)

        @pl.when(k == 0)
        def _():
            acc_ref[...] = jnp.zeros_like(acc_ref)

        av = a_ref[0] if a_stacked else a_ref[...]
        acc_ref[...] += _dot_tn(av.astype(BF16), b_ref[...].astype(BF16))

        @pl.when(k == nk - 1)
        def _():
            o_ref[0] = acc_ref[...]

    a_spec = (pl.BlockSpec((1, tk, ka), lambda j, k: (layer, k, 0)) if a_stacked
              else pl.BlockSpec((tk, ka), lambda j, k: (k, 0)))
    in_specs = [a_spec, pl.BlockSpec((tk, tn), lambda j, k: (k, j))]
    args = [a, b]
    aliases = {}
    if stack is not None:
        in_specs.append(ANY)
        args.append(stack)
        aliases = {2: 0}
    return _pcall(body, name=name, grid=(ns, nk), in_specs=in_specs,
                  out_specs=pl.BlockSpec((1, ka, tn), lambda j, k: (layer, 0, j)),
                  out_shape=jax.ShapeDtypeStruct((DEPTH, ka, n), F32),
                  scratch_shapes=[pltpu.VMEM((ka, tn), F32)], input_output_aliases=aliases,
                  compiler_params=_cp(2))(*args)


def _norm_gate_bwd(dy, yv, zg, g, gm):
    r = lax.rsqrt(_group_mean(yv * yv, gm) + EPS)
    n = yv * r
    sg = _sigmoid(zg)
    sil = zg * sg
    dzg = dy * n * g * (sg * (1.0 + zg * (1.0 - sg)))
    dn = dy * g * sil
    dyv = r * (dn - n * _group_mean(dn * n, gm))
    return dyv, dzg, dy * n * sil


def _convmix_bwd(dy, proj, conv_w, conv_b, bg, name):
    _, s, sw = proj.shape
    nh = sw // LANES
    tc = _tile(s, 256)
    nr = s // tc

    def body(dy_ref, cb_ref, cc_ref, ch_ref, cz_ref, w_ref, b_ref, g_ref,
             dp_ref, dw_ref, db_ref, dg_ref, dcv_ref):
        gm = _group_mat()
        dcv_ref[pl.ds(s, 8), :] = jnp.zeros((8, LANES), F32)

        def pass1(r, carry):
            dw0, dw1, dw2, db, dg = carry
            r0, u, s1, s2, cv = _conv_rows(cc_ref, ch_ref, w_ref, b_ref, r, tc)
            cb = cb_ref[0, pl.ds(r0, tc), :].astype(F32)
            dyc, dcz, dgr = _norm_gate_bwd(dy_ref[pl.ds(r0, tc), :].astype(F32), cb * cv,
                                           cz_ref[0, pl.ds(r0, tc), :].astype(F32), g_ref[...], gm)
            dp_ref[0, pl.ds(r0, tc), :] = (dyc * cv).astype(BF16)
            dp_ref[3, pl.ds(r0, tc), :] = dcz.astype(BF16)
            dcv = dyc * cb
            dcv_ref[pl.ds(r0, tc), :] = dcv
            return (dw0 + _colsum8(dcv * s2), dw1 + _colsum8(dcv * s1), dw2 + _colsum8(dcv * u),
                    db + _colsum8(dcv), dg + _colsum8(dgr))

        z8 = jnp.zeros((8, LANES), F32)
        dw0, dw1, dw2, db, dg = lax.fori_loop(0, nr, pass1, (z8, z8, z8, z8, z8))
        dw_ref[0] = dw0
        dw_ref[1] = dw1
        dw_ref[2] = dw2
        db_ref[...] = db
        dg_ref[...] = dg

        def pass2(r, carry):
            r0 = pl.multiple_of(r * tc, tc)
            dcv = dcv_ref[pl.ds(r0, tc), :]
            nxt = dcv_ref[pl.ds(pl.multiple_of(r0 + tc, 8), 8), :]
            rid = lax.broadcasted_iota(jnp.int32, dcv.shape, 0)
            n1 = jnp.where(rid == tc - 1, nxt[0:1, :], pltpu.roll(dcv, tc - 1, axis=0))
            n2 = jnp.where(rid == tc - 1, nxt[1:2, :],
                           jnp.where(rid == tc - 2, nxt[0:1, :], pltpu.roll(dcv, tc - 2, axis=0)))
            du = dcv * w_ref[2:3, :] + n1 * w_ref[1:2, :] + n2 * w_ref[0:1, :]
            dp_ref[1, pl.ds(r0, tc), :] = (du * ch_ref[0, pl.ds(r0, tc), :].astype(F32)).astype(BF16)
            dp_ref[2, pl.ds(r0, tc), :] = (du * cc_ref[0, pl.ds(r0, tc), :].astype(F32)).astype(BF16)
            return carry

        lax.fori_loop(0, nr, pass2, 0)

    def sec(k):
        return pl.BlockSpec((1, s, LANES), lambda c: (k, 0, c))

    col = lambda c: (0, c)
    return _pcall(
        body, name=name, grid=(nh,),
        in_specs=[pl.BlockSpec((s, LANES), col), sec(0), sec(1), sec(2), sec(3),
                  pl.BlockSpec((3, LANES), col), pl.BlockSpec((1, LANES), col), pl.BlockSpec((1, LANES), col)],
        out_specs=[pl.BlockSpec((4, s, LANES), lambda c: (0, 0, c)), pl.BlockSpec((3, 8, LANES), lambda c: (0, 0, c)),
                   pl.BlockSpec((8, LANES), col), pl.BlockSpec((8, LANES), col)],
        out_shape=[jax.ShapeDtypeStruct((8, s, sw), BF16), jax.ShapeDtypeStruct((3, 8, sw), F32),
                   jax.ShapeDtypeStruct((8, sw), F32), jax.ShapeDtypeStruct((8, sw), F32)],
        scratch_shapes=[pltpu.VMEM((s + 8, LANES), F32)], compiler_params=_cp(1),
    )(dy, proj, proj, proj, proj, conv_w, conv_b, bg)


def _attn_bwd(proj, dy, ya, tl, walked, bg, buf, name):
    _, s, sw = proj.shape
    nhp = sw // LANES
    tk, t, nd, rows_c = _attn_tiles(s)
    nq = s // t
    scale = 1.0 / math.sqrt(HEAD)

    def body(q_ref, k_ref, v_ref, az_ref, dy_ref, ya_ref, tl_ref, nw_ref, g_ref, buf_ref, out_ref, dg_ref,
             dka_ref, dva_ref, dqa_ref):
        step = pl.program_id(1)
        i = nq - 1 - step

        @pl.when(step == 0)
        def _():
            dka_ref[...] = jnp.zeros_like(dka_ref)
            dva_ref[...] = jnp.zeros_like(dva_ref)
            dg_ref[...] = jnp.zeros_like(dg_ref)

        dyv, dzg, dgr = _norm_gate_bwd(dy_ref[...].astype(F32), ya_ref[...].astype(F32), az_ref[0].astype(F32),
                                       g_ref[...], _group_mat())
        out_ref[3] = dzg.astype(BF16)
        dg_ref[...] += _colsum8(dgr)

        tri = (lax.broadcasted_iota(jnp.int32, (tk, tk), 0) <=
               lax.broadcasted_iota(jnp.int32, (tk, tk), 1)).astype(BF16)
        lane = lax.broadcasted_iota(jnp.int32, (t, LANES), 1)
        q = q_ref[0] * jnp.asarray(scale, BF16)
        do = dyv.astype(BF16)
        qms = [jnp.where((lane // HEAD) == h, q, jnp.zeros_like(q)) for h in range(2)]
        doms = [jnp.where((lane // HEAD) == h, do, jnp.zeros_like(do)) for h in range(2)]
        dqa_ref[...] = jnp.zeros_like(dqa_ref)
        chains = [(h, r0) for h in range(2) for r0 in range(0, t, rows_c)]
        qparts = [qms[h][r0:r0 + rows_c] for h, r0 in chains]
        doparts = [doms[h][r0:r0 + rows_c] for h, r0 in chains]
        tots = [tl_ref[h, r0:r0 + rows_c, :] for h, r0 in chains]

        def tile(j, carry, work):
            k0 = pl.multiple_of(j * tk, tk)
            kj = k_ref[0, pl.ds(k0, tk), :]
            vj = v_ref[0, pl.ds(k0, tk), :]
            zs = [_dot_nt(qparts[n], kj[:kw]) for n, kw, _ in work]
            das = [_dot_nt(doparts[n], vj[:kw]) for n, kw, _ in work]
            lms, lss, cls = [], [], []
            for z, (n, kw, mask) in zip(zs, work):
                lm, ls = _softplus_parts(z)
                if mask is not None:
                    lm = jnp.where(mask, lm, 0.0)
                lms.append(lm)
                lss.append(ls)
                cls.append(_split_dot(lm, tri[:kw, :kw], 2))
            abs_, gs, cgs = [], [], []
            for ls, cl, da, (n, kw, mask) in zip(lss, cls, das, work):
                a = jnp.exp(ls + (tots[n] - carry[n][0] - cl))
                if mask is not None:
                    a = jnp.where(mask, a, 0.0)
                g = a * da
                gs.append(g)
                abs_.append(a.astype(BF16))
                cgs.append(_split_dot(g, tri[:kw, :kw], 1))
            out = list(carry)
            dks, dvs = {}, {}
            for lm, ls, a, g, cg, (n, kw, mask) in zip(lms, lss, abs_, gs, cgs, work):
                h, r0 = chains[n]
                psum, gsum = carry[n]
                dz = g - jnp.exp(ls) * (gsum + cg)
                if mask is not None:
                    dz = jnp.where(mask, dz, 0.0)
                dz = dz.astype(BF16)
                dqa_ref[h, r0:r0 + rows_c, :] += jnp.dot(dz, kj[:kw], preferred_element_type=F32)
                dkh = _dot_tn(dz, qparts[n])
                dvh = _dot_tn(a, doparts[n])
                dks[kw] = dkh if kw not in dks else dks[kw] + dkh
                dvs[kw] = dvh if kw not in dvs else dvs[kw] + dvh
                out[n] = (psum + jnp.sum(lm, axis=1, keepdims=True), gsum + jnp.sum(g, axis=1, keepdims=True))
            for kw in dks:
                dka_ref[pl.ds(k0, kw), :] += dks[kw]
                dva_ref[pl.ds(k0, kw), :] += dvs[kw]
            return tuple(out)

        z1 = jnp.zeros((rows_c, 1), F32)
        code = jnp.clip(jnp.max(nw_ref[0].astype(jnp.int32)), 0, 2 * i * nd + 1)
        alone = jnp.minimum(code % 2, i * nd)
        whole = jnp.minimum(code // 2, i * nd - alone)
        carry = lax.fori_loop(i * nd - alone - whole, i * nd - alone,
                              lambda j, c: tile(j, c, _walk_work(chains, tk)), ((z1, z1),) * len(chains))
        if any(r0 >= tk for _, r0 in chains):
            carry = lax.cond(alone > 0, lambda c: tile(i * nd - 1, c, _walk_work(chains, tk, below=tk)),
                             lambda c: c, carry)
        for d in range(nd):
            carry = tile(i * nd + d, carry, _diag_work(chains, d, rows_c, tk))
        out_ref[0] = (jnp.where(lane < HEAD, dqa_ref[0], dqa_ref[1]) * scale).astype(BF16)
        own = pl.multiple_of(i * t, t)
        out_ref[1] = dka_ref[pl.ds(own, t), :].astype(BF16)
        out_ref[2] = dva_ref[pl.ds(own, t), :].astype(BF16)

    def rows(sec):
        return pl.BlockSpec((1, t, LANES), lambda hp, st: (sec, nq - 1 - st, hp))

    def whole(sec):
        return pl.BlockSpec((1, s, LANES), lambda hp, st: (sec, 0, hp))

    return _pcall(
        body, name=name, grid=(nhp, nq),
        in_specs=[rows(4), whole(5), whole(6), rows(7),
                  pl.BlockSpec((t, LANES), lambda hp, st: (nq - 1 - st, hp + nhp)),
                  pl.BlockSpec((t, LANES), lambda hp, st: (nq - 1 - st, hp)),
                  pl.BlockSpec((2, t, 1), lambda hp, st: (hp, nq - 1 - st, 0)),
                  pl.BlockSpec((1, 8, LANES), lambda hp, st: (hp * nq + nq - 1 - st, 0, 0)),
                  pl.BlockSpec((1, LANES), lambda hp, st: (0, hp + nhp)), ANY],
        out_specs=[pl.BlockSpec((4, t, LANES), lambda hp, st: (1, nq - 1 - st, hp)),
                   pl.BlockSpec((8, LANES), lambda hp, st: (0, hp))],
        out_shape=[jax.ShapeDtypeStruct(buf.shape, buf.dtype), jax.ShapeDtypeStruct((8, sw), F32)],
        input_output_aliases={9: 0},
        scratch_shapes=[pltpu.VMEM((s, LANES), F32), pltpu.VMEM((s, LANES), F32), pltpu.VMEM((2, t, LANES), F32)],
        compiler_params=_cp(2))(proj, proj, proj, proj, dy, ya, tl, walked, bg, buf)


def _grad_w_in(h, dproj, stack, layer, name):
    s, d = h.shape
    ns, _, sw = dproj.shape

    def body(*refs):
        h_ref, b_ref = refs[0], refs[1]
        o_ref, ht_ref = refs[-2], refs[-1]

        @pl.when(pl.program_id(0) == 0)
        def _():
            ht_ref[...] = h_ref[...].T

        o_ref[0] = jnp.dot(ht_ref[...], b_ref[0], preferred_element_type=F32)

    in_specs = [pl.BlockSpec((s, d), lambda j: (0, 0)), pl.BlockSpec((1, s, sw), lambda j: (j, 0, 0))]
    args = [h, dproj]
    aliases = {}
    if stack is not None:
        in_specs.append(ANY)
        args.append(stack)
        aliases = {2: 0}
    return _pcall(body, name=name, grid=(ns,), in_specs=in_specs,
                  out_specs=pl.BlockSpec((1, d, sw), lambda j: (layer, 0, j)),
                  out_shape=jax.ShapeDtypeStruct((DEPTH, d, ns * sw), F32),
                  scratch_shapes=[pltpu.VMEM((d, s), BF16)], input_output_aliases=aliases,
                  compiler_params=_cp(1))(*args)


def _inproj_bwd(dproj, w, layer, x, g, dx1, name):
    ns, s, sw = dproj.shape
    d = x.shape[1]
    tm = _tile(s, 256)

    def body(dp_ref, w_ref, x_ref, g_ref, dx1_ref, dx_ref, dg_ref):
        @pl.when(pl.program_id(0) == 0)
        def _():
            dg_ref[...] = jnp.zeros_like(dg_ref)

        dh = _dot_nt(dp_ref[0], w_ref[0, :, 0:sw])
        for k in range(1, ns):
            dh = dh + _dot_nt(dp_ref[k], w_ref[0, :, k * sw:(k + 1) * sw])
        dxr, dgr = _rms_bwd_rows(dh, x_ref[...], g_ref[...])
        dx_ref[...] = dx1_ref[...] + dxr
        dg_ref[...] += _colsum8(dgr)

    row = lambda m: (m, 0)
    fix = lambda m: (0, 0)
    return _pcall(body, name=name, grid=(s // tm,),
                  in_specs=[pl.BlockSpec((ns, tm, sw), lambda m: (0, m, 0)),
                            pl.BlockSpec((1, d, ns * sw), lambda m: (layer, 0, 0)),
                            pl.BlockSpec((tm, d), row), pl.BlockSpec((1, d), fix), pl.BlockSpec((tm, d), row)],
                  out_specs=[pl.BlockSpec((tm, d), row), pl.BlockSpec((8, d), fix)],
                  out_shape=[jax.ShapeDtypeStruct((s, d), F32), jax.ShapeDtypeStruct((8, d), F32)],
                  compiler_params=_cp(1))(dproj, w, x, g, dx1)


def _adamw(w, g, m, v, name):
    r, c = w.shape
    tr = _tile(r, 256)
    c1 = 1.0 - ADAM_B1 ** ADAM_STEP
    c2 = 1.0 - ADAM_B2 ** ADAM_STEP

    def body(w_ref, g_ref, m_ref, v_ref, d_ref, mo_ref, vo_ref):
        gv = g_ref[...]
        mn = ADAM_B1 * m_ref[...] + (1.0 - ADAM_B1) * gv
        vn = ADAM_B2 * v_ref[...] + (1.0 - ADAM_B2) * (gv * gv)
        d_ref[...] = -ADAM_LR * ((mn / c1) / (jnp.sqrt(vn / c2) + ADAM_EPS) + ADAM_WD * w_ref[...])
        mo_ref[...] = mn
        vo_ref[...] = vn

    t = pl.BlockSpec((tr, c), lambda i: (i, 0))
    return _pcall(body, name=name, grid=(r // tr,), in_specs=[t] * 4, out_specs=[t] * 3,
                  out_shape=[jax.ShapeDtypeStruct((r, c), F32)] * 3, compiler_params=_cp(1))(w, g, m, v)


def _add_layer(stack, other, layer, name):
    _, r, c = stack.shape
    tr = _tile(r, 256)

    def body(l_ref, s_ref, o_ref, out_ref, outb_ref):
        v = s_ref[0] + o_ref[...]
        out_ref[...] = v
        outb_ref[...] = v.astype(BF16)

    t = pl.BlockSpec((tr, c), lambda i, l: (i, 0))
    grid_spec = pltpu.PrefetchScalarGridSpec(
        num_scalar_prefetch=1, grid=(r // tr,),
        in_specs=[pl.BlockSpec((1, tr, c), lambda i, l: (l[0], i, 0)), t], out_specs=[t, t])
    return _pcall(body, name=name, grid_spec=grid_spec,
                  out_shape=[jax.ShapeDtypeStruct((r, c), F32), jax.ShapeDtypeStruct((r, c), BF16)],
                  compiler_params=_cp(1))(layer.reshape(1).astype(jnp.int32), stack, other)


def _sum_shard(full, parts, chip, layer, axis, name):
    _, r, c = parts.shape
    tr = _tile(r, 256)

    def body(k_ref, f_ref, p_ref, out_ref):
        out_ref[0] = ((f_ref[...] + p_ref[0].astype(F32)) + p_ref[1].astype(F32)) + p_ref[2].astype(F32)

    if axis == 1:
        f_spec = pl.BlockSpec((tr, c), lambda i, k: (i, k[0]))
    else:
        nb = r // tr
        f_spec = pl.BlockSpec((tr, c), lambda i, k: (k[0] * nb + i, 0))
    grid_spec = pltpu.PrefetchScalarGridSpec(
        num_scalar_prefetch=1, grid=(r // tr,),
        in_specs=[f_spec, pl.BlockSpec((3, tr, c), lambda i, k: (0, i, 0))],
        out_specs=pl.BlockSpec((1, tr, c), lambda i, k: (k[1], i, 0)))
    return _pcall(body, name=name, grid_spec=grid_spec, out_shape=jax.ShapeDtypeStruct((DEPTH, r, c), F32),
                  compiler_params=_cp(1))(jnp.stack([chip, layer]).astype(jnp.int32), full, parts)


def _sum_slots(slots, name):
    n = slots.shape[0]

    def body(s_ref, o_ref):
        acc = s_ref[0]
        for i in range(1, n):
            acc = acc + s_ref[i]
        o_ref[...] = acc

    return _pcall(body, name=name, out_shape=jax.ShapeDtypeStruct(slots.shape[1:], F32))(slots)


def _place():
    return lax.axis_index("x"), lax.axis_index("y"), lax.axis_index("c")


def _shard_view(ref, axis, chip, size):
    if axis == 0:
        return ref.at[pl.ds(chip * size, size), :]
    return ref.at[:, pl.ds(chip * size, size)]


SHARD_AXES = (1, 0, 0, 1)


def _gather_weights(fulls):
    n = len(fulls)
    sizes = [f.shape[1 + ax] // 4 for f, ax in zip(fulls, SHARD_AXES)]

    def body(*refs):
        ins, outs = refs[:n], refs[n:2 * n]
        ssem, rsem = refs[2 * n:]
        x, y, c = _place()
        me = 2 * x + y
        chips = [(1 - x, y), (x, 1 - y), (1 - x, 1 - y)]

        def piece(a, layer, chip, of=outs):
            return _shard_view(of[a].at[layer], SHARD_AXES[a], chip, sizes[a])

        sends = []
        for a in range(n):
            for j, (cx, cy) in enumerate(chips):
                cp = pltpu.make_async_remote_copy(
                    src_ref=piece(a, c, me, ins), dst_ref=piece(a, c, me), send_sem=ssem.at[a, j],
                    recv_sem=rsem.at[a, j], device_id=(cx, cy, c), device_id_type=MESH)
                cp.start()
                sends.append(cp)
        for a in range(n):
            for j, (cx, cy) in enumerate(chips):
                got = piece(a, c, 2 * cx + cy)
                pltpu.make_async_remote_copy(
                    src_ref=got, dst_ref=got, send_sem=ssem.at[a, j], recv_sem=rsem.at[a, j],
                    device_id=(cx, cy, c), device_id_type=MESH).wait_recv()
                cp = pltpu.make_async_remote_copy(
                    src_ref=got, dst_ref=got, send_sem=ssem.at[a, 3 + j], recv_sem=rsem.at[a, 3 + j],
                    device_id=(x, y, 1 - c), device_id_type=MESH)
                cp.start()
                sends.append(cp)
        for a in range(n):
            for j, (cx, cy) in enumerate(chips):
                got = piece(a, 1 - c, 2 * cx + cy)
                pltpu.make_async_remote_copy(
                    src_ref=got, dst_ref=got, send_sem=ssem.at[a, 3 + j], recv_sem=rsem.at[a, 3 + j],
                    device_id=(x, y, 1 - c), device_id_type=MESH).wait_recv()
        for cp in sends:
            cp.wait_send()

    return _pcall(body, name="gather_weights", in_specs=[ANY] * n, out_specs=[ANY] * n,
                  out_shape=[jax.ShapeDtypeStruct(f.shape, f.dtype) for f in fulls],
                  input_output_aliases={a: a for a in range(n)},
                  scratch_shapes=[pltpu.SemaphoreType.DMA((n, 6)), pltpu.SemaphoreType.DMA((n, 6))])(*fulls)


def _swap_layers(stacks):
    n = len(stacks)

    def body(*refs):
        srcs, outs = refs[:n], refs[n:2 * n]
        ssem, rsem = refs[2 * n:]
        x, y, c = _place()
        cps = [pltpu.make_async_remote_copy(src_ref=srcs[a].at[1 - c], dst_ref=outs[a], send_sem=ssem.at[a],
                                            recv_sem=rsem.at[a], device_id=(x, y, 1 - c), device_id_type=MESH)
               for a in range(n)]
        for cp in cps:
            cp.start()
        for cp in cps:
            cp.wait()

    return _pcall(body, name="swap_layers", in_specs=[ANY] * n, out_specs=[ANY] * n,
                  out_shape=[jax.ShapeDtypeStruct(st.shape[1:], st.dtype) for st in stacks],
                  scratch_shapes=[pltpu.SemaphoreType.DMA((n,)), pltpu.SemaphoreType.DMA((n,))])(*stacks)


def _scatter_shards(fulls):
    n = len(fulls)
    shard_shapes = []
    for f, ax in zip(fulls, SHARD_AXES):
        sh = list(f.shape)
        sh[ax] //= 4
        shard_shapes.append(tuple(sh))

    def body(*refs):
        srcs, outs = refs[:n], refs[n:2 * n]
        ssem, rsem = refs[2 * n:]
        x, y, c = _place()
        chips = [(1 - x, y), (x, 1 - y), (1 - x, 1 - y)]
        cps = []
        for a in range(n):
            for j, (cx, cy) in enumerate(chips):
                src = _shard_view(srcs[a], SHARD_AXES[a], 2 * cx + cy, shard_shapes[a][SHARD_AXES[a]])
                cps.append(pltpu.make_async_remote_copy(
                    src_ref=src, dst_ref=outs[a].at[j], send_sem=ssem.at[a, j], recv_sem=rsem.at[a, j],
                    device_id=(cx, cy, c), device_id_type=MESH))
        for cp in cps:
            cp.start()
        for cp in cps:
            cp.wait()

    return _pcall(body, name="scatter_shards", in_specs=[ANY] * n, out_specs=[ANY] * n,
                  out_shape=[jax.ShapeDtypeStruct((3,) + sh, f.dtype) for sh, f in zip(shard_shapes, fulls)],
                  scratch_shapes=[pltpu.SemaphoreType.DMA((n, 3)), pltpu.SemaphoreType.DMA((n, 3))])(*fulls)


def _pair_layers(stacks):
    n = len(stacks)

    def body(*refs):
        ins, outs = refs[:n], refs[n:2 * n]
        ssem, rsem = refs[2 * n:]
        x, y, c = _place()
        cps = [pltpu.make_async_remote_copy(src_ref=ins[a].at[c], dst_ref=outs[a].at[c], send_sem=ssem.at[a],
                                            recv_sem=rsem.at[a], device_id=(x, y, 1 - c), device_id_type=MESH)
               for a in range(n)]
        for cp in cps:
            cp.start()
        for a in range(n):
            got = outs[a].at[1 - c]
            pltpu.make_async_remote_copy(src_ref=got, dst_ref=got, send_sem=ssem.at[a], recv_sem=rsem.at[a],
                                         device_id=(x, y, 1 - c), device_id_type=MESH).wait_recv()
        for cp in cps:
            cp.wait_send()

    return _pcall(body, name="pair_layers", in_specs=[ANY] * n, out_specs=[ANY] * n,
                  out_shape=[jax.ShapeDtypeStruct(st.shape, st.dtype) for st in stacks],
                  input_output_aliases={a: a for a in range(n)},
                  scratch_shapes=[pltpu.SemaphoreType.DMA((n,)), pltpu.SemaphoreType.DMA((n,))])(*stacks)


def _exchange_small(pack, name):
    nd = 8

    def body(p_ref, o_ref, ssem, rsem):
        x, y, c = _place()
        me = 4 * x + 2 * y + c
        o_ref[me] = p_ref[...]
        cps = []
        for j in range(1, nd):
            px, py, pc = x ^ (j >> 2), y ^ ((j >> 1) & 1), c ^ (j & 1)
            cps.append(pltpu.make_async_remote_copy(
                src_ref=p_ref, dst_ref=o_ref.at[me], send_sem=ssem.at[j - 1], recv_sem=rsem.at[j - 1],
                device_id=(px, py, pc), device_id_type=MESH))
        for cp in cps:
            cp.start()
        for j in range(1, nd):
            peer = me ^ j
            got = o_ref.at[peer]
            pltpu.make_async_remote_copy(src_ref=got, dst_ref=got, send_sem=ssem.at[j - 1], recv_sem=rsem.at[j - 1],
                                         device_id=(x, y, c), device_id_type=MESH).wait_recv()
        for cp in cps:
            cp.wait_send()

    vm = pl.BlockSpec(memory_space=pltpu.VMEM)
    return _pcall(body, name=name, in_specs=[vm], out_specs=vm,
                  out_shape=jax.ShapeDtypeStruct((nd,) + pack.shape, pack.dtype),
                  scratch_shapes=[pltpu.SemaphoreType.DMA((nd - 1,)), pltpu.SemaphoreType.DMA((nd - 1,))])(pack)


def _row(v):
    return v.reshape(1, -1)


def _local_step(x, p, tgt, norm_g, conv_w, conv_b, branch_g, ple_norm_g, b_pg, final_g, w_in, w_out, w_pg, w_pe):
    saved = []
    xl = x
    for l in range(DEPTH):
        h, proj = _inproj(xl, _row(norm_g[l]), w_in, l, f"inproj_{l}")
        ya, tl, walked = _attn_fwd(proj, f"attn_fwd_{l}")
        y = _mix_fwd(proj, ya, conv_w[l], _row(conv_b[l]), _row(branch_g[l]), f"mix_fwd_{l}")
        x1, hn = _outproj(y, w_out, l, xl, _row(ple_norm_g[l]), f"outproj_{l}")
        x2, gate, e = _ple_fwd(hn, w_pg, _row(b_pg[l]), p, w_pe, l, x1, f"ple_fwd_{l}")
        saved.append((xl, h, proj, ya, tl, walked, y, x1, hn, gate, e))
        xl = x2

    sq, dx, d_final = _loss_head(xl, tgt, _row(final_g), "loss_head")

    g_in = g_out = g_pg = g_pe = None
    small = {k: [None] * DEPTH for k in ("norm_g", "conv_w", "conv_b", "branch_g", "ple_norm_g", "b_pg")}
    for l in reversed(range(DEPTH)):
        xl, h, proj, ya, tl, walked, y, x1, hn, gate, e = saved[l]
        du, de, dx1, dy, db_pg, d_ple = _ple_bwd(dx, gate, e, x1, w_pg, _row(ple_norm_g[l]), w_out, l, f"ple_bwd_{l}")
        g_pg = _mm_tn(hn, du, g_pg, l, f"grad_w_pg_{l}")
        g_pe = _mm_tn(p, de, g_pe, l, f"grad_w_pe_{l}", a_stacked=True)
        g_out = _mm_tn(y, dx1, g_out, l, f"grad_w_out_{l}")
        dpc, d_cw, d_cb, d_bg_c = _convmix_bwd(dy, proj, conv_w[l], _row(conv_b[l]), _row(branch_g[l]), f"convmix_bwd_{l}")
        dproj, d_bg_a = _attn_bwd(proj, dy, ya, tl, walked, _row(branch_g[l]), dpc, f"attn_bwd_{l}")
        g_in = _grad_w_in(h, dproj, g_in, l, f"grad_w_in_{l}")
        dx, d_norm = _inproj_bwd(dproj, w_in, l, xl, _row(norm_g[l]), dx1, f"inproj_bwd_{l}")
        small["norm_g"][l] = jnp.sum(d_norm, axis=0)
        small["conv_w"][l] = jnp.sum(d_cw, axis=1)
        small["conv_b"][l] = jnp.sum(d_cb, axis=0)
        small["branch_g"][l] = jnp.concatenate([jnp.sum(d_bg_c, axis=0), jnp.sum(d_bg_a, axis=0)])
        small["ple_norm_g"][l] = jnp.sum(d_ple, axis=0)
        small["b_pg"][l] = jnp.sum(db_pg, axis=0)
    small = {k: jnp.stack(v) for k, v in small.items()}
    small["final_g"] = jnp.sum(d_final, axis=0)
    return sq[0, 0], dx, (g_in, g_out, g_pg, g_pe), small


SMALL_ORDER = ("norm_g", "conv_w", "conv_b", "branch_g", "ple_norm_g", "b_pg", "final_g")


def _pack(parts, width):
    flat = jnp.concatenate([v.reshape(-1) for v in parts])
    rows = -(-flat.shape[0] // width)
    rows = -(-rows // 8) * 8
    return jnp.pad(flat, (0, rows * width - flat.shape[0])).reshape(rows, width)


def _unpack(packed, like):
    flat = packed.reshape(-1)
    out, off = [], 0
    for v in like:
        out.append(flat[off:off + v.size].reshape(v.shape))
        off += v.size
    return out


def kernel(x, p, norm_g, w_in, conv_w, conv_b, branch_g, w_out, ple_norm_g, w_pg, b_pg, w_pe, final_g, loss_target, m_norm_g, m_w_in, m_conv_w, m_conv_b, m_branch_g, m_w_out, m_ple_norm_g, m_w_pg, m_b_pg, m_w_pe, m_final_g, v_norm_g, v_w_in, v_conv_w, v_conv_b, v_branch_g, v_w_out, v_ple_norm_g, v_w_pg, v_b_pg, v_w_pe, v_final_g):
    ix, iy, ic = _place()
    chip = 2 * ix + iy
    d = x.shape[-1]

    big_w = (w_in, w_out, w_pg, w_pe)
    own = [_cast_into_full(w, chip, ax, f"cast_{i}") for i, (w, ax) in enumerate(zip(big_w, SHARD_AXES))]
    full_in, full_out, full_pg, full_pe = _gather_weights(own)
    cw_shard = conv_w.shape[-1]
    cw_slots = _exchange_small(_pack([conv_w], LANES), "exchange_conv_w")
    conv_full = jnp.concatenate([_unpack(cw_slots[2 * k], [conv_w])[0] for k in range(4)], axis=-1)

    sq, dx, big_g, small_g = _local_step(
        x[0], p[:, 0], loss_target[0], norm_g, conv_full, conv_b, branch_g, ple_norm_g, b_pg, final_g,
        full_in, full_out, full_pg, full_pe)

    from_sibling = _swap_layers(big_g)
    chip_sums = [_add_layer(g, o, ic, f"add_layer_{i}") for i, (g, o) in enumerate(zip(big_g, from_sibling))]
    partials = _scatter_shards([narrow for _, narrow in chip_sums])
    reduced = [_sum_shard(wide, pr, chip, ic, ax, f"sum_shard_{i}")
               for i, ((wide, _), pr, ax) in enumerate(zip(chip_sums, partials, SHARD_AXES))]
    g_big = _pair_layers(reduced)

    parts = [small_g[k] for k in SMALL_ORDER] + [sq.reshape(1)]
    slots = _exchange_small(_pack(parts, d), "exchange_small_grads")
    total = _unpack(_sum_slots(slots, "sum_small"), parts)
    g_small = dict(zip(SMALL_ORDER, total[:-1]))
    loss = 0.5 * total[-1][0] / d
    g_small["conv_w"] = lax.dynamic_slice_in_dim(g_small["conv_w"], chip * cw_shard, cw_shard, axis=2)

    grads = dict(g_small)
    grads.update(w_in=g_big[0], w_out=g_big[1], w_pg=g_big[2], w_pe=g_big[3])
    weights = dict(norm_g=norm_g, w_in=w_in, conv_w=conv_w, conv_b=conv_b, branch_g=branch_g, w_out=w_out,
                   ple_norm_g=ple_norm_g, w_pg=w_pg, b_pg=b_pg, w_pe=w_pe, final_g=final_g)
    ms = dict(norm_g=m_norm_g, w_in=m_w_in, conv_w=m_conv_w, conv_b=m_conv_b, branch_g=m_branch_g, w_out=m_w_out,
              ple_norm_g=m_ple_norm_g, w_pg=m_w_pg, b_pg=m_b_pg, w_pe=m_w_pe, final_g=m_final_g)
    vs = dict(norm_g=v_norm_g, w_in=v_w_in, conv_w=v_conv_w, conv_b=v_conv_b, branch_g=v_branch_g, w_out=v_w_out,
              ple_norm_g=v_ple_norm_g, w_pg=v_w_pg, b_pg=v_b_pg, w_pe=v_w_pe, final_g=v_final_g)
    names = ("norm_g", "w_in", "conv_w", "conv_b", "branch_g", "w_out", "ple_norm_g", "w_pg", "b_pg", "w_pe", "final_g")
    delta, new_m, new_v = {}, {}, {}
    for k in ("w_in", "w_out", "w_pg", "w_pe"):
        shp = weights[k].shape
        two = lambda a: a.reshape(-1, shp[-1])
        dl, mn, vn = _adamw(two(weights[k]), two(grads[k]), two(ms[k]), two(vs[k]), f"adamw_{k}")
        delta[k], new_m[k], new_v[k] = dl.reshape(shp), mn.reshape(shp), vn.reshape(shp)
        grads[k] = grads[k].reshape(shp)
    like = [weights[k] for k in SMALL_ORDER]
    packs = [_pack([src[k] for k in SMALL_ORDER], d) for src in (weights, grads, ms, vs)]
    outs = _adamw(*packs, "adamw_small")
    for res, o in zip((delta, new_m, new_v), outs):
        res.update(dict(zip(SMALL_ORDER, _unpack(o, like))))

    return (loss, dx[None], *[grads[k] for k in names], *[delta[k] for k in names],
            *[new_m[k] for k in names], *[new_v[k] for k in names])
```

```python
import math

import jax
import jax.numpy as jnp
from jax import lax
from jax.experimental import pallas as pl
from jax.experimental.pallas import tpu as pltpu

F32 = jnp.float32
BF16 = jnp.bfloat16
EPS = 1e-6
HEAD = 64
LANES = 128
ATT_TK = 256
ATT_TQ = 512
ATT_ROWS = 128
ALIVE_LOG = -105.0
DEPTH = 2
VMEM_LIMIT = 56 * 1024 * 1024
MESH = pl.DeviceIdType.MESH
ANY = pl.BlockSpec(memory_space=pl.ANY)

ADAM_LR = 0.001
ADAM_B1 = 0.9
ADAM_B2 = 0.999
ADAM_EPS = 1e-08
ADAM_WD = 0.01
ADAM_STEP = 10


def _pcall(body, **kw):
    return pl.pallas_call(body, **kw)


def _cp(n_axes):
    return pltpu.CompilerParams(dimension_semantics=("arbitrary",) * n_axes, vmem_limit_bytes=VMEM_LIMIT)


def _tile(n, pref):
    return pref if n % pref == 0 else n


def _split_dot(a, b, passes):
    out = None
    rem = a
    for _ in range(passes):
        hi = rem.astype(BF16)
        t = jnp.dot(hi, b, preferred_element_type=F32)
        out = t if out is None else out + t
        rem = rem - hi.astype(F32)
    return out


def _group_mat():
    r = lax.broadcasted_iota(jnp.int32, (LANES, LANES), 0) // HEAD
    c = lax.broadcasted_iota(jnp.int32, (LANES, LANES), 1) // HEAD
    return jnp.where(r == c, 1.0 / HEAD, 0.0).astype(BF16)


def _group_mean(v, gm):
    return _split_dot(v, gm, 3)


def _sigmoid(z):
    return 1.0 / (1.0 + jnp.exp(-z))


def _dot_nt(a, b):
    return lax.dot_general(a, b, (((1,), (1,)), ((), ())), preferred_element_type=F32)


def _dot_tn(a, b):
    return lax.dot_general(a, b, (((0,), (0,)), ((), ())), preferred_element_type=F32)


def _cast_into_full(w, chip, axis, name):
    _, r, c = w.shape
    tr = _tile(r, 256)
    nb = r // tr
    full = (DEPTH, 4 * r, c) if axis == 0 else (DEPTH, r, 4 * c)

    def body(k_ref, w_ref, o_ref):
        o_ref[...] = w_ref[...].astype(BF16)

    out_map = (lambda l, i, k: (l, k[0] * nb + i, 0)) if axis == 0 else (lambda l, i, k: (l, i, k[0]))
    grid_spec = pltpu.PrefetchScalarGridSpec(
        num_scalar_prefetch=1, grid=(DEPTH, nb),
        in_specs=[pl.BlockSpec((1, tr, c), lambda l, i, k: (l, i, 0))],
        out_specs=pl.BlockSpec((1, tr, c), out_map))
    return _pcall(body, name=name, grid_spec=grid_spec, out_shape=jax.ShapeDtypeStruct(full, BF16),
                  compiler_params=_cp(2))(chip.reshape(1).astype(jnp.int32), w)


def _rms_bwd_rows(dh, xv, g):
    r = lax.rsqrt(jnp.mean(xv * xv, axis=-1, keepdims=True) + EPS)
    xn = xv * r
    dxn = dh * g
    dx = r * (dxn - xn * jnp.mean(dxn * xn, axis=-1, keepdims=True))
    return dx, dh * xn


def _colsum8(v):
    tm, d = v.shape
    return jnp.sum(v.reshape(tm // 8, 8, d), axis=0)


def _inproj(x, g, w, layer, name):
    s, d = x.shape
    n = w.shape[2]
    sw = d // 2
    ns = n // sw
    tm = _tile(s, 512)

    def body(x_ref, g_ref, w_ref, h_ref, o_ref):
        xv = x_ref[...]
        r = lax.rsqrt(jnp.mean(xv * xv, axis=-1, keepdims=True) + EPS)
        h = (xv * r * g_ref[...]).astype(BF16)
        h_ref[...] = h
        for k in range(ns):
            o_ref[k] = jnp.dot(h, w_ref[0, :, k * sw:(k + 1) * sw], preferred_element_type=F32).astype(BF16)

    return _pcall(body, name=name, grid=(s // tm,),
                  in_specs=[pl.BlockSpec((tm, d), lambda m: (m, 0)), pl.BlockSpec((1, d), lambda m: (0, 0)),
                            pl.BlockSpec((1, d, n), lambda m: (layer, 0, 0))],
                  out_specs=[pl.BlockSpec((tm, d), lambda m: (m, 0)), pl.BlockSpec((ns, tm, sw), lambda m: (0, m, 0))],
                  out_shape=[jax.ShapeDtypeStruct((s, d), BF16), jax.ShapeDtypeStruct((ns, s, sw), BF16)],
                  compiler_params=_cp(1))(x, g, w)


def _softplus_parts(z):
    lm = jnp.minimum(-z, 0.0) - jnp.log(1.0 + jnp.exp(-jnp.abs(z)))
    return lm, lm + z


def _attn_tiles(s):
    tk = _tile(s, ATT_TK)
    tq = _tile(s, ATT_TQ)
    return tk, tq, tq // tk, min(ATT_ROWS, tq)


def _diag_work(chains, d, rows, tk):
    work = []
    for n, (_, r0) in enumerate(chains):
        if r0 + rows - 1 <= d * tk:
            continue
        kw = tk // 2 if (tk % 2 == 0 and r0 + rows <= d * tk + tk // 2) else tk
        if r0 >= d * tk + kw:
            mask = None
        else:
            row = lax.broadcasted_iota(jnp.int32, (rows, kw), 0)
            col = lax.broadcasted_iota(jnp.int32, (rows, kw), 1)
            mask = col + d * tk < row + r0
        work.append((n, kw, mask))
    return work


def _walk_work(chains, tk, below=None):
    return [(n, tk, None) for n, (_, r0) in enumerate(chains) if below is None or r0 < below]


def _any_alive(rsums):
    m = rsums[0]
    for r in rsums[1:]:
        m = jnp.maximum(m, r)
    return jnp.max((m > ALIVE_LOG).astype(jnp.int32))


def _attn_fwd(proj, name, weights=None, gather_layer=None):
    _, s, sw = proj.shape
    nhp = sw // LANES
    tk, tq, nd, rows = _attn_tiles(s)
    nq = s // tq
    scale = 1.0 / math.sqrt(HEAD)

    nw = 0 if weights is None else len(weights)

    def body(*refs):
        q_ref, k_ref, v_ref = refs[:3]
        o_ref, tl_ref, nw_ref = refs[3 + nw:6 + nw]
        acc_ref = refs[6 + 2 * nw]
        i = pl.program_id(1)
        if weights is not None:
            start, pass_on, finish = _gather_phases(refs[3:3 + nw], refs[6 + nw:6 + 2 * nw], refs[-2], refs[-1],
                                                    gather_layer)
            step = pl.program_id(0) * nq + i
            pl.when(step == 0)(start)
            pl.when(step == (3 * nhp * nq) // 4)(pass_on)
        tri = (lax.broadcasted_iota(jnp.int32, (tk, tk), 0) >
               lax.broadcasted_iota(jnp.int32, (tk, tk), 1)).astype(BF16)
        lane = lax.broadcasted_iota(jnp.int32, (tq, LANES), 1)
        q = q_ref[0] * jnp.asarray(scale, BF16)
        qms = [jnp.where((lane // HEAD) == h, q, jnp.zeros_like(q)) for h in range(2)]
        acc_ref[...] = jnp.zeros_like(acc_ref)
        chains = [(h, r0) for h in range(2) for r0 in range(0, tq, rows)]
        qparts = [qms[h][r0:r0 + rows] for h, r0 in chains]

        def tile(j, rsums, work):
            k0 = pl.multiple_of(j * tk, tk)
            kj = k_ref[0, pl.ds(k0, tk), :]
            vj = v_ref[0, pl.ds(k0, tk), :]
            zs = [_dot_nt(qparts[n], kj[:kw]) for n, kw, _ in work]
            lms, lss, css = [], [], []
            for z, (n, kw, mask) in zip(zs, work):
                lm, ls = _softplus_parts(z)
                if mask is not None:
                    lm = jnp.where(mask, lm, 0.0)
                lms.append(lm)
                lss.append(ls)
                css.append(_split_dot(lm, tri[:kw, :kw], 2))
            out = list(rsums)
            for lm, ls, cs, (n, kw, mask) in zip(lms, lss, css, work):
                h, r0 = chains[n]
                a = jnp.exp(ls + (rsums[n] + cs))
                if mask is not None:
                    a = jnp.where(mask, a, 0.0)
                acc_ref[h, r0:r0 + rows, :] += jnp.dot(a.astype(BF16), vj[:kw], preferred_element_type=F32)
                out[n] = rsums[n] + jnp.sum(lm, axis=1, keepdims=True)
            return tuple(out)

        rsums = (jnp.zeros((rows, 1), F32),) * len(chains)
        for d in reversed(range(nd)):
            rsums = tile(i * nd + d, rsums, _diag_work(chains, d, rows, tk))

        upper = [rs for rs, (_, r0) in zip(rsums, chains) if r0 >= tk]
        lower = [rs for rs, (_, r0) in zip(rsums, chains) if r0 < tk]
        if upper:
            alone = (i > 0) & (_any_alive(upper) == 0) & (_any_alive(lower) > 0)
            rsums = lax.cond(alone, lambda rs: tile(i * nd - 1, rs, _walk_work(chains, tk, below=tk)),
                             lambda rs: rs, rsums)
            alone = alone.astype(jnp.int32)
        else:
            alone = jnp.int32(0)

        def walk(c):
            jj, rs, _ = c
            rs = tile(i * nd - 1 - jj, rs, _walk_work(chains, tk))
            return jj + 1, rs, _any_alive(rs)

        last, rsums, _ = lax.while_loop(lambda c: (c[0] < i * nd) & (c[2] > 0), walk,
                                        (alone, rsums, _any_alive(rsums)))
        for n, (h, r0) in enumerate(chains):
            tl_ref[h, r0:r0 + rows, :] = rsums[n]
        nw_ref[0] = (jnp.zeros((8, LANES), jnp.int32) + (2 * (last - alone) + alone)).astype(F32)
        o_ref[...] = jnp.where(lane < HEAD, acc_ref[0], acc_ref[1]).astype(BF16)
        if weights is not None:
            pl.when(step == nhp * nq - 1)(finish)

    hosted = [] if weights is None else list(weights)
    sems = [] if weights is None else [pltpu.SemaphoreType.DMA((nw, 6)), pltpu.SemaphoreType.DMA((nw, 6))]
    outs = _pcall(
        body, name=name, grid=(nhp, nq),
        in_specs=[pl.BlockSpec((1, tq, LANES), lambda hp, i: (4, i, hp)),
                  pl.BlockSpec((1, s, LANES), lambda hp, i: (5, 0, hp)),
                  pl.BlockSpec((1, s, LANES), lambda hp, i: (6, 0, hp))] + [ANY] * nw,
        out_specs=[pl.BlockSpec((tq, LANES), lambda hp, i: (i, hp)),
                   pl.BlockSpec((2, tq, 1), lambda hp, i: (hp, i, 0)),
                   pl.BlockSpec((1, 8, LANES), lambda hp, i: (hp * nq + i, 0, 0))] + [ANY] * nw,
        out_shape=[jax.ShapeDtypeStruct((s, sw), BF16), jax.ShapeDtypeStruct((2 * nhp, s, 1), F32),
                   jax.ShapeDtypeStruct((nhp * nq, 8, LANES), F32)]
        + [jax.ShapeDtypeStruct(w.shape, w.dtype) for w in hosted],
        input_output_aliases={3 + a: 3 + a for a in range(nw)},
        scratch_shapes=[pltpu.VMEM((2, tq, LANES), F32)] + sems,
        compiler_params=_cp(2))(proj, proj, proj, *hosted)
    return outs[0], outs[1], outs[2], tuple(outs[3:])


def _conv_rows(cc_ref, ch_ref, w_ref, b_ref, r, tc):
    r0 = pl.multiple_of(r * tc, tc)
    u = cc_ref[0, pl.ds(r0, tc), :].astype(F32) * ch_ref[0, pl.ds(r0, tc), :].astype(F32)
    p0 = pl.multiple_of(jnp.maximum(r0 - 16, 0), 16)
    up = cc_ref[0, pl.ds(p0, 16), :].astype(F32) * ch_ref[0, pl.ds(p0, 16), :].astype(F32)
    up = up * (r > 0).astype(F32)
    prev1 = up[15:16, :]
    prev2 = up[14:15, :]
    rid = lax.broadcasted_iota(jnp.int32, u.shape, 0)
    s1 = jnp.where(rid == 0, prev1, pltpu.roll(u, 1, axis=0))
    s2 = jnp.where(rid == 0, prev2, jnp.where(rid == 1, prev1, pltpu.roll(u, 2, axis=0)))
    cv = b_ref[...] + s2 * w_ref[0:1, :] + s1 * w_ref[1:2, :] + u * w_ref[2:3, :]
    return r0, u, s1, s2, cv


def _mix_fwd(proj, ya, conv_w, conv_b, bg, name):
    _, s, sw = proj.shape
    nh = sw // LANES
    tc = _tile(s, 256)

    def body(cb_ref, cc_ref, ch_ref, cz_ref, ya_ref, az_ref, w_ref, b_ref, g_ref, y_ref):
        c = pl.program_id(0)
        gm = _group_mat()

        def finish(r0, yv, zg):
            n = yv * lax.rsqrt(_group_mean(yv * yv, gm) + EPS)
            y_ref[pl.ds(r0, tc), :] = (n * g_ref[...] * (zg * _sigmoid(zg))).astype(BF16)

        @pl.when(c < nh)
        def _():
            def step(r, carry):
                r0, _, _, _, cv = _conv_rows(cc_ref, ch_ref, w_ref, b_ref, r, tc)
                yc = cb_ref[0, pl.ds(r0, tc), :].astype(F32) * cv
                finish(r0, yc, cz_ref[0, pl.ds(r0, tc), :].astype(F32))
                return carry
            lax.fori_loop(0, s // tc, step, 0)

        @pl.when(c >= nh)
        def _():
            def step(r, carry):
                r0 = pl.multiple_of(r * tc, tc)
                finish(r0, ya_ref[pl.ds(r0, tc), :].astype(F32), az_ref[0, pl.ds(r0, tc), :].astype(F32))
                return carry
            lax.fori_loop(0, s // tc, step, 0)

    def sec(k):
        return pl.BlockSpec((1, s, LANES), lambda c: (k, 0, jnp.minimum(c, nh - 1)))

    return _pcall(
        body, name=name, grid=(2 * nh,),
        in_specs=[sec(0), sec(1), sec(2), sec(3),
                  pl.BlockSpec((s, LANES), lambda c: (0, jnp.maximum(c - nh, 0))),
                  pl.BlockSpec((1, s, LANES), lambda c: (7, 0, jnp.maximum(c - nh, 0))),
                  pl.BlockSpec((3, LANES), lambda c: (0, jnp.minimum(c, nh - 1))),
                  pl.BlockSpec((1, LANES), lambda c: (0, jnp.minimum(c, nh - 1))),
                  pl.BlockSpec((1, LANES), lambda c: (0, c))],
        out_specs=pl.BlockSpec((s, LANES), lambda c: (0, c)),
        out_shape=jax.ShapeDtypeStruct((s, 2 * sw), BF16), compiler_params=_cp(1),
    )(proj, proj, proj, proj, ya, proj, conv_w, conv_b, bg)


def _outproj(y, w, layer, x, g, name):
    s, d = x.shape
    tm = _tile(s, 256)

    def body(y_ref, w_ref, x_ref, g_ref, x1_ref, hn_ref):
        x1 = x_ref[...] + jnp.dot(y_ref[...], w_ref[0], preferred_element_type=F32)
        x1_ref[...] = x1
        r = lax.rsqrt(jnp.mean(x1 * x1, axis=-1, keepdims=True) + EPS)
        hn_ref[...] = (x1 * r * g_ref[...]).astype(BF16)

    row = lambda m: (m, 0)
    fix = lambda m: (0, 0)
    return _pcall(body, name=name, grid=(s // tm,),
                  in_specs=[pl.BlockSpec((tm, d), row), pl.BlockSpec((1, d, d), lambda m: (layer, 0, 0)),
                            pl.BlockSpec((tm, d), row), pl.BlockSpec((1, d), fix)],
                  out_specs=[pl.BlockSpec((tm, d), row), pl.BlockSpec((tm, d), row)],
                  out_shape=[jax.ShapeDtypeStruct((s, d), F32), jax.ShapeDtypeStruct((s, d), BF16)],
                  compiler_params=_cp(1))(y, w, x, g)


def _ple_fwd(hn, w_pg, b_pg, p, w_pe, layer, x1, name):
    s, d = x1.shape
    pd = p.shape[2]
    tm = _tile(s, 256)

    def body(hn_ref, wg_ref, b_ref, p_ref, we_ref, x1_ref, x2_ref, gate_ref, e_ref):
        gate = _sigmoid(jnp.dot(hn_ref[...], wg_ref[0], preferred_element_type=F32) + b_ref[...])
        e = jnp.dot(p_ref[0].astype(BF16), we_ref[0], preferred_element_type=F32)
        x2_ref[...] = x1_ref[...] + gate * e
        gate_ref[...] = gate.astype(BF16)
        e_ref[...] = e.astype(BF16)

    row = lambda m: (m, 0)
    fix = lambda m: (0, 0)
    return _pcall(body, name=name, grid=(s // tm,),
                  in_specs=[pl.BlockSpec((tm, d), row), pl.BlockSpec((1, d, d), lambda m: (layer, 0, 0)),
                            pl.BlockSpec((1, d), fix), pl.BlockSpec((1, tm, pd), lambda m: (layer, m, 0)),
                            pl.BlockSpec((1, pd, d), lambda m: (layer, 0, 0)), pl.BlockSpec((tm, d), row)],
                  out_specs=[pl.BlockSpec((tm, d), row)] * 3,
                  out_shape=[jax.ShapeDtypeStruct((s, d), F32), jax.ShapeDtypeStruct((s, d), BF16),
                             jax.ShapeDtypeStruct((s, d), BF16)],
                  compiler_params=_cp(1))(hn, w_pg, b_pg, p, w_pe, x1)


def _loss_head(x, tgt, g, name):
    s, d = x.shape
    tm = _tile(s, 256)

    def body(x_ref, t_ref, g_ref, l_ref, dx_ref, dg_ref):
        m = pl.program_id(0)

        @pl.when(m == 0)
        def _():
            l_ref[...] = jnp.zeros_like(l_ref)
            dg_ref[...] = jnp.zeros_like(dg_ref)

        xv = x_ref[...]
        gv = g_ref[...]
        r = lax.rsqrt(jnp.mean(xv * xv, axis=-1, keepdims=True) + EPS)
        xn = xv * r
        err = xn * gv - t_ref[...]
        l_ref[...] += jnp.sum(err * err)
        dy = err * (1.0 / d)
        dxn = dy * gv
        dx_ref[...] = r * (dxn - xn * jnp.mean(dxn * xn, axis=-1, keepdims=True))
        dg_ref[...] += _colsum8(dy * xn)

    row = lambda m: (m, 0)
    fix = lambda m: (0, 0)
    return _pcall(body, name=name, grid=(s // tm,),
                  in_specs=[pl.BlockSpec((tm, d), row), pl.BlockSpec((tm, d), row), pl.BlockSpec((1, d), fix)],
                  out_specs=[pl.BlockSpec((8, LANES), fix), pl.BlockSpec((tm, d), row), pl.BlockSpec((8, d), fix)],
                  out_shape=[jax.ShapeDtypeStruct((8, LANES), F32), jax.ShapeDtypeStruct((s, d), F32),
                             jax.ShapeDtypeStruct((8, d), F32)],
                  compiler_params=_cp(1))(x, tgt, g)


def _ple_bwd(dx2, gate, e, x1, w_pg, g_ple, w_out, layer, name):
    s, d = dx2.shape
    tm = _tile(s, 256)

    def body(dx2_ref, gate_ref, e_ref, x1_ref, wg_ref, g_ref, wo_ref,
             du_ref, de_ref, dx1_ref, dy_ref, db_ref, dg_ref):
        m = pl.program_id(0)

        @pl.when(m == 0)
        def _():
            db_ref[...] = jnp.zeros_like(db_ref)
            dg_ref[...] = jnp.zeros_like(dg_ref)

        dx2v = dx2_ref[...]
        gate = gate_ref[...].astype(F32)
        du = dx2v * e_ref[...].astype(F32) * gate * (1.0 - gate)
        de_ref[...] = (dx2v * gate).astype(BF16)
        dub = du.astype(BF16)
        du_ref[...] = dub
        db_ref[...] += _colsum8(du)
        dhn = _dot_nt(dub, wg_ref[0])
        dxr, dgr = _rms_bwd_rows(dhn, x1_ref[...], g_ref[...])
        dx1 = dx2v + dxr
        dx1_ref[...] = dx1
        dg_ref[...] += _colsum8(dgr)
        dy_ref[...] = _dot_nt(dx1.astype(BF16), wo_ref[0]).astype(BF16)

    row = lambda m: (m, 0)
    fix = lambda m: (0, 0)
    t = pl.BlockSpec((tm, d), row)
    return _pcall(body, name=name, grid=(s // tm,),
                  in_specs=[t, t, t, t, pl.BlockSpec((1, d, d), lambda m: (layer, 0, 0)), pl.BlockSpec((1, d), fix),
                            pl.BlockSpec((1, d, d), lambda m: (layer, 0, 0))],
                  out_specs=[t, t, t, t, pl.BlockSpec((8, d), fix), pl.BlockSpec((8, d), fix)],
                  out_shape=[jax.ShapeDtypeStruct((s, d), BF16), jax.ShapeDtypeStruct((s, d), BF16),
                             jax.ShapeDtypeStruct((s, d), F32), jax.ShapeDtypeStruct((s, d), BF16),
                             jax.ShapeDtypeStruct((8, d), F32), jax.ShapeDtypeStruct((8, d), F32)],
                  compiler_params=_cp(1))(dx2, gate, e, x1, w_pg, g_ple, w_out)


def _mm_tn(a, b, stack, layer, name, a_stacked=False):
    s, ka = a.shape[-2:]
    n = b.shape[1]
    tn = _tile(n, 1024)
    ns = n // tn
    tk = _tile(s, 512)
    nk = s // tk

    def body(*refs):
        a_ref, b_ref = refs[0], refs[1]
        o_ref, acc_ref = refs[-2], refs[-1]
        k = pl.program_id(1)

        @pl.when(k == 0)
        def _():
            acc_ref[...] = jnp.zeros_like(acc_ref)

        av = a_ref[0] if a_stacked else a_ref[...]
        acc_ref[...] += _dot_tn(av.astype(BF16), b_ref[...].astype(BF16))

        @pl.when(k == nk - 1)
        def _():
            o_ref[0] = acc_ref[...]

    a_spec = (pl.BlockSpec((1, tk, ka), lambda j, k: (layer, k, 0)) if a_stacked
              else pl.BlockSpec((tk, ka), lambda j, k: (k, 0)))
    in_specs = [a_spec, pl.BlockSpec((tk, tn), lambda j, k: (k, j))]
    args = [a, b]
    aliases = {}
    if stack is not None:
        in_specs.append(ANY)
        args.append(stack)
        aliases = {2: 0}
    return _pcall(body, name=name, grid=(ns, nk), in_specs=in_specs,
                  out_specs=pl.BlockSpec((1, ka, tn), lambda j, k: (layer, 0, j)),
                  out_shape=jax.ShapeDtypeStruct((DEPTH, ka, n), F32),
                  scratch_shapes=[pltpu.VMEM((ka, tn), F32)], input_output_aliases=aliases,
                  compiler_params=_cp(2))(*args)


def _norm_gate_bwd(dy, yv, zg, g, gm):
    r = lax.rsqrt(_group_mean(yv * yv, gm) + EPS)
    n = yv * r
    sg = _sigmoid(zg)
    sil = zg * sg
    dzg = dy * n * g * (sg * (1.0 + zg * (1.0 - sg)))
    dn = dy * g * sil
    dyv = r * (dn - n * _group_mean(dn * n, gm))
    return dyv, dzg, dy * n * sil


def _convmix_bwd(dy, proj, conv_w, conv_b, bg, name):
    _, s, sw = proj.shape
    nh = sw // LANES
    tc = _tile(s, 256)
    nr = s // tc

    def body(dy_ref, cb_ref, cc_ref, ch_ref, cz_ref, w_ref, b_ref, g_ref,
             dp_ref, dw_ref, db_ref, dg_ref, dcv_ref):
        gm = _group_mat()
        dcv_ref[pl.ds(s, 8), :] = jnp.zeros((8, LANES), F32)

        def pass1(r, carry):
            dw0, dw1, dw2, db, dg = carry
            r0, u, s1, s2, cv = _conv_rows(cc_ref, ch_ref, w_ref, b_ref, r, tc)
            cb = cb_ref[0, pl.ds(r0, tc), :].astype(F32)
            dyc, dcz, dgr = _norm_gate_bwd(dy_ref[pl.ds(r0, tc), :].astype(F32), cb * cv,
                                           cz_ref[0, pl.ds(r0, tc), :].astype(F32), g_ref[...], gm)
            dp_ref[0, pl.ds(r0, tc), :] = (dyc * cv).astype(BF16)
            dp_ref[3, pl.ds(r0, tc), :] = dcz.astype(BF16)
            dcv = dyc * cb
            dcv_ref[pl.ds(r0, tc), :] = dcv
            return (dw0 + _colsum8(dcv * s2), dw1 + _colsum8(dcv * s1), dw2 + _colsum8(dcv * u),
                    db + _colsum8(dcv), dg + _colsum8(dgr))

        z8 = jnp.zeros((8, LANES), F32)
        dw0, dw1, dw2, db, dg = lax.fori_loop(0, nr, pass1, (z8, z8, z8, z8, z8))
        dw_ref[0] = dw0
        dw_ref[1] = dw1
        dw_ref[2] = dw2
        db_ref[...] = db
        dg_ref[...] = dg

        def pass2(r, carry):
            r0 = pl.multiple_of(r * tc, tc)
            dcv = dcv_ref[pl.ds(r0, tc), :]
            nxt = dcv_ref[pl.ds(pl.multiple_of(r0 + tc, 8), 8), :]
            rid = lax.broadcasted_iota(jnp.int32, dcv.shape, 0)
            n1 = jnp.where(rid == tc - 1, nxt[0:1, :], pltpu.roll(dcv, tc - 1, axis=0))
            n2 = jnp.where(rid == tc - 1, nxt[1:2, :],
                           jnp.where(rid == tc - 2, nxt[0:1, :], pltpu.roll(dcv, tc - 2, axis=0)))
            du = dcv * w_ref[2:3, :] + n1 * w_ref[1:2, :] + n2 * w_ref[0:1, :]
            dp_ref[1, pl.ds(r0, tc), :] = (du * ch_ref[0, pl.ds(r0, tc), :].astype(F32)).astype(BF16)
            dp_ref[2, pl.ds(r0, tc), :] = (du * cc_ref[0, pl.ds(r0, tc), :].astype(F32)).astype(BF16)
            return carry

        lax.fori_loop(0, nr, pass2, 0)

    def sec(k):
        return pl.BlockSpec((1, s, LANES), lambda c: (k, 0, c))

    col = lambda c: (0, c)
    return _pcall(
        body, name=name, grid=(nh,),
        in_specs=[pl.BlockSpec((s, LANES), col), sec(0), sec(1), sec(2), sec(3),
                  pl.BlockSpec((3, LANES), col), pl.BlockSpec((1, LANES), col), pl.BlockSpec((1, LANES), col)],
        out_specs=[pl.BlockSpec((4, s, LANES), lambda c: (0, 0, c)), pl.BlockSpec((3, 8, LANES), lambda c: (0, 0, c)),
                   pl.BlockSpec((8, LANES), col), pl.BlockSpec((8, LANES), col)],
        out_shape=[jax.ShapeDtypeStruct((8, s, sw), BF16), jax.ShapeDtypeStruct((3, 8, sw), F32),
                   jax.ShapeDtypeStruct((8, sw), F32), jax.ShapeDtypeStruct((8, sw), F32)],
        scratch_shapes=[pltpu.VMEM((s + 8, LANES), F32)], compiler_params=_cp(1),
    )(dy, proj, proj, proj, proj, conv_w, conv_b, bg)


def _attn_bwd(proj, dy, ya, tl, walked, bg, buf, name):
    _, s, sw = proj.shape
    nhp = sw // LANES
    tk, t, nd, rows_c = _attn_tiles(s)
    nq = s // t
    scale = 1.0 / math.sqrt(HEAD)

    def body(q_ref, k_ref, v_ref, az_ref, dy_ref, ya_ref, tl_ref, nw_ref, g_ref, buf_ref, out_ref, dg_ref,
             dka_ref, dva_ref, dqa_ref):
        step = pl.program_id(1)
        i = nq - 1 - step

        @pl.when(step == 0)
        def _():
            dka_ref[...] = jnp.zeros_like(dka_ref)
            dva_ref[...] = jnp.zeros_like(dva_ref)
            dg_ref[...] = jnp.zeros_like(dg_ref)

        dyv, dzg, dgr = _norm_gate_bwd(dy_ref[...].astype(F32), ya_ref[...].astype(F32), az_ref[0].astype(F32),
                                       g_ref[...], _group_mat())
        out_ref[3] = dzg.astype(BF16)
        dg_ref[...] += _colsum8(dgr)

        tri = (lax.broadcasted_iota(jnp.int32, (tk, tk), 0) <=
               lax.broadcasted_iota(jnp.int32, (tk, tk), 1)).astype(BF16)
        lane = lax.broadcasted_iota(jnp.int32, (t, LANES), 1)
        q = q_ref[0] * jnp.asarray(scale, BF16)
        do = dyv.astype(BF16)
        qms = [jnp.where((lane // HEAD) == h, q, jnp.zeros_like(q)) for h in range(2)]
        doms = [jnp.where((lane // HEAD) == h, do, jnp.zeros_like(do)) for h in range(2)]
        dqa_ref[...] = jnp.zeros_like(dqa_ref)
        chains = [(h, r0) for h in range(2) for r0 in range(0, t, rows_c)]
        qparts = [qms[h][r0:r0 + rows_c] for h, r0 in chains]
        doparts = [doms[h][r0:r0 + rows_c] for h, r0 in chains]
        tots = [tl_ref[h, r0:r0 + rows_c, :] for h, r0 in chains]

        def tile(j, carry, work):
            k0 = pl.multiple_of(j * tk, tk)
            kj = k_ref[0, pl.ds(k0, tk), :]
            vj = v_ref[0, pl.ds(k0, tk), :]
            zs = [_dot_nt(qparts[n], kj[:kw]) for n, kw, _ in work]
            das = [_dot_nt(doparts[n], vj[:kw]) for n, kw, _ in work]
            lms, lss, cls = [], [], []
            for z, (n, kw, mask) in zip(zs, work):
                lm, ls = _softplus_parts(z)
                if mask is not None:
                    lm = jnp.where(mask, lm, 0.0)
                lms.append(lm)
                lss.append(ls)
                cls.append(_split_dot(lm, tri[:kw, :kw], 2))
            abs_, gs, cgs = [], [], []
            for ls, cl, da, (n, kw, mask) in zip(lss, cls, das, work):
                a = jnp.exp(ls + (tots[n] - carry[n][0] - cl))
                if mask is not None:
                    a = jnp.where(mask, a, 0.0)
                g = a * da
                gs.append(g)
                abs_.append(a.astype(BF16))
                cgs.append(_split_dot(g, tri[:kw, :kw], 1))
            out = list(carry)
            dks, dvs = {}, {}
            for lm, ls, a, g, cg, (n, kw, mask) in zip(lms, lss, abs_, gs, cgs, work):
                h, r0 = chains[n]
                psum, gsum = carry[n]
                dz = g - jnp.exp(ls) * (gsum + cg)
                if mask is not None:
                    dz = jnp.where(mask, dz, 0.0)
                dz = dz.astype(BF16)
                dqa_ref[h, r0:r0 + rows_c, :] += jnp.dot(dz, kj[:kw], preferred_element_type=F32)
                dkh = _dot_tn(dz, qparts[n])
                dvh = _dot_tn(a, doparts[n])
                dks[kw] = dkh if kw not in dks else dks[kw] + dkh
                dvs[kw] = dvh if kw not in dvs else dvs[kw] + dvh
                out[n] = (psum + jnp.sum(lm, axis=1, keepdims=True), gsum + jnp.sum(g, axis=1, keepdims=True))
            for kw in dks:
                dka_ref[pl.ds(k0, kw), :] += dks[kw]
                dva_ref[pl.ds(k0, kw), :] += dvs[kw]
            return tuple(out)

        z1 = jnp.zeros((rows_c, 1), F32)
        code = jnp.clip(jnp.max(nw_ref[0].astype(jnp.int32)), 0, 2 * i * nd + 1)
        alone = jnp.minimum(code % 2, i * nd)
        whole = jnp.minimum(code // 2, i * nd - alone)
        carry = lax.fori_loop(i * nd - alone - whole, i * nd - alone,
                              lambda j, c: tile(j, c, _walk_work(chains, tk)), ((z1, z1),) * len(chains))
        if any(r0 >= tk for _, r0 in chains):
            carry = lax.cond(alone > 0, lambda c: tile(i * nd - 1, c, _walk_work(chains, tk, below=tk)),
                             lambda c: c, carry)
        for d in range(nd):
            carry = tile(i * nd + d, carry, _diag_work(chains, d, rows_c, tk))
        out_ref[0] = (jnp.where(lane < HEAD, dqa_ref[0], dqa_ref[1]) * scale).astype(BF16)
        own = pl.multiple_of(i * t, t)
        out_ref[1] = dka_ref[pl.ds(own, t), :].astype(BF16)
        out_ref[2] = dva_ref[pl.ds(own, t), :].astype(BF16)

    def rows(sec):
        return pl.BlockSpec((1, t, LANES), lambda hp, st: (sec, nq - 1 - st, hp))

    def whole(sec):
        return pl.BlockSpec((1, s, LANES), lambda hp, st: (sec, 0, hp))

    return _pcall(
        body, name=name, grid=(nhp, nq),
        in_specs=[rows(4), whole(5), whole(6), rows(7),
                  pl.BlockSpec((t, LANES), lambda hp, st: (nq - 1 - st, hp + nhp)),
                  pl.BlockSpec((t, LANES), lambda hp, st: (nq - 1 - st, hp)),
                  pl.BlockSpec((2, t, 1), lambda hp, st: (hp, nq - 1 - st, 0)),
                  pl.BlockSpec((1, 8, LANES), lambda hp, st: (hp * nq + nq - 1 - st, 0, 0)),
                  pl.BlockSpec((1, LANES), lambda hp, st: (0, hp + nhp)), ANY],
        out_specs=[pl.BlockSpec((4, t, LANES), lambda hp, st: (1, nq - 1 - st, hp)),
                   pl.BlockSpec((8, LANES), lambda hp, st: (0, hp))],
        out_shape=[jax.ShapeDtypeStruct(buf.shape, buf.dtype), jax.ShapeDtypeStruct((8, sw), F32)],
        input_output_aliases={9: 0},
        scratch_shapes=[pltpu.VMEM((s, LANES), F32), pltpu.VMEM((s, LANES), F32), pltpu.VMEM((2, t, LANES), F32)],
        compiler_params=_cp(2))(proj, proj, proj, proj, dy, ya, tl, walked, bg, buf)


def _grad_w_in(h, dproj, stack, layer, name):
    s, d = h.shape
    ns, _, sw = dproj.shape

    def body(*refs):
        h_ref, b_ref = refs[0], refs[1]
        o_ref, ht_ref = refs[-2], refs[-1]

        @pl.when(pl.program_id(0) == 0)
        def _():
            ht_ref[...] = h_ref[...].T

        o_ref[0] = jnp.dot(ht_ref[...], b_ref[0], preferred_element_type=F32)

    in_specs = [pl.BlockSpec((s, d), lambda j: (0, 0)), pl.BlockSpec((1, s, sw), lambda j: (j, 0, 0))]
    args = [h, dproj]
    aliases = {}
    if stack is not None:
        in_specs.append(ANY)
        args.append(stack)
        aliases = {2: 0}
    return _pcall(body, name=name, grid=(ns,), in_specs=in_specs,
                  out_specs=pl.BlockSpec((1, d, sw), lambda j: (layer, 0, j)),
                  out_shape=jax.ShapeDtypeStruct((DEPTH, d, ns * sw), F32),
                  scratch_shapes=[pltpu.VMEM((d, s), BF16)], input_output_aliases=aliases,
                  compiler_params=_cp(1))(*args)


def _inproj_bwd(dproj, w, layer, x, g, dx1, name):
    ns, s, sw = dproj.shape
    d = x.shape[1]
    tm = _tile(s, 256)

    def body(dp_ref, w_ref, x_ref, g_ref, dx1_ref, dx_ref, dg_ref):
        @pl.when(pl.program_id(0) == 0)
        def _():
            dg_ref[...] = jnp.zeros_like(dg_ref)

        dh = _dot_nt(dp_ref[0], w_ref[0, :, 0:sw])
        for k in range(1, ns):
            dh = dh + _dot_nt(dp_ref[k], w_ref[0, :, k * sw:(k + 1) * sw])
        dxr, dgr = _rms_bwd_rows(dh, x_ref[...], g_ref[...])
        dx_ref[...] = dx1_ref[...] + dxr
        dg_ref[...] += _colsum8(dgr)

    row = lambda m: (m, 0)
    fix = lambda m: (0, 0)
    return _pcall(body, name=name, grid=(s // tm,),
                  in_specs=[pl.BlockSpec((ns, tm, sw), lambda m: (0, m, 0)),
                            pl.BlockSpec((1, d, ns * sw), lambda m: (layer, 0, 0)),
                            pl.BlockSpec((tm, d), row), pl.BlockSpec((1, d), fix), pl.BlockSpec((tm, d), row)],
                  out_specs=[pl.BlockSpec((tm, d), row), pl.BlockSpec((8, d), fix)],
                  out_shape=[jax.ShapeDtypeStruct((s, d), F32), jax.ShapeDtypeStruct((8, d), F32)],
                  compiler_params=_cp(1))(dproj, w, x, g, dx1)


def _adamw(w, g, m, v, name):
    r, c = w.shape
    tr = _tile(r, 256)
    c1 = 1.0 - ADAM_B1 ** ADAM_STEP
    c2 = 1.0 - ADAM_B2 ** ADAM_STEP

    def body(w_ref, g_ref, m_ref, v_ref, d_ref, mo_ref, vo_ref):
        gv = g_ref[...]
        mn = ADAM_B1 * m_ref[...] + (1.0 - ADAM_B1) * gv
        vn = ADAM_B2 * v_ref[...] + (1.0 - ADAM_B2) * (gv * gv)
        d_ref[...] = -ADAM_LR * ((mn / c1) / (jnp.sqrt(vn / c2) + ADAM_EPS) + ADAM_WD * w_ref[...])
        mo_ref[...] = mn
        vo_ref[...] = vn

    t = pl.BlockSpec((tr, c), lambda i: (i, 0))
    return _pcall(body, name=name, grid=(r // tr,), in_specs=[t] * 4, out_specs=[t] * 3,
                  out_shape=[jax.ShapeDtypeStruct((r, c), F32)] * 3, compiler_params=_cp(1))(w, g, m, v)


def _add_layer(stack, other, layer, name):
    _, r, c = stack.shape
    tr = _tile(r, 256)

    def body(l_ref, s_ref, o_ref, out_ref, outb_ref):
        v = s_ref[0] + o_ref[...]
        out_ref[...] = v
        outb_ref[...] = v.astype(BF16)

    t = pl.BlockSpec((tr, c), lambda i, l: (i, 0))
    grid_spec = pltpu.PrefetchScalarGridSpec(
        num_scalar_prefetch=1, grid=(r // tr,),
        in_specs=[pl.BlockSpec((1, tr, c), lambda i, l: (l[0], i, 0)), t], out_specs=[t, t])
    return _pcall(body, name=name, grid_spec=grid_spec,
                  out_shape=[jax.ShapeDtypeStruct((r, c), F32), jax.ShapeDtypeStruct((r, c), BF16)],
                  compiler_params=_cp(1))(layer.reshape(1).astype(jnp.int32), stack, other)


def _sum_shard(full, parts, chip, layer, axis, name):
    _, r, c = parts.shape
    tr = _tile(r, 256)

    def body(k_ref, f_ref, p_ref, out_ref):
        out_ref[0] = ((f_ref[...] + p_ref[0].astype(F32)) + p_ref[1].astype(F32)) + p_ref[2].astype(F32)

    if axis == 1:
        f_spec = pl.BlockSpec((tr, c), lambda i, k: (i, k[0]))
    else:
        nb = r // tr
        f_spec = pl.BlockSpec((tr, c), lambda i, k: (k[0] * nb + i, 0))
    grid_spec = pltpu.PrefetchScalarGridSpec(
        num_scalar_prefetch=1, grid=(r // tr,),
        in_specs=[f_spec, pl.BlockSpec((3, tr, c), lambda i, k: (0, i, 0))],
        out_specs=pl.BlockSpec((1, tr, c), lambda i, k: (k[1], i, 0)))
    return _pcall(body, name=name, grid_spec=grid_spec, out_shape=jax.ShapeDtypeStruct((DEPTH, r, c), F32),
                  compiler_params=_cp(1))(jnp.stack([chip, layer]).astype(jnp.int32), full, parts)


def _sum_slots(slots, name):
    n = slots.shape[0]

    def body(s_ref, o_ref):
        acc = s_ref[0]
        for i in range(1, n):
            acc = acc + s_ref[i]
        o_ref[...] = acc

    return _pcall(body, name=name, out_shape=jax.ShapeDtypeStruct(slots.shape[1:], F32))(slots)


def _place():
    return lax.axis_index("x"), lax.axis_index("y"), lax.axis_index("c")


def _shard_view(ref, axis, chip, size):
    if axis == 0:
        return ref.at[pl.ds(chip * size, size), :]
    return ref.at[:, pl.ds(chip * size, size)]


SHARD_AXES = (1, 0, 0, 1)


def _gather_phases(ins, outs, ssem, rsem, layer):
    n = len(ins)
    x, y, c = _place()
    me = 2 * x + y
    chips = [(1 - x, y), (x, 1 - y), (1 - x, 1 - y)]

    def piece(a, chip, half, of):
        block = _shard_view(of[a].at[layer], SHARD_AXES[a], chip, of[a].shape[1 + SHARD_AXES[a]] // 4)
        r = block.shape[0] // 2
        return block.at[pl.ds(half * r, r), :]

    def over_ici(a, j):
        cx, cy = chips[j]
        return pltpu.make_async_remote_copy(
            src_ref=piece(a, me, c, ins), dst_ref=piece(a, me, c, outs), send_sem=ssem.at[a, j],
            recv_sem=rsem.at[a, j], device_id=(cx, cy, c), device_id_type=MESH)

    def landed(a, j, half):
        cx, cy = chips[j]
        return piece(a, 2 * cx + cy, half, outs)

    def to_sibling(a, j):
        got = landed(a, j, c)
        return pltpu.make_async_remote_copy(
            src_ref=got, dst_ref=got, send_sem=ssem.at[a, 3 + j], recv_sem=rsem.at[a, 3 + j],
            device_id=(x, y, 1 - c), device_id_type=MESH)

    def wait_arrival(a, k, place):
        pltpu.make_async_remote_copy(src_ref=place, dst_ref=place, send_sem=ssem.at[a, k], recv_sem=rsem.at[a, k],
                                     device_id=(x, y, c), device_id_type=MESH).wait_recv()

    def start():
        for a in range(n):
            for j in range(3):
                over_ici(a, j).start()

    def pass_on():
        for a in range(n):
            for j in range(3):
                wait_arrival(a, j, landed(a, j, c))
                to_sibling(a, j).start()

    def finish():
        for a in range(n):
            for j in range(3):
                wait_arrival(a, 3 + j, landed(a, j, 1 - c))
        for a in range(n):
            for j in range(3):
                over_ici(a, j).wait_send()
                to_sibling(a, j).wait_send()

    return start, pass_on, finish


def _gather_weights(fulls, layer):
    n = len(fulls)

    def body(*refs):
        for phase in _gather_phases(refs[:n], refs[n:2 * n], refs[2 * n], refs[2 * n + 1], layer):
            phase()

    return _pcall(body, name=f"gather_weights_{layer}", in_specs=[ANY] * n, out_specs=[ANY] * n,
                  out_shape=[jax.ShapeDtypeStruct(f.shape, f.dtype) for f in fulls],
                  input_output_aliases={a: a for a in range(n)},
                  scratch_shapes=[pltpu.SemaphoreType.DMA((n, 6)), pltpu.SemaphoreType.DMA((n, 6))])(*fulls)


def _swap_layers(stacks):
    n = len(stacks)

    def body(*refs):
        srcs, outs = refs[:n], refs[n:2 * n]
        ssem, rsem = refs[2 * n:]
        x, y, c = _place()
        cps = [pltpu.make_async_remote_copy(src_ref=srcs[a].at[1 - c], dst_ref=outs[a], send_sem=ssem.at[a],
                                            recv_sem=rsem.at[a], device_id=(x, y, 1 - c), device_id_type=MESH)
               for a in range(n)]
        for cp in cps:
            cp.start()
        for cp in cps:
            cp.wait()

    return _pcall(body, name="swap_layers", in_specs=[ANY] * n, out_specs=[ANY] * n,
                  out_shape=[jax.ShapeDtypeStruct(st.shape[1:], st.dtype) for st in stacks],
                  scratch_shapes=[pltpu.SemaphoreType.DMA((n,)), pltpu.SemaphoreType.DMA((n,))])(*stacks)


def _scatter_shards(fulls):
    n = len(fulls)
    shard_shapes = []
    for f, ax in zip(fulls, SHARD_AXES):
        sh = list(f.shape)
        sh[ax] //= 4
        shard_shapes.append(tuple(sh))

    def body(*refs):
        srcs, outs = refs[:n], refs[n:2 * n]
        ssem, rsem = refs[2 * n:]
        x, y, c = _place()
        chips = [(1 - x, y), (x, 1 - y), (1 - x, 1 - y)]
        cps = []
        for a in range(n):
            for j, (cx, cy) in enumerate(chips):
                src = _shard_view(srcs[a], SHARD_AXES[a], 2 * cx + cy, shard_shapes[a][SHARD_AXES[a]])
                cps.append(pltpu.make_async_remote_copy(
                    src_ref=src, dst_ref=outs[a].at[j], send_sem=ssem.at[a, j], recv_sem=rsem.at[a, j],
                    device_id=(cx, cy, c), device_id_type=MESH))
        for cp in cps:
            cp.start()
        for cp in cps:
            cp.wait()

    return _pcall(body, name="scatter_shards", in_specs=[ANY] * n, out_specs=[ANY] * n,
                  out_shape=[jax.ShapeDtypeStruct((3,) + sh, f.dtype) for sh, f in zip(shard_shapes, fulls)],
                  scratch_shapes=[pltpu.SemaphoreType.DMA((n, 3)), pltpu.SemaphoreType.DMA((n, 3))])(*fulls)


def _pair_layers(stacks):
    n = len(stacks)

    def body(*refs):
        ins, outs = refs[:n], refs[n:2 * n]
        ssem, rsem = refs[2 * n:]
        x, y, c = _place()
        cps = [pltpu.make_async_remote_copy(src_ref=ins[a].at[c], dst_ref=outs[a].at[c], send_sem=ssem.at[a],
                                            recv_sem=rsem.at[a], device_id=(x, y, 1 - c), device_id_type=MESH)
               for a in range(n)]
        for cp in cps:
            cp.start()
        for a in range(n):
            got = outs[a].at[1 - c]
            pltpu.make_async_remote_copy(src_ref=got, dst_ref=got, send_sem=ssem.at[a], recv_sem=rsem.at[a],
                                         device_id=(x, y, 1 - c), device_id_type=MESH).wait_recv()
        for cp in cps:
            cp.wait_send()

    return _pcall(body, name="pair_layers", in_specs=[ANY] * n, out_specs=[ANY] * n,
                  out_shape=[jax.ShapeDtypeStruct(st.shape, st.dtype) for st in stacks],
                  input_output_aliases={a: a for a in range(n)},
                  scratch_shapes=[pltpu.SemaphoreType.DMA((n,)), pltpu.SemaphoreType.DMA((n,))])(*stacks)


def _exchange_small(pack, name):
    nd = 8

    def body(p_ref, o_ref, ssem, rsem):
        x, y, c = _place()
        me = 4 * x + 2 * y + c
        o_ref[me] = p_ref[...]
        cps = []
        for j in range(1, nd):
            px, py, pc = x ^ (j >> 2), y ^ ((j >> 1) & 1), c ^ (j & 1)
            cps.append(pltpu.make_async_remote_copy(
                src_ref=p_ref, dst_ref=o_ref.at[me], send_sem=ssem.at[j - 1], recv_sem=rsem.at[j - 1],
                device_id=(px, py, pc), device_id_type=MESH))
        for cp in cps:
            cp.start()
        for j in range(1, nd):
            peer = me ^ j
            got = o_ref.at[peer]
            pltpu.make_async_remote_copy(src_ref=got, dst_ref=got, send_sem=ssem.at[j - 1], recv_sem=rsem.at[j - 1],
                                         device_id=(x, y, c), device_id_type=MESH).wait_recv()
        for cp in cps:
            cp.wait_send()

    vm = pl.BlockSpec(memory_space=pltpu.VMEM)
    return _pcall(body, name=name, in_specs=[vm], out_specs=vm,
                  out_shape=jax.ShapeDtypeStruct((nd,) + pack.shape, pack.dtype),
                  scratch_shapes=[pltpu.SemaphoreType.DMA((nd - 1,)), pltpu.SemaphoreType.DMA((nd - 1,))])(pack)


def _row(v):
    return v.reshape(1, -1)


def _local_step(x, p, tgt, norm_g, conv_w, conv_b, branch_g, ple_norm_g, b_pg, final_g, w_in, w_out, w_pg, w_pe,
                gather_later_layers=False):
    saved = []
    xl = x
    for l in range(DEPTH):
        h, proj = _inproj(xl, _row(norm_g[l]), w_in, l, f"inproj_{l}")
        if gather_later_layers and l + 1 < DEPTH:
            ya, tl, walked, (w_in, w_out, w_pg, w_pe) = _attn_fwd(
                proj, f"attn_fwd_{l}", weights=(w_in, w_out, w_pg, w_pe), gather_layer=l + 1)
        else:
            ya, tl, walked, _ = _attn_fwd(proj, f"attn_fwd_{l}")
        y = _mix_fwd(proj, ya, conv_w[l], _row(conv_b[l]), _row(branch_g[l]), f"mix_fwd_{l}")
        x1, hn = _outproj(y, w_out, l, xl, _row(ple_norm_g[l]), f"outproj_{l}")
        x2, gate, e = _ple_fwd(hn, w_pg, _row(b_pg[l]), p, w_pe, l, x1, f"ple_fwd_{l}")
        saved.append((xl, h, proj, ya, tl, walked, y, x1, hn, gate, e))
        xl = x2

    sq, dx, d_final = _loss_head(xl, tgt, _row(final_g), "loss_head")

    g_in = g_out = g_pg = g_pe = None
    small = {k: [None] * DEPTH for k in ("norm_g", "conv_w", "conv_b", "branch_g", "ple_norm_g", "b_pg")}
    for l in reversed(range(DEPTH)):
        xl, h, proj, ya, tl, walked, y, x1, hn, gate, e = saved[l]
        du, de, dx1, dy, db_pg, d_ple = _ple_bwd(dx, gate, e, x1, w_pg, _row(ple_norm_g[l]), w_out, l, f"ple_bwd_{l}")
        g_pg = _mm_tn(hn, du, g_pg, l, f"grad_w_pg_{l}")
        g_pe = _mm_tn(p, de, g_pe, l, f"grad_w_pe_{l}", a_stacked=True)
        g_out = _mm_tn(y, dx1, g_out, l, f"grad_w_out_{l}")
        dpc, d_cw, d_cb, d_bg_c = _convmix_bwd(dy, proj, conv_w[l], _row(conv_b[l]), _row(branch_g[l]), f"convmix_bwd_{l}")
        dproj, d_bg_a = _attn_bwd(proj, dy, ya, tl, walked, _row(branch_g[l]), dpc, f"attn_bwd_{l}")
        g_in = _grad_w_in(h, dproj, g_in, l, f"grad_w_in_{l}")
        dx, d_norm = _inproj_bwd(dproj, w_in, l, xl, _row(norm_g[l]), dx1, f"inproj_bwd_{l}")
        small["norm_g"][l] = jnp.sum(d_norm, axis=0)
        small["conv_w"][l] = jnp.sum(d_cw, axis=1)
        small["conv_b"][l] = jnp.sum(d_cb, axis=0)
        small["branch_g"][l] = jnp.concatenate([jnp.sum(d_bg_c, axis=0), jnp.sum(d_bg_a, axis=0)])
        small["ple_norm_g"][l] = jnp.sum(d_ple, axis=0)
        small["b_pg"][l] = jnp.sum(db_pg, axis=0)
    small = {k: jnp.stack(v) for k, v in small.items()}
    small["final_g"] = jnp.sum(d_final, axis=0)
    return sq[0, 0], dx, (g_in, g_out, g_pg, g_pe), small


SMALL_ORDER = ("norm_g", "conv_w", "conv_b", "branch_g", "ple_norm_g", "b_pg", "final_g")


def _pack(parts, width):
    flat = jnp.concatenate([v.reshape(-1) for v in parts])
    rows = -(-flat.shape[0] // width)
    rows = -(-rows // 8) * 8
    return jnp.pad(flat, (0, rows * width - flat.shape[0])).reshape(rows, width)


def _unpack(packed, like):
    flat = packed.reshape(-1)
    out, off = [], 0
    for v in like:
        out.append(flat[off:off + v.size].reshape(v.shape))
        off += v.size
    return out


def kernel(x, p, norm_g, w_in, conv_w, conv_b, branch_g, w_out, ple_norm_g, w_pg, b_pg, w_pe, final_g, loss_target, m_norm_g, m_w_in, m_conv_w, m_conv_b, m_branch_g, m_w_out, m_ple_norm_g, m_w_pg, m_b_pg, m_w_pe, m_final_g, v_norm_g, v_w_in, v_conv_w, v_conv_b, v_branch_g, v_w_out, v_ple_norm_g, v_w_pg, v_b_pg, v_w_pe, v_final_g):
    ix, iy, ic = _place()
    chip = 2 * ix + iy
    d = x.shape[-1]

    big_w = (w_in, w_out, w_pg, w_pe)
    own = [_cast_into_full(w, chip, ax, f"cast_{i}") for i, (w, ax) in enumerate(zip(big_w, SHARD_AXES))]
    full_in, full_out, full_pg, full_pe = _gather_weights(own, 0)
    cw_shard = conv_w.shape[-1]
    cw_slots = _exchange_small(_pack([conv_w], LANES), "exchange_conv_w")
    conv_full = jnp.concatenate([_unpack(cw_slots[2 * k], [conv_w])[0] for k in range(4)], axis=-1)

    sq, dx, big_g, small_g = _local_step(
        x[0], p[:, 0], loss_target[0], norm_g, conv_full, conv_b, branch_g, ple_norm_g, b_pg, final_g,
        full_in, full_out, full_pg, full_pe, gather_later_layers=True)

    from_sibling = _swap_layers(big_g)
    chip_sums = [_add_layer(g, o, ic, f"add_layer_{i}") for i, (g, o) in enumerate(zip(big_g, from_sibling))]
    partials = _scatter_shards([narrow for _, narrow in chip_sums])
    reduced = [_sum_shard(wide, pr, chip, ic, ax, f"sum_shard_{i}")
               for i, ((wide, _), pr, ax) in enumerate(zip(chip_sums, partials, SHARD_AXES))]
    g_big = _pair_layers(reduced)

    parts = [small_g[k] for k in SMALL_ORDER] + [sq.reshape(1)]
    slots = _exchange_small(_pack(parts, d), "exchange_small_grads")
    total = _unpack(_sum_slots(slots, "sum_small"), parts)
    g_small = dict(zip(SMALL_ORDER, total[:-1]))
    loss = 0.5 * total[-1][0] / d
    g_small["conv_w"] = lax.dynamic_slice_in_dim(g_small["conv_w"], chip * cw_shard, cw_shard, axis=2)

    grads = dict(g_small)
    grads.update(w_in=g_big[0], w_out=g_big[1], w_pg=g_big[2], w_pe=g_big[3])
    weights = dict(norm_g=norm_g, w_in=w_in, conv_w=conv_w, conv_b=conv_b, branch_g=branch_g, w_out=w_out,
                   ple_norm_g=ple_norm_g, w_pg=w_pg, b_pg=b_pg, w_pe=w_pe, final_g=final_g)
    ms = dict(norm_g=m_norm_g, w_in=m_w_in, conv_w=m_conv_w, conv_b=m_conv_b, branch_g=m_branch_g, w_out=m_w_out,
              ple_norm_g=m_ple_norm_g, w_pg=m_w_pg, b_pg=m_b_pg, w_pe=m_w_pe, final_g=m_final_g)
    vs = dict(norm_g=v_norm_g, w_in=v_w_in, conv_w=v_conv_w, conv_b=v_conv_b, branch_g=v_branch_g, w_out=v_w_out,
              ple_norm_g=v_ple_norm_g, w_pg=v_w_pg, b_pg=v_b_pg, w_pe=v_w_pe, final_g=v_final_g)
    names = ("norm_g", "w_in", "conv_w", "conv_b", "branch_g", "w_out", "ple_norm_g", "w_pg", "b_pg", "w_pe", "final_g")
    delta, new_m, new_v = {}, {}, {}
    for k in ("w_in", "w_out", "w_pg", "w_pe"):
        shp = weights[k].shape
        two = lambda a: a.reshape(-1, shp[-1])
        dl, mn, vn = _adamw(two(weights[k]), two(grads[k]), two(ms[k]), two(vs[k]), f"adamw_{k}")
        delta[k], new_m[k], new_v[k] = dl.reshape(shp), mn.reshape(shp), vn.reshape(shp)
        grads[k] = grads[k].reshape(shp)
    like = [weights[k] for k in SMALL_ORDER]
    packs = [_pack([src[k] for k in SMALL_ORDER], d) for src in (weights, grads, ms, vs)]
    outs = _adamw(*packs, "adamw_small")
    for res, o in zip((delta, new_m, new_v), outs):
        res.update(dict(zip(SMALL_ORDER, _unpack(o, like))))

    return (loss, dx[None], *[grads[k] for k in names], *[delta[k] for k in names],
            *[new_m[k] for k in names], *[new_v[k] for k in names])
```

```python
import math

import jax
import jax.numpy as jnp
from jax import lax
from jax.experimental import pallas as pl
from jax.experimental.pallas import tpu as pltpu

F32 = jnp.float32
BF16 = jnp.bfloat16
EPS = 1e-6
HEAD = 64
LANES = 128
ATT_TK = 256
ATT_TQ = 512
ATT_ROWS = 128
ALIVE_LOG = -105.0
DEPTH = 2
VMEM_LIMIT = 56 * 1024 * 1024
MESH = pl.DeviceIdType.MESH
ANY = pl.BlockSpec(memory_space=pl.ANY)

ADAM_LR = 0.001
ADAM_B1 = 0.9
ADAM_B2 = 0.999
ADAM_EPS = 1e-08
ADAM_WD = 0.01
ADAM_STEP = 10


def _pcall(body, **kw):
    return pl.pallas_call(body, **kw)


def _cp(n_axes):
    return pltpu.CompilerParams(dimension_semantics=("arbitrary",) * n_axes, vmem_limit_bytes=VMEM_LIMIT)


def _tile(n, pref):
    return pref if n % pref == 0 else n


def _split_dot(a, b, passes):
    out = None
    rem = a
    for _ in range(passes):
        hi = rem.astype(BF16)
        t = jnp.dot(hi, b, preferred_element_type=F32)
        out = t if out is None else out + t
        rem = rem - hi.astype(F32)
    return out


def _group_mat():
    r = lax.broadcasted_iota(jnp.int32, (LANES, LANES), 0) // HEAD
    c = lax.broadcasted_iota(jnp.int32, (LANES, LANES), 1) // HEAD
    return jnp.where(r == c, 1.0 / HEAD, 0.0).astype(BF16)


def _group_mean(v, gm):
    return _split_dot(v, gm, 3)


def _sigmoid(z):
    return 1.0 / (1.0 + jnp.exp(-z))


def _dot_nt(a, b):
    return lax.dot_general(a, b, (((1,), (1,)), ((), ())), preferred_element_type=F32)


def _dot_tn(a, b):
    return lax.dot_general(a, b, (((0,), (0,)), ((), ())), preferred_element_type=F32)


def _cast_into_full(w, chip, axis, name):
    _, r, c = w.shape
    tr = _tile(r, 256)
    nb = r // tr
    full = (DEPTH, 4 * r, c) if axis == 0 else (DEPTH, r, 4 * c)

    def body(k_ref, w_ref, o_ref):
        o_ref[...] = w_ref[...].astype(BF16)

    out_map = (lambda l, i, k: (l, k[0] * nb + i, 0)) if axis == 0 else (lambda l, i, k: (l, i, k[0]))
    grid_spec = pltpu.PrefetchScalarGridSpec(
        num_scalar_prefetch=1, grid=(DEPTH, nb),
        in_specs=[pl.BlockSpec((1, tr, c), lambda l, i, k: (l, i, 0))],
        out_specs=pl.BlockSpec((1, tr, c), out_map))
    return _pcall(body, name=name, grid_spec=grid_spec, out_shape=jax.ShapeDtypeStruct(full, BF16),
                  compiler_params=_cp(2))(chip.reshape(1).astype(jnp.int32), w)


def _rms_bwd_rows(dh, xv, g):
    r = lax.rsqrt(jnp.mean(xv * xv, axis=-1, keepdims=True) + EPS)
    xn = xv * r
    dxn = dh * g
    dx = r * (dxn - xn * jnp.mean(dxn * xn, axis=-1, keepdims=True))
    return dx, dh * xn


def _colsum8(v):
    tm, d = v.shape
    return jnp.sum(v.reshape(tm // 8, 8, d), axis=0)


def _inproj(x, g, w, layer, name):
    s, d = x.shape
    n = w.shape[2]
    sw = d // 2
    ns = n // sw
    tm = _tile(s, 512)

    def body(x_ref, g_ref, w_ref, h_ref, o_ref):
        xv = x_ref[...]
        r = lax.rsqrt(jnp.mean(xv * xv, axis=-1, keepdims=True) + EPS)
        h = (xv * r * g_ref[...]).astype(BF16)
        h_ref[...] = h
        for k in range(ns):
            o_ref[k] = jnp.dot(h, w_ref[0, :, k * sw:(k + 1) * sw], preferred_element_type=F32).astype(BF16)

    return _pcall(body, name=name, grid=(s // tm,),
                  in_specs=[pl.BlockSpec((tm, d), lambda m: (m, 0)), pl.BlockSpec((1, d), lambda m: (0, 0)),
                            pl.BlockSpec((1, d, n), lambda m: (layer, 0, 0))],
                  out_specs=[pl.BlockSpec((tm, d), lambda m: (m, 0)), pl.BlockSpec((ns, tm, sw), lambda m: (0, m, 0))],
                  out_shape=[jax.ShapeDtypeStruct((s, d), BF16), jax.ShapeDtypeStruct((ns, s, sw), BF16)],
                  compiler_params=_cp(1))(x, g, w)


def _softplus_parts(z):
    lm = jnp.minimum(-z, 0.0) - jnp.log(1.0 + jnp.exp(-jnp.abs(z)))
    return lm, lm + z


def _attn_tiles(s):
    tk = _tile(s, ATT_TK)
    tq = _tile(s, ATT_TQ)
    return tk, tq, tq // tk, min(ATT_ROWS, tq)


def _diag_work(chains, d, rows, tk):
    work = []
    for n, (_, r0) in enumerate(chains):
        if r0 + rows - 1 <= d * tk:
            continue
        kw = tk // 2 if (tk % 2 == 0 and r0 + rows <= d * tk + tk // 2) else tk
        if r0 >= d * tk + kw:
            mask = None
        else:
            row = lax.broadcasted_iota(jnp.int32, (rows, kw), 0)
            col = lax.broadcasted_iota(jnp.int32, (rows, kw), 1)
            mask = col + d * tk < row + r0
        work.append((n, kw, mask))
    return work


def _walk_work(chains, tk, below=None):
    return [(n, tk, None) for n, (_, r0) in enumerate(chains) if below is None or r0 < below]


def _any_alive(rsums):
    m = rsums[0]
    for r in rsums[1:]:
        m = jnp.maximum(m, r)
    return jnp.max((m > ALIVE_LOG).astype(jnp.int32))


def _attn_fwd(proj, name, weights=None, gather_layer=None):
    _, s, sw = proj.shape
    nhp = sw // LANES
    tk, tq, nd, rows = _attn_tiles(s)
    nq = s // tq
    scale = 1.0 / math.sqrt(HEAD)

    nw = 0 if weights is None else len(weights)

    def body(*refs):
        q_ref, k_ref, v_ref = refs[:3]
        o_ref, tl_ref, nw_ref = refs[3 + nw:6 + nw]
        acc_ref = refs[6 + 2 * nw]
        i = pl.program_id(1)
        if weights is not None:
            start, pass_on, finish = _gather_phases(refs[3:3 + nw], refs[6 + nw:6 + 2 * nw], refs[-2], refs[-1],
                                                    gather_layer)
            step = pl.program_id(0) * nq + i
            pl.when(step == 0)(start)
            pl.when(step == (3 * nhp * nq) // 4)(pass_on)
        tri = (lax.broadcasted_iota(jnp.int32, (tk, tk), 0) >
               lax.broadcasted_iota(jnp.int32, (tk, tk), 1)).astype(BF16)
        lane = lax.broadcasted_iota(jnp.int32, (tq, LANES), 1)
        q = q_ref[0] * jnp.asarray(scale, BF16)
        qms = [jnp.where((lane // HEAD) == h, q, jnp.zeros_like(q)) for h in range(2)]
        acc_ref[...] = jnp.zeros_like(acc_ref)
        chains = [(h, r0) for h in range(2) for r0 in range(0, tq, rows)]
        qparts = [qms[h][r0:r0 + rows] for h, r0 in chains]

        def tile(j, rsums, work):
            k0 = pl.multiple_of(j * tk, tk)
            kj = k_ref[0, pl.ds(k0, tk), :]
            vj = v_ref[0, pl.ds(k0, tk), :]
            zs = [_dot_nt(qparts[n], kj[:kw]) for n, kw, _ in work]
            lms, lss, css = [], [], []
            for z, (n, kw, mask) in zip(zs, work):
                lm, ls = _softplus_parts(z)
                if mask is not None:
                    lm = jnp.where(mask, lm, 0.0)
                lms.append(lm)
                lss.append(ls)
                css.append(_split_dot(lm, tri[:kw, :kw], 2))
            out = list(rsums)
            for lm, ls, cs, (n, kw, mask) in zip(lms, lss, css, work):
                h, r0 = chains[n]
                a = jnp.exp(ls + (rsums[n] + cs))
                if mask is not None:
                    a = jnp.where(mask, a, 0.0)
                acc_ref[h, r0:r0 + rows, :] += jnp.dot(a.astype(BF16), vj[:kw], preferred_element_type=F32)
                out[n] = rsums[n] + jnp.sum(lm, axis=1, keepdims=True)
            return tuple(out)

        rsums = (jnp.zeros((rows, 1), F32),) * len(chains)
        for d in reversed(range(nd)):
            rsums = tile(i * nd + d, rsums, _diag_work(chains, d, rows, tk))

        upper = [rs for rs, (_, r0) in zip(rsums, chains) if r0 >= tk]
        lower = [rs for rs, (_, r0) in zip(rsums, chains) if r0 < tk]
        if upper:
            alone = (i > 0) & (_any_alive(upper) == 0) & (_any_alive(lower) > 0)
            rsums = lax.cond(alone, lambda rs: tile(i * nd - 1, rs, _walk_work(chains, tk, below=tk)),
                             lambda rs: rs, rsums)
            alone = alone.astype(jnp.int32)
        else:
            alone = jnp.int32(0)

        def walk(c):
            jj, rs, _ = c
            rs = tile(i * nd - 1 - jj, rs, _walk_work(chains, tk))
            return jj + 1, rs, _any_alive(rs)

        last, rsums, _ = lax.while_loop(lambda c: (c[0] < i * nd) & (c[2] > 0), walk,
                                        (alone, rsums, _any_alive(rsums)))
        for n, (h, r0) in enumerate(chains):
            tl_ref[h, r0:r0 + rows, :] = rsums[n]
        nw_ref[0] = (jnp.zeros((8, LANES), jnp.int32) + (2 * (last - alone) + alone)).astype(F32)
        o_ref[...] = jnp.where(lane < HEAD, acc_ref[0], acc_ref[1]).astype(BF16)
        if weights is not None:
            pl.when(step == nhp * nq - 1)(finish)

    hosted = [] if weights is None else list(weights)
    sems = [] if weights is None else [pltpu.SemaphoreType.DMA((nw, 6)), pltpu.SemaphoreType.DMA((nw, 6))]
    outs = _pcall(
        body, name=name, grid=(nhp, nq),
        in_specs=[pl.BlockSpec((1, tq, LANES), lambda hp, i: (4, i, hp)),
                  pl.BlockSpec((1, s, LANES), lambda hp, i: (5, 0, hp)),
                  pl.BlockSpec((1, s, LANES), lambda hp, i: (6, 0, hp))] + [ANY] * nw,
        out_specs=[pl.BlockSpec((tq, LANES), lambda hp, i: (i, hp)),
                   pl.BlockSpec((2, tq, 1), lambda hp, i: (hp, i, 0)),
                   pl.BlockSpec((1, 8, LANES), lambda hp, i: (hp * nq + i, 0, 0))] + [ANY] * nw,
        out_shape=[jax.ShapeDtypeStruct((s, sw), BF16), jax.ShapeDtypeStruct((2 * nhp, s, 1), F32),
                   jax.ShapeDtypeStruct((nhp * nq, 8, LANES), F32)]
        + [jax.ShapeDtypeStruct(w.shape, w.dtype) for w in hosted],
        input_output_aliases={3 + a: 3 + a for a in range(nw)},
        scratch_shapes=[pltpu.VMEM((2, tq, LANES), F32)] + sems,
        compiler_params=_cp(2))(proj, proj, proj, *hosted)
    return outs[0], outs[1], outs[2], tuple(outs[3:])


def _conv_rows(cc_ref, ch_ref, w_ref, b_ref, r, tc):
    r0 = pl.multiple_of(r * tc, tc)
    u = cc_ref[0, pl.ds(r0, tc), :].astype(F32) * ch_ref[0, pl.ds(r0, tc), :].astype(F32)
    p0 = pl.multiple_of(jnp.maximum(r0 - 16, 0), 16)
    up = cc_ref[0, pl.ds(p0, 16), :].astype(F32) * ch_ref[0, pl.ds(p0, 16), :].astype(F32)
    up = up * (r > 0).astype(F32)
    prev1 = up[15:16, :]
    prev2 = up[14:15, :]
    rid = lax.broadcasted_iota(jnp.int32, u.shape, 0)
    s1 = jnp.where(rid == 0, prev1, pltpu.roll(u, 1, axis=0))
    s2 = jnp.where(rid == 0, prev2, jnp.where(rid == 1, prev1, pltpu.roll(u, 2, axis=0)))
    cv = b_ref[...] + s2 * w_ref[0:1, :] + s1 * w_ref[1:2, :] + u * w_ref[2:3, :]
    return r0, u, s1, s2, cv


def _mix_fwd(proj, ya, conv_w, conv_b, bg, name):
    _, s, sw = proj.shape
    nh = sw // LANES
    tc = _tile(s, 256)

    def body(cb_ref, cc_ref, ch_ref, cz_ref, ya_ref, az_ref, w_ref, b_ref, g_ref, y_ref):
        c = pl.program_id(0)
        gm = _group_mat()

        def finish(r0, yv, zg):
            n = yv * lax.rsqrt(_group_mean(yv * yv, gm) + EPS)
            y_ref[pl.ds(r0, tc), :] = (n * g_ref[...] * (zg * _sigmoid(zg))).astype(BF16)

        @pl.when(c < nh)
        def _():
            def step(r, carry):
                r0, _, _, _, cv = _conv_rows(cc_ref, ch_ref, w_ref, b_ref, r, tc)
                yc = cb_ref[0, pl.ds(r0, tc), :].astype(F32) * cv
                finish(r0, yc, cz_ref[0, pl.ds(r0, tc), :].astype(F32))
                return carry
            lax.fori_loop(0, s // tc, step, 0)

        @pl.when(c >= nh)
        def _():
            def step(r, carry):
                r0 = pl.multiple_of(r * tc, tc)
                finish(r0, ya_ref[pl.ds(r0, tc), :].astype(F32), az_ref[0, pl.ds(r0, tc), :].astype(F32))
                return carry
            lax.fori_loop(0, s // tc, step, 0)

    def sec(k):
        return pl.BlockSpec((1, s, LANES), lambda c: (k, 0, jnp.minimum(c, nh - 1)))

    return _pcall(
        body, name=name, grid=(2 * nh,),
        in_specs=[sec(0), sec(1), sec(2), sec(3),
                  pl.BlockSpec((s, LANES), lambda c: (0, jnp.maximum(c - nh, 0))),
                  pl.BlockSpec((1, s, LANES), lambda c: (7, 0, jnp.maximum(c - nh, 0))),
                  pl.BlockSpec((3, LANES), lambda c: (0, jnp.minimum(c, nh - 1))),
                  pl.BlockSpec((1, LANES), lambda c: (0, jnp.minimum(c, nh - 1))),
                  pl.BlockSpec((1, LANES), lambda c: (0, c))],
        out_specs=pl.BlockSpec((s, LANES), lambda c: (0, c)),
        out_shape=jax.ShapeDtypeStruct((s, 2 * sw), BF16), compiler_params=_cp(1),
    )(proj, proj, proj, proj, ya, proj, conv_w, conv_b, bg)


def _outproj(y, w, layer, x, g, name):
    s, d = x.shape
    tm = _tile(s, 256)

    def body(y_ref, w_ref, x_ref, g_ref, x1_ref, hn_ref):
        x1 = x_ref[...] + jnp.dot(y_ref[...], w_ref[0], preferred_element_type=F32)
        x1_ref[...] = x1
        r = lax.rsqrt(jnp.mean(x1 * x1, axis=-1, keepdims=True) + EPS)
        hn_ref[...] = (x1 * r * g_ref[...]).astype(BF16)

    row = lambda m: (m, 0)
    fix = lambda m: (0, 0)
    return _pcall(body, name=name, grid=(s // tm,),
                  in_specs=[pl.BlockSpec((tm, d), row), pl.BlockSpec((1, d, d), lambda m: (layer, 0, 0)),
                            pl.BlockSpec((tm, d), row), pl.BlockSpec((1, d), fix)],
                  out_specs=[pl.BlockSpec((tm, d), row), pl.BlockSpec((tm, d), row)],
                  out_shape=[jax.ShapeDtypeStruct((s, d), F32), jax.ShapeDtypeStruct((s, d), BF16)],
                  compiler_params=_cp(1))(y, w, x, g)


def _ple_fwd(hn, w_pg, b_pg, p, w_pe, layer, x1, name):
    s, d = x1.shape
    pd = p.shape[2]
    tm = _tile(s, 256)

    def body(hn_ref, wg_ref, b_ref, p_ref, we_ref, x1_ref, x2_ref, gate_ref, e_ref):
        gate = _sigmoid(jnp.dot(hn_ref[...], wg_ref[0], preferred_element_type=F32) + b_ref[...])
        e = jnp.dot(p_ref[0].astype(BF16), we_ref[0], preferred_element_type=F32)
        x2_ref[...] = x1_ref[...] + gate * e
        gate_ref[...] = gate.astype(BF16)
        e_ref[...] = e.astype(BF16)

    row = lambda m: (m, 0)
    fix = lambda m: (0, 0)
    return _pcall(body, name=name, grid=(s // tm,),
                  in_specs=[pl.BlockSpec((tm, d), row), pl.BlockSpec((1, d, d), lambda m: (layer, 0, 0)),
                            pl.BlockSpec((1, d), fix), pl.BlockSpec((1, tm, pd), lambda m: (layer, m, 0)),
                            pl.BlockSpec((1, pd, d), lambda m: (layer, 0, 0)), pl.BlockSpec((tm, d), row)],
                  out_specs=[pl.BlockSpec((tm, d), row)] * 3,
                  out_shape=[jax.ShapeDtypeStruct((s, d), F32), jax.ShapeDtypeStruct((s, d), BF16),
                             jax.ShapeDtypeStruct((s, d), BF16)],
                  compiler_params=_cp(1))(hn, w_pg, b_pg, p, w_pe, x1)


def _loss_head(x, tgt, g, name):
    s, d = x.shape
    tm = _tile(s, 256)

    def body(x_ref, t_ref, g_ref, l_ref, dx_ref, dg_ref):
        m = pl.program_id(0)

        @pl.when(m == 0)
        def _():
            l_ref[...] = jnp.zeros_like(l_ref)
            dg_ref[...] = jnp.zeros_like(dg_ref)

        xv = x_ref[...]
        gv = g_ref[...]
        r = lax.rsqrt(jnp.mean(xv * xv, axis=-1, keepdims=True) + EPS)
        xn = xv * r
        err = xn * gv - t_ref[...]
        l_ref[...] += jnp.sum(err * err)
        dy = err * (1.0 / d)
        dxn = dy * gv
        dx_ref[...] = r * (dxn - xn * jnp.mean(dxn * xn, axis=-1, keepdims=True))
        dg_ref[...] += _colsum8(dy * xn)

    row = lambda m: (m, 0)
    fix = lambda m: (0, 0)
    return _pcall(body, name=name, grid=(s // tm,),
                  in_specs=[pl.BlockSpec((tm, d), row), pl.BlockSpec((tm, d), row), pl.BlockSpec((1, d), fix)],
                  out_specs=[pl.BlockSpec((8, LANES), fix), pl.BlockSpec((tm, d), row), pl.BlockSpec((8, d), fix)],
                  out_shape=[jax.ShapeDtypeStruct((8, LANES), F32), jax.ShapeDtypeStruct((s, d), F32),
                             jax.ShapeDtypeStruct((8, d), F32)],
                  compiler_params=_cp(1))(x, tgt, g)


def _ple_bwd(dx2, gate, e, x1, w_pg, g_ple, w_out, layer, name):
    s, d = dx2.shape
    tm = _tile(s, 256)

    def body(dx2_ref, gate_ref, e_ref, x1_ref, wg_ref, g_ref, wo_ref,
             du_ref, de_ref, dx1_ref, dy_ref, db_ref, dg_ref):
        m = pl.program_id(0)

        @pl.when(m == 0)
        def _():
            db_ref[...] = jnp.zeros_like(db_ref)
            dg_ref[...] = jnp.zeros_like(dg_ref)

        dx2v = dx2_ref[...]
        gate = gate_ref[...].astype(F32)
        du = dx2v * e_ref[...].astype(F32) * gate * (1.0 - gate)
        de_ref[...] = (dx2v * gate).astype(BF16)
        dub = du.astype(BF16)
        du_ref[...] = dub
        db_ref[...] += _colsum8(du)
        dhn = _dot_nt(dub, wg_ref[0])
        dxr, dgr = _rms_bwd_rows(dhn, x1_ref[...], g_ref[...])
        dx1 = dx2v + dxr
        dx1_ref[...] = dx1
        dg_ref[...] += _colsum8(dgr)
        dy_ref[...] = _dot_nt(dx1.astype(BF16), wo_ref[0]).astype(BF16)

    row = lambda m: (m, 0)
    fix = lambda m: (0, 0)
    t = pl.BlockSpec((tm, d), row)
    return _pcall(body, name=name, grid=(s // tm,),
                  in_specs=[t, t, t, t, pl.BlockSpec((1, d, d), lambda m: (layer, 0, 0)), pl.BlockSpec((1, d), fix),
                            pl.BlockSpec((1, d, d), lambda m: (layer, 0, 0))],
                  out_specs=[t, t, t, t, pl.BlockSpec((8, d), fix), pl.BlockSpec((8, d), fix)],
                  out_shape=[jax.ShapeDtypeStruct((s, d), BF16), jax.ShapeDtypeStruct((s, d), BF16),
                             jax.ShapeDtypeStruct((s, d), F32), jax.ShapeDtypeStruct((s, d), BF16),
                             jax.ShapeDtypeStruct((8, d), F32), jax.ShapeDtypeStruct((8, d), F32)],
                  compiler_params=_cp(1))(dx2, gate, e, x1, w_pg, g_ple, w_out)


def _mm_tn(a, b, name, a_layer=None):
    s, ka = a.shape[-2:]
    n = b.shape[1]
    tn = _tile(n, 1024)
    ns = n // tn
    tk = _tile(s, 512)
    nk = s // tk

    def body(a_ref, b_ref, o_ref, acc_ref):
        k = pl.program_id(1)

        @pl.when(k == 0)
        def _():
            acc_ref[...] = jnp.zeros_like(acc_ref)

        av = a_ref[...] if a_layer is None else a_ref[0]
        acc_ref[...] += _dot_tn(av.astype(BF16), b_ref[...].astype(BF16))

        @pl.when(k == nk - 1)
        def _():
            o_ref[...] = acc_ref[...]

    a_spec = (pl.BlockSpec((tk, ka), lambda j, k: (k, 0)) if a_layer is None
              else pl.BlockSpec((1, tk, ka), lambda j, k: (a_layer, k, 0)))
    return _pcall(body, name=name, grid=(ns, nk),
                  in_specs=[a_spec, pl.BlockSpec((tk, tn), lambda j, k: (k, j))],
                  out_specs=pl.BlockSpec((ka, tn), lambda j, k: (0, j)),
                  out_shape=jax.ShapeDtypeStruct((ka, n), F32),
                  scratch_shapes=[pltpu.VMEM((ka, tn), F32)], compiler_params=_cp(2))(a, b)


def _norm_gate_bwd(dy, yv, zg, g, gm):
    r = lax.rsqrt(_group_mean(yv * yv, gm) + EPS)
    n = yv * r
    sg = _sigmoid(zg)
    sil = zg * sg
    dzg = dy * n * g * (sg * (1.0 + zg * (1.0 - sg)))
    dn = dy * g * sil
    dyv = r * (dn - n * _group_mean(dn * n, gm))
    return dyv, dzg, dy * n * sil


def _convmix_bwd(dy, proj, conv_w, conv_b, bg, name):
    _, s, sw = proj.shape
    nh = sw // LANES
    tc = _tile(s, 256)
    nr = s // tc

    def body(dy_ref, cb_ref, cc_ref, ch_ref, cz_ref, w_ref, b_ref, g_ref,
             dp_ref, dw_ref, db_ref, dg_ref, dcv_ref):
        gm = _group_mat()
        dcv_ref[pl.ds(s, 8), :] = jnp.zeros((8, LANES), F32)

        def pass1(r, carry):
            dw0, dw1, dw2, db, dg = carry
            r0, u, s1, s2, cv = _conv_rows(cc_ref, ch_ref, w_ref, b_ref, r, tc)
            cb = cb_ref[0, pl.ds(r0, tc), :].astype(F32)
            dyc, dcz, dgr = _norm_gate_bwd(dy_ref[pl.ds(r0, tc), :].astype(F32), cb * cv,
                                           cz_ref[0, pl.ds(r0, tc), :].astype(F32), g_ref[...], gm)
            dp_ref[0, pl.ds(r0, tc), :] = (dyc * cv).astype(BF16)
            dp_ref[3, pl.ds(r0, tc), :] = dcz.astype(BF16)
            dcv = dyc * cb
            dcv_ref[pl.ds(r0, tc), :] = dcv
            return (dw0 + _colsum8(dcv * s2), dw1 + _colsum8(dcv * s1), dw2 + _colsum8(dcv * u),
                    db + _colsum8(dcv), dg + _colsum8(dgr))

        z8 = jnp.zeros((8, LANES), F32)
        dw0, dw1, dw2, db, dg = lax.fori_loop(0, nr, pass1, (z8, z8, z8, z8, z8))
        dw_ref[0] = dw0
        dw_ref[1] = dw1
        dw_ref[2] = dw2
        db_ref[...] = db
        dg_ref[...] = dg

        def pass2(r, carry):
            r0 = pl.multiple_of(r * tc, tc)
            dcv = dcv_ref[pl.ds(r0, tc), :]
            nxt = dcv_ref[pl.ds(pl.multiple_of(r0 + tc, 8), 8), :]
            rid = lax.broadcasted_iota(jnp.int32, dcv.shape, 0)
            n1 = jnp.where(rid == tc - 1, nxt[0:1, :], pltpu.roll(dcv, tc - 1, axis=0))
            n2 = jnp.where(rid == tc - 1, nxt[1:2, :],
                           jnp.where(rid == tc - 2, nxt[0:1, :], pltpu.roll(dcv, tc - 2, axis=0)))
            du = dcv * w_ref[2:3, :] + n1 * w_ref[1:2, :] + n2 * w_ref[0:1, :]
            dp_ref[1, pl.ds(r0, tc), :] = (du * ch_ref[0, pl.ds(r0, tc), :].astype(F32)).astype(BF16)
            dp_ref[2, pl.ds(r0, tc), :] = (du * cc_ref[0, pl.ds(r0, tc), :].astype(F32)).astype(BF16)
            return carry

        lax.fori_loop(0, nr, pass2, 0)

    def sec(k):
        return pl.BlockSpec((1, s, LANES), lambda c: (k, 0, c))

    col = lambda c: (0, c)
    return _pcall(
        body, name=name, grid=(nh,),
        in_specs=[pl.BlockSpec((s, LANES), col), sec(0), sec(1), sec(2), sec(3),
                  pl.BlockSpec((3, LANES), col), pl.BlockSpec((1, LANES), col), pl.BlockSpec((1, LANES), col)],
        out_specs=[pl.BlockSpec((4, s, LANES), lambda c: (0, 0, c)), pl.BlockSpec((3, 8, LANES), lambda c: (0, 0, c)),
                   pl.BlockSpec((8, LANES), col), pl.BlockSpec((8, LANES), col)],
        out_shape=[jax.ShapeDtypeStruct((8, s, sw), BF16), jax.ShapeDtypeStruct((3, 8, sw), F32),
                   jax.ShapeDtypeStruct((8, sw), F32), jax.ShapeDtypeStruct((8, sw), F32)],
        scratch_shapes=[pltpu.VMEM((s + 8, LANES), F32)], compiler_params=_cp(1),
    )(dy, proj, proj, proj, proj, conv_w, conv_b, bg)


def _attn_bwd(proj, dy, ya, tl, walked, bg, buf, name, scatter=None):
    _, s, sw = proj.shape
    nhp = sw // LANES
    tk, t, nd, rows_c = _attn_tiles(s)
    nq = s // t
    scale = 1.0 / math.sqrt(HEAD)

    ns = 0 if scatter is None else len(scatter)

    def body(*refs):
        q_ref, k_ref, v_ref, az_ref, dy_ref, ya_ref, tl_ref, nw_ref, g_ref, buf_ref = refs[:10]
        out_ref, dg_ref = refs[10 + ns:12 + ns]
        dka_ref, dva_ref, dqa_ref = refs[12 + 2 * ns:15 + 2 * ns]
        step = pl.program_id(1)
        i = nq - 1 - step
        if scatter is not None:
            start, finish = _scatter_phases(refs[10:10 + ns], refs[12 + ns:12 + 2 * ns], refs[-2], refs[-1])
            pl.when((pl.program_id(0) == 0) & (step == 0))(start)

        @pl.when(step == 0)
        def _():
            dka_ref[...] = jnp.zeros_like(dka_ref)
            dva_ref[...] = jnp.zeros_like(dva_ref)
            dg_ref[...] = jnp.zeros_like(dg_ref)

        dyv, dzg, dgr = _norm_gate_bwd(dy_ref[...].astype(F32), ya_ref[...].astype(F32), az_ref[0].astype(F32),
                                       g_ref[...], _group_mat())
        out_ref[3] = dzg.astype(BF16)
        dg_ref[...] += _colsum8(dgr)

        tri = (lax.broadcasted_iota(jnp.int32, (tk, tk), 0) <=
               lax.broadcasted_iota(jnp.int32, (tk, tk), 1)).astype(BF16)
        lane = lax.broadcasted_iota(jnp.int32, (t, LANES), 1)
        q = q_ref[0] * jnp.asarray(scale, BF16)
        do = dyv.astype(BF16)
        qms = [jnp.where((lane // HEAD) == h, q, jnp.zeros_like(q)) for h in range(2)]
        doms = [jnp.where((lane // HEAD) == h, do, jnp.zeros_like(do)) for h in range(2)]
        dqa_ref[...] = jnp.zeros_like(dqa_ref)
        chains = [(h, r0) for h in range(2) for r0 in range(0, t, rows_c)]
        qparts = [qms[h][r0:r0 + rows_c] for h, r0 in chains]
        doparts = [doms[h][r0:r0 + rows_c] for h, r0 in chains]
        tots = [tl_ref[h, r0:r0 + rows_c, :] for h, r0 in chains]

        def tile(j, carry, work):
            k0 = pl.multiple_of(j * tk, tk)
            kj = k_ref[0, pl.ds(k0, tk), :]
            vj = v_ref[0, pl.ds(k0, tk), :]
            zs = [_dot_nt(qparts[n], kj[:kw]) for n, kw, _ in work]
            das = [_dot_nt(doparts[n], vj[:kw]) for n, kw, _ in work]
            lms, lss, cls = [], [], []
            for z, (n, kw, mask) in zip(zs, work):
                lm, ls = _softplus_parts(z)
                if mask is not None:
                    lm = jnp.where(mask, lm, 0.0)
                lms.append(lm)
                lss.append(ls)
                cls.append(_split_dot(lm, tri[:kw, :kw], 2))
            abs_, gs, cgs = [], [], []
            for ls, cl, da, (n, kw, mask) in zip(lss, cls, das, work):
                a = jnp.exp(ls + (tots[n] - carry[n][0] - cl))
                if mask is not None:
                    a = jnp.where(mask, a, 0.0)
                g = a * da
                gs.append(g)
                abs_.append(a.astype(BF16))
                cgs.append(_split_dot(g, tri[:kw, :kw], 1))
            out = list(carry)
            dks, dvs = {}, {}
            for lm, ls, a, g, cg, (n, kw, mask) in zip(lms, lss, abs_, gs, cgs, work):
                h, r0 = chains[n]
                psum, gsum = carry[n]
                dz = g - jnp.exp(ls) * (gsum + cg)
                if mask is not None:
                    dz = jnp.where(mask, dz, 0.0)
                dz = dz.astype(BF16)
                dqa_ref[h, r0:r0 + rows_c, :] += jnp.dot(dz, kj[:kw], preferred_element_type=F32)
                dkh = _dot_tn(dz, qparts[n])
                dvh = _dot_tn(a, doparts[n])
                dks[kw] = dkh if kw not in dks else dks[kw] + dkh
                dvs[kw] = dvh if kw not in dvs else dvs[kw] + dvh
                out[n] = (psum + jnp.sum(lm, axis=1, keepdims=True), gsum + jnp.sum(g, axis=1, keepdims=True))
            for kw in dks:
                dka_ref[pl.ds(k0, kw), :] += dks[kw]
                dva_ref[pl.ds(k0, kw), :] += dvs[kw]
            return tuple(out)

        z1 = jnp.zeros((rows_c, 1), F32)
        code = jnp.clip(jnp.max(nw_ref[0].astype(jnp.int32)), 0, 2 * i * nd + 1)
        alone = jnp.minimum(code % 2, i * nd)
        whole = jnp.minimum(code // 2, i * nd - alone)
        carry = lax.fori_loop(i * nd - alone - whole, i * nd - alone,
                              lambda j, c: tile(j, c, _walk_work(chains, tk)), ((z1, z1),) * len(chains))
        if any(r0 >= tk for _, r0 in chains):
            carry = lax.cond(alone > 0, lambda c: tile(i * nd - 1, c, _walk_work(chains, tk, below=tk)),
                             lambda c: c, carry)
        for d in range(nd):
            carry = tile(i * nd + d, carry, _diag_work(chains, d, rows_c, tk))
        out_ref[0] = (jnp.where(lane < HEAD, dqa_ref[0], dqa_ref[1]) * scale).astype(BF16)
        own = pl.multiple_of(i * t, t)
        out_ref[1] = dka_ref[pl.ds(own, t), :].astype(BF16)
        out_ref[2] = dva_ref[pl.ds(own, t), :].astype(BF16)
        if scatter is not None:
            pl.when((pl.program_id(0) == nhp - 1) & (step == nq - 1))(finish)

    def rows(sec):
        return pl.BlockSpec((1, t, LANES), lambda hp, st: (sec, nq - 1 - st, hp))

    def whole(sec):
        return pl.BlockSpec((1, s, LANES), lambda hp, st: (sec, 0, hp))

    hosted = [] if scatter is None else list(scatter)
    sems = [] if scatter is None else [pltpu.SemaphoreType.DMA((ns, 3)), pltpu.SemaphoreType.DMA((ns, 3))]
    outs = _pcall(
        body, name=name, grid=(nhp, nq),
        in_specs=[rows(4), whole(5), whole(6), rows(7),
                  pl.BlockSpec((t, LANES), lambda hp, st: (nq - 1 - st, hp + nhp)),
                  pl.BlockSpec((t, LANES), lambda hp, st: (nq - 1 - st, hp)),
                  pl.BlockSpec((2, t, 1), lambda hp, st: (hp, nq - 1 - st, 0)),
                  pl.BlockSpec((1, 8, LANES), lambda hp, st: (hp * nq + nq - 1 - st, 0, 0)),
                  pl.BlockSpec((1, LANES), lambda hp, st: (0, hp + nhp)), ANY] + [ANY] * ns,
        out_specs=[pl.BlockSpec((4, t, LANES), lambda hp, st: (1, nq - 1 - st, hp)),
                   pl.BlockSpec((8, LANES), lambda hp, st: (0, hp))] + [ANY] * ns,
        out_shape=[jax.ShapeDtypeStruct(buf.shape, buf.dtype), jax.ShapeDtypeStruct((8, sw), F32)]
        + (_scatter_shapes(hosted) if hosted else []),
        input_output_aliases={9: 0},
        scratch_shapes=[pltpu.VMEM((s, LANES), F32), pltpu.VMEM((s, LANES), F32), pltpu.VMEM((2, t, LANES), F32)] + sems,
        compiler_params=_cp(2))(proj, proj, proj, proj, dy, ya, tl, walked, bg, buf, *hosted)
    return outs[0], outs[1], list(outs[2:])


def _grad_w_in(h, dproj, name):
    s, d = h.shape
    ns, _, sw = dproj.shape

    def body(h_ref, b_ref, o_ref, ht_ref):
        @pl.when(pl.program_id(0) == 0)
        def _():
            ht_ref[...] = h_ref[...].T

        o_ref[...] = jnp.dot(ht_ref[...], b_ref[0], preferred_element_type=F32)

    return _pcall(body, name=name, grid=(ns,),
                  in_specs=[pl.BlockSpec((s, d), lambda j: (0, 0)), pl.BlockSpec((1, s, sw), lambda j: (j, 0, 0))],
                  out_specs=pl.BlockSpec((d, sw), lambda j: (0, j)),
                  out_shape=jax.ShapeDtypeStruct((d, ns * sw), F32),
                  scratch_shapes=[pltpu.VMEM((d, s), BF16)], compiler_params=_cp(1))(h, dproj)


def _inproj_bwd(dproj, w, layer, x, g, dx1, name):
    ns, s, sw = dproj.shape
    d = x.shape[1]
    tm = _tile(s, 256)

    def body(dp_ref, w_ref, x_ref, g_ref, dx1_ref, dx_ref, dg_ref):
        @pl.when(pl.program_id(0) == 0)
        def _():
            dg_ref[...] = jnp.zeros_like(dg_ref)

        dh = _dot_nt(dp_ref[0], w_ref[0, :, 0:sw])
        for k in range(1, ns):
            dh = dh + _dot_nt(dp_ref[k], w_ref[0, :, k * sw:(k + 1) * sw])
        dxr, dgr = _rms_bwd_rows(dh, x_ref[...], g_ref[...])
        dx_ref[...] = dx1_ref[...] + dxr
        dg_ref[...] += _colsum8(dgr)

    row = lambda m: (m, 0)
    fix = lambda m: (0, 0)
    return _pcall(body, name=name, grid=(s // tm,),
                  in_specs=[pl.BlockSpec((ns, tm, sw), lambda m: (0, m, 0)),
                            pl.BlockSpec((1, d, ns * sw), lambda m: (layer, 0, 0)),
                            pl.BlockSpec((tm, d), row), pl.BlockSpec((1, d), fix), pl.BlockSpec((tm, d), row)],
                  out_specs=[pl.BlockSpec((tm, d), row), pl.BlockSpec((8, d), fix)],
                  out_shape=[jax.ShapeDtypeStruct((s, d), F32), jax.ShapeDtypeStruct((8, d), F32)],
                  compiler_params=_cp(1))(dproj, w, x, g, dx1)


def _adamw(w, g, m, v, name):
    r, c = w.shape
    tr = _tile(r, 256)
    c1 = 1.0 - ADAM_B1 ** ADAM_STEP
    c2 = 1.0 - ADAM_B2 ** ADAM_STEP

    def body(w_ref, g_ref, m_ref, v_ref, d_ref, mo_ref, vo_ref):
        gv = g_ref[...]
        mn = ADAM_B1 * m_ref[...] + (1.0 - ADAM_B1) * gv
        vn = ADAM_B2 * v_ref[...] + (1.0 - ADAM_B2) * (gv * gv)
        d_ref[...] = -ADAM_LR * ((mn / c1) / (jnp.sqrt(vn / c2) + ADAM_EPS) + ADAM_WD * w_ref[...])
        mo_ref[...] = mn
        vo_ref[...] = vn

    t = pl.BlockSpec((tr, c), lambda i: (i, 0))
    return _pcall(body, name=name, grid=(r // tr,), in_specs=[t] * 4, out_specs=[t] * 3,
                  out_shape=[jax.ShapeDtypeStruct((r, c), F32)] * 3, compiler_params=_cp(1))(w, g, m, v)


def _add_half(grad, other, core, a, name):
    hr, hc = other.shape
    tr = _tile(hr, 256)
    nb = hr // tr

    def body(c_ref, g_ref, o_ref, out_ref, outb_ref):
        v = g_ref[...] + o_ref[...]
        out_ref[...] = v
        outb_ref[...] = v.astype(BF16)

    t = pl.BlockSpec((tr, hc), lambda i, c: (i, 0))
    own = (lambda i, c: (c[0] * nb + i, 0)) if HALF_AXES[a] == 0 else (lambda i, c: (i, c[0]))
    grid_spec = pltpu.PrefetchScalarGridSpec(
        num_scalar_prefetch=1, grid=(nb,), in_specs=[pl.BlockSpec((tr, hc), own), t], out_specs=[t, t])
    return _pcall(body, name=name, grid_spec=grid_spec,
                  out_shape=[jax.ShapeDtypeStruct((hr, hc), F32), jax.ShapeDtypeStruct((hr, hc), BF16)],
                  compiler_params=_cp(1))(core.reshape(1).astype(jnp.int32), grad, other)


def _sum_half(wide, parts, chip, core, layer, a, stack, name):
    _, sr, sc = parts.shape
    tr = _tile(sr, 256)
    nbs = sr // tr

    def body(k_ref, f_ref, p_ref, *rest):
        rest[-1][0] = ((f_ref[...] + p_ref[0].astype(F32)) + p_ref[1].astype(F32)) + p_ref[2].astype(F32)

    f_map = (lambda i, k: (i, k[0])) if SHARD_AXES[a] == 1 else (lambda i, k: (k[0] * nbs + i, 0))
    if HALF_AXES[a] == 0:
        shape, o_map = (DEPTH, 2 * sr, sc), (lambda i, k: (layer, k[1] * nbs + i, 0))
    else:
        shape, o_map = (DEPTH, sr, 2 * sc), (lambda i, k: (layer, i, k[1]))
    in_specs = [pl.BlockSpec((tr, sc), f_map), pl.BlockSpec((3, tr, sc), lambda i, k: (0, i, 0))]
    args = [wide, parts]
    aliases = {}
    if stack is not None:
        in_specs.append(ANY)
        args.append(stack)
        aliases = {3: 0}
    grid_spec = pltpu.PrefetchScalarGridSpec(
        num_scalar_prefetch=1, grid=(nbs,), in_specs=in_specs, out_specs=pl.BlockSpec((1, tr, sc), o_map))
    return _pcall(body, name=name, grid_spec=grid_spec, out_shape=jax.ShapeDtypeStruct(shape, F32),
                  input_output_aliases=aliases,
                  compiler_params=_cp(1))(jnp.stack([chip, core]).astype(jnp.int32), *args)


def _sum_slots(slots, name):
    n = slots.shape[0]

    def body(s_ref, o_ref):
        acc = s_ref[0]
        for i in range(1, n):
            acc = acc + s_ref[i]
        o_ref[...] = acc

    return _pcall(body, name=name, out_shape=jax.ShapeDtypeStruct(slots.shape[1:], F32))(slots)


def _place():
    return lax.axis_index("x"), lax.axis_index("y"), lax.axis_index("c")


def _shard_view(ref, axis, chip, size):
    if axis == 0:
        return ref.at[pl.ds(chip * size, size), :]
    return ref.at[:, pl.ds(chip * size, size)]


SHARD_AXES = (1, 0, 0, 1)
HALF_AXES = tuple(1 - ax for ax in SHARD_AXES)


def _gather_phases(ins, outs, ssem, rsem, layer):
    n = len(ins)
    x, y, c = _place()
    me = 2 * x + y
    chips = [(1 - x, y), (x, 1 - y), (1 - x, 1 - y)]

    def piece(a, chip, half, of):
        block = _shard_view(of[a].at[layer], SHARD_AXES[a], chip, of[a].shape[1 + SHARD_AXES[a]] // 4)
        r = block.shape[0] // 2
        return block.at[pl.ds(half * r, r), :]

    def over_ici(a, j):
        cx, cy = chips[j]
        return pltpu.make_async_remote_copy(
            src_ref=piece(a, me, c, ins), dst_ref=piece(a, me, c, outs), send_sem=ssem.at[a, j],
            recv_sem=rsem.at[a, j], device_id=(cx, cy, c), device_id_type=MESH)

    def landed(a, j, half):
        cx, cy = chips[j]
        return piece(a, 2 * cx + cy, half, outs)

    def to_sibling(a, j):
        got = landed(a, j, c)
        return pltpu.make_async_remote_copy(
            src_ref=got, dst_ref=got, send_sem=ssem.at[a, 3 + j], recv_sem=rsem.at[a, 3 + j],
            device_id=(x, y, 1 - c), device_id_type=MESH)

    def wait_arrival(a, k, place):
        pltpu.make_async_remote_copy(src_ref=place, dst_ref=place, send_sem=ssem.at[a, k], recv_sem=rsem.at[a, k],
                                     device_id=(x, y, c), device_id_type=MESH).wait_recv()

    def start():
        for a in range(n):
            for j in range(3):
                over_ici(a, j).start()

    def pass_on():
        for a in range(n):
            for j in range(3):
                wait_arrival(a, j, landed(a, j, c))
                to_sibling(a, j).start()

    def finish():
        for a in range(n):
            for j in range(3):
                wait_arrival(a, 3 + j, landed(a, j, 1 - c))
        for a in range(n):
            for j in range(3):
                over_ici(a, j).wait_send()
                to_sibling(a, j).wait_send()

    return start, pass_on, finish


def _gather_weights(fulls, layer):
    n = len(fulls)

    def body(*refs):
        for phase in _gather_phases(refs[:n], refs[n:2 * n], refs[2 * n], refs[2 * n + 1], layer):
            phase()

    return _pcall(body, name=f"gather_weights_{layer}", in_specs=[ANY] * n, out_specs=[ANY] * n,
                  out_shape=[jax.ShapeDtypeStruct(f.shape, f.dtype) for f in fulls],
                  input_output_aliases={a: a for a in range(n)},
                  scratch_shapes=[pltpu.SemaphoreType.DMA((n, 6)), pltpu.SemaphoreType.DMA((n, 6))])(*fulls)


def _half_view(ref, a, half):
    n = ref.shape[HALF_AXES[a]] // 2
    if HALF_AXES[a] == 0:
        return ref.at[pl.ds(half * n, n), :]
    return ref.at[:, pl.ds(half * n, n)]


def _swap_halves(grads, name):
    n = len(grads)
    halves = []
    for a, g in enumerate(grads):
        sh = list(g.shape)
        sh[HALF_AXES[a]] //= 2
        halves.append(jax.ShapeDtypeStruct(tuple(sh), g.dtype))

    def body(*refs):
        srcs, outs = refs[:n], refs[n:2 * n]
        ssem, rsem = refs[2 * n:]
        x, y, c = _place()
        cps = [pltpu.make_async_remote_copy(src_ref=_half_view(srcs[a], a, 1 - c), dst_ref=outs[a],
                                            send_sem=ssem.at[a], recv_sem=rsem.at[a], device_id=(x, y, 1 - c),
                                            device_id_type=MESH)
               for a in range(n)]
        for cp in cps:
            cp.start()
        for cp in cps:
            cp.wait()

    return _pcall(body, name=name, in_specs=[ANY] * n, out_specs=[ANY] * n, out_shape=halves,
                  scratch_shapes=[pltpu.SemaphoreType.DMA((n,)), pltpu.SemaphoreType.DMA((n,))])(*grads)


def _scatter_shapes(sums):
    out = []
    for f, ax in zip(sums, SHARD_AXES):
        sh = list(f.shape)
        sh[ax] //= 4
        out.append(jax.ShapeDtypeStruct((3,) + tuple(sh), f.dtype))
    return out


def _scatter_phases(srcs, outs, ssem, rsem):
    n = len(srcs)
    x, y, c = _place()
    chips = [(1 - x, y), (x, 1 - y), (1 - x, 1 - y)]

    def copy(a, j):
        cx, cy = chips[j]
        src = _shard_view(srcs[a], SHARD_AXES[a], 2 * cx + cy, srcs[a].shape[SHARD_AXES[a]] // 4)
        return pltpu.make_async_remote_copy(src_ref=src, dst_ref=outs[a].at[j], send_sem=ssem.at[a, j],
                                            recv_sem=rsem.at[a, j], device_id=(cx, cy, c), device_id_type=MESH)

    def start():
        for a in range(n):
            for j in range(3):
                copy(a, j).start()

    def finish():
        for a in range(n):
            for j in range(3):
                copy(a, j).wait()

    return start, finish


def _scatter_shards(sums, name):
    n = len(sums)

    def body(*refs):
        for phase in _scatter_phases(refs[:n], refs[n:2 * n], refs[2 * n], refs[2 * n + 1]):
            phase()

    return _pcall(body, name=name, in_specs=[ANY] * n, out_specs=[ANY] * n, out_shape=_scatter_shapes(sums),
                  scratch_shapes=[pltpu.SemaphoreType.DMA((n, 3)), pltpu.SemaphoreType.DMA((n, 3))])(*sums)


def _pair_halves(stacks):
    n = len(stacks)

    def body(*refs):
        ins, outs = refs[:n], refs[n:2 * n]
        ssem, rsem = refs[2 * n:]
        x, y, c = _place()
        cps = [pltpu.make_async_remote_copy(
            src_ref=_half_view(ins[a].at[l], a, c), dst_ref=_half_view(outs[a].at[l], a, c), send_sem=ssem.at[a, l],
            recv_sem=rsem.at[a, l], device_id=(x, y, 1 - c), device_id_type=MESH)
            for a in range(n) for l in range(DEPTH)]
        for cp in cps:
            cp.start()
        for a in range(n):
            for l in range(DEPTH):
                got = _half_view(outs[a].at[l], a, 1 - c)
                pltpu.make_async_remote_copy(src_ref=got, dst_ref=got, send_sem=ssem.at[a, l], recv_sem=rsem.at[a, l],
                                             device_id=(x, y, 1 - c), device_id_type=MESH).wait_recv()
        for cp in cps:
            cp.wait_send()

    return _pcall(body, name="pair_halves", in_specs=[ANY] * n, out_specs=[ANY] * n,
                  out_shape=[jax.ShapeDtypeStruct(st.shape, st.dtype) for st in stacks],
                  input_output_aliases={a: a for a in range(n)},
                  scratch_shapes=[pltpu.SemaphoreType.DMA((n, DEPTH)), pltpu.SemaphoreType.DMA((n, DEPTH))])(*stacks)


class _GradReduce:
    def __init__(self, chip, core):
        self.chip, self.core = chip, core
        self.stacks = [None] * len(SHARD_AXES)

    def begin(self, layer, grads):
        got = _swap_halves(grads, f"swap_halves_{layer}")
        return [_add_half(g, o, self.core, a, f"add_half_{layer}_{a}") for a, (g, o) in enumerate(zip(grads, got))]

    def finish(self, layer, sums, partials):
        self.stacks = [_sum_half(wide, pr, self.chip, self.core, layer, a, self.stacks[a], f"sum_half_{layer}_{a}")
                       for a, ((wide, _), pr) in enumerate(zip(sums, partials))]

    def result(self):
        return _pair_halves(self.stacks)


def _exchange_small(pack, name):
    nd = 8

    def body(p_ref, o_ref, ssem, rsem):
        x, y, c = _place()
        me = 4 * x + 2 * y + c
        o_ref[me] = p_ref[...]
        cps = []
        for j in range(1, nd):
            px, py, pc = x ^ (j >> 2), y ^ ((j >> 1) & 1), c ^ (j & 1)
            cps.append(pltpu.make_async_remote_copy(
                src_ref=p_ref, dst_ref=o_ref.at[me], send_sem=ssem.at[j - 1], recv_sem=rsem.at[j - 1],
                device_id=(px, py, pc), device_id_type=MESH))
        for cp in cps:
            cp.start()
        for j in range(1, nd):
            peer = me ^ j
            got = o_ref.at[peer]
            pltpu.make_async_remote_copy(src_ref=got, dst_ref=got, send_sem=ssem.at[j - 1], recv_sem=rsem.at[j - 1],
                                         device_id=(x, y, c), device_id_type=MESH).wait_recv()
        for cp in cps:
            cp.wait_send()

    vm = pl.BlockSpec(memory_space=pltpu.VMEM)
    return _pcall(body, name=name, in_specs=[vm], out_specs=vm,
                  out_shape=jax.ShapeDtypeStruct((nd,) + pack.shape, pack.dtype),
                  scratch_shapes=[pltpu.SemaphoreType.DMA((nd - 1,)), pltpu.SemaphoreType.DMA((nd - 1,))])(pack)


def _row(v):
    return v.reshape(1, -1)


def _local_step(x, p, tgt, norm_g, conv_w, conv_b, branch_g, ple_norm_g, b_pg, final_g, w_in, w_out, w_pg, w_pe,
                gather_later_layers=False, reduce=None):
    saved = []
    xl = x
    for l in range(DEPTH):
        h, proj = _inproj(xl, _row(norm_g[l]), w_in, l, f"inproj_{l}")
        if gather_later_layers and l + 1 < DEPTH:
            ya, tl, walked, (w_in, w_out, w_pg, w_pe) = _attn_fwd(
                proj, f"attn_fwd_{l}", weights=(w_in, w_out, w_pg, w_pe), gather_layer=l + 1)
        else:
            ya, tl, walked, _ = _attn_fwd(proj, f"attn_fwd_{l}")
        y = _mix_fwd(proj, ya, conv_w[l], _row(conv_b[l]), _row(branch_g[l]), f"mix_fwd_{l}")
        x1, hn = _outproj(y, w_out, l, xl, _row(ple_norm_g[l]), f"outproj_{l}")
        x2, gate, e = _ple_fwd(hn, w_pg, _row(b_pg[l]), p, w_pe, l, x1, f"ple_fwd_{l}")
        saved.append((xl, h, proj, ya, tl, walked, y, x1, hn, gate, e))
        xl = x2

    sq, dx, d_final = _loss_head(xl, tgt, _row(final_g), "loss_head")

    big = [None] * DEPTH
    pending = None
    small = {k: [None] * DEPTH for k in ("norm_g", "conv_w", "conv_b", "branch_g", "ple_norm_g", "b_pg")}
    for l in reversed(range(DEPTH)):
        xl, h, proj, ya, tl, walked, y, x1, hn, gate, e = saved[l]
        du, de, dx1, dy, db_pg, d_ple = _ple_bwd(dx, gate, e, x1, w_pg, _row(ple_norm_g[l]), w_out, l, f"ple_bwd_{l}")
        g_pg = _mm_tn(hn, du, f"grad_w_pg_{l}")
        g_pe = _mm_tn(p, de, f"grad_w_pe_{l}", a_layer=l)
        g_out = _mm_tn(y, dx1, f"grad_w_out_{l}")
        dpc, d_cw, d_cb, d_bg_c = _convmix_bwd(dy, proj, conv_w[l], _row(conv_b[l]), _row(branch_g[l]), f"convmix_bwd_{l}")
        dproj, d_bg_a, partials = _attn_bwd(
            proj, dy, ya, tl, walked, _row(branch_g[l]), dpc, f"attn_bwd_{l}",
            scatter=None if pending is None else [narrow for _, narrow in pending[1]])
        if pending is not None:
            reduce.finish(pending[0], pending[1], partials)
            pending = None
        g_in = _grad_w_in(h, dproj, f"grad_w_in_{l}")
        dx, d_norm = _inproj_bwd(dproj, w_in, l, xl, _row(norm_g[l]), dx1, f"inproj_bwd_{l}")
        big[l] = (g_in, g_out, g_pg, g_pe)
        if reduce is not None:
            pending = (l, reduce.begin(l, big[l]))
        small["norm_g"][l] = jnp.sum(d_norm, axis=0)
        small["conv_w"][l] = jnp.sum(d_cw, axis=1)
        small["conv_b"][l] = jnp.sum(d_cb, axis=0)
        small["branch_g"][l] = jnp.concatenate([jnp.sum(d_bg_c, axis=0), jnp.sum(d_bg_a, axis=0)])
        small["ple_norm_g"][l] = jnp.sum(d_ple, axis=0)
        small["b_pg"][l] = jnp.sum(db_pg, axis=0)
    if pending is not None:
        sums = pending[1]
        reduce.finish(pending[0], sums, _scatter_shards([narrow for _, narrow in sums], f"scatter_shards_{pending[0]}"))
    small = {k: jnp.stack(v) for k, v in small.items()}
    small["final_g"] = jnp.sum(d_final, axis=0)
    return sq[0, 0], dx, big, small


SMALL_ORDER = ("norm_g", "conv_w", "conv_b", "branch_g", "ple_norm_g", "b_pg", "final_g")


def _pack(parts, width):
    flat = jnp.concatenate([v.reshape(-1) for v in parts])
    rows = -(-flat.shape[0] // width)
    rows = -(-rows // 8) * 8
    return jnp.pad(flat, (0, rows * width - flat.shape[0])).reshape(rows, width)


def _unpack(packed, like):
    flat = packed.reshape(-1)
    out, off = [], 0
    for v in like:
        out.append(flat[off:off + v.size].reshape(v.shape))
        off += v.size
    return out


def kernel(x, p, norm_g, w_in, conv_w, conv_b, branch_g, w_out, ple_norm_g, w_pg, b_pg, w_pe, final_g, loss_target, m_norm_g, m_w_in, m_conv_w, m_conv_b, m_branch_g, m_w_out, m_ple_norm_g, m_w_pg, m_b_pg, m_w_pe, m_final_g, v_norm_g, v_w_in, v_conv_w, v_conv_b, v_branch_g, v_w_out, v_ple_norm_g, v_w_pg, v_b_pg, v_w_pe, v_final_g):
    ix, iy, ic = _place()
    chip = 2 * ix + iy
    d = x.shape[-1]

    big_w = (w_in, w_out, w_pg, w_pe)
    own = [_cast_into_full(w, chip, ax, f"cast_{i}") for i, (w, ax) in enumerate(zip(big_w, SHARD_AXES))]
    full_in, full_out, full_pg, full_pe = _gather_weights(own, 0)
    cw_shard = conv_w.shape[-1]
    cw_slots = _exchange_small(_pack([conv_w], LANES), "exchange_conv_w")
    conv_full = jnp.concatenate([_unpack(cw_slots[2 * k], [conv_w])[0] for k in range(4)], axis=-1)

    reduce = _GradReduce(chip, ic)
    sq, dx, _, small_g = _local_step(
        x[0], p[:, 0], loss_target[0], norm_g, conv_full, conv_b, branch_g, ple_norm_g, b_pg, final_g,
        full_in, full_out, full_pg, full_pe, gather_later_layers=True, reduce=reduce)
    g_big = reduce.result()

    parts = [small_g[k] for k in SMALL_ORDER] + [sq.reshape(1)]
    slots = _exchange_small(_pack(parts, d), "exchange_small_grads")
    total = _unpack(_sum_slots(slots, "sum_small"), parts)
    g_small = dict(zip(SMALL_ORDER, total[:-1]))
    loss = 0.5 * total[-1][0] / d
    g_small["conv_w"] = lax.dynamic_slice_in_dim(g_small["conv_w"], chip * cw_shard, cw_shard, axis=2)

    grads = dict(g_small)
    grads.update(w_in=g_big[0], w_out=g_big[1], w_pg=g_big[2], w_pe=g_big[3])
    weights = dict(norm_g=norm_g, w_in=w_in, conv_w=conv_w, conv_b=conv_b, branch_g=branch_g, w_out=w_out,
                   ple_norm_g=ple_norm_g, w_pg=w_pg, b_pg=b_pg, w_pe=w_pe, final_g=final_g)
    ms = dict(norm_g=m_norm_g, w_in=m_w_in, conv_w=m_conv_w, conv_b=m_conv_b, branch_g=m_branch_g, w_out=m_w_out,
              ple_norm_g=m_ple_norm_g, w_pg=m_w_pg, b_pg=m_b_pg, w_pe=m_w_pe, final_g=m_final_g)
    vs = dict(norm_g=v_norm_g, w_in=v_w_in, conv_w=v_conv_w, conv_b=v_conv_b, branch_g=v_branch_g, w_out=v_w_out,
              ple_norm_g=v_ple_norm_g, w_pg=v_w_pg, b_pg=v_b_pg, w_pe=v_w_pe, final_g=v_final_g)
    names = ("norm_g", "w_in", "conv_w", "conv_b", "branch_g", "w_out", "ple_norm_g", "w_pg", "b_pg", "w_pe", "final_g")
    delta, new_m, new_v = {}, {}, {}
    for k in ("w_in", "w_out", "w_pg", "w_pe"):
        shp = weights[k].shape
        two = lambda a: a.reshape(-1, shp[-1])
        dl, mn, vn = _adamw(two(weights[k]), two(grads[k]), two(ms[k]), two(vs[k]), f"adamw_{k}")
        delta[k], new_m[k], new_v[k] = dl.reshape(shp), mn.reshape(shp), vn.reshape(shp)
        grads[k] = grads[k].reshape(shp)
    like = [weights[k] for k in SMALL_ORDER]
    packs = [_pack([src[k] for k in SMALL_ORDER], d) for src in (weights, grads, ms, vs)]
    outs = _adamw(*packs, "adamw_small")
    for res, o in zip((delta, new_m, new_v), outs):
        res.update(dict(zip(SMALL_ORDER, _unpack(o, like))))

    return (loss, dx[None], *[grads[k] for k in names], *[delta[k] for k in names],
            *[new_m[k] for k in names], *[new_v[k] for k in names])
```

```python
import math

import jax
import jax.numpy as jnp
from jax import lax
from jax.experimental import pallas as pl
from jax.experimental.pallas import tpu as pltpu

F32 = jnp.float32
BF16 = jnp.bfloat16
EPS = 1e-6
HEAD = 64
LANES = 128
ATT_TK = 256
ATT_TQ = 512
ATT_ROWS = 128
ALIVE_LOG = -105.0
DEPTH = 2
VMEM_LIMIT = 56 * 1024 * 1024
MESH = pl.DeviceIdType.MESH
ANY = pl.BlockSpec(memory_space=pl.ANY)

ADAM_LR = 0.001
ADAM_B1 = 0.9
ADAM_B2 = 0.999
ADAM_EPS = 1e-08
ADAM_WD = 0.01
ADAM_STEP = 10


def _pcall(body, **kw):
    return pl.pallas_call(body, **kw)


def _cp(n_axes):
    return pltpu.CompilerParams(dimension_semantics=("arbitrary",) * n_axes, vmem_limit_bytes=VMEM_LIMIT)


def _tile(n, pref):
    return pref if n % pref == 0 else n


def _split_dot(a, b, passes):
    out = None
    rem = a
    for _ in range(passes):
        hi = rem.astype(BF16)
        t = jnp.dot(hi, b, preferred_element_type=F32)
        out = t if out is None else out + t
        rem = rem - hi.astype(F32)
    return out


def _group_mat():
    r = lax.broadcasted_iota(jnp.int32, (LANES, LANES), 0) // HEAD
    c = lax.broadcasted_iota(jnp.int32, (LANES, LANES), 1) // HEAD
    return jnp.where(r == c, 1.0 / HEAD, 0.0).astype(BF16)


def _group_mean(v, gm):
    return _split_dot(v, gm, 2)


def _sigmoid(z):
    return 1.0 / (1.0 + jnp.exp(-z))


def _dot_nt(a, b):
    return lax.dot_general(a, b, (((1,), (1,)), ((), ())), preferred_element_type=F32)


def _dot_tn(a, b):
    return lax.dot_general(a, b, (((0,), (0,)), ((), ())), preferred_element_type=F32)


def _cast_into_full(w, chip, axis, name):
    _, r, c = w.shape
    tr = _tile(r, 256)
    nb = r // tr
    full = (DEPTH, 4 * r, c) if axis == 0 else (DEPTH, r, 4 * c)

    def body(k_ref, w_ref, o_ref):
        o_ref[...] = w_ref[...].astype(BF16)

    out_map = (lambda l, i, k: (l, k[0] * nb + i, 0)) if axis == 0 else (lambda l, i, k: (l, i, k[0]))
    grid_spec = pltpu.PrefetchScalarGridSpec(
        num_scalar_prefetch=1, grid=(DEPTH, nb),
        in_specs=[pl.BlockSpec((1, tr, c), lambda l, i, k: (l, i, 0))],
        out_specs=pl.BlockSpec((1, tr, c), out_map))
    return _pcall(body, name=name, grid_spec=grid_spec, out_shape=jax.ShapeDtypeStruct(full, BF16),
                  compiler_params=_cp(2))(chip.reshape(1).astype(jnp.int32), w)


def _rms_bwd_rows(dh, xv, g):
    r = lax.rsqrt(jnp.mean(xv * xv, axis=-1, keepdims=True) + EPS)
    xn = xv * r
    dxn = dh * g
    dx = r * (dxn - xn * jnp.mean(dxn * xn, axis=-1, keepdims=True))
    return dx, dh * xn


def _colsum8(v):
    tm, d = v.shape
    return jnp.sum(v.reshape(tm // 8, 8, d), axis=0)


def _inproj(x, g, w, layer, name, riders=()):
    s, d = x.shape
    n = w.shape[2]
    sw = d // 2
    ns = n // sw
    tm = _tile(s, 512)

    def body(*refs):
        own, riders_end = _riders_run(riders, refs, 3, 2, 0, pl.program_id(0), s // tm)
        x_ref, g_ref, w_ref, h_ref, o_ref = own
        xv = x_ref[...]
        r = lax.rsqrt(jnp.mean(xv * xv, axis=-1, keepdims=True) + EPS)
        h = (xv * r * g_ref[...]).astype(BF16)
        h_ref[...] = h
        for k in range(ns):
            o_ref[k] = jnp.dot(h, w_ref[0, :, k * sw:(k + 1) * sw], preferred_element_type=F32).astype(BF16)
        riders_end()

    r_ops, r_shapes, r_scratch, r_aliases = _riders_plumb(riders, 3, 2)
    outs = _pcall(body, name=name, grid=(s // tm,),
                  in_specs=[pl.BlockSpec((tm, d), lambda m: (m, 0)), pl.BlockSpec((1, d), lambda m: (0, 0)),
                            pl.BlockSpec((1, d, n), lambda m: (layer, 0, 0))] + [ANY] * len(r_ops),
                  out_specs=[pl.BlockSpec((tm, d), lambda m: (m, 0)), pl.BlockSpec((ns, tm, sw), lambda m: (0, m, 0))]
                  + [ANY] * len(r_shapes),
                  out_shape=[jax.ShapeDtypeStruct((s, d), BF16), jax.ShapeDtypeStruct((ns, s, sw), BF16)] + r_shapes,
                  input_output_aliases=r_aliases, scratch_shapes=r_scratch,
                  compiler_params=_cp(1))(x, g, w, *r_ops)
    return outs[0], outs[1], list(outs[2:])


def _softplus_parts(z):
    lm = jnp.minimum(-z, 0.0) - jnp.log(1.0 + jnp.exp(-jnp.abs(z)))
    return lm, lm + z


def _attn_tiles(s):
    tk = _tile(s, ATT_TK)
    tq = _tile(s, ATT_TQ)
    return tk, tq, tq // tk, min(ATT_ROWS, tq)


def _diag_work(chains, d, rows, tk):
    work = []
    for n, (_, r0) in enumerate(chains):
        if r0 + rows - 1 <= d * tk:
            continue
        kw = tk // 2 if (tk % 2 == 0 and r0 + rows <= d * tk + tk // 2) else tk
        if r0 >= d * tk + kw:
            mask = None
        else:
            row = lax.broadcasted_iota(jnp.int32, (rows, kw), 0)
            col = lax.broadcasted_iota(jnp.int32, (rows, kw), 1)
            mask = col + d * tk < row + r0
        work.append((n, kw, mask))
    return work


def _walk_work(chains, tk, below=None):
    return [(n, tk, None) for n, (_, r0) in enumerate(chains) if below is None or r0 < below]


def _any_alive(rsums):
    m = rsums[0]
    for r in rsums[1:]:
        m = jnp.maximum(m, r)
    return jnp.max((m > ALIVE_LOG).astype(jnp.int32))


def _attn_fwd(proj, name, riders=()):
    _, s, sw = proj.shape
    nhp = sw // LANES
    tk, tq, nd, rows = _attn_tiles(s)
    nq = s // tq
    scale = 1.0 / math.sqrt(HEAD)

    def body(*refs):
        i = pl.program_id(1)
        own, riders_end = _riders_run(riders, refs, 3, 3, 1, pl.program_id(0) * nq + i, nhp * nq)
        q_ref, k_ref, v_ref, o_ref, tl_ref, nw_ref, acc_ref = own
        tri = (lax.broadcasted_iota(jnp.int32, (tk, tk), 0) >
               lax.broadcasted_iota(jnp.int32, (tk, tk), 1)).astype(BF16)
        lane = lax.broadcasted_iota(jnp.int32, (tq, LANES), 1)
        q = q_ref[0] * jnp.asarray(scale, BF16)
        qms = [jnp.where((lane // HEAD) == h, q, jnp.zeros_like(q)) for h in range(2)]
        acc_ref[...] = jnp.zeros_like(acc_ref)
        chains = [(h, r0) for h in range(2) for r0 in range(0, tq, rows)]
        qparts = [qms[h][r0:r0 + rows] for h, r0 in chains]

        def tile(j, rsums, work):
            k0 = pl.multiple_of(j * tk, tk)
            kj = k_ref[0, pl.ds(k0, tk), :]
            vj = v_ref[0, pl.ds(k0, tk), :]
            zs = [_dot_nt(qparts[n], kj[:kw]) for n, kw, _ in work]
            lms, lss, css = [], [], []
            for z, (n, kw, mask) in zip(zs, work):
                lm, ls = _softplus_parts(z)
                if mask is not None:
                    lm = jnp.where(mask, lm, 0.0)
                lms.append(lm)
                lss.append(ls)
                css.append(_split_dot(lm, tri[:kw, :kw], 2))
            out = list(rsums)
            for lm, ls, cs, (n, kw, mask) in zip(lms, lss, css, work):
                h, r0 = chains[n]
                a = jnp.exp(ls + (rsums[n] + cs))
                if mask is not None:
                    a = jnp.where(mask, a, 0.0)
                acc_ref[h, r0:r0 + rows, :] += jnp.dot(a.astype(BF16), vj[:kw], preferred_element_type=F32)
                out[n] = rsums[n] + jnp.sum(lm, axis=1, keepdims=True)
            return tuple(out)

        rsums = (jnp.zeros((rows, 1), F32),) * len(chains)
        for d in reversed(range(nd)):
            rsums = tile(i * nd + d, rsums, _diag_work(chains, d, rows, tk))

        upper = [rs for rs, (_, r0) in zip(rsums, chains) if r0 >= tk]
        lower = [rs for rs, (_, r0) in zip(rsums, chains) if r0 < tk]
        if upper:
            alone = (i > 0) & (_any_alive(upper) == 0) & (_any_alive(lower) > 0)
            rsums = lax.cond(alone, lambda rs: tile(i * nd - 1, rs, _walk_work(chains, tk, below=tk)),
                             lambda rs: rs, rsums)
            alone = alone.astype(jnp.int32)
        else:
            alone = jnp.int32(0)

        def walk(c):
            jj, rs, _ = c
            rs = tile(i * nd - 1 - jj, rs, _walk_work(chains, tk))
            return jj + 1, rs, _any_alive(rs)

        last, rsums, _ = lax.while_loop(lambda c: (c[0] < i * nd) & (c[2] > 0), walk,
                                        (alone, rsums, _any_alive(rsums)))
        for n, (h, r0) in enumerate(chains):
            tl_ref[h, r0:r0 + rows, :] = rsums[n]
        nw_ref[0] = (jnp.zeros((8, LANES), jnp.int32) + (2 * (last - alone) + alone)).astype(F32)
        o_ref[...] = jnp.where(lane < HEAD, acc_ref[0], acc_ref[1]).astype(BF16)
        riders_end()

    r_ops, r_shapes, r_scratch, r_aliases = _riders_plumb(riders, 3, 3)
    outs = _pcall(
        body, name=name, grid=(nhp, nq),
        in_specs=[pl.BlockSpec((1, tq, LANES), lambda hp, i: (4, i, hp)),
                  pl.BlockSpec((1, s, LANES), lambda hp, i: (5, 0, hp)),
                  pl.BlockSpec((1, s, LANES), lambda hp, i: (6, 0, hp))] + [ANY] * len(r_ops),
        out_specs=[pl.BlockSpec((tq, LANES), lambda hp, i: (i, hp)),
                   pl.BlockSpec((2, tq, 1), lambda hp, i: (hp, i, 0)),
                   pl.BlockSpec((1, 8, LANES), lambda hp, i: (hp * nq + i, 0, 0))] + [ANY] * len(r_shapes),
        out_shape=[jax.ShapeDtypeStruct((s, sw), BF16), jax.ShapeDtypeStruct((2 * nhp, s, 1), F32),
                   jax.ShapeDtypeStruct((nhp * nq, 8, LANES), F32)] + r_shapes,
        input_output_aliases=r_aliases,
        scratch_shapes=[pltpu.VMEM((2, tq, LANES), F32)] + r_scratch,
        compiler_params=_cp(2))(proj, proj, proj, *r_ops)
    return outs[0], outs[1], outs[2], list(outs[3:])


def _conv_rows(cc_ref, ch_ref, w_ref, b_ref, r, tc):
    r0 = pl.multiple_of(r * tc, tc)
    u = cc_ref[0, pl.ds(r0, tc), :].astype(F32) * ch_ref[0, pl.ds(r0, tc), :].astype(F32)
    p0 = pl.multiple_of(jnp.maximum(r0 - 16, 0), 16)
    up = cc_ref[0, pl.ds(p0, 16), :].astype(F32) * ch_ref[0, pl.ds(p0, 16), :].astype(F32)
    up = up * (r > 0).astype(F32)
    prev1 = up[15:16, :]
    prev2 = up[14:15, :]
    rid = lax.broadcasted_iota(jnp.int32, u.shape, 0)
    s1 = jnp.where(rid == 0, prev1, pltpu.roll(u, 1, axis=0))
    s2 = jnp.where(rid == 0, prev2, jnp.where(rid == 1, prev1, pltpu.roll(u, 2, axis=0)))
    cv = b_ref[...] + s2 * w_ref[0:1, :] + s1 * w_ref[1:2, :] + u * w_ref[2:3, :]
    return r0, u, s1, s2, cv


def _mix_fwd(proj, ya, conv_w, conv_b, bg, name):
    _, s, sw = proj.shape
    nh = sw // LANES
    tc = _tile(s, 256)

    def body(cb_ref, cc_ref, ch_ref, cz_ref, ya_ref, az_ref, w_ref, b_ref, g_ref, y_ref):
        c = pl.program_id(0)
        gm = _group_mat()

        def finish(r0, yv, zg):
            n = yv * lax.rsqrt(_group_mean(yv * yv, gm) + EPS)
            y_ref[pl.ds(r0, tc), :] = (n * g_ref[...] * (zg * _sigmoid(zg))).astype(BF16)

        @pl.when(c < nh)
        def _():
            def step(r, carry):
                r0, _, _, _, cv = _conv_rows(cc_ref, ch_ref, w_ref, b_ref, r, tc)
                yc = cb_ref[0, pl.ds(r0, tc), :].astype(F32) * cv
                finish(r0, yc, cz_ref[0, pl.ds(r0, tc), :].astype(F32))
                return carry
            lax.fori_loop(0, s // tc, step, 0)

        @pl.when(c >= nh)
        def _():
            def step(r, carry):
                r0 = pl.multiple_of(r * tc, tc)
                finish(r0, ya_ref[pl.ds(r0, tc), :].astype(F32), az_ref[0, pl.ds(r0, tc), :].astype(F32))
                return carry
            lax.fori_loop(0, s // tc, step, 0)

    def sec(k):
        return pl.BlockSpec((1, s, LANES), lambda c: (k, 0, jnp.minimum(c, nh - 1)))

    return _pcall(
        body, name=name, grid=(2 * nh,),
        in_specs=[sec(0), sec(1), sec(2), sec(3),
                  pl.BlockSpec((s, LANES), lambda c: (0, jnp.maximum(c - nh, 0))),
                  pl.BlockSpec((1, s, LANES), lambda c: (7, 0, jnp.maximum(c - nh, 0))),
                  pl.BlockSpec((3, LANES), lambda c: (0, jnp.minimum(c, nh - 1))),
                  pl.BlockSpec((1, LANES), lambda c: (0, jnp.minimum(c, nh - 1))),
                  pl.BlockSpec((1, LANES), lambda c: (0, c))],
        out_specs=pl.BlockSpec((s, LANES), lambda c: (0, c)),
        out_shape=jax.ShapeDtypeStruct((s, 2 * sw), BF16), compiler_params=_cp(1),
    )(proj, proj, proj, proj, ya, proj, conv_w, conv_b, bg)


def _outproj(y, w, layer, x, g, name):
    s, d = x.shape
    tm = _tile(s, 256)

    def body(y_ref, w_ref, x_ref, g_ref, x1_ref, hn_ref):
        x1 = x_ref[...] + jnp.dot(y_ref[...], w_ref[0], preferred_element_type=F32)
        x1_ref[...] = x1
        r = lax.rsqrt(jnp.mean(x1 * x1, axis=-1, keepdims=True) + EPS)
        hn_ref[...] = (x1 * r * g_ref[...]).astype(BF16)

    row = lambda m: (m, 0)
    fix = lambda m: (0, 0)
    return _pcall(body, name=name, grid=(s // tm,),
                  in_specs=[pl.BlockSpec((tm, d), row), pl.BlockSpec((1, d, d), lambda m: (layer, 0, 0)),
                            pl.BlockSpec((tm, d), row), pl.BlockSpec((1, d), fix)],
                  out_specs=[pl.BlockSpec((tm, d), row), pl.BlockSpec((tm, d), row)],
                  out_shape=[jax.ShapeDtypeStruct((s, d), F32), jax.ShapeDtypeStruct((s, d), BF16)],
                  compiler_params=_cp(1))(y, w, x, g)


def _ple_fwd(hn, w_pg, b_pg, p, w_pe, layer, x1, name):
    s, d = x1.shape
    pd = p.shape[2]
    tm = _tile(s, 256)

    def body(hn_ref, wg_ref, b_ref, p_ref, we_ref, x1_ref, x2_ref, gate_ref, e_ref):
        gate = _sigmoid(jnp.dot(hn_ref[...], wg_ref[0], preferred_element_type=F32) + b_ref[...])
        e = jnp.dot(p_ref[0].astype(BF16), we_ref[0], preferred_element_type=F32)
        x2_ref[...] = x1_ref[...] + gate * e
        gate_ref[...] = gate.astype(BF16)
        e_ref[...] = e.astype(BF16)

    row = lambda m: (m, 0)
    fix = lambda m: (0, 0)
    return _pcall(body, name=name, grid=(s // tm,),
                  in_specs=[pl.BlockSpec((tm, d), row), pl.BlockSpec((1, d, d), lambda m: (layer, 0, 0)),
                            pl.BlockSpec((1, d), fix), pl.BlockSpec((1, tm, pd), lambda m: (layer, m, 0)),
                            pl.BlockSpec((1, pd, d), lambda m: (layer, 0, 0)), pl.BlockSpec((tm, d), row)],
                  out_specs=[pl.BlockSpec((tm, d), row)] * 3,
                  out_shape=[jax.ShapeDtypeStruct((s, d), F32), jax.ShapeDtypeStruct((s, d), BF16),
                             jax.ShapeDtypeStruct((s, d), BF16)],
                  compiler_params=_cp(1))(hn, w_pg, b_pg, p, w_pe, x1)


def _loss_head(x, tgt, g, name):
    s, d = x.shape
    tm = _tile(s, 256)

    def body(x_ref, t_ref, g_ref, l_ref, dx_ref, dg_ref):
        m = pl.program_id(0)

        @pl.when(m == 0)
        def _():
            l_ref[...] = jnp.zeros_like(l_ref)
            dg_ref[...] = jnp.zeros_like(dg_ref)

        xv = x_ref[...]
        gv = g_ref[...]
        r = lax.rsqrt(jnp.mean(xv * xv, axis=-1, keepdims=True) + EPS)
        xn = xv * r
        err = xn * gv - t_ref[...]
        l_ref[...] += jnp.sum(err * err)
        dy = err * (1.0 / d)
        dxn = dy * gv
        dx_ref[...] = r * (dxn - xn * jnp.mean(dxn * xn, axis=-1, keepdims=True))
        dg_ref[...] += _colsum8(dy * xn)

    row = lambda m: (m, 0)
    fix = lambda m: (0, 0)
    return _pcall(body, name=name, grid=(s // tm,),
                  in_specs=[pl.BlockSpec((tm, d), row), pl.BlockSpec((tm, d), row), pl.BlockSpec((1, d), fix)],
                  out_specs=[pl.BlockSpec((8, LANES), fix), pl.BlockSpec((tm, d), row), pl.BlockSpec((8, d), fix)],
                  out_shape=[jax.ShapeDtypeStruct((8, LANES), F32), jax.ShapeDtypeStruct((s, d), F32),
                             jax.ShapeDtypeStruct((8, d), F32)],
                  compiler_params=_cp(1))(x, tgt, g)


def _ple_bwd(dx2, gate, e, x1, w_pg, g_ple, w_out, layer, name, riders=()):
    s, d = dx2.shape
    tm = _tile(s, 256)

    def body(*refs):
        m = pl.program_id(0)
        own, riders_end = _riders_run(riders, refs, 7, 6, 0, m, s // tm)
        (dx2_ref, gate_ref, e_ref, x1_ref, wg_ref, g_ref, wo_ref,
         du_ref, de_ref, dx1_ref, dy_ref, db_ref, dg_ref) = own

        @pl.when(m == 0)
        def _():
            db_ref[...] = jnp.zeros_like(db_ref)
            dg_ref[...] = jnp.zeros_like(dg_ref)

        dx2v = dx2_ref[...]
        gate = gate_ref[...].astype(F32)
        du = dx2v * e_ref[...].astype(F32) * gate * (1.0 - gate)
        de_ref[...] = (dx2v * gate).astype(BF16)
        dub = du.astype(BF16)
        du_ref[...] = dub
        db_ref[...] += _colsum8(du)
        dhn = _dot_nt(dub, wg_ref[0])
        dxr, dgr = _rms_bwd_rows(dhn, x1_ref[...], g_ref[...])
        dx1 = dx2v + dxr
        dx1_ref[...] = dx1
        dg_ref[...] += _colsum8(dgr)
        dy_ref[...] = _dot_nt(dx1.astype(BF16), wo_ref[0]).astype(BF16)
        riders_end()

    row = lambda m: (m, 0)
    fix = lambda m: (0, 0)
    t = pl.BlockSpec((tm, d), row)
    r_ops, r_shapes, r_scratch, r_aliases = _riders_plumb(riders, 7, 6)
    outs = _pcall(body, name=name, grid=(s // tm,),
                  in_specs=[t, t, t, t, pl.BlockSpec((1, d, d), lambda m: (layer, 0, 0)), pl.BlockSpec((1, d), fix),
                            pl.BlockSpec((1, d, d), lambda m: (layer, 0, 0))] + [ANY] * len(r_ops),
                  out_specs=[t, t, t, t, pl.BlockSpec((8, d), fix), pl.BlockSpec((8, d), fix)] + [ANY] * len(r_shapes),
                  out_shape=[jax.ShapeDtypeStruct((s, d), BF16), jax.ShapeDtypeStruct((s, d), BF16),
                             jax.ShapeDtypeStruct((s, d), F32), jax.ShapeDtypeStruct((s, d), BF16),
                             jax.ShapeDtypeStruct((8, d), F32), jax.ShapeDtypeStruct((8, d), F32)] + r_shapes,
                  input_output_aliases=r_aliases, scratch_shapes=r_scratch,
                  compiler_params=_cp(1))(dx2, gate, e, x1, w_pg, g_ple, w_out, *r_ops)
    return tuple(outs[:6]) + (list(outs[6:]),)


def _mm_tn(a, b, name, a_layer=None):
    s, ka = a.shape[-2:]
    n = b.shape[1]
    tn = _tile(n, 1024)
    ns = n // tn
    tk = _tile(s, 512)
    nk = s // tk

    def body(a_ref, b_ref, o_ref, acc_ref):
        k = pl.program_id(1)

        @pl.when(k == 0)
        def _():
            acc_ref[...] = jnp.zeros_like(acc_ref)

        av = a_ref[...] if a_layer is None else a_ref[0]
        acc_ref[...] += _dot_tn(av.astype(BF16), b_ref[...].astype(BF16))

        @pl.when(k == nk - 1)
        def _():
            o_ref[...] = acc_ref[...]

    a_spec = (pl.BlockSpec((tk, ka), lambda j, k: (k, 0)) if a_layer is None
              else pl.BlockSpec((1, tk, ka), lambda j, k: (a_layer, k, 0)))
    return _pcall(body, name=name, grid=(ns, nk),
                  in_specs=[a_spec, pl.BlockSpec((tk, tn), lambda j, k: (k, j))],
                  out_specs=pl.BlockSpec((ka, tn), lambda j, k: (0, j)),
                  out_shape=jax.ShapeDtypeStruct((ka, n), F32),
                  scratch_shapes=[pltpu.VMEM((ka, tn), F32)], compiler_params=_cp(2))(a, b)


def _norm_gate_bwd(dy, yv, zg, g, gm):
    r = lax.rsqrt(_group_mean(yv * yv, gm) + EPS)
    n = yv * r
    sg = _sigmoid(zg)
    sil = zg * sg
    dzg = dy * n * g * (sg * (1.0 + zg * (1.0 - sg)))
    dn = dy * g * sil
    dyv = r * (dn - n * _group_mean(dn * n, gm))
    return dyv, dzg, dy * n * sil


def _convmix_bwd(dy, proj, conv_w, conv_b, bg, name, riders=()):
    _, s, sw = proj.shape
    nh = sw // LANES
    tc = _tile(s, 256)
    nr = s // tc

    def body(*refs):
        own, riders_end = _riders_run(riders, refs, 8, 4, 1, pl.program_id(0), nh)
        (dy_ref, cb_ref, cc_ref, ch_ref, cz_ref, w_ref, b_ref, g_ref,
         dp_ref, dw_ref, db_ref, dg_ref, dcv_ref) = own
        gm = _group_mat()
        dcv_ref[pl.ds(s, 8), :] = jnp.zeros((8, LANES), F32)

        def pass1(r, carry):
            dw0, dw1, dw2, db, dg = carry
            r0, u, s1, s2, cv = _conv_rows(cc_ref, ch_ref, w_ref, b_ref, r, tc)
            cb = cb_ref[0, pl.ds(r0, tc), :].astype(F32)
            dyc, dcz, dgr = _norm_gate_bwd(dy_ref[pl.ds(r0, tc), :].astype(F32), cb * cv,
                                           cz_ref[0, pl.ds(r0, tc), :].astype(F32), g_ref[...], gm)
            dp_ref[0, pl.ds(r0, tc), :] = (dyc * cv).astype(BF16)
            dp_ref[3, pl.ds(r0, tc), :] = dcz.astype(BF16)
            dcv = dyc * cb
            dcv_ref[pl.ds(r0, tc), :] = dcv
            return (dw0 + _colsum8(dcv * s2), dw1 + _colsum8(dcv * s1), dw2 + _colsum8(dcv * u),
                    db + _colsum8(dcv), dg + _colsum8(dgr))

        z8 = jnp.zeros((8, LANES), F32)
        dw0, dw1, dw2, db, dg = lax.fori_loop(0, nr, pass1, (z8, z8, z8, z8, z8))
        dw_ref[0] = dw0
        dw_ref[1] = dw1
        dw_ref[2] = dw2
        db_ref[...] = db
        dg_ref[...] = dg

        def pass2(r, carry):
            r0 = pl.multiple_of(r * tc, tc)
            dcv = dcv_ref[pl.ds(r0, tc), :]
            nxt = dcv_ref[pl.ds(pl.multiple_of(r0 + tc, 8), 8), :]
            rid = lax.broadcasted_iota(jnp.int32, dcv.shape, 0)
            n1 = jnp.where(rid == tc - 1, nxt[0:1, :], pltpu.roll(dcv, tc - 1, axis=0))
            n2 = jnp.where(rid == tc - 1, nxt[1:2, :],
                           jnp.where(rid == tc - 2, nxt[0:1, :], pltpu.roll(dcv, tc - 2, axis=0)))
            du = dcv * w_ref[2:3, :] + n1 * w_ref[1:2, :] + n2 * w_ref[0:1, :]
            dp_ref[1, pl.ds(r0, tc), :] = (du * ch_ref[0, pl.ds(r0, tc), :].astype(F32)).astype(BF16)
            dp_ref[2, pl.ds(r0, tc), :] = (du * cc_ref[0, pl.ds(r0, tc), :].astype(F32)).astype(BF16)
            return carry

        lax.fori_loop(0, nr, pass2, 0)
        riders_end()

    def sec(k):
        return pl.BlockSpec((1, s, LANES), lambda c: (k, 0, c))

    col = lambda c: (0, c)
    r_ops, r_shapes, r_scratch, r_aliases = _riders_plumb(riders, 8, 4)
    outs = _pcall(
        body, name=name, grid=(nh,),
        in_specs=[pl.BlockSpec((s, LANES), col), sec(0), sec(1), sec(2), sec(3),
                  pl.BlockSpec((3, LANES), col), pl.BlockSpec((1, LANES), col), pl.BlockSpec((1, LANES), col)]
        + [ANY] * len(r_ops),
        out_specs=[pl.BlockSpec((4, s, LANES), lambda c: (0, 0, c)), pl.BlockSpec((3, 8, LANES), lambda c: (0, 0, c)),
                   pl.BlockSpec((8, LANES), col), pl.BlockSpec((8, LANES), col)] + [ANY] * len(r_shapes),
        out_shape=[jax.ShapeDtypeStruct((8, s, sw), BF16), jax.ShapeDtypeStruct((3, 8, sw), F32),
                   jax.ShapeDtypeStruct((8, sw), F32), jax.ShapeDtypeStruct((8, sw), F32)] + r_shapes,
        input_output_aliases=r_aliases,
        scratch_shapes=[pltpu.VMEM((s + 8, LANES), F32)] + r_scratch, compiler_params=_cp(1),
    )(dy, proj, proj, proj, proj, conv_w, conv_b, bg, *r_ops)
    return tuple(outs[:4]) + (list(outs[4:]),)


def _attn_bwd(proj, dy, ya, tl, walked, bg, buf, name, riders=()):
    _, s, sw = proj.shape
    nhp = sw // LANES
    tk, t, nd, rows_c = _attn_tiles(s)
    nq = s // t
    scale = 1.0 / math.sqrt(HEAD)

    def body(*refs):
        step = pl.program_id(1)
        i = nq - 1 - step
        own, riders_end = _riders_run(riders, refs, 10, 2, 3, pl.program_id(0) * nq + step, nhp * nq)
        (q_ref, k_ref, v_ref, az_ref, dy_ref, ya_ref, tl_ref, nw_ref, g_ref, buf_ref, out_ref, dg_ref,
         dka_ref, dva_ref, dqa_ref) = own

        @pl.when(step == 0)
        def _():
            dka_ref[...] = jnp.zeros_like(dka_ref)
            dva_ref[...] = jnp.zeros_like(dva_ref)
            dg_ref[...] = jnp.zeros_like(dg_ref)

        dyv, dzg, dgr = _norm_gate_bwd(dy_ref[...].astype(F32), ya_ref[...].astype(F32), az_ref[0].astype(F32),
                                       g_ref[...], _group_mat())
        out_ref[3] = dzg.astype(BF16)
        dg_ref[...] += _colsum8(dgr)

        tri = (lax.broadcasted_iota(jnp.int32, (tk, tk), 0) <=
               lax.broadcasted_iota(jnp.int32, (tk, tk), 1)).astype(BF16)
        lane = lax.broadcasted_iota(jnp.int32, (t, LANES), 1)
        q = q_ref[0] * jnp.asarray(scale, BF16)
        do = dyv.astype(BF16)
        qms = [jnp.where((lane // HEAD) == h, q, jnp.zeros_like(q)) for h in range(2)]
        doms = [jnp.where((lane // HEAD) == h, do, jnp.zeros_like(do)) for h in range(2)]
        dqa_ref[...] = jnp.zeros_like(dqa_ref)
        chains = [(h, r0) for h in range(2) for r0 in range(0, t, rows_c)]
        qparts = [qms[h][r0:r0 + rows_c] for h, r0 in chains]
        doparts = [doms[h][r0:r0 + rows_c] for h, r0 in chains]
        tots = [tl_ref[h, r0:r0 + rows_c, :] for h, r0 in chains]

        def tile(j, carry, work):
            k0 = pl.multiple_of(j * tk, tk)
            kj = k_ref[0, pl.ds(k0, tk), :]
            vj = v_ref[0, pl.ds(k0, tk), :]
            zs = [_dot_nt(qparts[n], kj[:kw]) for n, kw, _ in work]
            das = [_dot_nt(doparts[n], vj[:kw]) for n, kw, _ in work]
            lms, lss, cls = [], [], []
            for z, (n, kw, mask) in zip(zs, work):
                lm, ls = _softplus_parts(z)
                if mask is not None:
                    lm = jnp.where(mask, lm, 0.0)
                lms.append(lm)
                lss.append(ls)
                cls.append(_split_dot(lm, tri[:kw, :kw], 2))
            abs_, gs, cgs = [], [], []
            for ls, cl, da, (n, kw, mask) in zip(lss, cls, das, work):
                a = jnp.exp(ls + (tots[n] - carry[n][0] - cl))
                if mask is not None:
                    a = jnp.where(mask, a, 0.0)
                g = a * da
                gs.append(g)
                abs_.append(a.astype(BF16))
                cgs.append(_split_dot(g, tri[:kw, :kw], 1))
            out = list(carry)
            dks, dvs = {}, {}
            for lm, ls, a, g, cg, (n, kw, mask) in zip(lms, lss, abs_, gs, cgs, work):
                h, r0 = chains[n]
                psum, gsum = carry[n]
                dz = g - jnp.exp(ls) * (gsum + cg)
                if mask is not None:
                    dz = jnp.where(mask, dz, 0.0)
                dz = dz.astype(BF16)
                dqa_ref[h, r0:r0 + rows_c, :] += jnp.dot(dz, kj[:kw], preferred_element_type=F32)
                dkh = _dot_tn(dz, qparts[n])
                dvh = _dot_tn(a, doparts[n])
                dks[kw] = dkh if kw not in dks else dks[kw] + dkh
                dvs[kw] = dvh if kw not in dvs else dvs[kw] + dvh
                out[n] = (psum + jnp.sum(lm, axis=1, keepdims=True), gsum + jnp.sum(g, axis=1, keepdims=True))
            for kw in dks:
                dka_ref[pl.ds(k0, kw), :] += dks[kw]
                dva_ref[pl.ds(k0, kw), :] += dvs[kw]
            return tuple(out)

        z1 = jnp.zeros((rows_c, 1), F32)
        code = jnp.clip(jnp.max(nw_ref[0].astype(jnp.int32)), 0, 2 * i * nd + 1)
        alone = jnp.minimum(code % 2, i * nd)
        whole = jnp.minimum(code // 2, i * nd - alone)
        carry = lax.fori_loop(i * nd - alone - whole, i * nd - alone,
                              lambda j, c: tile(j, c, _walk_work(chains, tk)), ((z1, z1),) * len(chains))
        if any(r0 >= tk for _, r0 in chains):
            carry = lax.cond(alone > 0, lambda c: tile(i * nd - 1, c, _walk_work(chains, tk, below=tk)),
                             lambda c: c, carry)
        for d in range(nd):
            carry = tile(i * nd + d, carry, _diag_work(chains, d, rows_c, tk))
        out_ref[0] = (jnp.where(lane < HEAD, dqa_ref[0], dqa_ref[1]) * scale).astype(BF16)
        own = pl.multiple_of(i * t, t)
        out_ref[1] = dka_ref[pl.ds(own, t), :].astype(BF16)
        out_ref[2] = dva_ref[pl.ds(own, t), :].astype(BF16)
        riders_end()

    def rows(sec):
        return pl.BlockSpec((1, t, LANES), lambda hp, st: (sec, nq - 1 - st, hp))

    def whole(sec):
        return pl.BlockSpec((1, s, LANES), lambda hp, st: (sec, 0, hp))

    r_ops, r_shapes, r_scratch, r_aliases = _riders_plumb(riders, 10, 2)
    outs = _pcall(
        body, name=name, grid=(nhp, nq),
        in_specs=[rows(4), whole(5), whole(6), rows(7),
                  pl.BlockSpec((t, LANES), lambda hp, st: (nq - 1 - st, hp + nhp)),
                  pl.BlockSpec((t, LANES), lambda hp, st: (nq - 1 - st, hp)),
                  pl.BlockSpec((2, t, 1), lambda hp, st: (hp, nq - 1 - st, 0)),
                  pl.BlockSpec((1, 8, LANES), lambda hp, st: (hp * nq + nq - 1 - st, 0, 0)),
                  pl.BlockSpec((1, LANES), lambda hp, st: (0, hp + nhp)), ANY] + [ANY] * len(r_ops),
        out_specs=[pl.BlockSpec((4, t, LANES), lambda hp, st: (1, nq - 1 - st, hp)),
                   pl.BlockSpec((8, LANES), lambda hp, st: (0, hp))] + [ANY] * len(r_shapes),
        out_shape=[jax.ShapeDtypeStruct(buf.shape, buf.dtype), jax.ShapeDtypeStruct((8, sw), F32)] + r_shapes,
        input_output_aliases={9: 0, **r_aliases},
        scratch_shapes=[pltpu.VMEM((s, LANES), F32), pltpu.VMEM((s, LANES), F32), pltpu.VMEM((2, t, LANES), F32)]
        + r_scratch,
        compiler_params=_cp(2))(proj, proj, proj, proj, dy, ya, tl, walked, bg, buf, *r_ops)
    return outs[0], outs[1], list(outs[2:])


def _grad_w_in(h, dproj, name):
    s, d = h.shape
    ns, _, sw = dproj.shape

    def body(h_ref, b_ref, o_ref, ht_ref):
        @pl.when(pl.program_id(0) == 0)
        def _():
            ht_ref[...] = h_ref[...].T

        o_ref[...] = jnp.dot(ht_ref[...], b_ref[0], preferred_element_type=F32)

    return _pcall(body, name=name, grid=(ns,),
                  in_specs=[pl.BlockSpec((s, d), lambda j: (0, 0)), pl.BlockSpec((1, s, sw), lambda j: (j, 0, 0))],
                  out_specs=pl.BlockSpec((d, sw), lambda j: (0, j)),
                  out_shape=jax.ShapeDtypeStruct((d, ns * sw), F32),
                  scratch_shapes=[pltpu.VMEM((d, s), BF16)], compiler_params=_cp(1))(h, dproj)


def _inproj_bwd(dproj, w, layer, x, g, dx1, name):
    ns, s, sw = dproj.shape
    d = x.shape[1]
    tm = _tile(s, 256)

    def body(dp_ref, w_ref, x_ref, g_ref, dx1_ref, dx_ref, dg_ref):
        @pl.when(pl.program_id(0) == 0)
        def _():
            dg_ref[...] = jnp.zeros_like(dg_ref)

        dh = _dot_nt(dp_ref[0], w_ref[0, :, 0:sw])
        for k in range(1, ns):
            dh = dh + _dot_nt(dp_ref[k], w_ref[0, :, k * sw:(k + 1) * sw])
        dxr, dgr = _rms_bwd_rows(dh, x_ref[...], g_ref[...])
        dx_ref[...] = dx1_ref[...] + dxr
        dg_ref[...] += _colsum8(dgr)

    row = lambda m: (m, 0)
    fix = lambda m: (0, 0)
    return _pcall(body, name=name, grid=(s // tm,),
                  in_specs=[pl.BlockSpec((ns, tm, sw), lambda m: (0, m, 0)),
                            pl.BlockSpec((1, d, ns * sw), lambda m: (layer, 0, 0)),
                            pl.BlockSpec((tm, d), row), pl.BlockSpec((1, d), fix), pl.BlockSpec((tm, d), row)],
                  out_specs=[pl.BlockSpec((tm, d), row), pl.BlockSpec((8, d), fix)],
                  out_shape=[jax.ShapeDtypeStruct((s, d), F32), jax.ShapeDtypeStruct((8, d), F32)],
                  compiler_params=_cp(1))(dproj, w, x, g, dx1)


def _adamw(w, g, m, v, name):
    r, c = w.shape
    tr = _tile(r, 256)
    c1 = 1.0 - ADAM_B1 ** ADAM_STEP
    c2 = 1.0 - ADAM_B2 ** ADAM_STEP

    def body(w_ref, g_ref, m_ref, v_ref, d_ref, mo_ref, vo_ref):
        gv = g_ref[...]
        mn = ADAM_B1 * m_ref[...] + (1.0 - ADAM_B1) * gv
        vn = ADAM_B2 * v_ref[...] + (1.0 - ADAM_B2) * (gv * gv)
        d_ref[...] = -ADAM_LR * ((mn / c1) / (jnp.sqrt(vn / c2) + ADAM_EPS) + ADAM_WD * w_ref[...])
        mo_ref[...] = mn
        vo_ref[...] = vn

    t = pl.BlockSpec((tr, c), lambda i: (i, 0))
    return _pcall(body, name=name, grid=(r // tr,), in_specs=[t] * 4, out_specs=[t] * 3,
                  out_shape=[jax.ShapeDtypeStruct((r, c), F32)] * 3, compiler_params=_cp(1))(w, g, m, v)


def _add_half(grad, other, core, a, name):
    hr, hc = other.shape
    tr = _tile(hr, 256)
    nb = hr // tr

    def body(c_ref, g_ref, o_ref, out_ref, outb_ref):
        v = g_ref[...] + o_ref[...]
        out_ref[...] = v
        outb_ref[...] = v.astype(BF16)

    t = pl.BlockSpec((tr, hc), lambda i, c: (i, 0))
    own = (lambda i, c: (c[0] * nb + i, 0)) if HALF_AXES[a] == 0 else (lambda i, c: (i, c[0]))
    grid_spec = pltpu.PrefetchScalarGridSpec(
        num_scalar_prefetch=1, grid=(nb,), in_specs=[pl.BlockSpec((tr, hc), own), t], out_specs=[t, t])
    return _pcall(body, name=name, grid_spec=grid_spec,
                  out_shape=[jax.ShapeDtypeStruct((hr, hc), F32), jax.ShapeDtypeStruct((hr, hc), BF16)],
                  compiler_params=_cp(1))(core.reshape(1).astype(jnp.int32), grad, other)


def _sum_half(wide, parts, chip, core, layer, a, stack, name):
    _, sr, sc = parts.shape
    tr = _tile(sr, 256)
    nbs = sr // tr

    def body(k_ref, f_ref, p_ref, *rest):
        rest[-1][0] = ((f_ref[...] + p_ref[0].astype(F32)) + p_ref[1].astype(F32)) + p_ref[2].astype(F32)

    f_map = (lambda i, k: (i, k[0])) if SHARD_AXES[a] == 1 else (lambda i, k: (k[0] * nbs + i, 0))
    if HALF_AXES[a] == 0:
        shape, o_map = (DEPTH, 2 * sr, sc), (lambda i, k: (layer, k[1] * nbs + i, 0))
    else:
        shape, o_map = (DEPTH, sr, 2 * sc), (lambda i, k: (layer, i, k[1]))
    in_specs = [pl.BlockSpec((tr, sc), f_map), pl.BlockSpec((3, tr, sc), lambda i, k: (0, i, 0))]
    args = [wide, parts]
    aliases = {}
    if stack is not None:
        in_specs.append(ANY)
        args.append(stack)
        aliases = {3: 0}
    grid_spec = pltpu.PrefetchScalarGridSpec(
        num_scalar_prefetch=1, grid=(nbs,), in_specs=in_specs, out_specs=pl.BlockSpec((1, tr, sc), o_map))
    return _pcall(body, name=name, grid_spec=grid_spec, out_shape=jax.ShapeDtypeStruct(shape, F32),
                  input_output_aliases=aliases,
                  compiler_params=_cp(1))(jnp.stack([chip, core]).astype(jnp.int32), *args)


def _sum_slots(slots, name):
    n = slots.shape[0]

    def body(s_ref, o_ref):
        acc = s_ref[0]
        for i in range(1, n):
            acc = acc + s_ref[i]
        o_ref[...] = acc

    return _pcall(body, name=name, out_shape=jax.ShapeDtypeStruct(slots.shape[1:], F32))(slots)


def _place():
    return lax.axis_index("x"), lax.axis_index("y"), lax.axis_index("c")


def _shard_view(ref, axis, chip, size):
    if axis == 0:
        return ref.at[pl.ds(chip * size, size), :]
    return ref.at[:, pl.ds(chip * size, size)]


SHARD_AXES = (1, 0, 0, 1)
HALF_AXES = tuple(1 - ax for ax in SHARD_AXES)


class _Rider:
    def __init__(self, operands, out_shape, sems, phases, aliased=False):
        self.operands, self.out_shape, self.sems = list(operands), list(out_shape), list(sems)
        self.phases, self.aliased = phases, aliased


def _riders_plumb(riders, n_in, n_out):
    ops, out_shape, scratch, aliases = [], [], [], {}
    for r in riders:
        if r.aliased:
            for k in range(len(r.operands)):
                aliases[n_in + len(ops) + k] = n_out + len(out_shape) + k
        ops += r.operands
        out_shape += r.out_shape
        scratch += r.sems
    return ops, out_shape, scratch, aliases


def _riders_run(riders, refs, n_in, n_out, n_scr, step, nsteps):
    n_rin = sum(len(r.operands) for r in riders)
    n_rout = sum(len(r.out_shape) for r in riders)
    rin = refs[n_in:n_in + n_rin]
    o0 = n_in + n_rin
    rout = refs[o0 + n_out:o0 + n_out + n_rout]
    s0 = o0 + n_out + n_rout
    rsem = refs[s0 + n_scr:]
    own = list(refs[:n_in]) + list(refs[o0:o0 + n_out]) + list(refs[s0:s0 + n_scr])
    lasts = []
    for r in riders:
        ph = r.phases(rin[:len(r.operands)], rout[:len(r.out_shape)], rsem[:len(r.sems)])
        rin, rout, rsem = rin[len(r.operands):], rout[len(r.out_shape):], rsem[len(r.sems):]
        pl.when(step == 0)(ph[0])
        for mid in ph[1:-1]:
            pl.when(step == (3 * nsteps) // 4)(mid)
        lasts.append(ph[-1])

    def finish():
        for last in lasts:
            pl.when(step == nsteps - 1)(last)

    return own, finish


def _gather_phases(ins, outs, ssem, rsem, layer, which):
    n = len(ins)
    x, y, c = _place()
    me = 2 * x + y
    chips = [(1 - x, y), (x, 1 - y), (1 - x, 1 - y)]

    def piece(a, chip, half, of):
        ax = SHARD_AXES[which[a]]
        block = _shard_view(of[a].at[layer], ax, chip, of[a].shape[1 + ax] // 4)
        r = block.shape[0] // 2
        return block.at[pl.ds(half * r, r), :]

    def over_ici(a, j):
        cx, cy = chips[j]
        return pltpu.make_async_remote_copy(
            src_ref=piece(a, me, c, ins), dst_ref=piece(a, me, c, outs), send_sem=ssem.at[a, j],
            recv_sem=rsem.at[a, j], device_id=(cx, cy, c), device_id_type=MESH)

    def landed(a, j, half):
        cx, cy = chips[j]
        return piece(a, 2 * cx + cy, half, outs)

    def to_sibling(a, j):
        got = landed(a, j, c)
        return pltpu.make_async_remote_copy(
            src_ref=got, dst_ref=got, send_sem=ssem.at[a, 3 + j], recv_sem=rsem.at[a, 3 + j],
            device_id=(x, y, 1 - c), device_id_type=MESH)

    def wait_arrival(a, k, place):
        pltpu.make_async_remote_copy(src_ref=place, dst_ref=place, send_sem=ssem.at[a, k], recv_sem=rsem.at[a, k],
                                     device_id=(x, y, c), device_id_type=MESH).wait_recv()

    def start():
        for a in range(n):
            for j in range(3):
                over_ici(a, j).start()

    def pass_on():
        for a in range(n):
            for j in range(3):
                wait_arrival(a, j, landed(a, j, c))
                to_sibling(a, j).start()

    def finish():
        for a in range(n):
            for j in range(3):
                wait_arrival(a, 3 + j, landed(a, j, 1 - c))
        for a in range(n):
            for j in range(3):
                over_ici(a, j).wait_send()
                to_sibling(a, j).wait_send()

    return start, pass_on, finish


def _gather_rider(fulls, layer, which):
    n = len(fulls)
    return _Rider(fulls, [jax.ShapeDtypeStruct(f.shape, f.dtype) for f in fulls],
                  [pltpu.SemaphoreType.DMA((n, 6)), pltpu.SemaphoreType.DMA((n, 6))],
                  lambda ins, outs, sems: _gather_phases(ins, outs, sems[0], sems[1], layer, which), aliased=True)


def _ride_alone(rider, name):
    n = len(rider.operands)

    def body(*refs):
        for phase in rider.phases(refs[:n], refs[n:n + len(rider.out_shape)], refs[n + len(rider.out_shape):]):
            phase()

    return _pcall(body, name=name, in_specs=[ANY] * n, out_specs=[ANY] * len(rider.out_shape),
                  out_shape=rider.out_shape, scratch_shapes=rider.sems,
                  input_output_aliases={a: a for a in range(n)} if rider.aliased else {})(*rider.operands)


def _half_view(ref, a, half):
    n = ref.shape[HALF_AXES[a]] // 2
    if HALF_AXES[a] == 0:
        return ref.at[pl.ds(half * n, n), :]
    return ref.at[:, pl.ds(half * n, n)]


def _swap_rider(grads, which):
    n = len(grads)
    halves = []
    for g, w in zip(grads, which):
        sh = list(g.shape)
        sh[HALF_AXES[w]] //= 2
        halves.append(jax.ShapeDtypeStruct(tuple(sh), g.dtype))

    def phases(srcs, outs, sems):
        x, y, c = _place()

        def copy(a):
            return pltpu.make_async_remote_copy(
                src_ref=_half_view(srcs[a], which[a], 1 - c), dst_ref=outs[a], send_sem=sems[0].at[a],
                recv_sem=sems[1].at[a], device_id=(x, y, 1 - c), device_id_type=MESH)

        def start():
            for a in range(n):
                copy(a).start()

        def finish():
            for a in range(n):
                copy(a).wait()

        return start, finish

    return _Rider(grads, halves, [pltpu.SemaphoreType.DMA((n,)), pltpu.SemaphoreType.DMA((n,))], phases)


def _scatter_rider(sums, which):
    n = len(sums)
    shapes = []
    for f, w in zip(sums, which):
        sh = list(f.shape)
        sh[SHARD_AXES[w]] //= 4
        shapes.append(jax.ShapeDtypeStruct((3,) + tuple(sh), f.dtype))

    def phases(srcs, outs, sems):
        x, y, c = _place()
        chips = [(1 - x, y), (x, 1 - y), (1 - x, 1 - y)]

        def copy(a, j):
            cx, cy = chips[j]
            ax = SHARD_AXES[which[a]]
            src = _shard_view(srcs[a], ax, 2 * cx + cy, srcs[a].shape[ax] // 4)
            return pltpu.make_async_remote_copy(src_ref=src, dst_ref=outs[a].at[j], send_sem=sems[0].at[a, j],
                                                recv_sem=sems[1].at[a, j], device_id=(cx, cy, c), device_id_type=MESH)

        def start():
            for a in range(n):
                for j in range(3):
                    copy(a, j).start()

        def finish():
            for a in range(n):
                for j in range(3):
                    copy(a, j).wait()

        return start, finish

    return _Rider(sums, shapes, [pltpu.SemaphoreType.DMA((n, 3)), pltpu.SemaphoreType.DMA((n, 3))], phases)


def _pair_halves(stacks):
    n = len(stacks)

    def body(*refs):
        ins, outs = refs[:n], refs[n:2 * n]
        ssem, rsem = refs[2 * n:]
        x, y, c = _place()
        cps = [pltpu.make_async_remote_copy(
            src_ref=_half_view(ins[a].at[l], a, c), dst_ref=_half_view(outs[a].at[l], a, c), send_sem=ssem.at[a, l],
            recv_sem=rsem.at[a, l], device_id=(x, y, 1 - c), device_id_type=MESH)
            for a in range(n) for l in range(DEPTH)]
        for cp in cps:
            cp.start()
        for a in range(n):
            for l in range(DEPTH):
                got = _half_view(outs[a].at[l], a, 1 - c)
                pltpu.make_async_remote_copy(src_ref=got, dst_ref=got, send_sem=ssem.at[a, l], recv_sem=rsem.at[a, l],
                                             device_id=(x, y, 1 - c), device_id_type=MESH).wait_recv()
        for cp in cps:
            cp.wait_send()

    return _pcall(body, name="pair_halves", in_specs=[ANY] * n, out_specs=[ANY] * n,
                  out_shape=[jax.ShapeDtypeStruct(st.shape, st.dtype) for st in stacks],
                  input_output_aliases={a: a for a in range(n)},
                  scratch_shapes=[pltpu.SemaphoreType.DMA((n, DEPTH)), pltpu.SemaphoreType.DMA((n, DEPTH))])(*stacks)


class _GradReduce:
    def __init__(self, chip, core):
        self.chip, self.core = chip, core
        self.stacks = [None] * len(SHARD_AXES)

    def add(self, layer, grads, which, got):
        return [(layer, w) + tuple(_add_half(g, o, self.core, w, f"add_half_{layer}_{w}"))
                for g, o, w in zip(grads, got, which)]

    def finish(self, sums, partials):
        for (layer, w, wide, _), pr in zip(sums, partials):
            self.stacks[w] = _sum_half(wide, pr, self.chip, self.core, layer, w, self.stacks[w], f"sum_half_{layer}_{w}")

    def result(self):
        return _pair_halves(self.stacks)


def _exchange_small(pack, name):
    nd = 8

    def body(p_ref, o_ref, ssem, rsem):
        x, y, c = _place()
        me = 4 * x + 2 * y + c
        o_ref[me] = p_ref[...]
        cps = []
        for j in range(1, nd):
            px, py, pc = x ^ (j >> 2), y ^ ((j >> 1) & 1), c ^ (j & 1)
            cps.append(pltpu.make_async_remote_copy(
                src_ref=p_ref, dst_ref=o_ref.at[me], send_sem=ssem.at[j - 1], recv_sem=rsem.at[j - 1],
                device_id=(px, py, pc), device_id_type=MESH))
        for cp in cps:
            cp.start()
        for j in range(1, nd):
            peer = me ^ j
            got = o_ref.at[peer]
            pltpu.make_async_remote_copy(src_ref=got, dst_ref=got, send_sem=ssem.at[j - 1], recv_sem=rsem.at[j - 1],
                                         device_id=(x, y, c), device_id_type=MESH).wait_recv()
        for cp in cps:
            cp.wait_send()

    vm = pl.BlockSpec(memory_space=pltpu.VMEM)
    return _pcall(body, name=name, in_specs=[vm], out_specs=vm,
                  out_shape=jax.ShapeDtypeStruct((nd,) + pack.shape, pack.dtype),
                  scratch_shapes=[pltpu.SemaphoreType.DMA((nd - 1,)), pltpu.SemaphoreType.DMA((nd - 1,))])(pack)


def _row(v):
    return v.reshape(1, -1)


def _local_step(x, p, tgt, norm_g, conv_w, conv_b, branch_g, ple_norm_g, b_pg, final_g, w_in, w_out, w_pg, w_pe,
                gather=False, reduce=None):
    saved = []
    xl = x
    for l in range(DEPTH):
        riders = [_gather_rider([w_out, w_pg, w_pe], 0, [1, 2, 3])] if gather and l == 0 else []
        h, proj, got = _inproj(xl, _row(norm_g[l]), w_in, l, f"inproj_{l}", riders)
        if riders:
            w_out, w_pg, w_pe = got
        riders = [_gather_rider([w_in, w_out, w_pg, w_pe], l + 1, [0, 1, 2, 3])] if gather and l + 1 < DEPTH else []
        ya, tl, walked, got = _attn_fwd(proj, f"attn_fwd_{l}", riders)
        if riders:
            w_in, w_out, w_pg, w_pe = got
        y = _mix_fwd(proj, ya, conv_w[l], _row(conv_b[l]), _row(branch_g[l]), f"mix_fwd_{l}")
        x1, hn = _outproj(y, w_out, l, xl, _row(ple_norm_g[l]), f"outproj_{l}")
        x2, gate, e = _ple_fwd(hn, w_pg, _row(b_pg[l]), p, w_pe, l, x1, f"ple_fwd_{l}")
        saved.append((xl, h, proj, ya, tl, walked, y, x1, hn, gate, e))
        xl = x2

    sq, dx, d_final = _loss_head(xl, tgt, _row(final_g), "loss_head")

    big = [None] * DEPTH
    carried = None
    small = {k: [None] * DEPTH for k in ("norm_g", "conv_w", "conv_b", "branch_g", "ple_norm_g", "b_pg")}
    for l in reversed(range(DEPTH)):
        xl, h, proj, ya, tl, walked, y, x1, hn, gate, e = saved[l]
        riders = [_swap_rider([carried[1]], [0])] if carried else []
        du, de, dx1, dy, db_pg, d_ple, got = _ple_bwd(dx, gate, e, x1, w_pg, _row(ple_norm_g[l]), w_out, l,
                                                      f"ple_bwd_{l}", riders)
        sums = reduce.add(carried[0], [carried[1]], [0], got) if carried else []
        g_pg = _mm_tn(hn, du, f"grad_w_pg_{l}")
        g_pe = _mm_tn(p, de, f"grad_w_pe_{l}", a_layer=l)
        g_out = _mm_tn(y, dx1, f"grad_w_out_{l}")
        others = [g_out, g_pg, g_pe]
        riders = [_swap_rider(others, [1, 2, 3])] if reduce is not None else []
        dpc, d_cw, d_cb, d_bg_c, got = _convmix_bwd(dy, proj, conv_w[l], _row(conv_b[l]), _row(branch_g[l]),
                                                    f"convmix_bwd_{l}", riders)
        if reduce is not None:
            sums += reduce.add(l, others, [1, 2, 3], got)
        riders = [_scatter_rider([sm[3] for sm in sums], [sm[1] for sm in sums])] if sums else []
        dproj, d_bg_a, got = _attn_bwd(proj, dy, ya, tl, walked, _row(branch_g[l]), dpc, f"attn_bwd_{l}", riders)
        if sums:
            reduce.finish(sums, got)
        g_in = _grad_w_in(h, dproj, f"grad_w_in_{l}")
        dx, d_norm = _inproj_bwd(dproj, w_in, l, xl, _row(norm_g[l]), dx1, f"inproj_bwd_{l}")
        big[l] = (g_in, g_out, g_pg, g_pe)
        carried = (l, g_in) if reduce is not None else None
        small["norm_g"][l] = jnp.sum(d_norm, axis=0)
        small["conv_w"][l] = jnp.sum(d_cw, axis=1)
        small["conv_b"][l] = jnp.sum(d_cb, axis=0)
        small["branch_g"][l] = jnp.concatenate([jnp.sum(d_bg_c, axis=0), jnp.sum(d_bg_a, axis=0)])
        small["ple_norm_g"][l] = jnp.sum(d_ple, axis=0)
        small["b_pg"][l] = jnp.sum(db_pg, axis=0)
    if carried:
        got = _ride_alone(_swap_rider([carried[1]], [0]), "swap_halves_last")
        sums = reduce.add(carried[0], [carried[1]], [0], got)
        reduce.finish(sums, _ride_alone(_scatter_rider([sm[3] for sm in sums], [0]), "scatter_shards_last"))
    small = {k: jnp.stack(v) for k, v in small.items()}
    small["final_g"] = jnp.sum(d_final, axis=0)
    return sq[0, 0], dx, big, small


SMALL_ORDER = ("norm_g", "conv_w", "conv_b", "branch_g", "ple_norm_g", "b_pg", "final_g")


def _pack(parts, width):
    flat = jnp.concatenate([v.reshape(-1) for v in parts])
    rows = -(-flat.shape[0] // width)
    rows = -(-rows // 8) * 8
    return jnp.pad(flat, (0, rows * width - flat.shape[0])).reshape(rows, width)


def _unpack(packed, like):
    flat = packed.reshape(-1)
    out, off = [], 0
    for v in like:
        out.append(flat[off:off + v.size].reshape(v.shape))
        off += v.size
    return out


def kernel(x, p, norm_g, w_in, conv_w, conv_b, branch_g, w_out, ple_norm_g, w_pg, b_pg, w_pe, final_g, loss_target, m_norm_g, m_w_in, m_conv_w, m_conv_b, m_branch_g, m_w_out, m_ple_norm_g, m_w_pg, m_b_pg, m_w_pe, m_final_g, v_norm_g, v_w_in, v_conv_w, v_conv_b, v_branch_g, v_w_out, v_ple_norm_g, v_w_pg, v_b_pg, v_w_pe, v_final_g):
    ix, iy, ic = _place()
    chip = 2 * ix + iy
    d = x.shape[-1]

    big_w = (w_in, w_out, w_pg, w_pe)
    own = [_cast_into_full(w, chip, ax, f"cast_{i}") for i, (w, ax) in enumerate(zip(big_w, SHARD_AXES))]
    full_in, = _ride_alone(_gather_rider([own[0]], 0, [0]), "gather_w_in_0")
    full_out, full_pg, full_pe = own[1:]
    cw_shard = conv_w.shape[-1]
    cw_slots = _exchange_small(_pack([conv_w], LANES), "exchange_conv_w")
    conv_full = jnp.concatenate([_unpack(cw_slots[2 * k], [conv_w])[0] for k in range(4)], axis=-1)

    reduce = _GradReduce(chip, ic)
    sq, dx, _, small_g = _local_step(
        x[0], p[:, 0], loss_target[0], norm_g, conv_full, conv_b, branch_g, ple_norm_g, b_pg, final_g,
        full_in, full_out, full_pg, full_pe, gather=True, reduce=reduce)
    g_big = reduce.result()

    parts = [small_g[k] for k in SMALL_ORDER] + [sq.reshape(1)]
    slots = _exchange_small(_pack(parts, d), "exchange_small_grads")
    total = _unpack(_sum_slots(slots, "sum_small"), parts)
    g_small = dict(zip(SMALL_ORDER, total[:-1]))
    loss = 0.5 * total[-1][0] / d
    g_small["conv_w"] = lax.dynamic_slice_in_dim(g_small["conv_w"], chip * cw_shard, cw_shard, axis=2)

    grads = dict(g_small)
    grads.update(w_in=g_big[0], w_out=g_big[1], w_pg=g_big[2], w_pe=g_big[3])
    weights = dict(norm_g=norm_g, w_in=w_in, conv_w=conv_w, conv_b=conv_b, branch_g=branch_g, w_out=w_out,
                   ple_norm_g=ple_norm_g, w_pg=w_pg, b_pg=b_pg, w_pe=w_pe, final_g=final_g)
    ms = dict(norm_g=m_norm_g, w_in=m_w_in, conv_w=m_conv_w, conv_b=m_conv_b, branch_g=m_branch_g, w_out=m_w_out,
              ple_norm_g=m_ple_norm_g, w_pg=m_w_pg, b_pg=m_b_pg, w_pe=m_w_pe, final_g=m_final_g)
    vs = dict(norm_g=v_norm_g, w_in=v_w_in, conv_w=v_conv_w, conv_b=v_conv_b, branch_g=v_branch_g, w_out=v_w_out,
              ple_norm_g=v_ple_norm_g, w_pg=v_w_pg, b_pg=v_b_pg, w_pe=v_w_pe, final_g=v_final_g)
    names = ("norm_g", "w_in", "conv_w", "conv_b", "branch_g", "w_out", "ple_norm_g", "w_pg", "b_pg", "w_pe", "final_g")
    delta, new_m, new_v = {}, {}, {}
    for k in ("w_in", "w_out", "w_pg", "w_pe"):
        shp = weights[k].shape
        two = lambda a: a.reshape(-1, shp[-1])
        dl, mn, vn = _adamw(two(weights[k]), two(grads[k]), two(ms[k]), two(vs[k]), f"adamw_{k}")
        delta[k], new_m[k], new_v[k] = dl.reshape(shp), mn.reshape(shp), vn.reshape(shp)
        grads[k] = grads[k].reshape(shp)
    like = [weights[k] for k in SMALL_ORDER]
    packs = [_pack([src[k] for k in SMALL_ORDER], d) for src in (weights, grads, ms, vs)]
    outs = _adamw(*packs, "adamw_small")
    for res, o in zip((delta, new_m, new_v), outs):
        res.update(dict(zip(SMALL_ORDER, _unpack(o, like))))

    return (loss, dx[None], *[grads[k] for k in names], *[delta[k] for k in names],
            *[new_m[k] for k in names], *[new_v[k] for k in names])
```

```python
import math

import jax
import jax.numpy as jnp
from jax import lax
from jax.experimental import pallas as pl
from jax.experimental.pallas import tpu as pltpu

F32 = jnp.float32
BF16 = jnp.bfloat16
EPS = 1e-6
HEAD = 64
LANES = 128
ATT_TK = 256
ATT_TQ = 512
ATT_ROWS = 128
ALIVE_LOG = -105.0
DEPTH = 2
VMEM_LIMIT = 56 * 1024 * 1024
MESH = pl.DeviceIdType.MESH
ANY = pl.BlockSpec(memory_space=pl.ANY)

ADAM_LR = 0.001
ADAM_B1 = 0.9
ADAM_B2 = 0.999
ADAM_EPS = 1e-08
ADAM_WD = 0.01
ADAM_STEP = 10


def _pcall(body, **kw):
    return pl.pallas_call(body, **kw)


def _cp(n_axes):
    return pltpu.CompilerParams(dimension_semantics=("arbitrary",) * n_axes, vmem_limit_bytes=VMEM_LIMIT)


def _tile(n, pref):
    return pref if n % pref == 0 else n


def _split_dot(a, b, passes):
    out = None
    rem = a
    for _ in range(passes):
        hi = rem.astype(BF16)
        t = jnp.dot(hi, b, preferred_element_type=F32)
        out = t if out is None else out + t
        rem = rem - hi.astype(F32)
    return out


def _group_mat():
    r = lax.broadcasted_iota(jnp.int32, (LANES, LANES), 0) // HEAD
    c = lax.broadcasted_iota(jnp.int32, (LANES, LANES), 1) // HEAD
    return jnp.where(r == c, 1.0 / HEAD, 0.0).astype(BF16)


def _group_mean(v, gm):
    return _split_dot(v, gm, 2)


def _sigmoid(z):
    return 1.0 / (1.0 + jnp.exp(-z))


def _dot_nt(a, b):
    return lax.dot_general(a, b, (((1,), (1,)), ((), ())), preferred_element_type=F32)


def _dot_tn(a, b):
    return lax.dot_general(a, b, (((0,), (0,)), ((), ())), preferred_element_type=F32)


def _cast_into_full(w, chip, axis, name):
    _, r, c = w.shape
    tr = _tile(r, 256)
    nb = r // tr
    full = (DEPTH, 4 * r, c) if axis == 0 else (DEPTH, r, 4 * c)

    def body(k_ref, w_ref, o_ref):
        o_ref[...] = w_ref[...].astype(BF16)

    out_map = (lambda l, i, k: (l, k[0] * nb + i, 0)) if axis == 0 else (lambda l, i, k: (l, i, k[0]))
    grid_spec = pltpu.PrefetchScalarGridSpec(
        num_scalar_prefetch=1, grid=(DEPTH, nb),
        in_specs=[pl.BlockSpec((1, tr, c), lambda l, i, k: (l, i, 0))],
        out_specs=pl.BlockSpec((1, tr, c), out_map))
    return _pcall(body, name=name, grid_spec=grid_spec, out_shape=jax.ShapeDtypeStruct(full, BF16),
                  compiler_params=_cp(2))(chip.reshape(1).astype(jnp.int32), w)


def _rms_bwd_rows(dh, xv, g):
    r = lax.rsqrt(jnp.mean(xv * xv, axis=-1, keepdims=True) + EPS)
    xn = xv * r
    dxn = dh * g
    dx = r * (dxn - xn * jnp.mean(dxn * xn, axis=-1, keepdims=True))
    return dx, dh * xn


def _colsum8(v):
    tm, d = v.shape
    return jnp.sum(v.reshape(tm // 8, 8, d), axis=0)


def _inproj(x, g, w, layer, name, riders=()):
    s, d = x.shape
    n = w.shape[2]
    sw = d // 2
    ns = n // sw
    tm = _tile(s, 512)

    def body(*refs):
        own, riders_end = _riders_run(riders, refs, 3, 2, 0, pl.program_id(0), s // tm)
        x_ref, g_ref, w_ref, h_ref, o_ref = own
        xv = x_ref[...]
        r = lax.rsqrt(jnp.mean(xv * xv, axis=-1, keepdims=True) + EPS)
        h = (xv * r * g_ref[...]).astype(BF16)
        h_ref[...] = h
        for k in range(ns):
            o_ref[k] = jnp.dot(h, w_ref[0, :, k * sw:(k + 1) * sw], preferred_element_type=F32).astype(BF16)
        riders_end()

    r_ops, r_shapes, r_scratch, r_aliases = _riders_plumb(riders, 3, 2)
    outs = _pcall(body, name=name, grid=(s // tm,),
                  in_specs=[pl.BlockSpec((tm, d), lambda m: (m, 0)), pl.BlockSpec((1, d), lambda m: (0, 0)),
                            pl.BlockSpec((1, d, n), lambda m: (layer, 0, 0))] + [ANY] * len(r_ops),
                  out_specs=[pl.BlockSpec((tm, d), lambda m: (m, 0)), pl.BlockSpec((ns, tm, sw), lambda m: (0, m, 0))]
                  + [ANY] * len(r_shapes),
                  out_shape=[jax.ShapeDtypeStruct((s, d), BF16), jax.ShapeDtypeStruct((ns, s, sw), BF16)] + r_shapes,
                  input_output_aliases=r_aliases, scratch_shapes=r_scratch,
                  compiler_params=_cp(1))(x, g, w, *r_ops)
    return outs[0], outs[1], list(outs[2:])


def _softplus_parts(z):
    lm = jnp.minimum(-z, 0.0) - jnp.log(1.0 + jnp.exp(-jnp.abs(z)))
    return lm, lm + z


def _attn_tiles(s):
    tk = _tile(s, ATT_TK)
    tq = _tile(s, ATT_TQ)
    return tk, tq, tq // tk, min(ATT_ROWS, tq)


def _diag_work(chains, d, rows, tk):
    work = []
    for n, (_, r0) in enumerate(chains):
        if r0 + rows - 1 <= d * tk:
            continue
        kw = tk // 2 if (tk % 2 == 0 and r0 + rows <= d * tk + tk // 2) else tk
        if r0 >= d * tk + kw:
            mask = None
        else:
            row = lax.broadcasted_iota(jnp.int32, (rows, kw), 0)
            col = lax.broadcasted_iota(jnp.int32, (rows, kw), 1)
            mask = col + d * tk < row + r0
        work.append((n, kw, mask))
    return work


def _walk_work(chains, tk, below=None):
    return [(n, tk, None) for n, (_, r0) in enumerate(chains) if below is None or r0 < below]


def _any_alive(rsums):
    m = rsums[0]
    for r in rsums[1:]:
        m = jnp.maximum(m, r)
    return jnp.max((m > ALIVE_LOG).astype(jnp.int32))


def _attn_fwd(proj, name, riders=()):
    _, s, sw = proj.shape
    nhp = sw // LANES
    tk, tq, nd, rows = _attn_tiles(s)
    nq = s // tq
    scale = 1.0 / math.sqrt(HEAD)

    def body(*refs):
        i = pl.program_id(1)
        own, riders_end = _riders_run(riders, refs, 3, 3, 1, pl.program_id(0) * nq + i, nhp * nq)
        q_ref, k_ref, v_ref, o_ref, tl_ref, nw_ref, acc_ref = own
        tri = (lax.broadcasted_iota(jnp.int32, (tk, tk), 0) >
               lax.broadcasted_iota(jnp.int32, (tk, tk), 1)).astype(BF16)
        lane = lax.broadcasted_iota(jnp.int32, (tq, LANES), 1)
        q = q_ref[0] * jnp.asarray(scale, BF16)
        qms = [jnp.where((lane // HEAD) == h, q, jnp.zeros_like(q)) for h in range(2)]
        acc_ref[...] = jnp.zeros_like(acc_ref)
        chains = [(h, r0) for h in range(2) for r0 in range(0, tq, rows)]
        qparts = [qms[h][r0:r0 + rows] for h, r0 in chains]

        def tile(j, rsums, work):
            k0 = pl.multiple_of(j * tk, tk)
            kj = k_ref[0, pl.ds(k0, tk), :]
            vj = v_ref[0, pl.ds(k0, tk), :]
            zs = [_dot_nt(qparts[n], kj[:kw]) for n, kw, _ in work]
            lms, lss, css = [], [], []
            for z, (n, kw, mask) in zip(zs, work):
                lm, ls = _softplus_parts(z)
                if mask is not None:
                    lm = jnp.where(mask, lm, 0.0)
                lms.append(lm)
                lss.append(ls)
                css.append(_split_dot(lm, tri[:kw, :kw], 2))
            out = list(rsums)
            for lm, ls, cs, (n, kw, mask) in zip(lms, lss, css, work):
                h, r0 = chains[n]
                a = jnp.exp(ls + (rsums[n] + cs))
                if mask is not None:
                    a = jnp.where(mask, a, 0.0)
                acc_ref[h, r0:r0 + rows, :] += jnp.dot(a.astype(BF16), vj[:kw], preferred_element_type=F32)
                out[n] = rsums[n] + jnp.sum(lm, axis=1, keepdims=True)
            return tuple(out)

        rsums = (jnp.zeros((rows, 1), F32),) * len(chains)
        for d in reversed(range(nd)):
            rsums = tile(i * nd + d, rsums, _diag_work(chains, d, rows, tk))

        upper = [rs for rs, (_, r0) in zip(rsums, chains) if r0 >= tk]
        lower = [rs for rs, (_, r0) in zip(rsums, chains) if r0 < tk]
        if upper:
            alone = (i > 0) & (_any_alive(upper) == 0) & (_any_alive(lower) > 0)
            rsums = lax.cond(alone, lambda rs: tile(i * nd - 1, rs, _walk_work(chains, tk, below=tk)),
                             lambda rs: rs, rsums)
            alone = alone.astype(jnp.int32)
        else:
            alone = jnp.int32(0)

        def walk(c):
            jj, rs, _ = c
            rs = tile(i * nd - 1 - jj, rs, _walk_work(chains, tk))
            return jj + 1, rs, _any_alive(rs)

        last, rsums, _ = lax.while_loop(lambda c: (c[0] < i * nd) & (c[2] > 0), walk,
                                        (alone, rsums, _any_alive(rsums)))
        for n, (h, r0) in enumerate(chains):
            tl_ref[h, r0:r0 + rows, :] = rsums[n]
        nw_ref[0] = (jnp.zeros((8, LANES), jnp.int32) + (2 * (last - alone) + alone)).astype(F32)
        o_ref[...] = jnp.where(lane < HEAD, acc_ref[0], acc_ref[1]).astype(BF16)
        riders_end()

    r_ops, r_shapes, r_scratch, r_aliases = _riders_plumb(riders, 3, 3)
    outs = _pcall(
        body, name=name, grid=(nhp, nq),
        in_specs=[pl.BlockSpec((1, tq, LANES), lambda hp, i: (4, i, hp)),
                  pl.BlockSpec((1, s, LANES), lambda hp, i: (5, 0, hp)),
                  pl.BlockSpec((1, s, LANES), lambda hp, i: (6, 0, hp))] + [ANY] * len(r_ops),
        out_specs=[pl.BlockSpec((tq, LANES), lambda hp, i: (i, hp)),
                   pl.BlockSpec((2, tq, 1), lambda hp, i: (hp, i, 0)),
                   pl.BlockSpec((1, 8, LANES), lambda hp, i: (hp * nq + i, 0, 0))] + [ANY] * len(r_shapes),
        out_shape=[jax.ShapeDtypeStruct((s, sw), BF16), jax.ShapeDtypeStruct((2 * nhp, s, 1), F32),
                   jax.ShapeDtypeStruct((nhp * nq, 8, LANES), F32)] + r_shapes,
        input_output_aliases=r_aliases,
        scratch_shapes=[pltpu.VMEM((2, tq, LANES), F32)] + r_scratch,
        compiler_params=_cp(2))(proj, proj, proj, *r_ops)
    return outs[0], outs[1], outs[2], list(outs[3:])


def _conv_rows(cc_ref, ch_ref, w_ref, b_ref, r, tc):
    r0 = pl.multiple_of(r * tc, tc)
    u = cc_ref[0, pl.ds(r0, tc), :].astype(F32) * ch_ref[0, pl.ds(r0, tc), :].astype(F32)
    p0 = pl.multiple_of(jnp.maximum(r0 - 16, 0), 16)
    up = cc_ref[0, pl.ds(p0, 16), :].astype(F32) * ch_ref[0, pl.ds(p0, 16), :].astype(F32)
    up = up * (r > 0).astype(F32)
    prev1 = up[15:16, :]
    prev2 = up[14:15, :]
    rid = lax.broadcasted_iota(jnp.int32, u.shape, 0)
    s1 = jnp.where(rid == 0, prev1, pltpu.roll(u, 1, axis=0))
    s2 = jnp.where(rid == 0, prev2, jnp.where(rid == 1, prev1, pltpu.roll(u, 2, axis=0)))
    cv = b_ref[...] + s2 * w_ref[0:1, :] + s1 * w_ref[1:2, :] + u * w_ref[2:3, :]
    return r0, u, s1, s2, cv


def _mix_fwd(proj, ya, conv_w, conv_b, bg, name):
    _, s, sw = proj.shape
    nh = sw // LANES
    tc = _tile(s, 256)

    def body(cb_ref, cc_ref, ch_ref, cz_ref, ya_ref, az_ref, w_ref, b_ref, g_ref, y_ref):
        c = pl.program_id(0)
        gm = _group_mat()

        def finish(r0, yv, zg):
            n = yv * lax.rsqrt(_group_mean(yv * yv, gm) + EPS)
            y_ref[pl.ds(r0, tc), :] = (n * g_ref[...] * (zg * _sigmoid(zg))).astype(BF16)

        @pl.when(c < nh)
        def _():
            def step(r, carry):
                r0, _, _, _, cv = _conv_rows(cc_ref, ch_ref, w_ref, b_ref, r, tc)
                yc = cb_ref[0, pl.ds(r0, tc), :].astype(F32) * cv
                finish(r0, yc, cz_ref[0, pl.ds(r0, tc), :].astype(F32))
                return carry
            lax.fori_loop(0, s // tc, step, 0)

        @pl.when(c >= nh)
        def _():
            def step(r, carry):
                r0 = pl.multiple_of(r * tc, tc)
                finish(r0, ya_ref[pl.ds(r0, tc), :].astype(F32), az_ref[0, pl.ds(r0, tc), :].astype(F32))
                return carry
            lax.fori_loop(0, s // tc, step, 0)

    def sec(k):
        return pl.BlockSpec((1, s, LANES), lambda c: (k, 0, jnp.minimum(c, nh - 1)))

    return _pcall(
        body, name=name, grid=(2 * nh,),
        in_specs=[sec(0), sec(1), sec(2), sec(3),
                  pl.BlockSpec((s, LANES), lambda c: (0, jnp.maximum(c - nh, 0))),
                  pl.BlockSpec((1, s, LANES), lambda c: (7, 0, jnp.maximum(c - nh, 0))),
                  pl.BlockSpec((3, LANES), lambda c: (0, jnp.minimum(c, nh - 1))),
                  pl.BlockSpec((1, LANES), lambda c: (0, jnp.minimum(c, nh - 1))),
                  pl.BlockSpec((1, LANES), lambda c: (0, c))],
        out_specs=pl.BlockSpec((s, LANES), lambda c: (0, c)),
        out_shape=jax.ShapeDtypeStruct((s, 2 * sw), BF16), compiler_params=_cp(1),
    )(proj, proj, proj, proj, ya, proj, conv_w, conv_b, bg)


def _outproj(y, w, layer, x, g, name):
    s, d = x.shape
    tm = _tile(s, 512)

    def body(y_ref, w_ref, x_ref, g_ref, x1_ref, hn_ref):
        x1 = x_ref[...] + jnp.dot(y_ref[...], w_ref[0], preferred_element_type=F32)
        x1_ref[...] = x1
        r = lax.rsqrt(jnp.mean(x1 * x1, axis=-1, keepdims=True) + EPS)
        hn_ref[...] = (x1 * r * g_ref[...]).astype(BF16)

    row = lambda m: (m, 0)
    fix = lambda m: (0, 0)
    return _pcall(body, name=name, grid=(s // tm,),
                  in_specs=[pl.BlockSpec((tm, d), row), pl.BlockSpec((1, d, d), lambda m: (layer, 0, 0)),
                            pl.BlockSpec((tm, d), row), pl.BlockSpec((1, d), fix)],
                  out_specs=[pl.BlockSpec((tm, d), row), pl.BlockSpec((tm, d), row)],
                  out_shape=[jax.ShapeDtypeStruct((s, d), F32), jax.ShapeDtypeStruct((s, d), BF16)],
                  compiler_params=_cp(1))(y, w, x, g)


def _ple_fwd(hn, w_pg, b_pg, p, w_pe, layer, x1, name):
    s, d = x1.shape
    pd = p.shape[2]
    tm = _tile(s, 512)

    def body(hn_ref, wg_ref, b_ref, p_ref, we_ref, x1_ref, x2_ref, gate_ref, e_ref):
        gate = _sigmoid(jnp.dot(hn_ref[...], wg_ref[0], preferred_element_type=F32) + b_ref[...])
        e = jnp.dot(p_ref[0].astype(BF16), we_ref[0], preferred_element_type=F32)
        x2_ref[...] = x1_ref[...] + gate * e
        gate_ref[...] = gate.astype(BF16)
        e_ref[...] = e.astype(BF16)

    row = lambda m: (m, 0)
    fix = lambda m: (0, 0)
    return _pcall(body, name=name, grid=(s // tm,),
                  in_specs=[pl.BlockSpec((tm, d), row), pl.BlockSpec((1, d, d), lambda m: (layer, 0, 0)),
                            pl.BlockSpec((1, d), fix), pl.BlockSpec((1, tm, pd), lambda m: (layer, m, 0)),
                            pl.BlockSpec((1, pd, d), lambda m: (layer, 0, 0)), pl.BlockSpec((tm, d), row)],
                  out_specs=[pl.BlockSpec((tm, d), row)] * 3,
                  out_shape=[jax.ShapeDtypeStruct((s, d), F32), jax.ShapeDtypeStruct((s, d), BF16),
                             jax.ShapeDtypeStruct((s, d), BF16)],
                  compiler_params=_cp(1))(hn, w_pg, b_pg, p, w_pe, x1)


def _loss_head(x, tgt, g, name):
    s, d = x.shape
    tm = _tile(s, 512)

    def body(x_ref, t_ref, g_ref, l_ref, dx_ref, dg_ref):
        m = pl.program_id(0)

        @pl.when(m == 0)
        def _():
            l_ref[...] = jnp.zeros_like(l_ref)
            dg_ref[...] = jnp.zeros_like(dg_ref)

        xv = x_ref[...]
        gv = g_ref[...]
        r = lax.rsqrt(jnp.mean(xv * xv, axis=-1, keepdims=True) + EPS)
        xn = xv * r
        err = xn * gv - t_ref[...]
        l_ref[...] += jnp.sum(err * err)
        dy = err * (1.0 / d)
        dxn = dy * gv
        dx_ref[...] = r * (dxn - xn * jnp.mean(dxn * xn, axis=-1, keepdims=True))
        dg_ref[...] += _colsum8(dy * xn)

    row = lambda m: (m, 0)
    fix = lambda m: (0, 0)
    return _pcall(body, name=name, grid=(s // tm,),
                  in_specs=[pl.BlockSpec((tm, d), row), pl.BlockSpec((tm, d), row), pl.BlockSpec((1, d), fix)],
                  out_specs=[pl.BlockSpec((8, LANES), fix), pl.BlockSpec((tm, d), row), pl.BlockSpec((8, d), fix)],
                  out_shape=[jax.ShapeDtypeStruct((8, LANES), F32), jax.ShapeDtypeStruct((s, d), F32),
                             jax.ShapeDtypeStruct((8, d), F32)],
                  compiler_params=_cp(1))(x, tgt, g)


def _ple_bwd(dx2, gate, e, x1, w_pg, g_ple, w_out, layer, name, riders=()):
    s, d = dx2.shape
    tm = _tile(s, 512)

    def body(*refs):
        m = pl.program_id(0)
        own, riders_end = _riders_run(riders, refs, 7, 6, 0, m, s // tm)
        (dx2_ref, gate_ref, e_ref, x1_ref, wg_ref, g_ref, wo_ref,
         du_ref, de_ref, dx1_ref, dy_ref, db_ref, dg_ref) = own

        @pl.when(m == 0)
        def _():
            db_ref[...] = jnp.zeros_like(db_ref)
            dg_ref[...] = jnp.zeros_like(dg_ref)

        dx2v = dx2_ref[...]
        gate = gate_ref[...].astype(F32)
        du = dx2v * e_ref[...].astype(F32) * gate * (1.0 - gate)
        de_ref[...] = (dx2v * gate).astype(BF16)
        dub = du.astype(BF16)
        du_ref[...] = dub
        db_ref[...] += _colsum8(du)
        dhn = _dot_nt(dub, wg_ref[0])
        dxr, dgr = _rms_bwd_rows(dhn, x1_ref[...], g_ref[...])
        dx1 = dx2v + dxr
        dx1_ref[...] = dx1
        dg_ref[...] += _colsum8(dgr)
        dy_ref[...] = _dot_nt(dx1.astype(BF16), wo_ref[0]).astype(BF16)
        riders_end()

    row = lambda m: (m, 0)
    fix = lambda m: (0, 0)
    t = pl.BlockSpec((tm, d), row)
    r_ops, r_shapes, r_scratch, r_aliases = _riders_plumb(riders, 7, 6)
    outs = _pcall(body, name=name, grid=(s // tm,),
                  in_specs=[t, t, t, t, pl.BlockSpec((1, d, d), lambda m: (layer, 0, 0)), pl.BlockSpec((1, d), fix),
                            pl.BlockSpec((1, d, d), lambda m: (layer, 0, 0))] + [ANY] * len(r_ops),
                  out_specs=[t, t, t, t, pl.BlockSpec((8, d), fix), pl.BlockSpec((8, d), fix)] + [ANY] * len(r_shapes),
                  out_shape=[jax.ShapeDtypeStruct((s, d), BF16), jax.ShapeDtypeStruct((s, d), BF16),
                             jax.ShapeDtypeStruct((s, d), F32), jax.ShapeDtypeStruct((s, d), BF16),
                             jax.ShapeDtypeStruct((8, d), F32), jax.ShapeDtypeStruct((8, d), F32)] + r_shapes,
                  input_output_aliases=r_aliases, scratch_shapes=r_scratch,
                  compiler_params=_cp(1))(dx2, gate, e, x1, w_pg, g_ple, w_out, *r_ops)
    return tuple(outs[:6]) + (list(outs[6:]),)


def _mm_tn(a, b, name, a_layer=None):
    s, ka = a.shape[-2:]
    n = b.shape[1]
    tn = _tile(n, 1024)
    ns = n // tn
    tk = _tile(s, 512)
    nk = s // tk

    def body(a_ref, b_ref, o_ref, acc_ref):
        k = pl.program_id(1)

        @pl.when(k == 0)
        def _():
            acc_ref[...] = jnp.zeros_like(acc_ref)

        av = a_ref[...] if a_layer is None else a_ref[0]
        acc_ref[...] += _dot_tn(av.astype(BF16), b_ref[...].astype(BF16))

        @pl.when(k == nk - 1)
        def _():
            o_ref[...] = acc_ref[...]

    a_spec = (pl.BlockSpec((tk, ka), lambda j, k: (k, 0)) if a_layer is None
              else pl.BlockSpec((1, tk, ka), lambda j, k: (a_layer, k, 0)))
    return _pcall(body, name=name, grid=(ns, nk),
                  in_specs=[a_spec, pl.BlockSpec((tk, tn), lambda j, k: (k, j))],
                  out_specs=pl.BlockSpec((ka, tn), lambda j, k: (0, j)),
                  out_shape=jax.ShapeDtypeStruct((ka, n), F32),
                  scratch_shapes=[pltpu.VMEM((ka, tn), F32)], compiler_params=_cp(2))(a, b)


def _norm_gate_bwd(dy, yv, zg, g, gm):
    r = lax.rsqrt(_group_mean(yv * yv, gm) + EPS)
    n = yv * r
    sg = _sigmoid(zg)
    sil = zg * sg
    dzg = dy * n * g * (sg * (1.0 + zg * (1.0 - sg)))
    dn = dy * g * sil
    dyv = r * (dn - n * _group_mean(dn * n, gm))
    return dyv, dzg, dy * n * sil


def _convmix_bwd(dy, proj, conv_w, conv_b, bg, name, riders=()):
    _, s, sw = proj.shape
    nh = sw // LANES
    tc = _tile(s, 256)
    nr = s // tc

    def body(*refs):
        own, riders_end = _riders_run(riders, refs, 8, 4, 1, pl.program_id(0), nh)
        (dy_ref, cb_ref, cc_ref, ch_ref, cz_ref, w_ref, b_ref, g_ref,
         dp_ref, dw_ref, db_ref, dg_ref, dcv_ref) = own
        gm = _group_mat()
        dcv_ref[pl.ds(s, 8), :] = jnp.zeros((8, LANES), F32)

        def pass1(r, carry):
            dw0, dw1, dw2, db, dg = carry
            r0, u, s1, s2, cv = _conv_rows(cc_ref, ch_ref, w_ref, b_ref, r, tc)
            cb = cb_ref[0, pl.ds(r0, tc), :].astype(F32)
            dyc, dcz, dgr = _norm_gate_bwd(dy_ref[pl.ds(r0, tc), :].astype(F32), cb * cv,
                                           cz_ref[0, pl.ds(r0, tc), :].astype(F32), g_ref[...], gm)
            dp_ref[0, pl.ds(r0, tc), :] = (dyc * cv).astype(BF16)
            dp_ref[3, pl.ds(r0, tc), :] = dcz.astype(BF16)
            dcv = dyc * cb
            dcv_ref[pl.ds(r0, tc), :] = dcv
            return (dw0 + _colsum8(dcv * s2), dw1 + _colsum8(dcv * s1), dw2 + _colsum8(dcv * u),
                    db + _colsum8(dcv), dg + _colsum8(dgr))

        z8 = jnp.zeros((8, LANES), F32)
        dw0, dw1, dw2, db, dg = lax.fori_loop(0, nr, pass1, (z8, z8, z8, z8, z8))
        dw_ref[0] = dw0
        dw_ref[1] = dw1
        dw_ref[2] = dw2
        db_ref[...] = db
        dg_ref[...] = dg

        def pass2(r, carry):
            r0 = pl.multiple_of(r * tc, tc)
            dcv = dcv_ref[pl.ds(r0, tc), :]
            nxt = dcv_ref[pl.ds(pl.multiple_of(r0 + tc, 8), 8), :]
            rid = lax.broadcasted_iota(jnp.int32, dcv.shape, 0)
            n1 = jnp.where(rid == tc - 1, nxt[0:1, :], pltpu.roll(dcv, tc - 1, axis=0))
            n2 = jnp.where(rid == tc - 1, nxt[1:2, :],
                           jnp.where(rid == tc - 2, nxt[0:1, :], pltpu.roll(dcv, tc - 2, axis=0)))
            du = dcv * w_ref[2:3, :] + n1 * w_ref[1:2, :] + n2 * w_ref[0:1, :]
            dp_ref[1, pl.ds(r0, tc), :] = (du * ch_ref[0, pl.ds(r0, tc), :].astype(F32)).astype(BF16)
            dp_ref[2, pl.ds(r0, tc), :] = (du * cc_ref[0, pl.ds(r0, tc), :].astype(F32)).astype(BF16)
            return carry

        lax.fori_loop(0, nr, pass2, 0)
        riders_end()

    def sec(k):
        return pl.BlockSpec((1, s, LANES), lambda c: (k, 0, c))

    col = lambda c: (0, c)
    r_ops, r_shapes, r_scratch, r_aliases = _riders_plumb(riders, 8, 4)
    outs = _pcall(
        body, name=name, grid=(nh,),
        in_specs=[pl.BlockSpec((s, LANES), col), sec(0), sec(1), sec(2), sec(3),
                  pl.BlockSpec((3, LANES), col), pl.BlockSpec((1, LANES), col), pl.BlockSpec((1, LANES), col)]
        + [ANY] * len(r_ops),
        out_specs=[pl.BlockSpec((4, s, LANES), lambda c: (0, 0, c)), pl.BlockSpec((3, 8, LANES), lambda c: (0, 0, c)),
                   pl.BlockSpec((8, LANES), col), pl.BlockSpec((8, LANES), col)] + [ANY] * len(r_shapes),
        out_shape=[jax.ShapeDtypeStruct((8, s, sw), BF16), jax.ShapeDtypeStruct((3, 8, sw), F32),
                   jax.ShapeDtypeStruct((8, sw), F32), jax.ShapeDtypeStruct((8, sw), F32)] + r_shapes,
        input_output_aliases=r_aliases,
        scratch_shapes=[pltpu.VMEM((s + 8, LANES), F32)] + r_scratch, compiler_params=_cp(1),
    )(dy, proj, proj, proj, proj, conv_w, conv_b, bg, *r_ops)
    return tuple(outs[:4]) + (list(outs[4:]),)


def _attn_bwd(proj, dy, ya, tl, walked, bg, buf, name, riders=()):
    _, s, sw = proj.shape
    nhp = sw // LANES
    tk, t, nd, rows_c = _attn_tiles(s)
    nq = s // t
    scale = 1.0 / math.sqrt(HEAD)

    def body(*refs):
        step = pl.program_id(1)
        i = nq - 1 - step
        own, riders_end = _riders_run(riders, refs, 10, 2, 3, pl.program_id(0) * nq + step, nhp * nq)
        (q_ref, k_ref, v_ref, az_ref, dy_ref, ya_ref, tl_ref, nw_ref, g_ref, buf_ref, out_ref, dg_ref,
         dka_ref, dva_ref, dqa_ref) = own

        @pl.when(step == 0)
        def _():
            dka_ref[...] = jnp.zeros_like(dka_ref)
            dva_ref[...] = jnp.zeros_like(dva_ref)
            dg_ref[...] = jnp.zeros_like(dg_ref)

        dyv, dzg, dgr = _norm_gate_bwd(dy_ref[...].astype(F32), ya_ref[...].astype(F32), az_ref[0].astype(F32),
                                       g_ref[...], _group_mat())
        out_ref[3] = dzg.astype(BF16)
        dg_ref[...] += _colsum8(dgr)

        tri = (lax.broadcasted_iota(jnp.int32, (tk, tk), 0) <=
               lax.broadcasted_iota(jnp.int32, (tk, tk), 1)).astype(BF16)
        lane = lax.broadcasted_iota(jnp.int32, (t, LANES), 1)
        q = q_ref[0] * jnp.asarray(scale, BF16)
        do = dyv.astype(BF16)
        qms = [jnp.where((lane // HEAD) == h, q, jnp.zeros_like(q)) for h in range(2)]
        doms = [jnp.where((lane // HEAD) == h, do, jnp.zeros_like(do)) for h in range(2)]
        dqa_ref[...] = jnp.zeros_like(dqa_ref)
        chains = [(h, r0) for h in range(2) for r0 in range(0, t, rows_c)]
        qparts = [qms[h][r0:r0 + rows_c] for h, r0 in chains]
        doparts = [doms[h][r0:r0 + rows_c] for h, r0 in chains]
        tots = [tl_ref[h, r0:r0 + rows_c, :] for h, r0 in chains]

        def tile(j, carry, work):
            k0 = pl.multiple_of(j * tk, tk)
            kj = k_ref[0, pl.ds(k0, tk), :]
            vj = v_ref[0, pl.ds(k0, tk), :]
            zs = [_dot_nt(qparts[n], kj[:kw]) for n, kw, _ in work]
            das = [_dot_nt(doparts[n], vj[:kw]) for n, kw, _ in work]
            lms, lss, cls = [], [], []
            for z, (n, kw, mask) in zip(zs, work):
                lm, ls = _softplus_parts(z)
                if mask is not None:
                    lm = jnp.where(mask, lm, 0.0)
                lms.append(lm)
                lss.append(ls)
                cls.append(_split_dot(lm, tri[:kw, :kw], 2))
            abs_, gs, cgs = [], [], []
            for ls, cl, da, (n, kw, mask) in zip(lss, cls, das, work):
                a = jnp.exp(ls + (tots[n] - carry[n][0] - cl))
                if mask is not None:
                    a = jnp.where(mask, a, 0.0)
                g = a * da
                gs.append(g)
                abs_.append(a.astype(BF16))
                cgs.append(_split_dot(g, tri[:kw, :kw], 1))
            out = list(carry)
            dks, dvs = {}, {}
            for lm, ls, a, g, cg, (n, kw, mask) in zip(lms, lss, abs_, gs, cgs, work):
                h, r0 = chains[n]
                psum, gsum = carry[n]
                dz = g - jnp.exp(ls) * (gsum + cg)
                if mask is not None:
                    dz = jnp.where(mask, dz, 0.0)
                dz = dz.astype(BF16)
                dqa_ref[h, r0:r0 + rows_c, :] += jnp.dot(dz, kj[:kw], preferred_element_type=F32)
                dkh = _dot_tn(dz, qparts[n])
                dvh = _dot_tn(a, doparts[n])
                dks[kw] = dkh if kw not in dks else dks[kw] + dkh
                dvs[kw] = dvh if kw not in dvs else dvs[kw] + dvh
                out[n] = (psum + jnp.sum(lm, axis=1, keepdims=True), gsum + jnp.sum(g, axis=1, keepdims=True))
            for kw in dks:
                dka_ref[pl.ds(k0, kw), :] += dks[kw]
                dva_ref[pl.ds(k0, kw), :] += dvs[kw]
            return tuple(out)

        z1 = jnp.zeros((rows_c, 1), F32)
        code = jnp.clip(jnp.max(nw_ref[0].astype(jnp.int32)), 0, 2 * i * nd + 1)
        alone = jnp.minimum(code % 2, i * nd)
        whole = jnp.minimum(code // 2, i * nd - alone)
        carry = lax.fori_loop(i * nd - alone - whole, i * nd - alone,
                              lambda j, c: tile(j, c, _walk_work(chains, tk)), ((z1, z1),) * len(chains))
        if any(r0 >= tk for _, r0 in chains):
            carry = lax.cond(alone > 0, lambda c: tile(i * nd - 1, c, _walk_work(chains, tk, below=tk)),
                             lambda c: c, carry)
        for d in range(nd):
            carry = tile(i * nd + d, carry, _diag_work(chains, d, rows_c, tk))
        out_ref[0] = (jnp.where(lane < HEAD, dqa_ref[0], dqa_ref[1]) * scale).astype(BF16)
        own = pl.multiple_of(i * t, t)
        out_ref[1] = dka_ref[pl.ds(own, t), :].astype(BF16)
        out_ref[2] = dva_ref[pl.ds(own, t), :].astype(BF16)
        riders_end()

    def rows(sec):
        return pl.BlockSpec((1, t, LANES), lambda hp, st: (sec, nq - 1 - st, hp))

    def whole(sec):
        return pl.BlockSpec((1, s, LANES), lambda hp, st: (sec, 0, hp))

    r_ops, r_shapes, r_scratch, r_aliases = _riders_plumb(riders, 10, 2)
    outs = _pcall(
        body, name=name, grid=(nhp, nq),
        in_specs=[rows(4), whole(5), whole(6), rows(7),
                  pl.BlockSpec((t, LANES), lambda hp, st: (nq - 1 - st, hp + nhp)),
                  pl.BlockSpec((t, LANES), lambda hp, st: (nq - 1 - st, hp)),
                  pl.BlockSpec((2, t, 1), lambda hp, st: (hp, nq - 1 - st, 0)),
                  pl.BlockSpec((1, 8, LANES), lambda hp, st: (hp * nq + nq - 1 - st, 0, 0)),
                  pl.BlockSpec((1, LANES), lambda hp, st: (0, hp + nhp)), ANY] + [ANY] * len(r_ops),
        out_specs=[pl.BlockSpec((4, t, LANES), lambda hp, st: (1, nq - 1 - st, hp)),
                   pl.BlockSpec((8, LANES), lambda hp, st: (0, hp))] + [ANY] * len(r_shapes),
        out_shape=[jax.ShapeDtypeStruct(buf.shape, buf.dtype), jax.ShapeDtypeStruct((8, sw), F32)] + r_shapes,
        input_output_aliases={9: 0, **r_aliases},
        scratch_shapes=[pltpu.VMEM((s, LANES), F32), pltpu.VMEM((s, LANES), F32), pltpu.VMEM((2, t, LANES), F32)]
        + r_scratch,
        compiler_params=_cp(2))(proj, proj, proj, proj, dy, ya, tl, walked, bg, buf, *r_ops)
    return outs[0], outs[1], list(outs[2:])


def _grad_w_in(h, dproj, name):
    s, d = h.shape
    ns, _, sw = dproj.shape

    def body(h_ref, b_ref, o_ref, ht_ref):
        @pl.when(pl.program_id(0) == 0)
        def _():
            ht_ref[...] = h_ref[...].T

        o_ref[...] = jnp.dot(ht_ref[...], b_ref[0], preferred_element_type=F32)

    return _pcall(body, name=name, grid=(ns,),
                  in_specs=[pl.BlockSpec((s, d), lambda j: (0, 0)), pl.BlockSpec((1, s, sw), lambda j: (j, 0, 0))],
                  out_specs=pl.BlockSpec((d, sw), lambda j: (0, j)),
                  out_shape=jax.ShapeDtypeStruct((d, ns * sw), F32),
                  scratch_shapes=[pltpu.VMEM((d, s), BF16)], compiler_params=_cp(1))(h, dproj)


def _inproj_bwd(dproj, w, layer, x, g, dx1, name, riders=()):
    ns, s, sw = dproj.shape
    d = x.shape[1]
    tm = _tile(s, 512)

    def body(*refs):
        own, riders_end = _riders_run(riders, refs, 5, 2, 0, pl.program_id(0), s // tm)
        dp_ref, w_ref, x_ref, g_ref, dx1_ref, dx_ref, dg_ref = own

        @pl.when(pl.program_id(0) == 0)
        def _():
            dg_ref[...] = jnp.zeros_like(dg_ref)

        dh = _dot_nt(dp_ref[0], w_ref[0, :, 0:sw])
        for k in range(1, ns):
            dh = dh + _dot_nt(dp_ref[k], w_ref[0, :, k * sw:(k + 1) * sw])
        dxr, dgr = _rms_bwd_rows(dh, x_ref[...], g_ref[...])
        dx_ref[...] = dx1_ref[...] + dxr
        dg_ref[...] += _colsum8(dgr)
        riders_end()

    row = lambda m: (m, 0)
    fix = lambda m: (0, 0)
    r_ops, r_shapes, r_scratch, r_aliases = _riders_plumb(riders, 5, 2)
    outs = _pcall(body, name=name, grid=(s // tm,),
                  in_specs=[pl.BlockSpec((ns, tm, sw), lambda m: (0, m, 0)),
                            pl.BlockSpec((1, d, ns * sw), lambda m: (layer, 0, 0)),
                            pl.BlockSpec((tm, d), row), pl.BlockSpec((1, d), fix), pl.BlockSpec((tm, d), row)]
                  + [ANY] * len(r_ops),
                  out_specs=[pl.BlockSpec((tm, d), row), pl.BlockSpec((8, d), fix)] + [ANY] * len(r_shapes),
                  out_shape=[jax.ShapeDtypeStruct((s, d), F32), jax.ShapeDtypeStruct((8, d), F32)] + r_shapes,
                  input_output_aliases=r_aliases, scratch_shapes=r_scratch,
                  compiler_params=_cp(1))(dproj, w, x, g, dx1, *r_ops)
    return outs[0], outs[1], list(outs[2:])


def _adamw(w, g, m, v, name):
    r, c = w.shape
    tr = _tile(r, 256)
    c1 = 1.0 - ADAM_B1 ** ADAM_STEP
    c2 = 1.0 - ADAM_B2 ** ADAM_STEP

    def body(w_ref, g_ref, m_ref, v_ref, d_ref, mo_ref, vo_ref):
        gv = g_ref[...]
        mn = ADAM_B1 * m_ref[...] + (1.0 - ADAM_B1) * gv
        vn = ADAM_B2 * v_ref[...] + (1.0 - ADAM_B2) * (gv * gv)
        d_ref[...] = -ADAM_LR * ((mn / c1) / (jnp.sqrt(vn / c2) + ADAM_EPS) + ADAM_WD * w_ref[...])
        mo_ref[...] = mn
        vo_ref[...] = vn

    t = pl.BlockSpec((tr, c), lambda i: (i, 0))
    return _pcall(body, name=name, grid=(r // tr,), in_specs=[t] * 4, out_specs=[t] * 3,
                  out_shape=[jax.ShapeDtypeStruct((r, c), F32)] * 3, compiler_params=_cp(1))(w, g, m, v)


def _add_half(grad, other, core, a, name):
    hr, hc = other.shape
    tr = _tile(hr, 256)
    nb = hr // tr

    def body(c_ref, g_ref, o_ref, out_ref, outb_ref):
        v = g_ref[...] + o_ref[...]
        out_ref[...] = v
        outb_ref[...] = v.astype(BF16)

    t = pl.BlockSpec((tr, hc), lambda i, c: (i, 0))
    own = (lambda i, c: (c[0] * nb + i, 0)) if HALF_AXES[a] == 0 else (lambda i, c: (i, c[0]))
    grid_spec = pltpu.PrefetchScalarGridSpec(
        num_scalar_prefetch=1, grid=(nb,), in_specs=[pl.BlockSpec((tr, hc), own), t], out_specs=[t, t])
    return _pcall(body, name=name, grid_spec=grid_spec,
                  out_shape=[jax.ShapeDtypeStruct((hr, hc), F32), jax.ShapeDtypeStruct((hr, hc), BF16)],
                  compiler_params=_cp(1))(core.reshape(1).astype(jnp.int32), grad, other)


def _sum_half(wide, parts, chip, core, layer, a, stack, name):
    _, sr, sc = parts.shape
    tr = _tile(sr, 256)
    nbs = sr // tr

    def body(k_ref, f_ref, p_ref, *rest):
        rest[-1][0] = ((f_ref[...] + p_ref[0].astype(F32)) + p_ref[1].astype(F32)) + p_ref[2].astype(F32)

    f_map = (lambda i, k: (i, k[0])) if SHARD_AXES[a] == 1 else (lambda i, k: (k[0] * nbs + i, 0))
    if HALF_AXES[a] == 0:
        shape, o_map = (DEPTH, 2 * sr, sc), (lambda i, k: (layer, k[1] * nbs + i, 0))
    else:
        shape, o_map = (DEPTH, sr, 2 * sc), (lambda i, k: (layer, i, k[1]))
    in_specs = [pl.BlockSpec((tr, sc), f_map), pl.BlockSpec((3, tr, sc), lambda i, k: (0, i, 0))]
    args = [wide, parts]
    aliases = {}
    if stack is not None:
        in_specs.append(ANY)
        args.append(stack)
        aliases = {3: 0}
    grid_spec = pltpu.PrefetchScalarGridSpec(
        num_scalar_prefetch=1, grid=(nbs,), in_specs=in_specs, out_specs=pl.BlockSpec((1, tr, sc), o_map))
    return _pcall(body, name=name, grid_spec=grid_spec, out_shape=jax.ShapeDtypeStruct(shape, F32),
                  input_output_aliases=aliases,
                  compiler_params=_cp(1))(jnp.stack([chip, core]).astype(jnp.int32), *args)


def _sum_slots(slots, name):
    n = slots.shape[0]

    def body(s_ref, o_ref):
        acc = s_ref[0]
        for i in range(1, n):
            acc = acc + s_ref[i]
        o_ref[...] = acc

    return _pcall(body, name=name, out_shape=jax.ShapeDtypeStruct(slots.shape[1:], F32))(slots)


def _place():
    return lax.axis_index("x"), lax.axis_index("y"), lax.axis_index("c")


def _shard_view(ref, axis, chip, size):
    if axis == 0:
        return ref.at[pl.ds(chip * size, size), :]
    return ref.at[:, pl.ds(chip * size, size)]


SHARD_AXES = (1, 0, 0, 1)
HALF_AXES = tuple(1 - ax for ax in SHARD_AXES)


class _Rider:
    def __init__(self, operands, out_shape, sems, phases, aliased=False):
        self.operands, self.out_shape, self.sems = list(operands), list(out_shape), list(sems)
        self.phases, self.aliased = phases, aliased


def _riders_plumb(riders, n_in, n_out):
    ops, out_shape, scratch, aliases = [], [], [], {}
    for r in riders:
        if r.aliased:
            for k in range(len(r.operands)):
                aliases[n_in + len(ops) + k] = n_out + len(out_shape) + k
        ops += r.operands
        out_shape += r.out_shape
        scratch += r.sems
    return ops, out_shape, scratch, aliases


def _riders_run(riders, refs, n_in, n_out, n_scr, step, nsteps):
    n_rin = sum(len(r.operands) for r in riders)
    n_rout = sum(len(r.out_shape) for r in riders)
    rin = refs[n_in:n_in + n_rin]
    o0 = n_in + n_rin
    rout = refs[o0 + n_out:o0 + n_out + n_rout]
    s0 = o0 + n_out + n_rout
    rsem = refs[s0 + n_scr:]
    own = list(refs[:n_in]) + list(refs[o0:o0 + n_out]) + list(refs[s0:s0 + n_scr])
    lasts = []
    for r in riders:
        ph = r.phases(rin[:len(r.operands)], rout[:len(r.out_shape)], rsem[:len(r.sems)])
        rin, rout, rsem = rin[len(r.operands):], rout[len(r.out_shape):], rsem[len(r.sems):]
        pl.when(step == 0)(ph[0])
        for mid in ph[1:-1]:
            pl.when(step == (3 * nsteps) // 4)(mid)
        lasts.append(ph[-1])

    def finish():
        for last in lasts:
            pl.when(step == nsteps - 1)(last)

    return own, finish


def _gather_phases(ins, outs, ssem, rsem, layer, which):
    n = len(ins)
    x, y, c = _place()
    me = 2 * x + y
    chips = [(1 - x, y), (x, 1 - y), (1 - x, 1 - y)]

    def piece(a, chip, half, of):
        ax = SHARD_AXES[which[a]]
        block = _shard_view(of[a].at[layer], ax, chip, of[a].shape[1 + ax] // 4)
        r = block.shape[0] // 2
        return block.at[pl.ds(half * r, r), :]

    def over_ici(a, j):
        cx, cy = chips[j]
        return pltpu.make_async_remote_copy(
            src_ref=piece(a, me, c, ins), dst_ref=piece(a, me, c, outs), send_sem=ssem.at[a, j],
            recv_sem=rsem.at[a, j], device_id=(cx, cy, c), device_id_type=MESH)

    def landed(a, j, half):
        cx, cy = chips[j]
        return piece(a, 2 * cx + cy, half, outs)

    def to_sibling(a, j):
        got = landed(a, j, c)
        return pltpu.make_async_remote_copy(
            src_ref=got, dst_ref=got, send_sem=ssem.at[a, 3 + j], recv_sem=rsem.at[a, 3 + j],
            device_id=(x, y, 1 - c), device_id_type=MESH)

    def wait_arrival(a, k, place):
        pltpu.make_async_remote_copy(src_ref=place, dst_ref=place, send_sem=ssem.at[a, k], recv_sem=rsem.at[a, k],
                                     device_id=(x, y, c), device_id_type=MESH).wait_recv()

    def start():
        for a in range(n):
            for j in range(3):
                over_ici(a, j).start()

    def pass_on():
        for a in range(n):
            for j in range(3):
                wait_arrival(a, j, landed(a, j, c))
                to_sibling(a, j).start()

    def finish():
        for a in range(n):
            for j in range(3):
                wait_arrival(a, 3 + j, landed(a, j, 1 - c))
        for a in range(n):
            for j in range(3):
                over_ici(a, j).wait_send()
                to_sibling(a, j).wait_send()

    return start, pass_on, finish


def _gather_rider(fulls, layer, which):
    n = len(fulls)
    return _Rider(fulls, [jax.ShapeDtypeStruct(f.shape, f.dtype) for f in fulls],
                  [pltpu.SemaphoreType.DMA((n, 6)), pltpu.SemaphoreType.DMA((n, 6))],
                  lambda ins, outs, sems: _gather_phases(ins, outs, sems[0], sems[1], layer, which), aliased=True)


def _ride_alone(rider, name):
    n = len(rider.operands)

    def body(*refs):
        for phase in rider.phases(refs[:n], refs[n:n + len(rider.out_shape)], refs[n + len(rider.out_shape):]):
            phase()

    return _pcall(body, name=name, in_specs=[ANY] * n, out_specs=[ANY] * len(rider.out_shape),
                  out_shape=rider.out_shape, scratch_shapes=rider.sems,
                  input_output_aliases={a: a for a in range(n)} if rider.aliased else {})(*rider.operands)


def _half_view(ref, a, half):
    n = ref.shape[HALF_AXES[a]] // 2
    if HALF_AXES[a] == 0:
        return ref.at[pl.ds(half * n, n), :]
    return ref.at[:, pl.ds(half * n, n)]


def _swap_rider(grads, which):
    n = len(grads)
    halves = []
    for g, w in zip(grads, which):
        sh = list(g.shape)
        sh[HALF_AXES[w]] //= 2
        halves.append(jax.ShapeDtypeStruct(tuple(sh), g.dtype))

    def phases(srcs, outs, sems):
        x, y, c = _place()

        def copy(a):
            return pltpu.make_async_remote_copy(
                src_ref=_half_view(srcs[a], which[a], 1 - c), dst_ref=outs[a], send_sem=sems[0].at[a],
                recv_sem=sems[1].at[a], device_id=(x, y, 1 - c), device_id_type=MESH)

        def start():
            for a in range(n):
                copy(a).start()

        def finish():
            for a in range(n):
                copy(a).wait()

        return start, finish

    return _Rider(grads, halves, [pltpu.SemaphoreType.DMA((n,)), pltpu.SemaphoreType.DMA((n,))], phases)


def _scatter_rider(sums, which):
    n = len(sums)
    shapes = []
    for f, w in zip(sums, which):
        sh = list(f.shape)
        sh[SHARD_AXES[w]] //= 4
        shapes.append(jax.ShapeDtypeStruct((3,) + tuple(sh), f.dtype))

    def phases(srcs, outs, sems):
        x, y, c = _place()
        chips = [(1 - x, y), (x, 1 - y), (1 - x, 1 - y)]

        def copy(a, j):
            cx, cy = chips[j]
            ax = SHARD_AXES[which[a]]
            src = _shard_view(srcs[a], ax, 2 * cx + cy, srcs[a].shape[ax] // 4)
            return pltpu.make_async_remote_copy(src_ref=src, dst_ref=outs[a].at[j], send_sem=sems[0].at[a, j],
                                                recv_sem=sems[1].at[a, j], device_id=(cx, cy, c), device_id_type=MESH)

        def start():
            for a in range(n):
                for j in range(3):
                    copy(a, j).start()

        def finish():
            for a in range(n):
                for j in range(3):
                    copy(a, j).wait()

        return start, finish

    return _Rider(sums, shapes, [pltpu.SemaphoreType.DMA((n, 3)), pltpu.SemaphoreType.DMA((n, 3))], phases)


def _pair_halves(stacks):
    n = len(stacks)

    def body(*refs):
        ins, outs = refs[:n], refs[n:2 * n]
        ssem, rsem = refs[2 * n:]
        x, y, c = _place()
        cps = [pltpu.make_async_remote_copy(
            src_ref=_half_view(ins[a].at[l], a, c), dst_ref=_half_view(outs[a].at[l], a, c), send_sem=ssem.at[a, l],
            recv_sem=rsem.at[a, l], device_id=(x, y, 1 - c), device_id_type=MESH)
            for a in range(n) for l in range(DEPTH)]
        for cp in cps:
            cp.start()
        for a in range(n):
            for l in range(DEPTH):
                got = _half_view(outs[a].at[l], a, 1 - c)
                pltpu.make_async_remote_copy(src_ref=got, dst_ref=got, send_sem=ssem.at[a, l], recv_sem=rsem.at[a, l],
                                             device_id=(x, y, 1 - c), device_id_type=MESH).wait_recv()
        for cp in cps:
            cp.wait_send()

    return _pcall(body, name="pair_halves", in_specs=[ANY] * n, out_specs=[ANY] * n,
                  out_shape=[jax.ShapeDtypeStruct(st.shape, st.dtype) for st in stacks],
                  input_output_aliases={a: a for a in range(n)},
                  scratch_shapes=[pltpu.SemaphoreType.DMA((n, DEPTH)), pltpu.SemaphoreType.DMA((n, DEPTH))])(*stacks)


class _GradReduce:
    def __init__(self, chip, core):
        self.chip, self.core = chip, core
        self.stacks = [None] * len(SHARD_AXES)

    def add(self, layer, grads, which, got):
        return [(layer, w) + tuple(_add_half(g, o, self.core, w, f"add_half_{layer}_{w}"))
                for g, o, w in zip(grads, got, which)]

    def finish(self, sums, partials):
        for (layer, w, wide, _), pr in zip(sums, partials):
            self.stacks[w] = _sum_half(wide, pr, self.chip, self.core, layer, w, self.stacks[w], f"sum_half_{layer}_{w}")

    def result(self):
        return _pair_halves(self.stacks)


def _exchange_small(pack, name):
    nd = 8

    def body(p_ref, o_ref, ssem, rsem):
        x, y, c = _place()
        me = 4 * x + 2 * y + c
        o_ref[me] = p_ref[...]
        cps = []
        for j in range(1, nd):
            px, py, pc = x ^ (j >> 2), y ^ ((j >> 1) & 1), c ^ (j & 1)
            cps.append(pltpu.make_async_remote_copy(
                src_ref=p_ref, dst_ref=o_ref.at[me], send_sem=ssem.at[j - 1], recv_sem=rsem.at[j - 1],
                device_id=(px, py, pc), device_id_type=MESH))
        for cp in cps:
            cp.start()
        for j in range(1, nd):
            peer = me ^ j
            got = o_ref.at[peer]
            pltpu.make_async_remote_copy(src_ref=got, dst_ref=got, send_sem=ssem.at[j - 1], recv_sem=rsem.at[j - 1],
                                         device_id=(x, y, c), device_id_type=MESH).wait_recv()
        for cp in cps:
            cp.wait_send()

    vm = pl.BlockSpec(memory_space=pltpu.VMEM)
    return _pcall(body, name=name, in_specs=[vm], out_specs=vm,
                  out_shape=jax.ShapeDtypeStruct((nd,) + pack.shape, pack.dtype),
                  scratch_shapes=[pltpu.SemaphoreType.DMA((nd - 1,)), pltpu.SemaphoreType.DMA((nd - 1,))])(pack)


def _row(v):
    return v.reshape(1, -1)


def _local_step(x, p, tgt, norm_g, conv_w, conv_b, branch_g, ple_norm_g, b_pg, final_g, w_in, w_out, w_pg, w_pe,
                gather=False, reduce=None):
    saved = []
    xl = x
    for l in range(DEPTH):
        riders = [_gather_rider([w_out, w_pg, w_pe], 0, [1, 2, 3])] if gather and l == 0 else []
        h, proj, got = _inproj(xl, _row(norm_g[l]), w_in, l, f"inproj_{l}", riders)
        if riders:
            w_out, w_pg, w_pe = got
        riders = [_gather_rider([w_in, w_out, w_pg, w_pe], l + 1, [0, 1, 2, 3])] if gather and l + 1 < DEPTH else []
        ya, tl, walked, got = _attn_fwd(proj, f"attn_fwd_{l}", riders)
        if riders:
            w_in, w_out, w_pg, w_pe = got
        y = _mix_fwd(proj, ya, conv_w[l], _row(conv_b[l]), _row(branch_g[l]), f"mix_fwd_{l}")
        x1, hn = _outproj(y, w_out, l, xl, _row(ple_norm_g[l]), f"outproj_{l}")
        x2, gate, e = _ple_fwd(hn, w_pg, _row(b_pg[l]), p, w_pe, l, x1, f"ple_fwd_{l}")
        saved.append((xl, h, proj, ya, tl, walked, y, x1, hn, gate, e))
        xl = x2

    sq, dx, d_final = _loss_head(xl, tgt, _row(final_g), "loss_head")

    big = [None] * DEPTH
    carried = []
    small = {k: [None] * DEPTH for k in ("norm_g", "conv_w", "conv_b", "branch_g", "ple_norm_g", "b_pg")}
    for l in reversed(range(DEPTH)):
        xl, h, proj, ya, tl, walked, y, x1, hn, gate, e = saved[l]
        du, de, dx1, dy, db_pg, d_ple, _ = _ple_bwd(dx, gate, e, x1, w_pg, _row(ple_norm_g[l]), w_out, l,
                                                    f"ple_bwd_{l}")
        sums = carried
        g_pg = _mm_tn(hn, du, f"grad_w_pg_{l}")
        g_pe = _mm_tn(p, de, f"grad_w_pe_{l}", a_layer=l)
        g_out = _mm_tn(y, dx1, f"grad_w_out_{l}")
        others = [g_out, g_pg, g_pe]
        riders = [_swap_rider(others, [1, 2, 3])] if reduce is not None else []
        dpc, d_cw, d_cb, d_bg_c, got = _convmix_bwd(dy, proj, conv_w[l], _row(conv_b[l]), _row(branch_g[l]),
                                                    f"convmix_bwd_{l}", riders)
        if reduce is not None:
            sums += reduce.add(l, others, [1, 2, 3], got)
        riders = [_scatter_rider([sm[3] for sm in sums], [sm[1] for sm in sums])] if sums else []
        dproj, d_bg_a, got = _attn_bwd(proj, dy, ya, tl, walked, _row(branch_g[l]), dpc, f"attn_bwd_{l}", riders)
        if sums:
            reduce.finish(sums, got)
        g_in = _grad_w_in(h, dproj, f"grad_w_in_{l}")
        riders = [_swap_rider([g_in], [0])] if reduce is not None else []
        dx, d_norm, got = _inproj_bwd(dproj, w_in, l, xl, _row(norm_g[l]), dx1, f"inproj_bwd_{l}", riders)
        big[l] = (g_in, g_out, g_pg, g_pe)
        carried = reduce.add(l, [g_in], [0], got) if reduce is not None else []
        small["norm_g"][l] = jnp.sum(d_norm, axis=0)
        small["conv_w"][l] = jnp.sum(d_cw, axis=1)
        small["conv_b"][l] = jnp.sum(d_cb, axis=0)
        small["branch_g"][l] = jnp.concatenate([jnp.sum(d_bg_c, axis=0), jnp.sum(d_bg_a, axis=0)])
        small["ple_norm_g"][l] = jnp.sum(d_ple, axis=0)
        small["b_pg"][l] = jnp.sum(db_pg, axis=0)
    if carried:
        reduce.finish(carried, _ride_alone(_scatter_rider([sm[3] for sm in carried], [0]), "scatter_shards_last"))
    small = {k: jnp.stack(v) for k, v in small.items()}
    small["final_g"] = jnp.sum(d_final, axis=0)
    return sq[0, 0], dx, big, small


SMALL_ORDER = ("norm_g", "conv_w", "conv_b", "branch_g", "ple_norm_g", "b_pg", "final_g")


def _pack(parts, width):
    flat = jnp.concatenate([v.reshape(-1) for v in parts])
    rows = -(-flat.shape[0] // width)
    rows = -(-rows // 8) * 8
    return jnp.pad(flat, (0, rows * width - flat.shape[0])).reshape(rows, width)


def _unpack(packed, like):
    flat = packed.reshape(-1)
    out, off = [], 0
    for v in like:
        out.append(flat[off:off + v.size].reshape(v.shape))
        off += v.size
    return out


def kernel(x, p, norm_g, w_in, conv_w, conv_b, branch_g, w_out, ple_norm_g, w_pg, b_pg, w_pe, final_g, loss_target, m_norm_g, m_w_in, m_conv_w, m_conv_b, m_branch_g, m_w_out, m_ple_norm_g, m_w_pg, m_b_pg, m_w_pe, m_final_g, v_norm_g, v_w_in, v_conv_w, v_conv_b, v_branch_g, v_w_out, v_ple_norm_g, v_w_pg, v_b_pg, v_w_pe, v_final_g):
    ix, iy, ic = _place()
    chip = 2 * ix + iy
    d = x.shape[-1]

    big_w = (w_in, w_out, w_pg, w_pe)
    own = [_cast_into_full(w, chip, ax, f"cast_{i}") for i, (w, ax) in enumerate(zip(big_w, SHARD_AXES))]
    full_in, = _ride_alone(_gather_rider([own[0]], 0, [0]), "gather_w_in_0")
    full_out, full_pg, full_pe = own[1:]
    cw_shard = conv_w.shape[-1]
    cw_slots = _exchange_small(_pack([conv_w], LANES), "exchange_conv_w")
    conv_full = jnp.concatenate([_unpack(cw_slots[2 * k], [conv_w])[0] for k in range(4)], axis=-1)

    reduce = _GradReduce(chip, ic)
    sq, dx, _, small_g = _local_step(
        x[0], p[:, 0], loss_target[0], norm_g, conv_full, conv_b, branch_g, ple_norm_g, b_pg, final_g,
        full_in, full_out, full_pg, full_pe, gather=True, reduce=reduce)
    g_big = reduce.result()

    parts = [small_g[k] for k in SMALL_ORDER] + [sq.reshape(1)]
    slots = _exchange_small(_pack(parts, d), "exchange_small_grads")
    total = _unpack(_sum_slots(slots, "sum_small"), parts)
    g_small = dict(zip(SMALL_ORDER, total[:-1]))
    loss = 0.5 * total[-1][0] / d
    g_small["conv_w"] = lax.dynamic_slice_in_dim(g_small["conv_w"], chip * cw_shard, cw_shard, axis=2)

    grads = dict(g_small)
    grads.update(w_in=g_big[0], w_out=g_big[1], w_pg=g_big[2], w_pe=g_big[3])
    weights = dict(norm_g=norm_g, w_in=w_in, conv_w=conv_w, conv_b=conv_b, branch_g=branch_g, w_out=w_out,
                   ple_norm_g=ple_norm_g, w_pg=w_pg, b_pg=b_pg, w_pe=w_pe, final_g=final_g)
    ms = dict(norm_g=m_norm_g, w_in=m_w_in, conv_w=m_conv_w, conv_b=m_conv_b, branch_g=m_branch_g, w_out=m_w_out,
              ple_norm_g=m_ple_norm_g, w_pg=m_w_pg, b_pg=m_b_pg, w_pe=m_w_pe, final_g=m_final_g)
    vs = dict(norm_g=v_norm_g, w_in=v_w_in, conv_w=v_conv_w, conv_b=v_conv_b, branch_g=v_branch_g, w_out=v_w_out,
              ple_norm_g=v_ple_norm_g, w_pg=v_w_pg, b_pg=v_b_pg, w_pe=v_w_pe, final_g=v_final_g)
    names = ("norm_g", "w_in", "conv_w", "conv_b", "branch_g", "w_out", "ple_norm_g", "w_pg", "b_pg", "w_pe", "final_g")
    delta, new_m, new_v = {}, {}, {}
    for k in ("w_in", "w_out", "w_pg", "w_pe"):
        shp = weights[k].shape
        two = lambda a: a.reshape(-1, shp[-1])
        dl, mn, vn = _adamw(two(weights[k]), two(grads[k]), two(ms[k]), two(vs[k]), f"adamw_{k}")
        delta[k], new_m[k], new_v[k] = dl.reshape(shp), mn.reshape(shp), vn.reshape(shp)
        grads[k] = grads[k].reshape(shp)
    like = [weights[k] for k in SMALL_ORDER]
    packs = [_pack([src[k] for k in SMALL_ORDER], d) for src in (weights, grads, ms, vs)]
    outs = _adamw(*packs, "adamw_small")
    for res, o in zip((delta, new_m, new_v), outs):
        res.update(dict(zip(SMALL_ORDER, _unpack(o, like))))

    return (loss, dx[None], *[grads[k] for k in names], *[delta[k] for k in names],
            *[new_m[k] for k in names], *[new_v[k] for k in names])
```

```python
import math

import jax
import jax.numpy as jnp
from jax import lax
from jax.experimental import pallas as pl
from jax.experimental.pallas import tpu as pltpu

F32 = jnp.float32
BF16 = jnp.bfloat16
EPS = 1e-6
HEAD = 64
LANES = 128
ATT_TK = 256
ATT_TQ = 512
ATT_ROWS = 128
ALIVE_LOG = -105.0
DEPTH = 2
VMEM_LIMIT = 56 * 1024 * 1024
MESH = pl.DeviceIdType.MESH
ANY = pl.BlockSpec(memory_space=pl.ANY)

ADAM_LR = 0.001
ADAM_B1 = 0.9
ADAM_B2 = 0.999
ADAM_EPS = 1e-08
ADAM_WD = 0.01
ADAM_STEP = 10


def _pcall(body, **kw):
    return pl.pallas_call(body, **kw)


def _cp(n_axes):
    return pltpu.CompilerParams(dimension_semantics=("arbitrary",) * n_axes, vmem_limit_bytes=VMEM_LIMIT)


def _tile(n, pref):
    return pref if n % pref == 0 else n


def _split_dot(a, b, passes):
    out = None
    rem = a
    for _ in range(passes):
        hi = rem.astype(BF16)
        t = jnp.dot(hi, b, preferred_element_type=F32)
        out = t if out is None else out + t
        rem = rem - hi.astype(F32)
    return out


def _group_mat():
    r = lax.broadcasted_iota(jnp.int32, (LANES, LANES), 0) // HEAD
    c = lax.broadcasted_iota(jnp.int32, (LANES, LANES), 1) // HEAD
    return jnp.where(r == c, 1.0 / HEAD, 0.0).astype(BF16)


def _group_mean(v, gm):
    return _split_dot(v, gm, 2)


def _sigmoid(z):
    return 1.0 / (1.0 + jnp.exp(-z))


def _dot_nt(a, b):
    return lax.dot_general(a, b, (((1,), (1,)), ((), ())), preferred_element_type=F32)


def _dot_tn(a, b):
    return lax.dot_general(a, b, (((0,), (0,)), ((), ())), preferred_element_type=F32)


def _cast_into_full(w, chip, axis, name):
    _, r, c = w.shape
    tr = _tile(r, 256)
    nb = r // tr
    full = (DEPTH, 4 * r, c) if axis == 0 else (DEPTH, r, 4 * c)

    def body(k_ref, w_ref, o_ref):
        o_ref[...] = w_ref[...].astype(BF16)

    out_map = (lambda l, i, k: (l, k[0] * nb + i, 0)) if axis == 0 else (lambda l, i, k: (l, i, k[0]))
    grid_spec = pltpu.PrefetchScalarGridSpec(
        num_scalar_prefetch=1, grid=(DEPTH, nb),
        in_specs=[pl.BlockSpec((1, tr, c), lambda l, i, k: (l, i, 0))],
        out_specs=pl.BlockSpec((1, tr, c), out_map))
    return _pcall(body, name=name, grid_spec=grid_spec, out_shape=jax.ShapeDtypeStruct(full, BF16),
                  compiler_params=_cp(2))(chip.reshape(1).astype(jnp.int32), w)


def _rms_bwd_rows(dh, xv, g):
    r = lax.rsqrt(jnp.mean(xv * xv, axis=-1, keepdims=True) + EPS)
    xn = xv * r
    dxn = dh * g
    dx = r * (dxn - xn * jnp.mean(dxn * xn, axis=-1, keepdims=True))
    return dx, dh * xn


def _colsum8(v):
    tm, d = v.shape
    return jnp.sum(v.reshape(tm // 8, 8, d), axis=0)


def _inproj(x, g, w, layer, name, riders=()):
    s, d = x.shape
    n = w.shape[2]
    sw = d // 2
    ns = n // sw
    tm = _tile(s, 512)

    def body(*refs):
        own, riders_end = _riders_run(riders, refs, 3, 2, 0, pl.program_id(0), s // tm)
        x_ref, g_ref, w_ref, h_ref, o_ref = own
        xv = x_ref[...]
        r = lax.rsqrt(jnp.mean(xv * xv, axis=-1, keepdims=True) + EPS)
        h = (xv * r * g_ref[...]).astype(BF16)
        h_ref[...] = h
        for k in range(ns):
            o_ref[k] = jnp.dot(h, w_ref[0, :, k * sw:(k + 1) * sw], preferred_element_type=F32).astype(BF16)
        riders_end()

    r_ops, r_shapes, r_scratch, r_aliases = _riders_plumb(riders, 3, 2)
    outs = _pcall(body, name=name, grid=(s // tm,),
                  in_specs=[pl.BlockSpec((tm, d), lambda m: (m, 0)), pl.BlockSpec((1, d), lambda m: (0, 0)),
                            pl.BlockSpec((1, d, n), lambda m: (layer, 0, 0))] + [ANY] * len(r_ops),
                  out_specs=[pl.BlockSpec((tm, d), lambda m: (m, 0)), pl.BlockSpec((ns, tm, sw), lambda m: (0, m, 0))]
                  + [ANY] * len(r_shapes),
                  out_shape=[jax.ShapeDtypeStruct((s, d), BF16), jax.ShapeDtypeStruct((ns, s, sw), BF16)] + r_shapes,
                  input_output_aliases=r_aliases, scratch_shapes=r_scratch,
                  compiler_params=_cp(1))(x, g, w, *r_ops)
    return outs[0], outs[1], list(outs[2:])


def _softplus_parts(z):
    lm = jnp.minimum(-z, 0.0) - jnp.log(1.0 + jnp.exp(-jnp.abs(z)))
    return lm, lm + z


def _attn_tiles(s):
    tk = _tile(s, ATT_TK)
    tq = _tile(s, ATT_TQ)
    return tk, tq, tq // tk, min(ATT_ROWS, tq)


def _diag_work(chains, d, rows, tk):
    work = []
    for n, (_, r0) in enumerate(chains):
        if r0 + rows - 1 <= d * tk:
            continue
        kw = tk // 2 if (tk % 2 == 0 and r0 + rows <= d * tk + tk // 2) else tk
        if r0 >= d * tk + kw:
            mask = None
        else:
            row = lax.broadcasted_iota(jnp.int32, (rows, kw), 0)
            col = lax.broadcasted_iota(jnp.int32, (rows, kw), 1)
            mask = col + d * tk < row + r0
        work.append((n, kw, mask))
    return work


def _walk_work(chains, tk, below=None):
    return [(n, tk, None) for n, (_, r0) in enumerate(chains) if below is None or r0 < below]


def _any_alive(rsums):
    m = rsums[0]
    for r in rsums[1:]:
        m = jnp.maximum(m, r)
    return jnp.max((m > ALIVE_LOG).astype(jnp.int32))


def _attn_fwd(proj, name, riders=()):
    _, s, sw = proj.shape
    nhp = sw // LANES
    tk, tq, nd, rows = _attn_tiles(s)
    nq = s // tq
    scale = 1.0 / math.sqrt(HEAD)

    def body(*refs):
        i = pl.program_id(1)
        own, riders_end = _riders_run(riders, refs, 3, 3, 1, pl.program_id(0) * nq + i, nhp * nq)
        q_ref, k_ref, v_ref, o_ref, tl_ref, nw_ref, acc_ref = own
        tri = (lax.broadcasted_iota(jnp.int32, (tk, tk), 0) >
               lax.broadcasted_iota(jnp.int32, (tk, tk), 1)).astype(BF16)
        lane = lax.broadcasted_iota(jnp.int32, (tq, LANES), 1)
        q = q_ref[0] * jnp.asarray(scale, BF16)
        qms = [jnp.where((lane // HEAD) == h, q, jnp.zeros_like(q)) for h in range(2)]
        acc_ref[...] = jnp.zeros_like(acc_ref)
        chains = [(h, r0) for h in range(2) for r0 in range(0, tq, rows)]
        qparts = [qms[h][r0:r0 + rows] for h, r0 in chains]

        def tile(j, rsums, work):
            k0 = pl.multiple_of(j * tk, tk)
            kj = k_ref[0, pl.ds(k0, tk), :]
            vj = v_ref[0, pl.ds(k0, tk), :]
            zs = [_dot_nt(qparts[n], kj[:kw]) for n, kw, _ in work]
            lms, lss, css = [], [], []
            for z, (n, kw, mask) in zip(zs, work):
                lm, ls = _softplus_parts(z)
                if mask is not None:
                    lm = jnp.where(mask, lm, 0.0)
                lms.append(lm)
                lss.append(ls)
                css.append(_split_dot(lm, tri[:kw, :kw], 2))
            out = list(rsums)
            for lm, ls, cs, (n, kw, mask) in zip(lms, lss, css, work):
                h, r0 = chains[n]
                a = jnp.exp(ls + (rsums[n] + cs))
                if mask is not None:
                    a = jnp.where(mask, a, 0.0)
                acc_ref[h, r0:r0 + rows, :] += jnp.dot(a.astype(BF16), vj[:kw], preferred_element_type=F32)
                out[n] = rsums[n] + jnp.sum(lm, axis=1, keepdims=True)
            return tuple(out)

        rsums = (jnp.zeros((rows, 1), F32),) * len(chains)
        for d in reversed(range(nd)):
            rsums = tile(i * nd + d, rsums, _diag_work(chains, d, rows, tk))

        upper = [rs for rs, (_, r0) in zip(rsums, chains) if r0 >= tk]
        lower = [rs for rs, (_, r0) in zip(rsums, chains) if r0 < tk]
        if upper:
            alone = (i > 0) & (_any_alive(upper) == 0) & (_any_alive(lower) > 0)
            rsums = lax.cond(alone, lambda rs: tile(i * nd - 1, rs, _walk_work(chains, tk, below=tk)),
                             lambda rs: rs, rsums)
            alone = alone.astype(jnp.int32)
        else:
            alone = jnp.int32(0)

        def walk(c):
            jj, rs, _ = c
            rs = tile(i * nd - 1 - jj, rs, _walk_work(chains, tk))
            return jj + 1, rs, _any_alive(rs)

        last, rsums, _ = lax.while_loop(lambda c: (c[0] < i * nd) & (c[2] > 0), walk,
                                        (alone, rsums, _any_alive(rsums)))
        for n, (h, r0) in enumerate(chains):
            tl_ref[h, r0:r0 + rows, :] = rsums[n]
        nw_ref[0] = (jnp.zeros((8, LANES), jnp.int32) + (2 * (last - alone) + alone)).astype(F32)
        o_ref[...] = jnp.where(lane < HEAD, acc_ref[0], acc_ref[1]).astype(BF16)
        riders_end()

    r_ops, r_shapes, r_scratch, r_aliases = _riders_plumb(riders, 3, 3)
    outs = _pcall(
        body, name=name, grid=(nhp, nq),
        in_specs=[pl.BlockSpec((1, tq, LANES), lambda hp, i: (4, i, hp)),
                  pl.BlockSpec((1, s, LANES), lambda hp, i: (5, 0, hp)),
                  pl.BlockSpec((1, s, LANES), lambda hp, i: (6, 0, hp))] + [ANY] * len(r_ops),
        out_specs=[pl.BlockSpec((tq, LANES), lambda hp, i: (i, hp)),
                   pl.BlockSpec((2, tq, 1), lambda hp, i: (hp, i, 0)),
                   pl.BlockSpec((1, 8, LANES), lambda hp, i: (hp * nq + i, 0, 0))] + [ANY] * len(r_shapes),
        out_shape=[jax.ShapeDtypeStruct((s, sw), BF16), jax.ShapeDtypeStruct((2 * nhp, s, 1), F32),
                   jax.ShapeDtypeStruct((nhp * nq, 8, LANES), F32)] + r_shapes,
        input_output_aliases=r_aliases,
        scratch_shapes=[pltpu.VMEM((2, tq, LANES), F32)] + r_scratch,
        compiler_params=_cp(2))(proj, proj, proj, *r_ops)
    return outs[0], outs[1], outs[2], list(outs[3:])


def _conv_rows(cc_ref, ch_ref, w_ref, b_ref, r, tc):
    r0 = pl.multiple_of(r * tc, tc)
    u = cc_ref[0, pl.ds(r0, tc), :].astype(F32) * ch_ref[0, pl.ds(r0, tc), :].astype(F32)
    p0 = pl.multiple_of(jnp.maximum(r0 - 16, 0), 16)
    up = cc_ref[0, pl.ds(p0, 16), :].astype(F32) * ch_ref[0, pl.ds(p0, 16), :].astype(F32)
    up = up * (r > 0).astype(F32)
    prev1 = up[15:16, :]
    prev2 = up[14:15, :]
    rid = lax.broadcasted_iota(jnp.int32, u.shape, 0)
    s1 = jnp.where(rid == 0, prev1, pltpu.roll(u, 1, axis=0))
    s2 = jnp.where(rid == 0, prev2, jnp.where(rid == 1, prev1, pltpu.roll(u, 2, axis=0)))
    cv = b_ref[...] + s2 * w_ref[0:1, :] + s1 * w_ref[1:2, :] + u * w_ref[2:3, :]
    return r0, u, s1, s2, cv


def _mix_fwd(proj, ya, conv_w, conv_b, bg, name, riders=()):
    _, s, sw = proj.shape
    nh = sw // LANES
    tc = _tile(s, 256)

    def body(*refs):
        c = pl.program_id(0)
        own, riders_end = _riders_run(riders, refs, 9, 1, 0, c, 2 * nh)
        cb_ref, cc_ref, ch_ref, cz_ref, ya_ref, az_ref, w_ref, b_ref, g_ref, y_ref = own
        gm = _group_mat()

        def finish(r0, yv, zg):
            n = yv * lax.rsqrt(_group_mean(yv * yv, gm) + EPS)
            y_ref[pl.ds(r0, tc), :] = (n * g_ref[...] * (zg * _sigmoid(zg))).astype(BF16)

        @pl.when(c < nh)
        def _():
            def step(r, carry):
                r0, _, _, _, cv = _conv_rows(cc_ref, ch_ref, w_ref, b_ref, r, tc)
                yc = cb_ref[0, pl.ds(r0, tc), :].astype(F32) * cv
                finish(r0, yc, cz_ref[0, pl.ds(r0, tc), :].astype(F32))
                return carry
            lax.fori_loop(0, s // tc, step, 0)

        @pl.when(c >= nh)
        def _():
            def step(r, carry):
                r0 = pl.multiple_of(r * tc, tc)
                finish(r0, ya_ref[pl.ds(r0, tc), :].astype(F32), az_ref[0, pl.ds(r0, tc), :].astype(F32))
                return carry
            lax.fori_loop(0, s // tc, step, 0)

        riders_end()

    def sec(k):
        return pl.BlockSpec((1, s, LANES), lambda c: (k, 0, jnp.minimum(c, nh - 1)))

    r_ops, r_shapes, r_scratch, r_aliases = _riders_plumb(riders, 9, 1)
    outs = _pcall(
        body, name=name, grid=(2 * nh,),
        in_specs=[sec(0), sec(1), sec(2), sec(3),
                  pl.BlockSpec((s, LANES), lambda c: (0, jnp.maximum(c - nh, 0))),
                  pl.BlockSpec((1, s, LANES), lambda c: (7, 0, jnp.maximum(c - nh, 0))),
                  pl.BlockSpec((3, LANES), lambda c: (0, jnp.minimum(c, nh - 1))),
                  pl.BlockSpec((1, LANES), lambda c: (0, jnp.minimum(c, nh - 1))),
                  pl.BlockSpec((1, LANES), lambda c: (0, c))] + [ANY] * len(r_ops),
        out_specs=[pl.BlockSpec((s, LANES), lambda c: (0, c))] + [ANY] * len(r_shapes),
        out_shape=[jax.ShapeDtypeStruct((s, 2 * sw), BF16)] + r_shapes,
        input_output_aliases=r_aliases, scratch_shapes=r_scratch, compiler_params=_cp(1),
    )(proj, proj, proj, proj, ya, proj, conv_w, conv_b, bg, *r_ops)
    return outs[0], list(outs[1:])


def _outproj(y, w, layer, x, g, name):
    s, d = x.shape
    tm = _tile(s, 512)

    def body(y_ref, w_ref, x_ref, g_ref, x1_ref, hn_ref):
        x1 = x_ref[...] + jnp.dot(y_ref[...], w_ref[0], preferred_element_type=F32)
        x1_ref[...] = x1
        r = lax.rsqrt(jnp.mean(x1 * x1, axis=-1, keepdims=True) + EPS)
        hn_ref[...] = (x1 * r * g_ref[...]).astype(BF16)

    row = lambda m: (m, 0)
    fix = lambda m: (0, 0)
    return _pcall(body, name=name, grid=(s // tm,),
                  in_specs=[pl.BlockSpec((tm, d), row), pl.BlockSpec((1, d, d), lambda m: (layer, 0, 0)),
                            pl.BlockSpec((tm, d), row), pl.BlockSpec((1, d), fix)],
                  out_specs=[pl.BlockSpec((tm, d), row), pl.BlockSpec((tm, d), row)],
                  out_shape=[jax.ShapeDtypeStruct((s, d), F32), jax.ShapeDtypeStruct((s, d), BF16)],
                  compiler_params=_cp(1))(y, w, x, g)


def _ple_fwd(hn, w_pg, b_pg, p, w_pe, layer, x1, name):
    s, d = x1.shape
    pd = p.shape[2]
    tm = _tile(s, 512)

    def body(hn_ref, wg_ref, b_ref, p_ref, we_ref, x1_ref, x2_ref, gate_ref, e_ref):
        gate = _sigmoid(jnp.dot(hn_ref[...], wg_ref[0], preferred_element_type=F32) + b_ref[...])
        e = jnp.dot(p_ref[0].astype(BF16), we_ref[0], preferred_element_type=F32)
        x2_ref[...] = x1_ref[...] + gate * e
        gate_ref[...] = gate.astype(BF16)
        e_ref[...] = e.astype(BF16)

    row = lambda m: (m, 0)
    fix = lambda m: (0, 0)
    return _pcall(body, name=name, grid=(s // tm,),
                  in_specs=[pl.BlockSpec((tm, d), row), pl.BlockSpec((1, d, d), lambda m: (layer, 0, 0)),
                            pl.BlockSpec((1, d), fix), pl.BlockSpec((1, tm, pd), lambda m: (layer, m, 0)),
                            pl.BlockSpec((1, pd, d), lambda m: (layer, 0, 0)), pl.BlockSpec((tm, d), row)],
                  out_specs=[pl.BlockSpec((tm, d), row)] * 3,
                  out_shape=[jax.ShapeDtypeStruct((s, d), F32), jax.ShapeDtypeStruct((s, d), BF16),
                             jax.ShapeDtypeStruct((s, d), BF16)],
                  compiler_params=_cp(1))(hn, w_pg, b_pg, p, w_pe, x1)


def _loss_head(x, tgt, g, name):
    s, d = x.shape
    tm = _tile(s, 512)

    def body(x_ref, t_ref, g_ref, l_ref, dx_ref, dg_ref):
        m = pl.program_id(0)

        @pl.when(m == 0)
        def _():
            l_ref[...] = jnp.zeros_like(l_ref)
            dg_ref[...] = jnp.zeros_like(dg_ref)

        xv = x_ref[...]
        gv = g_ref[...]
        r = lax.rsqrt(jnp.mean(xv * xv, axis=-1, keepdims=True) + EPS)
        xn = xv * r
        err = xn * gv - t_ref[...]
        l_ref[...] += jnp.sum(err * err)
        dy = err * (1.0 / d)
        dxn = dy * gv
        dx_ref[...] = r * (dxn - xn * jnp.mean(dxn * xn, axis=-1, keepdims=True))
        dg_ref[...] += _colsum8(dy * xn)

    row = lambda m: (m, 0)
    fix = lambda m: (0, 0)
    return _pcall(body, name=name, grid=(s // tm,),
                  in_specs=[pl.BlockSpec((tm, d), row), pl.BlockSpec((tm, d), row), pl.BlockSpec((1, d), fix)],
                  out_specs=[pl.BlockSpec((8, LANES), fix), pl.BlockSpec((tm, d), row), pl.BlockSpec((8, d), fix)],
                  out_shape=[jax.ShapeDtypeStruct((8, LANES), F32), jax.ShapeDtypeStruct((s, d), F32),
                             jax.ShapeDtypeStruct((8, d), F32)],
                  compiler_params=_cp(1))(x, tgt, g)


def _ple_bwd(dx2, gate, e, x1, w_pg, g_ple, w_out, layer, name, riders=()):
    s, d = dx2.shape
    tm = _tile(s, 512)

    def body(*refs):
        m = pl.program_id(0)
        own, riders_end = _riders_run(riders, refs, 7, 6, 0, m, s // tm)
        (dx2_ref, gate_ref, e_ref, x1_ref, wg_ref, g_ref, wo_ref,
         du_ref, de_ref, dx1_ref, dy_ref, db_ref, dg_ref) = own

        @pl.when(m == 0)
        def _():
            db_ref[...] = jnp.zeros_like(db_ref)
            dg_ref[...] = jnp.zeros_like(dg_ref)

        dx2v = dx2_ref[...]
        gate = gate_ref[...].astype(F32)
        du = dx2v * e_ref[...].astype(F32) * gate * (1.0 - gate)
        de_ref[...] = (dx2v * gate).astype(BF16)
        dub = du.astype(BF16)
        du_ref[...] = dub
        db_ref[...] += _colsum8(du)
        dhn = _dot_nt(dub, wg_ref[0])
        dxr, dgr = _rms_bwd_rows(dhn, x1_ref[...], g_ref[...])
        dx1 = dx2v + dxr
        dx1_ref[...] = dx1
        dg_ref[...] += _colsum8(dgr)
        dy_ref[...] = _dot_nt(dx1.astype(BF16), wo_ref[0]).astype(BF16)
        riders_end()

    row = lambda m: (m, 0)
    fix = lambda m: (0, 0)
    t = pl.BlockSpec((tm, d), row)
    r_ops, r_shapes, r_scratch, r_aliases = _riders_plumb(riders, 7, 6)
    outs = _pcall(body, name=name, grid=(s // tm,),
                  in_specs=[t, t, t, t, pl.BlockSpec((1, d, d), lambda m: (layer, 0, 0)), pl.BlockSpec((1, d), fix),
                            pl.BlockSpec((1, d, d), lambda m: (layer, 0, 0))] + [ANY] * len(r_ops),
                  out_specs=[t, t, t, t, pl.BlockSpec((8, d), fix), pl.BlockSpec((8, d), fix)] + [ANY] * len(r_shapes),
                  out_shape=[jax.ShapeDtypeStruct((s, d), BF16), jax.ShapeDtypeStruct((s, d), BF16),
                             jax.ShapeDtypeStruct((s, d), F32), jax.ShapeDtypeStruct((s, d), BF16),
                             jax.ShapeDtypeStruct((8, d), F32), jax.ShapeDtypeStruct((8, d), F32)] + r_shapes,
                  input_output_aliases=r_aliases, scratch_shapes=r_scratch,
                  compiler_params=_cp(1))(dx2, gate, e, x1, w_pg, g_ple, w_out, *r_ops)
    return tuple(outs[:6]) + (list(outs[6:]),)


def _mm_tn(a, b, name, a_layer=None):
    s, ka = a.shape[-2:]
    n = b.shape[1]
    tn = _tile(n, 1024)
    ns = n // tn
    tk = _tile(s, 512)
    nk = s // tk

    def body(a_ref, b_ref, o_ref, acc_ref):
        k = pl.program_id(1)

        @pl.when(k == 0)
        def _():
            acc_ref[...] = jnp.zeros_like(acc_ref)

        av = a_ref[...] if a_layer is None else a_ref[0]
        acc_ref[...] += _dot_tn(av.astype(BF16), b_ref[...].astype(BF16))

        @pl.when(k == nk - 1)
        def _():
            o_ref[...] = acc_ref[...]

    a_spec = (pl.BlockSpec((tk, ka), lambda j, k: (k, 0)) if a_layer is None
              else pl.BlockSpec((1, tk, ka), lambda j, k: (a_layer, k, 0)))
    return _pcall(body, name=name, grid=(ns, nk),
                  in_specs=[a_spec, pl.BlockSpec((tk, tn), lambda j, k: (k, j))],
                  out_specs=pl.BlockSpec((ka, tn), lambda j, k: (0, j)),
                  out_shape=jax.ShapeDtypeStruct((ka, n), F32),
                  scratch_shapes=[pltpu.VMEM((ka, tn), F32)], compiler_params=_cp(2))(a, b)


def _norm_gate_bwd(dy, yv, zg, g, gm):
    r = lax.rsqrt(_group_mean(yv * yv, gm) + EPS)
    n = yv * r
    sg = _sigmoid(zg)
    sil = zg * sg
    dzg = dy * n * g * (sg * (1.0 + zg * (1.0 - sg)))
    dn = dy * g * sil
    dyv = r * (dn - n * _group_mean(dn * n, gm))
    return dyv, dzg, dy * n * sil


def _convmix_bwd(dy, proj, conv_w, conv_b, bg, name, riders=()):
    _, s, sw = proj.shape
    nh = sw // LANES
    tc = _tile(s, 256)
    nr = s // tc

    def body(*refs):
        own, riders_end = _riders_run(riders, refs, 8, 4, 1, pl.program_id(0), nh)
        (dy_ref, cb_ref, cc_ref, ch_ref, cz_ref, w_ref, b_ref, g_ref,
         dp_ref, dw_ref, db_ref, dg_ref, dcv_ref) = own
        gm = _group_mat()
        dcv_ref[pl.ds(s, 8), :] = jnp.zeros((8, LANES), F32)

        def pass1(r, carry):
            dw0, dw1, dw2, db, dg = carry
            r0, u, s1, s2, cv = _conv_rows(cc_ref, ch_ref, w_ref, b_ref, r, tc)
            cb = cb_ref[0, pl.ds(r0, tc), :].astype(F32)
            dyc, dcz, dgr = _norm_gate_bwd(dy_ref[pl.ds(r0, tc), :].astype(F32), cb * cv,
                                           cz_ref[0, pl.ds(r0, tc), :].astype(F32), g_ref[...], gm)
            dp_ref[0, pl.ds(r0, tc), :] = (dyc * cv).astype(BF16)
            dp_ref[3, pl.ds(r0, tc), :] = dcz.astype(BF16)
            dcv = dyc * cb
            dcv_ref[pl.ds(r0, tc), :] = dcv
            return (dw0 + _colsum8(dcv * s2), dw1 + _colsum8(dcv * s1), dw2 + _colsum8(dcv * u),
                    db + _colsum8(dcv), dg + _colsum8(dgr))

        z8 = jnp.zeros((8, LANES), F32)
        dw0, dw1, dw2, db, dg = lax.fori_loop(0, nr, pass1, (z8, z8, z8, z8, z8))
        dw_ref[0] = dw0
        dw_ref[1] = dw1
        dw_ref[2] = dw2
        db_ref[...] = db
        dg_ref[...] = dg

        def pass2(r, carry):
            r0 = pl.multiple_of(r * tc, tc)
            dcv = dcv_ref[pl.ds(r0, tc), :]
            nxt = dcv_ref[pl.ds(pl.multiple_of(r0 + tc, 8), 8), :]
            rid = lax.broadcasted_iota(jnp.int32, dcv.shape, 0)
            n1 = jnp.where(rid == tc - 1, nxt[0:1, :], pltpu.roll(dcv, tc - 1, axis=0))
            n2 = jnp.where(rid == tc - 1, nxt[1:2, :],
                           jnp.where(rid == tc - 2, nxt[0:1, :], pltpu.roll(dcv, tc - 2, axis=0)))
            du = dcv * w_ref[2:3, :] + n1 * w_ref[1:2, :] + n2 * w_ref[0:1, :]
            dp_ref[1, pl.ds(r0, tc), :] = (du * ch_ref[0, pl.ds(r0, tc), :].astype(F32)).astype(BF16)
            dp_ref[2, pl.ds(r0, tc), :] = (du * cc_ref[0, pl.ds(r0, tc), :].astype(F32)).astype(BF16)
            return carry

        lax.fori_loop(0, nr, pass2, 0)
        riders_end()

    def sec(k):
        return pl.BlockSpec((1, s, LANES), lambda c: (k, 0, c))

    col = lambda c: (0, c)
    r_ops, r_shapes, r_scratch, r_aliases = _riders_plumb(riders, 8, 4)
    outs = _pcall(
        body, name=name, grid=(nh,),
        in_specs=[pl.BlockSpec((s, LANES), col), sec(0), sec(1), sec(2), sec(3),
                  pl.BlockSpec((3, LANES), col), pl.BlockSpec((1, LANES), col), pl.BlockSpec((1, LANES), col)]
        + [ANY] * len(r_ops),
        out_specs=[pl.BlockSpec((4, s, LANES), lambda c: (0, 0, c)), pl.BlockSpec((3, 8, LANES), lambda c: (0, 0, c)),
                   pl.BlockSpec((8, LANES), col), pl.BlockSpec((8, LANES), col)] + [ANY] * len(r_shapes),
        out_shape=[jax.ShapeDtypeStruct((8, s, sw), BF16), jax.ShapeDtypeStruct((3, 8, sw), F32),
                   jax.ShapeDtypeStruct((8, sw), F32), jax.ShapeDtypeStruct((8, sw), F32)] + r_shapes,
        input_output_aliases=r_aliases,
        scratch_shapes=[pltpu.VMEM((s + 8, LANES), F32)] + r_scratch, compiler_params=_cp(1),
    )(dy, proj, proj, proj, proj, conv_w, conv_b, bg, *r_ops)
    return tuple(outs[:4]) + (list(outs[4:]),)


def _attn_bwd(proj, dy, ya, tl, walked, bg, buf, name, riders=()):
    _, s, sw = proj.shape
    nhp = sw // LANES
    tk, t, nd, rows_c = _attn_tiles(s)
    nq = s // t
    scale = 1.0 / math.sqrt(HEAD)

    def body(*refs):
        step = pl.program_id(1)
        i = nq - 1 - step
        own, riders_end = _riders_run(riders, refs, 10, 2, 3, pl.program_id(0) * nq + step, nhp * nq)
        (q_ref, k_ref, v_ref, az_ref, dy_ref, ya_ref, tl_ref, nw_ref, g_ref, buf_ref, out_ref, dg_ref,
         dka_ref, dva_ref, dqa_ref) = own

        @pl.when(step == 0)
        def _():
            dka_ref[...] = jnp.zeros_like(dka_ref)
            dva_ref[...] = jnp.zeros_like(dva_ref)
            dg_ref[...] = jnp.zeros_like(dg_ref)

        dyv, dzg, dgr = _norm_gate_bwd(dy_ref[...].astype(F32), ya_ref[...].astype(F32), az_ref[0].astype(F32),
                                       g_ref[...], _group_mat())
        out_ref[3] = dzg.astype(BF16)
        dg_ref[...] += _colsum8(dgr)

        tri = (lax.broadcasted_iota(jnp.int32, (tk, tk), 0) <=
               lax.broadcasted_iota(jnp.int32, (tk, tk), 1)).astype(BF16)
        lane = lax.broadcasted_iota(jnp.int32, (t, LANES), 1)
        q = q_ref[0] * jnp.asarray(scale, BF16)
        do = dyv.astype(BF16)
        qms = [jnp.where((lane // HEAD) == h, q, jnp.zeros_like(q)) for h in range(2)]
        doms = [jnp.where((lane // HEAD) == h, do, jnp.zeros_like(do)) for h in range(2)]
        dqa_ref[...] = jnp.zeros_like(dqa_ref)
        chains = [(h, r0) for h in range(2) for r0 in range(0, t, rows_c)]
        qparts = [qms[h][r0:r0 + rows_c] for h, r0 in chains]
        doparts = [doms[h][r0:r0 + rows_c] for h, r0 in chains]
        tots = [tl_ref[h, r0:r0 + rows_c, :] for h, r0 in chains]

        def tile(j, carry, work):
            k0 = pl.multiple_of(j * tk, tk)
            kj = k_ref[0, pl.ds(k0, tk), :]
            vj = v_ref[0, pl.ds(k0, tk), :]
            zs = [_dot_nt(qparts[n], kj[:kw]) for n, kw, _ in work]
            das = [_dot_nt(doparts[n], vj[:kw]) for n, kw, _ in work]
            lms, lss, cls = [], [], []
            for z, (n, kw, mask) in zip(zs, work):
                lm, ls = _softplus_parts(z)
                if mask is not None:
                    lm = jnp.where(mask, lm, 0.0)
                lms.append(lm)
                lss.append(ls)
                cls.append(_split_dot(lm, tri[:kw, :kw], 2))
            abs_, gs, cgs = [], [], []
            for ls, cl, da, (n, kw, mask) in zip(lss, cls, das, work):
                a = jnp.exp(ls + (tots[n] - carry[n][0] - cl))
                if mask is not None:
                    a = jnp.where(mask, a, 0.0)
                g = a * da
                gs.append(g)
                abs_.append(a.astype(BF16))
                cgs.append(_split_dot(g, tri[:kw, :kw], 1))
            out = list(carry)
            dks, dvs = {}, {}
            for lm, ls, a, g, cg, (n, kw, mask) in zip(lms, lss, abs_, gs, cgs, work):
                h, r0 = chains[n]
                psum, gsum = carry[n]
                dz = g - jnp.exp(ls) * (gsum + cg)
                if mask is not None:
                    dz = jnp.where(mask, dz, 0.0)
                dz = dz.astype(BF16)
                dqa_ref[h, r0:r0 + rows_c, :] += jnp.dot(dz, kj[:kw], preferred_element_type=F32)
                dkh = _dot_tn(dz, qparts[n])
                dvh = _dot_tn(a, doparts[n])
                dks[kw] = dkh if kw not in dks else dks[kw] + dkh
                dvs[kw] = dvh if kw not in dvs else dvs[kw] + dvh
                out[n] = (psum + jnp.sum(lm, axis=1, keepdims=True), gsum + jnp.sum(g, axis=1, keepdims=True))
            for kw in dks:
                dka_ref[pl.ds(k0, kw), :] += dks[kw]
                dva_ref[pl.ds(k0, kw), :] += dvs[kw]
            return tuple(out)

        z1 = jnp.zeros((rows_c, 1), F32)
        code = jnp.clip(jnp.max(nw_ref[0].astype(jnp.int32)), 0, 2 * i * nd + 1)
        alone = jnp.minimum(code % 2, i * nd)
        whole = jnp.minimum(code // 2, i * nd - alone)
        carry = lax.fori_loop(i * nd - alone - whole, i * nd - alone,
                              lambda j, c: tile(j, c, _walk_work(chains, tk)), ((z1, z1),) * len(chains))
        if any(r0 >= tk for _, r0 in chains):
            carry = lax.cond(alone > 0, lambda c: tile(i * nd - 1, c, _walk_work(chains, tk, below=tk)),
                             lambda c: c, carry)
        for d in range(nd):
            carry = tile(i * nd + d, carry, _diag_work(chains, d, rows_c, tk))
        out_ref[0] = (jnp.where(lane < HEAD, dqa_ref[0], dqa_ref[1]) * scale).astype(BF16)
        own = pl.multiple_of(i * t, t)
        out_ref[1] = dka_ref[pl.ds(own, t), :].astype(BF16)
        out_ref[2] = dva_ref[pl.ds(own, t), :].astype(BF16)
        riders_end()

    def rows(sec):
        return pl.BlockSpec((1, t, LANES), lambda hp, st: (sec, nq - 1 - st, hp))

    def whole(sec):
        return pl.BlockSpec((1, s, LANES), lambda hp, st: (sec, 0, hp))

    r_ops, r_shapes, r_scratch, r_aliases = _riders_plumb(riders, 10, 2)
    outs = _pcall(
        body, name=name, grid=(nhp, nq),
        in_specs=[rows(4), whole(5), whole(6), rows(7),
                  pl.BlockSpec((t, LANES), lambda hp, st: (nq - 1 - st, hp + nhp)),
                  pl.BlockSpec((t, LANES), lambda hp, st: (nq - 1 - st, hp)),
                  pl.BlockSpec((2, t, 1), lambda hp, st: (hp, nq - 1 - st, 0)),
                  pl.BlockSpec((1, 8, LANES), lambda hp, st: (hp * nq + nq - 1 - st, 0, 0)),
                  pl.BlockSpec((1, LANES), lambda hp, st: (0, hp + nhp)), ANY] + [ANY] * len(r_ops),
        out_specs=[pl.BlockSpec((4, t, LANES), lambda hp, st: (1, nq - 1 - st, hp)),
                   pl.BlockSpec((8, LANES), lambda hp, st: (0, hp))] + [ANY] * len(r_shapes),
        out_shape=[jax.ShapeDtypeStruct(buf.shape, buf.dtype), jax.ShapeDtypeStruct((8, sw), F32)] + r_shapes,
        input_output_aliases={9: 0, **r_aliases},
        scratch_shapes=[pltpu.VMEM((s, LANES), F32), pltpu.VMEM((s, LANES), F32), pltpu.VMEM((2, t, LANES), F32)]
        + r_scratch,
        compiler_params=_cp(2))(proj, proj, proj, proj, dy, ya, tl, walked, bg, buf, *r_ops)
    return outs[0], outs[1], list(outs[2:])


def _grad_w_in(h, dproj, name):
    s, d = h.shape
    ns, _, sw = dproj.shape

    def body(h_ref, b_ref, o_ref, ht_ref):
        @pl.when(pl.program_id(0) == 0)
        def _():
            ht_ref[...] = h_ref[...].T

        o_ref[...] = jnp.dot(ht_ref[...], b_ref[0], preferred_element_type=F32)

    return _pcall(body, name=name, grid=(ns,),
                  in_specs=[pl.BlockSpec((s, d), lambda j: (0, 0)), pl.BlockSpec((1, s, sw), lambda j: (j, 0, 0))],
                  out_specs=pl.BlockSpec((d, sw), lambda j: (0, j)),
                  out_shape=jax.ShapeDtypeStruct((d, ns * sw), F32),
                  scratch_shapes=[pltpu.VMEM((d, s), BF16)], compiler_params=_cp(1))(h, dproj)


def _inproj_bwd(dproj, w, layer, x, g, dx1, name, riders=()):
    ns, s, sw = dproj.shape
    d = x.shape[1]
    tm = _tile(s, 256)

    def body(*refs):
        own, riders_end = _riders_run(riders, refs, 5, 2, 0, pl.program_id(0), s // tm)
        dp_ref, w_ref, x_ref, g_ref, dx1_ref, dx_ref, dg_ref = own

        @pl.when(pl.program_id(0) == 0)
        def _():
            dg_ref[...] = jnp.zeros_like(dg_ref)

        dh = _dot_nt(dp_ref[0], w_ref[0, :, 0:sw])
        for k in range(1, ns):
            dh = dh + _dot_nt(dp_ref[k], w_ref[0, :, k * sw:(k + 1) * sw])
        dxr, dgr = _rms_bwd_rows(dh, x_ref[...], g_ref[...])
        dx_ref[...] = dx1_ref[...] + dxr
        dg_ref[...] += _colsum8(dgr)
        riders_end()

    row = lambda m: (m, 0)
    fix = lambda m: (0, 0)
    r_ops, r_shapes, r_scratch, r_aliases = _riders_plumb(riders, 5, 2)
    outs = _pcall(body, name=name, grid=(s // tm,),
                  in_specs=[pl.BlockSpec((ns, tm, sw), lambda m: (0, m, 0)),
                            pl.BlockSpec((1, d, ns * sw), lambda m: (layer, 0, 0)),
                            pl.BlockSpec((tm, d), row), pl.BlockSpec((1, d), fix), pl.BlockSpec((tm, d), row)]
                  + [ANY] * len(r_ops),
                  out_specs=[pl.BlockSpec((tm, d), row), pl.BlockSpec((8, d), fix)] + [ANY] * len(r_shapes),
                  out_shape=[jax.ShapeDtypeStruct((s, d), F32), jax.ShapeDtypeStruct((8, d), F32)] + r_shapes,
                  input_output_aliases=r_aliases, scratch_shapes=r_scratch,
                  compiler_params=_cp(1))(dproj, w, x, g, dx1, *r_ops)
    return outs[0], outs[1], list(outs[2:])


def _adamw(w, g, m, v, name):
    r, c = w.shape
    tr = _tile(r, 256)
    c1 = 1.0 - ADAM_B1 ** ADAM_STEP
    c2 = 1.0 - ADAM_B2 ** ADAM_STEP

    def body(w_ref, g_ref, m_ref, v_ref, go_ref, d_ref, mo_ref, vo_ref):
        gv = g_ref[...]
        go_ref[...] = gv
        mn = ADAM_B1 * m_ref[...] + (1.0 - ADAM_B1) * gv
        vn = ADAM_B2 * v_ref[...] + (1.0 - ADAM_B2) * (gv * gv)
        d_ref[...] = -ADAM_LR * ((mn / c1) / (jnp.sqrt(vn / c2) + ADAM_EPS) + ADAM_WD * w_ref[...])
        mo_ref[...] = mn
        vo_ref[...] = vn

    t = pl.BlockSpec((tr, c), lambda i: (i, 0))
    return _pcall(body, name=name, grid=(r // tr,), in_specs=[t] * 4, out_specs=[t] * 4,
                  out_shape=[jax.ShapeDtypeStruct((r, c), F32)] * 4, compiler_params=_cp(1))(w, g, m, v)


def _add_half(grad, other, core, a, name):
    hr, hc = other.shape
    tr = _tile(hr, 256)
    nb = hr // tr

    def body(c_ref, g_ref, o_ref, out_ref, outb_ref):
        v = g_ref[...] + o_ref[...]
        out_ref[...] = v
        outb_ref[...] = v.astype(BF16)

    t = pl.BlockSpec((tr, hc), lambda i, c: (i, 0))
    own = (lambda i, c: (c[0] * nb + i, 0)) if HALF_AXES[a] == 0 else (lambda i, c: (i, c[0]))
    grid_spec = pltpu.PrefetchScalarGridSpec(
        num_scalar_prefetch=1, grid=(nb,), in_specs=[pl.BlockSpec((tr, hc), own), t], out_specs=[t, t])
    return _pcall(body, name=name, grid_spec=grid_spec,
                  out_shape=[jax.ShapeDtypeStruct((hr, hc), F32), jax.ShapeDtypeStruct((hr, hc), BF16)],
                  compiler_params=_cp(1))(core.reshape(1).astype(jnp.int32), grad, other)


def _sum_half(wide, parts, chip, core, layer, a, stack, name):
    _, sr, sc = parts.shape
    tr = _tile(sr, 256)
    nbs = sr // tr

    def body(k_ref, f_ref, p_ref, *rest):
        rest[-1][0] = ((f_ref[...] + p_ref[0].astype(F32)) + p_ref[1].astype(F32)) + p_ref[2].astype(F32)

    f_map = (lambda i, k: (i, k[0])) if SHARD_AXES[a] == 1 else (lambda i, k: (k[0] * nbs + i, 0))
    if HALF_AXES[a] == 0:
        shape, o_map = (DEPTH, 2 * sr, sc), (lambda i, k: (layer, k[1] * nbs + i, 0))
    else:
        shape, o_map = (DEPTH, sr, 2 * sc), (lambda i, k: (layer, i, k[1]))
    in_specs = [pl.BlockSpec((tr, sc), f_map), pl.BlockSpec((3, tr, sc), lambda i, k: (0, i, 0))]
    args = [wide, parts]
    aliases = {}
    if stack is not None:
        in_specs.append(ANY)
        args.append(stack)
        aliases = {3: 0}
    grid_spec = pltpu.PrefetchScalarGridSpec(
        num_scalar_prefetch=1, grid=(nbs,), in_specs=in_specs, out_specs=pl.BlockSpec((1, tr, sc), o_map))
    return _pcall(body, name=name, grid_spec=grid_spec, out_shape=jax.ShapeDtypeStruct(shape, F32),
                  input_output_aliases=aliases,
                  compiler_params=_cp(1))(jnp.stack([chip, core]).astype(jnp.int32), *args)


def _sum_slots(slots, name):
    n = slots.shape[0]

    def body(s_ref, o_ref):
        acc = s_ref[0]
        for i in range(1, n):
            acc = acc + s_ref[i]
        o_ref[...] = acc

    return _pcall(body, name=name, out_shape=jax.ShapeDtypeStruct(slots.shape[1:], F32))(slots)


def _place():
    return lax.axis_index("x"), lax.axis_index("y"), lax.axis_index("c")


def _shard_view(ref, axis, chip, size):
    if axis == 0:
        return ref.at[pl.ds(chip * size, size), :]
    return ref.at[:, pl.ds(chip * size, size)]


SHARD_AXES = (1, 0, 0, 1)
HALF_AXES = tuple(1 - ax for ax in SHARD_AXES)


class _Rider:
    def __init__(self, operands, out_shape, sems, phases, aliased=False):
        self.operands, self.out_shape, self.sems = list(operands), list(out_shape), list(sems)
        self.phases, self.aliased = phases, aliased


def _riders_plumb(riders, n_in, n_out):
    ops, out_shape, scratch, aliases = [], [], [], {}
    for r in riders:
        if r.aliased:
            for k in range(len(r.operands)):
                aliases[n_in + len(ops) + k] = n_out + len(out_shape) + k
        ops += r.operands
        out_shape += r.out_shape
        scratch += r.sems
    return ops, out_shape, scratch, aliases


def _riders_run(riders, refs, n_in, n_out, n_scr, step, nsteps):
    n_rin = sum(len(r.operands) for r in riders)
    n_rout = sum(len(r.out_shape) for r in riders)
    rin = refs[n_in:n_in + n_rin]
    o0 = n_in + n_rin
    rout = refs[o0 + n_out:o0 + n_out + n_rout]
    s0 = o0 + n_out + n_rout
    rsem = refs[s0 + n_scr:]
    own = list(refs[:n_in]) + list(refs[o0:o0 + n_out]) + list(refs[s0:s0 + n_scr])
    lasts = []
    for r in riders:
        ph = r.phases(rin[:len(r.operands)], rout[:len(r.out_shape)], rsem[:len(r.sems)])
        rin, rout, rsem = rin[len(r.operands):], rout[len(r.out_shape):], rsem[len(r.sems):]
        pl.when(step == 0)(ph[0])
        for mid in ph[1:-1]:
            pl.when(step == (3 * nsteps) // 4)(mid)
        lasts.append(ph[-1])

    def finish():
        for last in lasts:
            pl.when(step == nsteps - 1)(last)

    return own, finish


def _gather_phases(ins, outs, ssem, rsem, layer, which):
    n = len(ins)
    x, y, c = _place()
    me = 2 * x + y
    chips = [(1 - x, y), (x, 1 - y), (1 - x, 1 - y)]

    def piece(a, chip, half, of):
        ax = SHARD_AXES[which[a]]
        block = _shard_view(of[a].at[layer], ax, chip, of[a].shape[1 + ax] // 4)
        r = block.shape[0] // 2
        return block.at[pl.ds(half * r, r), :]

    def over_ici(a, j):
        cx, cy = chips[j]
        return pltpu.make_async_remote_copy(
            src_ref=piece(a, me, c, ins), dst_ref=piece(a, me, c, outs), send_sem=ssem.at[a, j],
            recv_sem=rsem.at[a, j], device_id=(cx, cy, c), device_id_type=MESH)

    def landed(a, j, half):
        cx, cy = chips[j]
        return piece(a, 2 * cx + cy, half, outs)

    def to_sibling(a, j):
        got = landed(a, j, c)
        return pltpu.make_async_remote_copy(
            src_ref=got, dst_ref=got, send_sem=ssem.at[a, 3 + j], recv_sem=rsem.at[a, 3 + j],
            device_id=(x, y, 1 - c), device_id_type=MESH)

    def wait_arrival(a, k, place):
        pltpu.make_async_remote_copy(src_ref=place, dst_ref=place, send_sem=ssem.at[a, k], recv_sem=rsem.at[a, k],
                                     device_id=(x, y, c), device_id_type=MESH).wait_recv()

    def start():
        for a in range(n):
            for j in range(3):
                over_ici(a, j).start()

    def pass_on():
        for a in range(n):
            for j in range(3):
                wait_arrival(a, j, landed(a, j, c))
                to_sibling(a, j).start()

    def finish():
        for a in range(n):
            for j in range(3):
                wait_arrival(a, 3 + j, landed(a, j, 1 - c))
        for a in range(n):
            for j in range(3):
                over_ici(a, j).wait_send()
                to_sibling(a, j).wait_send()

    return start, pass_on, finish


def _gather_rider(fulls, layer, which):
    n = len(fulls)
    return _Rider(fulls, [jax.ShapeDtypeStruct(f.shape, f.dtype) for f in fulls],
                  [pltpu.SemaphoreType.DMA((n, 6)), pltpu.SemaphoreType.DMA((n, 6))],
                  lambda ins, outs, sems: _gather_phases(ins, outs, sems[0], sems[1], layer, which), aliased=True)


def _ride_alone(rider, name):
    n = len(rider.operands)

    def body(*refs):
        for phase in rider.phases(refs[:n], refs[n:n + len(rider.out_shape)], refs[n + len(rider.out_shape):]):
            phase()

    return _pcall(body, name=name, in_specs=[ANY] * n, out_specs=[ANY] * len(rider.out_shape),
                  out_shape=rider.out_shape, scratch_shapes=rider.sems,
                  input_output_aliases={a: a for a in range(n)} if rider.aliased else {})(*rider.operands)


def _half_view(ref, a, half):
    n = ref.shape[HALF_AXES[a]] // 2
    if HALF_AXES[a] == 0:
        return ref.at[pl.ds(half * n, n), :]
    return ref.at[:, pl.ds(half * n, n)]


def _swap_rider(grads, which):
    n = len(grads)
    halves = []
    for g, w in zip(grads, which):
        sh = list(g.shape)
        sh[HALF_AXES[w]] //= 2
        halves.append(jax.ShapeDtypeStruct(tuple(sh), g.dtype))

    def phases(srcs, outs, sems):
        x, y, c = _place()

        def copy(a):
            return pltpu.make_async_remote_copy(
                src_ref=_half_view(srcs[a], which[a], 1 - c), dst_ref=outs[a], send_sem=sems[0].at[a],
                recv_sem=sems[1].at[a], device_id=(x, y, 1 - c), device_id_type=MESH)

        def start():
            for a in range(n):
                copy(a).start()

        def finish():
            for a in range(n):
                copy(a).wait()

        return start, finish

    return _Rider(grads, halves, [pltpu.SemaphoreType.DMA((n,)), pltpu.SemaphoreType.DMA((n,))], phases)


def _scatter_rider(sums, which):
    n = len(sums)
    shapes = []
    for f, w in zip(sums, which):
        sh = list(f.shape)
        sh[SHARD_AXES[w]] //= 4
        shapes.append(jax.ShapeDtypeStruct((3,) + tuple(sh), f.dtype))

    def phases(srcs, outs, sems):
        x, y, c = _place()
        chips = [(1 - x, y), (x, 1 - y), (1 - x, 1 - y)]

        def copy(a, j):
            cx, cy = chips[j]
            ax = SHARD_AXES[which[a]]
            src = _shard_view(srcs[a], ax, 2 * cx + cy, srcs[a].shape[ax] // 4)
            return pltpu.make_async_remote_copy(src_ref=src, dst_ref=outs[a].at[j], send_sem=sems[0].at[a, j],
                                                recv_sem=sems[1].at[a, j], device_id=(cx, cy, c), device_id_type=MESH)

        def start():
            for a in range(n):
                for j in range(3):
                    copy(a, j).start()

        def finish():
            for a in range(n):
                for j in range(3):
                    copy(a, j).wait()

        return start, finish

    return _Rider(sums, shapes, [pltpu.SemaphoreType.DMA((n, 3)), pltpu.SemaphoreType.DMA((n, 3))], phases)


def _pair_halves(stacks):
    n = len(stacks)

    def body(*refs):
        ins, outs = refs[:n], refs[n:2 * n]
        ssem, rsem = refs[2 * n:]
        x, y, c = _place()
        cps = [pltpu.make_async_remote_copy(
            src_ref=_half_view(ins[a].at[l], a, c), dst_ref=_half_view(outs[a].at[l], a, c), send_sem=ssem.at[a, l],
            recv_sem=rsem.at[a, l], device_id=(x, y, 1 - c), device_id_type=MESH)
            for a in range(n) for l in range(DEPTH)]
        for cp in cps:
            cp.start()
        for a in range(n):
            for l in range(DEPTH):
                got = _half_view(outs[a].at[l], a, 1 - c)
                pltpu.make_async_remote_copy(src_ref=got, dst_ref=got, send_sem=ssem.at[a, l], recv_sem=rsem.at[a, l],
                                             device_id=(x, y, 1 - c), device_id_type=MESH).wait_recv()
        for cp in cps:
            cp.wait_send()

    return _pcall(body, name="pair_halves", in_specs=[ANY] * n, out_specs=[ANY] * n,
                  out_shape=[jax.ShapeDtypeStruct(st.shape, st.dtype) for st in stacks],
                  input_output_aliases={a: a for a in range(n)},
                  scratch_shapes=[pltpu.SemaphoreType.DMA((n, DEPTH)), pltpu.SemaphoreType.DMA((n, DEPTH))])(*stacks)


class _GradReduce:
    def __init__(self, chip, core):
        self.chip, self.core = chip, core
        self.stacks = [None] * len(SHARD_AXES)

    def add(self, layer, grads, which, got):
        return [(layer, w) + tuple(_add_half(g, o, self.core, w, f"add_half_{layer}_{w}"))
                for g, o, w in zip(grads, got, which)]

    def finish(self, sums, partials):
        for (layer, w, wide, _), pr in zip(sums, partials):
            self.stacks[w] = _sum_half(wide, pr, self.chip, self.core, layer, w, self.stacks[w], f"sum_half_{layer}_{w}")

    def result(self):
        return _pair_halves(self.stacks)


def _exchange_small(pack, name):
    nd = 8

    def body(p_ref, o_ref, ssem, rsem):
        x, y, c = _place()
        me = 4 * x + 2 * y + c
        o_ref[me] = p_ref[...]
        cps = []
        for j in range(1, nd):
            px, py, pc = x ^ (j >> 2), y ^ ((j >> 1) & 1), c ^ (j & 1)
            cps.append(pltpu.make_async_remote_copy(
                src_ref=p_ref, dst_ref=o_ref.at[me], send_sem=ssem.at[j - 1], recv_sem=rsem.at[j - 1],
                device_id=(px, py, pc), device_id_type=MESH))
        for cp in cps:
            cp.start()
        for j in range(1, nd):
            peer = me ^ j
            got = o_ref.at[peer]
            pltpu.make_async_remote_copy(src_ref=got, dst_ref=got, send_sem=ssem.at[j - 1], recv_sem=rsem.at[j - 1],
                                         device_id=(x, y, c), device_id_type=MESH).wait_recv()
        for cp in cps:
            cp.wait_send()

    vm = pl.BlockSpec(memory_space=pltpu.VMEM)
    return _pcall(body, name=name, in_specs=[vm], out_specs=vm,
                  out_shape=jax.ShapeDtypeStruct((nd,) + pack.shape, pack.dtype),
                  scratch_shapes=[pltpu.SemaphoreType.DMA((nd - 1,)), pltpu.SemaphoreType.DMA((nd - 1,))])(pack)


def _row(v):
    return v.reshape(1, -1)


def _local_step(x, p, tgt, norm_g, conv_w, conv_b, branch_g, ple_norm_g, b_pg, final_g, w_in, w_out, w_pg, w_pe,
                gather=False, reduce=None):
    saved = []
    xl = x
    for l in range(DEPTH):
        riders = [_gather_rider([w_out, w_pg, w_pe], 0, [1, 2, 3])] if gather and l == 0 else []
        h, proj, got = _inproj(xl, _row(norm_g[l]), w_in, l, f"inproj_{l}", riders)
        if riders:
            w_out, w_pg, w_pe = got
        later = gather and l + 1 < DEPTH
        riders = [_gather_rider([w_in], l + 1, [0])] if later else []
        ya, tl, walked, got = _attn_fwd(proj, f"attn_fwd_{l}", riders)
        if riders:
            w_in, = got
        riders = [_gather_rider([w_out, w_pg, w_pe], l + 1, [1, 2, 3])] if later else []
        y, got = _mix_fwd(proj, ya, conv_w[l], _row(conv_b[l]), _row(branch_g[l]), f"mix_fwd_{l}", riders)
        if riders:
            w_out, w_pg, w_pe = got
        x1, hn = _outproj(y, w_out, l, xl, _row(ple_norm_g[l]), f"outproj_{l}")
        x2, gate, e = _ple_fwd(hn, w_pg, _row(b_pg[l]), p, w_pe, l, x1, f"ple_fwd_{l}")
        saved.append((xl, h, proj, ya, tl, walked, y, x1, hn, gate, e))
        xl = x2

    sq, dx, d_final = _loss_head(xl, tgt, _row(final_g), "loss_head")

    big = [None] * DEPTH
    carried = []
    small = {k: [None] * DEPTH for k in ("norm_g", "conv_w", "conv_b", "branch_g", "ple_norm_g", "b_pg")}
    for l in reversed(range(DEPTH)):
        xl, h, proj, ya, tl, walked, y, x1, hn, gate, e = saved[l]
        du, de, dx1, dy, db_pg, d_ple, _ = _ple_bwd(dx, gate, e, x1, w_pg, _row(ple_norm_g[l]), w_out, l,
                                                    f"ple_bwd_{l}")
        sums = carried
        g_pg = _mm_tn(hn, du, f"grad_w_pg_{l}")
        g_pe = _mm_tn(p, de, f"grad_w_pe_{l}", a_layer=l)
        g_out = _mm_tn(y, dx1, f"grad_w_out_{l}")
        others = [g_out, g_pg, g_pe]
        riders = [_swap_rider(others, [1, 2, 3])] if reduce is not None else []
        dpc, d_cw, d_cb, d_bg_c, got = _convmix_bwd(dy, proj, conv_w[l], _row(conv_b[l]), _row(branch_g[l]),
                                                    f"convmix_bwd_{l}", riders)
        if reduce is not None:
            sums += reduce.add(l, others, [1, 2, 3], got)
        riders = [_scatter_rider([sm[3] for sm in sums], [sm[1] for sm in sums])] if sums else []
        dproj, d_bg_a, got = _attn_bwd(proj, dy, ya, tl, walked, _row(branch_g[l]), dpc, f"attn_bwd_{l}", riders)
        if sums:
            reduce.finish(sums, got)
        g_in = _grad_w_in(h, dproj, f"grad_w_in_{l}")
        riders = [_swap_rider([g_in], [0])] if reduce is not None else []
        dx, d_norm, got = _inproj_bwd(dproj, w_in, l, xl, _row(norm_g[l]), dx1, f"inproj_bwd_{l}", riders)
        big[l] = (g_in, g_out, g_pg, g_pe)
        carried = reduce.add(l, [g_in], [0], got) if reduce is not None else []
        small["norm_g"][l] = jnp.sum(d_norm, axis=0)
        small["conv_w"][l] = jnp.sum(d_cw, axis=1)
        small["conv_b"][l] = jnp.sum(d_cb, axis=0)
        small["branch_g"][l] = jnp.concatenate([jnp.sum(d_bg_c, axis=0), jnp.sum(d_bg_a, axis=0)])
        small["ple_norm_g"][l] = jnp.sum(d_ple, axis=0)
        small["b_pg"][l] = jnp.sum(db_pg, axis=0)
    if carried:
        reduce.finish(carried, _ride_alone(_scatter_rider([sm[3] for sm in carried], [0]), "scatter_shards_last"))
    small = {k: jnp.stack(v) for k, v in small.items()}
    small["final_g"] = jnp.sum(d_final, axis=0)
    return sq[0, 0], dx, big, small


SMALL_ORDER = ("norm_g", "conv_w", "conv_b", "branch_g", "ple_norm_g", "b_pg", "final_g")


def _pack(parts, width):
    flat = jnp.concatenate([v.reshape(-1) for v in parts])
    rows = -(-flat.shape[0] // width)
    rows = -(-rows // 8) * 8
    return jnp.pad(flat, (0, rows * width - flat.shape[0])).reshape(rows, width)


def _unpack(packed, like):
    flat = packed.reshape(-1)
    out, off = [], 0
    for v in like:
        out.append(flat[off:off + v.size].reshape(v.shape))
        off += v.size
    return out


def kernel(x, p, norm_g, w_in, conv_w, conv_b, branch_g, w_out, ple_norm_g, w_pg, b_pg, w_pe, final_g, loss_target, m_norm_g, m_w_in, m_conv_w, m_conv_b, m_branch_g, m_w_out, m_ple_norm_g, m_w_pg, m_b_pg, m_w_pe, m_final_g, v_norm_g, v_w_in, v_conv_w, v_conv_b, v_branch_g, v_w_out, v_ple_norm_g, v_w_pg, v_b_pg, v_w_pe, v_final_g):
    ix, iy, ic = _place()
    chip = 2 * ix + iy
    d = x.shape[-1]

    big_w = (w_in, w_out, w_pg, w_pe)
    own = [_cast_into_full(w, chip, ax, f"cast_{i}") for i, (w, ax) in enumerate(zip(big_w, SHARD_AXES))]
    full_in, = _ride_alone(_gather_rider([own[0]], 0, [0]), "gather_w_in_0")
    full_out, full_pg, full_pe = own[1:]
    cw_shard = conv_w.shape[-1]
    cw_slots = _exchange_small(_pack([conv_w], LANES), "exchange_conv_w")
    conv_full = jnp.concatenate([_unpack(cw_slots[2 * k], [conv_w])[0] for k in range(4)], axis=-1)

    reduce = _GradReduce(chip, ic)
    sq, dx, _, small_g = _local_step(
        x[0], p[:, 0], loss_target[0], norm_g, conv_full, conv_b, branch_g, ple_norm_g, b_pg, final_g,
        full_in, full_out, full_pg, full_pe, gather=True, reduce=reduce)
    g_big = reduce.result()

    parts = [small_g[k] for k in SMALL_ORDER] + [sq.reshape(1)]
    slots = _exchange_small(_pack(parts, d), "exchange_small_grads")
    total = _unpack(_sum_slots(slots, "sum_small"), parts)
    g_small = dict(zip(SMALL_ORDER, total[:-1]))
    loss = 0.5 * total[-1][0] / d
    g_small["conv_w"] = lax.dynamic_slice_in_dim(g_small["conv_w"], chip * cw_shard, cw_shard, axis=2)

    grads = dict(g_small)
    grads.update(w_in=g_big[0], w_out=g_big[1], w_pg=g_big[2], w_pe=g_big[3])
    weights = dict(norm_g=norm_g, w_in=w_in, conv_w=conv_w, conv_b=conv_b, branch_g=branch_g, w_out=w_out,
                   ple_norm_g=ple_norm_g, w_pg=w_pg, b_pg=b_pg, w_pe=w_pe, final_g=final_g)
    ms = dict(norm_g=m_norm_g, w_in=m_w_in, conv_w=m_conv_w, conv_b=m_conv_b, branch_g=m_branch_g, w_out=m_w_out,
              ple_norm_g=m_ple_norm_g, w_pg=m_w_pg, b_pg=m_b_pg, w_pe=m_w_pe, final_g=m_final_g)
    vs = dict(norm_g=v_norm_g, w_in=v_w_in, conv_w=v_conv_w, conv_b=v_conv_b, branch_g=v_branch_g, w_out=v_w_out,
              ple_norm_g=v_ple_norm_g, w_pg=v_w_pg, b_pg=v_b_pg, w_pe=v_w_pe, final_g=v_final_g)
    names = ("norm_g", "w_in", "conv_w", "conv_b", "branch_g", "w_out", "ple_norm_g", "w_pg", "b_pg", "w_pe", "final_g")
    delta, new_m, new_v = {}, {}, {}
    for k in ("w_in", "w_out", "w_pg", "w_pe"):
        shp = weights[k].shape
        two = lambda a: a.reshape(-1, shp[-1])
        gr, dl, mn, vn = _adamw(two(weights[k]), two(grads[k]), two(ms[k]), two(vs[k]), f"adamw_{k}")
        delta[k], new_m[k], new_v[k] = dl.reshape(shp), mn.reshape(shp), vn.reshape(shp)
        grads[k] = gr.reshape(shp)
    like = [weights[k] for k in SMALL_ORDER]
    packs = [_pack([src[k] for k in SMALL_ORDER], d) for src in (weights, grads, ms, vs)]
    outs = _adamw(*packs, "adamw_small")
    for res, o in zip((delta, new_m, new_v), outs[1:]):
        res.update(dict(zip(SMALL_ORDER, _unpack(o, like))))

    return (loss, dx[None], *[grads[k] for k in names], *[delta[k] for k in names],
            *[new_m[k] for k in names], *[new_v[k] for k in names])
```

```python
import math

import jax
import jax.numpy as jnp
from jax import lax
from jax.experimental import pallas as pl
from jax.experimental.pallas import tpu as pltpu

F32 = jnp.float32
BF16 = jnp.bfloat16
EPS = 1e-6
HEAD = 64
LANES = 128
ATT_TK = 256
ATT_TQ = 512
ATT_ROWS = 128
ALIVE_LOG = -105.0
DEPTH = 2
VMEM_LIMIT = 56 * 1024 * 1024
MESH = pl.DeviceIdType.MESH
ANY = pl.BlockSpec(memory_space=pl.ANY)

ADAM_LR = 0.001
ADAM_B1 = 0.9
ADAM_B2 = 0.999
ADAM_EPS = 1e-08
ADAM_WD = 0.01
ADAM_STEP = 10


def _pcall(body, **kw):
    return pl.pallas_call(body, **kw)


def _cp(n_axes):
    return pltpu.CompilerParams(dimension_semantics=("arbitrary",) * n_axes, vmem_limit_bytes=VMEM_LIMIT)


def _tile(n, pref):
    return pref if n % pref == 0 else n


def _split_dot(a, b, passes):
    out = None
    rem = a
    for _ in range(passes):
        hi = rem.astype(BF16)
        t = jnp.dot(hi, b, preferred_element_type=F32)
        out = t if out is None else out + t
        rem = rem - hi.astype(F32)
    return out


def _group_mat():
    r = lax.broadcasted_iota(jnp.int32, (LANES, LANES), 0) // HEAD
    c = lax.broadcasted_iota(jnp.int32, (LANES, LANES), 1) // HEAD
    return jnp.where(r == c, 1.0 / HEAD, 0.0).astype(BF16)


def _group_mean(v, gm):
    return _split_dot(v, gm, 2)


def _sigmoid(z):
    return 1.0 / (1.0 + jnp.exp(-z))


def _dot_nt(a, b):
    return lax.dot_general(a, b, (((1,), (1,)), ((), ())), preferred_element_type=F32)


def _dot_tn(a, b):
    return lax.dot_general(a, b, (((0,), (0,)), ((), ())), preferred_element_type=F32)


def _cast_into_full(w, chip, axis, name):
    _, r, c = w.shape
    tr = _tile(r, 256)
    nb = r // tr
    full = (DEPTH, 4 * r, c) if axis == 0 else (DEPTH, r, 4 * c)

    def body(k_ref, w_ref, o_ref):
        o_ref[...] = w_ref[...].astype(BF16)

    out_map = (lambda l, i, k: (l, k[0] * nb + i, 0)) if axis == 0 else (lambda l, i, k: (l, i, k[0]))
    grid_spec = pltpu.PrefetchScalarGridSpec(
        num_scalar_prefetch=1, grid=(DEPTH, nb),
        in_specs=[pl.BlockSpec((1, tr, c), lambda l, i, k: (l, i, 0))],
        out_specs=pl.BlockSpec((1, tr, c), out_map))
    return _pcall(body, name=name, grid_spec=grid_spec, out_shape=jax.ShapeDtypeStruct(full, BF16),
                  compiler_params=_cp(2))(chip.reshape(1).astype(jnp.int32), w)


def _rms_bwd_rows(dh, xv, g):
    r = lax.rsqrt(jnp.mean(xv * xv, axis=-1, keepdims=True) + EPS)
    xn = xv * r
    dxn = dh * g
    dx = r * (dxn - xn * jnp.mean(dxn * xn, axis=-1, keepdims=True))
    return dx, dh * xn


def _colsum8(v):
    tm, d = v.shape
    return jnp.sum(v.reshape(tm // 8, 8, d), axis=0)


def _inproj(x, g, w, layer, name, riders=()):
    s, d = x.shape
    n = w.shape[2]
    sw = d // 2
    ns = n // sw
    tm = _tile(s, 512)

    def body(*refs):
        own, riders_end = _riders_run(riders, refs, 3, 2, 0, pl.program_id(0), s // tm)
        x_ref, g_ref, w_ref, h_ref, o_ref = own
        xv = x_ref[...]
        r = lax.rsqrt(jnp.mean(xv * xv, axis=-1, keepdims=True) + EPS)
        h = (xv * r * g_ref[...]).astype(BF16)
        h_ref[...] = h
        for k in range(ns):
            o_ref[k] = jnp.dot(h, w_ref[0, :, k * sw:(k + 1) * sw], preferred_element_type=F32).astype(BF16)
        riders_end()

    r_ops, r_shapes, r_scratch, r_aliases = _riders_plumb(riders, 3, 2)
    outs = _pcall(body, name=name, grid=(s // tm,),
                  in_specs=[pl.BlockSpec((tm, d), lambda m: (m, 0)), pl.BlockSpec((1, d), lambda m: (0, 0)),
                            pl.BlockSpec((1, d, n), lambda m: (layer, 0, 0))] + [ANY] * len(r_ops),
                  out_specs=[pl.BlockSpec((tm, d), lambda m: (m, 0)), pl.BlockSpec((ns, tm, sw), lambda m: (0, m, 0))]
                  + [ANY] * len(r_shapes),
                  out_shape=[jax.ShapeDtypeStruct((s, d), BF16), jax.ShapeDtypeStruct((ns, s, sw), BF16)] + r_shapes,
                  input_output_aliases=r_aliases, scratch_shapes=r_scratch,
                  compiler_params=_cp(1))(x, g, w, *r_ops)
    return outs[0], outs[1], list(outs[2:])


def _softplus_parts(z):
    lm = jnp.minimum(-z, 0.0) - jnp.log(1.0 + jnp.exp(-jnp.abs(z)))
    return lm, lm + z


def _attn_tiles(s):
    tk = _tile(s, ATT_TK)
    tq = _tile(s, ATT_TQ)
    return tk, tq, tq // tk, min(ATT_ROWS, tq)


def _diag_work(chains, d, rows, tk):
    work = []
    for n, (_, r0) in enumerate(chains):
        if r0 + rows - 1 <= d * tk:
            continue
        kw = tk // 2 if (tk % 2 == 0 and r0 + rows <= d * tk + tk // 2) else tk
        if r0 >= d * tk + kw:
            mask = None
        else:
            row = lax.broadcasted_iota(jnp.int32, (rows, kw), 0)
            col = lax.broadcasted_iota(jnp.int32, (rows, kw), 1)
            mask = col + d * tk < row + r0
        work.append((n, kw, mask))
    return work


def _both(mask, gate):
    if mask is None:
        return gate
    if gate is None:
        return mask
    return jnp.logical_and(mask, gate)


def _any_alive(rsums):
    m = rsums[0]
    for r in rsums[1:]:
        m = jnp.maximum(m, r)
    return jnp.max((m > ALIVE_LOG).astype(jnp.int32))


def _attn_fwd(proj, name, riders=()):
    _, s, sw = proj.shape
    nhp = sw // LANES
    tk, tq, nd, rows = _attn_tiles(s)
    nq = s // tq
    scale = 1.0 / math.sqrt(HEAD)

    def body(*refs):
        i = pl.program_id(1)
        own, riders_end = _riders_run(riders, refs, 3, 3, 1, pl.program_id(0) * nq + i, nhp * nq)
        q_ref, k_ref, v_ref, o_ref, tl_ref, nw_ref, acc_ref = own
        tri = (lax.broadcasted_iota(jnp.int32, (tk, tk), 0) >
               lax.broadcasted_iota(jnp.int32, (tk, tk), 1)).astype(BF16)
        lane = lax.broadcasted_iota(jnp.int32, (tq, LANES), 1)
        q = q_ref[0] * jnp.asarray(scale, BF16)
        qms = [jnp.where((lane // HEAD) == h, q, jnp.zeros_like(q)) for h in range(2)]
        acc_ref[...] = jnp.zeros_like(acc_ref)
        chains = [(h, r0) for h in range(2) for r0 in range(0, tq, rows)]
        qparts = [qms[h][r0:r0 + rows] for h, r0 in chains]

        def block(rsums, tiles, items):
            kjs = [k_ref[0, pl.ds(pl.multiple_of(j * tk, tk), tk), :] for j in tiles]
            vjs = [v_ref[0, pl.ds(pl.multiple_of(j * tk, tk), tk), :] for j in tiles]
            zs = [_dot_nt(qparts[n], kjs[t][:kw]) for n, t, kw, _, _ in items]
            lms, lss, css = [], [], []
            for z, (n, t, kw, mask, gate) in zip(zs, items):
                lm, ls = _softplus_parts(z)
                keep = _both(mask, gate)
                if keep is not None:
                    lm = jnp.where(keep, lm, 0.0)
                lms.append(lm)
                lss.append(ls)
                css.append(_split_dot(lm, tri[:kw, :kw], 2))
            cur = list(rsums)
            for lm, ls, cs, (n, t, kw, mask, gate) in zip(lms, lss, css, items):
                h, r0 = chains[n]
                a = jnp.exp(ls + (cur[n] + cs))
                keep = _both(mask, gate)
                if keep is not None:
                    a = jnp.where(keep, a, 0.0)
                acc_ref[h, r0:r0 + rows, :] += jnp.dot(a.astype(BF16), vjs[t][:kw], preferred_element_type=F32)
                cur[n] = cur[n] + jnp.sum(lm, axis=1, keepdims=True)
            return tuple(cur)

        everyone = [(n, 0, tk, None, None) for n in range(len(chains))]
        upper = [n for n, (_, r0) in enumerate(chains) if r0 >= tk]
        lower = [n for n, (_, r0) in enumerate(chains) if r0 < tk]
        left = jnp.maximum(i * nd - 1, 0)
        rsums = (jnp.zeros((rows, 1), F32),) * len(chains)
        for d in reversed(range(1, nd)):
            rsums = block(rsums, [i * nd + d], [(n, 0, kw, m, None) for n, kw, m in _diag_work(chains, d, rows, tk)])
        rsums = block(rsums, [i * nd, left],
                      [(n, 0, kw, m, None) for n, kw, m in _diag_work(chains, 0, rows, tk)]
                      + [(n, 1, tk, None, i > 0) for n in lower])

        if upper:
            too = (i > 0) & (_any_alive([rsums[n] for n in upper]) > 0)
            rsums = lax.cond(too, lambda rs: block(rs, [left], [(n, 0, tk, None, None) for n in upper]),
                             lambda rs: rs, rsums)
            too = too.astype(jnp.int32)
        else:
            too = jnp.int32(0)

        def walk(c):
            jj, rs, _ = c
            rs = block(rs, [i * nd - 2 - jj], everyone)
            return jj + 1, rs, _any_alive(rs)

        whole, rsums, _ = lax.while_loop(lambda c: (c[0] < i * nd - 1) & (c[2] > 0), walk,
                                         (jnp.int32(0), rsums, _any_alive(rsums)))
        for n, (h, r0) in enumerate(chains):
            tl_ref[h, r0:r0 + rows, :] = rsums[n]
        nw_ref[0] = (jnp.zeros((8, LANES), jnp.int32) + (2 * whole + too)).astype(F32)
        o_ref[...] = jnp.where(lane < HEAD, acc_ref[0], acc_ref[1]).astype(BF16)
        riders_end()

    r_ops, r_shapes, r_scratch, r_aliases = _riders_plumb(riders, 3, 3)
    outs = _pcall(
        body, name=name, grid=(nhp, nq),
        in_specs=[pl.BlockSpec((1, tq, LANES), lambda hp, i: (4, i, hp)),
                  pl.BlockSpec((1, s, LANES), lambda hp, i: (5, 0, hp)),
                  pl.BlockSpec((1, s, LANES), lambda hp, i: (6, 0, hp))] + [ANY] * len(r_ops),
        out_specs=[pl.BlockSpec((tq, LANES), lambda hp, i: (i, hp)),
                   pl.BlockSpec((2, tq, 1), lambda hp, i: (hp, i, 0)),
                   pl.BlockSpec((1, 8, LANES), lambda hp, i: (hp * nq + i, 0, 0))] + [ANY] * len(r_shapes),
        out_shape=[jax.ShapeDtypeStruct((s, sw), BF16), jax.ShapeDtypeStruct((2 * nhp, s, 1), F32),
                   jax.ShapeDtypeStruct((nhp * nq, 8, LANES), F32)] + r_shapes,
        input_output_aliases=r_aliases,
        scratch_shapes=[pltpu.VMEM((2, tq, LANES), F32)] + r_scratch,
        compiler_params=_cp(2))(proj, proj, proj, *r_ops)
    return outs[0], outs[1], outs[2], list(outs[3:])


def _conv_rows(cc_ref, ch_ref, w_ref, b_ref, r, tc):
    r0 = pl.multiple_of(r * tc, tc)
    u = cc_ref[0, pl.ds(r0, tc), :].astype(F32) * ch_ref[0, pl.ds(r0, tc), :].astype(F32)
    p0 = pl.multiple_of(jnp.maximum(r0 - 16, 0), 16)
    up = cc_ref[0, pl.ds(p0, 16), :].astype(F32) * ch_ref[0, pl.ds(p0, 16), :].astype(F32)
    up = up * (r > 0).astype(F32)
    prev1 = up[15:16, :]
    prev2 = up[14:15, :]
    rid = lax.broadcasted_iota(jnp.int32, u.shape, 0)
    s1 = jnp.where(rid == 0, prev1, pltpu.roll(u, 1, axis=0))
    s2 = jnp.where(rid == 0, prev2, jnp.where(rid == 1, prev1, pltpu.roll(u, 2, axis=0)))
    cv = b_ref[...] + s2 * w_ref[0:1, :] + s1 * w_ref[1:2, :] + u * w_ref[2:3, :]
    return r0, u, s1, s2, cv


def _mix_fwd(proj, ya, conv_w, conv_b, bg, name, riders=()):
    _, s, sw = proj.shape
    nh = sw // LANES
    tc = _tile(s, 256)

    def body(*refs):
        c = pl.program_id(0)
        own, riders_end = _riders_run(riders, refs, 9, 1, 0, c, 2 * nh)
        cb_ref, cc_ref, ch_ref, cz_ref, ya_ref, az_ref, w_ref, b_ref, g_ref, y_ref = own
        gm = _group_mat()

        def finish(r0, yv, zg):
            n = yv * lax.rsqrt(_group_mean(yv * yv, gm) + EPS)
            y_ref[pl.ds(r0, tc), :] = (n * g_ref[...] * (zg * _sigmoid(zg))).astype(BF16)

        @pl.when(c < nh)
        def _():
            def step(r, carry):
                r0, _, _, _, cv = _conv_rows(cc_ref, ch_ref, w_ref, b_ref, r, tc)
                yc = cb_ref[0, pl.ds(r0, tc), :].astype(F32) * cv
                finish(r0, yc, cz_ref[0, pl.ds(r0, tc), :].astype(F32))
                return carry
            lax.fori_loop(0, s // tc, step, 0)

        @pl.when(c >= nh)
        def _():
            def step(r, carry):
                r0 = pl.multiple_of(r * tc, tc)
                finish(r0, ya_ref[pl.ds(r0, tc), :].astype(F32), az_ref[0, pl.ds(r0, tc), :].astype(F32))
                return carry
            lax.fori_loop(0, s // tc, step, 0)

        riders_end()

    def sec(k):
        return pl.BlockSpec((1, s, LANES), lambda c: (k, 0, jnp.minimum(c, nh - 1)))

    r_ops, r_shapes, r_scratch, r_aliases = _riders_plumb(riders, 9, 1)
    outs = _pcall(
        body, name=name, grid=(2 * nh,),
        in_specs=[sec(0), sec(1), sec(2), sec(3),
                  pl.BlockSpec((s, LANES), lambda c: (0, jnp.maximum(c - nh, 0))),
                  pl.BlockSpec((1, s, LANES), lambda c: (7, 0, jnp.maximum(c - nh, 0))),
                  pl.BlockSpec((3, LANES), lambda c: (0, jnp.minimum(c, nh - 1))),
                  pl.BlockSpec((1, LANES), lambda c: (0, jnp.minimum(c, nh - 1))),
                  pl.BlockSpec((1, LANES), lambda c: (0, c))] + [ANY] * len(r_ops),
        out_specs=[pl.BlockSpec((s, LANES), lambda c: (0, c))] + [ANY] * len(r_shapes),
        out_shape=[jax.ShapeDtypeStruct((s, 2 * sw), BF16)] + r_shapes,
        input_output_aliases=r_aliases, scratch_shapes=r_scratch, compiler_params=_cp(1),
    )(proj, proj, proj, proj, ya, proj, conv_w, conv_b, bg, *r_ops)
    return outs[0], list(outs[1:])


def _outproj(y, w, layer, x, g, name):
    s, d = x.shape
    tm = _tile(s, 512)

    def body(y_ref, w_ref, x_ref, g_ref, x1_ref, hn_ref):
        x1 = x_ref[...] + jnp.dot(y_ref[...], w_ref[0], preferred_element_type=F32)
        x1_ref[...] = x1
        r = lax.rsqrt(jnp.mean(x1 * x1, axis=-1, keepdims=True) + EPS)
        hn_ref[...] = (x1 * r * g_ref[...]).astype(BF16)

    row = lambda m: (m, 0)
    fix = lambda m: (0, 0)
    return _pcall(body, name=name, grid=(s // tm,),
                  in_specs=[pl.BlockSpec((tm, d), row), pl.BlockSpec((1, d, d), lambda m: (layer, 0, 0)),
                            pl.BlockSpec((tm, d), row), pl.BlockSpec((1, d), fix)],
                  out_specs=[pl.BlockSpec((tm, d), row), pl.BlockSpec((tm, d), row)],
                  out_shape=[jax.ShapeDtypeStruct((s, d), F32), jax.ShapeDtypeStruct((s, d), BF16)],
                  compiler_params=_cp(1))(y, w, x, g)


def _ple_fwd(hn, w_pg, b_pg, p, w_pe, layer, x1, name):
    s, d = x1.shape
    pd = p.shape[2]
    tm = _tile(s, 512)

    def body(hn_ref, wg_ref, b_ref, p_ref, we_ref, x1_ref, x2_ref, gate_ref, e_ref):
        gate = _sigmoid(jnp.dot(hn_ref[...], wg_ref[0], preferred_element_type=F32) + b_ref[...])
        e = jnp.dot(p_ref[0].astype(BF16), we_ref[0], preferred_element_type=F32)
        x2_ref[...] = x1_ref[...] + gate * e
        gate_ref[...] = gate.astype(BF16)
        e_ref[...] = e.astype(BF16)

    row = lambda m: (m, 0)
    fix = lambda m: (0, 0)
    return _pcall(body, name=name, grid=(s // tm,),
                  in_specs=[pl.BlockSpec((tm, d), row), pl.BlockSpec((1, d, d), lambda m: (layer, 0, 0)),
                            pl.BlockSpec((1, d), fix), pl.BlockSpec((1, tm, pd), lambda m: (layer, m, 0)),
                            pl.BlockSpec((1, pd, d), lambda m: (layer, 0, 0)), pl.BlockSpec((tm, d), row)],
                  out_specs=[pl.BlockSpec((tm, d), row)] * 3,
                  out_shape=[jax.ShapeDtypeStruct((s, d), F32), jax.ShapeDtypeStruct((s, d), BF16),
                             jax.ShapeDtypeStruct((s, d), BF16)],
                  compiler_params=_cp(1))(hn, w_pg, b_pg, p, w_pe, x1)


def _loss_head(x, tgt, g, name):
    s, d = x.shape
    tm = _tile(s, 512)

    def body(x_ref, t_ref, g_ref, l_ref, dx_ref, dg_ref):
        m = pl.program_id(0)

        @pl.when(m == 0)
        def _():
            l_ref[...] = jnp.zeros_like(l_ref)
            dg_ref[...] = jnp.zeros_like(dg_ref)

        xv = x_ref[...]
        gv = g_ref[...]
        r = lax.rsqrt(jnp.mean(xv * xv, axis=-1, keepdims=True) + EPS)
        xn = xv * r
        err = xn * gv - t_ref[...]
        l_ref[...] += jnp.sum(err * err)
        dy = err * (1.0 / d)
        dxn = dy * gv
        dx_ref[...] = r * (dxn - xn * jnp.mean(dxn * xn, axis=-1, keepdims=True))
        dg_ref[...] += _colsum8(dy * xn)

    row = lambda m: (m, 0)
    fix = lambda m: (0, 0)
    return _pcall(body, name=name, grid=(s // tm,),
                  in_specs=[pl.BlockSpec((tm, d), row), pl.BlockSpec((tm, d), row), pl.BlockSpec((1, d), fix)],
                  out_specs=[pl.BlockSpec((8, LANES), fix), pl.BlockSpec((tm, d), row), pl.BlockSpec((8, d), fix)],
                  out_shape=[jax.ShapeDtypeStruct((8, LANES), F32), jax.ShapeDtypeStruct((s, d), F32),
                             jax.ShapeDtypeStruct((8, d), F32)],
                  compiler_params=_cp(1))(x, tgt, g)


def _ple_bwd(dx2, gate, e, x1, w_pg, g_ple, w_out, layer, name, riders=()):
    s, d = dx2.shape
    tm = _tile(s, 512)

    def body(*refs):
        m = pl.program_id(0)
        own, riders_end = _riders_run(riders, refs, 7, 6, 0, m, s // tm)
        (dx2_ref, gate_ref, e_ref, x1_ref, wg_ref, g_ref, wo_ref,
         du_ref, de_ref, dx1_ref, dy_ref, db_ref, dg_ref) = own

        @pl.when(m == 0)
        def _():
            db_ref[...] = jnp.zeros_like(db_ref)
            dg_ref[...] = jnp.zeros_like(dg_ref)

        dx2v = dx2_ref[...]
        gate = gate_ref[...].astype(F32)
        du = dx2v * e_ref[...].astype(F32) * gate * (1.0 - gate)
        de_ref[...] = (dx2v * gate).astype(BF16)
        dub = du.astype(BF16)
        du_ref[...] = dub
        db_ref[...] += _colsum8(du)
        dhn = _dot_nt(dub, wg_ref[0])
        dxr, dgr = _rms_bwd_rows(dhn, x1_ref[...], g_ref[...])
        dx1 = dx2v + dxr
        dx1_ref[...] = dx1
        dg_ref[...] += _colsum8(dgr)
        dy_ref[...] = _dot_nt(dx1.astype(BF16), wo_ref[0]).astype(BF16)
        riders_end()

    row = lambda m: (m, 0)
    fix = lambda m: (0, 0)
    t = pl.BlockSpec((tm, d), row)
    r_ops, r_shapes, r_scratch, r_aliases = _riders_plumb(riders, 7, 6)
    outs = _pcall(body, name=name, grid=(s // tm,),
                  in_specs=[t, t, t, t, pl.BlockSpec((1, d, d), lambda m: (layer, 0, 0)), pl.BlockSpec((1, d), fix),
                            pl.BlockSpec((1, d, d), lambda m: (layer, 0, 0))] + [ANY] * len(r_ops),
                  out_specs=[t, t, t, t, pl.BlockSpec((8, d), fix), pl.BlockSpec((8, d), fix)] + [ANY] * len(r_shapes),
                  out_shape=[jax.ShapeDtypeStruct((s, d), BF16), jax.ShapeDtypeStruct((s, d), BF16),
                             jax.ShapeDtypeStruct((s, d), F32), jax.ShapeDtypeStruct((s, d), BF16),
                             jax.ShapeDtypeStruct((8, d), F32), jax.ShapeDtypeStruct((8, d), F32)] + r_shapes,
                  input_output_aliases=r_aliases, scratch_shapes=r_scratch,
                  compiler_params=_cp(1))(dx2, gate, e, x1, w_pg, g_ple, w_out, *r_ops)
    return tuple(outs[:6]) + (list(outs[6:]),)


def _mm_tn(a, b, name, a_layer=None):
    s, ka = a.shape[-2:]
    n = b.shape[1]
    tn = _tile(n, 1024)
    ns = n // tn
    tk = _tile(s, 512)
    nk = s // tk

    def body(a_ref, b_ref, o_ref, acc_ref):
        k = pl.program_id(1)

        @pl.when(k == 0)
        def _():
            acc_ref[...] = jnp.zeros_like(acc_ref)

        av = a_ref[...] if a_layer is None else a_ref[0]
        acc_ref[...] += _dot_tn(av.astype(BF16), b_ref[...].astype(BF16))

        @pl.when(k == nk - 1)
        def _():
            o_ref[...] = acc_ref[...]

    a_spec = (pl.BlockSpec((tk, ka), lambda j, k: (k, 0)) if a_layer is None
              else pl.BlockSpec((1, tk, ka), lambda j, k: (a_layer, k, 0)))
    return _pcall(body, name=name, grid=(ns, nk),
                  in_specs=[a_spec, pl.BlockSpec((tk, tn), lambda j, k: (k, j))],
                  out_specs=pl.BlockSpec((ka, tn), lambda j, k: (0, j)),
                  out_shape=jax.ShapeDtypeStruct((ka, n), F32),
                  scratch_shapes=[pltpu.VMEM((ka, tn), F32)], compiler_params=_cp(2))(a, b)


def _norm_gate_bwd(dy, yv, zg, g, gm):
    r = lax.rsqrt(_group_mean(yv * yv, gm) + EPS)
    n = yv * r
    sg = _sigmoid(zg)
    sil = zg * sg
    dzg = dy * n * g * (sg * (1.0 + zg * (1.0 - sg)))
    dn = dy * g * sil
    dyv = r * (dn - n * _group_mean(dn * n, gm))
    return dyv, dzg, dy * n * sil


def _convmix_bwd(dy, proj, conv_w, conv_b, bg, name, riders=()):
    _, s, sw = proj.shape
    nh = sw // LANES
    tc = _tile(s, 256)
    nr = s // tc

    def body(*refs):
        own, riders_end = _riders_run(riders, refs, 8, 4, 1, pl.program_id(0), nh)
        (dy_ref, cb_ref, cc_ref, ch_ref, cz_ref, w_ref, b_ref, g_ref,
         dp_ref, dw_ref, db_ref, dg_ref, dcv_ref) = own
        gm = _group_mat()
        dcv_ref[pl.ds(s, 8), :] = jnp.zeros((8, LANES), F32)

        def pass1(r, carry):
            dw0, dw1, dw2, db, dg = carry
            r0, u, s1, s2, cv = _conv_rows(cc_ref, ch_ref, w_ref, b_ref, r, tc)
            cb = cb_ref[0, pl.ds(r0, tc), :].astype(F32)
            dyc, dcz, dgr = _norm_gate_bwd(dy_ref[pl.ds(r0, tc), :].astype(F32), cb * cv,
                                           cz_ref[0, pl.ds(r0, tc), :].astype(F32), g_ref[...], gm)
            dp_ref[0, pl.ds(r0, tc), :] = (dyc * cv).astype(BF16)
            dp_ref[3, pl.ds(r0, tc), :] = dcz.astype(BF16)
            dcv = dyc * cb
            dcv_ref[pl.ds(r0, tc), :] = dcv
            return (dw0 + _colsum8(dcv * s2), dw1 + _colsum8(dcv * s1), dw2 + _colsum8(dcv * u),
                    db + _colsum8(dcv), dg + _colsum8(dgr))

        z8 = jnp.zeros((8, LANES), F32)
        dw0, dw1, dw2, db, dg = lax.fori_loop(0, nr, pass1, (z8, z8, z8, z8, z8))
        dw_ref[0] = dw0
        dw_ref[1] = dw1
        dw_ref[2] = dw2
        db_ref[...] = db
        dg_ref[...] = dg

        def pass2(r, carry):
            r0 = pl.multiple_of(r * tc, tc)
            dcv = dcv_ref[pl.ds(r0, tc), :]
            nxt = dcv_ref[pl.ds(pl.multiple_of(r0 + tc, 8), 8), :]
            rid = lax.broadcasted_iota(jnp.int32, dcv.shape, 0)
            n1 = jnp.where(rid == tc - 1, nxt[0:1, :], pltpu.roll(dcv, tc - 1, axis=0))
            n2 = jnp.where(rid == tc - 1, nxt[1:2, :],
                           jnp.where(rid == tc - 2, nxt[0:1, :], pltpu.roll(dcv, tc - 2, axis=0)))
            du = dcv * w_ref[2:3, :] + n1 * w_ref[1:2, :] + n2 * w_ref[0:1, :]
            dp_ref[1, pl.ds(r0, tc), :] = (du * ch_ref[0, pl.ds(r0, tc), :].astype(F32)).astype(BF16)
            dp_ref[2, pl.ds(r0, tc), :] = (du * cc_ref[0, pl.ds(r0, tc), :].astype(F32)).astype(BF16)
            return carry

        lax.fori_loop(0, nr, pass2, 0)
        riders_end()

    def sec(k):
        return pl.BlockSpec((1, s, LANES), lambda c: (k, 0, c))

    col = lambda c: (0, c)
    r_ops, r_shapes, r_scratch, r_aliases = _riders_plumb(riders, 8, 4)
    outs = _pcall(
        body, name=name, grid=(nh,),
        in_specs=[pl.BlockSpec((s, LANES), col), sec(0), sec(1), sec(2), sec(3),
                  pl.BlockSpec((3, LANES), col), pl.BlockSpec((1, LANES), col), pl.BlockSpec((1, LANES), col)]
        + [ANY] * len(r_ops),
        out_specs=[pl.BlockSpec((4, s, LANES), lambda c: (0, 0, c)), pl.BlockSpec((3, 8, LANES), lambda c: (0, 0, c)),
                   pl.BlockSpec((8, LANES), col), pl.BlockSpec((8, LANES), col)] + [ANY] * len(r_shapes),
        out_shape=[jax.ShapeDtypeStruct((8, s, sw), BF16), jax.ShapeDtypeStruct((3, 8, sw), F32),
                   jax.ShapeDtypeStruct((8, sw), F32), jax.ShapeDtypeStruct((8, sw), F32)] + r_shapes,
        input_output_aliases=r_aliases,
        scratch_shapes=[pltpu.VMEM((s + 8, LANES), F32)] + r_scratch, compiler_params=_cp(1),
    )(dy, proj, proj, proj, proj, conv_w, conv_b, bg, *r_ops)
    return tuple(outs[:4]) + (list(outs[4:]),)


def _attn_bwd(proj, dy, ya, tl, walked, bg, buf, name, riders=()):
    _, s, sw = proj.shape
    nhp = sw // LANES
    tk, t, nd, rows_c = _attn_tiles(s)
    nq = s // t
    scale = 1.0 / math.sqrt(HEAD)

    def body(*refs):
        step = pl.program_id(1)
        i = nq - 1 - step
        own, riders_end = _riders_run(riders, refs, 10, 2, 3, pl.program_id(0) * nq + step, nhp * nq)
        (q_ref, k_ref, v_ref, az_ref, dy_ref, ya_ref, tl_ref, nw_ref, g_ref, buf_ref, out_ref, dg_ref,
         dka_ref, dva_ref, dqa_ref) = own

        @pl.when(step == 0)
        def _():
            dka_ref[...] = jnp.zeros_like(dka_ref)
            dva_ref[...] = jnp.zeros_like(dva_ref)
            dg_ref[...] = jnp.zeros_like(dg_ref)

        dyv, dzg, dgr = _norm_gate_bwd(dy_ref[...].astype(F32), ya_ref[...].astype(F32), az_ref[0].astype(F32),
                                       g_ref[...], _group_mat())
        out_ref[3] = dzg.astype(BF16)
        dg_ref[...] += _colsum8(dgr)

        tri = (lax.broadcasted_iota(jnp.int32, (tk, tk), 0) <=
               lax.broadcasted_iota(jnp.int32, (tk, tk), 1)).astype(BF16)
        lane = lax.broadcasted_iota(jnp.int32, (t, LANES), 1)
        q = q_ref[0] * jnp.asarray(scale, BF16)
        do = dyv.astype(BF16)
        qms = [jnp.where((lane // HEAD) == h, q, jnp.zeros_like(q)) for h in range(2)]
        doms = [jnp.where((lane // HEAD) == h, do, jnp.zeros_like(do)) for h in range(2)]
        dqa_ref[...] = jnp.zeros_like(dqa_ref)
        chains = [(h, r0) for h in range(2) for r0 in range(0, t, rows_c)]
        qparts = [qms[h][r0:r0 + rows_c] for h, r0 in chains]
        doparts = [doms[h][r0:r0 + rows_c] for h, r0 in chains]
        tots = [tl_ref[h, r0:r0 + rows_c, :] for h, r0 in chains]

        def block(carry, tiles, items):
            k0s = [pl.multiple_of(j * tk, tk) for j in tiles]
            kjs = [k_ref[0, pl.ds(k0, tk), :] for k0 in k0s]
            vjs = [v_ref[0, pl.ds(k0, tk), :] for k0 in k0s]
            zs = [_dot_nt(qparts[n], kjs[t][:kw]) for n, t, kw, _, _ in items]
            das = [_dot_nt(doparts[n], vjs[t][:kw]) for n, t, kw, _, _ in items]
            keeps = [_both(mask, gate) for _, _, _, mask, gate in items]
            lms, lss, cls = [], [], []
            for z, keep, (n, t, kw, _, _) in zip(zs, keeps, items):
                lm, ls = _softplus_parts(z)
                if keep is not None:
                    lm = jnp.where(keep, lm, 0.0)
                lms.append(lm)
                lss.append(ls)
                cls.append(_split_dot(lm, tri[:kw, :kw], 2))
            cur = list(carry)
            psums, abs_, gs, cgs = [], [], [], []
            for lm, ls, cl, da, keep, (n, t, kw, _, _) in zip(lms, lss, cls, das, keeps, items):
                psum, gsum = cur[n]
                a = jnp.exp(ls + (tots[n] - psum - cl))
                if keep is not None:
                    a = jnp.where(keep, a, 0.0)
                g = a * da
                psums.append(gsum)
                gs.append(g)
                abs_.append(a.astype(BF16))
                cgs.append(_split_dot(g, tri[:kw, :kw], 1))
                cur[n] = (psum + jnp.sum(lm, axis=1, keepdims=True), gsum + jnp.sum(g, axis=1, keepdims=True))
            dks, dvs = {}, {}
            for ls, a, g, cg, gsum, keep, (n, t, kw, _, _) in zip(lss, abs_, gs, cgs, psums, keeps, items):
                h, r0 = chains[n]
                dz = g - jnp.exp(ls) * (gsum + cg)
                if keep is not None:
                    dz = jnp.where(keep, dz, 0.0)
                dz = dz.astype(BF16)
                dqa_ref[h, r0:r0 + rows_c, :] += jnp.dot(dz, kjs[t][:kw], preferred_element_type=F32)
                dkh = _dot_tn(dz, qparts[n])
                dvh = _dot_tn(a, doparts[n])
                dks[t, kw] = dkh if (t, kw) not in dks else dks[t, kw] + dkh
                dvs[t, kw] = dvh if (t, kw) not in dvs else dvs[t, kw] + dvh
            for t, kw in dks:
                dka_ref[pl.ds(k0s[t], kw), :] += dks[t, kw]
                dva_ref[pl.ds(k0s[t], kw), :] += dvs[t, kw]
            return tuple(cur)

        z1 = jnp.zeros((rows_c, 1), F32)
        everyone = [(n, 0, tk, None, None) for n in range(len(chains))]
        upper = [n for n, (_, r0) in enumerate(chains) if r0 >= tk]
        lower = [n for n, (_, r0) in enumerate(chains) if r0 < tk]
        left = jnp.maximum(i * nd - 1, 0)
        code = jnp.clip(jnp.max(nw_ref[0].astype(jnp.int32)), 0, 2 * left + 1)
        too = jnp.where(i > 0, code % 2, 0)
        whole = jnp.minimum(code // 2, left)
        carry = lax.fori_loop(left - whole, left, lambda j, c: block(c, [j], everyone), ((z1, z1),) * len(chains))
        if upper:
            carry = lax.cond(too > 0, lambda c: block(c, [left], [(n, 0, tk, None, None) for n in upper]),
                             lambda c: c, carry)
        carry = block(carry, [left, i * nd],
                      [(n, 0, tk, None, i > 0) for n in lower]
                      + [(n, 1, kw, m, None) for n, kw, m in _diag_work(chains, 0, rows_c, tk)])
        for d in range(1, nd):
            carry = block(carry, [i * nd + d], [(n, 0, kw, m, None) for n, kw, m in _diag_work(chains, d, rows_c, tk)])
        out_ref[0] = (jnp.where(lane < HEAD, dqa_ref[0], dqa_ref[1]) * scale).astype(BF16)
        own = pl.multiple_of(i * t, t)
        out_ref[1] = dka_ref[pl.ds(own, t), :].astype(BF16)
        out_ref[2] = dva_ref[pl.ds(own, t), :].astype(BF16)
        riders_end()

    def rows(sec):
        return pl.BlockSpec((1, t, LANES), lambda hp, st: (sec, nq - 1 - st, hp))

    def whole(sec):
        return pl.BlockSpec((1, s, LANES), lambda hp, st: (sec, 0, hp))

    r_ops, r_shapes, r_scratch, r_aliases = _riders_plumb(riders, 10, 2)
    outs = _pcall(
        body, name=name, grid=(nhp, nq),
        in_specs=[rows(4), whole(5), whole(6), rows(7),
                  pl.BlockSpec((t, LANES), lambda hp, st: (nq - 1 - st, hp + nhp)),
                  pl.BlockSpec((t, LANES), lambda hp, st: (nq - 1 - st, hp)),
                  pl.BlockSpec((2, t, 1), lambda hp, st: (hp, nq - 1 - st, 0)),
                  pl.BlockSpec((1, 8, LANES), lambda hp, st: (hp * nq + nq - 1 - st, 0, 0)),
                  pl.BlockSpec((1, LANES), lambda hp, st: (0, hp + nhp)), ANY] + [ANY] * len(r_ops),
        out_specs=[pl.BlockSpec((4, t, LANES), lambda hp, st: (1, nq - 1 - st, hp)),
                   pl.BlockSpec((8, LANES), lambda hp, st: (0, hp))] + [ANY] * len(r_shapes),
        out_shape=[jax.ShapeDtypeStruct(buf.shape, buf.dtype), jax.ShapeDtypeStruct((8, sw), F32)] + r_shapes,
        input_output_aliases={9: 0, **r_aliases},
        scratch_shapes=[pltpu.VMEM((s, LANES), F32), pltpu.VMEM((s, LANES), F32), pltpu.VMEM((2, t, LANES), F32)]
        + r_scratch,
        compiler_params=_cp(2))(proj, proj, proj, proj, dy, ya, tl, walked, bg, buf, *r_ops)
    return outs[0], outs[1], list(outs[2:])


def _grad_w_in(h, dproj, name):
    s, d = h.shape
    ns, _, sw = dproj.shape

    def body(h_ref, b_ref, o_ref, ht_ref):
        @pl.when(pl.program_id(0) == 0)
        def _():
            ht_ref[...] = h_ref[...].T

        o_ref[...] = jnp.dot(ht_ref[...], b_ref[0], preferred_element_type=F32)

    return _pcall(body, name=name, grid=(ns,),
                  in_specs=[pl.BlockSpec((s, d), lambda j: (0, 0)), pl.BlockSpec((1, s, sw), lambda j: (j, 0, 0))],
                  out_specs=pl.BlockSpec((d, sw), lambda j: (0, j)),
                  out_shape=jax.ShapeDtypeStruct((d, ns * sw), F32),
                  scratch_shapes=[pltpu.VMEM((d, s), BF16)], compiler_params=_cp(1))(h, dproj)


def _inproj_bwd(dproj, w, layer, x, g, dx1, name, riders=()):
    ns, s, sw = dproj.shape
    d = x.shape[1]
    tm = _tile(s, 512)

    def body(*refs):
        own, riders_end = _riders_run(riders, refs, 5, 2, 0, pl.program_id(0), s // tm)
        dp_ref, w_ref, x_ref, g_ref, dx1_ref, dx_ref, dg_ref = own

        @pl.when(pl.program_id(0) == 0)
        def _():
            dg_ref[...] = jnp.zeros_like(dg_ref)

        dh = _dot_nt(dp_ref[0], w_ref[0, :, 0:sw])
        for k in range(1, ns):
            dh = dh + _dot_nt(dp_ref[k], w_ref[0, :, k * sw:(k + 1) * sw])
        dxr, dgr = _rms_bwd_rows(dh, x_ref[...], g_ref[...])
        dx_ref[...] = dx1_ref[...] + dxr
        dg_ref[...] += _colsum8(dgr)
        riders_end()

    row = lambda m: (m, 0)
    fix = lambda m: (0, 0)
    r_ops, r_shapes, r_scratch, r_aliases = _riders_plumb(riders, 5, 2)
    outs = _pcall(body, name=name, grid=(s // tm,),
                  in_specs=[pl.BlockSpec((ns, tm, sw), lambda m: (0, m, 0)),
                            pl.BlockSpec((1, d, ns * sw), lambda m: (layer, 0, 0)),
                            pl.BlockSpec((tm, d), row), pl.BlockSpec((1, d), fix), pl.BlockSpec((tm, d), row)]
                  + [ANY] * len(r_ops),
                  out_specs=[pl.BlockSpec((tm, d), row), pl.BlockSpec((8, d), fix)] + [ANY] * len(r_shapes),
                  out_shape=[jax.ShapeDtypeStruct((s, d), F32), jax.ShapeDtypeStruct((8, d), F32)] + r_shapes,
                  input_output_aliases=r_aliases, scratch_shapes=r_scratch,
                  compiler_params=_cp(1))(dproj, w, x, g, dx1, *r_ops)
    return outs[0], outs[1], list(outs[2:])


def _adamw(w, g, m, v, name):
    r, c = w.shape
    tr = _tile(r, 256)
    c1 = 1.0 - ADAM_B1 ** ADAM_STEP
    c2 = 1.0 - ADAM_B2 ** ADAM_STEP

    def body(w_ref, g_ref, m_ref, v_ref, go_ref, d_ref, mo_ref, vo_ref):
        gv = g_ref[...]
        go_ref[...] = gv
        mn = ADAM_B1 * m_ref[...] + (1.0 - ADAM_B1) * gv
        vn = ADAM_B2 * v_ref[...] + (1.0 - ADAM_B2) * (gv * gv)
        d_ref[...] = -ADAM_LR * ((mn / c1) / (jnp.sqrt(vn / c2) + ADAM_EPS) + ADAM_WD * w_ref[...])
        mo_ref[...] = mn
        vo_ref[...] = vn

    t = pl.BlockSpec((tr, c), lambda i: (i, 0))
    return _pcall(body, name=name, grid=(r // tr,), in_specs=[t] * 4, out_specs=[t] * 4,
                  out_shape=[jax.ShapeDtypeStruct((r, c), F32)] * 4, compiler_params=_cp(1))(w, g, m, v)


def _add_half(grad, other, core, a, name):
    hr, hc = other.shape
    tr = _tile(hr, 256)
    nb = hr // tr

    def body(c_ref, g_ref, o_ref, out_ref, outb_ref):
        v = g_ref[...] + o_ref[...]
        out_ref[...] = v
        outb_ref[...] = v.astype(BF16)

    t = pl.BlockSpec((tr, hc), lambda i, c: (i, 0))
    own = (lambda i, c: (c[0] * nb + i, 0)) if HALF_AXES[a] == 0 else (lambda i, c: (i, c[0]))
    grid_spec = pltpu.PrefetchScalarGridSpec(
        num_scalar_prefetch=1, grid=(nb,), in_specs=[pl.BlockSpec((tr, hc), own), t], out_specs=[t, t])
    return _pcall(body, name=name, grid_spec=grid_spec,
                  out_shape=[jax.ShapeDtypeStruct((hr, hc), F32), jax.ShapeDtypeStruct((hr, hc), BF16)],
                  compiler_params=_cp(1))(core.reshape(1).astype(jnp.int32), grad, other)


def _sum_half(wide, parts, chip, core, layer, a, stack, name):
    _, sr, sc = parts.shape
    tr = _tile(sr, 256)
    nbs = sr // tr

    def body(k_ref, f_ref, p_ref, *rest):
        rest[-1][0] = ((f_ref[...] + p_ref[0].astype(F32)) + p_ref[1].astype(F32)) + p_ref[2].astype(F32)

    f_map = (lambda i, k: (i, k[0])) if SHARD_AXES[a] == 1 else (lambda i, k: (k[0] * nbs + i, 0))
    if HALF_AXES[a] == 0:
        shape, o_map = (DEPTH, 2 * sr, sc), (lambda i, k: (layer, k[1] * nbs + i, 0))
    else:
        shape, o_map = (DEPTH, sr, 2 * sc), (lambda i, k: (layer, i, k[1]))
    in_specs = [pl.BlockSpec((tr, sc), f_map), pl.BlockSpec((3, tr, sc), lambda i, k: (0, i, 0))]
    args = [wide, parts]
    aliases = {}
    if stack is not None:
        in_specs.append(ANY)
        args.append(stack)
        aliases = {3: 0}
    grid_spec = pltpu.PrefetchScalarGridSpec(
        num_scalar_prefetch=1, grid=(nbs,), in_specs=in_specs, out_specs=pl.BlockSpec((1, tr, sc), o_map))
    return _pcall(body, name=name, grid_spec=grid_spec, out_shape=jax.ShapeDtypeStruct(shape, F32),
                  input_output_aliases=aliases,
                  compiler_params=_cp(1))(jnp.stack([chip, core]).astype(jnp.int32), *args)


def _sum_slots(slots, name):
    n = slots.shape[0]

    def body(s_ref, o_ref):
        acc = s_ref[0]
        for i in range(1, n):
            acc = acc + s_ref[i]
        o_ref[...] = acc

    return _pcall(body, name=name, out_shape=jax.ShapeDtypeStruct(slots.shape[1:], F32))(slots)


def _place():
    return lax.axis_index("x"), lax.axis_index("y"), lax.axis_index("c")


def _shard_view(ref, axis, chip, size):
    if axis == 0:
        return ref.at[pl.ds(chip * size, size), :]
    return ref.at[:, pl.ds(chip * size, size)]


SHARD_AXES = (1, 0, 0, 1)
HALF_AXES = tuple(1 - ax for ax in SHARD_AXES)


class _Rider:
    def __init__(self, operands, out_shape, sems, phases, aliased=False):
        self.operands, self.out_shape, self.sems = list(operands), list(out_shape), list(sems)
        self.phases, self.aliased = phases, aliased


def _riders_plumb(riders, n_in, n_out):
    ops, out_shape, scratch, aliases = [], [], [], {}
    for r in riders:
        if r.aliased:
            for k in range(len(r.operands)):
                aliases[n_in + len(ops) + k] = n_out + len(out_shape) + k
        ops += r.operands
        out_shape += r.out_shape
        scratch += r.sems
    return ops, out_shape, scratch, aliases


def _riders_run(riders, refs, n_in, n_out, n_scr, step, nsteps):
    n_rin = sum(len(r.operands) for r in riders)
    n_rout = sum(len(r.out_shape) for r in riders)
    rin = refs[n_in:n_in + n_rin]
    o0 = n_in + n_rin
    rout = refs[o0 + n_out:o0 + n_out + n_rout]
    s0 = o0 + n_out + n_rout
    rsem = refs[s0 + n_scr:]
    own = list(refs[:n_in]) + list(refs[o0:o0 + n_out]) + list(refs[s0:s0 + n_scr])
    lasts = []
    for r in riders:
        ph = r.phases(rin[:len(r.operands)], rout[:len(r.out_shape)], rsem[:len(r.sems)])
        rin, rout, rsem = rin[len(r.operands):], rout[len(r.out_shape):], rsem[len(r.sems):]
        pl.when(step == 0)(ph[0])
        for mid in ph[1:-1]:
            pl.when(step == (3 * nsteps) // 4)(mid)
        lasts.append(ph[-1])

    def finish():
        for last in lasts:
            pl.when(step == nsteps - 1)(last)

    return own, finish


def _gather_phases(ins, outs, ssem, rsem, layer, which):
    n = len(ins)
    x, y, c = _place()
    me = 2 * x + y
    chips = [(1 - x, y), (x, 1 - y), (1 - x, 1 - y)]

    def piece(a, chip, half, of):
        ax = SHARD_AXES[which[a]]
        block = _shard_view(of[a].at[layer], ax, chip, of[a].shape[1 + ax] // 4)
        r = block.shape[0] // 2
        return block.at[pl.ds(half * r, r), :]

    def over_ici(a, j):
        cx, cy = chips[j]
        return pltpu.make_async_remote_copy(
            src_ref=piece(a, me, c, ins), dst_ref=piece(a, me, c, outs), send_sem=ssem.at[a, j],
            recv_sem=rsem.at[a, j], device_id=(cx, cy, c), device_id_type=MESH)

    def landed(a, j, half):
        cx, cy = chips[j]
        return piece(a, 2 * cx + cy, half, outs)

    def to_sibling(a, j):
        got = landed(a, j, c)
        return pltpu.make_async_remote_copy(
            src_ref=got, dst_ref=got, send_sem=ssem.at[a, 3 + j], recv_sem=rsem.at[a, 3 + j],
            device_id=(x, y, 1 - c), device_id_type=MESH)

    def wait_arrival(a, k, place):
        pltpu.make_async_remote_copy(src_ref=place, dst_ref=place, send_sem=ssem.at[a, k], recv_sem=rsem.at[a, k],
                                     device_id=(x, y, c), device_id_type=MESH).wait_recv()

    def start():
        for a in range(n):
            for j in range(3):
                over_ici(a, j).start()

    def pass_on():
        for a in range(n):
            for j in range(3):
                wait_arrival(a, j, landed(a, j, c))
                to_sibling(a, j).start()

    def finish():
        for a in range(n):
            for j in range(3):
                wait_arrival(a, 3 + j, landed(a, j, 1 - c))
        for a in range(n):
            for j in range(3):
                over_ici(a, j).wait_send()
                to_sibling(a, j).wait_send()

    return start, pass_on, finish


def _gather_rider(fulls, layer, which):
    n = len(fulls)
    return _Rider(fulls, [jax.ShapeDtypeStruct(f.shape, f.dtype) for f in fulls],
                  [pltpu.SemaphoreType.DMA((n, 6)), pltpu.SemaphoreType.DMA((n, 6))],
                  lambda ins, outs, sems: _gather_phases(ins, outs, sems[0], sems[1], layer, which), aliased=True)


def _ride_alone(rider, name):
    n = len(rider.operands)

    def body(*refs):
        for phase in rider.phases(refs[:n], refs[n:n + len(rider.out_shape)], refs[n + len(rider.out_shape):]):
            phase()

    return _pcall(body, name=name, in_specs=[ANY] * n, out_specs=[ANY] * len(rider.out_shape),
                  out_shape=rider.out_shape, scratch_shapes=rider.sems,
                  input_output_aliases={a: a for a in range(n)} if rider.aliased else {})(*rider.operands)


def _half_view(ref, a, half):
    n = ref.shape[HALF_AXES[a]] // 2
    if HALF_AXES[a] == 0:
        return ref.at[pl.ds(half * n, n), :]
    return ref.at[:, pl.ds(half * n, n)]


def _swap_rider(grads, which):
    n = len(grads)
    halves = []
    for g, w in zip(grads, which):
        sh = list(g.shape)
        sh[HALF_AXES[w]] //= 2
        halves.append(jax.ShapeDtypeStruct(tuple(sh), g.dtype))

    def phases(srcs, outs, sems):
        x, y, c = _place()

        def copy(a):
            return pltpu.make_async_remote_copy(
                src_ref=_half_view(srcs[a], which[a], 1 - c), dst_ref=outs[a], send_sem=sems[0].at[a],
                recv_sem=sems[1].at[a], device_id=(x, y, 1 - c), device_id_type=MESH)

        def start():
            for a in range(n):
                copy(a).start()

        def finish():
            for a in range(n):
                copy(a).wait()

        return start, finish

    return _Rider(grads, halves, [pltpu.SemaphoreType.DMA((n,)), pltpu.SemaphoreType.DMA((n,))], phases)


def _scatter_rider(sums, which):
    n = len(sums)
    shapes = []
    for f, w in zip(sums, which):
        sh = list(f.shape)
        sh[SHARD_AXES[w]] //= 4
        shapes.append(jax.ShapeDtypeStruct((3,) + tuple(sh), f.dtype))

    def phases(srcs, outs, sems):
        x, y, c = _place()
        chips = [(1 - x, y), (x, 1 - y), (1 - x, 1 - y)]

        def copy(a, j):
            cx, cy = chips[j]
            ax = SHARD_AXES[which[a]]
            src = _shard_view(srcs[a], ax, 2 * cx + cy, srcs[a].shape[ax] // 4)
            return pltpu.make_async_remote_copy(src_ref=src, dst_ref=outs[a].at[j], send_sem=sems[0].at[a, j],
                                                recv_sem=sems[1].at[a, j], device_id=(cx, cy, c), device_id_type=MESH)

        def start():
            for a in range(n):
                for j in range(3):
                    copy(a, j).start()

        def finish():
            for a in range(n):
                for j in range(3):
                    copy(a, j).wait()

        return start, finish

    return _Rider(sums, shapes, [pltpu.SemaphoreType.DMA((n, 3)), pltpu.SemaphoreType.DMA((n, 3))], phases)


def _pair_halves(stacks):
    n = len(stacks)

    def body(*refs):
        ins, outs = refs[:n], refs[n:2 * n]
        ssem, rsem = refs[2 * n:]
        x, y, c = _place()
        cps = [pltpu.make_async_remote_copy(
            src_ref=_half_view(ins[a].at[l], a, c), dst_ref=_half_view(outs[a].at[l], a, c), send_sem=ssem.at[a, l],
            recv_sem=rsem.at[a, l], device_id=(x, y, 1 - c), device_id_type=MESH)
            for a in range(n) for l in range(DEPTH)]
        for cp in cps:
            cp.start()
        for a in range(n):
            for l in range(DEPTH):
                got = _half_view(outs[a].at[l], a, 1 - c)
                pltpu.make_async_remote_copy(src_ref=got, dst_ref=got, send_sem=ssem.at[a, l], recv_sem=rsem.at[a, l],
                                             device_id=(x, y, 1 - c), device_id_type=MESH).wait_recv()
        for cp in cps:
            cp.wait_send()

    return _pcall(body, name="pair_halves", in_specs=[ANY] * n, out_specs=[ANY] * n,
                  out_shape=[jax.ShapeDtypeStruct(st.shape, st.dtype) for st in stacks],
                  input_output_aliases={a: a for a in range(n)},
                  scratch_shapes=[pltpu.SemaphoreType.DMA((n, DEPTH)), pltpu.SemaphoreType.DMA((n, DEPTH))])(*stacks)


class _GradReduce:
    def __init__(self, chip, core):
        self.chip, self.core = chip, core
        self.stacks = [None] * len(SHARD_AXES)

    def add(self, layer, grads, which, got):
        return [(layer, w) + tuple(_add_half(g, o, self.core, w, f"add_half_{layer}_{w}"))
                for g, o, w in zip(grads, got, which)]

    def finish(self, sums, partials):
        for (layer, w, wide, _), pr in zip(sums, partials):
            self.stacks[w] = _sum_half(wide, pr, self.chip, self.core, layer, w, self.stacks[w], f"sum_half_{layer}_{w}")

    def result(self):
        return _pair_halves(self.stacks)


def _exchange_small(pack, name):
    nd = 8

    def body(p_ref, o_ref, ssem, rsem):
        x, y, c = _place()
        me = 4 * x + 2 * y + c
        o_ref[me] = p_ref[...]
        cps = []
        for j in range(1, nd):
            px, py, pc = x ^ (j >> 2), y ^ ((j >> 1) & 1), c ^ (j & 1)
            cps.append(pltpu.make_async_remote_copy(
                src_ref=p_ref, dst_ref=o_ref.at[me], send_sem=ssem.at[j - 1], recv_sem=rsem.at[j - 1],
                device_id=(px, py, pc), device_id_type=MESH))
        for cp in cps:
            cp.start()
        for j in range(1, nd):
            peer = me ^ j
            got = o_ref.at[peer]
            pltpu.make_async_remote_copy(src_ref=got, dst_ref=got, send_sem=ssem.at[j - 1], recv_sem=rsem.at[j - 1],
                                         device_id=(x, y, c), device_id_type=MESH).wait_recv()
        for cp in cps:
            cp.wait_send()

    vm = pl.BlockSpec(memory_space=pltpu.VMEM)
    return _pcall(body, name=name, in_specs=[vm], out_specs=vm,
                  out_shape=jax.ShapeDtypeStruct((nd,) + pack.shape, pack.dtype),
                  scratch_shapes=[pltpu.SemaphoreType.DMA((nd - 1,)), pltpu.SemaphoreType.DMA((nd - 1,))])(pack)


def _row(v):
    return v.reshape(1, -1)


def _local_step(x, p, tgt, norm_g, conv_w, conv_b, branch_g, ple_norm_g, b_pg, final_g, w_in, w_out, w_pg, w_pe,
                gather=False, reduce=None):
    saved = []
    xl = x
    for l in range(DEPTH):
        riders = [_gather_rider([w_out, w_pg, w_pe], 0, [1, 2, 3])] if gather and l == 0 else []
        h, proj, got = _inproj(xl, _row(norm_g[l]), w_in, l, f"inproj_{l}", riders)
        if riders:
            w_out, w_pg, w_pe = got
        later = gather and l + 1 < DEPTH
        riders = [_gather_rider([w_in], l + 1, [0])] if later else []
        ya, tl, walked, got = _attn_fwd(proj, f"attn_fwd_{l}", riders)
        if riders:
            w_in, = got
        riders = [_gather_rider([w_out, w_pg, w_pe], l + 1, [1, 2, 3])] if later else []
        y, got = _mix_fwd(proj, ya, conv_w[l], _row(conv_b[l]), _row(branch_g[l]), f"mix_fwd_{l}", riders)
        if riders:
            w_out, w_pg, w_pe = got
        x1, hn = _outproj(y, w_out, l, xl, _row(ple_norm_g[l]), f"outproj_{l}")
        x2, gate, e = _ple_fwd(hn, w_pg, _row(b_pg[l]), p, w_pe, l, x1, f"ple_fwd_{l}")
        saved.append((xl, h, proj, ya, tl, walked, y, x1, hn, gate, e))
        xl = x2

    sq, dx, d_final = _loss_head(xl, tgt, _row(final_g), "loss_head")

    big = [None] * DEPTH
    carried = []
    small = {k: [None] * DEPTH for k in ("norm_g", "conv_w", "conv_b", "branch_g", "ple_norm_g", "b_pg")}
    for l in reversed(range(DEPTH)):
        xl, h, proj, ya, tl, walked, y, x1, hn, gate, e = saved[l]
        du, de, dx1, dy, db_pg, d_ple, _ = _ple_bwd(dx, gate, e, x1, w_pg, _row(ple_norm_g[l]), w_out, l,
                                                    f"ple_bwd_{l}")
        sums = carried
        g_pg = _mm_tn(hn, du, f"grad_w_pg_{l}")
        g_pe = _mm_tn(p, de, f"grad_w_pe_{l}", a_layer=l)
        g_out = _mm_tn(y, dx1, f"grad_w_out_{l}")
        others = [g_out, g_pg, g_pe]
        riders = [_swap_rider(others, [1, 2, 3])] if reduce is not None else []
        dpc, d_cw, d_cb, d_bg_c, got = _convmix_bwd(dy, proj, conv_w[l], _row(conv_b[l]), _row(branch_g[l]),
                                                    f"convmix_bwd_{l}", riders)
        if reduce is not None:
            sums += reduce.add(l, others, [1, 2, 3], got)
        riders = [_scatter_rider([sm[3] for sm in sums], [sm[1] for sm in sums])] if sums else []
        dproj, d_bg_a, got = _attn_bwd(proj, dy, ya, tl, walked, _row(branch_g[l]), dpc, f"attn_bwd_{l}", riders)
        if sums:
            reduce.finish(sums, got)
        g_in = _grad_w_in(h, dproj, f"grad_w_in_{l}")
        riders = [_swap_rider([g_in], [0])] if reduce is not None else []
        dx, d_norm, got = _inproj_bwd(dproj, w_in, l, xl, _row(norm_g[l]), dx1, f"inproj_bwd_{l}", riders)
        big[l] = (g_in, g_out, g_pg, g_pe)
        carried = reduce.add(l, [g_in], [0], got) if reduce is not None else []
        small["norm_g"][l] = jnp.sum(d_norm, axis=0)
        small["conv_w"][l] = jnp.sum(d_cw, axis=1)
        small["conv_b"][l] = jnp.sum(d_cb, axis=0)
        small["branch_g"][l] = jnp.concatenate([jnp.sum(d_bg_c, axis=0), jnp.sum(d_bg_a, axis=0)])
        small["ple_norm_g"][l] = jnp.sum(d_ple, axis=0)
        small["b_pg"][l] = jnp.sum(db_pg, axis=0)
    if carried:
        reduce.finish(carried, _ride_alone(_scatter_rider([sm[3] for sm in carried], [0]), "scatter_shards_last"))
    small = {k: jnp.stack(v) for k, v in small.items()}
    small["final_g"] = jnp.sum(d_final, axis=0)
    return sq[0, 0], dx, big, small


SMALL_ORDER = ("norm_g", "conv_w", "conv_b", "branch_g", "ple_norm_g", "b_pg", "final_g")


def _pack(parts, width):
    flat = jnp.concatenate([v.reshape(-1) for v in parts])
    rows = -(-flat.shape[0] // width)
    rows = -(-rows // 8) * 8
    return jnp.pad(flat, (0, rows * width - flat.shape[0])).reshape(rows, width)


def _unpack(packed, like):
    flat = packed.reshape(-1)
    out, off = [], 0
    for v in like:
        out.append(flat[off:off + v.size].reshape(v.shape))
        off += v.size
    return out


def kernel(x, p, norm_g, w_in, conv_w, conv_b, branch_g, w_out, ple_norm_g, w_pg, b_pg, w_pe, final_g, loss_target, m_norm_g, m_w_in, m_conv_w, m_conv_b, m_branch_g, m_w_out, m_ple_norm_g, m_w_pg, m_b_pg, m_w_pe, m_final_g, v_norm_g, v_w_in, v_conv_w, v_conv_b, v_branch_g, v_w_out, v_ple_norm_g, v_w_pg, v_b_pg, v_w_pe, v_final_g):
    ix, iy, ic = _place()
    chip = 2 * ix + iy
    d = x.shape[-1]

    big_w = (w_in, w_out, w_pg, w_pe)
    own = [_cast_into_full(w, chip, ax, f"cast_{i}") for i, (w, ax) in enumerate(zip(big_w, SHARD_AXES))]
    full_in, = _ride_alone(_gather_rider([own[0]], 0, [0]), "gather_w_in_0")
    full_out, full_pg, full_pe = own[1:]
    cw_shard = conv_w.shape[-1]
    cw_slots = _exchange_small(_pack([conv_w], LANES), "exchange_conv_w")
    conv_full = jnp.concatenate([_unpack(cw_slots[2 * k], [conv_w])[0] for k in range(4)], axis=-1)

    reduce = _GradReduce(chip, ic)
    sq, dx, _, small_g = _local_step(
        x[0], p[:, 0], loss_target[0], norm_g, conv_full, conv_b, branch_g, ple_norm_g, b_pg, final_g,
        full_in, full_out, full_pg, full_pe, gather=True, reduce=reduce)
    g_big = reduce.result()

    parts = [small_g[k] for k in SMALL_ORDER] + [sq.reshape(1)]
    slots = _exchange_small(_pack(parts, d), "exchange_small_grads")
    total = _unpack(_sum_slots(slots, "sum_small"), parts)
    g_small = dict(zip(SMALL_ORDER, total[:-1]))
    loss = 0.5 * total[-1][0] / d
    g_small["conv_w"] = lax.dynamic_slice_in_dim(g_small["conv_w"], chip * cw_shard, cw_shard, axis=2)

    grads = dict(g_small)
    grads.update(w_in=g_big[0], w_out=g_big[1], w_pg=g_big[2], w_pe=g_big[3])
    weights = dict(norm_g=norm_g, w_in=w_in, conv_w=conv_w, conv_b=conv_b, branch_g=branch_g, w_out=w_out,
                   ple_norm_g=ple_norm_g, w_pg=w_pg, b_pg=b_pg, w_pe=w_pe, final_g=final_g)
    ms = dict(norm_g=m_norm_g, w_in=m_w_in, conv_w=m_conv_w, conv_b=m_conv_b, branch_g=m_branch_g, w_out=m_w_out,
              ple_norm_g=m_ple_norm_g, w_pg=m_w_pg, b_pg=m_b_pg, w_pe=m_w_pe, final_g=m_final_g)
    vs = dict(norm_g=v_norm_g, w_in=v_w_in, conv_w=v_conv_w, conv_b=v_conv_b, branch_g=v_branch_g, w_out=v_w_out,
              ple_norm_g=v_ple_norm_g, w_pg=v_w_pg, b_pg=v_b_pg, w_pe=v_w_pe, final_g=v_final_g)
    names = ("norm_g", "w_in", "conv_w", "conv_b", "branch_g", "w_out", "ple_norm_g", "w_pg", "b_pg", "w_pe", "final_g")
    delta, new_m, new_v = {}, {}, {}
    for k in ("w_in", "w_out", "w_pg", "w_pe"):
        shp = weights[k].shape
        two = lambda a: a.reshape(-1, shp[-1])
        gr, dl, mn, vn = _adamw(two(weights[k]), two(grads[k]), two(ms[k]), two(vs[k]), f"adamw_{k}")
        delta[k], new_m[k], new_v[k] = dl.reshape(shp), mn.reshape(shp), vn.reshape(shp)
        grads[k] = gr.reshape(shp)
    like = [weights[k] for k in SMALL_ORDER]
    packs = [_pack([src[k] for k in SMALL_ORDER], d) for src in (weights, grads, ms, vs)]
    outs = _adamw(*packs, "adamw_small")
    for res, o in zip((delta, new_m, new_v), outs[1:]):
        res.update(dict(zip(SMALL_ORDER, _unpack(o, like))))

    return (loss, dx[None], *[grads[k] for k in names], *[delta[k] for k in names],
            *[new_m[k] for k in names], *[new_v[k] for k in names])
```

```python
import math

import jax
import jax.numpy as jnp
from jax import lax
from jax.experimental import pallas as pl
from jax.experimental.pallas import tpu as pltpu

F32 = jnp.float32
BF16 = jnp.bfloat16
EPS = 1e-6
HEAD = 64
LANES = 128
ATT_TK = 256
ATT_TQ = 512
ATT_ROWS = 128
ALIVE_LOG = -105.0
DEPTH = 2
VMEM_LIMIT = 56 * 1024 * 1024
MESH = pl.DeviceIdType.MESH
ANY = pl.BlockSpec(memory_space=pl.ANY)

ADAM_LR = 0.001
ADAM_B1 = 0.9
ADAM_B2 = 0.999
ADAM_EPS = 1e-08
ADAM_WD = 0.01
ADAM_STEP = 10


def _pcall(body, **kw):
    return pl.pallas_call(body, **kw)


def _cp(n_axes):
    return pltpu.CompilerParams(dimension_semantics=("arbitrary",) * n_axes, vmem_limit_bytes=VMEM_LIMIT)


def _tile(n, pref):
    return pref if n % pref == 0 else n


def _split_dot(a, b, passes):
    out = None
    rem = a
    for _ in range(passes):
        hi = rem.astype(BF16)
        t = jnp.dot(hi, b, preferred_element_type=F32)
        out = t if out is None else out + t
        rem = rem - hi.astype(F32)
    return out


def _group_mat():
    r = lax.broadcasted_iota(jnp.int32, (LANES, LANES), 0) // HEAD
    c = lax.broadcasted_iota(jnp.int32, (LANES, LANES), 1) // HEAD
    return jnp.where(r == c, 1.0 / HEAD, 0.0).astype(BF16)


def _group_mean(v, gm):
    return _split_dot(v, gm, 2)


def _sigmoid(z):
    return 1.0 / (1.0 + jnp.exp(-z))


def _dot_nt(a, b):
    return lax.dot_general(a, b, (((1,), (1,)), ((), ())), preferred_element_type=F32)


def _dot_tn(a, b):
    return lax.dot_general(a, b, (((0,), (0,)), ((), ())), preferred_element_type=F32)


def _cast_into_full(w, chip, axis, name):
    _, r, c = w.shape
    tr = _tile(r, 256)
    nb = r // tr
    full = (DEPTH, 4 * r, c) if axis == 0 else (DEPTH, r, 4 * c)

    def body(k_ref, w_ref, o_ref):
        o_ref[...] = w_ref[...].astype(BF16)

    out_map = (lambda l, i, k: (l, k[0] * nb + i, 0)) if axis == 0 else (lambda l, i, k: (l, i, k[0]))
    grid_spec = pltpu.PrefetchScalarGridSpec(
        num_scalar_prefetch=1, grid=(DEPTH, nb),
        in_specs=[pl.BlockSpec((1, tr, c), lambda l, i, k: (l, i, 0))],
        out_specs=pl.BlockSpec((1, tr, c), out_map))
    return _pcall(body, name=name, grid_spec=grid_spec, out_shape=jax.ShapeDtypeStruct(full, BF16),
                  compiler_params=_cp(2))(chip.reshape(1).astype(jnp.int32), w)


def _rms_bwd_rows(dh, xv, g):
    r = lax.rsqrt(jnp.mean(xv * xv, axis=-1, keepdims=True) + EPS)
    xn = xv * r
    dxn = dh * g
    dx = r * (dxn - xn * jnp.mean(dxn * xn, axis=-1, keepdims=True))
    return dx, dh * xn


def _colsum8(v):
    tm, d = v.shape
    return jnp.sum(v.reshape(tm // 8, 8, d), axis=0)


def _inproj(x, g, w, layer, name, riders=()):
    s, d = x.shape
    n = w.shape[2]
    sw = d // 2
    ns = n // sw
    tm = _tile(s, 512)

    def body(*refs):
        own, riders_end = _riders_run(riders, refs, 3, 2, 0, pl.program_id(0), s // tm)
        x_ref, g_ref, w_ref, h_ref, o_ref = own
        xv = x_ref[...]
        r = lax.rsqrt(jnp.mean(xv * xv, axis=-1, keepdims=True) + EPS)
        h = (xv * r * g_ref[...]).astype(BF16)
        h_ref[...] = h
        for k in range(ns):
            o_ref[k] = jnp.dot(h, w_ref[0, :, k * sw:(k + 1) * sw], preferred_element_type=F32).astype(BF16)
        riders_end()

    r_ops, r_shapes, r_scratch, r_aliases = _riders_plumb(riders, 3, 2)
    outs = _pcall(body, name=name, grid=(s // tm,),
                  in_specs=[pl.BlockSpec((tm, d), lambda m: (m, 0)), pl.BlockSpec((1, d), lambda m: (0, 0)),
                            pl.BlockSpec((1, d, n), lambda m: (layer, 0, 0))] + [ANY] * len(r_ops),
                  out_specs=[pl.BlockSpec((tm, d), lambda m: (m, 0)), pl.BlockSpec((ns, tm, sw), lambda m: (0, m, 0))]
                  + [ANY] * len(r_shapes),
                  out_shape=[jax.ShapeDtypeStruct((s, d), BF16), jax.ShapeDtypeStruct((ns, s, sw), BF16)] + r_shapes,
                  input_output_aliases=r_aliases, scratch_shapes=r_scratch,
                  compiler_params=_cp(1))(x, g, w, *r_ops)
    return outs[0], outs[1], list(outs[2:])


def _softplus_parts(z):
    lm = jnp.minimum(-z, 0.0) - jnp.log(1.0 + jnp.exp(-jnp.abs(z)))
    return lm, lm + z


def _attn_tiles(s):
    tk = _tile(s, ATT_TK)
    tq = _tile(s, ATT_TQ)
    return tk, tq, tq // tk, min(ATT_ROWS, tq)


def _diag_work(chains, d, rows, tk):
    work = []
    for n, (_, r0) in enumerate(chains):
        if r0 + rows - 1 <= d * tk:
            continue
        kw = tk // 2 if (tk % 2 == 0 and r0 + rows <= d * tk + tk // 2) else tk
        if r0 >= d * tk + kw:
            mask = None
        else:
            row = lax.broadcasted_iota(jnp.int32, (rows, kw), 0)
            col = lax.broadcasted_iota(jnp.int32, (rows, kw), 1)
            mask = col + d * tk < row + r0
        work.append((n, kw, mask))
    return work


def _both(mask, gate):
    if mask is None:
        return gate
    if gate is None:
        return mask
    return jnp.logical_and(mask, gate)


def _any_alive(rsums):
    m = rsums[0]
    for r in rsums[1:]:
        m = jnp.maximum(m, r)
    return jnp.max((m > ALIVE_LOG).astype(jnp.int32))


def _attn_fwd(proj, name, riders=()):
    _, s, sw = proj.shape
    nhp = sw // LANES
    tk, tq, nd, rows = _attn_tiles(s)
    nq = s // tq
    scale = 1.0 / math.sqrt(HEAD)

    def body(*refs):
        i = pl.program_id(1)
        own, riders_end = _riders_run(riders, refs, 3, 3, 1, pl.program_id(0) * nq + i, nhp * nq)
        q_ref, k_ref, v_ref, o_ref, tl_ref, nw_ref, acc_ref = own
        tri = (lax.broadcasted_iota(jnp.int32, (tk, tk), 0) >
               lax.broadcasted_iota(jnp.int32, (tk, tk), 1)).astype(BF16)
        lane = lax.broadcasted_iota(jnp.int32, (tq, LANES), 1)
        q = q_ref[0] * jnp.asarray(scale, BF16)
        qms = [jnp.where((lane // HEAD) == h, q, jnp.zeros_like(q)) for h in range(2)]
        acc_ref[...] = jnp.zeros_like(acc_ref)
        chains = [(h, r0) for h in range(2) for r0 in range(0, tq, rows)]
        qparts = [qms[h][r0:r0 + rows] for h, r0 in chains]

        def block(rsums, tiles, items):
            kjs = [k_ref[0, pl.ds(pl.multiple_of(j * tk, tk), tk), :] for j in tiles]
            vjs = [v_ref[0, pl.ds(pl.multiple_of(j * tk, tk), tk), :] for j in tiles]
            zs = [_dot_nt(qparts[n], kjs[t][:kw]) for n, t, kw, _, _ in items]
            lms, lss, css = [], [], []
            for z, (n, t, kw, mask, gate) in zip(zs, items):
                lm, ls = _softplus_parts(z)
                keep = _both(mask, gate)
                if keep is not None:
                    lm = jnp.where(keep, lm, 0.0)
                lms.append(lm)
                lss.append(ls)
                css.append(_split_dot(lm, tri[:kw, :kw], 2))
            cur = list(rsums)
            for lm, ls, cs, (n, t, kw, mask, gate) in zip(lms, lss, css, items):
                h, r0 = chains[n]
                a = jnp.exp(ls + (cur[n] + cs))
                keep = _both(mask, gate)
                if keep is not None:
                    a = jnp.where(keep, a, 0.0)
                acc_ref[h, r0:r0 + rows, :] += jnp.dot(a.astype(BF16), vjs[t][:kw], preferred_element_type=F32)
                cur[n] = cur[n] + jnp.sum(lm, axis=1, keepdims=True)
            return tuple(cur)

        everyone = [(n, 0, tk, None, None) for n in range(len(chains))]
        upper = [n for n, (_, r0) in enumerate(chains) if r0 >= tk]
        lower = [n for n, (_, r0) in enumerate(chains) if r0 < tk]
        left = jnp.maximum(i * nd - 1, 0)
        rsums = (jnp.zeros((rows, 1), F32),) * len(chains)
        for d in reversed(range(1, nd)):
            rsums = block(rsums, [i * nd + d], [(n, 0, kw, m, None) for n, kw, m in _diag_work(chains, d, rows, tk)])
        rsums = block(rsums, [i * nd, left],
                      [(n, 0, kw, m, None) for n, kw, m in _diag_work(chains, 0, rows, tk)]
                      + [(n, 1, tk, None, i > 0) for n in lower])

        if upper:
            too = (i > 0) & (_any_alive([rsums[n] for n in upper]) > 0)
            rsums = lax.cond(too, lambda rs: block(rs, [left], [(n, 0, tk, None, None) for n in upper]),
                             lambda rs: rs, rsums)
            too = too.astype(jnp.int32)
        else:
            too = jnp.int32(0)

        def walk(c):
            jj, rs, _ = c
            rs = block(rs, [i * nd - 2 - jj], everyone)
            return jj + 1, rs, _any_alive(rs)

        whole, rsums, _ = lax.while_loop(lambda c: (c[0] < i * nd - 1) & (c[2] > 0), walk,
                                         (jnp.int32(0), rsums, _any_alive(rsums)))
        for n, (h, r0) in enumerate(chains):
            tl_ref[h, r0:r0 + rows, :] = rsums[n]
        nw_ref[0] = (jnp.zeros((8, LANES), jnp.int32) + (2 * whole + too)).astype(F32)
        o_ref[...] = jnp.where(lane < HEAD, acc_ref[0], acc_ref[1]).astype(BF16)
        riders_end()

    r_ops, r_shapes, r_scratch, r_aliases = _riders_plumb(riders, 3, 3)
    outs = _pcall(
        body, name=name, grid=(nhp, nq),
        in_specs=[pl.BlockSpec((1, tq, LANES), lambda hp, i: (4, i, hp)),
                  pl.BlockSpec((1, s, LANES), lambda hp, i: (5, 0, hp)),
                  pl.BlockSpec((1, s, LANES), lambda hp, i: (6, 0, hp))] + [ANY] * len(r_ops),
        out_specs=[pl.BlockSpec((tq, LANES), lambda hp, i: (i, hp)),
                   pl.BlockSpec((2, tq, 1), lambda hp, i: (hp, i, 0)),
                   pl.BlockSpec((1, 8, LANES), lambda hp, i: (hp * nq + i, 0, 0))] + [ANY] * len(r_shapes),
        out_shape=[jax.ShapeDtypeStruct((s, sw), BF16), jax.ShapeDtypeStruct((2 * nhp, s, 1), F32),
                   jax.ShapeDtypeStruct((nhp * nq, 8, LANES), F32)] + r_shapes,
        input_output_aliases=r_aliases,
        scratch_shapes=[pltpu.VMEM((2, tq, LANES), F32)] + r_scratch,
        compiler_params=_cp(2))(proj, proj, proj, *r_ops)
    return outs[0], outs[1], outs[2], list(outs[3:])


def _conv_rows(cc_ref, ch_ref, w_ref, b_ref, r, tc):
    r0 = pl.multiple_of(r * tc, tc)
    u = cc_ref[0, pl.ds(r0, tc), :].astype(F32) * ch_ref[0, pl.ds(r0, tc), :].astype(F32)
    p0 = pl.multiple_of(jnp.maximum(r0 - 16, 0), 16)
    up = cc_ref[0, pl.ds(p0, 16), :].astype(F32) * ch_ref[0, pl.ds(p0, 16), :].astype(F32)
    up = up * (r > 0).astype(F32)
    prev1 = up[15:16, :]
    prev2 = up[14:15, :]
    rid = lax.broadcasted_iota(jnp.int32, u.shape, 0)
    s1 = jnp.where(rid == 0, prev1, pltpu.roll(u, 1, axis=0))
    s2 = jnp.where(rid == 0, prev2, jnp.where(rid == 1, prev1, pltpu.roll(u, 2, axis=0)))
    cv = b_ref[...] + s2 * w_ref[0:1, :] + s1 * w_ref[1:2, :] + u * w_ref[2:3, :]
    return r0, u, s1, s2, cv


def _mix_fwd(proj, ya, conv_w, conv_b, bg, name, riders=()):
    _, s, sw = proj.shape
    nh = sw // LANES
    tc = _tile(s, 256)

    def body(*refs):
        c = pl.program_id(0)
        own, riders_end = _riders_run(riders, refs, 9, 1, 0, c, 2 * nh)
        cb_ref, cc_ref, ch_ref, cz_ref, ya_ref, az_ref, w_ref, b_ref, g_ref, y_ref = own
        gm = _group_mat()

        def finish(r0, yv, zg):
            n = yv * lax.rsqrt(_group_mean(yv * yv, gm) + EPS)
            y_ref[pl.ds(r0, tc), :] = (n * g_ref[...] * (zg * _sigmoid(zg))).astype(BF16)

        @pl.when(c < nh)
        def _():
            def step(r, carry):
                r0, _, _, _, cv = _conv_rows(cc_ref, ch_ref, w_ref, b_ref, r, tc)
                yc = cb_ref[0, pl.ds(r0, tc), :].astype(F32) * cv
                finish(r0, yc, cz_ref[0, pl.ds(r0, tc), :].astype(F32))
                return carry
            lax.fori_loop(0, s // tc, step, 0)

        @pl.when(c >= nh)
        def _():
            def step(r, carry):
                r0 = pl.multiple_of(r * tc, tc)
                finish(r0, ya_ref[pl.ds(r0, tc), :].astype(F32), az_ref[0, pl.ds(r0, tc), :].astype(F32))
                return carry
            lax.fori_loop(0, s // tc, step, 0)

        riders_end()

    def sec(k):
        return pl.BlockSpec((1, s, LANES), lambda c: (k, 0, jnp.minimum(c, nh - 1)))

    r_ops, r_shapes, r_scratch, r_aliases = _riders_plumb(riders, 9, 1)
    outs = _pcall(
        body, name=name, grid=(2 * nh,),
        in_specs=[sec(0), sec(1), sec(2), sec(3),
                  pl.BlockSpec((s, LANES), lambda c: (0, jnp.maximum(c - nh, 0))),
                  pl.BlockSpec((1, s, LANES), lambda c: (7, 0, jnp.maximum(c - nh, 0))),
                  pl.BlockSpec((3, LANES), lambda c: (0, jnp.minimum(c, nh - 1))),
                  pl.BlockSpec((1, LANES), lambda c: (0, jnp.minimum(c, nh - 1))),
                  pl.BlockSpec((1, LANES), lambda c: (0, c))] + [ANY] * len(r_ops),
        out_specs=[pl.BlockSpec((s, LANES), lambda c: (0, c))] + [ANY] * len(r_shapes),
        out_shape=[jax.ShapeDtypeStruct((s, 2 * sw), BF16)] + r_shapes,
        input_output_aliases=r_aliases, scratch_shapes=r_scratch, compiler_params=_cp(1),
    )(proj, proj, proj, proj, ya, proj, conv_w, conv_b, bg, *r_ops)
    return outs[0], list(outs[1:])


def _outproj(y, w, layer, x, g, name):
    s, d = x.shape
    tm = _tile(s, 512)

    def body(y_ref, w_ref, x_ref, g_ref, x1_ref, hn_ref):
        x1 = x_ref[...] + jnp.dot(y_ref[...], w_ref[0], preferred_element_type=F32)
        x1_ref[...] = x1
        r = lax.rsqrt(jnp.mean(x1 * x1, axis=-1, keepdims=True) + EPS)
        hn_ref[...] = (x1 * r * g_ref[...]).astype(BF16)

    row = lambda m: (m, 0)
    fix = lambda m: (0, 0)
    return _pcall(body, name=name, grid=(s // tm,),
                  in_specs=[pl.BlockSpec((tm, d), row), pl.BlockSpec((1, d, d), lambda m: (layer, 0, 0)),
                            pl.BlockSpec((tm, d), row), pl.BlockSpec((1, d), fix)],
                  out_specs=[pl.BlockSpec((tm, d), row), pl.BlockSpec((tm, d), row)],
                  out_shape=[jax.ShapeDtypeStruct((s, d), F32), jax.ShapeDtypeStruct((s, d), BF16)],
                  compiler_params=_cp(1))(y, w, x, g)


def _ple_fwd(hn, w_pg, b_pg, p, w_pe, layer, x1, name):
    s, d = x1.shape
    pd = p.shape[2]
    tm = _tile(s, 512)

    def body(hn_ref, wg_ref, b_ref, p_ref, we_ref, x1_ref, x2_ref, gate_ref, e_ref):
        gate = _sigmoid(jnp.dot(hn_ref[...], wg_ref[0], preferred_element_type=F32) + b_ref[...])
        e = jnp.dot(p_ref[0].astype(BF16), we_ref[0], preferred_element_type=F32)
        x2_ref[...] = x1_ref[...] + gate * e
        gate_ref[...] = gate.astype(BF16)
        e_ref[...] = e.astype(BF16)

    row = lambda m: (m, 0)
    fix = lambda m: (0, 0)
    return _pcall(body, name=name, grid=(s // tm,),
                  in_specs=[pl.BlockSpec((tm, d), row), pl.BlockSpec((1, d, d), lambda m: (layer, 0, 0)),
                            pl.BlockSpec((1, d), fix), pl.BlockSpec((1, tm, pd), lambda m: (layer, m, 0)),
                            pl.BlockSpec((1, pd, d), lambda m: (layer, 0, 0)), pl.BlockSpec((tm, d), row)],
                  out_specs=[pl.BlockSpec((tm, d), row)] * 3,
                  out_shape=[jax.ShapeDtypeStruct((s, d), F32), jax.ShapeDtypeStruct((s, d), BF16),
                             jax.ShapeDtypeStruct((s, d), BF16)],
                  compiler_params=_cp(1))(hn, w_pg, b_pg, p, w_pe, x1)


def _loss_head(x, tgt, g, name):
    s, d = x.shape
    tm = _tile(s, 512)

    def body(x_ref, t_ref, g_ref, l_ref, dx_ref, dg_ref):
        m = pl.program_id(0)

        @pl.when(m == 0)
        def _():
            l_ref[...] = jnp.zeros_like(l_ref)
            dg_ref[...] = jnp.zeros_like(dg_ref)

        xv = x_ref[...]
        gv = g_ref[...]
        r = lax.rsqrt(jnp.mean(xv * xv, axis=-1, keepdims=True) + EPS)
        xn = xv * r
        err = xn * gv - t_ref[...]
        l_ref[...] += jnp.sum(err * err)
        dy = err * (1.0 / d)
        dxn = dy * gv
        dx_ref[...] = r * (dxn - xn * jnp.mean(dxn * xn, axis=-1, keepdims=True))
        dg_ref[...] += _colsum8(dy * xn)

    row = lambda m: (m, 0)
    fix = lambda m: (0, 0)
    return _pcall(body, name=name, grid=(s // tm,),
                  in_specs=[pl.BlockSpec((tm, d), row), pl.BlockSpec((tm, d), row), pl.BlockSpec((1, d), fix)],
                  out_specs=[pl.BlockSpec((8, LANES), fix), pl.BlockSpec((tm, d), row), pl.BlockSpec((8, d), fix)],
                  out_shape=[jax.ShapeDtypeStruct((8, LANES), F32), jax.ShapeDtypeStruct((s, d), F32),
                             jax.ShapeDtypeStruct((8, d), F32)],
                  compiler_params=_cp(1))(x, tgt, g)


def _ple_bwd(dx2, gate, e, x1, w_pg, g_ple, w_out, layer, name, riders=()):
    s, d = dx2.shape
    tm = _tile(s, 512)

    def body(*refs):
        m = pl.program_id(0)
        own, riders_end = _riders_run(riders, refs, 7, 6, 0, m, s // tm)
        (dx2_ref, gate_ref, e_ref, x1_ref, wg_ref, g_ref, wo_ref,
         du_ref, de_ref, dx1_ref, dy_ref, db_ref, dg_ref) = own

        @pl.when(m == 0)
        def _():
            db_ref[...] = jnp.zeros_like(db_ref)
            dg_ref[...] = jnp.zeros_like(dg_ref)

        dx2v = dx2_ref[...]
        gate = gate_ref[...].astype(F32)
        du = dx2v * e_ref[...].astype(F32) * gate * (1.0 - gate)
        de_ref[...] = (dx2v * gate).astype(BF16)
        dub = du.astype(BF16)
        du_ref[...] = dub
        db_ref[...] += _colsum8(du)
        dhn = _dot_nt(dub, wg_ref[0])
        dxr, dgr = _rms_bwd_rows(dhn, x1_ref[...], g_ref[...])
        dx1 = dx2v + dxr
        dx1_ref[...] = dx1
        dg_ref[...] += _colsum8(dgr)
        dy_ref[...] = _dot_nt(dx1.astype(BF16), wo_ref[0]).astype(BF16)
        riders_end()

    row = lambda m: (m, 0)
    fix = lambda m: (0, 0)
    t = pl.BlockSpec((tm, d), row)
    r_ops, r_shapes, r_scratch, r_aliases = _riders_plumb(riders, 7, 6)
    outs = _pcall(body, name=name, grid=(s // tm,),
                  in_specs=[t, t, t, t, pl.BlockSpec((1, d, d), lambda m: (layer, 0, 0)), pl.BlockSpec((1, d), fix),
                            pl.BlockSpec((1, d, d), lambda m: (layer, 0, 0))] + [ANY] * len(r_ops),
                  out_specs=[t, t, t, t, pl.BlockSpec((8, d), fix), pl.BlockSpec((8, d), fix)] + [ANY] * len(r_shapes),
                  out_shape=[jax.ShapeDtypeStruct((s, d), BF16), jax.ShapeDtypeStruct((s, d), BF16),
                             jax.ShapeDtypeStruct((s, d), F32), jax.ShapeDtypeStruct((s, d), BF16),
                             jax.ShapeDtypeStruct((8, d), F32), jax.ShapeDtypeStruct((8, d), F32)] + r_shapes,
                  input_output_aliases=r_aliases, scratch_shapes=r_scratch,
                  compiler_params=_cp(1))(dx2, gate, e, x1, w_pg, g_ple, w_out, *r_ops)
    return tuple(outs[:6]) + (list(outs[6:]),)


def _mm_tn(a, b, name, a_layer=None):
    s, ka = a.shape[-2:]
    n = b.shape[1]
    tn = _tile(n, 1024)
    ns = n // tn
    tk = _tile(s, 512)
    nk = s // tk

    def body(a_ref, b_ref, o_ref, acc_ref):
        k = pl.program_id(1)

        @pl.when(k == 0)
        def _():
            acc_ref[...] = jnp.zeros_like(acc_ref)

        av = a_ref[...] if a_layer is None else a_ref[0]
        acc_ref[...] += _dot_tn(av.astype(BF16), b_ref[...].astype(BF16))

        @pl.when(k == nk - 1)
        def _():
            o_ref[...] = acc_ref[...]

    a_spec = (pl.BlockSpec((tk, ka), lambda j, k: (k, 0)) if a_layer is None
              else pl.BlockSpec((1, tk, ka), lambda j, k: (a_layer, k, 0)))
    return _pcall(body, name=name, grid=(ns, nk),
                  in_specs=[a_spec, pl.BlockSpec((tk, tn), lambda j, k: (k, j))],
                  out_specs=pl.BlockSpec((ka, tn), lambda j, k: (0, j)),
                  out_shape=jax.ShapeDtypeStruct((ka, n), F32),
                  scratch_shapes=[pltpu.VMEM((ka, tn), F32)], compiler_params=_cp(2))(a, b)


def _norm_gate_bwd(dy, yv, zg, g, gm):
    r = lax.rsqrt(_group_mean(yv * yv, gm) + EPS)
    n = yv * r
    sg = _sigmoid(zg)
    sil = zg * sg
    dzg = dy * n * g * (sg * (1.0 + zg * (1.0 - sg)))
    dn = dy * g * sil
    dyv = r * (dn - n * _group_mean(dn * n, gm))
    return dyv, dzg, dy * n * sil


def _convmix_bwd(dy, proj, conv_w, conv_b, bg, name, riders=()):
    _, s, sw = proj.shape
    nh = sw // LANES
    tc = _tile(s, 256)
    nr = s // tc

    def body(*refs):
        own, riders_end = _riders_run(riders, refs, 8, 4, 1, pl.program_id(0), nh)
        (dy_ref, cb_ref, cc_ref, ch_ref, cz_ref, w_ref, b_ref, g_ref,
         dp_ref, dw_ref, db_ref, dg_ref, dcv_ref) = own
        gm = _group_mat()
        dcv_ref[pl.ds(s, 8), :] = jnp.zeros((8, LANES), F32)

        def pass1(r, carry):
            dw0, dw1, dw2, db, dg = carry
            r0, u, s1, s2, cv = _conv_rows(cc_ref, ch_ref, w_ref, b_ref, r, tc)
            cb = cb_ref[0, pl.ds(r0, tc), :].astype(F32)
            dyc, dcz, dgr = _norm_gate_bwd(dy_ref[pl.ds(r0, tc), :].astype(F32), cb * cv,
                                           cz_ref[0, pl.ds(r0, tc), :].astype(F32), g_ref[...], gm)
            dp_ref[0, pl.ds(r0, tc), :] = (dyc * cv).astype(BF16)
            dp_ref[3, pl.ds(r0, tc), :] = dcz.astype(BF16)
            dcv = dyc * cb
            dcv_ref[pl.ds(r0, tc), :] = dcv
            return (dw0 + _colsum8(dcv * s2), dw1 + _colsum8(dcv * s1), dw2 + _colsum8(dcv * u),
                    db + _colsum8(dcv), dg + _colsum8(dgr))

        z8 = jnp.zeros((8, LANES), F32)
        dw0, dw1, dw2, db, dg = lax.fori_loop(0, nr, pass1, (z8, z8, z8, z8, z8))
        dw_ref[0] = dw0
        dw_ref[1] = dw1
        dw_ref[2] = dw2
        db_ref[...] = db
        dg_ref[...] = dg

        def pass2(r, carry):
            r0 = pl.multiple_of(r * tc, tc)
            dcv = dcv_ref[pl.ds(r0, tc), :]
            nxt = dcv_ref[pl.ds(pl.multiple_of(r0 + tc, 8), 8), :]
            rid = lax.broadcasted_iota(jnp.int32, dcv.shape, 0)
            n1 = jnp.where(rid == tc - 1, nxt[0:1, :], pltpu.roll(dcv, tc - 1, axis=0))
            n2 = jnp.where(rid == tc - 1, nxt[1:2, :],
                           jnp.where(rid == tc - 2, nxt[0:1, :], pltpu.roll(dcv, tc - 2, axis=0)))
            du = dcv * w_ref[2:3, :] + n1 * w_ref[1:2, :] + n2 * w_ref[0:1, :]
            dp_ref[1, pl.ds(r0, tc), :] = (du * ch_ref[0, pl.ds(r0, tc), :].astype(F32)).astype(BF16)
            dp_ref[2, pl.ds(r0, tc), :] = (du * cc_ref[0, pl.ds(r0, tc), :].astype(F32)).astype(BF16)
            return carry

        lax.fori_loop(0, nr, pass2, 0)
        riders_end()

    def sec(k):
        return pl.BlockSpec((1, s, LANES), lambda c: (k, 0, c))

    col = lambda c: (0, c)
    r_ops, r_shapes, r_scratch, r_aliases = _riders_plumb(riders, 8, 4)
    outs = _pcall(
        body, name=name, grid=(nh,),
        in_specs=[pl.BlockSpec((s, LANES), col), sec(0), sec(1), sec(2), sec(3),
                  pl.BlockSpec((3, LANES), col), pl.BlockSpec((1, LANES), col), pl.BlockSpec((1, LANES), col)]
        + [ANY] * len(r_ops),
        out_specs=[pl.BlockSpec((4, s, LANES), lambda c: (0, 0, c)), pl.BlockSpec((3, 8, LANES), lambda c: (0, 0, c)),
                   pl.BlockSpec((8, LANES), col), pl.BlockSpec((8, LANES), col)] + [ANY] * len(r_shapes),
        out_shape=[jax.ShapeDtypeStruct((8, s, sw), BF16), jax.ShapeDtypeStruct((3, 8, sw), F32),
                   jax.ShapeDtypeStruct((8, sw), F32), jax.ShapeDtypeStruct((8, sw), F32)] + r_shapes,
        input_output_aliases=r_aliases,
        scratch_shapes=[pltpu.VMEM((s + 8, LANES), F32)] + r_scratch, compiler_params=_cp(1),
    )(dy, proj, proj, proj, proj, conv_w, conv_b, bg, *r_ops)
    return tuple(outs[:4]) + (list(outs[4:]),)


def _attn_bwd(proj, dy, ya, tl, walked, bg, buf, name, riders=()):
    _, s, sw = proj.shape
    nhp = sw // LANES
    tk, t, nd, rows_c = _attn_tiles(s)
    nq = s // t
    scale = 1.0 / math.sqrt(HEAD)

    def body(*refs):
        step = pl.program_id(1)
        i = nq - 1 - step
        own, riders_end = _riders_run(riders, refs, 10, 2, 3, pl.program_id(0) * nq + step, nhp * nq)
        (q_ref, k_ref, v_ref, az_ref, dy_ref, ya_ref, tl_ref, nw_ref, g_ref, buf_ref, out_ref, dg_ref,
         dka_ref, dva_ref, dqa_ref) = own

        @pl.when(step == 0)
        def _():
            dka_ref[...] = jnp.zeros_like(dka_ref)
            dva_ref[...] = jnp.zeros_like(dva_ref)
            dg_ref[...] = jnp.zeros_like(dg_ref)

        dyv, dzg, dgr = _norm_gate_bwd(dy_ref[...].astype(F32), ya_ref[...].astype(F32), az_ref[0].astype(F32),
                                       g_ref[...], _group_mat())
        out_ref[3] = dzg.astype(BF16)
        dg_ref[...] += _colsum8(dgr)

        tri = (lax.broadcasted_iota(jnp.int32, (tk, tk), 0) <=
               lax.broadcasted_iota(jnp.int32, (tk, tk), 1)).astype(BF16)
        lane = lax.broadcasted_iota(jnp.int32, (t, LANES), 1)
        q = q_ref[0] * jnp.asarray(scale, BF16)
        do = dyv.astype(BF16)
        qms = [jnp.where((lane // HEAD) == h, q, jnp.zeros_like(q)) for h in range(2)]
        doms = [jnp.where((lane // HEAD) == h, do, jnp.zeros_like(do)) for h in range(2)]
        dqa_ref[...] = jnp.zeros_like(dqa_ref)
        chains = [(h, r0) for h in range(2) for r0 in range(0, t, rows_c)]
        qparts = [qms[h][r0:r0 + rows_c] for h, r0 in chains]
        doparts = [doms[h][r0:r0 + rows_c] for h, r0 in chains]
        tots = [tl_ref[h, r0:r0 + rows_c, :] for h, r0 in chains]

        def block(carry, tiles, items):
            k0s = [pl.multiple_of(j * tk, tk) for j in tiles]
            kjs = [k_ref[0, pl.ds(k0, tk), :] for k0 in k0s]
            vjs = [v_ref[0, pl.ds(k0, tk), :] for k0 in k0s]
            zs = [_dot_nt(qparts[n], kjs[t][:kw]) for n, t, kw, _, _ in items]
            das = [_dot_nt(doparts[n], vjs[t][:kw]) for n, t, kw, _, _ in items]
            keeps = [_both(mask, gate) for _, _, _, mask, gate in items]
            lms, lss, cls = [], [], []
            for z, keep, (n, t, kw, _, _) in zip(zs, keeps, items):
                lm, ls = _softplus_parts(z)
                if keep is not None:
                    lm = jnp.where(keep, lm, 0.0)
                lms.append(lm)
                lss.append(ls)
                cls.append(_split_dot(lm, tri[:kw, :kw], 2))
            cur = list(carry)
            psums, abs_, gs, cgs = [], [], [], []
            for lm, ls, cl, da, keep, (n, t, kw, _, _) in zip(lms, lss, cls, das, keeps, items):
                psum, gsum = cur[n]
                a = jnp.exp(ls + (tots[n] - psum - cl))
                if keep is not None:
                    a = jnp.where(keep, a, 0.0)
                g = a * da
                psums.append(gsum)
                gs.append(g)
                abs_.append(a.astype(BF16))
                cgs.append(_split_dot(g, tri[:kw, :kw], 1))
                cur[n] = (psum + jnp.sum(lm, axis=1, keepdims=True), gsum + jnp.sum(g, axis=1, keepdims=True))
            dks, dvs = {}, {}
            for ls, a, g, cg, gsum, keep, (n, t, kw, _, _) in zip(lss, abs_, gs, cgs, psums, keeps, items):
                h, r0 = chains[n]
                dz = g - jnp.exp(ls) * (gsum + cg)
                if keep is not None:
                    dz = jnp.where(keep, dz, 0.0)
                dz = dz.astype(BF16)
                dqa_ref[h, r0:r0 + rows_c, :] += jnp.dot(dz, kjs[t][:kw], preferred_element_type=F32)
                dkh = _dot_tn(dz, qparts[n])
                dvh = _dot_tn(a, doparts[n])
                dks[t, kw] = dkh if (t, kw) not in dks else dks[t, kw] + dkh
                dvs[t, kw] = dvh if (t, kw) not in dvs else dvs[t, kw] + dvh
            for t, kw in dks:
                dka_ref[pl.ds(k0s[t], kw), :] += dks[t, kw]
                dva_ref[pl.ds(k0s[t], kw), :] += dvs[t, kw]
            return tuple(cur)

        z1 = jnp.zeros((rows_c, 1), F32)
        everyone = [(n, 0, tk, None, None) for n in range(len(chains))]
        upper = [n for n, (_, r0) in enumerate(chains) if r0 >= tk]
        lower = [n for n, (_, r0) in enumerate(chains) if r0 < tk]
        left = jnp.maximum(i * nd - 1, 0)
        code = jnp.clip(jnp.max(nw_ref[0].astype(jnp.int32)), 0, 2 * left + 1)
        too = jnp.where(i > 0, code % 2, 0)
        whole = jnp.minimum(code // 2, left)
        carry = lax.fori_loop(left - whole, left, lambda j, c: block(c, [j], everyone), ((z1, z1),) * len(chains))
        if upper:
            carry = lax.cond(too > 0, lambda c: block(c, [left], [(n, 0, tk, None, None) for n in upper]),
                             lambda c: c, carry)
        carry = block(carry, [left, i * nd],
                      [(n, 0, tk, None, i > 0) for n in lower]
                      + [(n, 1, kw, m, None) for n, kw, m in _diag_work(chains, 0, rows_c, tk)])
        for d in range(1, nd):
            carry = block(carry, [i * nd + d], [(n, 0, kw, m, None) for n, kw, m in _diag_work(chains, d, rows_c, tk)])
        out_ref[0] = (jnp.where(lane < HEAD, dqa_ref[0], dqa_ref[1]) * scale).astype(BF16)
        own = pl.multiple_of(i * t, t)
        out_ref[1] = dka_ref[pl.ds(own, t), :].astype(BF16)
        out_ref[2] = dva_ref[pl.ds(own, t), :].astype(BF16)
        riders_end()

    def rows(sec):
        return pl.BlockSpec((1, t, LANES), lambda hp, st: (sec, nq - 1 - st, hp))

    def whole(sec):
        return pl.BlockSpec((1, s, LANES), lambda hp, st: (sec, 0, hp))

    r_ops, r_shapes, r_scratch, r_aliases = _riders_plumb(riders, 10, 2)
    outs = _pcall(
        body, name=name, grid=(nhp, nq),
        in_specs=[rows(4), whole(5), whole(6), rows(7),
                  pl.BlockSpec((t, LANES), lambda hp, st: (nq - 1 - st, hp + nhp)),
                  pl.BlockSpec((t, LANES), lambda hp, st: (nq - 1 - st, hp)),
                  pl.BlockSpec((2, t, 1), lambda hp, st: (hp, nq - 1 - st, 0)),
                  pl.BlockSpec((1, 8, LANES), lambda hp, st: (hp * nq + nq - 1 - st, 0, 0)),
                  pl.BlockSpec((1, LANES), lambda hp, st: (0, hp + nhp)), ANY] + [ANY] * len(r_ops),
        out_specs=[pl.BlockSpec((4, t, LANES), lambda hp, st: (1, nq - 1 - st, hp)),
                   pl.BlockSpec((8, LANES), lambda hp, st: (0, hp))] + [ANY] * len(r_shapes),
        out_shape=[jax.ShapeDtypeStruct(buf.shape, buf.dtype), jax.ShapeDtypeStruct((8, sw), F32)] + r_shapes,
        input_output_aliases={9: 0, **r_aliases},
        scratch_shapes=[pltpu.VMEM((s, LANES), F32), pltpu.VMEM((s, LANES), F32), pltpu.VMEM((2, t, LANES), F32)]
        + r_scratch,
        compiler_params=_cp(2))(proj, proj, proj, proj, dy, ya, tl, walked, bg, buf, *r_ops)
    return outs[0], outs[1], list(outs[2:])


def _grad_w_in(h, dproj, name):
    s, d = h.shape
    ns, _, sw = dproj.shape

    def body(h_ref, b_ref, o_ref, ht_ref):
        @pl.when(pl.program_id(0) == 0)
        def _():
            ht_ref[...] = h_ref[...].T

        o_ref[...] = jnp.dot(ht_ref[...], b_ref[0], preferred_element_type=F32)

    return _pcall(body, name=name, grid=(ns,),
                  in_specs=[pl.BlockSpec((s, d), lambda j: (0, 0)), pl.BlockSpec((1, s, sw), lambda j: (j, 0, 0))],
                  out_specs=pl.BlockSpec((d, sw), lambda j: (0, j)),
                  out_shape=jax.ShapeDtypeStruct((d, ns * sw), F32),
                  scratch_shapes=[pltpu.VMEM((d, s), BF16)], compiler_params=_cp(1))(h, dproj)


def _inproj_bwd(dproj, w, layer, x, g, dx1, name, riders=()):
    ns, s, sw = dproj.shape
    d = x.shape[1]
    tm = _tile(s, 512)

    def body(*refs):
        own, riders_end = _riders_run(riders, refs, 5, 2, 0, pl.program_id(0), s // tm)
        dp_ref, w_ref, x_ref, g_ref, dx1_ref, dx_ref, dg_ref = own

        @pl.when(pl.program_id(0) == 0)
        def _():
            dg_ref[...] = jnp.zeros_like(dg_ref)

        dh = _dot_nt(dp_ref[0], w_ref[0, :, 0:sw])
        for k in range(1, ns):
            dh = dh + _dot_nt(dp_ref[k], w_ref[0, :, k * sw:(k + 1) * sw])
        dxr, dgr = _rms_bwd_rows(dh, x_ref[...], g_ref[...])
        dx_ref[...] = dx1_ref[...] + dxr
        dg_ref[...] += _colsum8(dgr)
        riders_end()

    row = lambda m: (m, 0)
    fix = lambda m: (0, 0)
    r_ops, r_shapes, r_scratch, r_aliases = _riders_plumb(riders, 5, 2)
    outs = _pcall(body, name=name, grid=(s // tm,),
                  in_specs=[pl.BlockSpec((ns, tm, sw), lambda m: (0, m, 0)),
                            pl.BlockSpec((1, d, ns * sw), lambda m: (layer, 0, 0)),
                            pl.BlockSpec((tm, d), row), pl.BlockSpec((1, d), fix), pl.BlockSpec((tm, d), row)]
                  + [ANY] * len(r_ops),
                  out_specs=[pl.BlockSpec((tm, d), row), pl.BlockSpec((8, d), fix)] + [ANY] * len(r_shapes),
                  out_shape=[jax.ShapeDtypeStruct((s, d), F32), jax.ShapeDtypeStruct((8, d), F32)] + r_shapes,
                  input_output_aliases=r_aliases, scratch_shapes=r_scratch,
                  compiler_params=_cp(1))(dproj, w, x, g, dx1, *r_ops)
    return outs[0], outs[1], list(outs[2:])


def _adamw(w, g, m, v, name):
    r, c = w.shape
    tr = _tile(r, 256)
    c1 = 1.0 - ADAM_B1 ** ADAM_STEP
    c2 = 1.0 - ADAM_B2 ** ADAM_STEP

    def body(w_ref, g_ref, m_ref, v_ref, go_ref, d_ref, mo_ref, vo_ref):
        gv = g_ref[...]
        go_ref[...] = gv
        mn = ADAM_B1 * m_ref[...] + (1.0 - ADAM_B1) * gv
        vn = ADAM_B2 * v_ref[...] + (1.0 - ADAM_B2) * (gv * gv)
        d_ref[...] = -ADAM_LR * ((mn / c1) / (jnp.sqrt(vn / c2) + ADAM_EPS) + ADAM_WD * w_ref[...])
        mo_ref[...] = mn
        vo_ref[...] = vn

    t = pl.BlockSpec((tr, c), lambda i: (i, 0))
    return _pcall(body, name=name, grid=(r // tr,), in_specs=[t] * 4, out_specs=[t] * 4,
                  out_shape=[jax.ShapeDtypeStruct((r, c), F32)] * 4, compiler_params=_cp(1))(w, g, m, v)


def _add_half(grad, other, core, a, name):
    hr, hc = other.shape
    tr = _tile(hr, 256)
    nb = hr // tr

    def body(c_ref, g_ref, o_ref, out_ref, outb_ref):
        v = g_ref[...] + o_ref[...]
        out_ref[...] = v
        outb_ref[...] = v.astype(BF16)

    t = pl.BlockSpec((tr, hc), lambda i, c: (i, 0))
    own = (lambda i, c: (c[0] * nb + i, 0)) if HALF_AXES[a] == 0 else (lambda i, c: (i, c[0]))
    grid_spec = pltpu.PrefetchScalarGridSpec(
        num_scalar_prefetch=1, grid=(nb,), in_specs=[pl.BlockSpec((tr, hc), own), t], out_specs=[t, t])
    return _pcall(body, name=name, grid_spec=grid_spec,
                  out_shape=[jax.ShapeDtypeStruct((hr, hc), F32), jax.ShapeDtypeStruct((hr, hc), BF16)],
                  compiler_params=_cp(1))(core.reshape(1).astype(jnp.int32), grad, other)


def _sum_half(wide, parts, chip, core, layer, a, stack, name):
    _, sr, sc = parts.shape
    tr = _tile(sr, 256)
    nbs = sr // tr

    def body(k_ref, f_ref, p_ref, *rest):
        rest[-1][0] = ((f_ref[...] + p_ref[0].astype(F32)) + p_ref[1].astype(F32)) + p_ref[2].astype(F32)

    f_map = (lambda i, k: (i, k[0])) if SHARD_AXES[a] == 1 else (lambda i, k: (k[0] * nbs + i, 0))
    if HALF_AXES[a] == 0:
        shape, o_map = (DEPTH, 2 * sr, sc), (lambda i, k: (layer, k[1] * nbs + i, 0))
    else:
        shape, o_map = (DEPTH, sr, 2 * sc), (lambda i, k: (layer, i, k[1]))
    in_specs = [pl.BlockSpec((tr, sc), f_map), pl.BlockSpec((3, tr, sc), lambda i, k: (0, i, 0))]
    args = [wide, parts]
    aliases = {}
    if stack is not None:
        in_specs.append(ANY)
        args.append(stack)
        aliases = {3: 0}
    grid_spec = pltpu.PrefetchScalarGridSpec(
        num_scalar_prefetch=1, grid=(nbs,), in_specs=in_specs, out_specs=pl.BlockSpec((1, tr, sc), o_map))
    return _pcall(body, name=name, grid_spec=grid_spec, out_shape=jax.ShapeDtypeStruct(shape, F32),
                  input_output_aliases=aliases,
                  compiler_params=_cp(1))(jnp.stack([chip, core]).astype(jnp.int32), *args)


def _sum_slots(slots, name):
    n = slots.shape[0]

    def body(s_ref, o_ref):
        acc = s_ref[0]
        for i in range(1, n):
            acc = acc + s_ref[i]
        o_ref[...] = acc

    return _pcall(body, name=name, out_shape=jax.ShapeDtypeStruct(slots.shape[1:], F32))(slots)


def _place():
    return lax.axis_index("x"), lax.axis_index("y"), lax.axis_index("c")


def _shard_view(ref, axis, chip, size):
    if axis == 0:
        return ref.at[pl.ds(chip * size, size), :]
    return ref.at[:, pl.ds(chip * size, size)]


SHARD_AXES = (1, 0, 0, 1)
HALF_AXES = tuple(1 - ax for ax in SHARD_AXES)


class _Rider:
    def __init__(self, operands, out_shape, sems, phases, aliased=False):
        self.operands, self.out_shape, self.sems = list(operands), list(out_shape), list(sems)
        self.phases, self.aliased = phases, aliased


def _riders_plumb(riders, n_in, n_out):
    ops, out_shape, scratch, aliases = [], [], [], {}
    for r in riders:
        if r.aliased:
            for k in range(len(r.operands)):
                aliases[n_in + len(ops) + k] = n_out + len(out_shape) + k
        ops += r.operands
        out_shape += r.out_shape
        scratch += r.sems
    return ops, out_shape, scratch, aliases


def _riders_run(riders, refs, n_in, n_out, n_scr, step, nsteps):
    n_rin = sum(len(r.operands) for r in riders)
    n_rout = sum(len(r.out_shape) for r in riders)
    rin = refs[n_in:n_in + n_rin]
    o0 = n_in + n_rin
    rout = refs[o0 + n_out:o0 + n_out + n_rout]
    s0 = o0 + n_out + n_rout
    rsem = refs[s0 + n_scr:]
    own = list(refs[:n_in]) + list(refs[o0:o0 + n_out]) + list(refs[s0:s0 + n_scr])
    lasts = []
    for r in riders:
        ph = r.phases(rin[:len(r.operands)], rout[:len(r.out_shape)], rsem[:len(r.sems)])
        rin, rout, rsem = rin[len(r.operands):], rout[len(r.out_shape):], rsem[len(r.sems):]
        pl.when(step == 0)(ph[0])
        for mid in ph[1:-1]:
            pl.when(step == (3 * nsteps) // 4)(mid)
        lasts.append(ph[-1])

    def finish():
        for last in lasts:
            pl.when(step == nsteps - 1)(last)

    return own, finish


def _gather_phases(ins, outs, ssem, rsem, layer, which):
    n = len(ins)
    x, y, c = _place()
    me = 2 * x + y
    chips = [(1 - x, y), (x, 1 - y), (1 - x, 1 - y)]

    def piece(a, chip, half, of):
        ax = SHARD_AXES[which[a]]
        block = _shard_view(of[a].at[layer], ax, chip, of[a].shape[1 + ax] // 4)
        r = block.shape[0] // 2
        return block.at[pl.ds(half * r, r), :]

    def over_ici(a, j):
        cx, cy = chips[j]
        return pltpu.make_async_remote_copy(
            src_ref=piece(a, me, c, ins), dst_ref=piece(a, me, c, outs), send_sem=ssem.at[a, j],
            recv_sem=rsem.at[a, j], device_id=(cx, cy, c), device_id_type=MESH)

    def landed(a, j, half):
        cx, cy = chips[j]
        return piece(a, 2 * cx + cy, half, outs)

    def to_sibling(a, j):
        got = landed(a, j, c)
        return pltpu.make_async_remote_copy(
            src_ref=got, dst_ref=got, send_sem=ssem.at[a, 3 + j], recv_sem=rsem.at[a, 3 + j],
            device_id=(x, y, 1 - c), device_id_type=MESH)

    def wait_arrival(a, k, place):
        pltpu.make_async_remote_copy(src_ref=place, dst_ref=place, send_sem=ssem.at[a, k], recv_sem=rsem.at[a, k],
                                     device_id=(x, y, c), device_id_type=MESH).wait_recv()

    def start():
        for a in range(n):
            for j in range(3):
                over_ici(a, j).start()

    def pass_on():
        for a in range(n):
            for j in range(3):
                wait_arrival(a, j, landed(a, j, c))
                to_sibling(a, j).start()

    def finish():
        for a in range(n):
            for j in range(3):
                wait_arrival(a, 3 + j, landed(a, j, 1 - c))
        for a in range(n):
            for j in range(3):
                over_ici(a, j).wait_send()
                to_sibling(a, j).wait_send()

    return start, pass_on, finish


def _gather_rider(fulls, layer, which):
    n = len(fulls)
    return _Rider(fulls, [jax.ShapeDtypeStruct(f.shape, f.dtype) for f in fulls],
                  [pltpu.SemaphoreType.DMA((n, 6)), pltpu.SemaphoreType.DMA((n, 6))],
                  lambda ins, outs, sems: _gather_phases(ins, outs, sems[0], sems[1], layer, which), aliased=True)


def _ride_alone(rider, name):
    n = len(rider.operands)

    def body(*refs):
        for phase in rider.phases(refs[:n], refs[n:n + len(rider.out_shape)], refs[n + len(rider.out_shape):]):
            phase()

    return _pcall(body, name=name, in_specs=[ANY] * n, out_specs=[ANY] * len(rider.out_shape),
                  out_shape=rider.out_shape, scratch_shapes=rider.sems,
                  input_output_aliases={a: a for a in range(n)} if rider.aliased else {})(*rider.operands)


def _half_view(ref, a, half):
    n = ref.shape[HALF_AXES[a]] // 2
    if HALF_AXES[a] == 0:
        return ref.at[pl.ds(half * n, n), :]
    return ref.at[:, pl.ds(half * n, n)]


def _swap_rider(grads, which):
    n = len(grads)
    halves = []
    for g, w in zip(grads, which):
        sh = list(g.shape)
        sh[HALF_AXES[w]] //= 2
        halves.append(jax.ShapeDtypeStruct(tuple(sh), g.dtype))

    def phases(srcs, outs, sems):
        x, y, c = _place()

        def copy(a):
            return pltpu.make_async_remote_copy(
                src_ref=_half_view(srcs[a], which[a], 1 - c), dst_ref=outs[a], send_sem=sems[0].at[a],
                recv_sem=sems[1].at[a], device_id=(x, y, 1 - c), device_id_type=MESH)

        def start():
            for a in range(n):
                copy(a).start()

        def finish():
            for a in range(n):
                copy(a).wait()

        return start, finish

    return _Rider(grads, halves, [pltpu.SemaphoreType.DMA((n,)), pltpu.SemaphoreType.DMA((n,))], phases)


def _scatter_rider(sums, which):
    n = len(sums)
    shapes = []
    for f, w in zip(sums, which):
        sh = list(f.shape)
        sh[SHARD_AXES[w]] //= 4
        shapes.append(jax.ShapeDtypeStruct((3,) + tuple(sh), f.dtype))

    def phases(srcs, outs, sems):
        x, y, c = _place()
        chips = [(1 - x, y), (x, 1 - y), (1 - x, 1 - y)]

        def copy(a, j):
            cx, cy = chips[j]
            ax = SHARD_AXES[which[a]]
            src = _shard_view(srcs[a], ax, 2 * cx + cy, srcs[a].shape[ax] // 4)
            return pltpu.make_async_remote_copy(src_ref=src, dst_ref=outs[a].at[j], send_sem=sems[0].at[a, j],
                                                recv_sem=sems[1].at[a, j], device_id=(cx, cy, c), device_id_type=MESH)

        def start():
            for a in range(n):
                for j in range(3):
                    copy(a, j).start()

        def finish():
            for a in range(n):
                for j in range(3):
                    copy(a, j).wait()

        return start, finish

    return _Rider(sums, shapes, [pltpu.SemaphoreType.DMA((n, 3)), pltpu.SemaphoreType.DMA((n, 3))], phases)


def _pair_halves(stacks):
    n = len(stacks)

    def body(*refs):
        ins, outs = refs[:n], refs[n:2 * n]
        ssem, rsem = refs[2 * n:]
        x, y, c = _place()
        cps = [pltpu.make_async_remote_copy(
            src_ref=_half_view(ins[a].at[l], a, c), dst_ref=_half_view(outs[a].at[l], a, c), send_sem=ssem.at[a, l],
            recv_sem=rsem.at[a, l], device_id=(x, y, 1 - c), device_id_type=MESH)
            for a in range(n) for l in range(DEPTH)]
        for cp in cps:
            cp.start()
        for a in range(n):
            for l in range(DEPTH):
                got = _half_view(outs[a].at[l], a, 1 - c)
                pltpu.make_async_remote_copy(src_ref=got, dst_ref=got, send_sem=ssem.at[a, l], recv_sem=rsem.at[a, l],
                                             device_id=(x, y, 1 - c), device_id_type=MESH).wait_recv()
        for cp in cps:
            cp.wait_send()

    return _pcall(body, name="pair_halves", in_specs=[ANY] * n, out_specs=[ANY] * n,
                  out_shape=[jax.ShapeDtypeStruct(st.shape, st.dtype) for st in stacks],
                  input_output_aliases={a: a for a in range(n)},
                  scratch_shapes=[pltpu.SemaphoreType.DMA((n, DEPTH)), pltpu.SemaphoreType.DMA((n, DEPTH))])(*stacks)


class _GradReduce:
    def __init__(self, chip, core):
        self.chip, self.core = chip, core
        self.stacks = [None] * len(SHARD_AXES)

    def add(self, layer, grads, which, got):
        return [(layer, w) + tuple(_add_half(g, o, self.core, w, f"add_half_{layer}_{w}"))
                for g, o, w in zip(grads, got, which)]

    def finish(self, sums, partials):
        for (layer, w, wide, _), pr in zip(sums, partials):
            self.stacks[w] = _sum_half(wide, pr, self.chip, self.core, layer, w, self.stacks[w], f"sum_half_{layer}_{w}")

    def result(self):
        return _pair_halves(self.stacks)


def _exchange_small(pack, name, riders=()):
    nd = 8

    def body(*refs):
        own, riders_end = _riders_run(riders, refs, 1, 1, 2, jnp.int32(0), 1)
        p_ref, o_ref, ssem, rsem = own
        x, y, c = _place()
        me = 4 * x + 2 * y + c
        o_ref[me] = p_ref[...]
        cps = []
        for j in range(1, nd):
            px, py, pc = x ^ (j >> 2), y ^ ((j >> 1) & 1), c ^ (j & 1)
            cps.append(pltpu.make_async_remote_copy(
                src_ref=p_ref, dst_ref=o_ref.at[me], send_sem=ssem.at[j - 1], recv_sem=rsem.at[j - 1],
                device_id=(px, py, pc), device_id_type=MESH))
        for cp in cps:
            cp.start()
        for j in range(1, nd):
            peer = me ^ j
            got = o_ref.at[peer]
            pltpu.make_async_remote_copy(src_ref=got, dst_ref=got, send_sem=ssem.at[j - 1], recv_sem=rsem.at[j - 1],
                                         device_id=(x, y, c), device_id_type=MESH).wait_recv()
        for cp in cps:
            cp.wait_send()
        riders_end()

    vm = pl.BlockSpec(memory_space=pltpu.VMEM)
    r_ops, r_shapes, r_scratch, r_aliases = _riders_plumb(riders, 1, 1)
    outs = _pcall(body, name=name, in_specs=[vm] + [ANY] * len(r_ops), out_specs=[vm] + [ANY] * len(r_shapes),
                  out_shape=[jax.ShapeDtypeStruct((nd,) + pack.shape, pack.dtype)] + r_shapes,
                  input_output_aliases=r_aliases,
                  scratch_shapes=[pltpu.SemaphoreType.DMA((nd - 1,)), pltpu.SemaphoreType.DMA((nd - 1,))]
                  + r_scratch)(pack, *r_ops)
    return (outs[0], list(outs[1:])) if riders else outs[0]


def _row(v):
    return v.reshape(1, -1)


def _local_step(x, p, tgt, norm_g, conv_w, conv_b, branch_g, ple_norm_g, b_pg, final_g, w_in, w_out, w_pg, w_pe,
                gather=False, reduce=None):
    saved = []
    xl = x
    for l in range(DEPTH):
        riders = [_gather_rider([w_out, w_pg, w_pe], 0, [1, 2, 3])] if gather and l == 0 else []
        h, proj, got = _inproj(xl, _row(norm_g[l]), w_in, l, f"inproj_{l}", riders)
        if riders:
            w_out, w_pg, w_pe = got
        later = gather and l + 1 < DEPTH
        riders = [_gather_rider([w_in], l + 1, [0])] if later else []
        ya, tl, walked, got = _attn_fwd(proj, f"attn_fwd_{l}", riders)
        if riders:
            w_in, = got
        riders = [_gather_rider([w_out, w_pg, w_pe], l + 1, [1, 2, 3])] if later else []
        y, got = _mix_fwd(proj, ya, conv_w[l], _row(conv_b[l]), _row(branch_g[l]), f"mix_fwd_{l}", riders)
        if riders:
            w_out, w_pg, w_pe = got
        x1, hn = _outproj(y, w_out, l, xl, _row(ple_norm_g[l]), f"outproj_{l}")
        x2, gate, e = _ple_fwd(hn, w_pg, _row(b_pg[l]), p, w_pe, l, x1, f"ple_fwd_{l}")
        saved.append((xl, h, proj, ya, tl, walked, y, x1, hn, gate, e))
        xl = x2

    sq, dx, d_final = _loss_head(xl, tgt, _row(final_g), "loss_head")

    big = [None] * DEPTH
    carried = []
    small = {k: [None] * DEPTH for k in ("norm_g", "conv_w", "conv_b", "branch_g", "ple_norm_g", "b_pg")}
    for l in reversed(range(DEPTH)):
        xl, h, proj, ya, tl, walked, y, x1, hn, gate, e = saved[l]
        du, de, dx1, dy, db_pg, d_ple, _ = _ple_bwd(dx, gate, e, x1, w_pg, _row(ple_norm_g[l]), w_out, l,
                                                    f"ple_bwd_{l}")
        sums = carried
        g_pg = _mm_tn(hn, du, f"grad_w_pg_{l}")
        g_pe = _mm_tn(p, de, f"grad_w_pe_{l}", a_layer=l)
        g_out = _mm_tn(y, dx1, f"grad_w_out_{l}")
        others = [g_out, g_pg, g_pe]
        riders = [_swap_rider(others, [1, 2, 3])] if reduce is not None else []
        dpc, d_cw, d_cb, d_bg_c, got = _convmix_bwd(dy, proj, conv_w[l], _row(conv_b[l]), _row(branch_g[l]),
                                                    f"convmix_bwd_{l}", riders)
        if reduce is not None:
            sums += reduce.add(l, others, [1, 2, 3], got)
        riders = [_scatter_rider([sm[3] for sm in sums], [sm[1] for sm in sums])] if sums else []
        dproj, d_bg_a, got = _attn_bwd(proj, dy, ya, tl, walked, _row(branch_g[l]), dpc, f"attn_bwd_{l}", riders)
        if sums:
            reduce.finish(sums, got)
        g_in = _grad_w_in(h, dproj, f"grad_w_in_{l}")
        riders = [_swap_rider([g_in], [0])] if reduce is not None and l > 0 else []
        dx, d_norm, got = _inproj_bwd(dproj, w_in, l, xl, _row(norm_g[l]), dx1, f"inproj_bwd_{l}", riders)
        big[l] = (g_in, g_out, g_pg, g_pe)
        carried = reduce.add(l, [g_in], [0], got) if riders else []
        small["norm_g"][l] = jnp.sum(d_norm, axis=0)
        small["conv_w"][l] = jnp.sum(d_cw, axis=1)
        small["conv_b"][l] = jnp.sum(d_cb, axis=0)
        small["branch_g"][l] = jnp.concatenate([jnp.sum(d_bg_c, axis=0), jnp.sum(d_bg_a, axis=0)])
        small["ple_norm_g"][l] = jnp.sum(d_ple, axis=0)
        small["b_pg"][l] = jnp.sum(db_pg, axis=0)
    small = {k: jnp.stack(v) for k, v in small.items()}
    small["final_g"] = jnp.sum(d_final, axis=0)
    return sq[0, 0], dx, big, small


SMALL_ORDER = ("norm_g", "conv_w", "conv_b", "branch_g", "ple_norm_g", "b_pg", "final_g")


def _pack(parts, width):
    flat = jnp.concatenate([v.reshape(-1) for v in parts])
    rows = -(-flat.shape[0] // width)
    rows = -(-rows // 8) * 8
    return jnp.pad(flat, (0, rows * width - flat.shape[0])).reshape(rows, width)


def _unpack(packed, like):
    flat = packed.reshape(-1)
    out, off = [], 0
    for v in like:
        out.append(flat[off:off + v.size].reshape(v.shape))
        off += v.size
    return out


def kernel(x, p, norm_g, w_in, conv_w, conv_b, branch_g, w_out, ple_norm_g, w_pg, b_pg, w_pe, final_g, loss_target, m_norm_g, m_w_in, m_conv_w, m_conv_b, m_branch_g, m_w_out, m_ple_norm_g, m_w_pg, m_b_pg, m_w_pe, m_final_g, v_norm_g, v_w_in, v_conv_w, v_conv_b, v_branch_g, v_w_out, v_ple_norm_g, v_w_pg, v_b_pg, v_w_pe, v_final_g):
    ix, iy, ic = _place()
    chip = 2 * ix + iy
    d = x.shape[-1]

    big_w = (w_in, w_out, w_pg, w_pe)
    own = [_cast_into_full(w, chip, ax, f"cast_{i}") for i, (w, ax) in enumerate(zip(big_w, SHARD_AXES))]
    full_in, = _ride_alone(_gather_rider([own[0]], 0, [0]), "gather_w_in_0")
    full_out, full_pg, full_pe = own[1:]
    cw_shard = conv_w.shape[-1]
    cw_slots = _exchange_small(_pack([conv_w], LANES), "exchange_conv_w")
    conv_full = jnp.concatenate([_unpack(cw_slots[2 * k], [conv_w])[0] for k in range(4)], axis=-1)

    reduce = _GradReduce(chip, ic)
    sq, dx, big_g, small_g = _local_step(
        x[0], p[:, 0], loss_target[0], norm_g, conv_full, conv_b, branch_g, ple_norm_g, b_pg, final_g,
        full_in, full_out, full_pg, full_pe, gather=True, reduce=reduce)

    parts = [small_g[k] for k in SMALL_ORDER] + [sq.reshape(1)]
    last = [big_g[0][0]]
    slots, got = _exchange_small(_pack(parts, d), "exchange_small_grads", [_swap_rider(last, [0])])
    sums = reduce.add(0, last, [0], got)
    reduce.finish(sums, _ride_alone(_scatter_rider([sm[3] for sm in sums], [0]), "scatter_shards_last"))
    g_big = reduce.result()
    total = _unpack(_sum_slots(slots, "sum_small"), parts)
    g_small = dict(zip(SMALL_ORDER, total[:-1]))
    loss = 0.5 * total[-1][0] / d
    g_small["conv_w"] = lax.dynamic_slice_in_dim(g_small["conv_w"], chip * cw_shard, cw_shard, axis=2)

    grads = dict(g_small)
    grads.update(w_in=g_big[0], w_out=g_big[1], w_pg=g_big[2], w_pe=g_big[3])
    weights = dict(norm_g=norm_g, w_in=w_in, conv_w=conv_w, conv_b=conv_b, branch_g=branch_g, w_out=w_out,
                   ple_norm_g=ple_norm_g, w_pg=w_pg, b_pg=b_pg, w_pe=w_pe, final_g=final_g)
    ms = dict(norm_g=m_norm_g, w_in=m_w_in, conv_w=m_conv_w, conv_b=m_conv_b, branch_g=m_branch_g, w_out=m_w_out,
              ple_norm_g=m_ple_norm_g, w_pg=m_w_pg, b_pg=m_b_pg, w_pe=m_w_pe, final_g=m_final_g)
    vs = dict(norm_g=v_norm_g, w_in=v_w_in, conv_w=v_conv_w, conv_b=v_conv_b, branch_g=v_branch_g, w_out=v_w_out,
              ple_norm_g=v_ple_norm_g, w_pg=v_w_pg, b_pg=v_b_pg, w_pe=v_w_pe, final_g=v_final_g)
    names = ("norm_g", "w_in", "conv_w", "conv_b", "branch_g", "w_out", "ple_norm_g", "w_pg", "b_pg", "w_pe", "final_g")
    delta, new_m, new_v = {}, {}, {}
    for k in ("w_in", "w_out", "w_pg", "w_pe"):
        shp = weights[k].shape
        two = lambda a: a.reshape(-1, shp[-1])
        gr, dl, mn, vn = _adamw(two(weights[k]), two(grads[k]), two(ms[k]), two(vs[k]), f"adamw_{k}")
        delta[k], new_m[k], new_v[k] = dl.reshape(shp), mn.reshape(shp), vn.reshape(shp)
        grads[k] = gr.reshape(shp)
    like = [weights[k] for k in SMALL_ORDER]
    packs = [_pack([src[k] for k in SMALL_ORDER], d) for src in (weights, grads, ms, vs)]
    outs = _adamw(*packs, "adamw_small")
    for res, o in zip((delta, new_m, new_v), outs[1:]):
        res.update(dict(zip(SMALL_ORDER, _unpack(o, like))))

    return (loss, dx[None], *[grads[k] for k in names], *[delta[k] for k in names],
            *[new_m[k] for k in names], *[new_v[k] for k in names])
```

```python
import math

import jax
import jax.numpy as jnp
from jax import lax
from jax.experimental import pallas as pl
from jax.experimental.pallas import tpu as pltpu

F32 = jnp.float32
BF16 = jnp.bfloat16
EPS = 1e-6
HEAD = 64
LANES = 128
ATT_TK = 256
ATT_TQ = 512
ATT_ROWS = 128
ALIVE_LOG = -105.0
DEPTH = 2
VMEM_LIMIT = 56 * 1024 * 1024
MESH = pl.DeviceIdType.MESH
ANY = pl.BlockSpec(memory_space=pl.ANY)

ADAM_LR = 0.001
ADAM_B1 = 0.9
ADAM_B2 = 0.999
ADAM_EPS = 1e-08
ADAM_WD = 0.01
ADAM_STEP = 10


def _pcall(body, **kw):
    return pl.pallas_call(body, **kw)


def _cp(n_axes):
    return pltpu.CompilerParams(dimension_semantics=("arbitrary",) * n_axes, vmem_limit_bytes=VMEM_LIMIT)


def _tile(n, pref):
    return pref if n % pref == 0 else n


def _split_dot(a, b, passes):
    out = None
    rem = a
    for _ in range(passes):
        hi = rem.astype(BF16)
        t = jnp.dot(hi, b, preferred_element_type=F32)
        out = t if out is None else out + t
        rem = rem - hi.astype(F32)
    return out


def _group_mat():
    r = lax.broadcasted_iota(jnp.int32, (LANES, LANES), 0) // HEAD
    c = lax.broadcasted_iota(jnp.int32, (LANES, LANES), 1) // HEAD
    return jnp.where(r == c, 1.0 / HEAD, 0.0).astype(BF16)


def _group_mean(v, gm):
    return _split_dot(v, gm, 2)


def _sigmoid(z):
    return 1.0 / (1.0 + jnp.exp(-z))


def _dot_nt(a, b):
    return lax.dot_general(a, b, (((1,), (1,)), ((), ())), preferred_element_type=F32)


def _dot_tn(a, b):
    return lax.dot_general(a, b, (((0,), (0,)), ((), ())), preferred_element_type=F32)


def _cast_into_full(w, chip, axis, name):
    _, r, c = w.shape
    tr = _tile(r, 256)
    nb = r // tr
    full = (DEPTH, 4 * r, c) if axis == 0 else (DEPTH, r, 4 * c)

    def body(k_ref, w_ref, o_ref):
        o_ref[...] = w_ref[...].astype(BF16)

    out_map = (lambda l, i, k: (l, k[0] * nb + i, 0)) if axis == 0 else (lambda l, i, k: (l, i, k[0]))
    grid_spec = pltpu.PrefetchScalarGridSpec(
        num_scalar_prefetch=1, grid=(DEPTH, nb),
        in_specs=[pl.BlockSpec((1, tr, c), lambda l, i, k: (l, i, 0))],
        out_specs=pl.BlockSpec((1, tr, c), out_map))
    return _pcall(body, name=name, grid_spec=grid_spec, out_shape=jax.ShapeDtypeStruct(full, BF16),
                  compiler_params=_cp(2))(chip.reshape(1).astype(jnp.int32), w)


def _rms_bwd_rows(dh, xv, g):
    r = lax.rsqrt(jnp.mean(xv * xv, axis=-1, keepdims=True) + EPS)
    xn = xv * r
    dxn = dh * g
    dx = r * (dxn - xn * jnp.mean(dxn * xn, axis=-1, keepdims=True))
    return dx, dh * xn


def _colsum8(v):
    tm, d = v.shape
    return jnp.sum(v.reshape(tm // 8, 8, d), axis=0)


def _inproj(x, g, w, layer, name, riders=()):
    s, d = x.shape
    n = w.shape[2]
    sw = d // 2
    ns = n // sw
    tm = _tile(s, 512)

    def body(*refs):
        own, riders_end = _riders_run(riders, refs, 3, 2, 0, pl.program_id(0), s // tm)
        x_ref, g_ref, w_ref, h_ref, o_ref = own
        xv = x_ref[...]
        r = lax.rsqrt(jnp.mean(xv * xv, axis=-1, keepdims=True) + EPS)
        h = (xv * r * g_ref[...]).astype(BF16)
        h_ref[...] = h
        for k in range(ns):
            o_ref[k] = jnp.dot(h, w_ref[0, :, k * sw:(k + 1) * sw], preferred_element_type=F32).astype(BF16)
        riders_end()

    r_ops, r_shapes, r_scratch, r_aliases = _riders_plumb(riders, 3, 2)
    outs = _pcall(body, name=name, grid=(s // tm,),
                  in_specs=[pl.BlockSpec((tm, d), lambda m: (m, 0)), pl.BlockSpec((1, d), lambda m: (0, 0)),
                            pl.BlockSpec((1, d, n), lambda m: (layer, 0, 0))] + [ANY] * len(r_ops),
                  out_specs=[pl.BlockSpec((tm, d), lambda m: (m, 0)), pl.BlockSpec((ns, tm, sw), lambda m: (0, m, 0))]
                  + [ANY] * len(r_shapes),
                  out_shape=[jax.ShapeDtypeStruct((s, d), BF16), jax.ShapeDtypeStruct((ns, s, sw), BF16)] + r_shapes,
                  input_output_aliases=r_aliases, scratch_shapes=r_scratch,
                  compiler_params=_cp(1))(x, g, w, *r_ops)
    return outs[0], outs[1], list(outs[2:])


def _softplus_parts(z):
    lm = jnp.minimum(-z, 0.0) - jnp.log(1.0 + jnp.exp(-jnp.abs(z)))
    return lm, lm + z


def _attn_tiles(s):
    tk = _tile(s, ATT_TK)
    tq = _tile(s, ATT_TQ)
    return tk, tq, tq // tk, min(ATT_ROWS, tq)


def _diag_work(chains, d, rows, tk):
    work = []
    for n, (_, r0) in enumerate(chains):
        if r0 + rows - 1 <= d * tk:
            continue
        kw = tk // 2 if (tk % 2 == 0 and r0 + rows <= d * tk + tk // 2) else tk
        if r0 >= d * tk + kw:
            mask = None
        else:
            row = lax.broadcasted_iota(jnp.int32, (rows, kw), 0)
            col = lax.broadcasted_iota(jnp.int32, (rows, kw), 1)
            mask = col + d * tk < row + r0
        work.append((n, kw, mask))
    return work


def _both(mask, gate):
    if mask is None:
        return gate
    if gate is None:
        return mask
    return jnp.logical_and(mask, gate)


def _any_alive(rsums):
    m = rsums[0]
    for r in rsums[1:]:
        m = jnp.maximum(m, r)
    return jnp.max((m > ALIVE_LOG).astype(jnp.int32))


def _attn_fwd(proj, name, riders=()):
    _, s, sw = proj.shape
    nhp = sw // LANES
    tk, tq, nd, rows = _attn_tiles(s)
    nq = s // tq
    scale = 1.0 / math.sqrt(HEAD)

    def body(*refs):
        i = pl.program_id(1)
        own, riders_end = _riders_run(riders, refs, 3, 3, 1, pl.program_id(0) * nq + i, nhp * nq)
        q_ref, k_ref, v_ref, o_ref, tl_ref, nw_ref, acc_ref = own
        tri = (lax.broadcasted_iota(jnp.int32, (tk, tk), 0) >
               lax.broadcasted_iota(jnp.int32, (tk, tk), 1)).astype(BF16)
        lane = lax.broadcasted_iota(jnp.int32, (tq, LANES), 1)
        q = q_ref[0] * jnp.asarray(scale, BF16)
        qms = [jnp.where((lane // HEAD) == h, q, jnp.zeros_like(q)) for h in range(2)]
        acc_ref[...] = jnp.zeros_like(acc_ref)
        chains = [(h, r0) for h in range(2) for r0 in range(0, tq, rows)]
        qparts = [qms[h][r0:r0 + rows] for h, r0 in chains]

        def block(rsums, tiles, items):
            kjs = [k_ref[0, pl.ds(pl.multiple_of(j * tk, tk), tk), :] for j in tiles]
            vjs = [v_ref[0, pl.ds(pl.multiple_of(j * tk, tk), tk), :] for j in tiles]
            zs = [_dot_nt(qparts[n], kjs[t][:kw]) for n, t, kw, _, _ in items]
            lms, lss, css = [], [], []
            for z, (n, t, kw, mask, gate) in zip(zs, items):
                lm, ls = _softplus_parts(z)
                keep = _both(mask, gate)
                if keep is not None:
                    lm = jnp.where(keep, lm, 0.0)
                lms.append(lm)
                lss.append(ls)
                css.append(_split_dot(lm, tri[:kw, :kw], 2))
            cur = list(rsums)
            for lm, ls, cs, (n, t, kw, mask, gate) in zip(lms, lss, css, items):
                h, r0 = chains[n]
                a = jnp.exp(ls + (cur[n] + cs))
                keep = _both(mask, gate)
                if keep is not None:
                    a = jnp.where(keep, a, 0.0)
                acc_ref[h, r0:r0 + rows, :] += jnp.dot(a.astype(BF16), vjs[t][:kw], preferred_element_type=F32)
                cur[n] = cur[n] + jnp.sum(lm, axis=1, keepdims=True)
            return tuple(cur)

        everyone = [(n, 0, tk, None, None) for n in range(len(chains))]
        upper = [n for n, (_, r0) in enumerate(chains) if r0 >= tk]
        lower = [n for n, (_, r0) in enumerate(chains) if r0 < tk]
        left = jnp.maximum(i * nd - 1, 0)
        rsums = (jnp.zeros((rows, 1), F32),) * len(chains)
        for d in reversed(range(1, nd)):
            rsums = block(rsums, [i * nd + d], [(n, 0, kw, m, None) for n, kw, m in _diag_work(chains, d, rows, tk)])
        rsums = block(rsums, [i * nd, left],
                      [(n, 0, kw, m, None) for n, kw, m in _diag_work(chains, 0, rows, tk)]
                      + [(n, 1, tk, None, i > 0) for n in lower])

        if upper:
            too = (i > 0) & (_any_alive([rsums[n] for n in upper]) > 0)
            rsums = lax.cond(too, lambda rs: block(rs, [left], [(n, 0, tk, None, None) for n in upper]),
                             lambda rs: rs, rsums)
            too = too.astype(jnp.int32)
        else:
            too = jnp.int32(0)

        def walk(c):
            jj, rs, _ = c
            rs = block(rs, [i * nd - 2 - jj], everyone)
            return jj + 1, rs, _any_alive(rs)

        whole, rsums, _ = lax.while_loop(lambda c: (c[0] < i * nd - 1) & (c[2] > 0), walk,
                                         (jnp.int32(0), rsums, _any_alive(rsums)))
        for n, (h, r0) in enumerate(chains):
            tl_ref[h, r0:r0 + rows, :] = rsums[n]
        nw_ref[0] = (jnp.zeros((8, LANES), jnp.int32) + (2 * whole + too)).astype(F32)
        o_ref[...] = jnp.where(lane < HEAD, acc_ref[0], acc_ref[1]).astype(BF16)
        riders_end()

    r_ops, r_shapes, r_scratch, r_aliases = _riders_plumb(riders, 3, 3)
    outs = _pcall(
        body, name=name, grid=(nhp, nq),
        in_specs=[pl.BlockSpec((1, tq, LANES), lambda hp, i: (4, i, hp)),
                  pl.BlockSpec((1, s, LANES), lambda hp, i: (5, 0, hp)),
                  pl.BlockSpec((1, s, LANES), lambda hp, i: (6, 0, hp))] + [ANY] * len(r_ops),
        out_specs=[pl.BlockSpec((tq, LANES), lambda hp, i: (i, hp)),
                   pl.BlockSpec((2, tq, 1), lambda hp, i: (hp, i, 0)),
                   pl.BlockSpec((1, 8, LANES), lambda hp, i: (hp * nq + i, 0, 0))] + [ANY] * len(r_shapes),
        out_shape=[jax.ShapeDtypeStruct((s, sw), BF16), jax.ShapeDtypeStruct((2 * nhp, s, 1), F32),
                   jax.ShapeDtypeStruct((nhp * nq, 8, LANES), F32)] + r_shapes,
        input_output_aliases=r_aliases,
        scratch_shapes=[pltpu.VMEM((2, tq, LANES), F32)] + r_scratch,
        compiler_params=_cp(2))(proj, proj, proj, *r_ops)
    return outs[0], outs[1], outs[2], list(outs[3:])


def _conv_rows(cc_ref, ch_ref, w_ref, b_ref, r, tc):
    r0 = pl.multiple_of(r * tc, tc)
    u = cc_ref[0, pl.ds(r0, tc), :].astype(F32) * ch_ref[0, pl.ds(r0, tc), :].astype(F32)
    p0 = pl.multiple_of(jnp.maximum(r0 - 16, 0), 16)
    up = cc_ref[0, pl.ds(p0, 16), :].astype(F32) * ch_ref[0, pl.ds(p0, 16), :].astype(F32)
    up = up * (r > 0).astype(F32)
    prev1 = up[15:16, :]
    prev2 = up[14:15, :]
    rid = lax.broadcasted_iota(jnp.int32, u.shape, 0)
    s1 = jnp.where(rid == 0, prev1, pltpu.roll(u, 1, axis=0))
    s2 = jnp.where(rid == 0, prev2, jnp.where(rid == 1, prev1, pltpu.roll(u, 2, axis=0)))
    cv = b_ref[...] + s2 * w_ref[0:1, :] + s1 * w_ref[1:2, :] + u * w_ref[2:3, :]
    return r0, u, s1, s2, cv


def _mix_fwd(proj, ya, conv_w, conv_b, bg, name, riders=()):
    _, s, sw = proj.shape
    nh = sw // LANES
    tc = _tile(s, 256)

    def body(*refs):
        c = pl.program_id(0)
        own, riders_end = _riders_run(riders, refs, 9, 1, 0, c, 2 * nh)
        cb_ref, cc_ref, ch_ref, cz_ref, ya_ref, az_ref, w_ref, b_ref, g_ref, y_ref = own
        gm = _group_mat()

        def finish(r0, yv, zg):
            n = yv * lax.rsqrt(_group_mean(yv * yv, gm) + EPS)
            y_ref[pl.ds(r0, tc), :] = (n * g_ref[...] * (zg * _sigmoid(zg))).astype(BF16)

        @pl.when(c < nh)
        def _():
            def step(r, carry):
                r0, _, _, _, cv = _conv_rows(cc_ref, ch_ref, w_ref, b_ref, r, tc)
                yc = cb_ref[0, pl.ds(r0, tc), :].astype(F32) * cv
                finish(r0, yc, cz_ref[0, pl.ds(r0, tc), :].astype(F32))
                return carry
            lax.fori_loop(0, s // tc, step, 0)

        @pl.when(c >= nh)
        def _():
            def step(r, carry):
                r0 = pl.multiple_of(r * tc, tc)
                finish(r0, ya_ref[pl.ds(r0, tc), :].astype(F32), az_ref[0, pl.ds(r0, tc), :].astype(F32))
                return carry
            lax.fori_loop(0, s // tc, step, 0)

        riders_end()

    def sec(k):
        return pl.BlockSpec((1, s, LANES), lambda c: (k, 0, jnp.minimum(c, nh - 1)))

    r_ops, r_shapes, r_scratch, r_aliases = _riders_plumb(riders, 9, 1)
    outs = _pcall(
        body, name=name, grid=(2 * nh,),
        in_specs=[sec(0), sec(1), sec(2), sec(3),
                  pl.BlockSpec((s, LANES), lambda c: (0, jnp.maximum(c - nh, 0))),
                  pl.BlockSpec((1, s, LANES), lambda c: (7, 0, jnp.maximum(c - nh, 0))),
                  pl.BlockSpec((3, LANES), lambda c: (0, jnp.minimum(c, nh - 1))),
                  pl.BlockSpec((1, LANES), lambda c: (0, jnp.minimum(c, nh - 1))),
                  pl.BlockSpec((1, LANES), lambda c: (0, c))] + [ANY] * len(r_ops),
        out_specs=[pl.BlockSpec((s, LANES), lambda c: (0, c))] + [ANY] * len(r_shapes),
        out_shape=[jax.ShapeDtypeStruct((s, 2 * sw), BF16)] + r_shapes,
        input_output_aliases=r_aliases, scratch_shapes=r_scratch, compiler_params=_cp(1),
    )(proj, proj, proj, proj, ya, proj, conv_w, conv_b, bg, *r_ops)
    return outs[0], list(outs[1:])


def _outproj(y, w, layer, x, g, name):
    s, d = x.shape
    tm = _tile(s, 512)

    def body(y_ref, w_ref, x_ref, g_ref, x1_ref, hn_ref):
        x1 = x_ref[...] + jnp.dot(y_ref[...], w_ref[0], preferred_element_type=F32)
        x1_ref[...] = x1
        r = lax.rsqrt(jnp.mean(x1 * x1, axis=-1, keepdims=True) + EPS)
        hn_ref[...] = (x1 * r * g_ref[...]).astype(BF16)

    row = lambda m: (m, 0)
    fix = lambda m: (0, 0)
    return _pcall(body, name=name, grid=(s // tm,),
                  in_specs=[pl.BlockSpec((tm, d), row), pl.BlockSpec((1, d, d), lambda m: (layer, 0, 0)),
                            pl.BlockSpec((tm, d), row), pl.BlockSpec((1, d), fix)],
                  out_specs=[pl.BlockSpec((tm, d), row), pl.BlockSpec((tm, d), row)],
                  out_shape=[jax.ShapeDtypeStruct((s, d), F32), jax.ShapeDtypeStruct((s, d), BF16)],
                  compiler_params=_cp(1))(y, w, x, g)


def _ple_fwd(hn, w_pg, b_pg, p, w_pe, layer, x1, name):
    s, d = x1.shape
    pd = p.shape[2]
    tm = _tile(s, 512)

    def body(hn_ref, wg_ref, b_ref, p_ref, we_ref, x1_ref, x2_ref, gate_ref, e_ref):
        gate = _sigmoid(jnp.dot(hn_ref[...], wg_ref[0], preferred_element_type=F32) + b_ref[...])
        e = jnp.dot(p_ref[0].astype(BF16), we_ref[0], preferred_element_type=F32)
        x2_ref[...] = x1_ref[...] + gate * e
        gate_ref[...] = gate.astype(BF16)
        e_ref[...] = e.astype(BF16)

    row = lambda m: (m, 0)
    fix = lambda m: (0, 0)
    return _pcall(body, name=name, grid=(s // tm,),
                  in_specs=[pl.BlockSpec((tm, d), row), pl.BlockSpec((1, d, d), lambda m: (layer, 0, 0)),
                            pl.BlockSpec((1, d), fix), pl.BlockSpec((1, tm, pd), lambda m: (layer, m, 0)),
                            pl.BlockSpec((1, pd, d), lambda m: (layer, 0, 0)), pl.BlockSpec((tm, d), row)],
                  out_specs=[pl.BlockSpec((tm, d), row)] * 3,
                  out_shape=[jax.ShapeDtypeStruct((s, d), F32), jax.ShapeDtypeStruct((s, d), BF16),
                             jax.ShapeDtypeStruct((s, d), BF16)],
                  compiler_params=_cp(1))(hn, w_pg, b_pg, p, w_pe, x1)


def _loss_head(x, tgt, g, name):
    s, d = x.shape
    tm = _tile(s, 512)

    def body(x_ref, t_ref, g_ref, l_ref, dx_ref, dg_ref):
        m = pl.program_id(0)

        @pl.when(m == 0)
        def _():
            l_ref[...] = jnp.zeros_like(l_ref)
            dg_ref[...] = jnp.zeros_like(dg_ref)

        xv = x_ref[...]
        gv = g_ref[...]
        r = lax.rsqrt(jnp.mean(xv * xv, axis=-1, keepdims=True) + EPS)
        xn = xv * r
        err = xn * gv - t_ref[...]
        l_ref[...] += jnp.sum(err * err)
        dy = err * (1.0 / d)
        dxn = dy * gv
        dx_ref[...] = r * (dxn - xn * jnp.mean(dxn * xn, axis=-1, keepdims=True))
        dg_ref[...] += _colsum8(dy * xn)

    row = lambda m: (m, 0)
    fix = lambda m: (0, 0)
    return _pcall(body, name=name, grid=(s // tm,),
                  in_specs=[pl.BlockSpec((tm, d), row), pl.BlockSpec((tm, d), row), pl.BlockSpec((1, d), fix)],
                  out_specs=[pl.BlockSpec((8, LANES), fix), pl.BlockSpec((tm, d), row), pl.BlockSpec((8, d), fix)],
                  out_shape=[jax.ShapeDtypeStruct((8, LANES), F32), jax.ShapeDtypeStruct((s, d), F32),
                             jax.ShapeDtypeStruct((8, d), F32)],
                  compiler_params=_cp(1))(x, tgt, g)


def _ple_bwd(dx2, gate, e, x1, w_pg, g_ple, w_out, layer, name, riders=()):
    s, d = dx2.shape
    tm = _tile(s, 512)

    def body(*refs):
        m = pl.program_id(0)
        own, riders_end = _riders_run(riders, refs, 7, 6, 0, m, s // tm)
        (dx2_ref, gate_ref, e_ref, x1_ref, wg_ref, g_ref, wo_ref,
         du_ref, de_ref, dx1_ref, dy_ref, db_ref, dg_ref) = own

        @pl.when(m == 0)
        def _():
            db_ref[...] = jnp.zeros_like(db_ref)
            dg_ref[...] = jnp.zeros_like(dg_ref)

        dx2v = dx2_ref[...]
        gate = gate_ref[...].astype(F32)
        du = dx2v * e_ref[...].astype(F32) * gate * (1.0 - gate)
        de_ref[...] = (dx2v * gate).astype(BF16)
        dub = du.astype(BF16)
        du_ref[...] = dub
        db_ref[...] += _colsum8(du)
        dhn = _dot_nt(dub, wg_ref[0])
        dxr, dgr = _rms_bwd_rows(dhn, x1_ref[...], g_ref[...])
        dx1 = dx2v + dxr
        dx1_ref[...] = dx1
        dg_ref[...] += _colsum8(dgr)
        dy_ref[...] = _dot_nt(dx1.astype(BF16), wo_ref[0]).astype(BF16)
        riders_end()

    row = lambda m: (m, 0)
    fix = lambda m: (0, 0)
    t = pl.BlockSpec((tm, d), row)
    r_ops, r_shapes, r_scratch, r_aliases = _riders_plumb(riders, 7, 6)
    outs = _pcall(body, name=name, grid=(s // tm,),
                  in_specs=[t, t, t, t, pl.BlockSpec((1, d, d), lambda m: (layer, 0, 0)), pl.BlockSpec((1, d), fix),
                            pl.BlockSpec((1, d, d), lambda m: (layer, 0, 0))] + [ANY] * len(r_ops),
                  out_specs=[t, t, t, t, pl.BlockSpec((8, d), fix), pl.BlockSpec((8, d), fix)] + [ANY] * len(r_shapes),
                  out_shape=[jax.ShapeDtypeStruct((s, d), BF16), jax.ShapeDtypeStruct((s, d), BF16),
                             jax.ShapeDtypeStruct((s, d), F32), jax.ShapeDtypeStruct((s, d), BF16),
                             jax.ShapeDtypeStruct((8, d), F32), jax.ShapeDtypeStruct((8, d), F32)] + r_shapes,
                  input_output_aliases=r_aliases, scratch_shapes=r_scratch,
                  compiler_params=_cp(1))(dx2, gate, e, x1, w_pg, g_ple, w_out, *r_ops)
    return tuple(outs[:6]) + (list(outs[6:]),)


def _mm_tn(a, b, name, a_layer=None):
    s, ka = a.shape[-2:]
    n = b.shape[1]
    tn = _tile(n, 1024)
    ns = n // tn
    tk = _tile(s, 512)
    nk = s // tk

    def body(a_ref, b_ref, o_ref, acc_ref):
        k = pl.program_id(1)

        @pl.when(k == 0)
        def _():
            acc_ref[...] = jnp.zeros_like(acc_ref)

        av = a_ref[...] if a_layer is None else a_ref[0]
        acc_ref[...] += _dot_tn(av.astype(BF16), b_ref[...].astype(BF16))

        @pl.when(k == nk - 1)
        def _():
            o_ref[...] = acc_ref[...]

    a_spec = (pl.BlockSpec((tk, ka), lambda j, k: (k, 0)) if a_layer is None
              else pl.BlockSpec((1, tk, ka), lambda j, k: (a_layer, k, 0)))
    return _pcall(body, name=name, grid=(ns, nk),
                  in_specs=[a_spec, pl.BlockSpec((tk, tn), lambda j, k: (k, j))],
                  out_specs=pl.BlockSpec((ka, tn), lambda j, k: (0, j)),
                  out_shape=jax.ShapeDtypeStruct((ka, n), F32),
                  scratch_shapes=[pltpu.VMEM((ka, tn), F32)], compiler_params=_cp(2))(a, b)


def _norm_gate_bwd(dy, yv, zg, g, gm):
    r = lax.rsqrt(_group_mean(yv * yv, gm) + EPS)
    n = yv * r
    sg = _sigmoid(zg)
    sil = zg * sg
    dzg = dy * n * g * (sg * (1.0 + zg * (1.0 - sg)))
    dn = dy * g * sil
    dyv = r * (dn - n * _group_mean(dn * n, gm))
    return dyv, dzg, dy * n * sil


def _convmix_bwd(dy, proj, conv_w, conv_b, bg, name, riders=()):
    _, s, sw = proj.shape
    nh = sw // LANES
    tc = _tile(s, 256)
    nr = s // tc

    def body(*refs):
        own, riders_end = _riders_run(riders, refs, 8, 4, 1, pl.program_id(0), nh)
        (dy_ref, cb_ref, cc_ref, ch_ref, cz_ref, w_ref, b_ref, g_ref,
         dp_ref, dw_ref, db_ref, dg_ref, dcv_ref) = own
        gm = _group_mat()
        dcv_ref[pl.ds(s, 8), :] = jnp.zeros((8, LANES), F32)

        def pass1(r, carry):
            dw0, dw1, dw2, db, dg = carry
            r0, u, s1, s2, cv = _conv_rows(cc_ref, ch_ref, w_ref, b_ref, r, tc)
            cb = cb_ref[0, pl.ds(r0, tc), :].astype(F32)
            dyc, dcz, dgr = _norm_gate_bwd(dy_ref[pl.ds(r0, tc), :].astype(F32), cb * cv,
                                           cz_ref[0, pl.ds(r0, tc), :].astype(F32), g_ref[...], gm)
            dp_ref[0, pl.ds(r0, tc), :] = (dyc * cv).astype(BF16)
            dp_ref[3, pl.ds(r0, tc), :] = dcz.astype(BF16)
            dcv = dyc * cb
            dcv_ref[pl.ds(r0, tc), :] = dcv
            return (dw0 + _colsum8(dcv * s2), dw1 + _colsum8(dcv * s1), dw2 + _colsum8(dcv * u),
                    db + _colsum8(dcv), dg + _colsum8(dgr))

        z8 = jnp.zeros((8, LANES), F32)
        dw0, dw1, dw2, db, dg = lax.fori_loop(0, nr, pass1, (z8, z8, z8, z8, z8))
        dw_ref[0] = dw0
        dw_ref[1] = dw1
        dw_ref[2] = dw2
        db_ref[...] = db
        dg_ref[...] = dg

        def pass2(r, carry):
            r0 = pl.multiple_of(r * tc, tc)
            dcv = dcv_ref[pl.ds(r0, tc), :]
            nxt = dcv_ref[pl.ds(pl.multiple_of(r0 + tc, 8), 8), :]
            rid = lax.broadcasted_iota(jnp.int32, dcv.shape, 0)
            n1 = jnp.where(rid == tc - 1, nxt[0:1, :], pltpu.roll(dcv, tc - 1, axis=0))
            n2 = jnp.where(rid == tc - 1, nxt[1:2, :],
                           jnp.where(rid == tc - 2, nxt[0:1, :], pltpu.roll(dcv, tc - 2, axis=0)))
            du = dcv * w_ref[2:3, :] + n1 * w_ref[1:2, :] + n2 * w_ref[0:1, :]
            dp_ref[1, pl.ds(r0, tc), :] = (du * ch_ref[0, pl.ds(r0, tc), :].astype(F32)).astype(BF16)
            dp_ref[2, pl.ds(r0, tc), :] = (du * cc_ref[0, pl.ds(r0, tc), :].astype(F32)).astype(BF16)
            return carry

        lax.fori_loop(0, nr, pass2, 0)
        riders_end()

    def sec(k):
        return pl.BlockSpec((1, s, LANES), lambda c: (k, 0, c))

    col = lambda c: (0, c)
    r_ops, r_shapes, r_scratch, r_aliases = _riders_plumb(riders, 8, 4)
    outs = _pcall(
        body, name=name, grid=(nh,),
        in_specs=[pl.BlockSpec((s, LANES), col), sec(0), sec(1), sec(2), sec(3),
                  pl.BlockSpec((3, LANES), col), pl.BlockSpec((1, LANES), col), pl.BlockSpec((1, LANES), col)]
        + [ANY] * len(r_ops),
        out_specs=[pl.BlockSpec((4, s, LANES), lambda c: (0, 0, c)), pl.BlockSpec((3, 8, LANES), lambda c: (0, 0, c)),
                   pl.BlockSpec((8, LANES), col), pl.BlockSpec((8, LANES), col)] + [ANY] * len(r_shapes),
        out_shape=[jax.ShapeDtypeStruct((8, s, sw), BF16), jax.ShapeDtypeStruct((3, 8, sw), F32),
                   jax.ShapeDtypeStruct((8, sw), F32), jax.ShapeDtypeStruct((8, sw), F32)] + r_shapes,
        input_output_aliases=r_aliases,
        scratch_shapes=[pltpu.VMEM((s + 8, LANES), F32)] + r_scratch, compiler_params=_cp(1),
    )(dy, proj, proj, proj, proj, conv_w, conv_b, bg, *r_ops)
    return tuple(outs[:4]) + (list(outs[4:]),)


def _attn_bwd(proj, dy, ya, tl, walked, bg, buf, name, riders=()):
    _, s, sw = proj.shape
    nhp = sw // LANES
    tk, t, nd, rows_c = _attn_tiles(s)
    nq = s // t
    scale = 1.0 / math.sqrt(HEAD)

    def body(*refs):
        step = pl.program_id(1)
        i = nq - 1 - step
        own, riders_end = _riders_run(riders, refs, 10, 2, 3, pl.program_id(0) * nq + step, nhp * nq)
        (q_ref, k_ref, v_ref, az_ref, dy_ref, ya_ref, tl_ref, nw_ref, g_ref, buf_ref, out_ref, dg_ref,
         dka_ref, dva_ref, dqa_ref) = own

        @pl.when(step == 0)
        def _():
            dka_ref[...] = jnp.zeros_like(dka_ref)
            dva_ref[...] = jnp.zeros_like(dva_ref)
            dg_ref[...] = jnp.zeros_like(dg_ref)

        dyv, dzg, dgr = _norm_gate_bwd(dy_ref[...].astype(F32), ya_ref[...].astype(F32), az_ref[0].astype(F32),
                                       g_ref[...], _group_mat())
        out_ref[3] = dzg.astype(BF16)
        dg_ref[...] += _colsum8(dgr)

        tri = (lax.broadcasted_iota(jnp.int32, (tk, tk), 0) <=
               lax.broadcasted_iota(jnp.int32, (tk, tk), 1)).astype(BF16)
        lane = lax.broadcasted_iota(jnp.int32, (t, LANES), 1)
        q = q_ref[0] * jnp.asarray(scale, BF16)
        do = dyv.astype(BF16)
        qms = [jnp.where((lane // HEAD) == h, q, jnp.zeros_like(q)) for h in range(2)]
        doms = [jnp.where((lane // HEAD) == h, do, jnp.zeros_like(do)) for h in range(2)]
        dqa_ref[...] = jnp.zeros_like(dqa_ref)
        chains = [(h, r0) for h in range(2) for r0 in range(0, t, rows_c)]
        qparts = [qms[h][r0:r0 + rows_c] for h, r0 in chains]
        doparts = [doms[h][r0:r0 + rows_c] for h, r0 in chains]
        tots = [tl_ref[h, r0:r0 + rows_c, :] for h, r0 in chains]

        def block(carry, tiles, items):
            k0s = [pl.multiple_of(j * tk, tk) for j in tiles]
            kjs = [k_ref[0, pl.ds(k0, tk), :] for k0 in k0s]
            vjs = [v_ref[0, pl.ds(k0, tk), :] for k0 in k0s]
            zs = [_dot_nt(qparts[n], kjs[t][:kw]) for n, t, kw, _, _ in items]
            das = [_dot_nt(doparts[n], vjs[t][:kw]) for n, t, kw, _, _ in items]
            keeps = [_both(mask, gate) for _, _, _, mask, gate in items]
            lms, lss, cls = [], [], []
            for z, keep, (n, t, kw, _, _) in zip(zs, keeps, items):
                lm, ls = _softplus_parts(z)
                if keep is not None:
                    lm = jnp.where(keep, lm, 0.0)
                lms.append(lm)
                lss.append(ls)
                cls.append(_split_dot(lm, tri[:kw, :kw], 2))
            cur = list(carry)
            psums, abs_, gs, cgs = [], [], [], []
            for lm, ls, cl, da, keep, (n, t, kw, _, _) in zip(lms, lss, cls, das, keeps, items):
                psum, gsum = cur[n]
                a = jnp.exp(ls + (tots[n] - psum - cl))
                if keep is not None:
                    a = jnp.where(keep, a, 0.0)
                g = a * da
                psums.append(gsum)
                gs.append(g)
                abs_.append(a.astype(BF16))
                cgs.append(_split_dot(g, tri[:kw, :kw], 1))
                cur[n] = (psum + jnp.sum(lm, axis=1, keepdims=True), gsum + jnp.sum(g, axis=1, keepdims=True))
            dks, dvs = {}, {}
            for ls, a, g, cg, gsum, keep, (n, t, kw, _, _) in zip(lss, abs_, gs, cgs, psums, keeps, items):
                h, r0 = chains[n]
                dz = g - jnp.exp(ls) * (gsum + cg)
                if keep is not None:
                    dz = jnp.where(keep, dz, 0.0)
                dz = dz.astype(BF16)
                dqa_ref[h, r0:r0 + rows_c, :] += jnp.dot(dz, kjs[t][:kw], preferred_element_type=F32)
                dkh = _dot_tn(dz, qparts[n])
                dvh = _dot_tn(a, doparts[n])
                dks[t, kw] = dkh if (t, kw) not in dks else dks[t, kw] + dkh
                dvs[t, kw] = dvh if (t, kw) not in dvs else dvs[t, kw] + dvh
            for t, kw in dks:
                dka_ref[pl.ds(k0s[t], kw), :] += dks[t, kw]
                dva_ref[pl.ds(k0s[t], kw), :] += dvs[t, kw]
            return tuple(cur)

        z1 = jnp.zeros((rows_c, 1), F32)
        everyone = [(n, 0, tk, None, None) for n in range(len(chains))]
        upper = [n for n, (_, r0) in enumerate(chains) if r0 >= tk]
        lower = [n for n, (_, r0) in enumerate(chains) if r0 < tk]
        left = jnp.maximum(i * nd - 1, 0)
        code = jnp.clip(jnp.max(nw_ref[0].astype(jnp.int32)), 0, 2 * left + 1)
        too = jnp.where(i > 0, code % 2, 0)
        whole = jnp.minimum(code // 2, left)
        carry = lax.fori_loop(left - whole, left, lambda j, c: block(c, [j], everyone), ((z1, z1),) * len(chains))
        if upper:
            carry = lax.cond(too > 0, lambda c: block(c, [left], [(n, 0, tk, None, None) for n in upper]),
                             lambda c: c, carry)
        carry = block(carry, [left, i * nd],
                      [(n, 0, tk, None, i > 0) for n in lower]
                      + [(n, 1, kw, m, None) for n, kw, m in _diag_work(chains, 0, rows_c, tk)])
        for d in range(1, nd):
            carry = block(carry, [i * nd + d], [(n, 0, kw, m, None) for n, kw, m in _diag_work(chains, d, rows_c, tk)])
        out_ref[0] = (jnp.where(lane < HEAD, dqa_ref[0], dqa_ref[1]) * scale).astype(BF16)
        own = pl.multiple_of(i * t, t)
        out_ref[1] = dka_ref[pl.ds(own, t), :].astype(BF16)
        out_ref[2] = dva_ref[pl.ds(own, t), :].astype(BF16)
        riders_end()

    def rows(sec):
        return pl.BlockSpec((1, t, LANES), lambda hp, st: (sec, nq - 1 - st, hp))

    def whole(sec):
        return pl.BlockSpec((1, s, LANES), lambda hp, st: (sec, 0, hp))

    r_ops, r_shapes, r_scratch, r_aliases = _riders_plumb(riders, 10, 2)
    outs = _pcall(
        body, name=name, grid=(nhp, nq),
        in_specs=[rows(4), whole(5), whole(6), rows(7),
                  pl.BlockSpec((t, LANES), lambda hp, st: (nq - 1 - st, hp + nhp)),
                  pl.BlockSpec((t, LANES), lambda hp, st: (nq - 1 - st, hp)),
                  pl.BlockSpec((2, t, 1), lambda hp, st: (hp, nq - 1 - st, 0)),
                  pl.BlockSpec((1, 8, LANES), lambda hp, st: (hp * nq + nq - 1 - st, 0, 0)),
                  pl.BlockSpec((1, LANES), lambda hp, st: (0, hp + nhp)), ANY] + [ANY] * len(r_ops),
        out_specs=[pl.BlockSpec((4, t, LANES), lambda hp, st: (1, nq - 1 - st, hp)),
                   pl.BlockSpec((8, LANES), lambda hp, st: (0, hp))] + [ANY] * len(r_shapes),
        out_shape=[jax.ShapeDtypeStruct(buf.shape, buf.dtype), jax.ShapeDtypeStruct((8, sw), F32)] + r_shapes,
        input_output_aliases={9: 0, **r_aliases},
        scratch_shapes=[pltpu.VMEM((s, LANES), F32), pltpu.VMEM((s, LANES), F32), pltpu.VMEM((2, t, LANES), F32)]
        + r_scratch,
        compiler_params=_cp(2))(proj, proj, proj, proj, dy, ya, tl, walked, bg, buf, *r_ops)
    return outs[0], outs[1], list(outs[2:])


def _grad_w_in(h, dproj, name):
    s, d = h.shape
    ns, _, sw = dproj.shape

    def body(h_ref, b_ref, o_ref, ht_ref):
        @pl.when(pl.program_id(0) == 0)
        def _():
            ht_ref[...] = h_ref[...].T

        o_ref[...] = jnp.dot(ht_ref[...], b_ref[0], preferred_element_type=F32)

    return _pcall(body, name=name, grid=(ns,),
                  in_specs=[pl.BlockSpec((s, d), lambda j: (0, 0)), pl.BlockSpec((1, s, sw), lambda j: (j, 0, 0))],
                  out_specs=pl.BlockSpec((d, sw), lambda j: (0, j)),
                  out_shape=jax.ShapeDtypeStruct((d, ns * sw), F32),
                  scratch_shapes=[pltpu.VMEM((d, s), BF16)], compiler_params=_cp(1))(h, dproj)


def _inproj_bwd(dproj, w, layer, x, g, dx1, name, riders=()):
    ns, s, sw = dproj.shape
    d = x.shape[1]
    tm = _tile(s, 512)

    def body(*refs):
        own, riders_end = _riders_run(riders, refs, 5, 2, 0, pl.program_id(0), s // tm)
        dp_ref, w_ref, x_ref, g_ref, dx1_ref, dx_ref, dg_ref = own

        @pl.when(pl.program_id(0) == 0)
        def _():
            dg_ref[...] = jnp.zeros_like(dg_ref)

        dh = _dot_nt(dp_ref[0], w_ref[0, :, 0:sw])
        for k in range(1, ns):
            dh = dh + _dot_nt(dp_ref[k], w_ref[0, :, k * sw:(k + 1) * sw])
        dxr, dgr = _rms_bwd_rows(dh, x_ref[...], g_ref[...])
        dx_ref[...] = dx1_ref[...] + dxr
        dg_ref[...] += _colsum8(dgr)
        riders_end()

    row = lambda m: (m, 0)
    fix = lambda m: (0, 0)
    r_ops, r_shapes, r_scratch, r_aliases = _riders_plumb(riders, 5, 2)
    outs = _pcall(body, name=name, grid=(s // tm,),
                  in_specs=[pl.BlockSpec((ns, tm, sw), lambda m: (0, m, 0)),
                            pl.BlockSpec((1, d, ns * sw), lambda m: (layer, 0, 0)),
                            pl.BlockSpec((tm, d), row), pl.BlockSpec((1, d), fix), pl.BlockSpec((tm, d), row)]
                  + [ANY] * len(r_ops),
                  out_specs=[pl.BlockSpec((tm, d), row), pl.BlockSpec((8, d), fix)] + [ANY] * len(r_shapes),
                  out_shape=[jax.ShapeDtypeStruct((s, d), F32), jax.ShapeDtypeStruct((8, d), F32)] + r_shapes,
                  input_output_aliases=r_aliases, scratch_shapes=r_scratch,
                  compiler_params=_cp(1))(dproj, w, x, g, dx1, *r_ops)
    return outs[0], outs[1], list(outs[2:])


def _adamw(w, g, m, v, name):
    r, c = w.shape
    tr = _tile(r, 256)
    c1 = 1.0 - ADAM_B1 ** ADAM_STEP
    c2 = 1.0 - ADAM_B2 ** ADAM_STEP

    def body(w_ref, g_ref, m_ref, v_ref, go_ref, d_ref, mo_ref, vo_ref):
        gv = g_ref[...]
        go_ref[...] = gv
        mn = ADAM_B1 * m_ref[...] + (1.0 - ADAM_B1) * gv
        vn = ADAM_B2 * v_ref[...] + (1.0 - ADAM_B2) * (gv * gv)
        d_ref[...] = -ADAM_LR * ((mn / c1) / (jnp.sqrt(vn / c2) + ADAM_EPS) + ADAM_WD * w_ref[...])
        mo_ref[...] = mn
        vo_ref[...] = vn

    t = pl.BlockSpec((tr, c), lambda i: (i, 0))
    return _pcall(body, name=name, grid=(r // tr,), in_specs=[t] * 4, out_specs=[t] * 4,
                  out_shape=[jax.ShapeDtypeStruct((r, c), F32)] * 4, compiler_params=_cp(1))(w, g, m, v)


def _add_half(grad, other, core, a, name):
    hr, hc = other.shape
    tr = _tile(hr, 256)
    nb = hr // tr

    def body(c_ref, g_ref, o_ref, out_ref, outb_ref):
        v = g_ref[...] + o_ref[...]
        out_ref[...] = v
        outb_ref[...] = v.astype(BF16)

    t = pl.BlockSpec((tr, hc), lambda i, c: (i, 0))
    own = (lambda i, c: (c[0] * nb + i, 0)) if HALF_AXES[a] == 0 else (lambda i, c: (i, c[0]))
    grid_spec = pltpu.PrefetchScalarGridSpec(
        num_scalar_prefetch=1, grid=(nb,), in_specs=[pl.BlockSpec((tr, hc), own), t], out_specs=[t, t])
    return _pcall(body, name=name, grid_spec=grid_spec,
                  out_shape=[jax.ShapeDtypeStruct((hr, hc), F32), jax.ShapeDtypeStruct((hr, hc), BF16)],
                  compiler_params=_cp(1))(core.reshape(1).astype(jnp.int32), grad, other)


def _sum_half(wide, parts, chip, core, layer, a, stack, name):
    _, sr, sc = parts.shape
    tr = _tile(sr, 256)
    nbs = sr // tr

    def body(k_ref, f_ref, p_ref, *rest):
        rest[-1][0] = ((f_ref[...] + p_ref[0].astype(F32)) + p_ref[1].astype(F32)) + p_ref[2].astype(F32)

    f_map = (lambda i, k: (i, k[0])) if SHARD_AXES[a] == 1 else (lambda i, k: (k[0] * nbs + i, 0))
    if HALF_AXES[a] == 0:
        shape, o_map = (DEPTH, 2 * sr, sc), (lambda i, k: (layer, k[1] * nbs + i, 0))
    else:
        shape, o_map = (DEPTH, sr, 2 * sc), (lambda i, k: (layer, i, k[1]))
    in_specs = [pl.BlockSpec((tr, sc), f_map), pl.BlockSpec((3, tr, sc), lambda i, k: (0, i, 0))]
    args = [wide, parts]
    aliases = {}
    if stack is not None:
        in_specs.append(ANY)
        args.append(stack)
        aliases = {3: 0}
    grid_spec = pltpu.PrefetchScalarGridSpec(
        num_scalar_prefetch=1, grid=(nbs,), in_specs=in_specs, out_specs=pl.BlockSpec((1, tr, sc), o_map))
    return _pcall(body, name=name, grid_spec=grid_spec, out_shape=jax.ShapeDtypeStruct(shape, F32),
                  input_output_aliases=aliases,
                  compiler_params=_cp(1))(jnp.stack([chip, core]).astype(jnp.int32), *args)


def _sum_slots(slots, name):
    n = slots.shape[0]

    def body(s_ref, o_ref):
        acc = s_ref[0]
        for i in range(1, n):
            acc = acc + s_ref[i]
        o_ref[...] = acc

    return _pcall(body, name=name, out_shape=jax.ShapeDtypeStruct(slots.shape[1:], F32))(slots)


def _place():
    return lax.axis_index("x"), lax.axis_index("y"), lax.axis_index("c")


def _shard_view(ref, axis, chip, size):
    if axis == 0:
        return ref.at[pl.ds(chip * size, size), :]
    return ref.at[:, pl.ds(chip * size, size)]


SHARD_AXES = (1, 0, 0, 1)
HALF_AXES = tuple(1 - ax for ax in SHARD_AXES)


class _Rider:
    def __init__(self, operands, out_shape, sems, phases, aliased=False):
        self.operands, self.out_shape, self.sems = list(operands), list(out_shape), list(sems)
        self.phases, self.aliased = phases, aliased


def _riders_plumb(riders, n_in, n_out):
    ops, out_shape, scratch, aliases = [], [], [], {}
    for r in riders:
        if r.aliased:
            for k in range(len(r.operands)):
                aliases[n_in + len(ops) + k] = n_out + len(out_shape) + k
        ops += r.operands
        out_shape += r.out_shape
        scratch += r.sems
    return ops, out_shape, scratch, aliases


def _riders_run(riders, refs, n_in, n_out, n_scr, step, nsteps):
    n_rin = sum(len(r.operands) for r in riders)
    n_rout = sum(len(r.out_shape) for r in riders)
    rin = refs[n_in:n_in + n_rin]
    o0 = n_in + n_rin
    rout = refs[o0 + n_out:o0 + n_out + n_rout]
    s0 = o0 + n_out + n_rout
    rsem = refs[s0 + n_scr:]
    own = list(refs[:n_in]) + list(refs[o0:o0 + n_out]) + list(refs[s0:s0 + n_scr])
    lasts = []
    for r in riders:
        ph = r.phases(rin[:len(r.operands)], rout[:len(r.out_shape)], rsem[:len(r.sems)])
        rin, rout, rsem = rin[len(r.operands):], rout[len(r.out_shape):], rsem[len(r.sems):]
        pl.when(step == 0)(ph[0])
        for mid in ph[1:-1]:
            pl.when(step == (3 * nsteps) // 4)(mid)
        lasts.append(ph[-1])

    def finish():
        for last in lasts:
            pl.when(step == nsteps - 1)(last)

    return own, finish


def _gather_phases(ins, outs, ssem, rsem, layer, which):
    n = len(ins)
    x, y, c = _place()
    me = 2 * x + y
    chips = [(1 - x, y), (x, 1 - y), (1 - x, 1 - y)]

    def piece(a, chip, half, of):
        ax = SHARD_AXES[which[a]]
        block = _shard_view(of[a].at[layer], ax, chip, of[a].shape[1 + ax] // 4)
        r = block.shape[0] // 2
        return block.at[pl.ds(half * r, r), :]

    def over_ici(a, j):
        cx, cy = chips[j]
        return pltpu.make_async_remote_copy(
            src_ref=piece(a, me, c, ins), dst_ref=piece(a, me, c, outs), send_sem=ssem.at[a, j],
            recv_sem=rsem.at[a, j], device_id=(cx, cy, c), device_id_type=MESH)

    def landed(a, j, half):
        cx, cy = chips[j]
        return piece(a, 2 * cx + cy, half, outs)

    def to_sibling(a, j):
        got = landed(a, j, c)
        return pltpu.make_async_remote_copy(
            src_ref=got, dst_ref=got, send_sem=ssem.at[a, 3 + j], recv_sem=rsem.at[a, 3 + j],
            device_id=(x, y, 1 - c), device_id_type=MESH)

    def wait_arrival(a, k, place):
        pltpu.make_async_remote_copy(src_ref=place, dst_ref=place, send_sem=ssem.at[a, k], recv_sem=rsem.at[a, k],
                                     device_id=(x, y, c), device_id_type=MESH).wait_recv()

    def start():
        for a in range(n):
            for j in range(3):
                over_ici(a, j).start()

    def pass_on():
        for a in range(n):
            for j in range(3):
                wait_arrival(a, j, landed(a, j, c))
                to_sibling(a, j).start()

    def finish():
        for a in range(n):
            for j in range(3):
                wait_arrival(a, 3 + j, landed(a, j, 1 - c))
        for a in range(n):
            for j in range(3):
                over_ici(a, j).wait_send()
                to_sibling(a, j).wait_send()

    return start, pass_on, finish


def _gather_rider(fulls, layer, which):
    n = len(fulls)
    return _Rider(fulls, [jax.ShapeDtypeStruct(f.shape, f.dtype) for f in fulls],
                  [pltpu.SemaphoreType.DMA((n, 6)), pltpu.SemaphoreType.DMA((n, 6))],
                  lambda ins, outs, sems: _gather_phases(ins, outs, sems[0], sems[1], layer, which), aliased=True)


def _ride_alone(rider, name):
    n = len(rider.operands)

    def body(*refs):
        for phase in rider.phases(refs[:n], refs[n:n + len(rider.out_shape)], refs[n + len(rider.out_shape):]):
            phase()

    return _pcall(body, name=name, in_specs=[ANY] * n, out_specs=[ANY] * len(rider.out_shape),
                  out_shape=rider.out_shape, scratch_shapes=rider.sems,
                  input_output_aliases={a: a for a in range(n)} if rider.aliased else {})(*rider.operands)


def _half_view(ref, a, half):
    n = ref.shape[HALF_AXES[a]] // 2
    if HALF_AXES[a] == 0:
        return ref.at[pl.ds(half * n, n), :]
    return ref.at[:, pl.ds(half * n, n)]


def _swap_rider(grads, which):
    n = len(grads)
    halves = []
    for g, w in zip(grads, which):
        sh = list(g.shape)
        sh[HALF_AXES[w]] //= 2
        halves.append(jax.ShapeDtypeStruct(tuple(sh), g.dtype))

    def phases(srcs, outs, sems):
        x, y, c = _place()

        def copy(a):
            return pltpu.make_async_remote_copy(
                src_ref=_half_view(srcs[a], which[a], 1 - c), dst_ref=outs[a], send_sem=sems[0].at[a],
                recv_sem=sems[1].at[a], device_id=(x, y, 1 - c), device_id_type=MESH)

        def start():
            for a in range(n):
                copy(a).start()

        def finish():
            for a in range(n):
                copy(a).wait()

        return start, finish

    return _Rider(grads, halves, [pltpu.SemaphoreType.DMA((n,)), pltpu.SemaphoreType.DMA((n,))], phases)


def _scatter_rider(sums, which):
    n = len(sums)
    shapes = []
    for f, w in zip(sums, which):
        sh = list(f.shape)
        sh[SHARD_AXES[w]] //= 4
        shapes.append(jax.ShapeDtypeStruct((3,) + tuple(sh), f.dtype))

    def phases(srcs, outs, sems):
        x, y, c = _place()
        chips = [(1 - x, y), (x, 1 - y), (1 - x, 1 - y)]

        def copy(a, j):
            cx, cy = chips[j]
            ax = SHARD_AXES[which[a]]
            src = _shard_view(srcs[a], ax, 2 * cx + cy, srcs[a].shape[ax] // 4)
            return pltpu.make_async_remote_copy(src_ref=src, dst_ref=outs[a].at[j], send_sem=sems[0].at[a, j],
                                                recv_sem=sems[1].at[a, j], device_id=(cx, cy, c), device_id_type=MESH)

        def start():
            for a in range(n):
                for j in range(3):
                    copy(a, j).start()

        def finish():
            for a in range(n):
                for j in range(3):
                    copy(a, j).wait()

        return start, finish

    return _Rider(sums, shapes, [pltpu.SemaphoreType.DMA((n, 3)), pltpu.SemaphoreType.DMA((n, 3))], phases)


def _pair_halves(stacks):
    n = len(stacks)

    def body(*refs):
        ins, outs = refs[:n], refs[n:2 * n]
        ssem, rsem = refs[2 * n:]
        x, y, c = _place()
        cps = [pltpu.make_async_remote_copy(
            src_ref=_half_view(ins[a].at[l], a, c), dst_ref=_half_view(outs[a].at[l], a, c), send_sem=ssem.at[a, l],
            recv_sem=rsem.at[a, l], device_id=(x, y, 1 - c), device_id_type=MESH)
            for a in range(n) for l in range(DEPTH)]
        for cp in cps:
            cp.start()
        for a in range(n):
            for l in range(DEPTH):
                got = _half_view(outs[a].at[l], a, 1 - c)
                pltpu.make_async_remote_copy(src_ref=got, dst_ref=got, send_sem=ssem.at[a, l], recv_sem=rsem.at[a, l],
                                             device_id=(x, y, 1 - c), device_id_type=MESH).wait_recv()
        for cp in cps:
            cp.wait_send()

    return _pcall(body, name="pair_halves", in_specs=[ANY] * n, out_specs=[ANY] * n,
                  out_shape=[jax.ShapeDtypeStruct(st.shape, st.dtype) for st in stacks],
                  input_output_aliases={a: a for a in range(n)},
                  scratch_shapes=[pltpu.SemaphoreType.DMA((n, DEPTH)), pltpu.SemaphoreType.DMA((n, DEPTH))])(*stacks)


class _GradReduce:
    def __init__(self, chip, core):
        self.chip, self.core = chip, core
        self.stacks = [None] * len(SHARD_AXES)

    def add(self, layer, grads, which, got):
        return [(layer, w) + tuple(_add_half(g, o, self.core, w, f"add_half_{layer}_{w}"))
                for g, o, w in zip(grads, got, which)]

    def finish(self, sums, partials):
        for (layer, w, wide, _), pr in zip(sums, partials):
            self.stacks[w] = _sum_half(wide, pr, self.chip, self.core, layer, w, self.stacks[w], f"sum_half_{layer}_{w}")

    def result(self):
        return _pair_halves(self.stacks)


def _exchange_small(pack, name, riders=()):
    nd = 8

    def body(*refs):
        own, riders_end = _riders_run(riders, refs, 1, 1, 2, jnp.int32(0), 1)
        p_ref, o_ref, ssem, rsem = own
        x, y, c = _place()
        me = 4 * x + 2 * y + c
        o_ref[me] = p_ref[...]
        cps = []
        for j in range(1, nd):
            px, py, pc = x ^ (j >> 2), y ^ ((j >> 1) & 1), c ^ (j & 1)
            cps.append(pltpu.make_async_remote_copy(
                src_ref=p_ref, dst_ref=o_ref.at[me], send_sem=ssem.at[j - 1], recv_sem=rsem.at[j - 1],
                device_id=(px, py, pc), device_id_type=MESH))
        for cp in cps:
            cp.start()
        for j in range(1, nd):
            peer = me ^ j
            got = o_ref.at[peer]
            pltpu.make_async_remote_copy(src_ref=got, dst_ref=got, send_sem=ssem.at[j - 1], recv_sem=rsem.at[j - 1],
                                         device_id=(x, y, c), device_id_type=MESH).wait_recv()
        for cp in cps:
            cp.wait_send()
        riders_end()

    vm = pl.BlockSpec(memory_space=pltpu.VMEM)
    r_ops, r_shapes, r_scratch, r_aliases = _riders_plumb(riders, 1, 1)
    outs = _pcall(body, name=name, in_specs=[vm] + [ANY] * len(r_ops), out_specs=[vm] + [ANY] * len(r_shapes),
                  out_shape=[jax.ShapeDtypeStruct((nd,) + pack.shape, pack.dtype)] + r_shapes,
                  input_output_aliases=r_aliases,
                  scratch_shapes=[pltpu.SemaphoreType.DMA((nd - 1,)), pltpu.SemaphoreType.DMA((nd - 1,))]
                  + r_scratch)(pack, *r_ops)
    return (outs[0], list(outs[1:])) if riders else outs[0]


def _row(v):
    return v.reshape(1, -1)


def _local_step(x, p, tgt, norm_g, conv_w, conv_b, branch_g, ple_norm_g, b_pg, final_g, w_in, w_out, w_pg, w_pe,
                gather=False, reduce=None):
    saved = []
    xl = x
    for l in range(DEPTH):
        riders = [_gather_rider([w_out, w_pg, w_pe], 0, [1, 2, 3])] if gather and l == 0 else []
        h, proj, got = _inproj(xl, _row(norm_g[l]), w_in, l, f"inproj_{l}", riders)
        if riders:
            w_out, w_pg, w_pe = got
        later = gather and l + 1 < DEPTH
        riders = [_gather_rider([w_in, w_pg, w_pe], l + 1, [0, 2, 3])] if later else []
        ya, tl, walked, got = _attn_fwd(proj, f"attn_fwd_{l}", riders)
        if riders:
            w_in, w_pg, w_pe = got
        riders = [_gather_rider([w_out], l + 1, [1])] if later else []
        y, got = _mix_fwd(proj, ya, conv_w[l], _row(conv_b[l]), _row(branch_g[l]), f"mix_fwd_{l}", riders)
        if riders:
            w_out, = got
        x1, hn = _outproj(y, w_out, l, xl, _row(ple_norm_g[l]), f"outproj_{l}")
        x2, gate, e = _ple_fwd(hn, w_pg, _row(b_pg[l]), p, w_pe, l, x1, f"ple_fwd_{l}")
        saved.append((xl, h, proj, ya, tl, walked, y, x1, hn, gate, e))
        xl = x2

    sq, dx, d_final = _loss_head(xl, tgt, _row(final_g), "loss_head")

    big = [None] * DEPTH
    carried = []
    small = {k: [None] * DEPTH for k in ("norm_g", "conv_w", "conv_b", "branch_g", "ple_norm_g", "b_pg")}
    for l in reversed(range(DEPTH)):
        xl, h, proj, ya, tl, walked, y, x1, hn, gate, e = saved[l]
        du, de, dx1, dy, db_pg, d_ple, _ = _ple_bwd(dx, gate, e, x1, w_pg, _row(ple_norm_g[l]), w_out, l,
                                                    f"ple_bwd_{l}")
        sums = carried
        g_pg = _mm_tn(hn, du, f"grad_w_pg_{l}")
        g_pe = _mm_tn(p, de, f"grad_w_pe_{l}", a_layer=l)
        g_out = _mm_tn(y, dx1, f"grad_w_out_{l}")
        others = [g_out, g_pg, g_pe]
        riders = [_swap_rider(others, [1, 2, 3])] if reduce is not None else []
        dpc, d_cw, d_cb, d_bg_c, got = _convmix_bwd(dy, proj, conv_w[l], _row(conv_b[l]), _row(branch_g[l]),
                                                    f"convmix_bwd_{l}", riders)
        if reduce is not None:
            sums += reduce.add(l, others, [1, 2, 3], got)
        riders = [_scatter_rider([sm[3] for sm in sums], [sm[1] for sm in sums])] if sums else []
        dproj, d_bg_a, got = _attn_bwd(proj, dy, ya, tl, walked, _row(branch_g[l]), dpc, f"attn_bwd_{l}", riders)
        if sums:
            reduce.finish(sums, got)
        g_in = _grad_w_in(h, dproj, f"grad_w_in_{l}")
        riders = [_swap_rider([g_in], [0])] if reduce is not None and l > 0 else []
        dx, d_norm, got = _inproj_bwd(dproj, w_in, l, xl, _row(norm_g[l]), dx1, f"inproj_bwd_{l}", riders)
        big[l] = (g_in, g_out, g_pg, g_pe)
        carried = reduce.add(l, [g_in], [0], got) if riders else []
        small["norm_g"][l] = jnp.sum(d_norm, axis=0)
        small["conv_w"][l] = jnp.sum(d_cw, axis=1)
        small["conv_b"][l] = jnp.sum(d_cb, axis=0)
        small["branch_g"][l] = jnp.concatenate([jnp.sum(d_bg_c, axis=0), jnp.sum(d_bg_a, axis=0)])
        small["ple_norm_g"][l] = jnp.sum(d_ple, axis=0)
        small["b_pg"][l] = jnp.sum(db_pg, axis=0)
    small = {k: jnp.stack(v) for k, v in small.items()}
    small["final_g"] = jnp.sum(d_final, axis=0)
    return sq[0, 0], dx, big, small


SMALL_ORDER = ("norm_g", "conv_w", "conv_b", "branch_g", "ple_norm_g", "b_pg", "final_g")


def _pack(parts, width):
    flat = jnp.concatenate([v.reshape(-1) for v in parts])
    rows = -(-flat.shape[0] // width)
    rows = -(-rows // 8) * 8
    return jnp.pad(flat, (0, rows * width - flat.shape[0])).reshape(rows, width)


def _unpack(packed, like):
    flat = packed.reshape(-1)
    out, off = [], 0
    for v in like:
        out.append(flat[off:off + v.size].reshape(v.shape))
        off += v.size
    return out


def kernel(x, p, norm_g, w_in, conv_w, conv_b, branch_g, w_out, ple_norm_g, w_pg, b_pg, w_pe, final_g, loss_target, m_norm_g, m_w_in, m_conv_w, m_conv_b, m_branch_g, m_w_out, m_ple_norm_g, m_w_pg, m_b_pg, m_w_pe, m_final_g, v_norm_g, v_w_in, v_conv_w, v_conv_b, v_branch_g, v_w_out, v_ple_norm_g, v_w_pg, v_b_pg, v_w_pe, v_final_g):
    ix, iy, ic = _place()
    chip = 2 * ix + iy
    d = x.shape[-1]

    big_w = (w_in, w_out, w_pg, w_pe)
    own = [_cast_into_full(w, chip, ax, f"cast_{i}") for i, (w, ax) in enumerate(zip(big_w, SHARD_AXES))]
    full_in, = _ride_alone(_gather_rider([own[0]], 0, [0]), "gather_w_in_0")
    full_out, full_pg, full_pe = own[1:]
    cw_shard = conv_w.shape[-1]
    cw_slots = _exchange_small(_pack([conv_w], LANES), "exchange_conv_w")
    conv_full = jnp.concatenate([_unpack(cw_slots[2 * k], [conv_w])[0] for k in range(4)], axis=-1)

    reduce = _GradReduce(chip, ic)
    sq, dx, big_g, small_g = _local_step(
        x[0], p[:, 0], loss_target[0], norm_g, conv_full, conv_b, branch_g, ple_norm_g, b_pg, final_g,
        full_in, full_out, full_pg, full_pe, gather=True, reduce=reduce)

    parts = [small_g[k] for k in SMALL_ORDER] + [sq.reshape(1)]
    last = [big_g[0][0]]
    slots, got = _exchange_small(_pack(parts, d), "exchange_small_grads", [_swap_rider(last, [0])])
    sums = reduce.add(0, last, [0], got)
    reduce.finish(sums, _ride_alone(_scatter_rider([sm[3] for sm in sums], [0]), "scatter_shards_last"))
    g_big = reduce.result()
    total = _unpack(_sum_slots(slots, "sum_small"), parts)
    g_small = dict(zip(SMALL_ORDER, total[:-1]))
    loss = 0.5 * total[-1][0] / d
    g_small["conv_w"] = lax.dynamic_slice_in_dim(g_small["conv_w"], chip * cw_shard, cw_shard, axis=2)

    grads = dict(g_small)
    grads.update(w_in=g_big[0], w_out=g_big[1], w_pg=g_big[2], w_pe=g_big[3])
    weights = dict(norm_g=norm_g, w_in=w_in, conv_w=conv_w, conv_b=conv_b, branch_g=branch_g, w_out=w_out,
                   ple_norm_g=ple_norm_g, w_pg=w_pg, b_pg=b_pg, w_pe=w_pe, final_g=final_g)
    ms = dict(norm_g=m_norm_g, w_in=m_w_in, conv_w=m_conv_w, conv_b=m_conv_b, branch_g=m_branch_g, w_out=m_w_out,
              ple_norm_g=m_ple_norm_g, w_pg=m_w_pg, b_pg=m_b_pg, w_pe=m_w_pe, final_g=m_final_g)
    vs = dict(norm_g=v_norm_g, w_in=v_w_in, conv_w=v_conv_w, conv_b=v_conv_b, branch_g=v_branch_g, w_out=v_w_out,
              ple_norm_g=v_ple_norm_g, w_pg=v_w_pg, b_pg=v_b_pg, w_pe=v_w_pe, final_g=v_final_g)
    names = ("norm_g", "w_in", "conv_w", "conv_b", "branch_g", "w_out", "ple_norm_g", "w_pg", "b_pg", "w_pe", "final_g")
    delta, new_m, new_v = {}, {}, {}
    for k in ("w_in", "w_out", "w_pg", "w_pe"):
        shp = weights[k].shape
        two = lambda a: a.reshape(-1, shp[-1])
        gr, dl, mn, vn = _adamw(two(weights[k]), two(grads[k]), two(ms[k]), two(vs[k]), f"adamw_{k}")
        delta[k], new_m[k], new_v[k] = dl.reshape(shp), mn.reshape(shp), vn.reshape(shp)
        grads[k] = gr.reshape(shp)
    like = [weights[k] for k in SMALL_ORDER]
    packs = [_pack([src[k] for k in SMALL_ORDER], d) for src in (weights, grads, ms, vs)]
    outs = _adamw(*packs, "adamw_small")
    for res, o in zip((delta, new_m, new_v), outs[1:]):
        res.update(dict(zip(SMALL_ORDER, _unpack(o, like))))

    return (loss, dx[None], *[grads[k] for k in names], *[delta[k] for k in names],
            *[new_m[k] for k in names], *[new_v[k] for k in names])
```

```python
import math

import jax
import jax.numpy as jnp
from jax import lax
from jax.experimental import pallas as pl
from jax.experimental.pallas import tpu as pltpu

F32 = jnp.float32
BF16 = jnp.bfloat16
EPS = 1e-6
HEAD = 64
LANES = 128
ATT_TK = 256
ATT_TQ = 512
ATT_ROWS = 128
ALIVE_LOG = -105.0
DEPTH = 2
VMEM_LIMIT = 56 * 1024 * 1024
MESH = pl.DeviceIdType.MESH
ANY = pl.BlockSpec(memory_space=pl.ANY)

ADAM_LR = 0.001
ADAM_B1 = 0.9
ADAM_B2 = 0.999
ADAM_EPS = 1e-08
ADAM_WD = 0.01
ADAM_STEP = 10


def _pcall(body, **kw):
    return pl.pallas_call(body, **kw)


def _cp(n_axes):
    return pltpu.CompilerParams(dimension_semantics=("arbitrary",) * n_axes, vmem_limit_bytes=VMEM_LIMIT)


def _tile(n, pref):
    return pref if n % pref == 0 else n


def _split_dot(a, b, passes):
    out = None
    rem = a
    for _ in range(passes):
        hi = rem.astype(BF16)
        t = jnp.dot(hi, b, preferred_element_type=F32)
        out = t if out is None else out + t
        rem = rem - hi.astype(F32)
    return out


def _group_mat():
    r = lax.broadcasted_iota(jnp.int32, (LANES, LANES), 0) // HEAD
    c = lax.broadcasted_iota(jnp.int32, (LANES, LANES), 1) // HEAD
    return jnp.where(r == c, 1.0 / HEAD, 0.0).astype(BF16)


def _group_mean(v, gm):
    return _split_dot(v, gm, 2)


def _sigmoid(z):
    return 1.0 / (1.0 + jnp.exp(-z))


def _dot_nt(a, b):
    return lax.dot_general(a, b, (((1,), (1,)), ((), ())), preferred_element_type=F32)


def _dot_tn(a, b):
    return lax.dot_general(a, b, (((0,), (0,)), ((), ())), preferred_element_type=F32)


def _cast_into_full(w, chip, axis, name):
    _, r, c = w.shape
    tr = _tile(r, 256)
    nb = r // tr
    full = (DEPTH, 4 * r, c) if axis == 0 else (DEPTH, r, 4 * c)

    def body(k_ref, w_ref, o_ref):
        o_ref[...] = w_ref[...].astype(BF16)

    out_map = (lambda l, i, k: (l, k[0] * nb + i, 0)) if axis == 0 else (lambda l, i, k: (l, i, k[0]))
    grid_spec = pltpu.PrefetchScalarGridSpec(
        num_scalar_prefetch=1, grid=(DEPTH, nb),
        in_specs=[pl.BlockSpec((1, tr, c), lambda l, i, k: (l, i, 0))],
        out_specs=pl.BlockSpec((1, tr, c), out_map))
    return _pcall(body, name=name, grid_spec=grid_spec, out_shape=jax.ShapeDtypeStruct(full, BF16),
                  compiler_params=_cp(2))(chip.reshape(1).astype(jnp.int32), w)


def _rms_bwd_rows(dh, xv, g):
    r = lax.rsqrt(jnp.mean(xv * xv, axis=-1, keepdims=True) + EPS)
    xn = xv * r
    dxn = dh * g
    dx = r * (dxn - xn * jnp.mean(dxn * xn, axis=-1, keepdims=True))
    return dx, dh * xn


def _colsum8(v):
    tm, d = v.shape
    return jnp.sum(v.reshape(tm // 8, 8, d), axis=0)


def _inproj(x, g, w, layer, name, riders=()):
    s, d = x.shape
    n = w.shape[2]
    sw = d // 2
    ns = n // sw
    tm = _tile(s, 512)

    def body(*refs):
        own, riders_end = _riders_run(riders, refs, 3, 2, 0, pl.program_id(0), s // tm)
        x_ref, g_ref, w_ref, h_ref, o_ref = own
        xv = x_ref[...]
        r = lax.rsqrt(jnp.mean(xv * xv, axis=-1, keepdims=True) + EPS)
        h = (xv * r * g_ref[...]).astype(BF16)
        h_ref[...] = h
        for k in range(ns):
            o_ref[k] = jnp.dot(h, w_ref[0, :, k * sw:(k + 1) * sw], preferred_element_type=F32).astype(BF16)
        riders_end()

    r_ops, r_shapes, r_scratch, r_aliases = _riders_plumb(riders, 3, 2)
    outs = _pcall(body, name=name, grid=(s // tm,),
                  in_specs=[pl.BlockSpec((tm, d), lambda m: (m, 0)), pl.BlockSpec((1, d), lambda m: (0, 0)),
                            pl.BlockSpec((1, d, n), lambda m: (layer, 0, 0))] + [ANY] * len(r_ops),
                  out_specs=[pl.BlockSpec((tm, d), lambda m: (m, 0)), pl.BlockSpec((ns, tm, sw), lambda m: (0, m, 0))]
                  + [ANY] * len(r_shapes),
                  out_shape=[jax.ShapeDtypeStruct((s, d), BF16), jax.ShapeDtypeStruct((ns, s, sw), BF16)] + r_shapes,
                  input_output_aliases=r_aliases, scratch_shapes=r_scratch,
                  compiler_params=_cp(1))(x, g, w, *r_ops)
    return outs[0], outs[1], list(outs[2:])


def _softplus_parts(z):
    lm = jnp.minimum(-z, 0.0) - jnp.log(1.0 + jnp.exp(-jnp.abs(z)))
    return lm, lm + z


def _attn_tiles(s):
    tk = _tile(s, ATT_TK)
    tq = _tile(s, ATT_TQ)
    return tk, tq, tq // tk, min(ATT_ROWS, tq)


def _diag_work(chains, d, rows, tk):
    work = []
    for n, (_, r0) in enumerate(chains):
        if r0 + rows - 1 <= d * tk:
            continue
        kw = tk // 2 if (tk % 2 == 0 and r0 + rows <= d * tk + tk // 2) else tk
        if r0 >= d * tk + kw:
            mask = None
        else:
            row = lax.broadcasted_iota(jnp.int32, (rows, kw), 0)
            col = lax.broadcasted_iota(jnp.int32, (rows, kw), 1)
            mask = col + d * tk < row + r0
        work.append((n, kw, mask))
    return work


def _both(mask, gate):
    if mask is None:
        return gate
    if gate is None:
        return mask
    return jnp.logical_and(mask, gate)


def _any_alive(rsums):
    m = rsums[0]
    for r in rsums[1:]:
        m = jnp.maximum(m, r)
    return jnp.max((m > ALIVE_LOG).astype(jnp.int32))


def _attn_fwd(proj, name, riders=()):
    _, s, sw = proj.shape
    nhp = sw // LANES
    tk, tq, nd, rows = _attn_tiles(s)
    nq = s // tq
    scale = 1.0 / math.sqrt(HEAD)

    def body(*refs):
        i = pl.program_id(1)
        own, riders_end = _riders_run(riders, refs, 3, 3, 1, pl.program_id(0) * nq + i, nhp * nq)
        q_ref, k_ref, v_ref, o_ref, tl_ref, nw_ref, acc_ref = own
        tri = (lax.broadcasted_iota(jnp.int32, (tk, tk), 0) >
               lax.broadcasted_iota(jnp.int32, (tk, tk), 1)).astype(BF16)
        lane = lax.broadcasted_iota(jnp.int32, (tq, LANES), 1)
        q = q_ref[0] * jnp.asarray(scale, BF16)
        qms = [jnp.where((lane // HEAD) == h, q, jnp.zeros_like(q)) for h in range(2)]
        acc_ref[...] = jnp.zeros_like(acc_ref)
        chains = [(h, r0) for h in range(2) for r0 in range(0, tq, rows)]
        qparts = [qms[h][r0:r0 + rows] for h, r0 in chains]

        def block(rsums, tiles, items):
            kjs = [k_ref[0, pl.ds(pl.multiple_of(j * tk, tk), tk), :] for j in tiles]
            vjs = [v_ref[0, pl.ds(pl.multiple_of(j * tk, tk), tk), :] for j in tiles]
            zs = [_dot_nt(qparts[n], kjs[t][:kw]) for n, t, kw, _, _ in items]
            lms, lss, css = [], [], []
            for z, (n, t, kw, mask, gate) in zip(zs, items):
                lm, ls = _softplus_parts(z)
                keep = _both(mask, gate)
                if keep is not None:
                    lm = jnp.where(keep, lm, 0.0)
                lms.append(lm)
                lss.append(ls)
                css.append(_split_dot(lm, tri[:kw, :kw], 2))
            cur = list(rsums)
            for lm, ls, cs, (n, t, kw, mask, gate) in zip(lms, lss, css, items):
                h, r0 = chains[n]
                a = jnp.exp(ls + (cur[n] + cs))
                keep = _both(mask, gate)
                if keep is not None:
                    a = jnp.where(keep, a, 0.0)
                acc_ref[h, r0:r0 + rows, :] += jnp.dot(a.astype(BF16), vjs[t][:kw], preferred_element_type=F32)
                cur[n] = cur[n] + jnp.sum(lm, axis=1, keepdims=True)
            return tuple(cur)

        everyone = [(n, 0, tk, None, None) for n in range(len(chains))]
        upper = [n for n, (_, r0) in enumerate(chains) if r0 >= tk]
        lower = [n for n, (_, r0) in enumerate(chains) if r0 < tk]
        left = jnp.maximum(i * nd - 1, 0)
        rsums = (jnp.zeros((rows, 1), F32),) * len(chains)
        for d in reversed(range(1, nd)):
            rsums = block(rsums, [i * nd + d], [(n, 0, kw, m, None) for n, kw, m in _diag_work(chains, d, rows, tk)])
        rsums = block(rsums, [i * nd, left],
                      [(n, 0, kw, m, None) for n, kw, m in _diag_work(chains, 0, rows, tk)]
                      + [(n, 1, tk, None, i > 0) for n in lower])

        if upper:
            too = (i > 0) & (_any_alive([rsums[n] for n in upper]) > 0)
            rsums = lax.cond(too, lambda rs: block(rs, [left], [(n, 0, tk, None, None) for n in upper]),
                             lambda rs: rs, rsums)
            too = too.astype(jnp.int32)
        else:
            too = jnp.int32(0)

        def walk(c):
            jj, rs, _ = c
            rs = block(rs, [i * nd - 2 - jj], everyone)
            return jj + 1, rs, _any_alive(rs)

        whole, rsums, _ = lax.while_loop(lambda c: (c[0] < i * nd - 1) & (c[2] > 0), walk,
                                         (jnp.int32(0), rsums, _any_alive(rsums)))
        for n, (h, r0) in enumerate(chains):
            tl_ref[h, r0:r0 + rows, :] = rsums[n]
        nw_ref[0] = (jnp.zeros((8, LANES), jnp.int32) + (2 * whole + too)).astype(F32)
        o_ref[...] = jnp.where(lane < HEAD, acc_ref[0], acc_ref[1]).astype(BF16)
        riders_end()

    r_ops, r_shapes, r_scratch, r_aliases = _riders_plumb(riders, 3, 3)
    outs = _pcall(
        body, name=name, grid=(nhp, nq),
        in_specs=[pl.BlockSpec((1, tq, LANES), lambda hp, i: (4, i, hp)),
                  pl.BlockSpec((1, s, LANES), lambda hp, i: (5, 0, hp)),
                  pl.BlockSpec((1, s, LANES), lambda hp, i: (6, 0, hp))] + [ANY] * len(r_ops),
        out_specs=[pl.BlockSpec((tq, LANES), lambda hp, i: (i, hp)),
                   pl.BlockSpec((2, tq, 1), lambda hp, i: (hp, i, 0)),
                   pl.BlockSpec((1, 8, LANES), lambda hp, i: (hp * nq + i, 0, 0))] + [ANY] * len(r_shapes),
        out_shape=[jax.ShapeDtypeStruct((s, sw), BF16), jax.ShapeDtypeStruct((2 * nhp, s, 1), F32),
                   jax.ShapeDtypeStruct((nhp * nq, 8, LANES), F32)] + r_shapes,
        input_output_aliases=r_aliases,
        scratch_shapes=[pltpu.VMEM((2, tq, LANES), F32)] + r_scratch,
        compiler_params=_cp(2))(proj, proj, proj, *r_ops)
    return outs[0], outs[1], outs[2], list(outs[3:])


def _conv_rows(cc_ref, ch_ref, w_ref, b_ref, r, tc):
    r0 = pl.multiple_of(r * tc, tc)
    u = cc_ref[0, pl.ds(r0, tc), :].astype(F32) * ch_ref[0, pl.ds(r0, tc), :].astype(F32)
    p0 = pl.multiple_of(jnp.maximum(r0 - 16, 0), 16)
    up = cc_ref[0, pl.ds(p0, 16), :].astype(F32) * ch_ref[0, pl.ds(p0, 16), :].astype(F32)
    up = up * (r > 0).astype(F32)
    prev1 = up[15:16, :]
    prev2 = up[14:15, :]
    rid = lax.broadcasted_iota(jnp.int32, u.shape, 0)
    s1 = jnp.where(rid == 0, prev1, pltpu.roll(u, 1, axis=0))
    s2 = jnp.where(rid == 0, prev2, jnp.where(rid == 1, prev1, pltpu.roll(u, 2, axis=0)))
    cv = b_ref[...] + s2 * w_ref[0:1, :] + s1 * w_ref[1:2, :] + u * w_ref[2:3, :]
    return r0, u, s1, s2, cv


def _mix_fwd(proj, ya, conv_w, conv_b, bg, name, riders=()):
    _, s, sw = proj.shape
    nh = sw // LANES
    tc = _tile(s, 256)

    def body(*refs):
        c = pl.program_id(0)
        own, riders_end = _riders_run(riders, refs, 9, 1, 0, c, 2 * nh)
        cb_ref, cc_ref, ch_ref, cz_ref, ya_ref, az_ref, w_ref, b_ref, g_ref, y_ref = own
        gm = _group_mat()

        def finish(r0, yv, zg):
            n = yv * lax.rsqrt(_group_mean(yv * yv, gm) + EPS)
            y_ref[pl.ds(r0, tc), :] = (n * g_ref[...] * (zg * _sigmoid(zg))).astype(BF16)

        @pl.when(c < nh)
        def _():
            def step(r, carry):
                r0, _, _, _, cv = _conv_rows(cc_ref, ch_ref, w_ref, b_ref, r, tc)
                yc = cb_ref[0, pl.ds(r0, tc), :].astype(F32) * cv
                finish(r0, yc, cz_ref[0, pl.ds(r0, tc), :].astype(F32))
                return carry
            lax.fori_loop(0, s // tc, step, 0)

        @pl.when(c >= nh)
        def _():
            def step(r, carry):
                r0 = pl.multiple_of(r * tc, tc)
                finish(r0, ya_ref[pl.ds(r0, tc), :].astype(F32), az_ref[0, pl.ds(r0, tc), :].astype(F32))
                return carry
            lax.fori_loop(0, s // tc, step, 0)

        riders_end()

    def sec(k):
        return pl.BlockSpec((1, s, LANES), lambda c: (k, 0, jnp.minimum(c, nh - 1)))

    r_ops, r_shapes, r_scratch, r_aliases = _riders_plumb(riders, 9, 1)
    outs = _pcall(
        body, name=name, grid=(2 * nh,),
        in_specs=[sec(0), sec(1), sec(2), sec(3),
                  pl.BlockSpec((s, LANES), lambda c: (0, jnp.maximum(c - nh, 0))),
                  pl.BlockSpec((1, s, LANES), lambda c: (7, 0, jnp.maximum(c - nh, 0))),
                  pl.BlockSpec((3, LANES), lambda c: (0, jnp.minimum(c, nh - 1))),
                  pl.BlockSpec((1, LANES), lambda c: (0, jnp.minimum(c, nh - 1))),
                  pl.BlockSpec((1, LANES), lambda c: (0, c))] + [ANY] * len(r_ops),
        out_specs=[pl.BlockSpec((s, LANES), lambda c: (0, c))] + [ANY] * len(r_shapes),
        out_shape=[jax.ShapeDtypeStruct((s, 2 * sw), BF16)] + r_shapes,
        input_output_aliases=r_aliases, scratch_shapes=r_scratch, compiler_params=_cp(1),
    )(proj, proj, proj, proj, ya, proj, conv_w, conv_b, bg, *r_ops)
    return outs[0], list(outs[1:])


def _outproj(y, w, layer, x, g, name):
    s, d = x.shape
    tm = _tile(s, 512)

    def body(y_ref, w_ref, x_ref, g_ref, x1_ref, hn_ref):
        x1 = x_ref[...] + jnp.dot(y_ref[...], w_ref[0], preferred_element_type=F32)
        x1_ref[...] = x1
        r = lax.rsqrt(jnp.mean(x1 * x1, axis=-1, keepdims=True) + EPS)
        hn_ref[...] = (x1 * r * g_ref[...]).astype(BF16)

    row = lambda m: (m, 0)
    fix = lambda m: (0, 0)
    return _pcall(body, name=name, grid=(s // tm,),
                  in_specs=[pl.BlockSpec((tm, d), row), pl.BlockSpec((1, d, d), lambda m: (layer, 0, 0)),
                            pl.BlockSpec((tm, d), row), pl.BlockSpec((1, d), fix)],
                  out_specs=[pl.BlockSpec((tm, d), row), pl.BlockSpec((tm, d), row)],
                  out_shape=[jax.ShapeDtypeStruct((s, d), F32), jax.ShapeDtypeStruct((s, d), BF16)],
                  compiler_params=_cp(1))(y, w, x, g)


def _ple_fwd(hn, w_pg, b_pg, p, w_pe, layer, x1, name):
    s, d = x1.shape
    pd = p.shape[2]
    tm = _tile(s, 512)

    def body(hn_ref, wg_ref, b_ref, p_ref, we_ref, x1_ref, x2_ref, gate_ref, e_ref):
        gate = _sigmoid(jnp.dot(hn_ref[...], wg_ref[0], preferred_element_type=F32) + b_ref[...])
        e = jnp.dot(p_ref[0].astype(BF16), we_ref[0], preferred_element_type=F32)
        x2_ref[...] = x1_ref[...] + gate * e
        gate_ref[...] = gate.astype(BF16)
        e_ref[...] = e.astype(BF16)

    row = lambda m: (m, 0)
    fix = lambda m: (0, 0)
    return _pcall(body, name=name, grid=(s // tm,),
                  in_specs=[pl.BlockSpec((tm, d), row), pl.BlockSpec((1, d, d), lambda m: (layer, 0, 0)),
                            pl.BlockSpec((1, d), fix), pl.BlockSpec((1, tm, pd), lambda m: (layer, m, 0)),
                            pl.BlockSpec((1, pd, d), lambda m: (layer, 0, 0)), pl.BlockSpec((tm, d), row)],
                  out_specs=[pl.BlockSpec((tm, d), row)] * 3,
                  out_shape=[jax.ShapeDtypeStruct((s, d), F32), jax.ShapeDtypeStruct((s, d), BF16),
                             jax.ShapeDtypeStruct((s, d), BF16)],
                  compiler_params=_cp(1))(hn, w_pg, b_pg, p, w_pe, x1)


def _loss_head(x, tgt, g, name):
    s, d = x.shape
    tm = _tile(s, 512)

    def body(x_ref, t_ref, g_ref, l_ref, dx_ref, dg_ref):
        m = pl.program_id(0)

        @pl.when(m == 0)
        def _():
            l_ref[...] = jnp.zeros_like(l_ref)
            dg_ref[...] = jnp.zeros_like(dg_ref)

        xv = x_ref[...]
        gv = g_ref[...]
        r = lax.rsqrt(jnp.mean(xv * xv, axis=-1, keepdims=True) + EPS)
        xn = xv * r
        err = xn * gv - t_ref[...]
        l_ref[...] += jnp.sum(err * err)
        dy = err * (1.0 / d)
        dxn = dy * gv
        dx_ref[...] = r * (dxn - xn * jnp.mean(dxn * xn, axis=-1, keepdims=True))
        dg_ref[...] += _colsum8(dy * xn)

    row = lambda m: (m, 0)
    fix = lambda m: (0, 0)
    return _pcall(body, name=name, grid=(s // tm,),
                  in_specs=[pl.BlockSpec((tm, d), row), pl.BlockSpec((tm, d), row), pl.BlockSpec((1, d), fix)],
                  out_specs=[pl.BlockSpec((8, LANES), fix), pl.BlockSpec((tm, d), row), pl.BlockSpec((8, d), fix)],
                  out_shape=[jax.ShapeDtypeStruct((8, LANES), F32), jax.ShapeDtypeStruct((s, d), F32),
                             jax.ShapeDtypeStruct((8, d), F32)],
                  compiler_params=_cp(1))(x, tgt, g)


def _ple_bwd(dx2, gate, e, x1, w_pg, g_ple, w_out, layer, name, riders=()):
    s, d = dx2.shape
    tm = _tile(s, 512)

    def body(*refs):
        m = pl.program_id(0)
        own, riders_end = _riders_run(riders, refs, 7, 6, 0, m, s // tm)
        (dx2_ref, gate_ref, e_ref, x1_ref, wg_ref, g_ref, wo_ref,
         du_ref, de_ref, dx1_ref, dy_ref, db_ref, dg_ref) = own

        @pl.when(m == 0)
        def _():
            db_ref[...] = jnp.zeros_like(db_ref)
            dg_ref[...] = jnp.zeros_like(dg_ref)

        dx2v = dx2_ref[...]
        gate = gate_ref[...].astype(F32)
        du = dx2v * e_ref[...].astype(F32) * gate * (1.0 - gate)
        de_ref[...] = (dx2v * gate).astype(BF16)
        dub = du.astype(BF16)
        du_ref[...] = dub
        db_ref[...] += _colsum8(du)
        dhn = _dot_nt(dub, wg_ref[0])
        dxr, dgr = _rms_bwd_rows(dhn, x1_ref[...], g_ref[...])
        dx1 = dx2v + dxr
        dx1_ref[...] = dx1
        dg_ref[...] += _colsum8(dgr)
        dy_ref[...] = _dot_nt(dx1.astype(BF16), wo_ref[0]).astype(BF16)
        riders_end()

    row = lambda m: (m, 0)
    fix = lambda m: (0, 0)
    t = pl.BlockSpec((tm, d), row)
    r_ops, r_shapes, r_scratch, r_aliases = _riders_plumb(riders, 7, 6)
    outs = _pcall(body, name=name, grid=(s // tm,),
                  in_specs=[t, t, t, t, pl.BlockSpec((1, d, d), lambda m: (layer, 0, 0)), pl.BlockSpec((1, d), fix),
                            pl.BlockSpec((1, d, d), lambda m: (layer, 0, 0))] + [ANY] * len(r_ops),
                  out_specs=[t, t, t, t, pl.BlockSpec((8, d), fix), pl.BlockSpec((8, d), fix)] + [ANY] * len(r_shapes),
                  out_shape=[jax.ShapeDtypeStruct((s, d), BF16), jax.ShapeDtypeStruct((s, d), BF16),
                             jax.ShapeDtypeStruct((s, d), F32), jax.ShapeDtypeStruct((s, d), BF16),
                             jax.ShapeDtypeStruct((8, d), F32), jax.ShapeDtypeStruct((8, d), F32)] + r_shapes,
                  input_output_aliases=r_aliases, scratch_shapes=r_scratch,
                  compiler_params=_cp(1))(dx2, gate, e, x1, w_pg, g_ple, w_out, *r_ops)
    return tuple(outs[:6]) + (list(outs[6:]),)


def _mm_tn(a, b, name, a_layer=None):
    s, ka = a.shape[-2:]
    n = b.shape[1]
    tn = _tile(n, 1024)
    ns = n // tn
    tk = _tile(s, 512)
    nk = s // tk

    def body(a_ref, b_ref, o_ref, acc_ref):
        k = pl.program_id(1)

        @pl.when(k == 0)
        def _():
            acc_ref[...] = jnp.zeros_like(acc_ref)

        av = a_ref[...] if a_layer is None else a_ref[0]
        acc_ref[...] += _dot_tn(av.astype(BF16), b_ref[...].astype(BF16))

        @pl.when(k == nk - 1)
        def _():
            o_ref[...] = acc_ref[...]

    a_spec = (pl.BlockSpec((tk, ka), lambda j, k: (k, 0)) if a_layer is None
              else pl.BlockSpec((1, tk, ka), lambda j, k: (a_layer, k, 0)))
    return _pcall(body, name=name, grid=(ns, nk),
                  in_specs=[a_spec, pl.BlockSpec((tk, tn), lambda j, k: (k, j))],
                  out_specs=pl.BlockSpec((ka, tn), lambda j, k: (0, j)),
                  out_shape=jax.ShapeDtypeStruct((ka, n), F32),
                  scratch_shapes=[pltpu.VMEM((ka, tn), F32)], compiler_params=_cp(2))(a, b)


def _norm_gate_bwd(dy, yv, zg, g, gm):
    r = lax.rsqrt(_group_mean(yv * yv, gm) + EPS)
    n = yv * r
    sg = _sigmoid(zg)
    sil = zg * sg
    dzg = dy * n * g * (sg * (1.0 + zg * (1.0 - sg)))
    dn = dy * g * sil
    dyv = r * (dn - n * _group_mean(dn * n, gm))
    return dyv, dzg, dy * n * sil


def _convmix_bwd(dy, proj, conv_w, conv_b, bg, name, riders=()):
    _, s, sw = proj.shape
    nh = sw // LANES
    tc = _tile(s, 256)
    nr = s // tc

    def body(*refs):
        own, riders_end = _riders_run(riders, refs, 8, 4, 1, pl.program_id(0), nh)
        (dy_ref, cb_ref, cc_ref, ch_ref, cz_ref, w_ref, b_ref, g_ref,
         dp_ref, dw_ref, db_ref, dg_ref, dcv_ref) = own
        gm = _group_mat()
        dcv_ref[pl.ds(s, 8), :] = jnp.zeros((8, LANES), F32)

        def pass1(r, carry):
            dw0, dw1, dw2, db, dg = carry
            r0, u, s1, s2, cv = _conv_rows(cc_ref, ch_ref, w_ref, b_ref, r, tc)
            cb = cb_ref[0, pl.ds(r0, tc), :].astype(F32)
            dyc, dcz, dgr = _norm_gate_bwd(dy_ref[pl.ds(r0, tc), :].astype(F32), cb * cv,
                                           cz_ref[0, pl.ds(r0, tc), :].astype(F32), g_ref[...], gm)
            dp_ref[0, pl.ds(r0, tc), :] = (dyc * cv).astype(BF16)
            dp_ref[3, pl.ds(r0, tc), :] = dcz.astype(BF16)
            dcv = dyc * cb
            dcv_ref[pl.ds(r0, tc), :] = dcv
            return (dw0 + _colsum8(dcv * s2), dw1 + _colsum8(dcv * s1), dw2 + _colsum8(dcv * u),
                    db + _colsum8(dcv), dg + _colsum8(dgr))

        z8 = jnp.zeros((8, LANES), F32)
        dw0, dw1, dw2, db, dg = lax.fori_loop(0, nr, pass1, (z8, z8, z8, z8, z8))
        dw_ref[0] = dw0
        dw_ref[1] = dw1
        dw_ref[2] = dw2
        db_ref[...] = db
        dg_ref[...] = dg

        def pass2(r, carry):
            r0 = pl.multiple_of(r * tc, tc)
            dcv = dcv_ref[pl.ds(r0, tc), :]
            nxt = dcv_ref[pl.ds(pl.multiple_of(r0 + tc, 8), 8), :]
            rid = lax.broadcasted_iota(jnp.int32, dcv.shape, 0)
            n1 = jnp.where(rid == tc - 1, nxt[0:1, :], pltpu.roll(dcv, tc - 1, axis=0))
            n2 = jnp.where(rid == tc - 1, nxt[1:2, :],
                           jnp.where(rid == tc - 2, nxt[0:1, :], pltpu.roll(dcv, tc - 2, axis=0)))
            du = dcv * w_ref[2:3, :] + n1 * w_ref[1:2, :] + n2 * w_ref[0:1, :]
            dp_ref[1, pl.ds(r0, tc), :] = (du * ch_ref[0, pl.ds(r0, tc), :].astype(F32)).astype(BF16)
            dp_ref[2, pl.ds(r0, tc), :] = (du * cc_ref[0, pl.ds(r0, tc), :].astype(F32)).astype(BF16)
            return carry

        lax.fori_loop(0, nr, pass2, 0)
        riders_end()

    def sec(k):
        return pl.BlockSpec((1, s, LANES), lambda c: (k, 0, c))

    col = lambda c: (0, c)
    r_ops, r_shapes, r_scratch, r_aliases = _riders_plumb(riders, 8, 4)
    outs = _pcall(
        body, name=name, grid=(nh,),
        in_specs=[pl.BlockSpec((s, LANES), col), sec(0), sec(1), sec(2), sec(3),
                  pl.BlockSpec((3, LANES), col), pl.BlockSpec((1, LANES), col), pl.BlockSpec((1, LANES), col)]
        + [ANY] * len(r_ops),
        out_specs=[pl.BlockSpec((4, s, LANES), lambda c: (0, 0, c)), pl.BlockSpec((3, 8, LANES), lambda c: (0, 0, c)),
                   pl.BlockSpec((8, LANES), col), pl.BlockSpec((8, LANES), col)] + [ANY] * len(r_shapes),
        out_shape=[jax.ShapeDtypeStruct((8, s, sw), BF16), jax.ShapeDtypeStruct((3, 8, sw), F32),
                   jax.ShapeDtypeStruct((8, sw), F32), jax.ShapeDtypeStruct((8, sw), F32)] + r_shapes,
        input_output_aliases=r_aliases,
        scratch_shapes=[pltpu.VMEM((s + 8, LANES), F32)] + r_scratch, compiler_params=_cp(1),
    )(dy, proj, proj, proj, proj, conv_w, conv_b, bg, *r_ops)
    return tuple(outs[:4]) + (list(outs[4:]),)


def _attn_bwd(proj, dy, ya, tl, walked, bg, buf, name, riders=()):
    _, s, sw = proj.shape
    nhp = sw // LANES
    tk, t, nd, rows_c = _attn_tiles(s)
    nq = s // t
    scale = 1.0 / math.sqrt(HEAD)

    def body(*refs):
        step = pl.program_id(1)
        i = nq - 1 - step
        own, riders_end = _riders_run(riders, refs, 10, 2, 3, pl.program_id(0) * nq + step, nhp * nq)
        (q_ref, k_ref, v_ref, az_ref, dy_ref, ya_ref, tl_ref, nw_ref, g_ref, buf_ref, out_ref, dg_ref,
         dka_ref, dva_ref, dqa_ref) = own

        @pl.when(step == 0)
        def _():
            dka_ref[...] = jnp.zeros_like(dka_ref)
            dva_ref[...] = jnp.zeros_like(dva_ref)
            dg_ref[...] = jnp.zeros_like(dg_ref)

        dyv, dzg, dgr = _norm_gate_bwd(dy_ref[...].astype(F32), ya_ref[...].astype(F32), az_ref[0].astype(F32),
                                       g_ref[...], _group_mat())
        out_ref[3] = dzg.astype(BF16)
        dg_ref[...] += _colsum8(dgr)

        tri = (lax.broadcasted_iota(jnp.int32, (tk, tk), 0) <=
               lax.broadcasted_iota(jnp.int32, (tk, tk), 1)).astype(BF16)
        lane = lax.broadcasted_iota(jnp.int32, (t, LANES), 1)
        q = q_ref[0] * jnp.asarray(scale, BF16)
        do = dyv.astype(BF16)
        qms = [jnp.where((lane // HEAD) == h, q, jnp.zeros_like(q)) for h in range(2)]
        doms = [jnp.where((lane // HEAD) == h, do, jnp.zeros_like(do)) for h in range(2)]
        dqa_ref[...] = jnp.zeros_like(dqa_ref)
        chains = [(h, r0) for h in range(2) for r0 in range(0, t, rows_c)]
        qparts = [qms[h][r0:r0 + rows_c] for h, r0 in chains]
        doparts = [doms[h][r0:r0 + rows_c] for h, r0 in chains]
        tots = [tl_ref[h, r0:r0 + rows_c, :] for h, r0 in chains]

        def block(carry, tiles, items):
            k0s = [pl.multiple_of(j * tk, tk) for j in tiles]
            kjs = [k_ref[0, pl.ds(k0, tk), :] for k0 in k0s]
            vjs = [v_ref[0, pl.ds(k0, tk), :] for k0 in k0s]
            zs = [_dot_nt(qparts[n], kjs[t][:kw]) for n, t, kw, _, _ in items]
            das = [_dot_nt(doparts[n], vjs[t][:kw]) for n, t, kw, _, _ in items]
            keeps = [_both(mask, gate) for _, _, _, mask, gate in items]
            lms, lss, cls = [], [], []
            for z, keep, (n, t, kw, _, _) in zip(zs, keeps, items):
                lm, ls = _softplus_parts(z)
                if keep is not None:
                    lm = jnp.where(keep, lm, 0.0)
                lms.append(lm)
                lss.append(ls)
                cls.append(_split_dot(lm, tri[:kw, :kw], 2))
            cur = list(carry)
            psums, abs_, gs, cgs = [], [], [], []
            for lm, ls, cl, da, keep, (n, t, kw, _, _) in zip(lms, lss, cls, das, keeps, items):
                psum, gsum = cur[n]
                a = jnp.exp(ls + (tots[n] - psum - cl))
                if keep is not None:
                    a = jnp.where(keep, a, 0.0)
                g = a * da
                psums.append(gsum)
                gs.append(g)
                abs_.append(a.astype(BF16))
                cgs.append(_split_dot(g, tri[:kw, :kw], 1))
                cur[n] = (psum + jnp.sum(lm, axis=1, keepdims=True), gsum + jnp.sum(g, axis=1, keepdims=True))
            dks, dvs = {}, {}
            for ls, a, g, cg, gsum, keep, (n, t, kw, _, _) in zip(lss, abs_, gs, cgs, psums, keeps, items):
                h, r0 = chains[n]
                dz = g - jnp.exp(ls) * (gsum + cg)
                if keep is not None:
                    dz = jnp.where(keep, dz, 0.0)
                dz = dz.astype(BF16)
                dqa_ref[h, r0:r0 + rows_c, :] += jnp.dot(dz, kjs[t][:kw], preferred_element_type=F32)
                dkh = _dot_tn(dz, qparts[n])
                dvh = _dot_tn(a, doparts[n])
                dks[t, kw] = dkh if (t, kw) not in dks else dks[t, kw] + dkh
                dvs[t, kw] = dvh if (t, kw) not in dvs else dvs[t, kw] + dvh
            for t, kw in dks:
                dka_ref[pl.ds(k0s[t], kw), :] += dks[t, kw]
                dva_ref[pl.ds(k0s[t], kw), :] += dvs[t, kw]
            return tuple(cur)

        z1 = jnp.zeros((rows_c, 1), F32)
        everyone = [(n, 0, tk, None, None) for n in range(len(chains))]
        upper = [n for n, (_, r0) in enumerate(chains) if r0 >= tk]
        lower = [n for n, (_, r0) in enumerate(chains) if r0 < tk]
        left = jnp.maximum(i * nd - 1, 0)
        code = jnp.clip(jnp.max(nw_ref[0].astype(jnp.int32)), 0, 2 * left + 1)
        too = jnp.where(i > 0, code % 2, 0)
        whole = jnp.minimum(code // 2, left)
        carry = lax.fori_loop(left - whole, left, lambda j, c: block(c, [j], everyone), ((z1, z1),) * len(chains))
        if upper:
            carry = lax.cond(too > 0, lambda c: block(c, [left], [(n, 0, tk, None, None) for n in upper]),
                             lambda c: c, carry)
        carry = block(carry, [left, i * nd],
                      [(n, 0, tk, None, i > 0) for n in lower]
                      + [(n, 1, kw, m, None) for n, kw, m in _diag_work(chains, 0, rows_c, tk)])
        for d in range(1, nd):
            carry = block(carry, [i * nd + d], [(n, 0, kw, m, None) for n, kw, m in _diag_work(chains, d, rows_c, tk)])
        out_ref[0] = (jnp.where(lane < HEAD, dqa_ref[0], dqa_ref[1]) * scale).astype(BF16)
        own = pl.multiple_of(i * t, t)
        out_ref[1] = dka_ref[pl.ds(own, t), :].astype(BF16)
        out_ref[2] = dva_ref[pl.ds(own, t), :].astype(BF16)
        riders_end()

    def rows(sec):
        return pl.BlockSpec((1, t, LANES), lambda hp, st: (sec, nq - 1 - st, hp))

    def whole(sec):
        return pl.BlockSpec((1, s, LANES), lambda hp, st: (sec, 0, hp))

    r_ops, r_shapes, r_scratch, r_aliases = _riders_plumb(riders, 10, 2)
    outs = _pcall(
        body, name=name, grid=(nhp, nq),
        in_specs=[rows(4), whole(5), whole(6), rows(7),
                  pl.BlockSpec((t, LANES), lambda hp, st: (nq - 1 - st, hp + nhp)),
                  pl.BlockSpec((t, LANES), lambda hp, st: (nq - 1 - st, hp)),
                  pl.BlockSpec((2, t, 1), lambda hp, st: (hp, nq - 1 - st, 0)),
                  pl.BlockSpec((1, 8, LANES), lambda hp, st: (hp * nq + nq - 1 - st, 0, 0)),
                  pl.BlockSpec((1, LANES), lambda hp, st: (0, hp + nhp)), ANY] + [ANY] * len(r_ops),
        out_specs=[pl.BlockSpec((4, t, LANES), lambda hp, st: (1, nq - 1 - st, hp)),
                   pl.BlockSpec((8, LANES), lambda hp, st: (0, hp))] + [ANY] * len(r_shapes),
        out_shape=[jax.ShapeDtypeStruct(buf.shape, buf.dtype), jax.ShapeDtypeStruct((8, sw), F32)] + r_shapes,
        input_output_aliases={9: 0, **r_aliases},
        scratch_shapes=[pltpu.VMEM((s, LANES), F32), pltpu.VMEM((s, LANES), F32), pltpu.VMEM((2, t, LANES), F32)]
        + r_scratch,
        compiler_params=_cp(2))(proj, proj, proj, proj, dy, ya, tl, walked, bg, buf, *r_ops)
    return outs[0], outs[1], list(outs[2:])


def _grad_w_in(h, dproj, name):
    s, d = h.shape
    ns, _, sw = dproj.shape

    def body(h_ref, b_ref, o_ref, ht_ref):
        @pl.when(pl.program_id(0) == 0)
        def _():
            ht_ref[...] = h_ref[...].T

        o_ref[...] = jnp.dot(ht_ref[...], b_ref[0], preferred_element_type=F32)

    return _pcall(body, name=name, grid=(ns,),
                  in_specs=[pl.BlockSpec((s, d), lambda j: (0, 0)), pl.BlockSpec((1, s, sw), lambda j: (j, 0, 0))],
                  out_specs=pl.BlockSpec((d, sw), lambda j: (0, j)),
                  out_shape=jax.ShapeDtypeStruct((d, ns * sw), F32),
                  scratch_shapes=[pltpu.VMEM((d, s), BF16)], compiler_params=_cp(1))(h, dproj)


def _inproj_bwd(dproj, w, layer, x, g, dx1, name, riders=()):
    ns, s, sw = dproj.shape
    d = x.shape[1]
    tm = _tile(s, 512)

    def body(*refs):
        own, riders_end = _riders_run(riders, refs, 5, 2, 0, pl.program_id(0), s // tm)
        dp_ref, w_ref, x_ref, g_ref, dx1_ref, dx_ref, dg_ref = own

        @pl.when(pl.program_id(0) == 0)
        def _():
            dg_ref[...] = jnp.zeros_like(dg_ref)

        dh = _dot_nt(dp_ref[0], w_ref[0, :, 0:sw])
        for k in range(1, ns):
            dh = dh + _dot_nt(dp_ref[k], w_ref[0, :, k * sw:(k + 1) * sw])
        dxr, dgr = _rms_bwd_rows(dh, x_ref[...], g_ref[...])
        dx_ref[...] = dx1_ref[...] + dxr
        dg_ref[...] += _colsum8(dgr)
        riders_end()

    row = lambda m: (m, 0)
    fix = lambda m: (0, 0)
    r_ops, r_shapes, r_scratch, r_aliases = _riders_plumb(riders, 5, 2)
    outs = _pcall(body, name=name, grid=(s // tm,),
                  in_specs=[pl.BlockSpec((ns, tm, sw), lambda m: (0, m, 0)),
                            pl.BlockSpec((1, d, ns * sw), lambda m: (layer, 0, 0)),
                            pl.BlockSpec((tm, d), row), pl.BlockSpec((1, d), fix), pl.BlockSpec((tm, d), row)]
                  + [ANY] * len(r_ops),
                  out_specs=[pl.BlockSpec((tm, d), row), pl.BlockSpec((8, d), fix)] + [ANY] * len(r_shapes),
                  out_shape=[jax.ShapeDtypeStruct((s, d), F32), jax.ShapeDtypeStruct((8, d), F32)] + r_shapes,
                  input_output_aliases=r_aliases, scratch_shapes=r_scratch,
                  compiler_params=_cp(1))(dproj, w, x, g, dx1, *r_ops)
    return outs[0], outs[1], list(outs[2:])


def _adamw(w, g, m, v, name):
    r, c = w.shape
    tr = _tile(r, 256)
    c1 = 1.0 - ADAM_B1 ** ADAM_STEP
    c2 = 1.0 - ADAM_B2 ** ADAM_STEP

    def body(w_ref, g_ref, m_ref, v_ref, go_ref, d_ref, mo_ref, vo_ref):
        gv = g_ref[...]
        go_ref[...] = gv
        mn = ADAM_B1 * m_ref[...] + (1.0 - ADAM_B1) * gv
        vn = ADAM_B2 * v_ref[...] + (1.0 - ADAM_B2) * (gv * gv)
        d_ref[...] = -ADAM_LR * ((mn / c1) / (jnp.sqrt(vn / c2) + ADAM_EPS) + ADAM_WD * w_ref[...])
        mo_ref[...] = mn
        vo_ref[...] = vn

    t = pl.BlockSpec((tr, c), lambda i: (i, 0))
    return _pcall(body, name=name, grid=(r // tr,), in_specs=[t] * 4, out_specs=[t] * 4,
                  out_shape=[jax.ShapeDtypeStruct((r, c), F32)] * 4, compiler_params=_cp(1))(w, g, m, v)


def _add_half(grad, other, core, a, name):
    hr, hc = other.shape
    tr = _tile(hr, 256)
    nb = hr // tr

    def body(c_ref, g_ref, o_ref, out_ref, outb_ref):
        v = g_ref[...] + o_ref[...]
        out_ref[...] = v
        outb_ref[...] = v.astype(BF16)

    t = pl.BlockSpec((tr, hc), lambda i, c: (i, 0))
    own = (lambda i, c: (c[0] * nb + i, 0)) if HALF_AXES[a] == 0 else (lambda i, c: (i, c[0]))
    grid_spec = pltpu.PrefetchScalarGridSpec(
        num_scalar_prefetch=1, grid=(nb,), in_specs=[pl.BlockSpec((tr, hc), own), t], out_specs=[t, t])
    return _pcall(body, name=name, grid_spec=grid_spec,
                  out_shape=[jax.ShapeDtypeStruct((hr, hc), F32), jax.ShapeDtypeStruct((hr, hc), BF16)],
                  compiler_params=_cp(1))(core.reshape(1).astype(jnp.int32), grad, other)


def _sum_half(wide, parts, chip, core, layer, a, stack, name):
    _, sr, sc = parts.shape
    tr = _tile(sr, 256)
    nbs = sr // tr

    def body(k_ref, f_ref, p_ref, *rest):
        rest[-1][0] = ((f_ref[...] + p_ref[0].astype(F32)) + p_ref[1].astype(F32)) + p_ref[2].astype(F32)

    f_map = (lambda i, k: (i, k[0])) if SHARD_AXES[a] == 1 else (lambda i, k: (k[0] * nbs + i, 0))
    if HALF_AXES[a] == 0:
        shape, o_map = (DEPTH, 2 * sr, sc), (lambda i, k: (layer, k[1] * nbs + i, 0))
    else:
        shape, o_map = (DEPTH, sr, 2 * sc), (lambda i, k: (layer, i, k[1]))
    in_specs = [pl.BlockSpec((tr, sc), f_map), pl.BlockSpec((3, tr, sc), lambda i, k: (0, i, 0))]
    args = [wide, parts]
    aliases = {}
    if stack is not None:
        in_specs.append(ANY)
        args.append(stack)
        aliases = {3: 0}
    grid_spec = pltpu.PrefetchScalarGridSpec(
        num_scalar_prefetch=1, grid=(nbs,), in_specs=in_specs, out_specs=pl.BlockSpec((1, tr, sc), o_map))
    return _pcall(body, name=name, grid_spec=grid_spec, out_shape=jax.ShapeDtypeStruct(shape, F32),
                  input_output_aliases=aliases,
                  compiler_params=_cp(1))(jnp.stack([chip, core]).astype(jnp.int32), *args)


def _sum_slots(slots, name):
    n = slots.shape[0]

    def body(s_ref, o_ref):
        acc = s_ref[0]
        for i in range(1, n):
            acc = acc + s_ref[i]
        o_ref[...] = acc

    return _pcall(body, name=name, out_shape=jax.ShapeDtypeStruct(slots.shape[1:], F32))(slots)


def _place():
    return lax.axis_index("x"), lax.axis_index("y"), lax.axis_index("c")


def _shard_view(ref, axis, chip, size):
    if axis == 0:
        return ref.at[pl.ds(chip * size, size), :]
    return ref.at[:, pl.ds(chip * size, size)]


SHARD_AXES = (1, 0, 0, 1)
HALF_AXES = tuple(1 - ax for ax in SHARD_AXES)


class _Rider:
    def __init__(self, operands, out_shape, sems, phases, aliased=False):
        self.operands, self.out_shape, self.sems = list(operands), list(out_shape), list(sems)
        self.phases, self.aliased = phases, aliased


def _riders_plumb(riders, n_in, n_out):
    ops, out_shape, scratch, aliases = [], [], [], {}
    for r in riders:
        if r.aliased:
            for k in range(len(r.operands)):
                aliases[n_in + len(ops) + k] = n_out + len(out_shape) + k
        ops += r.operands
        out_shape += r.out_shape
        scratch += r.sems
    return ops, out_shape, scratch, aliases


def _riders_run(riders, refs, n_in, n_out, n_scr, step, nsteps):
    n_rin = sum(len(r.operands) for r in riders)
    n_rout = sum(len(r.out_shape) for r in riders)
    rin = refs[n_in:n_in + n_rin]
    o0 = n_in + n_rin
    rout = refs[o0 + n_out:o0 + n_out + n_rout]
    s0 = o0 + n_out + n_rout
    rsem = refs[s0 + n_scr:]
    own = list(refs[:n_in]) + list(refs[o0:o0 + n_out]) + list(refs[s0:s0 + n_scr])
    lasts = []
    for r in riders:
        ph = r.phases(rin[:len(r.operands)], rout[:len(r.out_shape)], rsem[:len(r.sems)])
        rin, rout, rsem = rin[len(r.operands):], rout[len(r.out_shape):], rsem[len(r.sems):]
        pl.when(step == 0)(ph[0])
        for mid in ph[1:-1]:
            pl.when(step == (3 * nsteps) // 4)(mid)
        lasts.append(ph[-1])

    def finish():
        for last in lasts:
            pl.when(step == nsteps - 1)(last)

    return own, finish


def _gather_phases(ins, outs, ssem, rsem, layer, which):
    n = len(ins)
    x, y, c = _place()
    me = 2 * x + y
    chips = [(1 - x, y), (x, 1 - y), (1 - x, 1 - y)]

    def piece(a, chip, half, of):
        ax = SHARD_AXES[which[a]]
        block = _shard_view(of[a].at[layer], ax, chip, of[a].shape[1 + ax] // 4)
        r = block.shape[0] // 2
        return block.at[pl.ds(half * r, r), :]

    def over_ici(a, j):
        cx, cy = chips[j]
        return pltpu.make_async_remote_copy(
            src_ref=piece(a, me, c, ins), dst_ref=piece(a, me, c, outs), send_sem=ssem.at[a, j],
            recv_sem=rsem.at[a, j], device_id=(cx, cy, c), device_id_type=MESH)

    def landed(a, j, half):
        cx, cy = chips[j]
        return piece(a, 2 * cx + cy, half, outs)

    def to_sibling(a, j):
        got = landed(a, j, c)
        return pltpu.make_async_remote_copy(
            src_ref=got, dst_ref=got, send_sem=ssem.at[a, 3 + j], recv_sem=rsem.at[a, 3 + j],
            device_id=(x, y, 1 - c), device_id_type=MESH)

    def wait_arrival(a, k, place):
        pltpu.make_async_remote_copy(src_ref=place, dst_ref=place, send_sem=ssem.at[a, k], recv_sem=rsem.at[a, k],
                                     device_id=(x, y, c), device_id_type=MESH).wait_recv()

    def start():
        for a in range(n):
            for j in range(3):
                over_ici(a, j).start()

    def pass_on():
        for a in range(n):
            for j in range(3):
                wait_arrival(a, j, landed(a, j, c))
                to_sibling(a, j).start()

    def finish():
        for a in range(n):
            for j in range(3):
                wait_arrival(a, 3 + j, landed(a, j, 1 - c))
        for a in range(n):
            for j in range(3):
                over_ici(a, j).wait_send()
                to_sibling(a, j).wait_send()

    return start, pass_on, finish


def _gather_rider(fulls, layer, which):
    n = len(fulls)
    return _Rider(fulls, [jax.ShapeDtypeStruct(f.shape, f.dtype) for f in fulls],
                  [pltpu.SemaphoreType.DMA((n, 6)), pltpu.SemaphoreType.DMA((n, 6))],
                  lambda ins, outs, sems: _gather_phases(ins, outs, sems[0], sems[1], layer, which), aliased=True)


def _ride_alone(rider, name):
    n = len(rider.operands)

    def body(*refs):
        for phase in rider.phases(refs[:n], refs[n:n + len(rider.out_shape)], refs[n + len(rider.out_shape):]):
            phase()

    return _pcall(body, name=name, in_specs=[ANY] * n, out_specs=[ANY] * len(rider.out_shape),
                  out_shape=rider.out_shape, scratch_shapes=rider.sems,
                  input_output_aliases={a: a for a in range(n)} if rider.aliased else {})(*rider.operands)


def _half_view(ref, a, half):
    n = ref.shape[HALF_AXES[a]] // 2
    if HALF_AXES[a] == 0:
        return ref.at[pl.ds(half * n, n), :]
    return ref.at[:, pl.ds(half * n, n)]


def _swap_rider(grads, which):
    n = len(grads)
    halves = []
    for g, w in zip(grads, which):
        sh = list(g.shape)
        sh[HALF_AXES[w]] //= 2
        halves.append(jax.ShapeDtypeStruct(tuple(sh), g.dtype))

    def phases(srcs, outs, sems):
        x, y, c = _place()

        def copy(a):
            return pltpu.make_async_remote_copy(
                src_ref=_half_view(srcs[a], which[a], 1 - c), dst_ref=outs[a], send_sem=sems[0].at[a],
                recv_sem=sems[1].at[a], device_id=(x, y, 1 - c), device_id_type=MESH)

        def start():
            for a in range(n):
                copy(a).start()

        def finish():
            for a in range(n):
                copy(a).wait()

        return start, finish

    return _Rider(grads, halves, [pltpu.SemaphoreType.DMA((n,)), pltpu.SemaphoreType.DMA((n,))], phases)


def _scatter_rider(sums, which):
    n = len(sums)
    shapes = []
    for f, w in zip(sums, which):
        sh = list(f.shape)
        sh[SHARD_AXES[w]] //= 4
        shapes.append(jax.ShapeDtypeStruct((3,) + tuple(sh), f.dtype))

    def phases(srcs, outs, sems):
        x, y, c = _place()
        chips = [(1 - x, y), (x, 1 - y), (1 - x, 1 - y)]

        def copy(a, j):
            cx, cy = chips[j]
            ax = SHARD_AXES[which[a]]
            src = _shard_view(srcs[a], ax, 2 * cx + cy, srcs[a].shape[ax] // 4)
            return pltpu.make_async_remote_copy(src_ref=src, dst_ref=outs[a].at[j], send_sem=sems[0].at[a, j],
                                                recv_sem=sems[1].at[a, j], device_id=(cx, cy, c), device_id_type=MESH)

        def start():
            for a in range(n):
                for j in range(3):
                    copy(a, j).start()

        def finish():
            for a in range(n):
                for j in range(3):
                    copy(a, j).wait()

        return start, finish

    return _Rider(sums, shapes, [pltpu.SemaphoreType.DMA((n, 3)), pltpu.SemaphoreType.DMA((n, 3))], phases)


def _pair_halves(stacks):
    n = len(stacks)

    def body(*refs):
        ins, outs = refs[:n], refs[n:2 * n]
        ssem, rsem = refs[2 * n:]
        x, y, c = _place()
        cps = [pltpu.make_async_remote_copy(
            src_ref=_half_view(ins[a].at[l], a, c), dst_ref=_half_view(outs[a].at[l], a, c), send_sem=ssem.at[a, l],
            recv_sem=rsem.at[a, l], device_id=(x, y, 1 - c), device_id_type=MESH)
            for a in range(n) for l in range(DEPTH)]
        for cp in cps:
            cp.start()
        for a in range(n):
            for l in range(DEPTH):
                got = _half_view(outs[a].at[l], a, 1 - c)
                pltpu.make_async_remote_copy(src_ref=got, dst_ref=got, send_sem=ssem.at[a, l], recv_sem=rsem.at[a, l],
                                             device_id=(x, y, 1 - c), device_id_type=MESH).wait_recv()
        for cp in cps:
            cp.wait_send()

    return _pcall(body, name="pair_halves", in_specs=[ANY] * n, out_specs=[ANY] * n,
                  out_shape=[jax.ShapeDtypeStruct(st.shape, st.dtype) for st in stacks],
                  input_output_aliases={a: a for a in range(n)},
                  scratch_shapes=[pltpu.SemaphoreType.DMA((n, DEPTH)), pltpu.SemaphoreType.DMA((n, DEPTH))])(*stacks)


class _GradReduce:
    def __init__(self, chip, core):
        self.chip, self.core = chip, core
        self.stacks = [None] * len(SHARD_AXES)

    def add(self, layer, grads, which, got):
        return [(layer, w) + tuple(_add_half(g, o, self.core, w, f"add_half_{layer}_{w}"))
                for g, o, w in zip(grads, got, which)]

    def finish(self, sums, partials):
        for (layer, w, wide, _), pr in zip(sums, partials):
            self.stacks[w] = _sum_half(wide, pr, self.chip, self.core, layer, w, self.stacks[w], f"sum_half_{layer}_{w}")

    def result(self):
        return _pair_halves(self.stacks)


def _exchange_small(pack, name, riders=()):
    nd = 8

    def body(*refs):
        own, riders_end = _riders_run(riders, refs, 1, 1, 2, jnp.int32(0), 1)
        p_ref, o_ref, ssem, rsem = own
        x, y, c = _place()
        me = 4 * x + 2 * y + c
        o_ref[me] = p_ref[...]
        cps = []
        for j in range(1, nd):
            px, py, pc = x ^ (j >> 2), y ^ ((j >> 1) & 1), c ^ (j & 1)
            cps.append(pltpu.make_async_remote_copy(
                src_ref=p_ref, dst_ref=o_ref.at[me], send_sem=ssem.at[j - 1], recv_sem=rsem.at[j - 1],
                device_id=(px, py, pc), device_id_type=MESH))
        for cp in cps:
            cp.start()
        for j in range(1, nd):
            peer = me ^ j
            got = o_ref.at[peer]
            pltpu.make_async_remote_copy(src_ref=got, dst_ref=got, send_sem=ssem.at[j - 1], recv_sem=rsem.at[j - 1],
                                         device_id=(x, y, c), device_id_type=MESH).wait_recv()
        for cp in cps:
            cp.wait_send()
        riders_end()

    vm = pl.BlockSpec(memory_space=pltpu.VMEM)
    r_ops, r_shapes, r_scratch, r_aliases = _riders_plumb(riders, 1, 1)
    outs = _pcall(body, name=name, in_specs=[vm] + [ANY] * len(r_ops), out_specs=[vm] + [ANY] * len(r_shapes),
                  out_shape=[jax.ShapeDtypeStruct((nd,) + pack.shape, pack.dtype)] + r_shapes,
                  input_output_aliases=r_aliases,
                  scratch_shapes=[pltpu.SemaphoreType.DMA((nd - 1,)), pltpu.SemaphoreType.DMA((nd - 1,))]
                  + r_scratch)(pack, *r_ops)
    return (outs[0], list(outs[1:])) if riders else outs[0]


def _row(v):
    return v.reshape(1, -1)


def _local_step(x, p, tgt, norm_g, conv_w, conv_b, branch_g, ple_norm_g, b_pg, final_g, w_in, w_out, w_pg, w_pe,
                gather=False, reduce=None):
    saved = []
    xl = x
    for l in range(DEPTH):
        riders = [_gather_rider([w_out, w_pg, w_pe], 0, [1, 2, 3])] if gather and l == 0 else []
        h, proj, got = _inproj(xl, _row(norm_g[l]), w_in, l, f"inproj_{l}", riders)
        if riders:
            w_out, w_pg, w_pe = got
        later = gather and l + 1 < DEPTH
        riders = [_gather_rider([w_in, w_pg, w_pe], l + 1, [0, 2, 3])] if later else []
        ya, tl, walked, got = _attn_fwd(proj, f"attn_fwd_{l}", riders)
        if riders:
            w_in, w_pg, w_pe = got
        riders = [_gather_rider([w_out], l + 1, [1])] if later else []
        y, got = _mix_fwd(proj, ya, conv_w[l], _row(conv_b[l]), _row(branch_g[l]), f"mix_fwd_{l}", riders)
        if riders:
            w_out, = got
        x1, hn = _outproj(y, w_out, l, xl, _row(ple_norm_g[l]), f"outproj_{l}")
        x2, gate, e = _ple_fwd(hn, w_pg, _row(b_pg[l]), p, w_pe, l, x1, f"ple_fwd_{l}")
        saved.append((xl, h, proj, ya, tl, walked, y, x1, hn, gate, e))
        xl = x2

    sq, dx, d_final = _loss_head(xl, tgt, _row(final_g), "loss_head")

    big = [None] * DEPTH
    carried = []
    small = {k: [None] * DEPTH for k in ("norm_g", "conv_w", "conv_b", "branch_g", "ple_norm_g", "b_pg")}
    for l in reversed(range(DEPTH)):
        xl, h, proj, ya, tl, walked, y, x1, hn, gate, e = saved[l]
        du, de, dx1, dy, db_pg, d_ple, _ = _ple_bwd(dx, gate, e, x1, w_pg, _row(ple_norm_g[l]), w_out, l,
                                                    f"ple_bwd_{l}")
        sums = carried
        g_pg = _mm_tn(hn, du, f"grad_w_pg_{l}")
        g_pe = _mm_tn(p, de, f"grad_w_pe_{l}", a_layer=l)
        g_out = _mm_tn(y, dx1, f"grad_w_out_{l}")
        others = [g_out, g_pg, g_pe]
        riders = [_swap_rider(others, [1, 2, 3])] if reduce is not None else []
        dpc, d_cw, d_cb, d_bg_c, got = _convmix_bwd(dy, proj, conv_w[l], _row(conv_b[l]), _row(branch_g[l]),
                                                    f"convmix_bwd_{l}", riders)
        if reduce is not None:
            sums += reduce.add(l, others, [1, 2, 3], got)
        riders = [_scatter_rider([sm[3] for sm in sums], [sm[1] for sm in sums])] if sums else []
        dproj, d_bg_a, got = _attn_bwd(proj, dy, ya, tl, walked, _row(branch_g[l]), dpc, f"attn_bwd_{l}", riders)
        if sums:
            reduce.finish(sums, got)
        g_in = _grad_w_in(h, dproj, f"grad_w_in_{l}")
        if reduce is None:
            dx, d_norm, _ = _inproj_bwd(dproj, w_in, l, xl, _row(norm_g[l]), dx1, f"inproj_bwd_{l}")
        elif l > 0:
            dx, d_norm, got = _inproj_bwd(dproj, w_in, l, xl, _row(norm_g[l]), dx1, f"inproj_bwd_{l}",
                                          [_swap_rider([g_in], [0])])
            carried = reduce.add(l, [g_in], [0], got)
        else:
            sums = reduce.add(l, [g_in], [0], _ride_alone(_swap_rider([g_in], [0]), "swap_halves_last"))
            dx, d_norm, got = _inproj_bwd(dproj, w_in, l, xl, _row(norm_g[l]), dx1, f"inproj_bwd_{l}",
                                          [_scatter_rider([sm[3] for sm in sums], [0])])
            reduce.finish(sums, got)
        big[l] = (g_in, g_out, g_pg, g_pe)
        small["norm_g"][l] = jnp.sum(d_norm, axis=0)
        small["conv_w"][l] = jnp.sum(d_cw, axis=1)
        small["conv_b"][l] = jnp.sum(d_cb, axis=0)
        small["branch_g"][l] = jnp.concatenate([jnp.sum(d_bg_c, axis=0), jnp.sum(d_bg_a, axis=0)])
        small["ple_norm_g"][l] = jnp.sum(d_ple, axis=0)
        small["b_pg"][l] = jnp.sum(db_pg, axis=0)
    small = {k: jnp.stack(v) for k, v in small.items()}
    small["final_g"] = jnp.sum(d_final, axis=0)
    return sq[0, 0], dx, big, small


SMALL_ORDER = ("norm_g", "conv_w", "conv_b", "branch_g", "ple_norm_g", "b_pg", "final_g")


def _pack(parts, width):
    flat = jnp.concatenate([v.reshape(-1) for v in parts])
    rows = -(-flat.shape[0] // width)
    rows = -(-rows // 8) * 8
    return jnp.pad(flat, (0, rows * width - flat.shape[0])).reshape(rows, width)


def _unpack(packed, like):
    flat = packed.reshape(-1)
    out, off = [], 0
    for v in like:
        out.append(flat[off:off + v.size].reshape(v.shape))
        off += v.size
    return out


def kernel(x, p, norm_g, w_in, conv_w, conv_b, branch_g, w_out, ple_norm_g, w_pg, b_pg, w_pe, final_g, loss_target, m_norm_g, m_w_in, m_conv_w, m_conv_b, m_branch_g, m_w_out, m_ple_norm_g, m_w_pg, m_b_pg, m_w_pe, m_final_g, v_norm_g, v_w_in, v_conv_w, v_conv_b, v_branch_g, v_w_out, v_ple_norm_g, v_w_pg, v_b_pg, v_w_pe, v_final_g):
    ix, iy, ic = _place()
    chip = 2 * ix + iy
    d = x.shape[-1]

    big_w = (w_in, w_out, w_pg, w_pe)
    own = [_cast_into_full(w, chip, ax, f"cast_{i}") for i, (w, ax) in enumerate(zip(big_w, SHARD_AXES))]
    full_in, = _ride_alone(_gather_rider([own[0]], 0, [0]), "gather_w_in_0")
    full_out, full_pg, full_pe = own[1:]
    cw_shard = conv_w.shape[-1]
    cw_slots = _exchange_small(_pack([conv_w], LANES), "exchange_conv_w")
    conv_full = jnp.concatenate([_unpack(cw_slots[2 * k], [conv_w])[0] for k in range(4)], axis=-1)

    reduce = _GradReduce(chip, ic)
    sq, dx, _, small_g = _local_step(
        x[0], p[:, 0], loss_target[0], norm_g, conv_full, conv_b, branch_g, ple_norm_g, b_pg, final_g,
        full_in, full_out, full_pg, full_pe, gather=True, reduce=reduce)

    g_big = reduce.result()

    parts = [small_g[k] for k in SMALL_ORDER] + [sq.reshape(1)]
    slots = _exchange_small(_pack(parts, d), "exchange_small_grads")
    total = _unpack(_sum_slots(slots, "sum_small"), parts)
    g_small = dict(zip(SMALL_ORDER, total[:-1]))
    loss = 0.5 * total[-1][0] / d
    g_small["conv_w"] = lax.dynamic_slice_in_dim(g_small["conv_w"], chip * cw_shard, cw_shard, axis=2)

    grads = dict(g_small)
    grads.update(w_in=g_big[0], w_out=g_big[1], w_pg=g_big[2], w_pe=g_big[3])
    weights = dict(norm_g=norm_g, w_in=w_in, conv_w=conv_w, conv_b=conv_b, branch_g=branch_g, w_out=w_out,
                   ple_norm_g=ple_norm_g, w_pg=w_pg, b_pg=b_pg, w_pe=w_pe, final_g=final_g)
    ms = dict(norm_g=m_norm_g, w_in=m_w_in, conv_w=m_conv_w, conv_b=m_conv_b, branch_g=m_branch_g, w_out=m_w_out,
              ple_norm_g=m_ple_norm_g, w_pg=m_w_pg, b_pg=m_b_pg, w_pe=m_w_pe, final_g=m_final_g)
    vs = dict(norm_g=v_norm_g, w_in=v_w_in, conv_w=v_conv_w, conv_b=v_conv_b, branch_g=v_branch_g, w_out=v_w_out,
              ple_norm_g=v_ple_norm_g, w_pg=v_w_pg, b_pg=v_b_pg, w_pe=v_w_pe, final_g=v_final_g)
    names = ("norm_g", "w_in", "conv_w", "conv_b", "branch_g", "w_out", "ple_norm_g", "w_pg", "b_pg", "w_pe", "final_g")
    delta, new_m, new_v = {}, {}, {}
    for k in ("w_in", "w_out", "w_pg", "w_pe"):
        shp = weights[k].shape
        two = lambda a: a.reshape(-1, shp[-1])
        gr, dl, mn, vn = _adamw(two(weights[k]), two(grads[k]), two(ms[k]), two(vs[k]), f"adamw_{k}")
        delta[k], new_m[k], new_v[k] = dl.reshape(shp), mn.reshape(shp), vn.reshape(shp)
        grads[k] = gr.reshape(shp)
    like = [weights[k] for k in SMALL_ORDER]
    packs = [_pack([src[k] for k in SMALL_ORDER], d) for src in (weights, grads, ms, vs)]
    outs = _adamw(*packs, "adamw_small")
    for res, o in zip((delta, new_m, new_v), outs[1:]):
        res.update(dict(zip(SMALL_ORDER, _unpack(o, like))))

    return (loss, dx[None], *[grads[k] for k in names], *[delta[k] for k in names],
            *[new_m[k] for k in names], *[new_v[k] for k in names])
```

```python
import math

import jax
import jax.numpy as jnp
from jax import lax
from jax.experimental import pallas as pl
from jax.experimental.pallas import tpu as pltpu

F32 = jnp.float32
BF16 = jnp.bfloat16
EPS = 1e-6
HEAD = 64
LANES = 128
ATT_TK = 256
ATT_TQ = 512
ATT_ROWS = 128
ALIVE_LOG = -105.0
DEPTH = 2
VMEM_LIMIT = 56 * 1024 * 1024
MESH = pl.DeviceIdType.MESH
ANY = pl.BlockSpec(memory_space=pl.ANY)

ADAM_LR = 0.001
ADAM_B1 = 0.9
ADAM_B2 = 0.999
ADAM_EPS = 1e-08
ADAM_WD = 0.01
ADAM_STEP = 10


def _pcall(body, **kw):
    return pl.pallas_call(body, **kw)


def _cp(n_axes):
    return pltpu.CompilerParams(dimension_semantics=("arbitrary",) * n_axes, vmem_limit_bytes=VMEM_LIMIT)


def _tile(n, pref):
    return pref if n % pref == 0 else n


def _split_dot(a, b, passes):
    out = None
    rem = a
    for _ in range(passes):
        hi = rem.astype(BF16)
        t = jnp.dot(hi, b, preferred_element_type=F32)
        out = t if out is None else out + t
        rem = rem - hi.astype(F32)
    return out


def _group_mat():
    r = lax.broadcasted_iota(jnp.int32, (LANES, LANES), 0) // HEAD
    c = lax.broadcasted_iota(jnp.int32, (LANES, LANES), 1) // HEAD
    return jnp.where(r == c, 1.0 / HEAD, 0.0).astype(BF16)


def _group_mean(v, gm):
    return _split_dot(v, gm, 2)


def _sigmoid(z):
    return 1.0 / (1.0 + jnp.exp(-z))


def _dot_nt(a, b):
    return lax.dot_general(a, b, (((1,), (1,)), ((), ())), preferred_element_type=F32)


def _dot_tn(a, b):
    return lax.dot_general(a, b, (((0,), (0,)), ((), ())), preferred_element_type=F32)


def _cast_into_full(w, chip, axis, name):
    _, r, c = w.shape
    tr = _tile(r, 256)
    nb = r // tr
    full = (DEPTH, 4 * r, c) if axis == 0 else (DEPTH, r, 4 * c)

    def body(k_ref, w_ref, o_ref):
        o_ref[...] = w_ref[...].astype(BF16)

    out_map = (lambda l, i, k: (l, k[0] * nb + i, 0)) if axis == 0 else (lambda l, i, k: (l, i, k[0]))
    grid_spec = pltpu.PrefetchScalarGridSpec(
        num_scalar_prefetch=1, grid=(DEPTH, nb),
        in_specs=[pl.BlockSpec((1, tr, c), lambda l, i, k: (l, i, 0))],
        out_specs=pl.BlockSpec((1, tr, c), out_map))
    return _pcall(body, name=name, grid_spec=grid_spec, out_shape=jax.ShapeDtypeStruct(full, BF16),
                  compiler_params=_cp(2))(chip.reshape(1).astype(jnp.int32), w)


def _rms_bwd_rows(dh, xv, g):
    r = lax.rsqrt(jnp.mean(xv * xv, axis=-1, keepdims=True) + EPS)
    xn = xv * r
    dxn = dh * g
    dx = r * (dxn - xn * jnp.mean(dxn * xn, axis=-1, keepdims=True))
    return dx, dh * xn


def _colsum8(v):
    tm, d = v.shape
    return jnp.sum(v.reshape(tm // 8, 8, d), axis=0)


def _inproj(x, g, w, layer, name, riders=()):
    s, d = x.shape
    n = w.shape[2]
    sw = d // 2
    ns = n // sw
    tm = _tile(s, 512)

    def body(*refs):
        own, riders_end = _riders_run(riders, refs, 3, 2, 0, pl.program_id(0), s // tm)
        x_ref, g_ref, w_ref, h_ref, o_ref = own
        xv = x_ref[...]
        r = lax.rsqrt(jnp.mean(xv * xv, axis=-1, keepdims=True) + EPS)
        h = (xv * r * g_ref[...]).astype(BF16)
        h_ref[...] = h
        for k in range(ns):
            o_ref[k] = jnp.dot(h, w_ref[0, :, k * sw:(k + 1) * sw], preferred_element_type=F32).astype(BF16)
        riders_end()

    r_ops, r_shapes, r_scratch, r_aliases = _riders_plumb(riders, 3, 2)
    outs = _pcall(body, name=name, grid=(s // tm,),
                  in_specs=[pl.BlockSpec((tm, d), lambda m: (m, 0)), pl.BlockSpec((1, d), lambda m: (0, 0)),
                            pl.BlockSpec((1, d, n), lambda m: (layer, 0, 0))] + [ANY] * len(r_ops),
                  out_specs=[pl.BlockSpec((tm, d), lambda m: (m, 0)), pl.BlockSpec((ns, tm, sw), lambda m: (0, m, 0))]
                  + [ANY] * len(r_shapes),
                  out_shape=[jax.ShapeDtypeStruct((s, d), BF16), jax.ShapeDtypeStruct((ns, s, sw), BF16)] + r_shapes,
                  input_output_aliases=r_aliases, scratch_shapes=r_scratch,
                  compiler_params=_cp(1))(x, g, w, *r_ops)
    return outs[0], outs[1], list(outs[2:])


def _softplus_parts(z):
    lm = jnp.minimum(-z, 0.0) - jnp.log(1.0 + jnp.exp(-jnp.abs(z)))
    return lm, lm + z


def _attn_tiles(s):
    tk = _tile(s, ATT_TK)
    tq = _tile(s, ATT_TQ)
    return tk, tq, tq // tk, min(ATT_ROWS, tq)


def _diag_work(chains, d, rows, tk):
    work = []
    for n, (_, r0) in enumerate(chains):
        if r0 + rows - 1 <= d * tk:
            continue
        kw = tk // 2 if (tk % 2 == 0 and r0 + rows <= d * tk + tk // 2) else tk
        if r0 >= d * tk + kw:
            mask = None
        else:
            row = lax.broadcasted_iota(jnp.int32, (rows, kw), 0)
            col = lax.broadcasted_iota(jnp.int32, (rows, kw), 1)
            mask = col + d * tk < row + r0
        work.append((n, kw, mask))
    return work


def _static_slots(chains, nd, rows, tk):
    diag, left = {}, {}
    for d in range(nd):
        for n, (_, r0) in enumerate(chains):
            if r0 + rows - 1 > d * tk:
                diag[d, n] = len(diag)
    for n, (_, r0) in enumerate(chains):
        if r0 < tk:
            left[n] = len(diag) + len(left)
    return diag, left, len(diag) + len(left)


def _both(mask, gate):
    if mask is None:
        return gate
    if gate is None:
        return mask
    return jnp.logical_and(mask, gate)


def _any_alive(rsums):
    m = rsums[0]
    for r in rsums[1:]:
        m = jnp.maximum(m, r)
    return jnp.max((m > ALIVE_LOG).astype(jnp.int32))


def _attn_fwd(proj, name, riders=()):
    _, s, sw = proj.shape
    nhp = sw // LANES
    tk, tq, nd, rows = _attn_tiles(s)
    nq = s // tq
    scale = 1.0 / math.sqrt(HEAD)

    def body(*refs):
        i = pl.program_id(1)
        own, riders_end = _riders_run(riders, refs, 3, 5, 1, pl.program_id(0) * nq + i, nhp * nq)
        q_ref, k_ref, v_ref, o_ref, tl_ref, nw_ref, sa_ref, sb_ref, acc_ref = own
        tri = (lax.broadcasted_iota(jnp.int32, (tk, tk), 0) >
               lax.broadcasted_iota(jnp.int32, (tk, tk), 1)).astype(BF16)
        lane = lax.broadcasted_iota(jnp.int32, (tq, LANES), 1)
        q = q_ref[0] * jnp.asarray(scale, BF16)
        qms = [jnp.where((lane // HEAD) == h, q, jnp.zeros_like(q)) for h in range(2)]
        acc_ref[...] = jnp.zeros_like(acc_ref)
        chains = [(h, r0) for h in range(2) for r0 in range(0, tq, rows)]
        qparts = [qms[h][r0:r0 + rows] for h, r0 in chains]

        def block(rsums, tiles, items):
            kjs = [k_ref[0, pl.ds(pl.multiple_of(j * tk, tk), tk), :] for j in tiles]
            vjs = [v_ref[0, pl.ds(pl.multiple_of(j * tk, tk), tk), :] for j in tiles]
            zs = [_dot_nt(qparts[n], kjs[t][:kw]) for n, t, kw, _, _, _ in items]
            lms, lss, css = [], [], []
            for z, (n, t, kw, mask, gate, _) in zip(zs, items):
                lm, ls = _softplus_parts(z)
                keep = _both(mask, gate)
                if keep is not None:
                    lm = jnp.where(keep, lm, 0.0)
                lms.append(lm)
                lss.append(ls)
                css.append(_split_dot(lm, tri[:kw, :kw], 2))
            cur = list(rsums)
            for lm, ls, cs, (n, t, kw, mask, gate, slot) in zip(lms, lss, css, items):
                h, r0 = chains[n]
                a = jnp.exp(ls + (cur[n] + cs))
                keep = _both(mask, gate)
                if keep is not None:
                    a = jnp.where(keep, a, 0.0)
                if slot is not None:
                    sa_ref[0, slot, :, 0:kw] = a.astype(BF16)
                    sb_ref[0, slot, :, 0:kw] = jnp.exp(ls).astype(BF16)
                acc_ref[h, r0:r0 + rows, :] += jnp.dot(a.astype(BF16), vjs[t][:kw], preferred_element_type=F32)
                cur[n] = cur[n] + jnp.sum(lm, axis=1, keepdims=True)
            return tuple(cur)

        everyone = [(n, 0, tk, None, None, None) for n in range(len(chains))]
        dslot, lslot, _ = _static_slots(chains, nd, rows, tk)
        upper = [n for n, (_, r0) in enumerate(chains) if r0 >= tk]
        lower = [n for n, (_, r0) in enumerate(chains) if r0 < tk]
        left = jnp.maximum(i * nd - 1, 0)
        rsums = (jnp.zeros((rows, 1), F32),) * len(chains)
        for d in reversed(range(1, nd)):
            rsums = block(rsums, [i * nd + d],
                          [(n, 0, kw, m, None, dslot[d, n]) for n, kw, m in _diag_work(chains, d, rows, tk)])
        rsums = block(rsums, [i * nd, left],
                      [(n, 0, kw, m, None, dslot[0, n]) for n, kw, m in _diag_work(chains, 0, rows, tk)]
                      + [(n, 1, tk, None, i > 0, lslot[n]) for n in lower])

        if upper:
            too = (i > 0) & (_any_alive([rsums[n] for n in upper]) > 0)
            rsums = lax.cond(too, lambda rs: block(rs, [left], [(n, 0, tk, None, None, None) for n in upper]),
                             lambda rs: rs, rsums)
            too = too.astype(jnp.int32)
        else:
            too = jnp.int32(0)

        def walk(c):
            jj, rs, _ = c
            rs = block(rs, [i * nd - 2 - jj], everyone)
            return jj + 1, rs, _any_alive(rs)

        whole, rsums, _ = lax.while_loop(lambda c: (c[0] < i * nd - 1) & (c[2] > 0), walk,
                                         (jnp.int32(0), rsums, _any_alive(rsums)))
        for n, (h, r0) in enumerate(chains):
            tl_ref[h, r0:r0 + rows, :] = rsums[n]
        nw_ref[0] = (jnp.zeros((8, LANES), jnp.int32) + (2 * whole + too)).astype(F32)
        o_ref[...] = jnp.where(lane < HEAD, acc_ref[0], acc_ref[1]).astype(BF16)
        riders_end()

    r_ops, r_shapes, r_scratch, r_aliases = _riders_plumb(riders, 3, 5)
    nslots = _static_slots([(h, r0) for h in range(2) for r0 in range(0, tq, rows)], nd, rows, tk)[2]
    kept = pl.BlockSpec((1, nslots, rows, tk), lambda hp, i: (hp * nq + i, 0, 0, 0))
    outs = _pcall(
        body, name=name, grid=(nhp, nq),
        in_specs=[pl.BlockSpec((1, tq, LANES), lambda hp, i: (4, i, hp)),
                  pl.BlockSpec((1, s, LANES), lambda hp, i: (5, 0, hp)),
                  pl.BlockSpec((1, s, LANES), lambda hp, i: (6, 0, hp))] + [ANY] * len(r_ops),
        out_specs=[pl.BlockSpec((tq, LANES), lambda hp, i: (i, hp)),
                   pl.BlockSpec((2, tq, 1), lambda hp, i: (hp, i, 0)),
                   pl.BlockSpec((1, 8, LANES), lambda hp, i: (hp * nq + i, 0, 0)), kept, kept]
        + [ANY] * len(r_shapes),
        out_shape=[jax.ShapeDtypeStruct((s, sw), BF16), jax.ShapeDtypeStruct((2 * nhp, s, 1), F32),
                   jax.ShapeDtypeStruct((nhp * nq, 8, LANES), F32)]
        + [jax.ShapeDtypeStruct((nhp * nq, nslots, rows, tk), BF16)] * 2 + r_shapes,
        input_output_aliases=r_aliases,
        scratch_shapes=[pltpu.VMEM((2, tq, LANES), F32)] + r_scratch,
        compiler_params=_cp(2))(proj, proj, proj, *r_ops)
    return outs[0], outs[1], outs[2], (outs[3], outs[4]), list(outs[5:])


def _conv_rows(cc_ref, ch_ref, w_ref, b_ref, r, tc):
    r0 = pl.multiple_of(r * tc, tc)
    u = cc_ref[0, pl.ds(r0, tc), :].astype(F32) * ch_ref[0, pl.ds(r0, tc), :].astype(F32)
    p0 = pl.multiple_of(jnp.maximum(r0 - 16, 0), 16)
    up = cc_ref[0, pl.ds(p0, 16), :].astype(F32) * ch_ref[0, pl.ds(p0, 16), :].astype(F32)
    up = up * (r > 0).astype(F32)
    prev1 = up[15:16, :]
    prev2 = up[14:15, :]
    rid = lax.broadcasted_iota(jnp.int32, u.shape, 0)
    s1 = jnp.where(rid == 0, prev1, pltpu.roll(u, 1, axis=0))
    s2 = jnp.where(rid == 0, prev2, jnp.where(rid == 1, prev1, pltpu.roll(u, 2, axis=0)))
    cv = b_ref[...] + s2 * w_ref[0:1, :] + s1 * w_ref[1:2, :] + u * w_ref[2:3, :]
    return r0, u, s1, s2, cv


def _mix_fwd(proj, ya, conv_w, conv_b, bg, name, riders=()):
    _, s, sw = proj.shape
    nh = sw // LANES
    tc = _tile(s, 256)

    def body(*refs):
        c = pl.program_id(0)
        own, riders_end = _riders_run(riders, refs, 9, 1, 0, c, 2 * nh)
        cb_ref, cc_ref, ch_ref, cz_ref, ya_ref, az_ref, w_ref, b_ref, g_ref, y_ref = own
        gm = _group_mat()

        def finish(r0, yv, zg):
            n = yv * lax.rsqrt(_group_mean(yv * yv, gm) + EPS)
            y_ref[pl.ds(r0, tc), :] = (n * g_ref[...] * (zg * _sigmoid(zg))).astype(BF16)

        @pl.when(c < nh)
        def _():
            def step(r, carry):
                r0, _, _, _, cv = _conv_rows(cc_ref, ch_ref, w_ref, b_ref, r, tc)
                yc = cb_ref[0, pl.ds(r0, tc), :].astype(F32) * cv
                finish(r0, yc, cz_ref[0, pl.ds(r0, tc), :].astype(F32))
                return carry
            lax.fori_loop(0, s // tc, step, 0)

        @pl.when(c >= nh)
        def _():
            def step(r, carry):
                r0 = pl.multiple_of(r * tc, tc)
                finish(r0, ya_ref[pl.ds(r0, tc), :].astype(F32), az_ref[0, pl.ds(r0, tc), :].astype(F32))
                return carry
            lax.fori_loop(0, s // tc, step, 0)

        riders_end()

    def sec(k):
        return pl.BlockSpec((1, s, LANES), lambda c: (k, 0, jnp.minimum(c, nh - 1)))

    r_ops, r_shapes, r_scratch, r_aliases = _riders_plumb(riders, 9, 1)
    outs = _pcall(
        body, name=name, grid=(2 * nh,),
        in_specs=[sec(0), sec(1), sec(2), sec(3),
                  pl.BlockSpec((s, LANES), lambda c: (0, jnp.maximum(c - nh, 0))),
                  pl.BlockSpec((1, s, LANES), lambda c: (7, 0, jnp.maximum(c - nh, 0))),
                  pl.BlockSpec((3, LANES), lambda c: (0, jnp.minimum(c, nh - 1))),
                  pl.BlockSpec((1, LANES), lambda c: (0, jnp.minimum(c, nh - 1))),
                  pl.BlockSpec((1, LANES), lambda c: (0, c))] + [ANY] * len(r_ops),
        out_specs=[pl.BlockSpec((s, LANES), lambda c: (0, c))] + [ANY] * len(r_shapes),
        out_shape=[jax.ShapeDtypeStruct((s, 2 * sw), BF16)] + r_shapes,
        input_output_aliases=r_aliases, scratch_shapes=r_scratch, compiler_params=_cp(1),
    )(proj, proj, proj, proj, ya, proj, conv_w, conv_b, bg, *r_ops)
    return outs[0], list(outs[1:])


def _outproj(y, w, layer, x, g, name):
    s, d = x.shape
    tm = _tile(s, 512)

    def body(y_ref, w_ref, x_ref, g_ref, x1_ref, hn_ref):
        x1 = x_ref[...] + jnp.dot(y_ref[...], w_ref[0], preferred_element_type=F32)
        x1_ref[...] = x1
        r = lax.rsqrt(jnp.mean(x1 * x1, axis=-1, keepdims=True) + EPS)
        hn_ref[...] = (x1 * r * g_ref[...]).astype(BF16)

    row = lambda m: (m, 0)
    fix = lambda m: (0, 0)
    return _pcall(body, name=name, grid=(s // tm,),
                  in_specs=[pl.BlockSpec((tm, d), row), pl.BlockSpec((1, d, d), lambda m: (layer, 0, 0)),
                            pl.BlockSpec((tm, d), row), pl.BlockSpec((1, d), fix)],
                  out_specs=[pl.BlockSpec((tm, d), row), pl.BlockSpec((tm, d), row)],
                  out_shape=[jax.ShapeDtypeStruct((s, d), F32), jax.ShapeDtypeStruct((s, d), BF16)],
                  compiler_params=_cp(1))(y, w, x, g)


def _ple_fwd(hn, w_pg, b_pg, p, w_pe, layer, x1, name):
    s, d = x1.shape
    pd = p.shape[2]
    tm = _tile(s, 512)

    def body(hn_ref, wg_ref, b_ref, p_ref, we_ref, x1_ref, x2_ref, gate_ref, e_ref):
        gate = _sigmoid(jnp.dot(hn_ref[...], wg_ref[0], preferred_element_type=F32) + b_ref[...])
        e = jnp.dot(p_ref[0].astype(BF16), we_ref[0], preferred_element_type=F32)
        x2_ref[...] = x1_ref[...] + gate * e
        gate_ref[...] = gate.astype(BF16)
        e_ref[...] = e.astype(BF16)

    row = lambda m: (m, 0)
    fix = lambda m: (0, 0)
    return _pcall(body, name=name, grid=(s // tm,),
                  in_specs=[pl.BlockSpec((tm, d), row), pl.BlockSpec((1, d, d), lambda m: (layer, 0, 0)),
                            pl.BlockSpec((1, d), fix), pl.BlockSpec((1, tm, pd), lambda m: (layer, m, 0)),
                            pl.BlockSpec((1, pd, d), lambda m: (layer, 0, 0)), pl.BlockSpec((tm, d), row)],
                  out_specs=[pl.BlockSpec((tm, d), row)] * 3,
                  out_shape=[jax.ShapeDtypeStruct((s, d), F32), jax.ShapeDtypeStruct((s, d), BF16),
                             jax.ShapeDtypeStruct((s, d), BF16)],
                  compiler_params=_cp(1))(hn, w_pg, b_pg, p, w_pe, x1)


def _loss_head(x, tgt, g, name):
    s, d = x.shape
    tm = _tile(s, 512)

    def body(x_ref, t_ref, g_ref, l_ref, dx_ref, dg_ref):
        m = pl.program_id(0)

        @pl.when(m == 0)
        def _():
            l_ref[...] = jnp.zeros_like(l_ref)
            dg_ref[...] = jnp.zeros_like(dg_ref)

        xv = x_ref[...]
        gv = g_ref[...]
        r = lax.rsqrt(jnp.mean(xv * xv, axis=-1, keepdims=True) + EPS)
        xn = xv * r
        err = xn * gv - t_ref[...]
        l_ref[...] += jnp.sum(err * err)
        dy = err * (1.0 / d)
        dxn = dy * gv
        dx_ref[...] = r * (dxn - xn * jnp.mean(dxn * xn, axis=-1, keepdims=True))
        dg_ref[...] += _colsum8(dy * xn)

    row = lambda m: (m, 0)
    fix = lambda m: (0, 0)
    return _pcall(body, name=name, grid=(s // tm,),
                  in_specs=[pl.BlockSpec((tm, d), row), pl.BlockSpec((tm, d), row), pl.BlockSpec((1, d), fix)],
                  out_specs=[pl.BlockSpec((8, LANES), fix), pl.BlockSpec((tm, d), row), pl.BlockSpec((8, d), fix)],
                  out_shape=[jax.ShapeDtypeStruct((8, LANES), F32), jax.ShapeDtypeStruct((s, d), F32),
                             jax.ShapeDtypeStruct((8, d), F32)],
                  compiler_params=_cp(1))(x, tgt, g)


def _ple_bwd(dx2, gate, e, x1, w_pg, g_ple, w_out, layer, name, riders=()):
    s, d = dx2.shape
    tm = _tile(s, 512)

    def body(*refs):
        m = pl.program_id(0)
        own, riders_end = _riders_run(riders, refs, 7, 6, 0, m, s // tm)
        (dx2_ref, gate_ref, e_ref, x1_ref, wg_ref, g_ref, wo_ref,
         du_ref, de_ref, dx1_ref, dy_ref, db_ref, dg_ref) = own

        @pl.when(m == 0)
        def _():
            db_ref[...] = jnp.zeros_like(db_ref)
            dg_ref[...] = jnp.zeros_like(dg_ref)

        dx2v = dx2_ref[...]
        gate = gate_ref[...].astype(F32)
        du = dx2v * e_ref[...].astype(F32) * gate * (1.0 - gate)
        de_ref[...] = (dx2v * gate).astype(BF16)
        dub = du.astype(BF16)
        du_ref[...] = dub
        db_ref[...] += _colsum8(du)
        dhn = _dot_nt(dub, wg_ref[0])
        dxr, dgr = _rms_bwd_rows(dhn, x1_ref[...], g_ref[...])
        dx1 = dx2v + dxr
        dx1_ref[...] = dx1
        dg_ref[...] += _colsum8(dgr)
        dy_ref[...] = _dot_nt(dx1.astype(BF16), wo_ref[0]).astype(BF16)
        riders_end()

    row = lambda m: (m, 0)
    fix = lambda m: (0, 0)
    t = pl.BlockSpec((tm, d), row)
    r_ops, r_shapes, r_scratch, r_aliases = _riders_plumb(riders, 7, 6)
    outs = _pcall(body, name=name, grid=(s // tm,),
                  in_specs=[t, t, t, t, pl.BlockSpec((1, d, d), lambda m: (layer, 0, 0)), pl.BlockSpec((1, d), fix),
                            pl.BlockSpec((1, d, d), lambda m: (layer, 0, 0))] + [ANY] * len(r_ops),
                  out_specs=[t, t, t, t, pl.BlockSpec((8, d), fix), pl.BlockSpec((8, d), fix)] + [ANY] * len(r_shapes),
                  out_shape=[jax.ShapeDtypeStruct((s, d), BF16), jax.ShapeDtypeStruct((s, d), BF16),
                             jax.ShapeDtypeStruct((s, d), F32), jax.ShapeDtypeStruct((s, d), BF16),
                             jax.ShapeDtypeStruct((8, d), F32), jax.ShapeDtypeStruct((8, d), F32)] + r_shapes,
                  input_output_aliases=r_aliases, scratch_shapes=r_scratch,
                  compiler_params=_cp(1))(dx2, gate, e, x1, w_pg, g_ple, w_out, *r_ops)
    return tuple(outs[:6]) + (list(outs[6:]),)


def _mm_tn(a, b, name, a_layer=None):
    s, ka = a.shape[-2:]
    n = b.shape[1]
    tn = _tile(n, 1024)
    ns = n // tn
    tk = _tile(s, 512)
    nk = s // tk

    def body(a_ref, b_ref, o_ref, acc_ref):
        k = pl.program_id(1)

        @pl.when(k == 0)
        def _():
            acc_ref[...] = jnp.zeros_like(acc_ref)

        av = a_ref[...] if a_layer is None else a_ref[0]
        acc_ref[...] += _dot_tn(av.astype(BF16), b_ref[...].astype(BF16))

        @pl.when(k == nk - 1)
        def _():
            o_ref[...] = acc_ref[...]

    a_spec = (pl.BlockSpec((tk, ka), lambda j, k: (k, 0)) if a_layer is None
              else pl.BlockSpec((1, tk, ka), lambda j, k: (a_layer, k, 0)))
    return _pcall(body, name=name, grid=(ns, nk),
                  in_specs=[a_spec, pl.BlockSpec((tk, tn), lambda j, k: (k, j))],
                  out_specs=pl.BlockSpec((ka, tn), lambda j, k: (0, j)),
                  out_shape=jax.ShapeDtypeStruct((ka, n), F32),
                  scratch_shapes=[pltpu.VMEM((ka, tn), F32)], compiler_params=_cp(2))(a, b)


def _norm_gate_bwd(dy, yv, zg, g, gm):
    r = lax.rsqrt(_group_mean(yv * yv, gm) + EPS)
    n = yv * r
    sg = _sigmoid(zg)
    sil = zg * sg
    dzg = dy * n * g * (sg * (1.0 + zg * (1.0 - sg)))
    dn = dy * g * sil
    dyv = r * (dn - n * _group_mean(dn * n, gm))
    return dyv, dzg, dy * n * sil


def _convmix_bwd(dy, proj, conv_w, conv_b, bg, name, riders=()):
    _, s, sw = proj.shape
    nh = sw // LANES
    tc = _tile(s, 256)
    nr = s // tc

    def body(*refs):
        own, riders_end = _riders_run(riders, refs, 8, 4, 1, pl.program_id(0), nh)
        (dy_ref, cb_ref, cc_ref, ch_ref, cz_ref, w_ref, b_ref, g_ref,
         dp_ref, dw_ref, db_ref, dg_ref, dcv_ref) = own
        gm = _group_mat()
        dcv_ref[pl.ds(s, 8), :] = jnp.zeros((8, LANES), F32)

        def pass1(r, carry):
            dw0, dw1, dw2, db, dg = carry
            r0, u, s1, s2, cv = _conv_rows(cc_ref, ch_ref, w_ref, b_ref, r, tc)
            cb = cb_ref[0, pl.ds(r0, tc), :].astype(F32)
            dyc, dcz, dgr = _norm_gate_bwd(dy_ref[pl.ds(r0, tc), :].astype(F32), cb * cv,
                                           cz_ref[0, pl.ds(r0, tc), :].astype(F32), g_ref[...], gm)
            dp_ref[0, pl.ds(r0, tc), :] = (dyc * cv).astype(BF16)
            dp_ref[3, pl.ds(r0, tc), :] = dcz.astype(BF16)
            dcv = dyc * cb
            dcv_ref[pl.ds(r0, tc), :] = dcv
            return (dw0 + _colsum8(dcv * s2), dw1 + _colsum8(dcv * s1), dw2 + _colsum8(dcv * u),
                    db + _colsum8(dcv), dg + _colsum8(dgr))

        z8 = jnp.zeros((8, LANES), F32)
        dw0, dw1, dw2, db, dg = lax.fori_loop(0, nr, pass1, (z8, z8, z8, z8, z8))
        dw_ref[0] = dw0
        dw_ref[1] = dw1
        dw_ref[2] = dw2
        db_ref[...] = db
        dg_ref[...] = dg

        def pass2(r, carry):
            r0 = pl.multiple_of(r * tc, tc)
            dcv = dcv_ref[pl.ds(r0, tc), :]
            nxt = dcv_ref[pl.ds(pl.multiple_of(r0 + tc, 8), 8), :]
            rid = lax.broadcasted_iota(jnp.int32, dcv.shape, 0)
            n1 = jnp.where(rid == tc - 1, nxt[0:1, :], pltpu.roll(dcv, tc - 1, axis=0))
            n2 = jnp.where(rid == tc - 1, nxt[1:2, :],
                           jnp.where(rid == tc - 2, nxt[0:1, :], pltpu.roll(dcv, tc - 2, axis=0)))
            du = dcv * w_ref[2:3, :] + n1 * w_ref[1:2, :] + n2 * w_ref[0:1, :]
            dp_ref[1, pl.ds(r0, tc), :] = (du * ch_ref[0, pl.ds(r0, tc), :].astype(F32)).astype(BF16)
            dp_ref[2, pl.ds(r0, tc), :] = (du * cc_ref[0, pl.ds(r0, tc), :].astype(F32)).astype(BF16)
            return carry

        lax.fori_loop(0, nr, pass2, 0)
        riders_end()

    def sec(k):
        return pl.BlockSpec((1, s, LANES), lambda c: (k, 0, c))

    col = lambda c: (0, c)
    r_ops, r_shapes, r_scratch, r_aliases = _riders_plumb(riders, 8, 4)
    outs = _pcall(
        body, name=name, grid=(nh,),
        in_specs=[pl.BlockSpec((s, LANES), col), sec(0), sec(1), sec(2), sec(3),
                  pl.BlockSpec((3, LANES), col), pl.BlockSpec((1, LANES), col), pl.BlockSpec((1, LANES), col)]
        + [ANY] * len(r_ops),
        out_specs=[pl.BlockSpec((4, s, LANES), lambda c: (0, 0, c)), pl.BlockSpec((3, 8, LANES), lambda c: (0, 0, c)),
                   pl.BlockSpec((8, LANES), col), pl.BlockSpec((8, LANES), col)] + [ANY] * len(r_shapes),
        out_shape=[jax.ShapeDtypeStruct((8, s, sw), BF16), jax.ShapeDtypeStruct((3, 8, sw), F32),
                   jax.ShapeDtypeStruct((8, sw), F32), jax.ShapeDtypeStruct((8, sw), F32)] + r_shapes,
        input_output_aliases=r_aliases,
        scratch_shapes=[pltpu.VMEM((s + 8, LANES), F32)] + r_scratch, compiler_params=_cp(1),
    )(dy, proj, proj, proj, proj, conv_w, conv_b, bg, *r_ops)
    return tuple(outs[:4]) + (list(outs[4:]),)


def _attn_bwd(proj, dy, ya, tl, walked, kept, bg, buf, name, riders=()):
    _, s, sw = proj.shape
    nhp = sw // LANES
    tk, t, nd, rows_c = _attn_tiles(s)
    nq = s // t
    scale = 1.0 / math.sqrt(HEAD)

    def body(*refs):
        step = pl.program_id(1)
        i = nq - 1 - step
        own, riders_end = _riders_run(riders, refs, 12, 2, 3, pl.program_id(0) * nq + step, nhp * nq)
        (q_ref, k_ref, v_ref, az_ref, dy_ref, ya_ref, tl_ref, nw_ref, g_ref, buf_ref, sa_ref, sb_ref, out_ref, dg_ref,
         dka_ref, dva_ref, dqa_ref) = own

        @pl.when(step == 0)
        def _():
            dka_ref[...] = jnp.zeros_like(dka_ref)
            dva_ref[...] = jnp.zeros_like(dva_ref)
            dg_ref[...] = jnp.zeros_like(dg_ref)

        dyv, dzg, dgr = _norm_gate_bwd(dy_ref[...].astype(F32), ya_ref[...].astype(F32), az_ref[0].astype(F32),
                                       g_ref[...], _group_mat())
        out_ref[3] = dzg.astype(BF16)
        dg_ref[...] += _colsum8(dgr)

        tri = (lax.broadcasted_iota(jnp.int32, (tk, tk), 0) <=
               lax.broadcasted_iota(jnp.int32, (tk, tk), 1)).astype(BF16)
        lane = lax.broadcasted_iota(jnp.int32, (t, LANES), 1)
        q = q_ref[0] * jnp.asarray(scale, BF16)
        do = dyv.astype(BF16)
        qms = [jnp.where((lane // HEAD) == h, q, jnp.zeros_like(q)) for h in range(2)]
        doms = [jnp.where((lane // HEAD) == h, do, jnp.zeros_like(do)) for h in range(2)]
        dqa_ref[...] = jnp.zeros_like(dqa_ref)
        chains = [(h, r0) for h in range(2) for r0 in range(0, t, rows_c)]
        qparts = [qms[h][r0:r0 + rows_c] for h, r0 in chains]
        doparts = [doms[h][r0:r0 + rows_c] for h, r0 in chains]
        tots = [tl_ref[h, r0:r0 + rows_c, :] for h, r0 in chains]

        def block(carry, tiles, items):
            k0s = [pl.multiple_of(j * tk, tk) for j in tiles]
            kjs = [k_ref[0, pl.ds(k0, tk), :] for k0 in k0s]
            vjs = [v_ref[0, pl.ds(k0, tk), :] for k0 in k0s]
            zs = [_dot_nt(qparts[n], kjs[t][:kw]) for n, t, kw, _, _ in items]
            das = [_dot_nt(doparts[n], vjs[t][:kw]) for n, t, kw, _, _ in items]
            keeps = [_both(mask, gate) for _, _, _, mask, gate in items]
            lms, lss, cls = [], [], []
            for z, keep, (n, t, kw, _, _) in zip(zs, keeps, items):
                lm, ls = _softplus_parts(z)
                if keep is not None:
                    lm = jnp.where(keep, lm, 0.0)
                lms.append(lm)
                lss.append(ls)
                cls.append(_split_dot(lm, tri[:kw, :kw], 2))
            cur = list(carry)
            psums, abs_, gs, cgs = [], [], [], []
            for lm, ls, cl, da, keep, (n, t, kw, _, _) in zip(lms, lss, cls, das, keeps, items):
                psum, gsum = cur[n]
                a = jnp.exp(ls + (tots[n] - psum - cl))
                if keep is not None:
                    a = jnp.where(keep, a, 0.0)
                g = a * da
                psums.append(gsum)
                gs.append(g)
                abs_.append(a.astype(BF16))
                cgs.append(_split_dot(g, tri[:kw, :kw], 1))
                cur[n] = (psum + jnp.sum(lm, axis=1, keepdims=True), gsum + jnp.sum(g, axis=1, keepdims=True))
            dks, dvs = {}, {}
            for ls, a, g, cg, gsum, keep, (n, t, kw, _, _) in zip(lss, abs_, gs, cgs, psums, keeps, items):
                h, r0 = chains[n]
                dz = g - jnp.exp(ls) * (gsum + cg)
                if keep is not None:
                    dz = jnp.where(keep, dz, 0.0)
                dz = dz.astype(BF16)
                dqa_ref[h, r0:r0 + rows_c, :] += jnp.dot(dz, kjs[t][:kw], preferred_element_type=F32)
                dkh = _dot_tn(dz, qparts[n])
                dvh = _dot_tn(a, doparts[n])
                dks[t, kw] = dkh if (t, kw) not in dks else dks[t, kw] + dkh
                dvs[t, kw] = dvh if (t, kw) not in dvs else dvs[t, kw] + dvh
            for t, kw in dks:
                dka_ref[pl.ds(k0s[t], kw), :] += dks[t, kw]
                dva_ref[pl.ds(k0s[t], kw), :] += dvs[t, kw]
            return tuple(cur)

        def kept_block(carry, tiles, items):
            k0s = [pl.multiple_of(j * tk, tk) for j in tiles]
            kjs = [k_ref[0, pl.ds(k0, tk), :] for k0 in k0s]
            vjs = [v_ref[0, pl.ds(k0, tk), :] for k0 in k0s]
            das = [_dot_nt(doparts[n], vjs[t][:kw]) for n, t, kw, _, _, _ in items]
            cur = list(carry)
            gsums, kept_a, gs, cgs = [], [], [], []
            for da, (n, t, kw, _, _, slot) in zip(das, items):
                psum, gsum = cur[n]
                a = sa_ref[0, slot, :, 0:kw]
                g = a.astype(F32) * da
                gsums.append(gsum)
                kept_a.append(a)
                gs.append(g)
                cgs.append(_split_dot(g, tri[:kw, :kw], 1))
                cur[n] = (psum, gsum + jnp.sum(g, axis=1, keepdims=True))
            dks, dvs = {}, {}
            for a, g, cg, gsum, (n, t, kw, mask, gate, slot) in zip(kept_a, gs, cgs, gsums, items):
                h, r0 = chains[n]
                dz = g - sb_ref[0, slot, :, 0:kw].astype(F32) * (gsum + cg)
                keep = _both(mask, gate)
                if keep is not None:
                    dz = jnp.where(keep, dz, 0.0)
                dz = dz.astype(BF16)
                dqa_ref[h, r0:r0 + rows_c, :] += jnp.dot(dz, kjs[t][:kw], preferred_element_type=F32)
                dkh = _dot_tn(dz, qparts[n])
                dvh = _dot_tn(a, doparts[n])
                dks[t, kw] = dkh if (t, kw) not in dks else dks[t, kw] + dkh
                dvs[t, kw] = dvh if (t, kw) not in dvs else dvs[t, kw] + dvh
            for t, kw in dks:
                dka_ref[pl.ds(k0s[t], kw), :] += dks[t, kw]
                dva_ref[pl.ds(k0s[t], kw), :] += dvs[t, kw]
            return tuple(cur)

        z1 = jnp.zeros((rows_c, 1), F32)
        everyone = [(n, 0, tk, None, None) for n in range(len(chains))]
        dslot, lslot, _ = _static_slots(chains, nd, rows_c, tk)
        upper = [n for n, (_, r0) in enumerate(chains) if r0 >= tk]
        lower = [n for n, (_, r0) in enumerate(chains) if r0 < tk]
        left = jnp.maximum(i * nd - 1, 0)
        code = jnp.clip(jnp.max(nw_ref[0].astype(jnp.int32)), 0, 2 * left + 1)
        too = jnp.where(i > 0, code % 2, 0)
        whole = jnp.minimum(code // 2, left)
        carry = lax.fori_loop(left - whole, left, lambda j, c: block(c, [j], everyone), ((z1, z1),) * len(chains))
        if upper:
            carry = lax.cond(too > 0, lambda c: block(c, [left], [(n, 0, tk, None, None) for n in upper]),
                             lambda c: c, carry)
        carry = kept_block(carry, [left, i * nd],
                           [(n, 0, tk, None, i > 0, lslot[n]) for n in lower]
                           + [(n, 1, kw, m, None, dslot[0, n]) for n, kw, m in _diag_work(chains, 0, rows_c, tk)])
        for d in range(1, nd):
            carry = kept_block(carry, [i * nd + d],
                               [(n, 0, kw, m, None, dslot[d, n]) for n, kw, m in _diag_work(chains, d, rows_c, tk)])
        out_ref[0] = (jnp.where(lane < HEAD, dqa_ref[0], dqa_ref[1]) * scale).astype(BF16)
        own = pl.multiple_of(i * t, t)
        out_ref[1] = dka_ref[pl.ds(own, t), :].astype(BF16)
        out_ref[2] = dva_ref[pl.ds(own, t), :].astype(BF16)
        riders_end()

    def rows(sec):
        return pl.BlockSpec((1, t, LANES), lambda hp, st: (sec, nq - 1 - st, hp))

    def whole(sec):
        return pl.BlockSpec((1, s, LANES), lambda hp, st: (sec, 0, hp))

    r_ops, r_shapes, r_scratch, r_aliases = _riders_plumb(riders, 12, 2)
    kept_spec = pl.BlockSpec((1,) + kept[0].shape[1:], lambda hp, st: (hp * nq + nq - 1 - st, 0, 0, 0))
    outs = _pcall(
        body, name=name, grid=(nhp, nq),
        in_specs=[rows(4), whole(5), whole(6), rows(7),
                  pl.BlockSpec((t, LANES), lambda hp, st: (nq - 1 - st, hp + nhp)),
                  pl.BlockSpec((t, LANES), lambda hp, st: (nq - 1 - st, hp)),
                  pl.BlockSpec((2, t, 1), lambda hp, st: (hp, nq - 1 - st, 0)),
                  pl.BlockSpec((1, 8, LANES), lambda hp, st: (hp * nq + nq - 1 - st, 0, 0)),
                  pl.BlockSpec((1, LANES), lambda hp, st: (0, hp + nhp)), ANY, kept_spec, kept_spec]
        + [ANY] * len(r_ops),
        out_specs=[pl.BlockSpec((4, t, LANES), lambda hp, st: (1, nq - 1 - st, hp)),
                   pl.BlockSpec((8, LANES), lambda hp, st: (0, hp))] + [ANY] * len(r_shapes),
        out_shape=[jax.ShapeDtypeStruct(buf.shape, buf.dtype), jax.ShapeDtypeStruct((8, sw), F32)] + r_shapes,
        input_output_aliases={9: 0, **r_aliases},
        scratch_shapes=[pltpu.VMEM((s, LANES), F32), pltpu.VMEM((s, LANES), F32), pltpu.VMEM((2, t, LANES), F32)]
        + r_scratch,
        compiler_params=_cp(2))(proj, proj, proj, proj, dy, ya, tl, walked, bg, buf, kept[0], kept[1], *r_ops)
    return outs[0], outs[1], list(outs[2:])


def _grad_w_in(h, dproj, name):
    s, d = h.shape
    ns, _, sw = dproj.shape

    def body(h_ref, b_ref, o_ref, ht_ref):
        @pl.when(pl.program_id(0) == 0)
        def _():
            ht_ref[...] = h_ref[...].T

        o_ref[...] = jnp.dot(ht_ref[...], b_ref[0], preferred_element_type=F32)

    return _pcall(body, name=name, grid=(ns,),
                  in_specs=[pl.BlockSpec((s, d), lambda j: (0, 0)), pl.BlockSpec((1, s, sw), lambda j: (j, 0, 0))],
                  out_specs=pl.BlockSpec((d, sw), lambda j: (0, j)),
                  out_shape=jax.ShapeDtypeStruct((d, ns * sw), F32),
                  scratch_shapes=[pltpu.VMEM((d, s), BF16)], compiler_params=_cp(1))(h, dproj)


def _inproj_bwd(dproj, w, layer, x, g, dx1, name, riders=()):
    ns, s, sw = dproj.shape
    d = x.shape[1]
    tm = _tile(s, 512)

    def body(*refs):
        own, riders_end = _riders_run(riders, refs, 5, 2, 0, pl.program_id(0), s // tm)
        dp_ref, w_ref, x_ref, g_ref, dx1_ref, dx_ref, dg_ref = own

        @pl.when(pl.program_id(0) == 0)
        def _():
            dg_ref[...] = jnp.zeros_like(dg_ref)

        dh = _dot_nt(dp_ref[0], w_ref[0, :, 0:sw])
        for k in range(1, ns):
            dh = dh + _dot_nt(dp_ref[k], w_ref[0, :, k * sw:(k + 1) * sw])
        dxr, dgr = _rms_bwd_rows(dh, x_ref[...], g_ref[...])
        dx_ref[...] = dx1_ref[...] + dxr
        dg_ref[...] += _colsum8(dgr)
        riders_end()

    row = lambda m: (m, 0)
    fix = lambda m: (0, 0)
    r_ops, r_shapes, r_scratch, r_aliases = _riders_plumb(riders, 5, 2)
    outs = _pcall(body, name=name, grid=(s // tm,),
                  in_specs=[pl.BlockSpec((ns, tm, sw), lambda m: (0, m, 0)),
                            pl.BlockSpec((1, d, ns * sw), lambda m: (layer, 0, 0)),
                            pl.BlockSpec((tm, d), row), pl.BlockSpec((1, d), fix), pl.BlockSpec((tm, d), row)]
                  + [ANY] * len(r_ops),
                  out_specs=[pl.BlockSpec((tm, d), row), pl.BlockSpec((8, d), fix)] + [ANY] * len(r_shapes),
                  out_shape=[jax.ShapeDtypeStruct((s, d), F32), jax.ShapeDtypeStruct((8, d), F32)] + r_shapes,
                  input_output_aliases=r_aliases, scratch_shapes=r_scratch,
                  compiler_params=_cp(1))(dproj, w, x, g, dx1, *r_ops)
    return outs[0], outs[1], list(outs[2:])


def _adamw(w, g, m, v, name):
    r, c = w.shape
    tr = _tile(r, 256)
    c1 = 1.0 - ADAM_B1 ** ADAM_STEP
    c2 = 1.0 - ADAM_B2 ** ADAM_STEP

    def body(w_ref, g_ref, m_ref, v_ref, go_ref, d_ref, mo_ref, vo_ref):
        gv = g_ref[...]
        go_ref[...] = gv
        mn = ADAM_B1 * m_ref[...] + (1.0 - ADAM_B1) * gv
        vn = ADAM_B2 * v_ref[...] + (1.0 - ADAM_B2) * (gv * gv)
        d_ref[...] = -ADAM_LR * ((mn / c1) / (jnp.sqrt(vn / c2) + ADAM_EPS) + ADAM_WD * w_ref[...])
        mo_ref[...] = mn
        vo_ref[...] = vn

    t = pl.BlockSpec((tr, c), lambda i: (i, 0))
    return _pcall(body, name=name, grid=(r // tr,), in_specs=[t] * 4, out_specs=[t] * 4,
                  out_shape=[jax.ShapeDtypeStruct((r, c), F32)] * 4, compiler_params=_cp(1))(w, g, m, v)


def _add_half(grad, other, core, a, name):
    hr, hc = other.shape
    tr = _tile(hr, 256)
    nb = hr // tr

    def body(c_ref, g_ref, o_ref, out_ref, outb_ref):
        v = g_ref[...] + o_ref[...]
        out_ref[...] = v
        outb_ref[...] = v.astype(BF16)

    t = pl.BlockSpec((tr, hc), lambda i, c: (i, 0))
    own = (lambda i, c: (c[0] * nb + i, 0)) if HALF_AXES[a] == 0 else (lambda i, c: (i, c[0]))
    grid_spec = pltpu.PrefetchScalarGridSpec(
        num_scalar_prefetch=1, grid=(nb,), in_specs=[pl.BlockSpec((tr, hc), own), t], out_specs=[t, t])
    return _pcall(body, name=name, grid_spec=grid_spec,
                  out_shape=[jax.ShapeDtypeStruct((hr, hc), F32), jax.ShapeDtypeStruct((hr, hc), BF16)],
                  compiler_params=_cp(1))(core.reshape(1).astype(jnp.int32), grad, other)


def _sum_half(wide, parts, chip, core, layer, a, stack, name):
    _, sr, sc = parts.shape
    tr = _tile(sr, 256)
    nbs = sr // tr

    def body(k_ref, f_ref, p_ref, *rest):
        rest[-1][0] = ((f_ref[...] + p_ref[0].astype(F32)) + p_ref[1].astype(F32)) + p_ref[2].astype(F32)

    f_map = (lambda i, k: (i, k[0])) if SHARD_AXES[a] == 1 else (lambda i, k: (k[0] * nbs + i, 0))
    if HALF_AXES[a] == 0:
        shape, o_map = (DEPTH, 2 * sr, sc), (lambda i, k: (layer, k[1] * nbs + i, 0))
    else:
        shape, o_map = (DEPTH, sr, 2 * sc), (lambda i, k: (layer, i, k[1]))
    in_specs = [pl.BlockSpec((tr, sc), f_map), pl.BlockSpec((3, tr, sc), lambda i, k: (0, i, 0))]
    args = [wide, parts]
    aliases = {}
    if stack is not None:
        in_specs.append(ANY)
        args.append(stack)
        aliases = {3: 0}
    grid_spec = pltpu.PrefetchScalarGridSpec(
        num_scalar_prefetch=1, grid=(nbs,), in_specs=in_specs, out_specs=pl.BlockSpec((1, tr, sc), o_map))
    return _pcall(body, name=name, grid_spec=grid_spec, out_shape=jax.ShapeDtypeStruct(shape, F32),
                  input_output_aliases=aliases,
                  compiler_params=_cp(1))(jnp.stack([chip, core]).astype(jnp.int32), *args)


def _sum_slots(slots, name):
    n = slots.shape[0]

    def body(s_ref, o_ref):
        acc = s_ref[0]
        for i in range(1, n):
            acc = acc + s_ref[i]
        o_ref[...] = acc

    return _pcall(body, name=name, out_shape=jax.ShapeDtypeStruct(slots.shape[1:], F32))(slots)


def _place():
    return lax.axis_index("x"), lax.axis_index("y"), lax.axis_index("c")


def _shard_view(ref, axis, chip, size):
    if axis == 0:
        return ref.at[pl.ds(chip * size, size), :]
    return ref.at[:, pl.ds(chip * size, size)]


SHARD_AXES = (1, 0, 0, 1)
HALF_AXES = tuple(1 - ax for ax in SHARD_AXES)


class _Rider:
    def __init__(self, operands, out_shape, sems, phases, aliased=False):
        self.operands, self.out_shape, self.sems = list(operands), list(out_shape), list(sems)
        self.phases, self.aliased = phases, aliased


def _riders_plumb(riders, n_in, n_out):
    ops, out_shape, scratch, aliases = [], [], [], {}
    for r in riders:
        if r.aliased:
            for k in range(len(r.operands)):
                aliases[n_in + len(ops) + k] = n_out + len(out_shape) + k
        ops += r.operands
        out_shape += r.out_shape
        scratch += r.sems
    return ops, out_shape, scratch, aliases


def _riders_run(riders, refs, n_in, n_out, n_scr, step, nsteps):
    n_rin = sum(len(r.operands) for r in riders)
    n_rout = sum(len(r.out_shape) for r in riders)
    rin = refs[n_in:n_in + n_rin]
    o0 = n_in + n_rin
    rout = refs[o0 + n_out:o0 + n_out + n_rout]
    s0 = o0 + n_out + n_rout
    rsem = refs[s0 + n_scr:]
    own = list(refs[:n_in]) + list(refs[o0:o0 + n_out]) + list(refs[s0:s0 + n_scr])
    lasts = []
    for r in riders:
        ph = r.phases(rin[:len(r.operands)], rout[:len(r.out_shape)], rsem[:len(r.sems)])
        rin, rout, rsem = rin[len(r.operands):], rout[len(r.out_shape):], rsem[len(r.sems):]
        pl.when(step == 0)(ph[0])
        for mid in ph[1:-1]:
            pl.when(step == (3 * nsteps) // 4)(mid)
        lasts.append(ph[-1])

    def finish():
        for last in lasts:
            pl.when(step == nsteps - 1)(last)

    return own, finish


def _gather_phases(ins, outs, ssem, rsem, layer, which):
    n = len(ins)
    x, y, c = _place()
    me = 2 * x + y
    chips = [(1 - x, y), (x, 1 - y), (1 - x, 1 - y)]

    def piece(a, chip, half, of):
        ax = SHARD_AXES[which[a]]
        block = _shard_view(of[a].at[layer], ax, chip, of[a].shape[1 + ax] // 4)
        r = block.shape[0] // 2
        return block.at[pl.ds(half * r, r), :]

    def over_ici(a, j):
        cx, cy = chips[j]
        return pltpu.make_async_remote_copy(
            src_ref=piece(a, me, c, ins), dst_ref=piece(a, me, c, outs), send_sem=ssem.at[a, j],
            recv_sem=rsem.at[a, j], device_id=(cx, cy, c), device_id_type=MESH)

    def landed(a, j, half):
        cx, cy = chips[j]
        return piece(a, 2 * cx + cy, half, outs)

    def to_sibling(a, j):
        got = landed(a, j, c)
        return pltpu.make_async_remote_copy(
            src_ref=got, dst_ref=got, send_sem=ssem.at[a, 3 + j], recv_sem=rsem.at[a, 3 + j],
            device_id=(x, y, 1 - c), device_id_type=MESH)

    def wait_arrival(a, k, place):
        pltpu.make_async_remote_copy(src_ref=place, dst_ref=place, send_sem=ssem.at[a, k], recv_sem=rsem.at[a, k],
                                     device_id=(x, y, c), device_id_type=MESH).wait_recv()

    def start():
        for a in range(n):
            for j in range(3):
                over_ici(a, j).start()

    def pass_on():
        for a in range(n):
            for j in range(3):
                wait_arrival(a, j, landed(a, j, c))
                to_sibling(a, j).start()

    def finish():
        for a in range(n):
            for j in range(3):
                wait_arrival(a, 3 + j, landed(a, j, 1 - c))
        for a in range(n):
            for j in range(3):
                over_ici(a, j).wait_send()
                to_sibling(a, j).wait_send()

    return start, pass_on, finish


def _gather_rider(fulls, layer, which):
    n = len(fulls)
    return _Rider(fulls, [jax.ShapeDtypeStruct(f.shape, f.dtype) for f in fulls],
                  [pltpu.SemaphoreType.DMA((n, 6)), pltpu.SemaphoreType.DMA((n, 6))],
                  lambda ins, outs, sems: _gather_phases(ins, outs, sems[0], sems[1], layer, which), aliased=True)


def _ride_alone(rider, name):
    n = len(rider.operands)

    def body(*refs):
        for phase in rider.phases(refs[:n], refs[n:n + len(rider.out_shape)], refs[n + len(rider.out_shape):]):
            phase()

    return _pcall(body, name=name, in_specs=[ANY] * n, out_specs=[ANY] * len(rider.out_shape),
                  out_shape=rider.out_shape, scratch_shapes=rider.sems,
                  input_output_aliases={a: a for a in range(n)} if rider.aliased else {})(*rider.operands)


def _half_view(ref, a, half):
    n = ref.shape[HALF_AXES[a]] // 2
    if HALF_AXES[a] == 0:
        return ref.at[pl.ds(half * n, n), :]
    return ref.at[:, pl.ds(half * n, n)]


def _swap_rider(grads, which):
    n = len(grads)
    halves = []
    for g, w in zip(grads, which):
        sh = list(g.shape)
        sh[HALF_AXES[w]] //= 2
        halves.append(jax.ShapeDtypeStruct(tuple(sh), g.dtype))

    def phases(srcs, outs, sems):
        x, y, c = _place()

        def copy(a):
            return pltpu.make_async_remote_copy(
                src_ref=_half_view(srcs[a], which[a], 1 - c), dst_ref=outs[a], send_sem=sems[0].at[a],
                recv_sem=sems[1].at[a], device_id=(x, y, 1 - c), device_id_type=MESH)

        def start():
            for a in range(n):
                copy(a).start()

        def finish():
            for a in range(n):
                copy(a).wait()

        return start, finish

    return _Rider(grads, halves, [pltpu.SemaphoreType.DMA((n,)), pltpu.SemaphoreType.DMA((n,))], phases)


def _scatter_rider(sums, which):
    n = len(sums)
    shapes = []
    for f, w in zip(sums, which):
        sh = list(f.shape)
        sh[SHARD_AXES[w]] //= 4
        shapes.append(jax.ShapeDtypeStruct((3,) + tuple(sh), f.dtype))

    def phases(srcs, outs, sems):
        x, y, c = _place()
        chips = [(1 - x, y), (x, 1 - y), (1 - x, 1 - y)]

        def copy(a, j):
            cx, cy = chips[j]
            ax = SHARD_AXES[which[a]]
            src = _shard_view(srcs[a], ax, 2 * cx + cy, srcs[a].shape[ax] // 4)
            return pltpu.make_async_remote_copy(src_ref=src, dst_ref=outs[a].at[j], send_sem=sems[0].at[a, j],
                                                recv_sem=sems[1].at[a, j], device_id=(cx, cy, c), device_id_type=MESH)

        def start():
            for a in range(n):
                for j in range(3):
                    copy(a, j).start()

        def finish():
            for a in range(n):
                for j in range(3):
                    copy(a, j).wait()

        return start, finish

    return _Rider(sums, shapes, [pltpu.SemaphoreType.DMA((n, 3)), pltpu.SemaphoreType.DMA((n, 3))], phases)


def _pair_halves(stacks):
    n = len(stacks)

    def body(*refs):
        ins, outs = refs[:n], refs[n:2 * n]
        ssem, rsem = refs[2 * n:]
        x, y, c = _place()
        cps = [pltpu.make_async_remote_copy(
            src_ref=_half_view(ins[a].at[l], a, c), dst_ref=_half_view(outs[a].at[l], a, c), send_sem=ssem.at[a, l],
            recv_sem=rsem.at[a, l], device_id=(x, y, 1 - c), device_id_type=MESH)
            for a in range(n) for l in range(DEPTH)]
        for cp in cps:
            cp.start()
        for a in range(n):
            for l in range(DEPTH):
                got = _half_view(outs[a].at[l], a, 1 - c)
                pltpu.make_async_remote_copy(src_ref=got, dst_ref=got, send_sem=ssem.at[a, l], recv_sem=rsem.at[a, l],
                                             device_id=(x, y, 1 - c), device_id_type=MESH).wait_recv()
        for cp in cps:
            cp.wait_send()

    return _pcall(body, name="pair_halves", in_specs=[ANY] * n, out_specs=[ANY] * n,
                  out_shape=[jax.ShapeDtypeStruct(st.shape, st.dtype) for st in stacks],
                  input_output_aliases={a: a for a in range(n)},
                  scratch_shapes=[pltpu.SemaphoreType.DMA((n, DEPTH)), pltpu.SemaphoreType.DMA((n, DEPTH))])(*stacks)


class _GradReduce:
    def __init__(self, chip, core):
        self.chip, self.core = chip, core
        self.stacks = [None] * len(SHARD_AXES)

    def add(self, layer, grads, which, got):
        return [(layer, w) + tuple(_add_half(g, o, self.core, w, f"add_half_{layer}_{w}"))
                for g, o, w in zip(grads, got, which)]

    def finish(self, sums, partials):
        for (layer, w, wide, _), pr in zip(sums, partials):
            self.stacks[w] = _sum_half(wide, pr, self.chip, self.core, layer, w, self.stacks[w], f"sum_half_{layer}_{w}")

    def result(self):
        return _pair_halves(self.stacks)


def _exchange_small(pack, name, riders=()):
    nd = 8

    def body(*refs):
        own, riders_end = _riders_run(riders, refs, 1, 1, 2, jnp.int32(0), 1)
        p_ref, o_ref, ssem, rsem = own
        x, y, c = _place()
        me = 4 * x + 2 * y + c
        o_ref[me] = p_ref[...]
        cps = []
        for j in range(1, nd):
            px, py, pc = x ^ (j >> 2), y ^ ((j >> 1) & 1), c ^ (j & 1)
            cps.append(pltpu.make_async_remote_copy(
                src_ref=p_ref, dst_ref=o_ref.at[me], send_sem=ssem.at[j - 1], recv_sem=rsem.at[j - 1],
                device_id=(px, py, pc), device_id_type=MESH))
        for cp in cps:
            cp.start()
        for j in range(1, nd):
            peer = me ^ j
            got = o_ref.at[peer]
            pltpu.make_async_remote_copy(src_ref=got, dst_ref=got, send_sem=ssem.at[j - 1], recv_sem=rsem.at[j - 1],
                                         device_id=(x, y, c), device_id_type=MESH).wait_recv()
        for cp in cps:
            cp.wait_send()
        riders_end()

    vm = pl.BlockSpec(memory_space=pltpu.VMEM)
    r_ops, r_shapes, r_scratch, r_aliases = _riders_plumb(riders, 1, 1)
    outs = _pcall(body, name=name, in_specs=[vm] + [ANY] * len(r_ops), out_specs=[vm] + [ANY] * len(r_shapes),
                  out_shape=[jax.ShapeDtypeStruct((nd,) + pack.shape, pack.dtype)] + r_shapes,
                  input_output_aliases=r_aliases,
                  scratch_shapes=[pltpu.SemaphoreType.DMA((nd - 1,)), pltpu.SemaphoreType.DMA((nd - 1,))]
                  + r_scratch)(pack, *r_ops)
    return (outs[0], list(outs[1:])) if riders else outs[0]


def _row(v):
    return v.reshape(1, -1)


def _local_step(x, p, tgt, norm_g, conv_w, conv_b, branch_g, ple_norm_g, b_pg, final_g, w_in, w_out, w_pg, w_pe,
                gather=False, reduce=None):
    saved = []
    xl = x
    for l in range(DEPTH):
        riders = [_gather_rider([w_out, w_pg, w_pe], 0, [1, 2, 3])] if gather and l == 0 else []
        h, proj, got = _inproj(xl, _row(norm_g[l]), w_in, l, f"inproj_{l}", riders)
        if riders:
            w_out, w_pg, w_pe = got
        later = gather and l + 1 < DEPTH
        riders = [_gather_rider([w_in, w_pg, w_pe], l + 1, [0, 2, 3])] if later else []
        ya, tl, walked, kept, got = _attn_fwd(proj, f"attn_fwd_{l}", riders)
        if riders:
            w_in, w_pg, w_pe = got
        riders = [_gather_rider([w_out], l + 1, [1])] if later else []
        y, got = _mix_fwd(proj, ya, conv_w[l], _row(conv_b[l]), _row(branch_g[l]), f"mix_fwd_{l}", riders)
        if riders:
            w_out, = got
        x1, hn = _outproj(y, w_out, l, xl, _row(ple_norm_g[l]), f"outproj_{l}")
        x2, gate, e = _ple_fwd(hn, w_pg, _row(b_pg[l]), p, w_pe, l, x1, f"ple_fwd_{l}")
        saved.append((xl, h, proj, ya, tl, walked, kept, y, x1, hn, gate, e))
        xl = x2

    sq, dx, d_final = _loss_head(xl, tgt, _row(final_g), "loss_head")

    big = [None] * DEPTH
    carried = []
    small = {k: [None] * DEPTH for k in ("norm_g", "conv_w", "conv_b", "branch_g", "ple_norm_g", "b_pg")}
    for l in reversed(range(DEPTH)):
        xl, h, proj, ya, tl, walked, kept, y, x1, hn, gate, e = saved[l]
        du, de, dx1, dy, db_pg, d_ple, _ = _ple_bwd(dx, gate, e, x1, w_pg, _row(ple_norm_g[l]), w_out, l,
                                                    f"ple_bwd_{l}")
        sums = carried
        g_pg = _mm_tn(hn, du, f"grad_w_pg_{l}")
        g_pe = _mm_tn(p, de, f"grad_w_pe_{l}", a_layer=l)
        g_out = _mm_tn(y, dx1, f"grad_w_out_{l}")
        others = [g_out, g_pg, g_pe]
        riders = [_swap_rider(others, [1, 2, 3])] if reduce is not None else []
        dpc, d_cw, d_cb, d_bg_c, got = _convmix_bwd(dy, proj, conv_w[l], _row(conv_b[l]), _row(branch_g[l]),
                                                    f"convmix_bwd_{l}", riders)
        if reduce is not None:
            sums += reduce.add(l, others, [1, 2, 3], got)
        riders = [_scatter_rider([sm[3] for sm in sums], [sm[1] for sm in sums])] if sums else []
        dproj, d_bg_a, got = _attn_bwd(proj, dy, ya, tl, walked, kept, _row(branch_g[l]), dpc, f"attn_bwd_{l}", riders)
        if sums:
            reduce.finish(sums, got)
        g_in = _grad_w_in(h, dproj, f"grad_w_in_{l}")
        if reduce is None:
            dx, d_norm, _ = _inproj_bwd(dproj, w_in, l, xl, _row(norm_g[l]), dx1, f"inproj_bwd_{l}")
        elif l > 0:
            dx, d_norm, got = _inproj_bwd(dproj, w_in, l, xl, _row(norm_g[l]), dx1, f"inproj_bwd_{l}",
                                          [_swap_rider([g_in], [0])])
            carried = reduce.add(l, [g_in], [0], got)
        else:
            sums = reduce.add(l, [g_in], [0], _ride_alone(_swap_rider([g_in], [0]), "swap_halves_last"))
            dx, d_norm, got = _inproj_bwd(dproj, w_in, l, xl, _row(norm_g[l]), dx1, f"inproj_bwd_{l}",
                                          [_scatter_rider([sm[3] for sm in sums], [0])])
            reduce.finish(sums, got)
        big[l] = (g_in, g_out, g_pg, g_pe)
        small["norm_g"][l] = jnp.sum(d_norm, axis=0)
        small["conv_w"][l] = jnp.sum(d_cw, axis=1)
        small["conv_b"][l] = jnp.sum(d_cb, axis=0)
        small["branch_g"][l] = jnp.concatenate([jnp.sum(d_bg_c, axis=0), jnp.sum(d_bg_a, axis=0)])
        small["ple_norm_g"][l] = jnp.sum(d_ple, axis=0)
        small["b_pg"][l] = jnp.sum(db_pg, axis=0)
    small = {k: jnp.stack(v) for k, v in small.items()}
    small["final_g"] = jnp.sum(d_final, axis=0)
    return sq[0, 0], dx, big, small


SMALL_ORDER = ("norm_g", "conv_w", "conv_b", "branch_g", "ple_norm_g", "b_pg", "final_g")


def _pack(parts, width):
    flat = jnp.concatenate([v.reshape(-1) for v in parts])
    rows = -(-flat.shape[0] // width)
    rows = -(-rows // 8) * 8
    return jnp.pad(flat, (0, rows * width - flat.shape[0])).reshape(rows, width)


def _unpack(packed, like):
    flat = packed.reshape(-1)
    out, off = [], 0
    for v in like:
        out.append(flat[off:off + v.size].reshape(v.shape))
        off += v.size
    return out


def kernel(x, p, norm_g, w_in, conv_w, conv_b, branch_g, w_out, ple_norm_g, w_pg, b_pg, w_pe, final_g, loss_target, m_norm_g, m_w_in, m_conv_w, m_conv_b, m_branch_g, m_w_out, m_ple_norm_g, m_w_pg, m_b_pg, m_w_pe, m_final_g, v_norm_g, v_w_in, v_conv_w, v_conv_b, v_branch_g, v_w_out, v_ple_norm_g, v_w_pg, v_b_pg, v_w_pe, v_final_g):
    ix, iy, ic = _place()
    chip = 2 * ix + iy
    d = x.shape[-1]

    big_w = (w_in, w_out, w_pg, w_pe)
    own = [_cast_into_full(w, chip, ax, f"cast_{i}") for i, (w, ax) in enumerate(zip(big_w, SHARD_AXES))]
    full_in, = _ride_alone(_gather_rider([own[0]], 0, [0]), "gather_w_in_0")
    full_out, full_pg, full_pe = own[1:]
    cw_shard = conv_w.shape[-1]
    cw_slots = _exchange_small(_pack([conv_w], LANES), "exchange_conv_w")
    conv_full = jnp.concatenate([_unpack(cw_slots[2 * k], [conv_w])[0] for k in range(4)], axis=-1)

    reduce = _GradReduce(chip, ic)
    sq, dx, _, small_g = _local_step(
        x[0], p[:, 0], loss_target[0], norm_g, conv_full, conv_b, branch_g, ple_norm_g, b_pg, final_g,
        full_in, full_out, full_pg, full_pe, gather=True, reduce=reduce)

    g_big = reduce.result()

    parts = [small_g[k] for k in SMALL_ORDER] + [sq.reshape(1)]
    slots = _exchange_small(_pack(parts, d), "exchange_small_grads")
    total = _unpack(_sum_slots(slots, "sum_small"), parts)
    g_small = dict(zip(SMALL_ORDER, total[:-1]))
    loss = 0.5 * total[-1][0] / d
    g_small["conv_w"] = lax.dynamic_slice_in_dim(g_small["conv_w"], chip * cw_shard, cw_shard, axis=2)

    grads = dict(g_small)
    grads.update(w_in=g_big[0], w_out=g_big[1], w_pg=g_big[2], w_pe=g_big[3])
    weights = dict(norm_g=norm_g, w_in=w_in, conv_w=conv_w, conv_b=conv_b, branch_g=branch_g, w_out=w_out,
                   ple_norm_g=ple_norm_g, w_pg=w_pg, b_pg=b_pg, w_pe=w_pe, final_g=final_g)
    ms = dict(norm_g=m_norm_g, w_in=m_w_in, conv_w=m_conv_w, conv_b=m_conv_b, branch_g=m_branch_g, w_out=m_w_out,
              ple_norm_g=m_ple_norm_g, w_pg=m_w_pg, b_pg=m_b_pg, w_pe=m_w_pe, final_g=m_final_g)
    vs = dict(norm_g=v_norm_g, w_in=v_w_in, conv_w=v_conv_w, conv_b=v_conv_b, branch_g=v_branch_g, w_out=v_w_out,
              ple_norm_g=v_ple_norm_g, w_pg=v_w_pg, b_pg=v_b_pg, w_pe=v_w_pe, final_g=v_final_g)
    names = ("norm_g", "w_in", "conv_w", "conv_b", "branch_g", "w_out", "ple_norm_g", "w_pg", "b_pg", "w_pe", "final_g")
    delta, new_m, new_v = {}, {}, {}
    for k in ("w_in", "w_out", "w_pg", "w_pe"):
        shp = weights[k].shape
        two = lambda a: a.reshape(-1, shp[-1])
        gr, dl, mn, vn = _adamw(two(weights[k]), two(grads[k]), two(ms[k]), two(vs[k]), f"adamw_{k}")
        delta[k], new_m[k], new_v[k] = dl.reshape(shp), mn.reshape(shp), vn.reshape(shp)
        grads[k] = gr.reshape(shp)
    like = [weights[k] for k in SMALL_ORDER]
    packs = [_pack([src[k] for k in SMALL_ORDER], d) for src in (weights, grads, ms, vs)]
    outs = _adamw(*packs, "adamw_small")
    for res, o in zip((delta, new_m, new_v), outs[1:]):
        res.update(dict(zip(SMALL_ORDER, _unpack(o, like))))

    return (loss, dx[None], *[grads[k] for k in names], *[delta[k] for k in names],
            *[new_m[k] for k in names], *[new_v[k] for k in names])
```

```python
import math

import jax
import jax.numpy as jnp
from jax import lax
from jax.experimental import pallas as pl
from jax.experimental.pallas import tpu as pltpu

F32 = jnp.float32
BF16 = jnp.bfloat16
EPS = 1e-6
HEAD = 64
LANES = 128
ATT_TK = 256
ATT_TQ = 512
ATT_ROWS = 128
ALIVE_LOG = -105.0
DEPTH = 2
VMEM_LIMIT = 56 * 1024 * 1024
MESH = pl.DeviceIdType.MESH
ANY = pl.BlockSpec(memory_space=pl.ANY)

ADAM_LR = 0.001
ADAM_B1 = 0.9
ADAM_B2 = 0.999
ADAM_EPS = 1e-08
ADAM_WD = 0.01
ADAM_STEP = 10


def _pcall(body, **kw):
    return pl.pallas_call(body, **kw)


def _cp(n_axes):
    return pltpu.CompilerParams(dimension_semantics=("arbitrary",) * n_axes, vmem_limit_bytes=VMEM_LIMIT)


def _tile(n, pref):
    return pref if n % pref == 0 else n


def _split_dot(a, b, passes):
    out = None
    rem = a
    for _ in range(passes):
        hi = rem.astype(BF16)
        t = jnp.dot(hi, b, preferred_element_type=F32)
        out = t if out is None else out + t
        rem = rem - hi.astype(F32)
    return out


def _group_mat():
    r = lax.broadcasted_iota(jnp.int32, (LANES, LANES), 0) // HEAD
    c = lax.broadcasted_iota(jnp.int32, (LANES, LANES), 1) // HEAD
    return jnp.where(r == c, 1.0 / HEAD, 0.0).astype(BF16)


def _group_mean(v, gm):
    return _split_dot(v, gm, 1)


def _sigmoid(z):
    return 1.0 / (1.0 + jnp.exp(-z))


def _dot_nt(a, b):
    return lax.dot_general(a, b, (((1,), (1,)), ((), ())), preferred_element_type=F32)


def _dot_tn(a, b):
    return lax.dot_general(a, b, (((0,), (0,)), ((), ())), preferred_element_type=F32)


def _cast_into_full(w, chip, axis, name):
    _, r, c = w.shape
    tr = _tile(r, 256)
    nb = r // tr
    full = (DEPTH, 4 * r, c) if axis == 0 else (DEPTH, r, 4 * c)

    def body(k_ref, w_ref, o_ref):
        o_ref[...] = w_ref[...].astype(BF16)

    out_map = (lambda l, i, k: (l, k[0] * nb + i, 0)) if axis == 0 else (lambda l, i, k: (l, i, k[0]))
    grid_spec = pltpu.PrefetchScalarGridSpec(
        num_scalar_prefetch=1, grid=(DEPTH, nb),
        in_specs=[pl.BlockSpec((1, tr, c), lambda l, i, k: (l, i, 0))],
        out_specs=pl.BlockSpec((1, tr, c), out_map))
    return _pcall(body, name=name, grid_spec=grid_spec, out_shape=jax.ShapeDtypeStruct(full, BF16),
                  compiler_params=_cp(2))(chip.reshape(1).astype(jnp.int32), w)


def _rms_bwd_rows(dh, xv, g):
    r = lax.rsqrt(jnp.mean(xv * xv, axis=-1, keepdims=True) + EPS)
    xn = xv * r
    dxn = dh * g
    dx = r * (dxn - xn * jnp.mean(dxn * xn, axis=-1, keepdims=True))
    return dx, dh * xn


def _colsum8(v):
    tm, d = v.shape
    return jnp.sum(v.reshape(tm // 8, 8, d), axis=0)


def _inproj(x, g, w, layer, name, riders=()):
    s, d = x.shape
    n = w.shape[2]
    sw = d // 2
    ns = n // sw
    tm = _tile(s, 512)

    def body(*refs):
        own, riders_end = _riders_run(riders, refs, 3, 2, 0, pl.program_id(0), s // tm)
        x_ref, g_ref, w_ref, h_ref, o_ref = own
        xv = x_ref[...]
        r = lax.rsqrt(jnp.mean(xv * xv, axis=-1, keepdims=True) + EPS)
        h = (xv * r * g_ref[...]).astype(BF16)
        h_ref[...] = h
        for k in range(ns):
            o_ref[k] = jnp.dot(h, w_ref[0, :, k * sw:(k + 1) * sw], preferred_element_type=F32).astype(BF16)
        riders_end()

    r_ops, r_shapes, r_scratch, r_aliases = _riders_plumb(riders, 3, 2)
    outs = _pcall(body, name=name, grid=(s // tm,),
                  in_specs=[pl.BlockSpec((tm, d), lambda m: (m, 0)), pl.BlockSpec((1, d), lambda m: (0, 0)),
                            pl.BlockSpec((1, d, n), lambda m: (layer, 0, 0))] + [ANY] * len(r_ops),
                  out_specs=[pl.BlockSpec((tm, d), lambda m: (m, 0)), pl.BlockSpec((ns, tm, sw), lambda m: (0, m, 0))]
                  + [ANY] * len(r_shapes),
                  out_shape=[jax.ShapeDtypeStruct((s, d), BF16), jax.ShapeDtypeStruct((ns, s, sw), BF16)] + r_shapes,
                  input_output_aliases=r_aliases, scratch_shapes=r_scratch,
                  compiler_params=_cp(1))(x, g, w, *r_ops)
    return outs[0], outs[1], list(outs[2:])


def _softplus_parts(z):
    lm = jnp.minimum(-z, 0.0) - jnp.log(1.0 + jnp.exp(-jnp.abs(z)))
    return lm, lm + z


def _attn_tiles(s):
    tk = _tile(s, ATT_TK)
    tq = _tile(s, ATT_TQ)
    return tk, tq, tq // tk, min(ATT_ROWS, tq)


def _diag_work(chains, d, rows, tk):
    work = []
    for n, (_, r0) in enumerate(chains):
        if r0 + rows - 1 <= d * tk:
            continue
        kw = tk // 2 if (tk % 2 == 0 and r0 + rows <= d * tk + tk // 2) else tk
        if r0 >= d * tk + kw:
            mask = None
        else:
            row = lax.broadcasted_iota(jnp.int32, (rows, kw), 0)
            col = lax.broadcasted_iota(jnp.int32, (rows, kw), 1)
            mask = col + d * tk < row + r0
        work.append((n, kw, mask))
    return work


def _static_slots(chains, nd, rows, tk):
    diag, left = {}, {}
    for d in range(nd):
        for n, (_, r0) in enumerate(chains):
            if r0 + rows - 1 > d * tk:
                diag[d, n] = len(diag)
    for n, (_, r0) in enumerate(chains):
        if r0 < tk:
            left[n] = len(diag) + len(left)
    return diag, left, len(diag) + len(left)


def _both(mask, gate):
    if mask is None:
        return gate
    if gate is None:
        return mask
    return jnp.logical_and(mask, gate)


def _any_alive(rsums):
    m = rsums[0]
    for r in rsums[1:]:
        m = jnp.maximum(m, r)
    return jnp.max((m > ALIVE_LOG).astype(jnp.int32))


def _attn_fwd(proj, name, riders=()):
    _, s, sw = proj.shape
    nhp = sw // LANES
    tk, tq, nd, rows = _attn_tiles(s)
    nq = s // tq
    scale = 1.0 / math.sqrt(HEAD)

    def body(*refs):
        i = pl.program_id(1)
        own, riders_end = _riders_run(riders, refs, 3, 5, 1, pl.program_id(0) * nq + i, nhp * nq)
        q_ref, k_ref, v_ref, o_ref, tl_ref, nw_ref, sa_ref, sb_ref, acc_ref = own
        tri = (lax.broadcasted_iota(jnp.int32, (tk, tk), 0) >
               lax.broadcasted_iota(jnp.int32, (tk, tk), 1)).astype(BF16)
        lane = lax.broadcasted_iota(jnp.int32, (tq, LANES), 1)
        q = q_ref[0] * jnp.asarray(scale, BF16)
        qms = [jnp.where((lane // HEAD) == h, q, jnp.zeros_like(q)) for h in range(2)]
        acc_ref[...] = jnp.zeros_like(acc_ref)
        chains = [(h, r0) for h in range(2) for r0 in range(0, tq, rows)]
        qparts = [qms[h][r0:r0 + rows] for h, r0 in chains]

        def block(rsums, tiles, items):
            kjs = [k_ref[0, pl.ds(pl.multiple_of(j * tk, tk), tk), :] for j in tiles]
            vjs = [v_ref[0, pl.ds(pl.multiple_of(j * tk, tk), tk), :] for j in tiles]
            zs = [_dot_nt(qparts[n], kjs[t][:kw]) for n, t, kw, _, _, _ in items]
            lms, lss, css = [], [], []
            for z, (n, t, kw, mask, gate, _) in zip(zs, items):
                lm, ls = _softplus_parts(z)
                keep = _both(mask, gate)
                if keep is not None:
                    lm = jnp.where(keep, lm, 0.0)
                lms.append(lm)
                lss.append(ls)
                css.append(_split_dot(lm, tri[:kw, :kw], 2))
            cur = list(rsums)
            for lm, ls, cs, (n, t, kw, mask, gate, slot) in zip(lms, lss, css, items):
                h, r0 = chains[n]
                a = jnp.exp(ls + (cur[n] + cs))
                keep = _both(mask, gate)
                if keep is not None:
                    a = jnp.where(keep, a, 0.0)
                if slot is not None:
                    sa_ref[0, slot, :, 0:kw] = a.astype(BF16)
                    sb_ref[0, slot, :, 0:kw] = jnp.exp(ls).astype(BF16)
                acc_ref[h, r0:r0 + rows, :] += jnp.dot(a.astype(BF16), vjs[t][:kw], preferred_element_type=F32)
                cur[n] = cur[n] + jnp.sum(lm, axis=1, keepdims=True)
            return tuple(cur)

        everyone = [(n, 0, tk, None, None, None) for n in range(len(chains))]
        dslot, lslot, _ = _static_slots(chains, nd, rows, tk)
        upper = [n for n, (_, r0) in enumerate(chains) if r0 >= tk]
        lower = [n for n, (_, r0) in enumerate(chains) if r0 < tk]
        left = jnp.maximum(i * nd - 1, 0)
        rsums = (jnp.zeros((rows, 1), F32),) * len(chains)
        for d in reversed(range(1, nd)):
            rsums = block(rsums, [i * nd + d],
                          [(n, 0, kw, m, None, dslot[d, n]) for n, kw, m in _diag_work(chains, d, rows, tk)])
        rsums = block(rsums, [i * nd, left],
                      [(n, 0, kw, m, None, dslot[0, n]) for n, kw, m in _diag_work(chains, 0, rows, tk)]
                      + [(n, 1, tk, None, i > 0, lslot[n]) for n in lower])

        if upper:
            too = (i > 0) & (_any_alive([rsums[n] for n in upper]) > 0)
            rsums = lax.cond(too, lambda rs: block(rs, [left], [(n, 0, tk, None, None, None) for n in upper]),
                             lambda rs: rs, rsums)
            too = too.astype(jnp.int32)
        else:
            too = jnp.int32(0)

        def walk(c):
            jj, rs, _ = c
            rs = block(rs, [i * nd - 2 - jj], everyone)
            return jj + 1, rs, _any_alive(rs)

        whole, rsums, _ = lax.while_loop(lambda c: (c[0] < i * nd - 1) & (c[2] > 0), walk,
                                         (jnp.int32(0), rsums, _any_alive(rsums)))
        for n, (h, r0) in enumerate(chains):
            tl_ref[h, r0:r0 + rows, :] = rsums[n]
        nw_ref[0] = (jnp.zeros((8, LANES), jnp.int32) + (2 * whole + too)).astype(F32)
        o_ref[...] = jnp.where(lane < HEAD, acc_ref[0], acc_ref[1]).astype(BF16)
        riders_end()

    r_ops, r_shapes, r_scratch, r_aliases = _riders_plumb(riders, 3, 5)
    nslots = _static_slots([(h, r0) for h in range(2) for r0 in range(0, tq, rows)], nd, rows, tk)[2]
    kept = pl.BlockSpec((1, nslots, rows, tk), lambda hp, i: (hp * nq + i, 0, 0, 0))
    outs = _pcall(
        body, name=name, grid=(nhp, nq),
        in_specs=[pl.BlockSpec((1, tq, LANES), lambda hp, i: (4, i, hp)),
                  pl.BlockSpec((1, s, LANES), lambda hp, i: (5, 0, hp)),
                  pl.BlockSpec((1, s, LANES), lambda hp, i: (6, 0, hp))] + [ANY] * len(r_ops),
        out_specs=[pl.BlockSpec((tq, LANES), lambda hp, i: (i, hp)),
                   pl.BlockSpec((2, tq, 1), lambda hp, i: (hp, i, 0)),
                   pl.BlockSpec((1, 8, LANES), lambda hp, i: (hp * nq + i, 0, 0)), kept, kept]
        + [ANY] * len(r_shapes),
        out_shape=[jax.ShapeDtypeStruct((s, sw), BF16), jax.ShapeDtypeStruct((2 * nhp, s, 1), F32),
                   jax.ShapeDtypeStruct((nhp * nq, 8, LANES), F32)]
        + [jax.ShapeDtypeStruct((nhp * nq, nslots, rows, tk), BF16)] * 2 + r_shapes,
        input_output_aliases=r_aliases,
        scratch_shapes=[pltpu.VMEM((2, tq, LANES), F32)] + r_scratch,
        compiler_params=_cp(2))(proj, proj, proj, *r_ops)
    return outs[0], outs[1], outs[2], (outs[3], outs[4]), list(outs[5:])


def _conv_rows(cc_ref, ch_ref, w_ref, b_ref, r, tc):
    r0 = pl.multiple_of(r * tc, tc)
    u = cc_ref[0, pl.ds(r0, tc), :].astype(F32) * ch_ref[0, pl.ds(r0, tc), :].astype(F32)
    p0 = pl.multiple_of(jnp.maximum(r0 - 16, 0), 16)
    up = cc_ref[0, pl.ds(p0, 16), :].astype(F32) * ch_ref[0, pl.ds(p0, 16), :].astype(F32)
    up = up * (r > 0).astype(F32)
    prev1 = up[15:16, :]
    prev2 = up[14:15, :]
    rid = lax.broadcasted_iota(jnp.int32, u.shape, 0)
    s1 = jnp.where(rid == 0, prev1, pltpu.roll(u, 1, axis=0))
    s2 = jnp.where(rid == 0, prev2, jnp.where(rid == 1, prev1, pltpu.roll(u, 2, axis=0)))
    cv = b_ref[...] + s2 * w_ref[0:1, :] + s1 * w_ref[1:2, :] + u * w_ref[2:3, :]
    return r0, u, s1, s2, cv


def _mix_fwd(proj, ya, conv_w, conv_b, bg, name, riders=()):
    _, s, sw = proj.shape
    nh = sw // LANES
    tc = _tile(s, 256)

    def body(*refs):
        c = pl.program_id(0)
        own, riders_end = _riders_run(riders, refs, 9, 1, 0, c, 2 * nh)
        cb_ref, cc_ref, ch_ref, cz_ref, ya_ref, az_ref, w_ref, b_ref, g_ref, y_ref = own
        gm = _group_mat()

        def finish(r0, yv, zg):
            n = yv * lax.rsqrt(_group_mean(yv * yv, gm) + EPS)
            y_ref[pl.ds(r0, tc), :] = (n * g_ref[...] * (zg * _sigmoid(zg))).astype(BF16)

        @pl.when(c < nh)
        def _():
            def step(r, carry):
                r0, _, _, _, cv = _conv_rows(cc_ref, ch_ref, w_ref, b_ref, r, tc)
                yc = cb_ref[0, pl.ds(r0, tc), :].astype(F32) * cv
                finish(r0, yc, cz_ref[0, pl.ds(r0, tc), :].astype(F32))
                return carry
            lax.fori_loop(0, s // tc, step, 0)

        @pl.when(c >= nh)
        def _():
            def step(r, carry):
                r0 = pl.multiple_of(r * tc, tc)
                finish(r0, ya_ref[pl.ds(r0, tc), :].astype(F32), az_ref[0, pl.ds(r0, tc), :].astype(F32))
                return carry
            lax.fori_loop(0, s // tc, step, 0)

        riders_end()

    def sec(k):
        return pl.BlockSpec((1, s, LANES), lambda c: (k, 0, jnp.minimum(c, nh - 1)))

    r_ops, r_shapes, r_scratch, r_aliases = _riders_plumb(riders, 9, 1)
    outs = _pcall(
        body, name=name, grid=(2 * nh,),
        in_specs=[sec(0), sec(1), sec(2), sec(3),
                  pl.BlockSpec((s, LANES), lambda c: (0, jnp.maximum(c - nh, 0))),
                  pl.BlockSpec((1, s, LANES), lambda c: (7, 0, jnp.maximum(c - nh, 0))),
                  pl.BlockSpec((3, LANES), lambda c: (0, jnp.minimum(c, nh - 1))),
                  pl.BlockSpec((1, LANES), lambda c: (0, jnp.minimum(c, nh - 1))),
                  pl.BlockSpec((1, LANES), lambda c: (0, c))] + [ANY] * len(r_ops),
        out_specs=[pl.BlockSpec((s, LANES), lambda c: (0, c))] + [ANY] * len(r_shapes),
        out_shape=[jax.ShapeDtypeStruct((s, 2 * sw), BF16)] + r_shapes,
        input_output_aliases=r_aliases, scratch_shapes=r_scratch, compiler_params=_cp(1),
    )(proj, proj, proj, proj, ya, proj, conv_w, conv_b, bg, *r_ops)
    return outs[0], list(outs[1:])


def _outproj(y, w, layer, x, g, name):
    s, d = x.shape
    tm = _tile(s, 512)

    def body(y_ref, w_ref, x_ref, g_ref, x1_ref, hn_ref):
        x1 = x_ref[...] + jnp.dot(y_ref[...], w_ref[0], preferred_element_type=F32)
        x1_ref[...] = x1
        r = lax.rsqrt(jnp.mean(x1 * x1, axis=-1, keepdims=True) + EPS)
        hn_ref[...] = (x1 * r * g_ref[...]).astype(BF16)

    row = lambda m: (m, 0)
    fix = lambda m: (0, 0)
    return _pcall(body, name=name, grid=(s // tm,),
                  in_specs=[pl.BlockSpec((tm, d), row), pl.BlockSpec((1, d, d), lambda m: (layer, 0, 0)),
                            pl.BlockSpec((tm, d), row), pl.BlockSpec((1, d), fix)],
                  out_specs=[pl.BlockSpec((tm, d), row), pl.BlockSpec((tm, d), row)],
                  out_shape=[jax.ShapeDtypeStruct((s, d), F32), jax.ShapeDtypeStruct((s, d), BF16)],
                  compiler_params=_cp(1))(y, w, x, g)


def _ple_fwd(hn, w_pg, b_pg, p, w_pe, layer, x1, name):
    s, d = x1.shape
    pd = p.shape[2]
    tm = _tile(s, 512)

    def body(hn_ref, wg_ref, b_ref, p_ref, we_ref, x1_ref, x2_ref, gate_ref, e_ref):
        gate = _sigmoid(jnp.dot(hn_ref[...], wg_ref[0], preferred_element_type=F32) + b_ref[...])
        e = jnp.dot(p_ref[0].astype(BF16), we_ref[0], preferred_element_type=F32)
        x2_ref[...] = x1_ref[...] + gate * e
        gate_ref[...] = gate.astype(BF16)
        e_ref[...] = e.astype(BF16)

    row = lambda m: (m, 0)
    fix = lambda m: (0, 0)
    return _pcall(body, name=name, grid=(s // tm,),
                  in_specs=[pl.BlockSpec((tm, d), row), pl.BlockSpec((1, d, d), lambda m: (layer, 0, 0)),
                            pl.BlockSpec((1, d), fix), pl.BlockSpec((1, tm, pd), lambda m: (layer, m, 0)),
                            pl.BlockSpec((1, pd, d), lambda m: (layer, 0, 0)), pl.BlockSpec((tm, d), row)],
                  out_specs=[pl.BlockSpec((tm, d), row)] * 3,
                  out_shape=[jax.ShapeDtypeStruct((s, d), F32), jax.ShapeDtypeStruct((s, d), BF16),
                             jax.ShapeDtypeStruct((s, d), BF16)],
                  compiler_params=_cp(1))(hn, w_pg, b_pg, p, w_pe, x1)


def _loss_head(x, tgt, g, name):
    s, d = x.shape
    tm = _tile(s, 512)

    def body(x_ref, t_ref, g_ref, l_ref, dx_ref, dg_ref):
        m = pl.program_id(0)

        @pl.when(m == 0)
        def _():
            l_ref[...] = jnp.zeros_like(l_ref)
            dg_ref[...] = jnp.zeros_like(dg_ref)

        xv = x_ref[...]
        gv = g_ref[...]
        r = lax.rsqrt(jnp.mean(xv * xv, axis=-1, keepdims=True) + EPS)
        xn = xv * r
        err = xn * gv - t_ref[...]
        l_ref[...] += jnp.sum(err * err)
        dy = err * (1.0 / d)
        dxn = dy * gv
        dx_ref[...] = r * (dxn - xn * jnp.mean(dxn * xn, axis=-1, keepdims=True))
        dg_ref[...] += _colsum8(dy * xn)

    row = lambda m: (m, 0)
    fix = lambda m: (0, 0)
    return _pcall(body, name=name, grid=(s // tm,),
                  in_specs=[pl.BlockSpec((tm, d), row), pl.BlockSpec((tm, d), row), pl.BlockSpec((1, d), fix)],
                  out_specs=[pl.BlockSpec((8, LANES), fix), pl.BlockSpec((tm, d), row), pl.BlockSpec((8, d), fix)],
                  out_shape=[jax.ShapeDtypeStruct((8, LANES), F32), jax.ShapeDtypeStruct((s, d), F32),
                             jax.ShapeDtypeStruct((8, d), F32)],
                  compiler_params=_cp(1))(x, tgt, g)


def _ple_bwd(dx2, gate, e, x1, w_pg, g_ple, w_out, layer, name, riders=()):
    s, d = dx2.shape
    tm = _tile(s, 512)

    def body(*refs):
        m = pl.program_id(0)
        own, riders_end = _riders_run(riders, refs, 7, 6, 0, m, s // tm)
        (dx2_ref, gate_ref, e_ref, x1_ref, wg_ref, g_ref, wo_ref,
         du_ref, de_ref, dx1_ref, dy_ref, db_ref, dg_ref) = own

        @pl.when(m == 0)
        def _():
            db_ref[...] = jnp.zeros_like(db_ref)
            dg_ref[...] = jnp.zeros_like(dg_ref)

        dx2v = dx2_ref[...]
        gate = gate_ref[...].astype(F32)
        du = dx2v * e_ref[...].astype(F32) * gate * (1.0 - gate)
        de_ref[...] = (dx2v * gate).astype(BF16)
        dub = du.astype(BF16)
        du_ref[...] = dub
        db_ref[...] += _colsum8(du)
        dhn = _dot_nt(dub, wg_ref[0])
        dxr, dgr = _rms_bwd_rows(dhn, x1_ref[...], g_ref[...])
        dx1 = dx2v + dxr
        dx1_ref[...] = dx1
        dg_ref[...] += _colsum8(dgr)
        dy_ref[...] = _dot_nt(dx1.astype(BF16), wo_ref[0]).astype(BF16)
        riders_end()

    row = lambda m: (m, 0)
    fix = lambda m: (0, 0)
    t = pl.BlockSpec((tm, d), row)
    r_ops, r_shapes, r_scratch, r_aliases = _riders_plumb(riders, 7, 6)
    outs = _pcall(body, name=name, grid=(s // tm,),
                  in_specs=[t, t, t, t, pl.BlockSpec((1, d, d), lambda m: (layer, 0, 0)), pl.BlockSpec((1, d), fix),
                            pl.BlockSpec((1, d, d), lambda m: (layer, 0, 0))] + [ANY] * len(r_ops),
                  out_specs=[t, t, t, t, pl.BlockSpec((8, d), fix), pl.BlockSpec((8, d), fix)] + [ANY] * len(r_shapes),
                  out_shape=[jax.ShapeDtypeStruct((s, d), BF16), jax.ShapeDtypeStruct((s, d), BF16),
                             jax.ShapeDtypeStruct((s, d), F32), jax.ShapeDtypeStruct((s, d), BF16),
                             jax.ShapeDtypeStruct((8, d), F32), jax.ShapeDtypeStruct((8, d), F32)] + r_shapes,
                  input_output_aliases=r_aliases, scratch_shapes=r_scratch,
                  compiler_params=_cp(1))(dx2, gate, e, x1, w_pg, g_ple, w_out, *r_ops)
    return tuple(outs[:6]) + (list(outs[6:]),)


def _mm_tn(a, b, name, a_layer=None):
    s, ka = a.shape[-2:]
    n = b.shape[1]
    tn = _tile(n, 1024)
    ns = n // tn
    tk = _tile(s, 512)
    nk = s // tk

    def body(a_ref, b_ref, o_ref, acc_ref):
        k = pl.program_id(1)

        @pl.when(k == 0)
        def _():
            acc_ref[...] = jnp.zeros_like(acc_ref)

        av = a_ref[...] if a_layer is None else a_ref[0]
        acc_ref[...] += _dot_tn(av.astype(BF16), b_ref[...].astype(BF16))

        @pl.when(k == nk - 1)
        def _():
            o_ref[...] = acc_ref[...]

    a_spec = (pl.BlockSpec((tk, ka), lambda j, k: (k, 0)) if a_layer is None
              else pl.BlockSpec((1, tk, ka), lambda j, k: (a_layer, k, 0)))
    return _pcall(body, name=name, grid=(ns, nk),
                  in_specs=[a_spec, pl.BlockSpec((tk, tn), lambda j, k: (k, j))],
                  out_specs=pl.BlockSpec((ka, tn), lambda j, k: (0, j)),
                  out_shape=jax.ShapeDtypeStruct((ka, n), F32),
                  scratch_shapes=[pltpu.VMEM((ka, tn), F32)], compiler_params=_cp(2))(a, b)


def _norm_gate_bwd(dy, yv, zg, g, gm):
    r = lax.rsqrt(_group_mean(yv * yv, gm) + EPS)
    n = yv * r
    sg = _sigmoid(zg)
    sil = zg * sg
    dzg = dy * n * g * (sg * (1.0 + zg * (1.0 - sg)))
    dn = dy * g * sil
    dyv = r * (dn - n * _group_mean(dn * n, gm))
    return dyv, dzg, dy * n * sil


def _convmix_bwd(dy, proj, conv_w, conv_b, bg, name, riders=()):
    _, s, sw = proj.shape
    nh = sw // LANES
    tc = _tile(s, 256)
    nr = s // tc

    def body(*refs):
        own, riders_end = _riders_run(riders, refs, 8, 4, 1, pl.program_id(0), nh)
        (dy_ref, cb_ref, cc_ref, ch_ref, cz_ref, w_ref, b_ref, g_ref,
         dp_ref, dw_ref, db_ref, dg_ref, dcv_ref) = own
        gm = _group_mat()
        dcv_ref[pl.ds(s, 8), :] = jnp.zeros((8, LANES), F32)

        def pass1(r, carry):
            dw0, dw1, dw2, db, dg = carry
            r0, u, s1, s2, cv = _conv_rows(cc_ref, ch_ref, w_ref, b_ref, r, tc)
            cb = cb_ref[0, pl.ds(r0, tc), :].astype(F32)
            dyc, dcz, dgr = _norm_gate_bwd(dy_ref[pl.ds(r0, tc), :].astype(F32), cb * cv,
                                           cz_ref[0, pl.ds(r0, tc), :].astype(F32), g_ref[...], gm)
            dp_ref[0, pl.ds(r0, tc), :] = (dyc * cv).astype(BF16)
            dp_ref[3, pl.ds(r0, tc), :] = dcz.astype(BF16)
            dcv = dyc * cb
            dcv_ref[pl.ds(r0, tc), :] = dcv
            return (dw0 + _colsum8(dcv * s2), dw1 + _colsum8(dcv * s1), dw2 + _colsum8(dcv * u),
                    db + _colsum8(dcv), dg + _colsum8(dgr))

        z8 = jnp.zeros((8, LANES), F32)
        dw0, dw1, dw2, db, dg = lax.fori_loop(0, nr, pass1, (z8, z8, z8, z8, z8))
        dw_ref[0] = dw0
        dw_ref[1] = dw1
        dw_ref[2] = dw2
        db_ref[...] = db
        dg_ref[...] = dg

        def pass2(r, carry):
            r0 = pl.multiple_of(r * tc, tc)
            dcv = dcv_ref[pl.ds(r0, tc), :]
            nxt = dcv_ref[pl.ds(pl.multiple_of(r0 + tc, 8), 8), :]
            rid = lax.broadcasted_iota(jnp.int32, dcv.shape, 0)
            n1 = jnp.where(rid == tc - 1, nxt[0:1, :], pltpu.roll(dcv, tc - 1, axis=0))
            n2 = jnp.where(rid == tc - 1, nxt[1:2, :],
                           jnp.where(rid == tc - 2, nxt[0:1, :], pltpu.roll(dcv, tc - 2, axis=0)))
            du = dcv * w_ref[2:3, :] + n1 * w_ref[1:2, :] + n2 * w_ref[0:1, :]
            dp_ref[1, pl.ds(r0, tc), :] = (du * ch_ref[0, pl.ds(r0, tc), :].astype(F32)).astype(BF16)
            dp_ref[2, pl.ds(r0, tc), :] = (du * cc_ref[0, pl.ds(r0, tc), :].astype(F32)).astype(BF16)
            return carry

        lax.fori_loop(0, nr, pass2, 0)
        riders_end()

    def sec(k):
        return pl.BlockSpec((1, s, LANES), lambda c: (k, 0, c))

    col = lambda c: (0, c)
    r_ops, r_shapes, r_scratch, r_aliases = _riders_plumb(riders, 8, 4)
    outs = _pcall(
        body, name=name, grid=(nh,),
        in_specs=[pl.BlockSpec((s, LANES), col), sec(0), sec(1), sec(2), sec(3),
                  pl.BlockSpec((3, LANES), col), pl.BlockSpec((1, LANES), col), pl.BlockSpec((1, LANES), col)]
        + [ANY] * len(r_ops),
        out_specs=[pl.BlockSpec((4, s, LANES), lambda c: (0, 0, c)), pl.BlockSpec((3, 8, LANES), lambda c: (0, 0, c)),
                   pl.BlockSpec((8, LANES), col), pl.BlockSpec((8, LANES), col)] + [ANY] * len(r_shapes),
        out_shape=[jax.ShapeDtypeStruct((8, s, sw), BF16), jax.ShapeDtypeStruct((3, 8, sw), F32),
                   jax.ShapeDtypeStruct((8, sw), F32), jax.ShapeDtypeStruct((8, sw), F32)] + r_shapes,
        input_output_aliases=r_aliases,
        scratch_shapes=[pltpu.VMEM((s + 8, LANES), F32)] + r_scratch, compiler_params=_cp(1),
    )(dy, proj, proj, proj, proj, conv_w, conv_b, bg, *r_ops)
    return tuple(outs[:4]) + (list(outs[4:]),)


def _attn_bwd(proj, dy, ya, tl, walked, kept, bg, buf, name, riders=()):
    _, s, sw = proj.shape
    nhp = sw // LANES
    tk, t, nd, rows_c = _attn_tiles(s)
    nq = s // t
    scale = 1.0 / math.sqrt(HEAD)

    def body(*refs):
        step = pl.program_id(1)
        i = nq - 1 - step
        own, riders_end = _riders_run(riders, refs, 12, 2, 3, pl.program_id(0) * nq + step, nhp * nq)
        (q_ref, k_ref, v_ref, az_ref, dy_ref, ya_ref, tl_ref, nw_ref, g_ref, buf_ref, sa_ref, sb_ref, out_ref, dg_ref,
         dka_ref, dva_ref, dqa_ref) = own

        @pl.when(step == 0)
        def _():
            dka_ref[...] = jnp.zeros_like(dka_ref)
            dva_ref[...] = jnp.zeros_like(dva_ref)
            dg_ref[...] = jnp.zeros_like(dg_ref)

        dyv, dzg, dgr = _norm_gate_bwd(dy_ref[...].astype(F32), ya_ref[...].astype(F32), az_ref[0].astype(F32),
                                       g_ref[...], _group_mat())
        out_ref[3] = dzg.astype(BF16)
        dg_ref[...] += _colsum8(dgr)

        tri = (lax.broadcasted_iota(jnp.int32, (tk, tk), 0) <=
               lax.broadcasted_iota(jnp.int32, (tk, tk), 1)).astype(BF16)
        lane = lax.broadcasted_iota(jnp.int32, (t, LANES), 1)
        q = q_ref[0] * jnp.asarray(scale, BF16)
        do = dyv.astype(BF16)
        qms = [jnp.where((lane // HEAD) == h, q, jnp.zeros_like(q)) for h in range(2)]
        doms = [jnp.where((lane // HEAD) == h, do, jnp.zeros_like(do)) for h in range(2)]
        dqa_ref[...] = jnp.zeros_like(dqa_ref)
        chains = [(h, r0) for h in range(2) for r0 in range(0, t, rows_c)]
        qparts = [qms[h][r0:r0 + rows_c] for h, r0 in chains]
        doparts = [doms[h][r0:r0 + rows_c] for h, r0 in chains]
        tots = [tl_ref[h, r0:r0 + rows_c, :] for h, r0 in chains]

        def block(carry, tiles, items):
            k0s = [pl.multiple_of(j * tk, tk) for j in tiles]
            kjs = [k_ref[0, pl.ds(k0, tk), :] for k0 in k0s]
            vjs = [v_ref[0, pl.ds(k0, tk), :] for k0 in k0s]
            zs = [_dot_nt(qparts[n], kjs[t][:kw]) for n, t, kw, _, _ in items]
            das = [_dot_nt(doparts[n], vjs[t][:kw]) for n, t, kw, _, _ in items]
            keeps = [_both(mask, gate) for _, _, _, mask, gate in items]
            lms, lss, cls = [], [], []
            for z, keep, (n, t, kw, _, _) in zip(zs, keeps, items):
                lm, ls = _softplus_parts(z)
                if keep is not None:
                    lm = jnp.where(keep, lm, 0.0)
                lms.append(lm)
                lss.append(ls)
                cls.append(_split_dot(lm, tri[:kw, :kw], 2))
            cur = list(carry)
            psums, abs_, gs, cgs = [], [], [], []
            for lm, ls, cl, da, keep, (n, t, kw, _, _) in zip(lms, lss, cls, das, keeps, items):
                psum, gsum = cur[n]
                a = jnp.exp(ls + (tots[n] - psum - cl))
                if keep is not None:
                    a = jnp.where(keep, a, 0.0)
                g = a * da
                psums.append(gsum)
                gs.append(g)
                abs_.append(a.astype(BF16))
                cgs.append(_split_dot(g, tri[:kw, :kw], 1))
                cur[n] = (psum + jnp.sum(lm, axis=1, keepdims=True), gsum + jnp.sum(g, axis=1, keepdims=True))
            dks, dvs = {}, {}
            for ls, a, g, cg, gsum, keep, (n, t, kw, _, _) in zip(lss, abs_, gs, cgs, psums, keeps, items):
                h, r0 = chains[n]
                dz = g - jnp.exp(ls) * (gsum + cg)
                if keep is not None:
                    dz = jnp.where(keep, dz, 0.0)
                dz = dz.astype(BF16)
                dqa_ref[h, r0:r0 + rows_c, :] += jnp.dot(dz, kjs[t][:kw], preferred_element_type=F32)
                dkh = _dot_tn(dz, qparts[n])
                dvh = _dot_tn(a, doparts[n])
                dks[t, kw] = dkh if (t, kw) not in dks else dks[t, kw] + dkh
                dvs[t, kw] = dvh if (t, kw) not in dvs else dvs[t, kw] + dvh
            for t, kw in dks:
                dka_ref[pl.ds(k0s[t], kw), :] += dks[t, kw]
                dva_ref[pl.ds(k0s[t], kw), :] += dvs[t, kw]
            return tuple(cur)

        def kept_block(carry, tiles, items):
            k0s = [pl.multiple_of(j * tk, tk) for j in tiles]
            kjs = [k_ref[0, pl.ds(k0, tk), :] for k0 in k0s]
            vjs = [v_ref[0, pl.ds(k0, tk), :] for k0 in k0s]
            das = [_dot_nt(doparts[n], vjs[t][:kw]) for n, t, kw, _, _, _ in items]
            cur = list(carry)
            gsums, kept_a, gs, cgs = [], [], [], []
            for da, (n, t, kw, _, _, slot) in zip(das, items):
                psum, gsum = cur[n]
                a = sa_ref[0, slot, :, 0:kw]
                g = a.astype(F32) * da
                gsums.append(gsum)
                kept_a.append(a)
                gs.append(g)
                cgs.append(_split_dot(g, tri[:kw, :kw], 1))
                cur[n] = (psum, gsum + jnp.sum(g, axis=1, keepdims=True))
            dks, dvs = {}, {}
            for a, g, cg, gsum, (n, t, kw, mask, gate, slot) in zip(kept_a, gs, cgs, gsums, items):
                h, r0 = chains[n]
                dz = g - sb_ref[0, slot, :, 0:kw].astype(F32) * (gsum + cg)
                keep = _both(mask, gate)
                if keep is not None:
                    dz = jnp.where(keep, dz, 0.0)
                dz = dz.astype(BF16)
                dqa_ref[h, r0:r0 + rows_c, :] += jnp.dot(dz, kjs[t][:kw], preferred_element_type=F32)
                dkh = _dot_tn(dz, qparts[n])
                dvh = _dot_tn(a, doparts[n])
                dks[t, kw] = dkh if (t, kw) not in dks else dks[t, kw] + dkh
                dvs[t, kw] = dvh if (t, kw) not in dvs else dvs[t, kw] + dvh
            for t, kw in dks:
                dka_ref[pl.ds(k0s[t], kw), :] += dks[t, kw]
                dva_ref[pl.ds(k0s[t], kw), :] += dvs[t, kw]
            return tuple(cur)

        z1 = jnp.zeros((rows_c, 1), F32)
        everyone = [(n, 0, tk, None, None) for n in range(len(chains))]
        dslot, lslot, _ = _static_slots(chains, nd, rows_c, tk)
        upper = [n for n, (_, r0) in enumerate(chains) if r0 >= tk]
        lower = [n for n, (_, r0) in enumerate(chains) if r0 < tk]
        left = jnp.maximum(i * nd - 1, 0)
        code = jnp.clip(jnp.max(nw_ref[0].astype(jnp.int32)), 0, 2 * left + 1)
        too = jnp.where(i > 0, code % 2, 0)
        whole = jnp.minimum(code // 2, left)
        carry = lax.fori_loop(left - whole, left, lambda j, c: block(c, [j], everyone), ((z1, z1),) * len(chains))
        if upper:
            carry = lax.cond(too > 0, lambda c: block(c, [left], [(n, 0, tk, None, None) for n in upper]),
                             lambda c: c, carry)
        carry = kept_block(carry, [left, i * nd],
                           [(n, 0, tk, None, i > 0, lslot[n]) for n in lower]
                           + [(n, 1, kw, m, None, dslot[0, n]) for n, kw, m in _diag_work(chains, 0, rows_c, tk)])
        for d in range(1, nd):
            carry = kept_block(carry, [i * nd + d],
                               [(n, 0, kw, m, None, dslot[d, n]) for n, kw, m in _diag_work(chains, d, rows_c, tk)])
        out_ref[0] = (jnp.where(lane < HEAD, dqa_ref[0], dqa_ref[1]) * scale).astype(BF16)
        own = pl.multiple_of(i * t, t)
        out_ref[1] = dka_ref[pl.ds(own, t), :].astype(BF16)
        out_ref[2] = dva_ref[pl.ds(own, t), :].astype(BF16)
        riders_end()

    def rows(sec):
        return pl.BlockSpec((1, t, LANES), lambda hp, st: (sec, nq - 1 - st, hp))

    def whole(sec):
        return pl.BlockSpec((1, s, LANES), lambda hp, st: (sec, 0, hp))

    r_ops, r_shapes, r_scratch, r_aliases = _riders_plumb(riders, 12, 2)
    kept_spec = pl.BlockSpec((1,) + kept[0].shape[1:], lambda hp, st: (hp * nq + nq - 1 - st, 0, 0, 0))
    outs = _pcall(
        body, name=name, grid=(nhp, nq),
        in_specs=[rows(4), whole(5), whole(6), rows(7),
                  pl.BlockSpec((t, LANES), lambda hp, st: (nq - 1 - st, hp + nhp)),
                  pl.BlockSpec((t, LANES), lambda hp, st: (nq - 1 - st, hp)),
                  pl.BlockSpec((2, t, 1), lambda hp, st: (hp, nq - 1 - st, 0)),
                  pl.BlockSpec((1, 8, LANES), lambda hp, st: (hp * nq + nq - 1 - st, 0, 0)),
                  pl.BlockSpec((1, LANES), lambda hp, st: (0, hp + nhp)), ANY, kept_spec, kept_spec]
        + [ANY] * len(r_ops),
        out_specs=[pl.BlockSpec((4, t, LANES), lambda hp, st: (1, nq - 1 - st, hp)),
                   pl.BlockSpec((8, LANES), lambda hp, st: (0, hp))] + [ANY] * len(r_shapes),
        out_shape=[jax.ShapeDtypeStruct(buf.shape, buf.dtype), jax.ShapeDtypeStruct((8, sw), F32)] + r_shapes,
        input_output_aliases={9: 0, **r_aliases},
        scratch_shapes=[pltpu.VMEM((s, LANES), F32), pltpu.VMEM((s, LANES), F32), pltpu.VMEM((2, t, LANES), F32)]
        + r_scratch,
        compiler_params=_cp(2))(proj, proj, proj, proj, dy, ya, tl, walked, bg, buf, kept[0], kept[1], *r_ops)
    return outs[0], outs[1], list(outs[2:])


def _grad_w_in(h, dproj, name):
    s, d = h.shape
    ns, _, sw = dproj.shape

    def body(h_ref, b_ref, o_ref, ht_ref):
        @pl.when(pl.program_id(0) == 0)
        def _():
            ht_ref[...] = h_ref[...].T

        o_ref[...] = jnp.dot(ht_ref[...], b_ref[0], preferred_element_type=F32)

    return _pcall(body, name=name, grid=(ns,),
                  in_specs=[pl.BlockSpec((s, d), lambda j: (0, 0)), pl.BlockSpec((1, s, sw), lambda j: (j, 0, 0))],
                  out_specs=pl.BlockSpec((d, sw), lambda j: (0, j)),
                  out_shape=jax.ShapeDtypeStruct((d, ns * sw), F32),
                  scratch_shapes=[pltpu.VMEM((d, s), BF16)], compiler_params=_cp(1))(h, dproj)


def _inproj_bwd(dproj, w, layer, x, g, dx1, name, riders=()):
    ns, s, sw = dproj.shape
    d = x.shape[1]
    tm = _tile(s, 512)

    def body(*refs):
        own, riders_end = _riders_run(riders, refs, 5, 2, 0, pl.program_id(0), s // tm)
        dp_ref, w_ref, x_ref, g_ref, dx1_ref, dx_ref, dg_ref = own

        @pl.when(pl.program_id(0) == 0)
        def _():
            dg_ref[...] = jnp.zeros_like(dg_ref)

        dh = _dot_nt(dp_ref[0], w_ref[0, :, 0:sw])
        for k in range(1, ns):
            dh = dh + _dot_nt(dp_ref[k], w_ref[0, :, k * sw:(k + 1) * sw])
        dxr, dgr = _rms_bwd_rows(dh, x_ref[...], g_ref[...])
        dx_ref[...] = dx1_ref[...] + dxr
        dg_ref[...] += _colsum8(dgr)
        riders_end()

    row = lambda m: (m, 0)
    fix = lambda m: (0, 0)
    r_ops, r_shapes, r_scratch, r_aliases = _riders_plumb(riders, 5, 2)
    outs = _pcall(body, name=name, grid=(s // tm,),
                  in_specs=[pl.BlockSpec((ns, tm, sw), lambda m: (0, m, 0)),
                            pl.BlockSpec((1, d, ns * sw), lambda m: (layer, 0, 0)),
                            pl.BlockSpec((tm, d), row), pl.BlockSpec((1, d), fix), pl.BlockSpec((tm, d), row)]
                  + [ANY] * len(r_ops),
                  out_specs=[pl.BlockSpec((tm, d), row), pl.BlockSpec((8, d), fix)] + [ANY] * len(r_shapes),
                  out_shape=[jax.ShapeDtypeStruct((s, d), F32), jax.ShapeDtypeStruct((8, d), F32)] + r_shapes,
                  input_output_aliases=r_aliases, scratch_shapes=r_scratch,
                  compiler_params=_cp(1))(dproj, w, x, g, dx1, *r_ops)
    return outs[0], outs[1], list(outs[2:])


def _adamw(w, g, m, v, name):
    r, c = w.shape
    tr = _tile(r, 256)
    c1 = 1.0 - ADAM_B1 ** ADAM_STEP
    c2 = 1.0 - ADAM_B2 ** ADAM_STEP

    def body(w_ref, g_ref, m_ref, v_ref, go_ref, d_ref, mo_ref, vo_ref):
        gv = g_ref[...]
        go_ref[...] = gv
        mn = ADAM_B1 * m_ref[...] + (1.0 - ADAM_B1) * gv
        vn = ADAM_B2 * v_ref[...] + (1.0 - ADAM_B2) * (gv * gv)
        d_ref[...] = -ADAM_LR * ((mn / c1) / (jnp.sqrt(vn / c2) + ADAM_EPS) + ADAM_WD * w_ref[...])
        mo_ref[...] = mn
        vo_ref[...] = vn

    t = pl.BlockSpec((tr, c), lambda i: (i, 0))
    return _pcall(body, name=name, grid=(r // tr,), in_specs=[t] * 4, out_specs=[t] * 4,
                  out_shape=[jax.ShapeDtypeStruct((r, c), F32)] * 4, compiler_params=_cp(1))(w, g, m, v)


def _add_half(grad, other, core, a, name):
    hr, hc = other.shape
    tr = _tile(hr, 256)
    nb = hr // tr

    def body(c_ref, g_ref, o_ref, out_ref, outb_ref):
        v = g_ref[...] + o_ref[...]
        out_ref[...] = v
        outb_ref[...] = v.astype(BF16)

    t = pl.BlockSpec((tr, hc), lambda i, c: (i, 0))
    own = (lambda i, c: (c[0] * nb + i, 0)) if HALF_AXES[a] == 0 else (lambda i, c: (i, c[0]))
    grid_spec = pltpu.PrefetchScalarGridSpec(
        num_scalar_prefetch=1, grid=(nb,), in_specs=[pl.BlockSpec((tr, hc), own), t], out_specs=[t, t])
    return _pcall(body, name=name, grid_spec=grid_spec,
                  out_shape=[jax.ShapeDtypeStruct((hr, hc), F32), jax.ShapeDtypeStruct((hr, hc), BF16)],
                  compiler_params=_cp(1))(core.reshape(1).astype(jnp.int32), grad, other)


def _sum_half(wide, parts, chip, core, layer, a, stack, name):
    _, sr, sc = parts.shape
    tr = _tile(sr, 256)
    nbs = sr // tr

    def body(k_ref, f_ref, p_ref, *rest):
        rest[-1][0] = ((f_ref[...] + p_ref[0].astype(F32)) + p_ref[1].astype(F32)) + p_ref[2].astype(F32)

    f_map = (lambda i, k: (i, k[0])) if SHARD_AXES[a] == 1 else (lambda i, k: (k[0] * nbs + i, 0))
    if HALF_AXES[a] == 0:
        shape, o_map = (DEPTH, 2 * sr, sc), (lambda i, k: (layer, k[1] * nbs + i, 0))
    else:
        shape, o_map = (DEPTH, sr, 2 * sc), (lambda i, k: (layer, i, k[1]))
    in_specs = [pl.BlockSpec((tr, sc), f_map), pl.BlockSpec((3, tr, sc), lambda i, k: (0, i, 0))]
    args = [wide, parts]
    aliases = {}
    if stack is not None:
        in_specs.append(ANY)
        args.append(stack)
        aliases = {3: 0}
    grid_spec = pltpu.PrefetchScalarGridSpec(
        num_scalar_prefetch=1, grid=(nbs,), in_specs=in_specs, out_specs=pl.BlockSpec((1, tr, sc), o_map))
    return _pcall(body, name=name, grid_spec=grid_spec, out_shape=jax.ShapeDtypeStruct(shape, F32),
                  input_output_aliases=aliases,
                  compiler_params=_cp(1))(jnp.stack([chip, core]).astype(jnp.int32), *args)


def _sum_slots(slots, name):
    n = slots.shape[0]

    def body(s_ref, o_ref):
        acc = s_ref[0]
        for i in range(1, n):
            acc = acc + s_ref[i]
        o_ref[...] = acc

    return _pcall(body, name=name, out_shape=jax.ShapeDtypeStruct(slots.shape[1:], F32))(slots)


def _place():
    return lax.axis_index("x"), lax.axis_index("y"), lax.axis_index("c")


def _shard_view(ref, axis, chip, size):
    if axis == 0:
        return ref.at[pl.ds(chip * size, size), :]
    return ref.at[:, pl.ds(chip * size, size)]


SHARD_AXES = (1, 0, 0, 1)
HALF_AXES = tuple(1 - ax for ax in SHARD_AXES)


class _Rider:
    def __init__(self, operands, out_shape, sems, phases, aliased=False):
        self.operands, self.out_shape, self.sems = list(operands), list(out_shape), list(sems)
        self.phases, self.aliased = phases, aliased


def _riders_plumb(riders, n_in, n_out):
    ops, out_shape, scratch, aliases = [], [], [], {}
    for r in riders:
        if r.aliased:
            for k in range(len(r.operands)):
                aliases[n_in + len(ops) + k] = n_out + len(out_shape) + k
        ops += r.operands
        out_shape += r.out_shape
        scratch += r.sems
    return ops, out_shape, scratch, aliases


def _riders_run(riders, refs, n_in, n_out, n_scr, step, nsteps):
    n_rin = sum(len(r.operands) for r in riders)
    n_rout = sum(len(r.out_shape) for r in riders)
    rin = refs[n_in:n_in + n_rin]
    o0 = n_in + n_rin
    rout = refs[o0 + n_out:o0 + n_out + n_rout]
    s0 = o0 + n_out + n_rout
    rsem = refs[s0 + n_scr:]
    own = list(refs[:n_in]) + list(refs[o0:o0 + n_out]) + list(refs[s0:s0 + n_scr])
    lasts = []
    for r in riders:
        ph = r.phases(rin[:len(r.operands)], rout[:len(r.out_shape)], rsem[:len(r.sems)])
        rin, rout, rsem = rin[len(r.operands):], rout[len(r.out_shape):], rsem[len(r.sems):]
        pl.when(step == 0)(ph[0])
        for mid in ph[1:-1]:
            pl.when(step == (3 * nsteps) // 4)(mid)
        lasts.append(ph[-1])

    def finish():
        for last in lasts:
            pl.when(step == nsteps - 1)(last)

    return own, finish


def _gather_phases(ins, outs, ssem, rsem, layer, which):
    n = len(ins)
    x, y, c = _place()
    me = 2 * x + y
    chips = [(1 - x, y), (x, 1 - y), (1 - x, 1 - y)]

    def piece(a, chip, half, of):
        ax = SHARD_AXES[which[a]]
        block = _shard_view(of[a].at[layer], ax, chip, of[a].shape[1 + ax] // 4)
        r = block.shape[0] // 2
        return block.at[pl.ds(half * r, r), :]

    def over_ici(a, j):
        cx, cy = chips[j]
        return pltpu.make_async_remote_copy(
            src_ref=piece(a, me, c, ins), dst_ref=piece(a, me, c, outs), send_sem=ssem.at[a, j],
            recv_sem=rsem.at[a, j], device_id=(cx, cy, c), device_id_type=MESH)

    def landed(a, j, half):
        cx, cy = chips[j]
        return piece(a, 2 * cx + cy, half, outs)

    def to_sibling(a, j):
        got = landed(a, j, c)
        return pltpu.make_async_remote_copy(
            src_ref=got, dst_ref=got, send_sem=ssem.at[a, 3 + j], recv_sem=rsem.at[a, 3 + j],
            device_id=(x, y, 1 - c), device_id_type=MESH)

    def wait_arrival(a, k, place):
        pltpu.make_async_remote_copy(src_ref=place, dst_ref=place, send_sem=ssem.at[a, k], recv_sem=rsem.at[a, k],
                                     device_id=(x, y, c), device_id_type=MESH).wait_recv()

    def start():
        for a in range(n):
            for j in range(3):
                over_ici(a, j).start()

    def pass_on():
        for a in range(n):
            for j in range(3):
                wait_arrival(a, j, landed(a, j, c))
                to_sibling(a, j).start()

    def finish():
        for a in range(n):
            for j in range(3):
                wait_arrival(a, 3 + j, landed(a, j, 1 - c))
        for a in range(n):
            for j in range(3):
                over_ici(a, j).wait_send()
                to_sibling(a, j).wait_send()

    return start, pass_on, finish


def _gather_rider(fulls, layer, which):
    n = len(fulls)
    return _Rider(fulls, [jax.ShapeDtypeStruct(f.shape, f.dtype) for f in fulls],
                  [pltpu.SemaphoreType.DMA((n, 6)), pltpu.SemaphoreType.DMA((n, 6))],
                  lambda ins, outs, sems: _gather_phases(ins, outs, sems[0], sems[1], layer, which), aliased=True)


def _ride_alone(rider, name):
    n = len(rider.operands)

    def body(*refs):
        for phase in rider.phases(refs[:n], refs[n:n + len(rider.out_shape)], refs[n + len(rider.out_shape):]):
            phase()

    return _pcall(body, name=name, in_specs=[ANY] * n, out_specs=[ANY] * len(rider.out_shape),
                  out_shape=rider.out_shape, scratch_shapes=rider.sems,
                  input_output_aliases={a: a for a in range(n)} if rider.aliased else {})(*rider.operands)


def _half_view(ref, a, half):
    n = ref.shape[HALF_AXES[a]] // 2
    if HALF_AXES[a] == 0:
        return ref.at[pl.ds(half * n, n), :]
    return ref.at[:, pl.ds(half * n, n)]


def _swap_rider(grads, which):
    n = len(grads)
    halves = []
    for g, w in zip(grads, which):
        sh = list(g.shape)
        sh[HALF_AXES[w]] //= 2
        halves.append(jax.ShapeDtypeStruct(tuple(sh), g.dtype))

    def phases(srcs, outs, sems):
        x, y, c = _place()

        def copy(a):
            return pltpu.make_async_remote_copy(
                src_ref=_half_view(srcs[a], which[a], 1 - c), dst_ref=outs[a], send_sem=sems[0].at[a],
                recv_sem=sems[1].at[a], device_id=(x, y, 1 - c), device_id_type=MESH)

        def start():
            for a in range(n):
                copy(a).start()

        def finish():
            for a in range(n):
                copy(a).wait()

        return start, finish

    return _Rider(grads, halves, [pltpu.SemaphoreType.DMA((n,)), pltpu.SemaphoreType.DMA((n,))], phases)


def _scatter_rider(sums, which):
    n = len(sums)
    shapes = []
    for f, w in zip(sums, which):
        sh = list(f.shape)
        sh[SHARD_AXES[w]] //= 4
        shapes.append(jax.ShapeDtypeStruct((3,) + tuple(sh), f.dtype))

    def phases(srcs, outs, sems):
        x, y, c = _place()
        chips = [(1 - x, y), (x, 1 - y), (1 - x, 1 - y)]

        def copy(a, j):
            cx, cy = chips[j]
            ax = SHARD_AXES[which[a]]
            src = _shard_view(srcs[a], ax, 2 * cx + cy, srcs[a].shape[ax] // 4)
            return pltpu.make_async_remote_copy(src_ref=src, dst_ref=outs[a].at[j], send_sem=sems[0].at[a, j],
                                                recv_sem=sems[1].at[a, j], device_id=(cx, cy, c), device_id_type=MESH)

        def start():
            for a in range(n):
                for j in range(3):
                    copy(a, j).start()

        def finish():
            for a in range(n):
                for j in range(3):
                    copy(a, j).wait()

        return start, finish

    return _Rider(sums, shapes, [pltpu.SemaphoreType.DMA((n, 3)), pltpu.SemaphoreType.DMA((n, 3))], phases)


def _pair_halves(stacks):
    n = len(stacks)

    def body(*refs):
        ins, outs = refs[:n], refs[n:2 * n]
        ssem, rsem = refs[2 * n:]
        x, y, c = _place()
        cps = [pltpu.make_async_remote_copy(
            src_ref=_half_view(ins[a].at[l], a, c), dst_ref=_half_view(outs[a].at[l], a, c), send_sem=ssem.at[a, l],
            recv_sem=rsem.at[a, l], device_id=(x, y, 1 - c), device_id_type=MESH)
            for a in range(n) for l in range(DEPTH)]
        for cp in cps:
            cp.start()
        for a in range(n):
            for l in range(DEPTH):
                got = _half_view(outs[a].at[l], a, 1 - c)
                pltpu.make_async_remote_copy(src_ref=got, dst_ref=got, send_sem=ssem.at[a, l], recv_sem=rsem.at[a, l],
                                             device_id=(x, y, 1 - c), device_id_type=MESH).wait_recv()
        for cp in cps:
            cp.wait_send()

    return _pcall(body, name="pair_halves", in_specs=[ANY] * n, out_specs=[ANY] * n,
                  out_shape=[jax.ShapeDtypeStruct(st.shape, st.dtype) for st in stacks],
                  input_output_aliases={a: a for a in range(n)},
                  scratch_shapes=[pltpu.SemaphoreType.DMA((n, DEPTH)), pltpu.SemaphoreType.DMA((n, DEPTH))])(*stacks)


class _GradReduce:
    def __init__(self, chip, core):
        self.chip, self.core = chip, core
        self.stacks = [None] * len(SHARD_AXES)

    def add(self, layer, grads, which, got):
        return [(layer, w) + tuple(_add_half(g, o, self.core, w, f"add_half_{layer}_{w}"))
                for g, o, w in zip(grads, got, which)]

    def finish(self, sums, partials):
        for (layer, w, wide, _), pr in zip(sums, partials):
            self.stacks[w] = _sum_half(wide, pr, self.chip, self.core, layer, w, self.stacks[w], f"sum_half_{layer}_{w}")

    def result(self):
        return _pair_halves(self.stacks)


def _exchange_small(pack, name, riders=()):
    nd = 8

    def body(*refs):
        own, riders_end = _riders_run(riders, refs, 1, 1, 2, jnp.int32(0), 1)
        p_ref, o_ref, ssem, rsem = own
        x, y, c = _place()
        me = 4 * x + 2 * y + c
        o_ref[me] = p_ref[...]
        cps = []
        for j in range(1, nd):
            px, py, pc = x ^ (j >> 2), y ^ ((j >> 1) & 1), c ^ (j & 1)
            cps.append(pltpu.make_async_remote_copy(
                src_ref=p_ref, dst_ref=o_ref.at[me], send_sem=ssem.at[j - 1], recv_sem=rsem.at[j - 1],
                device_id=(px, py, pc), device_id_type=MESH))
        for cp in cps:
            cp.start()
        for j in range(1, nd):
            peer = me ^ j
            got = o_ref.at[peer]
            pltpu.make_async_remote_copy(src_ref=got, dst_ref=got, send_sem=ssem.at[j - 1], recv_sem=rsem.at[j - 1],
                                         device_id=(x, y, c), device_id_type=MESH).wait_recv()
        for cp in cps:
            cp.wait_send()
        riders_end()

    vm = pl.BlockSpec(memory_space=pltpu.VMEM)
    r_ops, r_shapes, r_scratch, r_aliases = _riders_plumb(riders, 1, 1)
    outs = _pcall(body, name=name, in_specs=[vm] + [ANY] * len(r_ops), out_specs=[vm] + [ANY] * len(r_shapes),
                  out_shape=[jax.ShapeDtypeStruct((nd,) + pack.shape, pack.dtype)] + r_shapes,
                  input_output_aliases=r_aliases,
                  scratch_shapes=[pltpu.SemaphoreType.DMA((nd - 1,)), pltpu.SemaphoreType.DMA((nd - 1,))]
                  + r_scratch)(pack, *r_ops)
    return (outs[0], list(outs[1:])) if riders else outs[0]


def _row(v):
    return v.reshape(1, -1)


def _local_step(x, p, tgt, norm_g, conv_w, conv_b, branch_g, ple_norm_g, b_pg, final_g, w_in, w_out, w_pg, w_pe,
                gather=False, reduce=None):
    saved = []
    xl = x
    for l in range(DEPTH):
        riders = [_gather_rider([w_out, w_pg, w_pe], 0, [1, 2, 3])] if gather and l == 0 else []
        h, proj, got = _inproj(xl, _row(norm_g[l]), w_in, l, f"inproj_{l}", riders)
        if riders:
            w_out, w_pg, w_pe = got
        later = gather and l + 1 < DEPTH
        riders = [_gather_rider([w_in, w_pg, w_pe], l + 1, [0, 2, 3])] if later else []
        ya, tl, walked, kept, got = _attn_fwd(proj, f"attn_fwd_{l}", riders)
        if riders:
            w_in, w_pg, w_pe = got
        riders = [_gather_rider([w_out], l + 1, [1])] if later else []
        y, got = _mix_fwd(proj, ya, conv_w[l], _row(conv_b[l]), _row(branch_g[l]), f"mix_fwd_{l}", riders)
        if riders:
            w_out, = got
        x1, hn = _outproj(y, w_out, l, xl, _row(ple_norm_g[l]), f"outproj_{l}")
        x2, gate, e = _ple_fwd(hn, w_pg, _row(b_pg[l]), p, w_pe, l, x1, f"ple_fwd_{l}")
        saved.append((xl, h, proj, ya, tl, walked, kept, y, x1, hn, gate, e))
        xl = x2

    sq, dx, d_final = _loss_head(xl, tgt, _row(final_g), "loss_head")

    big = [None] * DEPTH
    carried = []
    small = {k: [None] * DEPTH for k in ("norm_g", "conv_w", "conv_b", "branch_g", "ple_norm_g", "b_pg")}
    for l in reversed(range(DEPTH)):
        xl, h, proj, ya, tl, walked, kept, y, x1, hn, gate, e = saved[l]
        du, de, dx1, dy, db_pg, d_ple, _ = _ple_bwd(dx, gate, e, x1, w_pg, _row(ple_norm_g[l]), w_out, l,
                                                    f"ple_bwd_{l}")
        sums = carried
        g_pg = _mm_tn(hn, du, f"grad_w_pg_{l}")
        g_pe = _mm_tn(p, de, f"grad_w_pe_{l}", a_layer=l)
        g_out = _mm_tn(y, dx1, f"grad_w_out_{l}")
        others = [g_out, g_pg, g_pe]
        riders = [_swap_rider(others, [1, 2, 3])] if reduce is not None else []
        dpc, d_cw, d_cb, d_bg_c, got = _convmix_bwd(dy, proj, conv_w[l], _row(conv_b[l]), _row(branch_g[l]),
                                                    f"convmix_bwd_{l}", riders)
        if reduce is not None:
            sums += reduce.add(l, others, [1, 2, 3], got)
        riders = [_scatter_rider([sm[3] for sm in sums], [sm[1] for sm in sums])] if sums else []
        dproj, d_bg_a, got = _attn_bwd(proj, dy, ya, tl, walked, kept, _row(branch_g[l]), dpc, f"attn_bwd_{l}", riders)
        if sums:
            reduce.finish(sums, got)
        g_in = _grad_w_in(h, dproj, f"grad_w_in_{l}")
        if reduce is None:
            dx, d_norm, _ = _inproj_bwd(dproj, w_in, l, xl, _row(norm_g[l]), dx1, f"inproj_bwd_{l}")
        elif l > 0:
            dx, d_norm, got = _inproj_bwd(dproj, w_in, l, xl, _row(norm_g[l]), dx1, f"inproj_bwd_{l}",
                                          [_swap_rider([g_in], [0])])
            carried = reduce.add(l, [g_in], [0], got)
        else:
            sums = reduce.add(l, [g_in], [0], _ride_alone(_swap_rider([g_in], [0]), "swap_halves_last"))
            dx, d_norm, got = _inproj_bwd(dproj, w_in, l, xl, _row(norm_g[l]), dx1, f"inproj_bwd_{l}",
                                          [_scatter_rider([sm[3] for sm in sums], [0])])
            reduce.finish(sums, got)
        big[l] = (g_in, g_out, g_pg, g_pe)
        small["norm_g"][l] = jnp.sum(d_norm, axis=0)
        small["conv_w"][l] = jnp.sum(d_cw, axis=1)
        small["conv_b"][l] = jnp.sum(d_cb, axis=0)
        small["branch_g"][l] = jnp.concatenate([jnp.sum(d_bg_c, axis=0), jnp.sum(d_bg_a, axis=0)])
        small["ple_norm_g"][l] = jnp.sum(d_ple, axis=0)
        small["b_pg"][l] = jnp.sum(db_pg, axis=0)
    small = {k: jnp.stack(v) for k, v in small.items()}
    small["final_g"] = jnp.sum(d_final, axis=0)
    return sq[0, 0], dx, big, small


SMALL_ORDER = ("norm_g", "conv_w", "conv_b", "branch_g", "ple_norm_g", "b_pg", "final_g")


def _pack(parts, width):
    flat = jnp.concatenate([v.reshape(-1) for v in parts])
    rows = -(-flat.shape[0] // width)
    rows = -(-rows // 8) * 8
    return jnp.pad(flat, (0, rows * width - flat.shape[0])).reshape(rows, width)


def _unpack(packed, like):
    flat = packed.reshape(-1)
    out, off = [], 0
    for v in like:
        out.append(flat[off:off + v.size].reshape(v.shape))
        off += v.size
    return out


def kernel(x, p, norm_g, w_in, conv_w, conv_b, branch_g, w_out, ple_norm_g, w_pg, b_pg, w_pe, final_g, loss_target, m_norm_g, m_w_in, m_conv_w, m_conv_b, m_branch_g, m_w_out, m_ple_norm_g, m_w_pg, m_b_pg, m_w_pe, m_final_g, v_norm_g, v_w_in, v_conv_w, v_conv_b, v_branch_g, v_w_out, v_ple_norm_g, v_w_pg, v_b_pg, v_w_pe, v_final_g):
    ix, iy, ic = _place()
    chip = 2 * ix + iy
    d = x.shape[-1]

    big_w = (w_in, w_out, w_pg, w_pe)
    own = [_cast_into_full(w, chip, ax, f"cast_{i}") for i, (w, ax) in enumerate(zip(big_w, SHARD_AXES))]
    full_in, = _ride_alone(_gather_rider([own[0]], 0, [0]), "gather_w_in_0")
    full_out, full_pg, full_pe = own[1:]
    cw_shard = conv_w.shape[-1]
    cw_slots = _exchange_small(_pack([conv_w], LANES), "exchange_conv_w")
    conv_full = jnp.concatenate([_unpack(cw_slots[2 * k], [conv_w])[0] for k in range(4)], axis=-1)

    reduce = _GradReduce(chip, ic)
    sq, dx, _, small_g = _local_step(
        x[0], p[:, 0], loss_target[0], norm_g, conv_full, conv_b, branch_g, ple_norm_g, b_pg, final_g,
        full_in, full_out, full_pg, full_pe, gather=True, reduce=reduce)

    g_big = reduce.result()

    parts = [small_g[k] for k in SMALL_ORDER] + [sq.reshape(1)]
    slots = _exchange_small(_pack(parts, d), "exchange_small_grads")
    total = _unpack(_sum_slots(slots, "sum_small"), parts)
    g_small = dict(zip(SMALL_ORDER, total[:-1]))
    loss = 0.5 * total[-1][0] / d
    g_small["conv_w"] = lax.dynamic_slice_in_dim(g_small["conv_w"], chip * cw_shard, cw_shard, axis=2)

    grads = dict(g_small)
    grads.update(w_in=g_big[0], w_out=g_big[1], w_pg=g_big[2], w_pe=g_big[3])
    weights = dict(norm_g=norm_g, w_in=w_in, conv_w=conv_w, conv_b=conv_b, branch_g=branch_g, w_out=w_out,
                   ple_norm_g=ple_norm_g, w_pg=w_pg, b_pg=b_pg, w_pe=w_pe, final_g=final_g)
    ms = dict(norm_g=m_norm_g, w_in=m_w_in, conv_w=m_conv_w, conv_b=m_conv_b, branch_g=m_branch_g, w_out=m_w_out,
              ple_norm_g=m_ple_norm_g, w_pg=m_w_pg, b_pg=m_b_pg, w_pe=m_w_pe, final_g=m_final_g)
    vs = dict(norm_g=v_norm_g, w_in=v_w_in, conv_w=v_conv_w, conv_b=v_conv_b, branch_g=v_branch_g, w_out=v_w_out,
              ple_norm_g=v_ple_norm_g, w_pg=v_w_pg, b_pg=v_b_pg, w_pe=v_w_pe, final_g=v_final_g)
    names = ("norm_g", "w_in", "conv_w", "conv_b", "branch_g", "w_out", "ple_norm_g", "w_pg", "b_pg", "w_pe", "final_g")
    delta, new_m, new_v = {}, {}, {}
    for k in ("w_in", "w_out", "w_pg", "w_pe"):
        shp = weights[k].shape
        two = lambda a: a.reshape(-1, shp[-1])
        gr, dl, mn, vn = _adamw(two(weights[k]), two(grads[k]), two(ms[k]), two(vs[k]), f"adamw_{k}")
        delta[k], new_m[k], new_v[k] = dl.reshape(shp), mn.reshape(shp), vn.reshape(shp)
        grads[k] = gr.reshape(shp)
    like = [weights[k] for k in SMALL_ORDER]
    packs = [_pack([src[k] for k in SMALL_ORDER], d) for src in (weights, grads, ms, vs)]
    outs = _adamw(*packs, "adamw_small")
    for res, o in zip((delta, new_m, new_v), outs[1:]):
        res.update(dict(zip(SMALL_ORDER, _unpack(o, like))))

    return (loss, dx[None], *[grads[k] for k in names], *[delta[k] for k in names],
            *[new_m[k] for k in names], *[new_v[k] for k in names])
```

```python
import math

import jax
import jax.numpy as jnp
from jax import lax
from jax.experimental import pallas as pl
from jax.experimental.pallas import tpu as pltpu

F32 = jnp.float32
BF16 = jnp.bfloat16
EPS = 1e-6
HEAD = 64
LANES = 128
ATT_TK = 256
ATT_TQ = 512
ATT_ROWS = 128
ALIVE_LOG = -105.0
DEPTH = 2
VMEM_LIMIT = 56 * 1024 * 1024
MESH = pl.DeviceIdType.MESH
ANY = pl.BlockSpec(memory_space=pl.ANY)

ADAM_LR = 0.001
ADAM_B1 = 0.9
ADAM_B2 = 0.999
ADAM_EPS = 1e-08
ADAM_WD = 0.01
ADAM_STEP = 10


def _pcall(body, **kw):
    return pl.pallas_call(body, **kw)


def _cp(n_axes):
    return pltpu.CompilerParams(dimension_semantics=("arbitrary",) * n_axes, vmem_limit_bytes=VMEM_LIMIT)


def _tile(n, pref):
    return pref if n % pref == 0 else n


def _split_dot(a, b, passes):
    out = None
    rem = a
    for _ in range(passes):
        hi = rem.astype(BF16)
        t = jnp.dot(hi, b, preferred_element_type=F32)
        out = t if out is None else out + t
        rem = rem - hi.astype(F32)
    return out


def _group_mat():
    r = lax.broadcasted_iota(jnp.int32, (LANES, LANES), 0) // HEAD
    c = lax.broadcasted_iota(jnp.int32, (LANES, LANES), 1) // HEAD
    return jnp.where(r == c, 1.0 / HEAD, 0.0).astype(BF16)


def _group_mean(v, gm):
    return _split_dot(v, gm, 1)


def _sigmoid(z):
    return 1.0 / (1.0 + jnp.exp(-z))


def _dot_nt(a, b):
    return lax.dot_general(a, b, (((1,), (1,)), ((), ())), preferred_element_type=F32)


def _dot_tn(a, b):
    return lax.dot_general(a, b, (((0,), (0,)), ((), ())), preferred_element_type=F32)


def _cast_into_full(w, chip, axis, name):
    _, r, c = w.shape
    tr = _tile(r, 256)
    nb = r // tr
    full = (DEPTH, 4 * r, c) if axis == 0 else (DEPTH, r, 4 * c)

    def body(k_ref, w_ref, o_ref):
        o_ref[...] = w_ref[...].astype(BF16)

    out_map = (lambda l, i, k: (l, k[0] * nb + i, 0)) if axis == 0 else (lambda l, i, k: (l, i, k[0]))
    grid_spec = pltpu.PrefetchScalarGridSpec(
        num_scalar_prefetch=1, grid=(DEPTH, nb),
        in_specs=[pl.BlockSpec((1, tr, c), lambda l, i, k: (l, i, 0))],
        out_specs=pl.BlockSpec((1, tr, c), out_map))
    return _pcall(body, name=name, grid_spec=grid_spec, out_shape=jax.ShapeDtypeStruct(full, BF16),
                  compiler_params=_cp(2))(chip.reshape(1).astype(jnp.int32), w)


def _rms_bwd_rows(dh, xv, g):
    r = lax.rsqrt(jnp.mean(xv * xv, axis=-1, keepdims=True) + EPS)
    xn = xv * r
    dxn = dh * g
    dx = r * (dxn - xn * jnp.mean(dxn * xn, axis=-1, keepdims=True))
    return dx, dh * xn


def _colsum8(v):
    tm, d = v.shape
    return jnp.sum(v.reshape(tm // 8, 8, d), axis=0)


def _inproj(x, g, w, layer, name, riders=()):
    s, d = x.shape
    n = w.shape[2]
    sw = d // 2
    ns = n // sw
    tm = _tile(s, 512)

    def body(*refs):
        own, riders_end = _riders_run(riders, refs, 3, 2, 0, pl.program_id(0), s // tm)
        x_ref, g_ref, w_ref, h_ref, o_ref = own
        xv = x_ref[...]
        r = lax.rsqrt(jnp.mean(xv * xv, axis=-1, keepdims=True) + EPS)
        h = (xv * r * g_ref[...]).astype(BF16)
        h_ref[...] = h
        for k in range(ns):
            o_ref[k] = jnp.dot(h, w_ref[0, :, k * sw:(k + 1) * sw], preferred_element_type=F32).astype(BF16)
        riders_end()

    r_ops, r_shapes, r_scratch, r_aliases = _riders_plumb(riders, 3, 2)
    outs = _pcall(body, name=name, grid=(s // tm,),
                  in_specs=[pl.BlockSpec((tm, d), lambda m: (m, 0)), pl.BlockSpec((1, d), lambda m: (0, 0)),
                            pl.BlockSpec((1, d, n), lambda m: (layer, 0, 0))] + [ANY] * len(r_ops),
                  out_specs=[pl.BlockSpec((tm, d), lambda m: (m, 0)), pl.BlockSpec((ns, tm, sw), lambda m: (0, m, 0))]
                  + [ANY] * len(r_shapes),
                  out_shape=[jax.ShapeDtypeStruct((s, d), BF16), jax.ShapeDtypeStruct((ns, s, sw), BF16)] + r_shapes,
                  input_output_aliases=r_aliases, scratch_shapes=r_scratch,
                  compiler_params=_cp(1))(x, g, w, *r_ops)
    return outs[0], outs[1], list(outs[2:])


def _softplus_parts(z):
    lm = jnp.minimum(-z, 0.0) - jnp.log(1.0 + jnp.exp(-jnp.abs(z)))
    return lm, lm + z


def _attn_tiles(s):
    tk = _tile(s, ATT_TK)
    tq = _tile(s, ATT_TQ)
    return tk, tq, tq // tk, min(ATT_ROWS, tq)


def _diag_work(chains, d, rows, tk):
    work = []
    for n, (_, r0) in enumerate(chains):
        if r0 + rows - 1 <= d * tk:
            continue
        kw = tk // 2 if (tk % 2 == 0 and r0 + rows <= d * tk + tk // 2) else tk
        if r0 >= d * tk + kw:
            mask = None
        else:
            row = lax.broadcasted_iota(jnp.int32, (rows, kw), 0)
            col = lax.broadcasted_iota(jnp.int32, (rows, kw), 1)
            mask = col + d * tk < row + r0
        work.append((n, kw, mask))
    return work


def _static_slots(chains, nd, rows, tk):
    diag, left = {}, {}
    for d in range(nd):
        for n, (_, r0) in enumerate(chains):
            if r0 + rows - 1 > d * tk:
                diag[d, n] = len(diag)
    for n, (_, r0) in enumerate(chains):
        if r0 < tk:
            left[n] = len(diag) + len(left)
    return diag, left, len(diag) + len(left)


def _both(mask, gate):
    if mask is None:
        return gate
    if gate is None:
        return mask
    return jnp.logical_and(mask, gate)


def _any_alive(rsums):
    m = rsums[0]
    for r in rsums[1:]:
        m = jnp.maximum(m, r)
    return jnp.max((m > ALIVE_LOG).astype(jnp.int32))


def _attn_fwd(proj, name, riders=()):
    _, s, sw = proj.shape
    nhp = sw // LANES
    tk, tq, nd, rows = _attn_tiles(s)
    nq = s // tq
    scale = 1.0 / math.sqrt(HEAD)

    def body(*refs):
        i = pl.program_id(1)
        own, riders_end = _riders_run(riders, refs, 3, 5, 1, pl.program_id(0) * nq + i, nhp * nq)
        q_ref, k_ref, v_ref, o_ref, tl_ref, nw_ref, sa_ref, sb_ref, acc_ref = own
        tri = (lax.broadcasted_iota(jnp.int32, (tk, tk), 0) >
               lax.broadcasted_iota(jnp.int32, (tk, tk), 1)).astype(BF16)
        lane = lax.broadcasted_iota(jnp.int32, (tq, LANES), 1)
        q = q_ref[0] * jnp.asarray(scale, BF16)
        qms = [jnp.where((lane // HEAD) == h, q, jnp.zeros_like(q)) for h in range(2)]
        acc_ref[...] = jnp.zeros_like(acc_ref)
        chains = [(h, r0) for h in range(2) for r0 in range(0, tq, rows)]
        qparts = [qms[h][r0:r0 + rows] for h, r0 in chains]

        def block(rsums, tiles, items):
            kjs = [k_ref[0, pl.ds(pl.multiple_of(j * tk, tk), tk), :] for j in tiles]
            vjs = [v_ref[0, pl.ds(pl.multiple_of(j * tk, tk), tk), :] for j in tiles]
            zs = [_dot_nt(qparts[n], kjs[t][:kw]) for n, t, kw, _, _, _ in items]
            lms, lss, css = [], [], []
            for z, (n, t, kw, mask, gate, _) in zip(zs, items):
                lm, ls = _softplus_parts(z)
                keep = _both(mask, gate)
                if keep is not None:
                    lm = jnp.where(keep, lm, 0.0)
                lms.append(lm)
                lss.append(ls)
                css.append(_split_dot(lm, tri[:kw, :kw], 2))
            cur = list(rsums)
            for lm, ls, cs, (n, t, kw, mask, gate, slot) in zip(lms, lss, css, items):
                h, r0 = chains[n]
                a = jnp.exp(ls + (cur[n] + cs))
                keep = _both(mask, gate)
                if keep is not None:
                    a = jnp.where(keep, a, 0.0)
                if slot is not None:
                    sa_ref[0, slot, :, 0:kw] = a.astype(BF16)
                    sb_ref[0, slot, :, 0:kw] = jnp.exp(ls).astype(BF16)
                acc_ref[h, r0:r0 + rows, :] += jnp.dot(a.astype(BF16), vjs[t][:kw], preferred_element_type=F32)
                cur[n] = cur[n] + jnp.sum(lm, axis=1, keepdims=True)
            return tuple(cur)

        everyone = [(n, 0, tk, None, None, None) for n in range(len(chains))]
        dslot, lslot, _ = _static_slots(chains, nd, rows, tk)
        upper = [n for n, (_, r0) in enumerate(chains) if r0 >= tk]
        lower = [n for n, (_, r0) in enumerate(chains) if r0 < tk]
        left = jnp.maximum(i * nd - 1, 0)
        rsums = (jnp.zeros((rows, 1), F32),) * len(chains)
        for d in reversed(range(1, nd)):
            rsums = block(rsums, [i * nd + d],
                          [(n, 0, kw, m, None, dslot[d, n]) for n, kw, m in _diag_work(chains, d, rows, tk)])
        rsums = block(rsums, [i * nd, left],
                      [(n, 0, kw, m, None, dslot[0, n]) for n, kw, m in _diag_work(chains, 0, rows, tk)]
                      + [(n, 1, tk, None, i > 0, lslot[n]) for n in lower])

        if upper:
            too = (i > 0) & (_any_alive([rsums[n] for n in upper]) > 0)
            rsums = lax.cond(too, lambda rs: block(rs, [left], [(n, 0, tk, None, None, None) for n in upper]),
                             lambda rs: rs, rsums)
            too = too.astype(jnp.int32)
        else:
            too = jnp.int32(0)

        def walk(c):
            jj, rs, _ = c
            rs = block(rs, [i * nd - 2 - jj], everyone)
            return jj + 1, rs, _any_alive(rs)

        whole, rsums, _ = lax.while_loop(lambda c: (c[0] < i * nd - 1) & (c[2] > 0), walk,
                                         (jnp.int32(0), rsums, _any_alive(rsums)))
        for n, (h, r0) in enumerate(chains):
            tl_ref[h, r0:r0 + rows, :] = rsums[n]
        nw_ref[0] = (jnp.zeros((8, LANES), jnp.int32) + (2 * whole + too)).astype(F32)
        o_ref[...] = jnp.where(lane < HEAD, acc_ref[0], acc_ref[1]).astype(BF16)
        riders_end()

    r_ops, r_shapes, r_scratch, r_aliases = _riders_plumb(riders, 3, 5)
    nslots = _static_slots([(h, r0) for h in range(2) for r0 in range(0, tq, rows)], nd, rows, tk)[2]
    kept = pl.BlockSpec((1, nslots, rows, tk), lambda hp, i: (hp * nq + i, 0, 0, 0))
    outs = _pcall(
        body, name=name, grid=(nhp, nq),
        in_specs=[pl.BlockSpec((1, tq, LANES), lambda hp, i: (4, i, hp)),
                  pl.BlockSpec((1, s, LANES), lambda hp, i: (5, 0, hp)),
                  pl.BlockSpec((1, s, LANES), lambda hp, i: (6, 0, hp))] + [ANY] * len(r_ops),
        out_specs=[pl.BlockSpec((tq, LANES), lambda hp, i: (i, hp)),
                   pl.BlockSpec((2, tq, 1), lambda hp, i: (hp, i, 0)),
                   pl.BlockSpec((1, 8, LANES), lambda hp, i: (hp * nq + i, 0, 0)), kept, kept]
        + [ANY] * len(r_shapes),
        out_shape=[jax.ShapeDtypeStruct((s, sw), BF16), jax.ShapeDtypeStruct((2 * nhp, s, 1), F32),
                   jax.ShapeDtypeStruct((nhp * nq, 8, LANES), F32)]
        + [jax.ShapeDtypeStruct((nhp * nq, nslots, rows, tk), BF16)] * 2 + r_shapes,
        input_output_aliases=r_aliases,
        scratch_shapes=[pltpu.VMEM((2, tq, LANES), F32)] + r_scratch,
        compiler_params=_cp(2))(proj, proj, proj, *r_ops)
    return outs[0], outs[1], outs[2], (outs[3], outs[4]), list(outs[5:])


def _conv_rows(cc_ref, ch_ref, w_ref, b_ref, r, tc):
    r0 = pl.multiple_of(r * tc, tc)
    u = cc_ref[0, pl.ds(r0, tc), :].astype(F32) * ch_ref[0, pl.ds(r0, tc), :].astype(F32)
    p0 = pl.multiple_of(jnp.maximum(r0 - 16, 0), 16)
    up = cc_ref[0, pl.ds(p0, 16), :].astype(F32) * ch_ref[0, pl.ds(p0, 16), :].astype(F32)
    up = up * (r > 0).astype(F32)
    prev1 = up[15:16, :]
    prev2 = up[14:15, :]
    rid = lax.broadcasted_iota(jnp.int32, u.shape, 0)
    s1 = jnp.where(rid == 0, prev1, pltpu.roll(u, 1, axis=0))
    s2 = jnp.where(rid == 0, prev2, jnp.where(rid == 1, prev1, pltpu.roll(u, 2, axis=0)))
    cv = b_ref[...] + s2 * w_ref[0:1, :] + s1 * w_ref[1:2, :] + u * w_ref[2:3, :]
    return r0, u, s1, s2, cv


def _mix_fwd(proj, ya, conv_w, conv_b, bg, name, riders=()):
    _, s, sw = proj.shape
    nh = sw // LANES
    tc = _tile(s, 256)

    def body(*refs):
        c = pl.program_id(0)
        own, riders_end = _riders_run(riders, refs, 9, 1, 0, c, 2 * nh)
        cb_ref, cc_ref, ch_ref, cz_ref, ya_ref, az_ref, w_ref, b_ref, g_ref, y_ref = own
        gm = _group_mat()

        def finish(r0, yv, zg):
            n = yv * lax.rsqrt(_group_mean(yv * yv, gm) + EPS)
            y_ref[pl.ds(r0, tc), :] = (n * g_ref[...] * (zg * _sigmoid(zg))).astype(BF16)

        @pl.when(c < nh)
        def _():
            def step(r, carry):
                r0, _, _, _, cv = _conv_rows(cc_ref, ch_ref, w_ref, b_ref, r, tc)
                yc = cb_ref[0, pl.ds(r0, tc), :].astype(F32) * cv
                finish(r0, yc, cz_ref[0, pl.ds(r0, tc), :].astype(F32))
                return carry
            lax.fori_loop(0, s // tc, step, 0)

        @pl.when(c >= nh)
        def _():
            def step(r, carry):
                r0 = pl.multiple_of(r * tc, tc)
                finish(r0, ya_ref[pl.ds(r0, tc), :].astype(F32), az_ref[0, pl.ds(r0, tc), :].astype(F32))
                return carry
            lax.fori_loop(0, s // tc, step, 0)

        riders_end()

    def sec(k):
        return pl.BlockSpec((1, s, LANES), lambda c: (k, 0, jnp.minimum(c, nh - 1)))

    r_ops, r_shapes, r_scratch, r_aliases = _riders_plumb(riders, 9, 1)
    outs = _pcall(
        body, name=name, grid=(2 * nh,),
        in_specs=[sec(0), sec(1), sec(2), sec(3),
                  pl.BlockSpec((s, LANES), lambda c: (0, jnp.maximum(c - nh, 0))),
                  pl.BlockSpec((1, s, LANES), lambda c: (7, 0, jnp.maximum(c - nh, 0))),
                  pl.BlockSpec((3, LANES), lambda c: (0, jnp.minimum(c, nh - 1))),
                  pl.BlockSpec((1, LANES), lambda c: (0, jnp.minimum(c, nh - 1))),
                  pl.BlockSpec((1, LANES), lambda c: (0, c))] + [ANY] * len(r_ops),
        out_specs=[pl.BlockSpec((s, LANES), lambda c: (0, c))] + [ANY] * len(r_shapes),
        out_shape=[jax.ShapeDtypeStruct((s, 2 * sw), BF16)] + r_shapes,
        input_output_aliases=r_aliases, scratch_shapes=r_scratch, compiler_params=_cp(1),
    )(proj, proj, proj, proj, ya, proj, conv_w, conv_b, bg, *r_ops)
    return outs[0], list(outs[1:])


def _outproj(y, w, layer, x, g, name):
    s, d = x.shape
    tm = _tile(s, 512)

    def body(y_ref, w_ref, x_ref, g_ref, x1_ref, hn_ref):
        x1 = x_ref[...] + jnp.dot(y_ref[...], w_ref[0], preferred_element_type=F32)
        x1_ref[...] = x1
        r = lax.rsqrt(jnp.mean(x1 * x1, axis=-1, keepdims=True) + EPS)
        hn_ref[...] = (x1 * r * g_ref[...]).astype(BF16)

    row = lambda m: (m, 0)
    fix = lambda m: (0, 0)
    return _pcall(body, name=name, grid=(s // tm,),
                  in_specs=[pl.BlockSpec((tm, d), row), pl.BlockSpec((1, d, d), lambda m: (layer, 0, 0)),
                            pl.BlockSpec((tm, d), row), pl.BlockSpec((1, d), fix)],
                  out_specs=[pl.BlockSpec((tm, d), row), pl.BlockSpec((tm, d), row)],
                  out_shape=[jax.ShapeDtypeStruct((s, d), F32), jax.ShapeDtypeStruct((s, d), BF16)],
                  compiler_params=_cp(1))(y, w, x, g)


def _ple_fwd(hn, w_pg, b_pg, p, w_pe, layer, x1, name):
    s, d = x1.shape
    pd = p.shape[2]
    tm = _tile(s, 512)

    def body(hn_ref, wg_ref, b_ref, p_ref, we_ref, x1_ref, x2_ref, gate_ref, e_ref):
        gate = _sigmoid(jnp.dot(hn_ref[...], wg_ref[0], preferred_element_type=F32) + b_ref[...])
        e = jnp.dot(p_ref[0].astype(BF16), we_ref[0], preferred_element_type=F32)
        x2_ref[...] = x1_ref[...] + gate * e
        gate_ref[...] = gate.astype(BF16)
        e_ref[...] = e.astype(BF16)

    row = lambda m: (m, 0)
    fix = lambda m: (0, 0)
    return _pcall(body, name=name, grid=(s // tm,),
                  in_specs=[pl.BlockSpec((tm, d), row), pl.BlockSpec((1, d, d), lambda m: (layer, 0, 0)),
                            pl.BlockSpec((1, d), fix), pl.BlockSpec((1, tm, pd), lambda m: (layer, m, 0)),
                            pl.BlockSpec((1, pd, d), lambda m: (layer, 0, 0)), pl.BlockSpec((tm, d), row)],
                  out_specs=[pl.BlockSpec((tm, d), row)] * 3,
                  out_shape=[jax.ShapeDtypeStruct((s, d), F32), jax.ShapeDtypeStruct((s, d), BF16),
                             jax.ShapeDtypeStruct((s, d), BF16)],
                  compiler_params=_cp(1))(hn, w_pg, b_pg, p, w_pe, x1)


def _loss_head(x, tgt, g, name):
    s, d = x.shape
    tm = _tile(s, 512)

    def body(x_ref, t_ref, g_ref, l_ref, dx_ref, dg_ref):
        m = pl.program_id(0)

        @pl.when(m == 0)
        def _():
            l_ref[...] = jnp.zeros_like(l_ref)
            dg_ref[...] = jnp.zeros_like(dg_ref)

        xv = x_ref[...]
        gv = g_ref[...]
        r = lax.rsqrt(jnp.mean(xv * xv, axis=-1, keepdims=True) + EPS)
        xn = xv * r
        err = xn * gv - t_ref[...]
        l_ref[...] += jnp.sum(err * err)
        dy = err * (1.0 / d)
        dxn = dy * gv
        dx_ref[...] = r * (dxn - xn * jnp.mean(dxn * xn, axis=-1, keepdims=True))
        dg_ref[...] += _colsum8(dy * xn)

    row = lambda m: (m, 0)
    fix = lambda m: (0, 0)
    return _pcall(body, name=name, grid=(s // tm,),
                  in_specs=[pl.BlockSpec((tm, d), row), pl.BlockSpec((tm, d), row), pl.BlockSpec((1, d), fix)],
                  out_specs=[pl.BlockSpec((8, LANES), fix), pl.BlockSpec((tm, d), row), pl.BlockSpec((8, d), fix)],
                  out_shape=[jax.ShapeDtypeStruct((8, LANES), F32), jax.ShapeDtypeStruct((s, d), F32),
                             jax.ShapeDtypeStruct((8, d), F32)],
                  compiler_params=_cp(1))(x, tgt, g)


def _ple_bwd(dx2, gate, e, x1, w_pg, g_ple, w_out, layer, name, riders=()):
    s, d = dx2.shape
    tm = _tile(s, 512)

    def body(*refs):
        m = pl.program_id(0)
        own, riders_end = _riders_run(riders, refs, 7, 6, 0, m, s // tm)
        (dx2_ref, gate_ref, e_ref, x1_ref, wg_ref, g_ref, wo_ref,
         du_ref, de_ref, dx1_ref, dy_ref, db_ref, dg_ref) = own

        @pl.when(m == 0)
        def _():
            db_ref[...] = jnp.zeros_like(db_ref)
            dg_ref[...] = jnp.zeros_like(dg_ref)

        dx2v = dx2_ref[...]
        gate = gate_ref[...].astype(F32)
        du = dx2v * e_ref[...].astype(F32) * gate * (1.0 - gate)
        de_ref[...] = (dx2v * gate).astype(BF16)
        dub = du.astype(BF16)
        du_ref[...] = dub
        db_ref[...] += _colsum8(du)
        dhn = _dot_nt(dub, wg_ref[0])
        dxr, dgr = _rms_bwd_rows(dhn, x1_ref[...], g_ref[...])
        dx1 = dx2v + dxr
        dx1_ref[...] = dx1
        dg_ref[...] += _colsum8(dgr)
        dy_ref[...] = _dot_nt(dx1.astype(BF16), wo_ref[0]).astype(BF16)
        riders_end()

    row = lambda m: (m, 0)
    fix = lambda m: (0, 0)
    t = pl.BlockSpec((tm, d), row)
    r_ops, r_shapes, r_scratch, r_aliases = _riders_plumb(riders, 7, 6)
    outs = _pcall(body, name=name, grid=(s // tm,),
                  in_specs=[t, t, t, t, pl.BlockSpec((1, d, d), lambda m: (layer, 0, 0)), pl.BlockSpec((1, d), fix),
                            pl.BlockSpec((1, d, d), lambda m: (layer, 0, 0))] + [ANY] * len(r_ops),
                  out_specs=[t, t, t, t, pl.BlockSpec((8, d), fix), pl.BlockSpec((8, d), fix)] + [ANY] * len(r_shapes),
                  out_shape=[jax.ShapeDtypeStruct((s, d), BF16), jax.ShapeDtypeStruct((s, d), BF16),
                             jax.ShapeDtypeStruct((s, d), F32), jax.ShapeDtypeStruct((s, d), BF16),
                             jax.ShapeDtypeStruct((8, d), F32), jax.ShapeDtypeStruct((8, d), F32)] + r_shapes,
                  input_output_aliases=r_aliases, scratch_shapes=r_scratch,
                  compiler_params=_cp(1))(dx2, gate, e, x1, w_pg, g_ple, w_out, *r_ops)
    return tuple(outs[:6]) + (list(outs[6:]),)


def _mm_tn(a, b, name, a_layer=None):
    s, ka = a.shape[-2:]
    n = b.shape[1]
    tn = _tile(n, 1024)
    ns = n // tn
    tk = _tile(s, 512)
    nk = s // tk

    def body(a_ref, b_ref, o_ref, acc_ref):
        k = pl.program_id(1)

        @pl.when(k == 0)
        def _():
            acc_ref[...] = jnp.zeros_like(acc_ref)

        av = a_ref[...] if a_layer is None else a_ref[0]
        acc_ref[...] += _dot_tn(av.astype(BF16), b_ref[...].astype(BF16))

        @pl.when(k == nk - 1)
        def _():
            o_ref[...] = acc_ref[...]

    a_spec = (pl.BlockSpec((tk, ka), lambda j, k: (k, 0)) if a_layer is None
              else pl.BlockSpec((1, tk, ka), lambda j, k: (a_layer, k, 0)))
    return _pcall(body, name=name, grid=(ns, nk),
                  in_specs=[a_spec, pl.BlockSpec((tk, tn), lambda j, k: (k, j))],
                  out_specs=pl.BlockSpec((ka, tn), lambda j, k: (0, j)),
                  out_shape=jax.ShapeDtypeStruct((ka, n), F32),
                  scratch_shapes=[pltpu.VMEM((ka, tn), F32)], compiler_params=_cp(2))(a, b)


def _norm_gate_bwd(dy, yv, zg, g, gm):
    r = lax.rsqrt(_group_mean(yv * yv, gm) + EPS)
    n = yv * r
    sg = _sigmoid(zg)
    sil = zg * sg
    dzg = dy * n * g * (sg * (1.0 + zg * (1.0 - sg)))
    dn = dy * g * sil
    dyv = r * (dn - n * _group_mean(dn * n, gm))
    return dyv, dzg, dy * n * sil


def _convmix_bwd(dy, proj, conv_w, conv_b, bg, name, riders=()):
    _, s, sw = proj.shape
    nh = sw // LANES
    tc = _tile(s, 256)
    nr = s // tc

    def body(*refs):
        own, riders_end = _riders_run(riders, refs, 8, 4, 1, pl.program_id(0), nh)
        (dy_ref, cb_ref, cc_ref, ch_ref, cz_ref, w_ref, b_ref, g_ref,
         dp_ref, dw_ref, db_ref, dg_ref, dcv_ref) = own
        gm = _group_mat()
        dcv_ref[pl.ds(s, 8), :] = jnp.zeros((8, LANES), F32)

        def pass1(r, carry):
            dw0, dw1, dw2, db, dg = carry
            r0, u, s1, s2, cv = _conv_rows(cc_ref, ch_ref, w_ref, b_ref, r, tc)
            cb = cb_ref[0, pl.ds(r0, tc), :].astype(F32)
            dyc, dcz, dgr = _norm_gate_bwd(dy_ref[pl.ds(r0, tc), :].astype(F32), cb * cv,
                                           cz_ref[0, pl.ds(r0, tc), :].astype(F32), g_ref[...], gm)
            dp_ref[0, pl.ds(r0, tc), :] = (dyc * cv).astype(BF16)
            dp_ref[3, pl.ds(r0, tc), :] = dcz.astype(BF16)
            dcv = dyc * cb
            dcv_ref[pl.ds(r0, tc), :] = dcv
            return (dw0 + _colsum8(dcv * s2), dw1 + _colsum8(dcv * s1), dw2 + _colsum8(dcv * u),
                    db + _colsum8(dcv), dg + _colsum8(dgr))

        z8 = jnp.zeros((8, LANES), F32)
        dw0, dw1, dw2, db, dg = lax.fori_loop(0, nr, pass1, (z8, z8, z8, z8, z8))
        dw_ref[0] = dw0
        dw_ref[1] = dw1
        dw_ref[2] = dw2
        db_ref[...] = db
        dg_ref[...] = dg

        def pass2(r, carry):
            r0 = pl.multiple_of(r * tc, tc)
            dcv = dcv_ref[pl.ds(r0, tc), :]
            nxt = dcv_ref[pl.ds(pl.multiple_of(r0 + tc, 8), 8), :]
            rid = lax.broadcasted_iota(jnp.int32, dcv.shape, 0)
            n1 = jnp.where(rid == tc - 1, nxt[0:1, :], pltpu.roll(dcv, tc - 1, axis=0))
            n2 = jnp.where(rid == tc - 1, nxt[1:2, :],
                           jnp.where(rid == tc - 2, nxt[0:1, :], pltpu.roll(dcv, tc - 2, axis=0)))
            du = dcv * w_ref[2:3, :] + n1 * w_ref[1:2, :] + n2 * w_ref[0:1, :]
            dp_ref[1, pl.ds(r0, tc), :] = (du * ch_ref[0, pl.ds(r0, tc), :].astype(F32)).astype(BF16)
            dp_ref[2, pl.ds(r0, tc), :] = (du * cc_ref[0, pl.ds(r0, tc), :].astype(F32)).astype(BF16)
            return carry

        lax.fori_loop(0, nr, pass2, 0)
        riders_end()

    def sec(k):
        return pl.BlockSpec((1, s, LANES), lambda c: (k, 0, c))

    col = lambda c: (0, c)
    r_ops, r_shapes, r_scratch, r_aliases = _riders_plumb(riders, 8, 4)
    outs = _pcall(
        body, name=name, grid=(nh,),
        in_specs=[pl.BlockSpec((s, LANES), col), sec(0), sec(1), sec(2), sec(3),
                  pl.BlockSpec((3, LANES), col), pl.BlockSpec((1, LANES), col), pl.BlockSpec((1, LANES), col)]
        + [ANY] * len(r_ops),
        out_specs=[pl.BlockSpec((4, s, LANES), lambda c: (0, 0, c)), pl.BlockSpec((3, 8, LANES), lambda c: (0, 0, c)),
                   pl.BlockSpec((8, LANES), col), pl.BlockSpec((8, LANES), col)] + [ANY] * len(r_shapes),
        out_shape=[jax.ShapeDtypeStruct((8, s, sw), BF16), jax.ShapeDtypeStruct((3, 8, sw), F32),
                   jax.ShapeDtypeStruct((8, sw), F32), jax.ShapeDtypeStruct((8, sw), F32)] + r_shapes,
        input_output_aliases=r_aliases,
        scratch_shapes=[pltpu.VMEM((s + 8, LANES), F32)] + r_scratch, compiler_params=_cp(1),
    )(dy, proj, proj, proj, proj, conv_w, conv_b, bg, *r_ops)
    return tuple(outs[:4]) + (list(outs[4:]),)


def _attn_bwd(proj, dy, ya, tl, walked, kept, bg, buf, name, riders=()):
    _, s, sw = proj.shape
    nhp = sw // LANES
    tk, t, nd, rows_c = _attn_tiles(s)
    nq = s // t
    scale = 1.0 / math.sqrt(HEAD)

    def body(*refs):
        step = pl.program_id(1)
        i = nq - 1 - step
        own, riders_end = _riders_run(riders, refs, 12, 2, 3, pl.program_id(0) * nq + step, nhp * nq)
        (q_ref, k_ref, v_ref, az_ref, dy_ref, ya_ref, tl_ref, nw_ref, g_ref, buf_ref, sa_ref, sb_ref, out_ref, dg_ref,
         dka_ref, dva_ref, dqa_ref) = own

        @pl.when(step == 0)
        def _():
            dka_ref[...] = jnp.zeros_like(dka_ref)
            dva_ref[...] = jnp.zeros_like(dva_ref)
            dg_ref[...] = jnp.zeros_like(dg_ref)

        dyv, dzg, dgr = _norm_gate_bwd(dy_ref[...].astype(F32), ya_ref[...].astype(F32), az_ref[0].astype(F32),
                                       g_ref[...], _group_mat())
        out_ref[3] = dzg.astype(BF16)
        dg_ref[...] += _colsum8(dgr)

        tri = (lax.broadcasted_iota(jnp.int32, (tk, tk), 0) <=
               lax.broadcasted_iota(jnp.int32, (tk, tk), 1)).astype(BF16)
        lane = lax.broadcasted_iota(jnp.int32, (t, LANES), 1)
        q = q_ref[0] * jnp.asarray(scale, BF16)
        do = dyv.astype(BF16)
        qms = [jnp.where((lane // HEAD) == h, q, jnp.zeros_like(q)) for h in range(2)]
        doms = [jnp.where((lane // HEAD) == h, do, jnp.zeros_like(do)) for h in range(2)]
        dqa_ref[...] = jnp.zeros_like(dqa_ref)
        chains = [(h, r0) for h in range(2) for r0 in range(0, t, rows_c)]
        qparts = [qms[h][r0:r0 + rows_c] for h, r0 in chains]
        doparts = [doms[h][r0:r0 + rows_c] for h, r0 in chains]
        tots = [tl_ref[h, r0:r0 + rows_c, :] for h, r0 in chains]

        def block(carry, tiles, items):
            k0s = [pl.multiple_of(j * tk, tk) for j in tiles]
            kjs = [k_ref[0, pl.ds(k0, tk), :] for k0 in k0s]
            vjs = [v_ref[0, pl.ds(k0, tk), :] for k0 in k0s]
            zs = [_dot_nt(qparts[n], kjs[t][:kw]) for n, t, kw, _, _ in items]
            das = [_dot_nt(doparts[n], vjs[t][:kw]) for n, t, kw, _, _ in items]
            keeps = [_both(mask, gate) for _, _, _, mask, gate in items]
            lms, lss, cls = [], [], []
            for z, keep, (n, t, kw, _, _) in zip(zs, keeps, items):
                lm, ls = _softplus_parts(z)
                if keep is not None:
                    lm = jnp.where(keep, lm, 0.0)
                lms.append(lm)
                lss.append(ls)
                cls.append(_split_dot(lm, tri[:kw, :kw], 2))
            cur = list(carry)
            psums, abs_, gs, cgs = [], [], [], []
            for lm, ls, cl, da, keep, (n, t, kw, _, _) in zip(lms, lss, cls, das, keeps, items):
                psum, gsum = cur[n]
                a = jnp.exp(ls + (tots[n] - psum - cl))
                if keep is not None:
                    a = jnp.where(keep, a, 0.0)
                g = a * da
                psums.append(gsum)
                gs.append(g)
                abs_.append(a.astype(BF16))
                cgs.append(_split_dot(g, tri[:kw, :kw], 1))
                cur[n] = (psum + jnp.sum(lm, axis=1, keepdims=True), gsum + jnp.sum(g, axis=1, keepdims=True))
            dks, dvs = {}, {}
            for ls, a, g, cg, gsum, keep, (n, t, kw, _, _) in zip(lss, abs_, gs, cgs, psums, keeps, items):
                h, r0 = chains[n]
                dz = g - jnp.exp(ls) * (gsum + cg)
                if keep is not None:
                    dz = jnp.where(keep, dz, 0.0)
                dz = dz.astype(BF16)
                dqa_ref[h, r0:r0 + rows_c, :] += jnp.dot(dz, kjs[t][:kw], preferred_element_type=F32)
                dkh = _dot_tn(dz, qparts[n])
                dvh = _dot_tn(a, doparts[n])
                dks[t, kw] = dkh if (t, kw) not in dks else dks[t, kw] + dkh
                dvs[t, kw] = dvh if (t, kw) not in dvs else dvs[t, kw] + dvh
            for t, kw in dks:
                dka_ref[pl.ds(k0s[t], kw), :] += dks[t, kw]
                dva_ref[pl.ds(k0s[t], kw), :] += dvs[t, kw]
            return tuple(cur)

        def kept_block(carry, tiles, items):
            k0s = [pl.multiple_of(j * tk, tk) for j in tiles]
            kjs = [k_ref[0, pl.ds(k0, tk), :] for k0 in k0s]
            vjs = [v_ref[0, pl.ds(k0, tk), :] for k0 in k0s]
            das = [_dot_nt(doparts[n], vjs[t][:kw]) for n, t, kw, _, _, _ in items]
            cur = list(carry)
            gsums, kept_a, gs, cgs = [], [], [], []
            for da, (n, t, kw, _, _, slot) in zip(das, items):
                psum, gsum = cur[n]
                a = sa_ref[0, slot, :, 0:kw]
                g = a.astype(F32) * da
                gsums.append(gsum)
                kept_a.append(a)
                gs.append(g)
                cgs.append(_split_dot(g, tri[:kw, :kw], 1))
                cur[n] = (psum, gsum + jnp.sum(g, axis=1, keepdims=True))
            dks, dvs = {}, {}
            for a, g, cg, gsum, (n, t, kw, mask, gate, slot) in zip(kept_a, gs, cgs, gsums, items):
                h, r0 = chains[n]
                dz = g - sb_ref[0, slot, :, 0:kw].astype(F32) * (gsum + cg)
                keep = _both(mask, gate)
                if keep is not None:
                    dz = jnp.where(keep, dz, 0.0)
                dz = dz.astype(BF16)
                dqa_ref[h, r0:r0 + rows_c, :] += jnp.dot(dz, kjs[t][:kw], preferred_element_type=F32)
                dkh = _dot_tn(dz, qparts[n])
                dvh = _dot_tn(a, doparts[n])
                dks[t, kw] = dkh if (t, kw) not in dks else dks[t, kw] + dkh
                dvs[t, kw] = dvh if (t, kw) not in dvs else dvs[t, kw] + dvh
            for t, kw in dks:
                dka_ref[pl.ds(k0s[t], kw), :] += dks[t, kw]
                dva_ref[pl.ds(k0s[t], kw), :] += dvs[t, kw]
            return tuple(cur)

        z1 = jnp.zeros((rows_c, 1), F32)
        everyone = [(n, 0, tk, None, None) for n in range(len(chains))]
        dslot, lslot, _ = _static_slots(chains, nd, rows_c, tk)
        upper = [n for n, (_, r0) in enumerate(chains) if r0 >= tk]
        lower = [n for n, (_, r0) in enumerate(chains) if r0 < tk]
        left = jnp.maximum(i * nd - 1, 0)
        code = jnp.clip(jnp.max(nw_ref[0].astype(jnp.int32)), 0, 2 * left + 1)
        too = jnp.where(i > 0, code % 2, 0)
        whole = jnp.minimum(code // 2, left)
        carry = lax.fori_loop(left - whole, left, lambda j, c: block(c, [j], everyone), ((z1, z1),) * len(chains))
        if upper:
            carry = lax.cond(too > 0, lambda c: block(c, [left], [(n, 0, tk, None, None) for n in upper]),
                             lambda c: c, carry)
        carry = kept_block(carry, [left, i * nd],
                           [(n, 0, tk, None, i > 0, lslot[n]) for n in lower]
                           + [(n, 1, kw, m, None, dslot[0, n]) for n, kw, m in _diag_work(chains, 0, rows_c, tk)])
        for d in range(1, nd):
            carry = kept_block(carry, [i * nd + d],
                               [(n, 0, kw, m, None, dslot[d, n]) for n, kw, m in _diag_work(chains, d, rows_c, tk)])
        out_ref[0] = (jnp.where(lane < HEAD, dqa_ref[0], dqa_ref[1]) * scale).astype(BF16)
        own = pl.multiple_of(i * t, t)
        out_ref[1] = dka_ref[pl.ds(own, t), :].astype(BF16)
        out_ref[2] = dva_ref[pl.ds(own, t), :].astype(BF16)
        riders_end()

    def rows(sec):
        return pl.BlockSpec((1, t, LANES), lambda hp, st: (sec, nq - 1 - st, hp))

    def whole(sec):
        return pl.BlockSpec((1, s, LANES), lambda hp, st: (sec, 0, hp))

    r_ops, r_shapes, r_scratch, r_aliases = _riders_plumb(riders, 12, 2)
    kept_spec = pl.BlockSpec((1,) + kept[0].shape[1:], lambda hp, st: (hp * nq + nq - 1 - st, 0, 0, 0))
    outs = _pcall(
        body, name=name, grid=(nhp, nq),
        in_specs=[rows(4), whole(5), whole(6), rows(7),
                  pl.BlockSpec((t, LANES), lambda hp, st: (nq - 1 - st, hp + nhp)),
                  pl.BlockSpec((t, LANES), lambda hp, st: (nq - 1 - st, hp)),
                  pl.BlockSpec((2, t, 1), lambda hp, st: (hp, nq - 1 - st, 0)),
                  pl.BlockSpec((1, 8, LANES), lambda hp, st: (hp * nq + nq - 1 - st, 0, 0)),
                  pl.BlockSpec((1, LANES), lambda hp, st: (0, hp + nhp)), ANY, kept_spec, kept_spec]
        + [ANY] * len(r_ops),
        out_specs=[pl.BlockSpec((4, t, LANES), lambda hp, st: (1, nq - 1 - st, hp)),
                   pl.BlockSpec((8, LANES), lambda hp, st: (0, hp))] + [ANY] * len(r_shapes),
        out_shape=[jax.ShapeDtypeStruct(buf.shape, buf.dtype), jax.ShapeDtypeStruct((8, sw), F32)] + r_shapes,
        input_output_aliases={9: 0, **r_aliases},
        scratch_shapes=[pltpu.VMEM((s, LANES), F32), pltpu.VMEM((s, LANES), F32), pltpu.VMEM((2, t, LANES), F32)]
        + r_scratch,
        compiler_params=_cp(2))(proj, proj, proj, proj, dy, ya, tl, walked, bg, buf, kept[0], kept[1], *r_ops)
    return outs[0], outs[1], list(outs[2:])


def _grad_w_in(h, dproj, name):
    s, d = h.shape
    ns, _, sw = dproj.shape

    def body(h_ref, b_ref, o_ref, ht_ref):
        @pl.when(pl.program_id(0) == 0)
        def _():
            ht_ref[...] = h_ref[...].T

        o_ref[...] = jnp.dot(ht_ref[...], b_ref[0], preferred_element_type=F32)

    return _pcall(body, name=name, grid=(ns,),
                  in_specs=[pl.BlockSpec((s, d), lambda j: (0, 0)), pl.BlockSpec((1, s, sw), lambda j: (j, 0, 0))],
                  out_specs=pl.BlockSpec((d, sw), lambda j: (0, j)),
                  out_shape=jax.ShapeDtypeStruct((d, ns * sw), F32),
                  scratch_shapes=[pltpu.VMEM((d, s), BF16)], compiler_params=_cp(1))(h, dproj)


def _inproj_bwd(dproj, w, layer, x, g, dx1, name, riders=()):
    ns, s, sw = dproj.shape
    d = x.shape[1]
    tm = _tile(s, 512)

    def body(*refs):
        own, riders_end = _riders_run(riders, refs, 5, 2, 0, pl.program_id(0), s // tm)
        dp_ref, w_ref, x_ref, g_ref, dx1_ref, dx_ref, dg_ref = own

        @pl.when(pl.program_id(0) == 0)
        def _():
            dg_ref[...] = jnp.zeros_like(dg_ref)

        dh = _dot_nt(dp_ref[0], w_ref[0, :, 0:sw])
        for k in range(1, ns):
            dh = dh + _dot_nt(dp_ref[k], w_ref[0, :, k * sw:(k + 1) * sw])
        dxr, dgr = _rms_bwd_rows(dh, x_ref[...], g_ref[...])
        dx_ref[...] = dx1_ref[...] + dxr
        dg_ref[...] += _colsum8(dgr)
        riders_end()

    row = lambda m: (m, 0)
    fix = lambda m: (0, 0)
    r_ops, r_shapes, r_scratch, r_aliases = _riders_plumb(riders, 5, 2)
    outs = _pcall(body, name=name, grid=(s // tm,),
                  in_specs=[pl.BlockSpec((ns, tm, sw), lambda m: (0, m, 0)),
                            pl.BlockSpec((1, d, ns * sw), lambda m: (layer, 0, 0)),
                            pl.BlockSpec((tm, d), row), pl.BlockSpec((1, d), fix), pl.BlockSpec((tm, d), row)]
                  + [ANY] * len(r_ops),
                  out_specs=[pl.BlockSpec((tm, d), row), pl.BlockSpec((8, d), fix)] + [ANY] * len(r_shapes),
                  out_shape=[jax.ShapeDtypeStruct((s, d), F32), jax.ShapeDtypeStruct((8, d), F32)] + r_shapes,
                  input_output_aliases=r_aliases, scratch_shapes=r_scratch,
                  compiler_params=_cp(1))(dproj, w, x, g, dx1, *r_ops)
    return outs[0], outs[1], list(outs[2:])


def _adamw(w, g, m, v, name):
    r, c = w.shape
    tr = _tile(r, 512)
    c1 = 1.0 - ADAM_B1 ** ADAM_STEP
    c2 = 1.0 - ADAM_B2 ** ADAM_STEP

    def body(w_ref, g_ref, m_ref, v_ref, go_ref, d_ref, mo_ref, vo_ref):
        gv = g_ref[...]
        go_ref[...] = gv
        mn = ADAM_B1 * m_ref[...] + (1.0 - ADAM_B1) * gv
        vn = ADAM_B2 * v_ref[...] + (1.0 - ADAM_B2) * (gv * gv)
        d_ref[...] = -ADAM_LR * ((mn / c1) / (jnp.sqrt(vn / c2) + ADAM_EPS) + ADAM_WD * w_ref[...])
        mo_ref[...] = mn
        vo_ref[...] = vn

    t = pl.BlockSpec((tr, c), lambda i: (i, 0))
    return _pcall(body, name=name, grid=(r // tr,), in_specs=[t] * 4, out_specs=[t] * 4,
                  out_shape=[jax.ShapeDtypeStruct((r, c), F32)] * 4, compiler_params=_cp(1))(w, g, m, v)


def _add_half(grad, other, core, a, name):
    hr, hc = other.shape
    tr = _tile(hr, 256)
    nb = hr // tr

    def body(c_ref, g_ref, o_ref, out_ref, outb_ref):
        v = g_ref[...] + o_ref[...]
        out_ref[...] = v
        outb_ref[...] = v.astype(BF16)

    t = pl.BlockSpec((tr, hc), lambda i, c: (i, 0))
    own = (lambda i, c: (c[0] * nb + i, 0)) if HALF_AXES[a] == 0 else (lambda i, c: (i, c[0]))
    grid_spec = pltpu.PrefetchScalarGridSpec(
        num_scalar_prefetch=1, grid=(nb,), in_specs=[pl.BlockSpec((tr, hc), own), t], out_specs=[t, t])
    return _pcall(body, name=name, grid_spec=grid_spec,
                  out_shape=[jax.ShapeDtypeStruct((hr, hc), F32), jax.ShapeDtypeStruct((hr, hc), BF16)],
                  compiler_params=_cp(1))(core.reshape(1).astype(jnp.int32), grad, other)


def _sum_half(wide, parts, chip, core, layer, a, stack, name):
    _, sr, sc = parts.shape
    tr = _tile(sr, 256)
    nbs = sr // tr

    def body(k_ref, f_ref, p_ref, *rest):
        rest[-1][0] = ((f_ref[...] + p_ref[0].astype(F32)) + p_ref[1].astype(F32)) + p_ref[2].astype(F32)

    f_map = (lambda i, k: (i, k[0])) if SHARD_AXES[a] == 1 else (lambda i, k: (k[0] * nbs + i, 0))
    if HALF_AXES[a] == 0:
        shape, o_map = (DEPTH, 2 * sr, sc), (lambda i, k: (layer, k[1] * nbs + i, 0))
    else:
        shape, o_map = (DEPTH, sr, 2 * sc), (lambda i, k: (layer, i, k[1]))
    in_specs = [pl.BlockSpec((tr, sc), f_map), pl.BlockSpec((3, tr, sc), lambda i, k: (0, i, 0))]
    args = [wide, parts]
    aliases = {}
    if stack is not None:
        in_specs.append(ANY)
        args.append(stack)
        aliases = {3: 0}
    grid_spec = pltpu.PrefetchScalarGridSpec(
        num_scalar_prefetch=1, grid=(nbs,), in_specs=in_specs, out_specs=pl.BlockSpec((1, tr, sc), o_map))
    return _pcall(body, name=name, grid_spec=grid_spec, out_shape=jax.ShapeDtypeStruct(shape, F32),
                  input_output_aliases=aliases,
                  compiler_params=_cp(1))(jnp.stack([chip, core]).astype(jnp.int32), *args)


def _sum_slots(slots, name):
    n = slots.shape[0]

    def body(s_ref, o_ref):
        acc = s_ref[0]
        for i in range(1, n):
            acc = acc + s_ref[i]
        o_ref[...] = acc

    return _pcall(body, name=name, out_shape=jax.ShapeDtypeStruct(slots.shape[1:], F32))(slots)


def _place():
    return lax.axis_index("x"), lax.axis_index("y"), lax.axis_index("c")


def _shard_view(ref, axis, chip, size):
    if axis == 0:
        return ref.at[pl.ds(chip * size, size), :]
    return ref.at[:, pl.ds(chip * size, size)]


SHARD_AXES = (1, 0, 0, 1)
HALF_AXES = tuple(1 - ax for ax in SHARD_AXES)


class _Rider:
    def __init__(self, operands, out_shape, sems, phases, aliased=False):
        self.operands, self.out_shape, self.sems = list(operands), list(out_shape), list(sems)
        self.phases, self.aliased = phases, aliased


def _riders_plumb(riders, n_in, n_out):
    ops, out_shape, scratch, aliases = [], [], [], {}
    for r in riders:
        if r.aliased:
            for k in range(len(r.operands)):
                aliases[n_in + len(ops) + k] = n_out + len(out_shape) + k
        ops += r.operands
        out_shape += r.out_shape
        scratch += r.sems
    return ops, out_shape, scratch, aliases


def _riders_run(riders, refs, n_in, n_out, n_scr, step, nsteps):
    n_rin = sum(len(r.operands) for r in riders)
    n_rout = sum(len(r.out_shape) for r in riders)
    rin = refs[n_in:n_in + n_rin]
    o0 = n_in + n_rin
    rout = refs[o0 + n_out:o0 + n_out + n_rout]
    s0 = o0 + n_out + n_rout
    rsem = refs[s0 + n_scr:]
    own = list(refs[:n_in]) + list(refs[o0:o0 + n_out]) + list(refs[s0:s0 + n_scr])
    lasts = []
    for r in riders:
        ph = r.phases(rin[:len(r.operands)], rout[:len(r.out_shape)], rsem[:len(r.sems)])
        rin, rout, rsem = rin[len(r.operands):], rout[len(r.out_shape):], rsem[len(r.sems):]
        pl.when(step == 0)(ph[0])
        for mid in ph[1:-1]:
            pl.when(step == (3 * nsteps) // 4)(mid)
        lasts.append(ph[-1])

    def finish():
        for last in lasts:
            pl.when(step == nsteps - 1)(last)

    return own, finish


def _gather_phases(ins, outs, ssem, rsem, layer, which):
    n = len(ins)
    x, y, c = _place()
    me = 2 * x + y
    chips = [(1 - x, y), (x, 1 - y), (1 - x, 1 - y)]

    def piece(a, chip, half, of):
        ax = SHARD_AXES[which[a]]
        block = _shard_view(of[a].at[layer], ax, chip, of[a].shape[1 + ax] // 4)
        r = block.shape[0] // 2
        return block.at[pl.ds(half * r, r), :]

    def over_ici(a, j):
        cx, cy = chips[j]
        return pltpu.make_async_remote_copy(
            src_ref=piece(a, me, c, ins), dst_ref=piece(a, me, c, outs), send_sem=ssem.at[a, j],
            recv_sem=rsem.at[a, j], device_id=(cx, cy, c), device_id_type=MESH)

    def landed(a, j, half):
        cx, cy = chips[j]
        return piece(a, 2 * cx + cy, half, outs)

    def to_sibling(a, j):
        got = landed(a, j, c)
        return pltpu.make_async_remote_copy(
            src_ref=got, dst_ref=got, send_sem=ssem.at[a, 3 + j], recv_sem=rsem.at[a, 3 + j],
            device_id=(x, y, 1 - c), device_id_type=MESH)

    def wait_arrival(a, k, place):
        pltpu.make_async_remote_copy(src_ref=place, dst_ref=place, send_sem=ssem.at[a, k], recv_sem=rsem.at[a, k],
                                     device_id=(x, y, c), device_id_type=MESH).wait_recv()

    def start():
        for a in range(n):
            for j in range(3):
                over_ici(a, j).start()

    def pass_on():
        for a in range(n):
            for j in range(3):
                wait_arrival(a, j, landed(a, j, c))
                to_sibling(a, j).start()

    def finish():
        for a in range(n):
            for j in range(3):
                wait_arrival(a, 3 + j, landed(a, j, 1 - c))
        for a in range(n):
            for j in range(3):
                over_ici(a, j).wait_send()
                to_sibling(a, j).wait_send()

    return start, pass_on, finish


def _gather_rider(fulls, layer, which):
    n = len(fulls)
    return _Rider(fulls, [jax.ShapeDtypeStruct(f.shape, f.dtype) for f in fulls],
                  [pltpu.SemaphoreType.DMA((n, 6)), pltpu.SemaphoreType.DMA((n, 6))],
                  lambda ins, outs, sems: _gather_phases(ins, outs, sems[0], sems[1], layer, which), aliased=True)


def _ride_alone(rider, name):
    n = len(rider.operands)

    def body(*refs):
        for phase in rider.phases(refs[:n], refs[n:n + len(rider.out_shape)], refs[n + len(rider.out_shape):]):
            phase()

    return _pcall(body, name=name, in_specs=[ANY] * n, out_specs=[ANY] * len(rider.out_shape),
                  out_shape=rider.out_shape, scratch_shapes=rider.sems,
                  input_output_aliases={a: a for a in range(n)} if rider.aliased else {})(*rider.operands)


def _half_view(ref, a, half):
    n = ref.shape[HALF_AXES[a]] // 2
    if HALF_AXES[a] == 0:
        return ref.at[pl.ds(half * n, n), :]
    return ref.at[:, pl.ds(half * n, n)]


def _swap_rider(grads, which):
    n = len(grads)
    halves = []
    for g, w in zip(grads, which):
        sh = list(g.shape)
        sh[HALF_AXES[w]] //= 2
        halves.append(jax.ShapeDtypeStruct(tuple(sh), g.dtype))

    def phases(srcs, outs, sems):
        x, y, c = _place()

        def copy(a):
            return pltpu.make_async_remote_copy(
                src_ref=_half_view(srcs[a], which[a], 1 - c), dst_ref=outs[a], send_sem=sems[0].at[a],
                recv_sem=sems[1].at[a], device_id=(x, y, 1 - c), device_id_type=MESH)

        def start():
            for a in range(n):
                copy(a).start()

        def finish():
            for a in range(n):
                copy(a).wait()

        return start, finish

    return _Rider(grads, halves, [pltpu.SemaphoreType.DMA((n,)), pltpu.SemaphoreType.DMA((n,))], phases)


def _scatter_rider(sums, which):
    n = len(sums)
    shapes = []
    for f, w in zip(sums, which):
        sh = list(f.shape)
        sh[SHARD_AXES[w]] //= 4
        shapes.append(jax.ShapeDtypeStruct((3,) + tuple(sh), f.dtype))

    def phases(srcs, outs, sems):
        x, y, c = _place()
        chips = [(1 - x, y), (x, 1 - y), (1 - x, 1 - y)]

        def copy(a, j):
            cx, cy = chips[j]
            ax = SHARD_AXES[which[a]]
            src = _shard_view(srcs[a], ax, 2 * cx + cy, srcs[a].shape[ax] // 4)
            return pltpu.make_async_remote_copy(src_ref=src, dst_ref=outs[a].at[j], send_sem=sems[0].at[a, j],
                                                recv_sem=sems[1].at[a, j], device_id=(cx, cy, c), device_id_type=MESH)

        def start():
            for a in range(n):
                for j in range(3):
                    copy(a, j).start()

        def finish():
            for a in range(n):
                for j in range(3):
                    copy(a, j).wait()

        return start, finish

    return _Rider(sums, shapes, [pltpu.SemaphoreType.DMA((n, 3)), pltpu.SemaphoreType.DMA((n, 3))], phases)


def _pair_halves(stacks):
    n = len(stacks)

    def body(*refs):
        ins, outs = refs[:n], refs[n:2 * n]
        ssem, rsem = refs[2 * n:]
        x, y, c = _place()
        cps = [pltpu.make_async_remote_copy(
            src_ref=_half_view(ins[a].at[l], a, c), dst_ref=_half_view(outs[a].at[l], a, c), send_sem=ssem.at[a, l],
            recv_sem=rsem.at[a, l], device_id=(x, y, 1 - c), device_id_type=MESH)
            for a in range(n) for l in range(DEPTH)]
        for cp in cps:
            cp.start()
        for a in range(n):
            for l in range(DEPTH):
                got = _half_view(outs[a].at[l], a, 1 - c)
                pltpu.make_async_remote_copy(src_ref=got, dst_ref=got, send_sem=ssem.at[a, l], recv_sem=rsem.at[a, l],
                                             device_id=(x, y, 1 - c), device_id_type=MESH).wait_recv()
        for cp in cps:
            cp.wait_send()

    return _pcall(body, name="pair_halves", in_specs=[ANY] * n, out_specs=[ANY] * n,
                  out_shape=[jax.ShapeDtypeStruct(st.shape, st.dtype) for st in stacks],
                  input_output_aliases={a: a for a in range(n)},
                  scratch_shapes=[pltpu.SemaphoreType.DMA((n, DEPTH)), pltpu.SemaphoreType.DMA((n, DEPTH))])(*stacks)


class _GradReduce:
    def __init__(self, chip, core):
        self.chip, self.core = chip, core
        self.stacks = [None] * len(SHARD_AXES)

    def add(self, layer, grads, which, got):
        return [(layer, w) + tuple(_add_half(g, o, self.core, w, f"add_half_{layer}_{w}"))
                for g, o, w in zip(grads, got, which)]

    def finish(self, sums, partials):
        for (layer, w, wide, _), pr in zip(sums, partials):
            self.stacks[w] = _sum_half(wide, pr, self.chip, self.core, layer, w, self.stacks[w], f"sum_half_{layer}_{w}")

    def result(self):
        return _pair_halves(self.stacks)


def _exchange_small(pack, name, riders=()):
    nd = 8

    def body(*refs):
        own, riders_end = _riders_run(riders, refs, 1, 1, 2, jnp.int32(0), 1)
        p_ref, o_ref, ssem, rsem = own
        x, y, c = _place()
        me = 4 * x + 2 * y + c
        o_ref[me] = p_ref[...]
        cps = []
        for j in range(1, nd):
            px, py, pc = x ^ (j >> 2), y ^ ((j >> 1) & 1), c ^ (j & 1)
            cps.append(pltpu.make_async_remote_copy(
                src_ref=p_ref, dst_ref=o_ref.at[me], send_sem=ssem.at[j - 1], recv_sem=rsem.at[j - 1],
                device_id=(px, py, pc), device_id_type=MESH))
        for cp in cps:
            cp.start()
        for j in range(1, nd):
            peer = me ^ j
            got = o_ref.at[peer]
            pltpu.make_async_remote_copy(src_ref=got, dst_ref=got, send_sem=ssem.at[j - 1], recv_sem=rsem.at[j - 1],
                                         device_id=(x, y, c), device_id_type=MESH).wait_recv()
        for cp in cps:
            cp.wait_send()
        riders_end()

    vm = pl.BlockSpec(memory_space=pltpu.VMEM)
    r_ops, r_shapes, r_scratch, r_aliases = _riders_plumb(riders, 1, 1)
    outs = _pcall(body, name=name, in_specs=[vm] + [ANY] * len(r_ops), out_specs=[vm] + [ANY] * len(r_shapes),
                  out_shape=[jax.ShapeDtypeStruct((nd,) + pack.shape, pack.dtype)] + r_shapes,
                  input_output_aliases=r_aliases,
                  scratch_shapes=[pltpu.SemaphoreType.DMA((nd - 1,)), pltpu.SemaphoreType.DMA((nd - 1,))]
                  + r_scratch)(pack, *r_ops)
    return (outs[0], list(outs[1:])) if riders else outs[0]


def _row(v):
    return v.reshape(1, -1)


def _local_step(x, p, tgt, norm_g, conv_w, conv_b, branch_g, ple_norm_g, b_pg, final_g, w_in, w_out, w_pg, w_pe,
                gather=False, reduce=None):
    saved = []
    xl = x
    for l in range(DEPTH):
        riders = [_gather_rider([w_out, w_pg, w_pe], 0, [1, 2, 3])] if gather and l == 0 else []
        h, proj, got = _inproj(xl, _row(norm_g[l]), w_in, l, f"inproj_{l}", riders)
        if riders:
            w_out, w_pg, w_pe = got
        later = gather and l + 1 < DEPTH
        riders = [_gather_rider([w_in, w_pg, w_pe], l + 1, [0, 2, 3])] if later else []
        ya, tl, walked, kept, got = _attn_fwd(proj, f"attn_fwd_{l}", riders)
        if riders:
            w_in, w_pg, w_pe = got
        riders = [_gather_rider([w_out], l + 1, [1])] if later else []
        y, got = _mix_fwd(proj, ya, conv_w[l], _row(conv_b[l]), _row(branch_g[l]), f"mix_fwd_{l}", riders)
        if riders:
            w_out, = got
        x1, hn = _outproj(y, w_out, l, xl, _row(ple_norm_g[l]), f"outproj_{l}")
        x2, gate, e = _ple_fwd(hn, w_pg, _row(b_pg[l]), p, w_pe, l, x1, f"ple_fwd_{l}")
        saved.append((xl, h, proj, ya, tl, walked, kept, y, x1, hn, gate, e))
        xl = x2

    sq, dx, d_final = _loss_head(xl, tgt, _row(final_g), "loss_head")

    big = [None] * DEPTH
    carried = []
    small = {k: [None] * DEPTH for k in ("norm_g", "conv_w", "conv_b", "branch_g", "ple_norm_g", "b_pg")}
    for l in reversed(range(DEPTH)):
        xl, h, proj, ya, tl, walked, kept, y, x1, hn, gate, e = saved[l]
        du, de, dx1, dy, db_pg, d_ple, _ = _ple_bwd(dx, gate, e, x1, w_pg, _row(ple_norm_g[l]), w_out, l,
                                                    f"ple_bwd_{l}")
        sums = carried
        g_pg = _mm_tn(hn, du, f"grad_w_pg_{l}")
        g_pe = _mm_tn(p, de, f"grad_w_pe_{l}", a_layer=l)
        g_out = _mm_tn(y, dx1, f"grad_w_out_{l}")
        others = [g_out, g_pg, g_pe]
        riders = [_swap_rider(others, [1, 2, 3])] if reduce is not None else []
        dpc, d_cw, d_cb, d_bg_c, got = _convmix_bwd(dy, proj, conv_w[l], _row(conv_b[l]), _row(branch_g[l]),
                                                    f"convmix_bwd_{l}", riders)
        if reduce is not None:
            sums += reduce.add(l, others, [1, 2, 3], got)
        riders = [_scatter_rider([sm[3] for sm in sums], [sm[1] for sm in sums])] if sums else []
        dproj, d_bg_a, got = _attn_bwd(proj, dy, ya, tl, walked, kept, _row(branch_g[l]), dpc, f"attn_bwd_{l}", riders)
        if sums:
            reduce.finish(sums, got)
        g_in = _grad_w_in(h, dproj, f"grad_w_in_{l}")
        if reduce is None:
            dx, d_norm, _ = _inproj_bwd(dproj, w_in, l, xl, _row(norm_g[l]), dx1, f"inproj_bwd_{l}")
        elif l > 0:
            dx, d_norm, got = _inproj_bwd(dproj, w_in, l, xl, _row(norm_g[l]), dx1, f"inproj_bwd_{l}",
                                          [_swap_rider([g_in], [0])])
            carried = reduce.add(l, [g_in], [0], got)
        else:
            sums = reduce.add(l, [g_in], [0], _ride_alone(_swap_rider([g_in], [0]), "swap_halves_last"))
            dx, d_norm, got = _inproj_bwd(dproj, w_in, l, xl, _row(norm_g[l]), dx1, f"inproj_bwd_{l}",
                                          [_scatter_rider([sm[3] for sm in sums], [0])])
            reduce.finish(sums, got)
        big[l] = (g_in, g_out, g_pg, g_pe)
        small["norm_g"][l] = jnp.sum(d_norm, axis=0)
        small["conv_w"][l] = jnp.sum(d_cw, axis=1)
        small["conv_b"][l] = jnp.sum(d_cb, axis=0)
        small["branch_g"][l] = jnp.concatenate([jnp.sum(d_bg_c, axis=0), jnp.sum(d_bg_a, axis=0)])
        small["ple_norm_g"][l] = jnp.sum(d_ple, axis=0)
        small["b_pg"][l] = jnp.sum(db_pg, axis=0)
    small = {k: jnp.stack(v) for k, v in small.items()}
    small["final_g"] = jnp.sum(d_final, axis=0)
    return sq[0, 0], dx, big, small


SMALL_ORDER = ("norm_g", "conv_w", "conv_b", "branch_g", "ple_norm_g", "b_pg", "final_g")


def _pack(parts, width):
    flat = jnp.concatenate([v.reshape(-1) for v in parts])
    rows = -(-flat.shape[0] // width)
    rows = -(-rows // 8) * 8
    return jnp.pad(flat, (0, rows * width - flat.shape[0])).reshape(rows, width)


def _unpack(packed, like):
    flat = packed.reshape(-1)
    out, off = [], 0
    for v in like:
        out.append(flat[off:off + v.size].reshape(v.shape))
        off += v.size
    return out


def kernel(x, p, norm_g, w_in, conv_w, conv_b, branch_g, w_out, ple_norm_g, w_pg, b_pg, w_pe, final_g, loss_target, m_norm_g, m_w_in, m_conv_w, m_conv_b, m_branch_g, m_w_out, m_ple_norm_g, m_w_pg, m_b_pg, m_w_pe, m_final_g, v_norm_g, v_w_in, v_conv_w, v_conv_b, v_branch_g, v_w_out, v_ple_norm_g, v_w_pg, v_b_pg, v_w_pe, v_final_g):
    ix, iy, ic = _place()
    chip = 2 * ix + iy
    d = x.shape[-1]

    big_w = (w_in, w_out, w_pg, w_pe)
    own = [_cast_into_full(w, chip, ax, f"cast_{i}") for i, (w, ax) in enumerate(zip(big_w, SHARD_AXES))]
    full_in, = _ride_alone(_gather_rider([own[0]], 0, [0]), "gather_w_in_0")
    full_out, full_pg, full_pe = own[1:]
    cw_shard = conv_w.shape[-1]
    cw_slots = _exchange_small(_pack([conv_w], LANES), "exchange_conv_w")
    conv_full = jnp.concatenate([_unpack(cw_slots[2 * k], [conv_w])[0] for k in range(4)], axis=-1)

    reduce = _GradReduce(chip, ic)
    sq, dx, _, small_g = _local_step(
        x[0], p[:, 0], loss_target[0], norm_g, conv_full, conv_b, branch_g, ple_norm_g, b_pg, final_g,
        full_in, full_out, full_pg, full_pe, gather=True, reduce=reduce)

    g_big = reduce.result()

    parts = [small_g[k] for k in SMALL_ORDER] + [sq.reshape(1)]
    slots = _exchange_small(_pack(parts, d), "exchange_small_grads")
    total = _unpack(_sum_slots(slots, "sum_small"), parts)
    g_small = dict(zip(SMALL_ORDER, total[:-1]))
    loss = 0.5 * total[-1][0] / d
    g_small["conv_w"] = lax.dynamic_slice_in_dim(g_small["conv_w"], chip * cw_shard, cw_shard, axis=2)

    grads = dict(g_small)
    grads.update(w_in=g_big[0], w_out=g_big[1], w_pg=g_big[2], w_pe=g_big[3])
    weights = dict(norm_g=norm_g, w_in=w_in, conv_w=conv_w, conv_b=conv_b, branch_g=branch_g, w_out=w_out,
                   ple_norm_g=ple_norm_g, w_pg=w_pg, b_pg=b_pg, w_pe=w_pe, final_g=final_g)
    ms = dict(norm_g=m_norm_g, w_in=m_w_in, conv_w=m_conv_w, conv_b=m_conv_b, branch_g=m_branch_g, w_out=m_w_out,
              ple_norm_g=m_ple_norm_g, w_pg=m_w_pg, b_pg=m_b_pg, w_pe=m_w_pe, final_g=m_final_g)
    vs = dict(norm_g=v_norm_g, w_in=v_w_in, conv_w=v_conv_w, conv_b=v_conv_b, branch_g=v_branch_g, w_out=v_w_out,
              ple_norm_g=v_ple_norm_g, w_pg=v_w_pg, b_pg=v_b_pg, w_pe=v_w_pe, final_g=v_final_g)
    names = ("norm_g", "w_in", "conv_w", "conv_b", "branch_g", "w_out", "ple_norm_g", "w_pg", "b_pg", "w_pe", "final_g")
    delta, new_m, new_v = {}, {}, {}
    for k in ("w_in", "w_out", "w_pg", "w_pe"):
        shp = weights[k].shape
        two = lambda a: a.reshape(-1, shp[-1])
        gr, dl, mn, vn = _adamw(two(weights[k]), two(grads[k]), two(ms[k]), two(vs[k]), f"adamw_{k}")
        delta[k], new_m[k], new_v[k] = dl.reshape(shp), mn.reshape(shp), vn.reshape(shp)
        grads[k] = gr.reshape(shp)
    like = [weights[k] for k in SMALL_ORDER]
    packs = [_pack([src[k] for k in SMALL_ORDER], d) for src in (weights, grads, ms, vs)]
    outs = _adamw(*packs, "adamw_small")
    for res, o in zip((delta, new_m, new_v), outs[1:]):
        res.update(dict(zip(SMALL_ORDER, _unpack(o, like))))

    return (loss, dx[None], *[grads[k] for k in names], *[delta[k] for k in names],
            *[new_m[k] for k in names], *[new_v[k] for k in names])
```

```python
import math

import jax
import jax.numpy as jnp
from jax import lax
from jax.experimental import pallas as pl
from jax.experimental.pallas import tpu as pltpu

F32 = jnp.float32
BF16 = jnp.bfloat16
EPS = 1e-6
HEAD = 64
LANES = 128
ATT_TK = 256
ATT_TQ = 512
ATT_ROWS = 128
ALIVE_LOG = -105.0
DEPTH = 2
VMEM_LIMIT = 56 * 1024 * 1024
MESH = pl.DeviceIdType.MESH
ANY = pl.BlockSpec(memory_space=pl.ANY)

ADAM_LR = 0.001
ADAM_B1 = 0.9
ADAM_B2 = 0.999
ADAM_EPS = 1e-08
ADAM_WD = 0.01
ADAM_STEP = 10


def _pcall(body, **kw):
    return pl.pallas_call(body, **kw)


def _cp(n_axes):
    return pltpu.CompilerParams(dimension_semantics=("arbitrary",) * n_axes, vmem_limit_bytes=VMEM_LIMIT)


def _tile(n, pref):
    return pref if n % pref == 0 else n


def _split_dot(a, b, passes):
    out = None
    rem = a
    for _ in range(passes):
        hi = rem.astype(BF16)
        t = jnp.dot(hi, b, preferred_element_type=F32)
        out = t if out is None else out + t
        rem = rem - hi.astype(F32)
    return out


def _group_mat():
    r = lax.broadcasted_iota(jnp.int32, (LANES, LANES), 0) // HEAD
    c = lax.broadcasted_iota(jnp.int32, (LANES, LANES), 1) // HEAD
    return jnp.where(r == c, 1.0 / HEAD, 0.0).astype(BF16)


def _group_mean(v, gm):
    return _split_dot(v, gm, 1)


def _sigmoid(z):
    return 1.0 / (1.0 + jnp.exp(-z))


def _dot_nt(a, b):
    return lax.dot_general(a, b, (((1,), (1,)), ((), ())), preferred_element_type=F32)


def _dot_tn(a, b):
    return lax.dot_general(a, b, (((0,), (0,)), ((), ())), preferred_element_type=F32)


def _cast_into_full(w, chip, axis, name):
    _, r, c = w.shape
    tr = _tile(r, 256)
    nb = r // tr
    full = (DEPTH, 4 * r, c) if axis == 0 else (DEPTH, r, 4 * c)

    def body(k_ref, w_ref, o_ref):
        o_ref[...] = w_ref[...].astype(BF16)

    out_map = (lambda l, i, k: (l, k[0] * nb + i, 0)) if axis == 0 else (lambda l, i, k: (l, i, k[0]))
    grid_spec = pltpu.PrefetchScalarGridSpec(
        num_scalar_prefetch=1, grid=(DEPTH, nb),
        in_specs=[pl.BlockSpec((1, tr, c), lambda l, i, k: (l, i, 0))],
        out_specs=pl.BlockSpec((1, tr, c), out_map))
    return _pcall(body, name=name, grid_spec=grid_spec, out_shape=jax.ShapeDtypeStruct(full, BF16),
                  compiler_params=_cp(2))(chip.reshape(1).astype(jnp.int32), w)


def _rms_bwd_rows(dh, xv, g):
    r = lax.rsqrt(jnp.mean(xv * xv, axis=-1, keepdims=True) + EPS)
    xn = xv * r
    dxn = dh * g
    dx = r * (dxn - xn * jnp.mean(dxn * xn, axis=-1, keepdims=True))
    return dx, dh * xn


def _colsum8(v):
    tm, d = v.shape
    return jnp.sum(v.reshape(tm // 8, 8, d), axis=0)


def _inproj(x, g, w, layer, name, riders=()):
    s, d = x.shape
    n = w.shape[2]
    sw = d // 2
    ns = n // sw
    tm = _tile(s, 512)

    def body(*refs):
        own, riders_end = _riders_run(riders, refs, 3, 2, 0, pl.program_id(0), s // tm)
        x_ref, g_ref, w_ref, h_ref, o_ref = own
        xv = x_ref[...]
        r = lax.rsqrt(jnp.mean(xv * xv, axis=-1, keepdims=True) + EPS)
        h = (xv * r * g_ref[...]).astype(BF16)
        h_ref[...] = h
        for k in range(ns):
            o_ref[k] = jnp.dot(h, w_ref[0, :, k * sw:(k + 1) * sw], preferred_element_type=F32).astype(BF16)
        riders_end()

    r_ops, r_shapes, r_scratch, r_aliases = _riders_plumb(riders, 3, 2)
    outs = _pcall(body, name=name, grid=(s // tm,),
                  in_specs=[pl.BlockSpec((tm, d), lambda m: (m, 0)), pl.BlockSpec((1, d), lambda m: (0, 0)),
                            pl.BlockSpec((1, d, n), lambda m: (layer, 0, 0))] + [ANY] * len(r_ops),
                  out_specs=[pl.BlockSpec((tm, d), lambda m: (m, 0)), pl.BlockSpec((ns, tm, sw), lambda m: (0, m, 0))]
                  + [ANY] * len(r_shapes),
                  out_shape=[jax.ShapeDtypeStruct((s, d), BF16), jax.ShapeDtypeStruct((ns, s, sw), BF16)] + r_shapes,
                  input_output_aliases=r_aliases, scratch_shapes=r_scratch,
                  compiler_params=_cp(1))(x, g, w, *r_ops)
    return outs[0], outs[1], list(outs[2:])


def _softplus_parts(z):
    lm = jnp.minimum(-z, 0.0) - jnp.log(1.0 + jnp.exp(-jnp.abs(z)))
    return lm, lm + z


def _attn_tiles(s):
    tk = _tile(s, ATT_TK)
    tq = _tile(s, ATT_TQ)
    return tk, tq, tq // tk, min(ATT_ROWS, tq)


def _diag_work(chains, d, rows, tk):
    work = []
    for n, (_, r0) in enumerate(chains):
        if r0 + rows - 1 <= d * tk:
            continue
        kw = tk // 2 if (tk % 2 == 0 and r0 + rows <= d * tk + tk // 2) else tk
        if r0 >= d * tk + kw:
            mask = None
        else:
            row = lax.broadcasted_iota(jnp.int32, (rows, kw), 0)
            col = lax.broadcasted_iota(jnp.int32, (rows, kw), 1)
            mask = col + d * tk < row + r0
        work.append((n, kw, mask))
    return work


def _static_slots(chains, nd, rows, tk):
    diag, left = {}, {}
    for d in range(nd):
        for n, (_, r0) in enumerate(chains):
            if r0 + rows - 1 > d * tk:
                diag[d, n] = len(diag)
    for n, (_, r0) in enumerate(chains):
        if r0 < tk:
            left[n] = len(diag) + len(left)
    return diag, left, len(diag) + len(left)


def _both(mask, gate):
    if mask is None:
        return gate
    if gate is None:
        return mask
    return jnp.logical_and(mask, gate)


def _any_alive(rsums):
    m = rsums[0]
    for r in rsums[1:]:
        m = jnp.maximum(m, r)
    return jnp.max((m > ALIVE_LOG).astype(jnp.int32))


def _attn_fwd(proj, name, riders=()):
    _, s, sw = proj.shape
    nhp = sw // LANES
    tk, tq, nd, rows = _attn_tiles(s)
    nq = s // tq
    scale = 1.0 / math.sqrt(HEAD)

    def body(*refs):
        i = pl.program_id(1)
        own, riders_end = _riders_run(riders, refs, 3, 5, 1, pl.program_id(0) * nq + i, nhp * nq)
        q_ref, k_ref, v_ref, o_ref, tl_ref, nw_ref, sa_ref, sb_ref, acc_ref = own
        tri = (lax.broadcasted_iota(jnp.int32, (tk, tk), 0) >
               lax.broadcasted_iota(jnp.int32, (tk, tk), 1)).astype(BF16)
        lane = lax.broadcasted_iota(jnp.int32, (tq, LANES), 1)
        q = q_ref[0] * jnp.asarray(scale, BF16)
        qms = [jnp.where((lane // HEAD) == h, q, jnp.zeros_like(q)) for h in range(2)]
        acc_ref[...] = jnp.zeros_like(acc_ref)
        chains = [(h, r0) for h in range(2) for r0 in range(0, tq, rows)]
        qparts = [qms[h][r0:r0 + rows] for h, r0 in chains]

        def block(rsums, tiles, items):
            kjs = [k_ref[0, pl.ds(pl.multiple_of(j * tk, tk), tk), :] for j in tiles]
            vjs = [v_ref[0, pl.ds(pl.multiple_of(j * tk, tk), tk), :] for j in tiles]
            zs = [_dot_nt(qparts[n], kjs[t][:kw]) for n, t, kw, _, _, _ in items]
            lms, lss, css = [], [], []
            for z, (n, t, kw, mask, gate, _) in zip(zs, items):
                lm, ls = _softplus_parts(z)
                keep = _both(mask, gate)
                if keep is not None:
                    lm = jnp.where(keep, lm, 0.0)
                lms.append(lm)
                lss.append(ls)
                css.append(_split_dot(lm, tri[:kw, :kw], 2))
            cur = list(rsums)
            for lm, ls, cs, (n, t, kw, mask, gate, slot) in zip(lms, lss, css, items):
                h, r0 = chains[n]
                a = jnp.exp(ls + (cur[n] + cs))
                keep = _both(mask, gate)
                if keep is not None:
                    a = jnp.where(keep, a, 0.0)
                if slot is not None:
                    sa_ref[0, slot, :, 0:kw] = a.astype(BF16)
                    sb_ref[0, slot, :, 0:kw] = jnp.exp(ls).astype(BF16)
                acc_ref[h, r0:r0 + rows, :] += jnp.dot(a.astype(BF16), vjs[t][:kw], preferred_element_type=F32)
                cur[n] = cur[n] + jnp.sum(lm, axis=1, keepdims=True)
            return tuple(cur)

        everyone = [(n, 0, tk, None, None, None) for n in range(len(chains))]
        dslot, lslot, _ = _static_slots(chains, nd, rows, tk)
        upper = [n for n, (_, r0) in enumerate(chains) if r0 >= tk]
        lower = [n for n, (_, r0) in enumerate(chains) if r0 < tk]
        left = jnp.maximum(i * nd - 1, 0)
        rsums = (jnp.zeros((rows, 1), F32),) * len(chains)
        for d in reversed(range(1, nd)):
            rsums = block(rsums, [i * nd + d],
                          [(n, 0, kw, m, None, dslot[d, n]) for n, kw, m in _diag_work(chains, d, rows, tk)])
        rsums = block(rsums, [i * nd, left],
                      [(n, 0, kw, m, None, dslot[0, n]) for n, kw, m in _diag_work(chains, 0, rows, tk)]
                      + [(n, 1, tk, None, i > 0, lslot[n]) for n in lower])

        if upper:
            too = (i > 0) & (_any_alive([rsums[n] for n in upper]) > 0)
            rsums = lax.cond(too, lambda rs: block(rs, [left], [(n, 0, tk, None, None, None) for n in upper]),
                             lambda rs: rs, rsums)
            too = too.astype(jnp.int32)
        else:
            too = jnp.int32(0)

        def walk(c):
            jj, rs, _ = c
            rs = block(rs, [i * nd - 2 - jj], everyone)
            return jj + 1, rs, _any_alive(rs)

        whole, rsums, _ = lax.while_loop(lambda c: (c[0] < i * nd - 1) & (c[2] > 0), walk,
                                         (jnp.int32(0), rsums, _any_alive(rsums)))
        for n, (h, r0) in enumerate(chains):
            tl_ref[h, r0:r0 + rows, :] = rsums[n]
        nw_ref[0] = (jnp.zeros((8, LANES), jnp.int32) + (2 * whole + too)).astype(F32)
        o_ref[...] = jnp.where(lane < HEAD, acc_ref[0], acc_ref[1]).astype(BF16)
        riders_end()

    r_ops, r_shapes, r_scratch, r_aliases = _riders_plumb(riders, 3, 5)
    nslots = _static_slots([(h, r0) for h in range(2) for r0 in range(0, tq, rows)], nd, rows, tk)[2]
    kept = pl.BlockSpec((1, nslots, rows, tk), lambda hp, i: (hp * nq + i, 0, 0, 0))
    outs = _pcall(
        body, name=name, grid=(nhp, nq),
        in_specs=[pl.BlockSpec((1, tq, LANES), lambda hp, i: (4, i, hp)),
                  pl.BlockSpec((1, s, LANES), lambda hp, i: (5, 0, hp)),
                  pl.BlockSpec((1, s, LANES), lambda hp, i: (6, 0, hp))] + [ANY] * len(r_ops),
        out_specs=[pl.BlockSpec((tq, LANES), lambda hp, i: (i, hp)),
                   pl.BlockSpec((2, tq, 1), lambda hp, i: (hp, i, 0)),
                   pl.BlockSpec((1, 8, LANES), lambda hp, i: (hp * nq + i, 0, 0)), kept, kept]
        + [ANY] * len(r_shapes),
        out_shape=[jax.ShapeDtypeStruct((s, sw), BF16), jax.ShapeDtypeStruct((2 * nhp, s, 1), F32),
                   jax.ShapeDtypeStruct((nhp * nq, 8, LANES), F32)]
        + [jax.ShapeDtypeStruct((nhp * nq, nslots, rows, tk), BF16)] * 2 + r_shapes,
        input_output_aliases=r_aliases,
        scratch_shapes=[pltpu.VMEM((2, tq, LANES), F32)] + r_scratch,
        compiler_params=_cp(2))(proj, proj, proj, *r_ops)
    return outs[0], outs[1], outs[2], (outs[3], outs[4]), list(outs[5:])


def _conv_rows(cc_ref, ch_ref, w_ref, b_ref, r, tc):
    r0 = pl.multiple_of(r * tc, tc)
    u = cc_ref[0, pl.ds(r0, tc), :].astype(F32) * ch_ref[0, pl.ds(r0, tc), :].astype(F32)
    p0 = pl.multiple_of(jnp.maximum(r0 - 16, 0), 16)
    up = cc_ref[0, pl.ds(p0, 16), :].astype(F32) * ch_ref[0, pl.ds(p0, 16), :].astype(F32)
    up = up * (r > 0).astype(F32)
    prev1 = up[15:16, :]
    prev2 = up[14:15, :]
    rid = lax.broadcasted_iota(jnp.int32, u.shape, 0)
    s1 = jnp.where(rid == 0, prev1, pltpu.roll(u, 1, axis=0))
    s2 = jnp.where(rid == 0, prev2, jnp.where(rid == 1, prev1, pltpu.roll(u, 2, axis=0)))
    cv = b_ref[...] + s2 * w_ref[0:1, :] + s1 * w_ref[1:2, :] + u * w_ref[2:3, :]
    return r0, u, s1, s2, cv


def _mix_fwd(proj, ya, conv_w, conv_b, bg, name, riders=()):
    _, s, sw = proj.shape
    nh = sw // LANES
    tc = _tile(s, 256)

    def body(*refs):
        c = pl.program_id(0)
        own, riders_end = _riders_run(riders, refs, 9, 1, 0, c, 2 * nh)
        cb_ref, cc_ref, ch_ref, cz_ref, ya_ref, az_ref, w_ref, b_ref, g_ref, y_ref = own
        gm = _group_mat()

        def finish(r0, yv, zg):
            n = yv * lax.rsqrt(_group_mean(yv * yv, gm) + EPS)
            y_ref[pl.ds(r0, tc), :] = (n * g_ref[...] * (zg * _sigmoid(zg))).astype(BF16)

        @pl.when(c < nh)
        def _():
            def step(r, carry):
                r0, _, _, _, cv = _conv_rows(cc_ref, ch_ref, w_ref, b_ref, r, tc)
                yc = cb_ref[0, pl.ds(r0, tc), :].astype(F32) * cv
                finish(r0, yc, cz_ref[0, pl.ds(r0, tc), :].astype(F32))
                return carry
            lax.fori_loop(0, s // tc, step, 0)

        @pl.when(c >= nh)
        def _():
            def step(r, carry):
                r0 = pl.multiple_of(r * tc, tc)
                finish(r0, ya_ref[pl.ds(r0, tc), :].astype(F32), az_ref[0, pl.ds(r0, tc), :].astype(F32))
                return carry
            lax.fori_loop(0, s // tc, step, 0)

        riders_end()

    def sec(k):
        return pl.BlockSpec((1, s, LANES), lambda c: (k, 0, jnp.minimum(c, nh - 1)))

    r_ops, r_shapes, r_scratch, r_aliases = _riders_plumb(riders, 9, 1)
    outs = _pcall(
        body, name=name, grid=(2 * nh,),
        in_specs=[sec(0), sec(1), sec(2), sec(3),
                  pl.BlockSpec((s, LANES), lambda c: (0, jnp.maximum(c - nh, 0))),
                  pl.BlockSpec((1, s, LANES), lambda c: (7, 0, jnp.maximum(c - nh, 0))),
                  pl.BlockSpec((3, LANES), lambda c: (0, jnp.minimum(c, nh - 1))),
                  pl.BlockSpec((1, LANES), lambda c: (0, jnp.minimum(c, nh - 1))),
                  pl.BlockSpec((1, LANES), lambda c: (0, c))] + [ANY] * len(r_ops),
        out_specs=[pl.BlockSpec((s, LANES), lambda c: (0, c))] + [ANY] * len(r_shapes),
        out_shape=[jax.ShapeDtypeStruct((s, 2 * sw), BF16)] + r_shapes,
        input_output_aliases=r_aliases, scratch_shapes=r_scratch, compiler_params=_cp(1),
    )(proj, proj, proj, proj, ya, proj, conv_w, conv_b, bg, *r_ops)
    return outs[0], list(outs[1:])


def _outproj(y, w, layer, x, g, name):
    s, d = x.shape
    tm = _tile(s, 512)

    def body(y_ref, w_ref, x_ref, g_ref, x1_ref, hn_ref):
        x1 = x_ref[...] + jnp.dot(y_ref[...], w_ref[0], preferred_element_type=F32)
        x1_ref[...] = x1
        r = lax.rsqrt(jnp.mean(x1 * x1, axis=-1, keepdims=True) + EPS)
        hn_ref[...] = (x1 * r * g_ref[...]).astype(BF16)

    row = lambda m: (m, 0)
    fix = lambda m: (0, 0)
    return _pcall(body, name=name, grid=(s // tm,),
                  in_specs=[pl.BlockSpec((tm, d), row), pl.BlockSpec((1, d, d), lambda m: (layer, 0, 0)),
                            pl.BlockSpec((tm, d), row), pl.BlockSpec((1, d), fix)],
                  out_specs=[pl.BlockSpec((tm, d), row), pl.BlockSpec((tm, d), row)],
                  out_shape=[jax.ShapeDtypeStruct((s, d), F32), jax.ShapeDtypeStruct((s, d), BF16)],
                  compiler_params=_cp(1))(y, w, x, g)


def _ple_fwd(hn, w_pg, b_pg, p, w_pe, layer, x1, name):
    s, d = x1.shape
    pd = p.shape[2]
    tm = _tile(s, 512)

    def body(hn_ref, wg_ref, b_ref, p_ref, we_ref, x1_ref, x2_ref, gate_ref, e_ref):
        gate = _sigmoid(jnp.dot(hn_ref[...], wg_ref[0], preferred_element_type=F32) + b_ref[...])
        e = jnp.dot(p_ref[0].astype(BF16), we_ref[0], preferred_element_type=F32)
        x2_ref[...] = x1_ref[...] + gate * e
        gate_ref[...] = gate.astype(BF16)
        e_ref[...] = e.astype(BF16)

    row = lambda m: (m, 0)
    fix = lambda m: (0, 0)
    return _pcall(body, name=name, grid=(s // tm,),
                  in_specs=[pl.BlockSpec((tm, d), row), pl.BlockSpec((1, d, d), lambda m: (layer, 0, 0)),
                            pl.BlockSpec((1, d), fix), pl.BlockSpec((1, tm, pd), lambda m: (layer, m, 0)),
                            pl.BlockSpec((1, pd, d), lambda m: (layer, 0, 0)), pl.BlockSpec((tm, d), row)],
                  out_specs=[pl.BlockSpec((tm, d), row)] * 3,
                  out_shape=[jax.ShapeDtypeStruct((s, d), F32), jax.ShapeDtypeStruct((s, d), BF16),
                             jax.ShapeDtypeStruct((s, d), BF16)],
                  compiler_params=_cp(1))(hn, w_pg, b_pg, p, w_pe, x1)


def _loss_head(x, tgt, g, name):
    s, d = x.shape
    tm = _tile(s, 512)

    def body(x_ref, t_ref, g_ref, l_ref, dx_ref, dg_ref):
        m = pl.program_id(0)

        @pl.when(m == 0)
        def _():
            l_ref[...] = jnp.zeros_like(l_ref)
            dg_ref[...] = jnp.zeros_like(dg_ref)

        xv = x_ref[...]
        gv = g_ref[...]
        r = lax.rsqrt(jnp.mean(xv * xv, axis=-1, keepdims=True) + EPS)
        xn = xv * r
        err = xn * gv - t_ref[...]
        l_ref[...] += jnp.sum(err * err)
        dy = err * (1.0 / d)
        dxn = dy * gv
        dx_ref[...] = r * (dxn - xn * jnp.mean(dxn * xn, axis=-1, keepdims=True))
        dg_ref[...] += _colsum8(dy * xn)

    row = lambda m: (m, 0)
    fix = lambda m: (0, 0)
    return _pcall(body, name=name, grid=(s // tm,),
                  in_specs=[pl.BlockSpec((tm, d), row), pl.BlockSpec((tm, d), row), pl.BlockSpec((1, d), fix)],
                  out_specs=[pl.BlockSpec((8, LANES), fix), pl.BlockSpec((tm, d), row), pl.BlockSpec((8, d), fix)],
                  out_shape=[jax.ShapeDtypeStruct((8, LANES), F32), jax.ShapeDtypeStruct((s, d), F32),
                             jax.ShapeDtypeStruct((8, d), F32)],
                  compiler_params=_cp(1))(x, tgt, g)


def _ple_bwd(dx2, gate, e, x1, w_pg, g_ple, w_out, layer, name, riders=()):
    s, d = dx2.shape
    tm = _tile(s, 512)

    def body(*refs):
        m = pl.program_id(0)
        own, riders_end = _riders_run(riders, refs, 7, 6, 0, m, s // tm)
        (dx2_ref, gate_ref, e_ref, x1_ref, wg_ref, g_ref, wo_ref,
         du_ref, de_ref, dx1_ref, dy_ref, db_ref, dg_ref) = own

        @pl.when(m == 0)
        def _():
            db_ref[...] = jnp.zeros_like(db_ref)
            dg_ref[...] = jnp.zeros_like(dg_ref)

        dx2v = dx2_ref[...]
        gate = gate_ref[...].astype(F32)
        du = dx2v * e_ref[...].astype(F32) * gate * (1.0 - gate)
        de_ref[...] = (dx2v * gate).astype(BF16)
        dub = du.astype(BF16)
        du_ref[...] = dub
        db_ref[...] += _colsum8(du)
        dhn = _dot_nt(dub, wg_ref[0])
        dxr, dgr = _rms_bwd_rows(dhn, x1_ref[...], g_ref[...])
        dx1 = dx2v + dxr
        dx1_ref[...] = dx1
        dg_ref[...] += _colsum8(dgr)
        dy_ref[...] = _dot_nt(dx1.astype(BF16), wo_ref[0]).astype(BF16)
        riders_end()

    row = lambda m: (m, 0)
    fix = lambda m: (0, 0)
    t = pl.BlockSpec((tm, d), row)
    r_ops, r_shapes, r_scratch, r_aliases = _riders_plumb(riders, 7, 6)
    outs = _pcall(body, name=name, grid=(s // tm,),
                  in_specs=[t, t, t, t, pl.BlockSpec((1, d, d), lambda m: (layer, 0, 0)), pl.BlockSpec((1, d), fix),
                            pl.BlockSpec((1, d, d), lambda m: (layer, 0, 0))] + [ANY] * len(r_ops),
                  out_specs=[t, t, t, t, pl.BlockSpec((8, d), fix), pl.BlockSpec((8, d), fix)] + [ANY] * len(r_shapes),
                  out_shape=[jax.ShapeDtypeStruct((s, d), BF16), jax.ShapeDtypeStruct((s, d), BF16),
                             jax.ShapeDtypeStruct((s, d), F32), jax.ShapeDtypeStruct((s, d), BF16),
                             jax.ShapeDtypeStruct((8, d), F32), jax.ShapeDtypeStruct((8, d), F32)] + r_shapes,
                  input_output_aliases=r_aliases, scratch_shapes=r_scratch,
                  compiler_params=_cp(1))(dx2, gate, e, x1, w_pg, g_ple, w_out, *r_ops)
    return tuple(outs[:6]) + (list(outs[6:]),)


def _mm_tn(a, b, name, a_layer=None):
    s, ka = a.shape[-2:]
    n = b.shape[1]
    tn = _tile(n, 1024)
    ns = n // tn
    tk = _tile(s, 1024)
    nk = s // tk

    def body(a_ref, b_ref, o_ref, acc_ref):
        k = pl.program_id(1)

        @pl.when(k == 0)
        def _():
            acc_ref[...] = jnp.zeros_like(acc_ref)

        av = a_ref[...] if a_layer is None else a_ref[0]
        acc_ref[...] += _dot_tn(av.astype(BF16), b_ref[...].astype(BF16))

        @pl.when(k == nk - 1)
        def _():
            o_ref[...] = acc_ref[...]

    a_spec = (pl.BlockSpec((tk, ka), lambda j, k: (k, 0)) if a_layer is None
              else pl.BlockSpec((1, tk, ka), lambda j, k: (a_layer, k, 0)))
    return _pcall(body, name=name, grid=(ns, nk),
                  in_specs=[a_spec, pl.BlockSpec((tk, tn), lambda j, k: (k, j))],
                  out_specs=pl.BlockSpec((ka, tn), lambda j, k: (0, j)),
                  out_shape=jax.ShapeDtypeStruct((ka, n), F32),
                  scratch_shapes=[pltpu.VMEM((ka, tn), F32)], compiler_params=_cp(2))(a, b)


def _norm_gate_bwd(dy, yv, zg, g, gm):
    r = lax.rsqrt(_group_mean(yv * yv, gm) + EPS)
    n = yv * r
    sg = _sigmoid(zg)
    sil = zg * sg
    dzg = dy * n * g * (sg * (1.0 + zg * (1.0 - sg)))
    dn = dy * g * sil
    dyv = r * (dn - n * _group_mean(dn * n, gm))
    return dyv, dzg, dy * n * sil


def _convmix_bwd(dy, proj, conv_w, conv_b, bg, name, riders=()):
    _, s, sw = proj.shape
    nh = sw // LANES
    tc = _tile(s, 256)
    nr = s // tc

    def body(*refs):
        own, riders_end = _riders_run(riders, refs, 8, 4, 1, pl.program_id(0), nh)
        (dy_ref, cb_ref, cc_ref, ch_ref, cz_ref, w_ref, b_ref, g_ref,
         dp_ref, dw_ref, db_ref, dg_ref, dcv_ref) = own
        gm = _group_mat()
        dcv_ref[pl.ds(s, 8), :] = jnp.zeros((8, LANES), F32)

        def pass1(r, carry):
            dw0, dw1, dw2, db, dg = carry
            r0, u, s1, s2, cv = _conv_rows(cc_ref, ch_ref, w_ref, b_ref, r, tc)
            cb = cb_ref[0, pl.ds(r0, tc), :].astype(F32)
            dyc, dcz, dgr = _norm_gate_bwd(dy_ref[pl.ds(r0, tc), :].astype(F32), cb * cv,
                                           cz_ref[0, pl.ds(r0, tc), :].astype(F32), g_ref[...], gm)
            dp_ref[0, pl.ds(r0, tc), :] = (dyc * cv).astype(BF16)
            dp_ref[3, pl.ds(r0, tc), :] = dcz.astype(BF16)
            dcv = dyc * cb
            dcv_ref[pl.ds(r0, tc), :] = dcv
            return (dw0 + _colsum8(dcv * s2), dw1 + _colsum8(dcv * s1), dw2 + _colsum8(dcv * u),
                    db + _colsum8(dcv), dg + _colsum8(dgr))

        z8 = jnp.zeros((8, LANES), F32)
        dw0, dw1, dw2, db, dg = lax.fori_loop(0, nr, pass1, (z8, z8, z8, z8, z8))
        dw_ref[0] = dw0
        dw_ref[1] = dw1
        dw_ref[2] = dw2
        db_ref[...] = db
        dg_ref[...] = dg

        def pass2(r, carry):
            r0 = pl.multiple_of(r * tc, tc)
            dcv = dcv_ref[pl.ds(r0, tc), :]
            nxt = dcv_ref[pl.ds(pl.multiple_of(r0 + tc, 8), 8), :]
            rid = lax.broadcasted_iota(jnp.int32, dcv.shape, 0)
            n1 = jnp.where(rid == tc - 1, nxt[0:1, :], pltpu.roll(dcv, tc - 1, axis=0))
            n2 = jnp.where(rid == tc - 1, nxt[1:2, :],
                           jnp.where(rid == tc - 2, nxt[0:1, :], pltpu.roll(dcv, tc - 2, axis=0)))
            du = dcv * w_ref[2:3, :] + n1 * w_ref[1:2, :] + n2 * w_ref[0:1, :]
            dp_ref[1, pl.ds(r0, tc), :] = (du * ch_ref[0, pl.ds(r0, tc), :].astype(F32)).astype(BF16)
            dp_ref[2, pl.ds(r0, tc), :] = (du * cc_ref[0, pl.ds(r0, tc), :].astype(F32)).astype(BF16)
            return carry

        lax.fori_loop(0, nr, pass2, 0)
        riders_end()

    def sec(k):
        return pl.BlockSpec((1, s, LANES), lambda c: (k, 0, c))

    col = lambda c: (0, c)
    r_ops, r_shapes, r_scratch, r_aliases = _riders_plumb(riders, 8, 4)
    outs = _pcall(
        body, name=name, grid=(nh,),
        in_specs=[pl.BlockSpec((s, LANES), col), sec(0), sec(1), sec(2), sec(3),
                  pl.BlockSpec((3, LANES), col), pl.BlockSpec((1, LANES), col), pl.BlockSpec((1, LANES), col)]
        + [ANY] * len(r_ops),
        out_specs=[pl.BlockSpec((4, s, LANES), lambda c: (0, 0, c)), pl.BlockSpec((3, 8, LANES), lambda c: (0, 0, c)),
                   pl.BlockSpec((8, LANES), col), pl.BlockSpec((8, LANES), col)] + [ANY] * len(r_shapes),
        out_shape=[jax.ShapeDtypeStruct((8, s, sw), BF16), jax.ShapeDtypeStruct((3, 8, sw), F32),
                   jax.ShapeDtypeStruct((8, sw), F32), jax.ShapeDtypeStruct((8, sw), F32)] + r_shapes,
        input_output_aliases=r_aliases,
        scratch_shapes=[pltpu.VMEM((s + 8, LANES), F32)] + r_scratch, compiler_params=_cp(1),
    )(dy, proj, proj, proj, proj, conv_w, conv_b, bg, *r_ops)
    return tuple(outs[:4]) + (list(outs[4:]),)


def _attn_bwd(proj, dy, ya, tl, walked, kept, bg, buf, name, riders=()):
    _, s, sw = proj.shape
    nhp = sw // LANES
    tk, t, nd, rows_c = _attn_tiles(s)
    nq = s // t
    scale = 1.0 / math.sqrt(HEAD)

    def body(*refs):
        step = pl.program_id(1)
        i = nq - 1 - step
        own, riders_end = _riders_run(riders, refs, 12, 2, 3, pl.program_id(0) * nq + step, nhp * nq)
        (q_ref, k_ref, v_ref, az_ref, dy_ref, ya_ref, tl_ref, nw_ref, g_ref, buf_ref, sa_ref, sb_ref, out_ref, dg_ref,
         dka_ref, dva_ref, dqa_ref) = own

        @pl.when(step == 0)
        def _():
            dka_ref[...] = jnp.zeros_like(dka_ref)
            dva_ref[...] = jnp.zeros_like(dva_ref)
            dg_ref[...] = jnp.zeros_like(dg_ref)

        dyv, dzg, dgr = _norm_gate_bwd(dy_ref[...].astype(F32), ya_ref[...].astype(F32), az_ref[0].astype(F32),
                                       g_ref[...], _group_mat())
        out_ref[3] = dzg.astype(BF16)
        dg_ref[...] += _colsum8(dgr)

        tri = (lax.broadcasted_iota(jnp.int32, (tk, tk), 0) <=
               lax.broadcasted_iota(jnp.int32, (tk, tk), 1)).astype(BF16)
        lane = lax.broadcasted_iota(jnp.int32, (t, LANES), 1)
        q = q_ref[0] * jnp.asarray(scale, BF16)
        do = dyv.astype(BF16)
        qms = [jnp.where((lane // HEAD) == h, q, jnp.zeros_like(q)) for h in range(2)]
        doms = [jnp.where((lane // HEAD) == h, do, jnp.zeros_like(do)) for h in range(2)]
        dqa_ref[...] = jnp.zeros_like(dqa_ref)
        chains = [(h, r0) for h in range(2) for r0 in range(0, t, rows_c)]
        qparts = [qms[h][r0:r0 + rows_c] for h, r0 in chains]
        doparts = [doms[h][r0:r0 + rows_c] for h, r0 in chains]
        tots = [tl_ref[h, r0:r0 + rows_c, :] for h, r0 in chains]

        def block(carry, tiles, items):
            k0s = [pl.multiple_of(j * tk, tk) for j in tiles]
            kjs = [k_ref[0, pl.ds(k0, tk), :] for k0 in k0s]
            vjs = [v_ref[0, pl.ds(k0, tk), :] for k0 in k0s]
            zs = [_dot_nt(qparts[n], kjs[t][:kw]) for n, t, kw, _, _ in items]
            das = [_dot_nt(doparts[n], vjs[t][:kw]) for n, t, kw, _, _ in items]
            keeps = [_both(mask, gate) for _, _, _, mask, gate in items]
            lms, lss, cls = [], [], []
            for z, keep, (n, t, kw, _, _) in zip(zs, keeps, items):
                lm, ls = _softplus_parts(z)
                if keep is not None:
                    lm = jnp.where(keep, lm, 0.0)
                lms.append(lm)
                lss.append(ls)
                cls.append(_split_dot(lm, tri[:kw, :kw], 2))
            cur = list(carry)
            psums, abs_, gs, cgs = [], [], [], []
            for lm, ls, cl, da, keep, (n, t, kw, _, _) in zip(lms, lss, cls, das, keeps, items):
                psum, gsum = cur[n]
                a = jnp.exp(ls + (tots[n] - psum - cl))
                if keep is not None:
                    a = jnp.where(keep, a, 0.0)
                g = a * da
                psums.append(gsum)
                gs.append(g)
                abs_.append(a.astype(BF16))
                cgs.append(_split_dot(g, tri[:kw, :kw], 1))
                cur[n] = (psum + jnp.sum(lm, axis=1, keepdims=True), gsum + jnp.sum(g, axis=1, keepdims=True))
            dks, dvs = {}, {}
            for ls, a, g, cg, gsum, keep, (n, t, kw, _, _) in zip(lss, abs_, gs, cgs, psums, keeps, items):
                h, r0 = chains[n]
                dz = g - jnp.exp(ls) * (gsum + cg)
                if keep is not None:
                    dz = jnp.where(keep, dz, 0.0)
                dz = dz.astype(BF16)
                dqa_ref[h, r0:r0 + rows_c, :] += jnp.dot(dz, kjs[t][:kw], preferred_element_type=F32)
                dkh = _dot_tn(dz, qparts[n])
                dvh = _dot_tn(a, doparts[n])
                dks[t, kw] = dkh if (t, kw) not in dks else dks[t, kw] + dkh
                dvs[t, kw] = dvh if (t, kw) not in dvs else dvs[t, kw] + dvh
            for t, kw in dks:
                dka_ref[pl.ds(k0s[t], kw), :] += dks[t, kw]
                dva_ref[pl.ds(k0s[t], kw), :] += dvs[t, kw]
            return tuple(cur)

        def kept_block(carry, tiles, items):
            k0s = [pl.multiple_of(j * tk, tk) for j in tiles]
            kjs = [k_ref[0, pl.ds(k0, tk), :] for k0 in k0s]
            vjs = [v_ref[0, pl.ds(k0, tk), :] for k0 in k0s]
            das = [_dot_nt(doparts[n], vjs[t][:kw]) for n, t, kw, _, _, _ in items]
            cur = list(carry)
            gsums, kept_a, gs, cgs = [], [], [], []
            for da, (n, t, kw, _, _, slot) in zip(das, items):
                psum, gsum = cur[n]
                a = sa_ref[0, slot, :, 0:kw]
                g = a.astype(F32) * da
                gsums.append(gsum)
                kept_a.append(a)
                gs.append(g)
                cgs.append(_split_dot(g, tri[:kw, :kw], 1))
                cur[n] = (psum, gsum + jnp.sum(g, axis=1, keepdims=True))
            dks, dvs = {}, {}
            for a, g, cg, gsum, (n, t, kw, mask, gate, slot) in zip(kept_a, gs, cgs, gsums, items):
                h, r0 = chains[n]
                dz = g - sb_ref[0, slot, :, 0:kw].astype(F32) * (gsum + cg)
                keep = _both(mask, gate)
                if keep is not None:
                    dz = jnp.where(keep, dz, 0.0)
                dz = dz.astype(BF16)
                dqa_ref[h, r0:r0 + rows_c, :] += jnp.dot(dz, kjs[t][:kw], preferred_element_type=F32)
                dkh = _dot_tn(dz, qparts[n])
                dvh = _dot_tn(a, doparts[n])
                dks[t, kw] = dkh if (t, kw) not in dks else dks[t, kw] + dkh
                dvs[t, kw] = dvh if (t, kw) not in dvs else dvs[t, kw] + dvh
            for t, kw in dks:
                dka_ref[pl.ds(k0s[t], kw), :] += dks[t, kw]
                dva_ref[pl.ds(k0s[t], kw), :] += dvs[t, kw]
            return tuple(cur)

        z1 = jnp.zeros((rows_c, 1), F32)
        everyone = [(n, 0, tk, None, None) for n in range(len(chains))]
        dslot, lslot, _ = _static_slots(chains, nd, rows_c, tk)
        upper = [n for n, (_, r0) in enumerate(chains) if r0 >= tk]
        lower = [n for n, (_, r0) in enumerate(chains) if r0 < tk]
        left = jnp.maximum(i * nd - 1, 0)
        code = jnp.clip(jnp.max(nw_ref[0].astype(jnp.int32)), 0, 2 * left + 1)
        too = jnp.where(i > 0, code % 2, 0)
        whole = jnp.minimum(code // 2, left)
        carry = lax.fori_loop(left - whole, left, lambda j, c: block(c, [j], everyone), ((z1, z1),) * len(chains))
        if upper:
            carry = lax.cond(too > 0, lambda c: block(c, [left], [(n, 0, tk, None, None) for n in upper]),
                             lambda c: c, carry)
        carry = kept_block(carry, [left, i * nd],
                           [(n, 0, tk, None, i > 0, lslot[n]) for n in lower]
                           + [(n, 1, kw, m, None, dslot[0, n]) for n, kw, m in _diag_work(chains, 0, rows_c, tk)])
        for d in range(1, nd):
            carry = kept_block(carry, [i * nd + d],
                               [(n, 0, kw, m, None, dslot[d, n]) for n, kw, m in _diag_work(chains, d, rows_c, tk)])
        out_ref[0] = (jnp.where(lane < HEAD, dqa_ref[0], dqa_ref[1]) * scale).astype(BF16)
        own = pl.multiple_of(i * t, t)
        out_ref[1] = dka_ref[pl.ds(own, t), :].astype(BF16)
        out_ref[2] = dva_ref[pl.ds(own, t), :].astype(BF16)
        riders_end()

    def rows(sec):
        return pl.BlockSpec((1, t, LANES), lambda hp, st: (sec, nq - 1 - st, hp))

    def whole(sec):
        return pl.BlockSpec((1, s, LANES), lambda hp, st: (sec, 0, hp))

    r_ops, r_shapes, r_scratch, r_aliases = _riders_plumb(riders, 12, 2)
    kept_spec = pl.BlockSpec((1,) + kept[0].shape[1:], lambda hp, st: (hp * nq + nq - 1 - st, 0, 0, 0))
    outs = _pcall(
        body, name=name, grid=(nhp, nq),
        in_specs=[rows(4), whole(5), whole(6), rows(7),
                  pl.BlockSpec((t, LANES), lambda hp, st: (nq - 1 - st, hp + nhp)),
                  pl.BlockSpec((t, LANES), lambda hp, st: (nq - 1 - st, hp)),
                  pl.BlockSpec((2, t, 1), lambda hp, st: (hp, nq - 1 - st, 0)),
                  pl.BlockSpec((1, 8, LANES), lambda hp, st: (hp * nq + nq - 1 - st, 0, 0)),
                  pl.BlockSpec((1, LANES), lambda hp, st: (0, hp + nhp)), ANY, kept_spec, kept_spec]
        + [ANY] * len(r_ops),
        out_specs=[pl.BlockSpec((4, t, LANES), lambda hp, st: (1, nq - 1 - st, hp)),
                   pl.BlockSpec((8, LANES), lambda hp, st: (0, hp))] + [ANY] * len(r_shapes),
        out_shape=[jax.ShapeDtypeStruct(buf.shape, buf.dtype), jax.ShapeDtypeStruct((8, sw), F32)] + r_shapes,
        input_output_aliases={9: 0, **r_aliases},
        scratch_shapes=[pltpu.VMEM((s, LANES), F32), pltpu.VMEM((s, LANES), F32), pltpu.VMEM((2, t, LANES), F32)]
        + r_scratch,
        compiler_params=_cp(2))(proj, proj, proj, proj, dy, ya, tl, walked, bg, buf, kept[0], kept[1], *r_ops)
    return outs[0], outs[1], list(outs[2:])


def _grad_w_in(h, dproj, name):
    s, d = h.shape
    ns, _, sw = dproj.shape

    def body(h_ref, b_ref, o_ref, ht_ref):
        @pl.when(pl.program_id(0) == 0)
        def _():
            ht_ref[...] = h_ref[...].T

        o_ref[...] = jnp.dot(ht_ref[...], b_ref[0], preferred_element_type=F32)

    return _pcall(body, name=name, grid=(ns,),
                  in_specs=[pl.BlockSpec((s, d), lambda j: (0, 0)), pl.BlockSpec((1, s, sw), lambda j: (j, 0, 0))],
                  out_specs=pl.BlockSpec((d, sw), lambda j: (0, j)),
                  out_shape=jax.ShapeDtypeStruct((d, ns * sw), F32),
                  scratch_shapes=[pltpu.VMEM((d, s), BF16)], compiler_params=_cp(1))(h, dproj)


def _inproj_bwd(dproj, w, layer, x, g, dx1, name, riders=()):
    ns, s, sw = dproj.shape
    d = x.shape[1]
    tm = _tile(s, 512)

    def body(*refs):
        own, riders_end = _riders_run(riders, refs, 5, 2, 0, pl.program_id(0), s // tm)
        dp_ref, w_ref, x_ref, g_ref, dx1_ref, dx_ref, dg_ref = own

        @pl.when(pl.program_id(0) == 0)
        def _():
            dg_ref[...] = jnp.zeros_like(dg_ref)

        dh = _dot_nt(dp_ref[0], w_ref[0, :, 0:sw])
        for k in range(1, ns):
            dh = dh + _dot_nt(dp_ref[k], w_ref[0, :, k * sw:(k + 1) * sw])
        dxr, dgr = _rms_bwd_rows(dh, x_ref[...], g_ref[...])
        dx_ref[...] = dx1_ref[...] + dxr
        dg_ref[...] += _colsum8(dgr)
        riders_end()

    row = lambda m: (m, 0)
    fix = lambda m: (0, 0)
    r_ops, r_shapes, r_scratch, r_aliases = _riders_plumb(riders, 5, 2)
    outs = _pcall(body, name=name, grid=(s // tm,),
                  in_specs=[pl.BlockSpec((ns, tm, sw), lambda m: (0, m, 0)),
                            pl.BlockSpec((1, d, ns * sw), lambda m: (layer, 0, 0)),
                            pl.BlockSpec((tm, d), row), pl.BlockSpec((1, d), fix), pl.BlockSpec((tm, d), row)]
                  + [ANY] * len(r_ops),
                  out_specs=[pl.BlockSpec((tm, d), row), pl.BlockSpec((8, d), fix)] + [ANY] * len(r_shapes),
                  out_shape=[jax.ShapeDtypeStruct((s, d), F32), jax.ShapeDtypeStruct((8, d), F32)] + r_shapes,
                  input_output_aliases=r_aliases, scratch_shapes=r_scratch,
                  compiler_params=_cp(1))(dproj, w, x, g, dx1, *r_ops)
    return outs[0], outs[1], list(outs[2:])


def _adamw(w, g, m, v, name):
    r, c = w.shape
    tr = _tile(r, 256)
    c1 = 1.0 - ADAM_B1 ** ADAM_STEP
    c2 = 1.0 - ADAM_B2 ** ADAM_STEP

    def body(w_ref, g_ref, m_ref, v_ref, go_ref, d_ref, mo_ref, vo_ref):
        gv = g_ref[...]
        go_ref[...] = gv
        mn = ADAM_B1 * m_ref[...] + (1.0 - ADAM_B1) * gv
        vn = ADAM_B2 * v_ref[...] + (1.0 - ADAM_B2) * (gv * gv)
        d_ref[...] = -ADAM_LR * ((mn / c1) / (jnp.sqrt(vn / c2) + ADAM_EPS) + ADAM_WD * w_ref[...])
        mo_ref[...] = mn
        vo_ref[...] = vn

    t = pl.BlockSpec((tr, c), lambda i: (i, 0))
    return _pcall(body, name=name, grid=(r // tr,), in_specs=[t] * 4, out_specs=[t] * 4,
                  out_shape=[jax.ShapeDtypeStruct((r, c), F32)] * 4, compiler_params=_cp(1))(w, g, m, v)


def _add_half(grad, other, core, a, name):
    hr, hc = other.shape
    tr = _tile(hr, 256)
    nb = hr // tr

    def body(c_ref, g_ref, o_ref, out_ref, outb_ref):
        v = g_ref[...] + o_ref[...]
        out_ref[...] = v
        outb_ref[...] = v.astype(BF16)

    t = pl.BlockSpec((tr, hc), lambda i, c: (i, 0))
    own = (lambda i, c: (c[0] * nb + i, 0)) if HALF_AXES[a] == 0 else (lambda i, c: (i, c[0]))
    grid_spec = pltpu.PrefetchScalarGridSpec(
        num_scalar_prefetch=1, grid=(nb,), in_specs=[pl.BlockSpec((tr, hc), own), t], out_specs=[t, t])
    return _pcall(body, name=name, grid_spec=grid_spec,
                  out_shape=[jax.ShapeDtypeStruct((hr, hc), F32), jax.ShapeDtypeStruct((hr, hc), BF16)],
                  compiler_params=_cp(1))(core.reshape(1).astype(jnp.int32), grad, other)


def _sum_half(wide, parts, chip, core, layer, a, stack, name):
    _, sr, sc = parts.shape
    tr = _tile(sr, 256)
    nbs = sr // tr

    def body(k_ref, f_ref, p_ref, *rest):
        rest[-1][0] = ((f_ref[...] + p_ref[0].astype(F32)) + p_ref[1].astype(F32)) + p_ref[2].astype(F32)

    f_map = (lambda i, k: (i, k[0])) if SHARD_AXES[a] == 1 else (lambda i, k: (k[0] * nbs + i, 0))
    if HALF_AXES[a] == 0:
        shape, o_map = (DEPTH, 2 * sr, sc), (lambda i, k: (layer, k[1] * nbs + i, 0))
    else:
        shape, o_map = (DEPTH, sr, 2 * sc), (lambda i, k: (layer, i, k[1]))
    in_specs = [pl.BlockSpec((tr, sc), f_map), pl.BlockSpec((3, tr, sc), lambda i, k: (0, i, 0))]
    args = [wide, parts]
    aliases = {}
    if stack is not None:
        in_specs.append(ANY)
        args.append(stack)
        aliases = {3: 0}
    grid_spec = pltpu.PrefetchScalarGridSpec(
        num_scalar_prefetch=1, grid=(nbs,), in_specs=in_specs, out_specs=pl.BlockSpec((1, tr, sc), o_map))
    return _pcall(body, name=name, grid_spec=grid_spec, out_shape=jax.ShapeDtypeStruct(shape, F32),
                  input_output_aliases=aliases,
                  compiler_params=_cp(1))(jnp.stack([chip, core]).astype(jnp.int32), *args)


def _sum_slots(slots, name):
    n = slots.shape[0]

    def body(s_ref, o_ref):
        acc = s_ref[0]
        for i in range(1, n):
            acc = acc + s_ref[i]
        o_ref[...] = acc

    return _pcall(body, name=name, out_shape=jax.ShapeDtypeStruct(slots.shape[1:], F32))(slots)


def _place():
    return lax.axis_index("x"), lax.axis_index("y"), lax.axis_index("c")


def _shard_view(ref, axis, chip, size):
    if axis == 0:
        return ref.at[pl.ds(chip * size, size), :]
    return ref.at[:, pl.ds(chip * size, size)]


SHARD_AXES = (1, 0, 0, 1)
HALF_AXES = tuple(1 - ax for ax in SHARD_AXES)


class _Rider:
    def __init__(self, operands, out_shape, sems, phases, aliased=False):
        self.operands, self.out_shape, self.sems = list(operands), list(out_shape), list(sems)
        self.phases, self.aliased = phases, aliased


def _riders_plumb(riders, n_in, n_out):
    ops, out_shape, scratch, aliases = [], [], [], {}
    for r in riders:
        if r.aliased:
            for k in range(len(r.operands)):
                aliases[n_in + len(ops) + k] = n_out + len(out_shape) + k
        ops += r.operands
        out_shape += r.out_shape
        scratch += r.sems
    return ops, out_shape, scratch, aliases


def _riders_run(riders, refs, n_in, n_out, n_scr, step, nsteps):
    n_rin = sum(len(r.operands) for r in riders)
    n_rout = sum(len(r.out_shape) for r in riders)
    rin = refs[n_in:n_in + n_rin]
    o0 = n_in + n_rin
    rout = refs[o0 + n_out:o0 + n_out + n_rout]
    s0 = o0 + n_out + n_rout
    rsem = refs[s0 + n_scr:]
    own = list(refs[:n_in]) + list(refs[o0:o0 + n_out]) + list(refs[s0:s0 + n_scr])
    lasts = []
    for r in riders:
        ph = r.phases(rin[:len(r.operands)], rout[:len(r.out_shape)], rsem[:len(r.sems)])
        rin, rout, rsem = rin[len(r.operands):], rout[len(r.out_shape):], rsem[len(r.sems):]
        pl.when(step == 0)(ph[0])
        for mid in ph[1:-1]:
            pl.when(step == (3 * nsteps) // 4)(mid)
        lasts.append(ph[-1])

    def finish():
        for last in lasts:
            pl.when(step == nsteps - 1)(last)

    return own, finish


def _gather_phases(ins, outs, ssem, rsem, layer, which):
    n = len(ins)
    x, y, c = _place()
    me = 2 * x + y
    chips = [(1 - x, y), (x, 1 - y), (1 - x, 1 - y)]

    def piece(a, chip, half, of):
        ax = SHARD_AXES[which[a]]
        block = _shard_view(of[a].at[layer], ax, chip, of[a].shape[1 + ax] // 4)
        r = block.shape[0] // 2
        return block.at[pl.ds(half * r, r), :]

    def over_ici(a, j):
        cx, cy = chips[j]
        return pltpu.make_async_remote_copy(
            src_ref=piece(a, me, c, ins), dst_ref=piece(a, me, c, outs), send_sem=ssem.at[a, j],
            recv_sem=rsem.at[a, j], device_id=(cx, cy, c), device_id_type=MESH)

    def landed(a, j, half):
        cx, cy = chips[j]
        return piece(a, 2 * cx + cy, half, outs)

    def to_sibling(a, j):
        got = landed(a, j, c)
        return pltpu.make_async_remote_copy(
            src_ref=got, dst_ref=got, send_sem=ssem.at[a, 3 + j], recv_sem=rsem.at[a, 3 + j],
            device_id=(x, y, 1 - c), device_id_type=MESH)

    def wait_arrival(a, k, place):
        pltpu.make_async_remote_copy(src_ref=place, dst_ref=place, send_sem=ssem.at[a, k], recv_sem=rsem.at[a, k],
                                     device_id=(x, y, c), device_id_type=MESH).wait_recv()

    def start():
        for a in range(n):
            for j in range(3):
                over_ici(a, j).start()

    def pass_on():
        for a in range(n):
            for j in range(3):
                wait_arrival(a, j, landed(a, j, c))
                to_sibling(a, j).start()

    def finish():
        for a in range(n):
            for j in range(3):
                wait_arrival(a, 3 + j, landed(a, j, 1 - c))
        for a in range(n):
            for j in range(3):
                over_ici(a, j).wait_send()
                to_sibling(a, j).wait_send()

    return start, pass_on, finish


def _gather_rider(fulls, layer, which):
    n = len(fulls)
    return _Rider(fulls, [jax.ShapeDtypeStruct(f.shape, f.dtype) for f in fulls],
                  [pltpu.SemaphoreType.DMA((n, 6)), pltpu.SemaphoreType.DMA((n, 6))],
                  lambda ins, outs, sems: _gather_phases(ins, outs, sems[0], sems[1], layer, which), aliased=True)


def _ride_alone(rider, name):
    n = len(rider.operands)

    def body(*refs):
        for phase in rider.phases(refs[:n], refs[n:n + len(rider.out_shape)], refs[n + len(rider.out_shape):]):
            phase()

    return _pcall(body, name=name, in_specs=[ANY] * n, out_specs=[ANY] * len(rider.out_shape),
                  out_shape=rider.out_shape, scratch_shapes=rider.sems,
                  input_output_aliases={a: a for a in range(n)} if rider.aliased else {})(*rider.operands)


def _half_view(ref, a, half):
    n = ref.shape[HALF_AXES[a]] // 2
    if HALF_AXES[a] == 0:
        return ref.at[pl.ds(half * n, n), :]
    return ref.at[:, pl.ds(half * n, n)]


def _swap_rider(grads, which):
    n = len(grads)
    halves = []
    for g, w in zip(grads, which):
        sh = list(g.shape)
        sh[HALF_AXES[w]] //= 2
        halves.append(jax.ShapeDtypeStruct(tuple(sh), g.dtype))

    def phases(srcs, outs, sems):
        x, y, c = _place()

        def copy(a):
            return pltpu.make_async_remote_copy(
                src_ref=_half_view(srcs[a], which[a], 1 - c), dst_ref=outs[a], send_sem=sems[0].at[a],
                recv_sem=sems[1].at[a], device_id=(x, y, 1 - c), device_id_type=MESH)

        def start():
            for a in range(n):
                copy(a).start()

        def finish():
            for a in range(n):
                copy(a).wait()

        return start, finish

    return _Rider(grads, halves, [pltpu.SemaphoreType.DMA((n,)), pltpu.SemaphoreType.DMA((n,))], phases)


def _scatter_rider(sums, which):
    n = len(sums)
    shapes = []
    for f, w in zip(sums, which):
        sh = list(f.shape)
        sh[SHARD_AXES[w]] //= 4
        shapes.append(jax.ShapeDtypeStruct((3,) + tuple(sh), f.dtype))

    def phases(srcs, outs, sems):
        x, y, c = _place()
        chips = [(1 - x, y), (x, 1 - y), (1 - x, 1 - y)]

        def copy(a, j):
            cx, cy = chips[j]
            ax = SHARD_AXES[which[a]]
            src = _shard_view(srcs[a], ax, 2 * cx + cy, srcs[a].shape[ax] // 4)
            return pltpu.make_async_remote_copy(src_ref=src, dst_ref=outs[a].at[j], send_sem=sems[0].at[a, j],
                                                recv_sem=sems[1].at[a, j], device_id=(cx, cy, c), device_id_type=MESH)

        def start():
            for a in range(n):
                for j in range(3):
                    copy(a, j).start()

        def finish():
            for a in range(n):
                for j in range(3):
                    copy(a, j).wait()

        return start, finish

    return _Rider(sums, shapes, [pltpu.SemaphoreType.DMA((n, 3)), pltpu.SemaphoreType.DMA((n, 3))], phases)


def _pair_halves(stacks):
    n = len(stacks)

    def body(*refs):
        ins, outs = refs[:n], refs[n:2 * n]
        ssem, rsem = refs[2 * n:]
        x, y, c = _place()
        cps = [pltpu.make_async_remote_copy(
            src_ref=_half_view(ins[a].at[l], a, c), dst_ref=_half_view(outs[a].at[l], a, c), send_sem=ssem.at[a, l],
            recv_sem=rsem.at[a, l], device_id=(x, y, 1 - c), device_id_type=MESH)
            for a in range(n) for l in range(DEPTH)]
        for cp in cps:
            cp.start()
        for a in range(n):
            for l in range(DEPTH):
                got = _half_view(outs[a].at[l], a, 1 - c)
                pltpu.make_async_remote_copy(src_ref=got, dst_ref=got, send_sem=ssem.at[a, l], recv_sem=rsem.at[a, l],
                                             device_id=(x, y, 1 - c), device_id_type=MESH).wait_recv()
        for cp in cps:
            cp.wait_send()

    return _pcall(body, name="pair_halves", in_specs=[ANY] * n, out_specs=[ANY] * n,
                  out_shape=[jax.ShapeDtypeStruct(st.shape, st.dtype) for st in stacks],
                  input_output_aliases={a: a for a in range(n)},
                  scratch_shapes=[pltpu.SemaphoreType.DMA((n, DEPTH)), pltpu.SemaphoreType.DMA((n, DEPTH))])(*stacks)


class _GradReduce:
    def __init__(self, chip, core):
        self.chip, self.core = chip, core
        self.stacks = [None] * len(SHARD_AXES)

    def add(self, layer, grads, which, got):
        return [(layer, w) + tuple(_add_half(g, o, self.core, w, f"add_half_{layer}_{w}"))
                for g, o, w in zip(grads, got, which)]

    def finish(self, sums, partials):
        for (layer, w, wide, _), pr in zip(sums, partials):
            self.stacks[w] = _sum_half(wide, pr, self.chip, self.core, layer, w, self.stacks[w], f"sum_half_{layer}_{w}")

    def result(self):
        return _pair_halves(self.stacks)


def _exchange_small(pack, name, riders=()):
    nd = 8

    def body(*refs):
        own, riders_end = _riders_run(riders, refs, 1, 1, 2, jnp.int32(0), 1)
        p_ref, o_ref, ssem, rsem = own
        x, y, c = _place()
        me = 4 * x + 2 * y + c
        o_ref[me] = p_ref[...]
        cps = []
        for j in range(1, nd):
            px, py, pc = x ^ (j >> 2), y ^ ((j >> 1) & 1), c ^ (j & 1)
            cps.append(pltpu.make_async_remote_copy(
                src_ref=p_ref, dst_ref=o_ref.at[me], send_sem=ssem.at[j - 1], recv_sem=rsem.at[j - 1],
                device_id=(px, py, pc), device_id_type=MESH))
        for cp in cps:
            cp.start()
        for j in range(1, nd):
            peer = me ^ j
            got = o_ref.at[peer]
            pltpu.make_async_remote_copy(src_ref=got, dst_ref=got, send_sem=ssem.at[j - 1], recv_sem=rsem.at[j - 1],
                                         device_id=(x, y, c), device_id_type=MESH).wait_recv()
        for cp in cps:
            cp.wait_send()
        riders_end()

    vm = pl.BlockSpec(memory_space=pltpu.VMEM)
    r_ops, r_shapes, r_scratch, r_aliases = _riders_plumb(riders, 1, 1)
    outs = _pcall(body, name=name, in_specs=[vm] + [ANY] * len(r_ops), out_specs=[vm] + [ANY] * len(r_shapes),
                  out_shape=[jax.ShapeDtypeStruct((nd,) + pack.shape, pack.dtype)] + r_shapes,
                  input_output_aliases=r_aliases,
                  scratch_shapes=[pltpu.SemaphoreType.DMA((nd - 1,)), pltpu.SemaphoreType.DMA((nd - 1,))]
                  + r_scratch)(pack, *r_ops)
    return (outs[0], list(outs[1:])) if riders else outs[0]


def _row(v):
    return v.reshape(1, -1)


def _local_step(x, p, tgt, norm_g, conv_w, conv_b, branch_g, ple_norm_g, b_pg, final_g, w_in, w_out, w_pg, w_pe,
                gather=False, reduce=None):
    saved = []
    xl = x
    for l in range(DEPTH):
        riders = [_gather_rider([w_out, w_pg, w_pe], 0, [1, 2, 3])] if gather and l == 0 else []
        h, proj, got = _inproj(xl, _row(norm_g[l]), w_in, l, f"inproj_{l}", riders)
        if riders:
            w_out, w_pg, w_pe = got
        later = gather and l + 1 < DEPTH
        riders = [_gather_rider([w_in, w_pg, w_pe], l + 1, [0, 2, 3])] if later else []
        ya, tl, walked, kept, got = _attn_fwd(proj, f"attn_fwd_{l}", riders)
        if riders:
            w_in, w_pg, w_pe = got
        riders = [_gather_rider([w_out], l + 1, [1])] if later else []
        y, got = _mix_fwd(proj, ya, conv_w[l], _row(conv_b[l]), _row(branch_g[l]), f"mix_fwd_{l}", riders)
        if riders:
            w_out, = got
        x1, hn = _outproj(y, w_out, l, xl, _row(ple_norm_g[l]), f"outproj_{l}")
        x2, gate, e = _ple_fwd(hn, w_pg, _row(b_pg[l]), p, w_pe, l, x1, f"ple_fwd_{l}")
        saved.append((xl, h, proj, ya, tl, walked, kept, y, x1, hn, gate, e))
        xl = x2

    sq, dx, d_final = _loss_head(xl, tgt, _row(final_g), "loss_head")

    big = [None] * DEPTH
    carried = []
    small = {k: [None] * DEPTH for k in ("norm_g", "conv_w", "conv_b", "branch_g", "ple_norm_g", "b_pg")}
    for l in reversed(range(DEPTH)):
        xl, h, proj, ya, tl, walked, kept, y, x1, hn, gate, e = saved[l]
        du, de, dx1, dy, db_pg, d_ple, _ = _ple_bwd(dx, gate, e, x1, w_pg, _row(ple_norm_g[l]), w_out, l,
                                                    f"ple_bwd_{l}")
        sums = carried
        g_pg = _mm_tn(hn, du, f"grad_w_pg_{l}")
        g_pe = _mm_tn(p, de, f"grad_w_pe_{l}", a_layer=l)
        g_out = _mm_tn(y, dx1, f"grad_w_out_{l}")
        others = [g_out, g_pg, g_pe]
        riders = [_swap_rider(others, [1, 2, 3])] if reduce is not None else []
        dpc, d_cw, d_cb, d_bg_c, got = _convmix_bwd(dy, proj, conv_w[l], _row(conv_b[l]), _row(branch_g[l]),
                                                    f"convmix_bwd_{l}", riders)
        if reduce is not None:
            sums += reduce.add(l, others, [1, 2, 3], got)
        riders = [_scatter_rider([sm[3] for sm in sums], [sm[1] for sm in sums])] if sums else []
        dproj, d_bg_a, got = _attn_bwd(proj, dy, ya, tl, walked, kept, _row(branch_g[l]), dpc, f"attn_bwd_{l}", riders)
        if sums:
            reduce.finish(sums, got)
        g_in = _grad_w_in(h, dproj, f"grad_w_in_{l}")
        if reduce is None:
            dx, d_norm, _ = _inproj_bwd(dproj, w_in, l, xl, _row(norm_g[l]), dx1, f"inproj_bwd_{l}")
        elif l > 0:
            dx, d_norm, got = _inproj_bwd(dproj, w_in, l, xl, _row(norm_g[l]), dx1, f"inproj_bwd_{l}",
                                          [_swap_rider([g_in], [0])])
            carried = reduce.add(l, [g_in], [0], got)
        else:
            sums = reduce.add(l, [g_in], [0], _ride_alone(_swap_rider([g_in], [0]), "swap_halves_last"))
            dx, d_norm, got = _inproj_bwd(dproj, w_in, l, xl, _row(norm_g[l]), dx1, f"inproj_bwd_{l}",
                                          [_scatter_rider([sm[3] for sm in sums], [0])])
            reduce.finish(sums, got)
        big[l] = (g_in, g_out, g_pg, g_pe)
        small["norm_g"][l] = jnp.sum(d_norm, axis=0)
        small["conv_w"][l] = jnp.sum(d_cw, axis=1)
        small["conv_b"][l] = jnp.sum(d_cb, axis=0)
        small["branch_g"][l] = jnp.concatenate([jnp.sum(d_bg_c, axis=0), jnp.sum(d_bg_a, axis=0)])
        small["ple_norm_g"][l] = jnp.sum(d_ple, axis=0)
        small["b_pg"][l] = jnp.sum(db_pg, axis=0)
    small = {k: jnp.stack(v) for k, v in small.items()}
    small["final_g"] = jnp.sum(d_final, axis=0)
    return sq[0, 0], dx, big, small


SMALL_ORDER = ("norm_g", "conv_w", "conv_b", "branch_g", "ple_norm_g", "b_pg", "final_g")


def _pack(parts, width):
    flat = jnp.concatenate([v.reshape(-1) for v in parts])
    rows = -(-flat.shape[0] // width)
    rows = -(-rows // 8) * 8
    return jnp.pad(flat, (0, rows * width - flat.shape[0])).reshape(rows, width)


def _unpack(packed, like):
    flat = packed.reshape(-1)
    out, off = [], 0
    for v in like:
        out.append(flat[off:off + v.size].reshape(v.shape))
        off += v.size
    return out


def kernel(x, p, norm_g, w_in, conv_w, conv_b, branch_g, w_out, ple_norm_g, w_pg, b_pg, w_pe, final_g, loss_target, m_norm_g, m_w_in, m_conv_w, m_conv_b, m_branch_g, m_w_out, m_ple_norm_g, m_w_pg, m_b_pg, m_w_pe, m_final_g, v_norm_g, v_w_in, v_conv_w, v_conv_b, v_branch_g, v_w_out, v_ple_norm_g, v_w_pg, v_b_pg, v_w_pe, v_final_g):
    ix, iy, ic = _place()
    chip = 2 * ix + iy
    d = x.shape[-1]

    big_w = (w_in, w_out, w_pg, w_pe)
    own = [_cast_into_full(w, chip, ax, f"cast_{i}") for i, (w, ax) in enumerate(zip(big_w, SHARD_AXES))]
    full_in, = _ride_alone(_gather_rider([own[0]], 0, [0]), "gather_w_in_0")
    full_out, full_pg, full_pe = own[1:]
    cw_shard = conv_w.shape[-1]
    cw_slots = _exchange_small(_pack([conv_w], LANES), "exchange_conv_w")
    conv_full = jnp.concatenate([_unpack(cw_slots[2 * k], [conv_w])[0] for k in range(4)], axis=-1)

    reduce = _GradReduce(chip, ic)
    sq, dx, _, small_g = _local_step(
        x[0], p[:, 0], loss_target[0], norm_g, conv_full, conv_b, branch_g, ple_norm_g, b_pg, final_g,
        full_in, full_out, full_pg, full_pe, gather=True, reduce=reduce)

    g_big = reduce.result()

    parts = [small_g[k] for k in SMALL_ORDER] + [sq.reshape(1)]
    slots = _exchange_small(_pack(parts, d), "exchange_small_grads")
    total = _unpack(_sum_slots(slots, "sum_small"), parts)
    g_small = dict(zip(SMALL_ORDER, total[:-1]))
    loss = 0.5 * total[-1][0] / d
    g_small["conv_w"] = lax.dynamic_slice_in_dim(g_small["conv_w"], chip * cw_shard, cw_shard, axis=2)

    grads = dict(g_small)
    grads.update(w_in=g_big[0], w_out=g_big[1], w_pg=g_big[2], w_pe=g_big[3])
    weights = dict(norm_g=norm_g, w_in=w_in, conv_w=conv_w, conv_b=conv_b, branch_g=branch_g, w_out=w_out,
                   ple_norm_g=ple_norm_g, w_pg=w_pg, b_pg=b_pg, w_pe=w_pe, final_g=final_g)
    ms = dict(norm_g=m_norm_g, w_in=m_w_in, conv_w=m_conv_w, conv_b=m_conv_b, branch_g=m_branch_g, w_out=m_w_out,
              ple_norm_g=m_ple_norm_g, w_pg=m_w_pg, b_pg=m_b_pg, w_pe=m_w_pe, final_g=m_final_g)
    vs = dict(norm_g=v_norm_g, w_in=v_w_in, conv_w=v_conv_w, conv_b=v_conv_b, branch_g=v_branch_g, w_out=v_w_out,
              ple_norm_g=v_ple_norm_g, w_pg=v_w_pg, b_pg=v_b_pg, w_pe=v_w_pe, final_g=v_final_g)
    names = ("norm_g", "w_in", "conv_w", "conv_b", "branch_g", "w_out", "ple_norm_g", "w_pg", "b_pg", "w_pe", "final_g")
    delta, new_m, new_v = {}, {}, {}
    for k in ("w_in", "w_out", "w_pg", "w_pe"):
        shp = weights[k].shape
        two = lambda a: a.reshape(-1, shp[-1])
        gr, dl, mn, vn = _adamw(two(weights[k]), two(grads[k]), two(ms[k]), two(vs[k]), f"adamw_{k}")
        delta[k], new_m[k], new_v[k] = dl.reshape(shp), mn.reshape(shp), vn.reshape(shp)
        grads[k] = gr.reshape(shp)
    like = [weights[k] for k in SMALL_ORDER]
    packs = [_pack([src[k] for k in SMALL_ORDER], d) for src in (weights, grads, ms, vs)]
    outs = _adamw(*packs, "adamw_small")
    for res, o in zip((delta, new_m, new_v), outs[1:]):
        res.update(dict(zip(SMALL_ORDER, _unpack(o, like))))

    return (loss, dx[None], *[grads[k] for k in names], *[delta[k] for k in names],
            *[new_m[k] for k in names], *[new_v[k] for k in names])
```

```python
import math

import jax
import jax.numpy as jnp
from jax import lax
from jax.experimental import pallas as pl
from jax.experimental.pallas import tpu as pltpu

F32 = jnp.float32
BF16 = jnp.bfloat16
EPS = 1e-6
HEAD = 64
LANES = 128
ATT_TK = 256
ATT_TQ = 512
ATT_ROWS = 128
ALIVE_LOG = -105.0
DEPTH = 2
VMEM_LIMIT = 56 * 1024 * 1024
MESH = pl.DeviceIdType.MESH
ANY = pl.BlockSpec(memory_space=pl.ANY)

ADAM_LR = 0.001
ADAM_B1 = 0.9
ADAM_B2 = 0.999
ADAM_EPS = 1e-08
ADAM_WD = 0.01
ADAM_STEP = 10


def _pcall(body, **kw):
    return pl.pallas_call(body, **kw)


def _cp(n_axes):
    return pltpu.CompilerParams(dimension_semantics=("arbitrary",) * n_axes, vmem_limit_bytes=VMEM_LIMIT)


def _tile(n, pref):
    return pref if n % pref == 0 else n


def _split_dot(a, b, passes):
    out = None
    rem = a
    for _ in range(passes):
        hi = rem.astype(BF16)
        t = jnp.dot(hi, b, preferred_element_type=F32)
        out = t if out is None else out + t
        rem = rem - hi.astype(F32)
    return out


def _group_mat():
    r = lax.broadcasted_iota(jnp.int32, (LANES, LANES), 0) // HEAD
    c = lax.broadcasted_iota(jnp.int32, (LANES, LANES), 1) // HEAD
    return jnp.where(r == c, 1.0 / HEAD, 0.0).astype(BF16)


def _group_mean(v, gm):
    return _split_dot(v, gm, 1)


def _sigmoid(z):
    return 1.0 / (1.0 + jnp.exp(-z))


def _dot_nt(a, b):
    return lax.dot_general(a, b, (((1,), (1,)), ((), ())), preferred_element_type=F32)


def _dot_tn(a, b):
    return lax.dot_general(a, b, (((0,), (0,)), ((), ())), preferred_element_type=F32)


def _cast_into_full(w, chip, axis, name):
    _, r, c = w.shape
    tr = _tile(r, 256)
    nb = r // tr
    full = (DEPTH, 4 * r, c) if axis == 0 else (DEPTH, r, 4 * c)

    def body(k_ref, w_ref, o_ref):
        o_ref[...] = w_ref[...].astype(BF16)

    out_map = (lambda l, i, k: (l, k[0] * nb + i, 0)) if axis == 0 else (lambda l, i, k: (l, i, k[0]))
    grid_spec = pltpu.PrefetchScalarGridSpec(
        num_scalar_prefetch=1, grid=(DEPTH, nb),
        in_specs=[pl.BlockSpec((1, tr, c), lambda l, i, k: (l, i, 0))],
        out_specs=pl.BlockSpec((1, tr, c), out_map))
    return _pcall(body, name=name, grid_spec=grid_spec, out_shape=jax.ShapeDtypeStruct(full, BF16),
                  compiler_params=_cp(2))(chip.reshape(1).astype(jnp.int32), w)


def _rms_bwd_rows(dh, xv, g):
    r = lax.rsqrt(jnp.mean(xv * xv, axis=-1, keepdims=True) + EPS)
    xn = xv * r
    dxn = dh * g
    dx = r * (dxn - xn * jnp.mean(dxn * xn, axis=-1, keepdims=True))
    return dx, dh * xn


def _colsum8(v):
    tm, d = v.shape
    return jnp.sum(v.reshape(tm // 8, 8, d), axis=0)


def _inproj(x, g, w, layer, name, riders=()):
    s, d = x.shape
    n = w.shape[2]
    sw = d // 2
    ns = n // sw
    tm = _tile(s, 512)

    def body(*refs):
        own, riders_end = _riders_run(riders, refs, 3, 2, 0, pl.program_id(0), s // tm)
        x_ref, g_ref, w_ref, h_ref, o_ref = own
        xv = x_ref[...]
        r = lax.rsqrt(jnp.mean(xv * xv, axis=-1, keepdims=True) + EPS)
        h = (xv * r * g_ref[...]).astype(BF16)
        h_ref[...] = h
        for k in range(ns):
            o_ref[k] = jnp.dot(h, w_ref[0, :, k * sw:(k + 1) * sw], preferred_element_type=F32).astype(BF16)
        riders_end()

    r_ops, r_shapes, r_scratch, r_aliases = _riders_plumb(riders, 3, 2)
    outs = _pcall(body, name=name, grid=(s // tm,),
                  in_specs=[pl.BlockSpec((tm, d), lambda m: (m, 0)), pl.BlockSpec((1, d), lambda m: (0, 0)),
                            pl.BlockSpec((1, d, n), lambda m: (layer, 0, 0))] + [ANY] * len(r_ops),
                  out_specs=[pl.BlockSpec((tm, d), lambda m: (m, 0)), pl.BlockSpec((ns, tm, sw), lambda m: (0, m, 0))]
                  + [ANY] * len(r_shapes),
                  out_shape=[jax.ShapeDtypeStruct((s, d), BF16), jax.ShapeDtypeStruct((ns, s, sw), BF16)] + r_shapes,
                  input_output_aliases=r_aliases, scratch_shapes=r_scratch,
                  compiler_params=_cp(1))(x, g, w, *r_ops)
    return outs[0], outs[1], list(outs[2:])


def _softplus_parts(z):
    lm = jnp.minimum(-z, 0.0) - jnp.log(1.0 + jnp.exp(-jnp.abs(z)))
    return lm, lm + z


def _attn_tiles(s):
    tk = _tile(s, ATT_TK)
    tq = _tile(s, ATT_TQ)
    return tk, tq, tq // tk, min(ATT_ROWS, tq)


def _diag_work(chains, d, rows, tk):
    work = []
    for n, (_, r0) in enumerate(chains):
        if r0 + rows - 1 <= d * tk:
            continue
        kw = tk // 2 if (tk % 2 == 0 and r0 + rows <= d * tk + tk // 2) else tk
        if r0 >= d * tk + kw:
            mask = None
        else:
            row = lax.broadcasted_iota(jnp.int32, (rows, kw), 0)
            col = lax.broadcasted_iota(jnp.int32, (rows, kw), 1)
            mask = col + d * tk < row + r0
        work.append((n, kw, mask))
    return work


def _static_slots(chains, nd, rows, tk):
    diag, left = {}, {}
    for d in range(nd):
        for n, (_, r0) in enumerate(chains):
            if r0 + rows - 1 > d * tk:
                diag[d, n] = len(diag)
    for n, (_, r0) in enumerate(chains):
        if r0 < tk:
            left[n] = len(diag) + len(left)
    return diag, left, len(diag) + len(left)


def _both(mask, gate):
    if mask is None:
        return gate
    if gate is None:
        return mask
    return jnp.logical_and(mask, gate)


def _any_alive(rsums):
    m = rsums[0]
    for r in rsums[1:]:
        m = jnp.maximum(m, r)
    return jnp.max((m > ALIVE_LOG).astype(jnp.int32))


def _attn_fwd(proj, name, riders=()):
    _, s, sw = proj.shape
    nhp = sw // LANES
    tk, tq, nd, rows = _attn_tiles(s)
    nq = s // tq
    scale = 1.0 / math.sqrt(HEAD)

    def body(*refs):
        i = pl.program_id(1)
        own, riders_end = _riders_run(riders, refs, 3, 5, 1, pl.program_id(0) * nq + i, nhp * nq)
        q_ref, k_ref, v_ref, o_ref, tl_ref, nw_ref, sa_ref, sb_ref, acc_ref = own
        tri = (lax.broadcasted_iota(jnp.int32, (tk, tk), 0) >
               lax.broadcasted_iota(jnp.int32, (tk, tk), 1)).astype(BF16)
        lane = lax.broadcasted_iota(jnp.int32, (tq, LANES), 1)
        q = q_ref[0] * jnp.asarray(scale, BF16)
        qms = [jnp.where((lane // HEAD) == h, q, jnp.zeros_like(q)) for h in range(2)]
        acc_ref[...] = jnp.zeros_like(acc_ref)
        chains = [(h, r0) for h in range(2) for r0 in range(0, tq, rows)]
        qparts = [qms[h][r0:r0 + rows] for h, r0 in chains]

        def block(rsums, tiles, items):
            kjs = [k_ref[0, pl.ds(pl.multiple_of(j * tk, tk), tk), :] for j in tiles]
            vjs = [v_ref[0, pl.ds(pl.multiple_of(j * tk, tk), tk), :] for j in tiles]
            zs = [_dot_nt(qparts[n], kjs[t][:kw]) for n, t, kw, _, _, _ in items]
            lms, lss, css = [], [], []
            for z, (n, t, kw, mask, gate, _) in zip(zs, items):
                lm, ls = _softplus_parts(z)
                keep = _both(mask, gate)
                if keep is not None:
                    lm = jnp.where(keep, lm, 0.0)
                lms.append(lm)
                lss.append(ls)
                css.append(_split_dot(lm, tri[:kw, :kw], 2))
            cur = list(rsums)
            for lm, ls, cs, (n, t, kw, mask, gate, slot) in zip(lms, lss, css, items):
                h, r0 = chains[n]
                a = jnp.exp(ls + (cur[n] + cs))
                keep = _both(mask, gate)
                if keep is not None:
                    a = jnp.where(keep, a, 0.0)
                if slot is not None:
                    sa_ref[0, slot, :, 0:kw] = a.astype(BF16)
                    sb_ref[0, slot, :, 0:kw] = jnp.exp(ls).astype(BF16)
                acc_ref[h, r0:r0 + rows, :] += jnp.dot(a.astype(BF16), vjs[t][:kw], preferred_element_type=F32)
                cur[n] = cur[n] + jnp.sum(lm, axis=1, keepdims=True)
            return tuple(cur)

        everyone = [(n, 0, tk, None, None, None) for n in range(len(chains))]
        dslot, lslot, _ = _static_slots(chains, nd, rows, tk)
        upper = [n for n, (_, r0) in enumerate(chains) if r0 >= tk]
        lower = [n for n, (_, r0) in enumerate(chains) if r0 < tk]
        left = jnp.maximum(i * nd - 1, 0)
        rsums = (jnp.zeros((rows, 1), F32),) * len(chains)
        for d in reversed(range(1, nd)):
            rsums = block(rsums, [i * nd + d],
                          [(n, 0, kw, m, None, dslot[d, n]) for n, kw, m in _diag_work(chains, d, rows, tk)])
        rsums = block(rsums, [i * nd, left],
                      [(n, 0, kw, m, None, dslot[0, n]) for n, kw, m in _diag_work(chains, 0, rows, tk)]
                      + [(n, 1, tk, None, i > 0, lslot[n]) for n in lower])

        if upper:
            too = (i > 0) & (_any_alive([rsums[n] for n in upper]) > 0)
            rsums = lax.cond(too, lambda rs: block(rs, [left], [(n, 0, tk, None, None, None) for n in upper]),
                             lambda rs: rs, rsums)
            too = too.astype(jnp.int32)
        else:
            too = jnp.int32(0)

        def walk(c):
            jj, rs, _ = c
            rs = block(rs, [i * nd - 2 - jj], everyone)
            return jj + 1, rs, _any_alive(rs)

        whole, rsums, _ = lax.while_loop(lambda c: (c[0] < i * nd - 1) & (c[2] > 0), walk,
                                         (jnp.int32(0), rsums, _any_alive(rsums)))
        for n, (h, r0) in enumerate(chains):
            tl_ref[h, r0:r0 + rows, :] = rsums[n]
        nw_ref[0] = (jnp.zeros((8, LANES), jnp.int32) + (2 * whole + too)).astype(F32)
        o_ref[...] = jnp.where(lane < HEAD, acc_ref[0], acc_ref[1]).astype(BF16)
        riders_end()

    r_ops, r_shapes, r_scratch, r_aliases = _riders_plumb(riders, 3, 5)
    nslots = _static_slots([(h, r0) for h in range(2) for r0 in range(0, tq, rows)], nd, rows, tk)[2]
    kept = pl.BlockSpec((1, nslots, rows, tk), lambda hp, i: (hp * nq + i, 0, 0, 0))
    outs = _pcall(
        body, name=name, grid=(nhp, nq),
        in_specs=[pl.BlockSpec((1, tq, LANES), lambda hp, i: (4, i, hp)),
                  pl.BlockSpec((1, s, LANES), lambda hp, i: (5, 0, hp)),
                  pl.BlockSpec((1, s, LANES), lambda hp, i: (6, 0, hp))] + [ANY] * len(r_ops),
        out_specs=[pl.BlockSpec((tq, LANES), lambda hp, i: (i, hp)),
                   pl.BlockSpec((2, tq, 1), lambda hp, i: (hp, i, 0)),
                   pl.BlockSpec((1, 8, LANES), lambda hp, i: (hp * nq + i, 0, 0)), kept, kept]
        + [ANY] * len(r_shapes),
        out_shape=[jax.ShapeDtypeStruct((s, sw), BF16), jax.ShapeDtypeStruct((2 * nhp, s, 1), F32),
                   jax.ShapeDtypeStruct((nhp * nq, 8, LANES), F32)]
        + [jax.ShapeDtypeStruct((nhp * nq, nslots, rows, tk), BF16)] * 2 + r_shapes,
        input_output_aliases=r_aliases,
        scratch_shapes=[pltpu.VMEM((2, tq, LANES), F32)] + r_scratch,
        compiler_params=_cp(2))(proj, proj, proj, *r_ops)
    return outs[0], outs[1], outs[2], (outs[3], outs[4]), list(outs[5:])


def _conv_rows(cc_ref, ch_ref, w_ref, b_ref, r, tc):
    r0 = pl.multiple_of(r * tc, tc)
    u = cc_ref[0, pl.ds(r0, tc), :].astype(F32) * ch_ref[0, pl.ds(r0, tc), :].astype(F32)
    p0 = pl.multiple_of(jnp.maximum(r0 - 16, 0), 16)
    up = cc_ref[0, pl.ds(p0, 16), :].astype(F32) * ch_ref[0, pl.ds(p0, 16), :].astype(F32)
    up = up * (r > 0).astype(F32)
    prev1 = up[15:16, :]
    prev2 = up[14:15, :]
    rid = lax.broadcasted_iota(jnp.int32, u.shape, 0)
    s1 = jnp.where(rid == 0, prev1, pltpu.roll(u, 1, axis=0))
    s2 = jnp.where(rid == 0, prev2, jnp.where(rid == 1, prev1, pltpu.roll(u, 2, axis=0)))
    cv = b_ref[...] + s2 * w_ref[0:1, :] + s1 * w_ref[1:2, :] + u * w_ref[2:3, :]
    return r0, u, s1, s2, cv


def _mix_fwd(proj, ya, conv_w, conv_b, bg, name, riders=()):
    _, s, sw = proj.shape
    nh = sw // LANES
    tc = _tile(s, 256)

    def body(*refs):
        c = pl.program_id(0)
        own, riders_end = _riders_run(riders, refs, 9, 1, 0, c, 2 * nh)
        cb_ref, cc_ref, ch_ref, cz_ref, ya_ref, az_ref, w_ref, b_ref, g_ref, y_ref = own
        gm = _group_mat()

        def finish(r0, yv, zg):
            n = yv * lax.rsqrt(_group_mean(yv * yv, gm) + EPS)
            y_ref[pl.ds(r0, tc), :] = (n * g_ref[...] * (zg * _sigmoid(zg))).astype(BF16)

        @pl.when(c < nh)
        def _():
            def step(r, carry):
                r0, _, _, _, cv = _conv_rows(cc_ref, ch_ref, w_ref, b_ref, r, tc)
                yc = cb_ref[0, pl.ds(r0, tc), :].astype(F32) * cv
                finish(r0, yc, cz_ref[0, pl.ds(r0, tc), :].astype(F32))
                return carry
            lax.fori_loop(0, s // tc, step, 0)

        @pl.when(c >= nh)
        def _():
            def step(r, carry):
                r0 = pl.multiple_of(r * tc, tc)
                finish(r0, ya_ref[pl.ds(r0, tc), :].astype(F32), az_ref[0, pl.ds(r0, tc), :].astype(F32))
                return carry
            lax.fori_loop(0, s // tc, step, 0)

        riders_end()

    def sec(k):
        return pl.BlockSpec((1, s, LANES), lambda c: (k, 0, jnp.minimum(c, nh - 1)))

    r_ops, r_shapes, r_scratch, r_aliases = _riders_plumb(riders, 9, 1)
    outs = _pcall(
        body, name=name, grid=(2 * nh,),
        in_specs=[sec(0), sec(1), sec(2), sec(3),
                  pl.BlockSpec((s, LANES), lambda c: (0, jnp.maximum(c - nh, 0))),
                  pl.BlockSpec((1, s, LANES), lambda c: (7, 0, jnp.maximum(c - nh, 0))),
                  pl.BlockSpec((3, LANES), lambda c: (0, jnp.minimum(c, nh - 1))),
                  pl.BlockSpec((1, LANES), lambda c: (0, jnp.minimum(c, nh - 1))),
                  pl.BlockSpec((1, LANES), lambda c: (0, c))] + [ANY] * len(r_ops),
        out_specs=[pl.BlockSpec((s, LANES), lambda c: (0, c))] + [ANY] * len(r_shapes),
        out_shape=[jax.ShapeDtypeStruct((s, 2 * sw), BF16)] + r_shapes,
        input_output_aliases=r_aliases, scratch_shapes=r_scratch, compiler_params=_cp(1),
    )(proj, proj, proj, proj, ya, proj, conv_w, conv_b, bg, *r_ops)
    return outs[0], list(outs[1:])


def _outproj(y, w, layer, x, g, name):
    s, d = x.shape
    tm = _tile(s, 512)

    def body(y_ref, w_ref, x_ref, g_ref, x1_ref, hn_ref):
        x1 = x_ref[...] + jnp.dot(y_ref[...], w_ref[0], preferred_element_type=F32)
        x1_ref[...] = x1
        r = lax.rsqrt(jnp.mean(x1 * x1, axis=-1, keepdims=True) + EPS)
        hn_ref[...] = (x1 * r * g_ref[...]).astype(BF16)

    row = lambda m: (m, 0)
    fix = lambda m: (0, 0)
    return _pcall(body, name=name, grid=(s // tm,),
                  in_specs=[pl.BlockSpec((tm, d), row), pl.BlockSpec((1, d, d), lambda m: (layer, 0, 0)),
                            pl.BlockSpec((tm, d), row), pl.BlockSpec((1, d), fix)],
                  out_specs=[pl.BlockSpec((tm, d), row), pl.BlockSpec((tm, d), row)],
                  out_shape=[jax.ShapeDtypeStruct((s, d), F32), jax.ShapeDtypeStruct((s, d), BF16)],
                  compiler_params=_cp(1))(y, w, x, g)


def _ple_fwd(hn, w_pg, b_pg, p, w_pe, layer, x1, name):
    s, d = x1.shape
    pd = p.shape[2]
    tm = _tile(s, 512)

    def body(hn_ref, wg_ref, b_ref, p_ref, we_ref, x1_ref, x2_ref, gate_ref, e_ref):
        gate = _sigmoid(jnp.dot(hn_ref[...], wg_ref[0], preferred_element_type=F32) + b_ref[...])
        e = jnp.dot(p_ref[0].astype(BF16), we_ref[0], preferred_element_type=F32)
        x2_ref[...] = x1_ref[...] + gate * e
        gate_ref[...] = gate.astype(BF16)
        e_ref[...] = e.astype(BF16)

    row = lambda m: (m, 0)
    fix = lambda m: (0, 0)
    return _pcall(body, name=name, grid=(s // tm,),
                  in_specs=[pl.BlockSpec((tm, d), row), pl.BlockSpec((1, d, d), lambda m: (layer, 0, 0)),
                            pl.BlockSpec((1, d), fix), pl.BlockSpec((1, tm, pd), lambda m: (layer, m, 0)),
                            pl.BlockSpec((1, pd, d), lambda m: (layer, 0, 0)), pl.BlockSpec((tm, d), row)],
                  out_specs=[pl.BlockSpec((tm, d), row)] * 3,
                  out_shape=[jax.ShapeDtypeStruct((s, d), F32), jax.ShapeDtypeStruct((s, d), BF16),
                             jax.ShapeDtypeStruct((s, d), BF16)],
                  compiler_params=_cp(1))(hn, w_pg, b_pg, p, w_pe, x1)


def _loss_head(x, tgt, g, name):
    s, d = x.shape
    tm = _tile(s, 512)

    def body(x_ref, t_ref, g_ref, l_ref, dx_ref, dg_ref):
        m = pl.program_id(0)

        @pl.when(m == 0)
        def _():
            l_ref[...] = jnp.zeros_like(l_ref)
            dg_ref[...] = jnp.zeros_like(dg_ref)

        xv = x_ref[...]
        gv = g_ref[...]
        r = lax.rsqrt(jnp.mean(xv * xv, axis=-1, keepdims=True) + EPS)
        xn = xv * r
        err = xn * gv - t_ref[...]
        l_ref[...] += jnp.sum(err * err)
        dy = err * (1.0 / d)
        dxn = dy * gv
        dx_ref[...] = r * (dxn - xn * jnp.mean(dxn * xn, axis=-1, keepdims=True))
        dg_ref[...] += _colsum8(dy * xn)

    row = lambda m: (m, 0)
    fix = lambda m: (0, 0)
    return _pcall(body, name=name, grid=(s // tm,),
                  in_specs=[pl.BlockSpec((tm, d), row), pl.BlockSpec((tm, d), row), pl.BlockSpec((1, d), fix)],
                  out_specs=[pl.BlockSpec((8, LANES), fix), pl.BlockSpec((tm, d), row), pl.BlockSpec((8, d), fix)],
                  out_shape=[jax.ShapeDtypeStruct((8, LANES), F32), jax.ShapeDtypeStruct((s, d), F32),
                             jax.ShapeDtypeStruct((8, d), F32)],
                  compiler_params=_cp(1))(x, tgt, g)


def _ple_bwd(dx2, gate, e, x1, w_pg, g_ple, w_out, layer, name, riders=()):
    s, d = dx2.shape
    tm = _tile(s, 512)

    def body(*refs):
        m = pl.program_id(0)
        own, riders_end = _riders_run(riders, refs, 7, 6, 0, m, s // tm)
        (dx2_ref, gate_ref, e_ref, x1_ref, wg_ref, g_ref, wo_ref,
         du_ref, de_ref, dx1_ref, dy_ref, db_ref, dg_ref) = own

        @pl.when(m == 0)
        def _():
            db_ref[...] = jnp.zeros_like(db_ref)
            dg_ref[...] = jnp.zeros_like(dg_ref)

        dx2v = dx2_ref[...]
        gate = gate_ref[...].astype(F32)
        du = dx2v * e_ref[...].astype(F32) * gate * (1.0 - gate)
        de_ref[...] = (dx2v * gate).astype(BF16)
        dub = du.astype(BF16)
        du_ref[...] = dub
        db_ref[...] += _colsum8(du)
        dhn = _dot_nt(dub, wg_ref[0])
        dxr, dgr = _rms_bwd_rows(dhn, x1_ref[...], g_ref[...])
        dx1 = dx2v + dxr
        dx1_ref[...] = dx1
        dg_ref[...] += _colsum8(dgr)
        dy_ref[...] = _dot_nt(dx1.astype(BF16), wo_ref[0]).astype(BF16)
        riders_end()

    row = lambda m: (m, 0)
    fix = lambda m: (0, 0)
    t = pl.BlockSpec((tm, d), row)
    r_ops, r_shapes, r_scratch, r_aliases = _riders_plumb(riders, 7, 6)
    outs = _pcall(body, name=name, grid=(s // tm,),
                  in_specs=[t, t, t, t, pl.BlockSpec((1, d, d), lambda m: (layer, 0, 0)), pl.BlockSpec((1, d), fix),
                            pl.BlockSpec((1, d, d), lambda m: (layer, 0, 0))] + [ANY] * len(r_ops),
                  out_specs=[t, t, t, t, pl.BlockSpec((8, d), fix), pl.BlockSpec((8, d), fix)] + [ANY] * len(r_shapes),
                  out_shape=[jax.ShapeDtypeStruct((s, d), BF16), jax.ShapeDtypeStruct((s, d), BF16),
                             jax.ShapeDtypeStruct((s, d), F32), jax.ShapeDtypeStruct((s, d), BF16),
                             jax.ShapeDtypeStruct((8, d), F32), jax.ShapeDtypeStruct((8, d), F32)] + r_shapes,
                  input_output_aliases=r_aliases, scratch_shapes=r_scratch,
                  compiler_params=_cp(1))(dx2, gate, e, x1, w_pg, g_ple, w_out, *r_ops)
    return tuple(outs[:6]) + (list(outs[6:]),)


def _mm_tn(a, b, name, a_layer=None):
    s, ka = a.shape[-2:]
    n = b.shape[1]
    tn = _tile(n, 1024)
    ns = n // tn
    tk = _tile(s, 2048)
    nk = s // tk

    def body(a_ref, b_ref, o_ref, acc_ref):
        k = pl.program_id(1)

        @pl.when(k == 0)
        def _():
            acc_ref[...] = jnp.zeros_like(acc_ref)

        av = a_ref[...] if a_layer is None else a_ref[0]
        acc_ref[...] += _dot_tn(av.astype(BF16), b_ref[...].astype(BF16))

        @pl.when(k == nk - 1)
        def _():
            o_ref[...] = acc_ref[...]

    a_spec = (pl.BlockSpec((tk, ka), lambda j, k: (k, 0)) if a_layer is None
              else pl.BlockSpec((1, tk, ka), lambda j, k: (a_layer, k, 0)))
    return _pcall(body, name=name, grid=(ns, nk),
                  in_specs=[a_spec, pl.BlockSpec((tk, tn), lambda j, k: (k, j))],
                  out_specs=pl.BlockSpec((ka, tn), lambda j, k: (0, j)),
                  out_shape=jax.ShapeDtypeStruct((ka, n), F32),
                  scratch_shapes=[pltpu.VMEM((ka, tn), F32)], compiler_params=_cp(2))(a, b)


def _norm_gate_bwd(dy, yv, zg, g, gm):
    r = lax.rsqrt(_group_mean(yv * yv, gm) + EPS)
    n = yv * r
    sg = _sigmoid(zg)
    sil = zg * sg
    dzg = dy * n * g * (sg * (1.0 + zg * (1.0 - sg)))
    dn = dy * g * sil
    dyv = r * (dn - n * _group_mean(dn * n, gm))
    return dyv, dzg, dy * n * sil


def _convmix_bwd(dy, proj, conv_w, conv_b, bg, name, riders=()):
    _, s, sw = proj.shape
    nh = sw // LANES
    tc = _tile(s, 256)
    nr = s // tc

    def body(*refs):
        own, riders_end = _riders_run(riders, refs, 8, 4, 1, pl.program_id(0), nh)
        (dy_ref, cb_ref, cc_ref, ch_ref, cz_ref, w_ref, b_ref, g_ref,
         dp_ref, dw_ref, db_ref, dg_ref, dcv_ref) = own
        gm = _group_mat()
        dcv_ref[pl.ds(s, 8), :] = jnp.zeros((8, LANES), F32)

        def pass1(r, carry):
            dw0, dw1, dw2, db, dg = carry
            r0, u, s1, s2, cv = _conv_rows(cc_ref, ch_ref, w_ref, b_ref, r, tc)
            cb = cb_ref[0, pl.ds(r0, tc), :].astype(F32)
            dyc, dcz, dgr = _norm_gate_bwd(dy_ref[pl.ds(r0, tc), :].astype(F32), cb * cv,
                                           cz_ref[0, pl.ds(r0, tc), :].astype(F32), g_ref[...], gm)
            dp_ref[0, pl.ds(r0, tc), :] = (dyc * cv).astype(BF16)
            dp_ref[3, pl.ds(r0, tc), :] = dcz.astype(BF16)
            dcv = dyc * cb
            dcv_ref[pl.ds(r0, tc), :] = dcv
            return (dw0 + _colsum8(dcv * s2), dw1 + _colsum8(dcv * s1), dw2 + _colsum8(dcv * u),
                    db + _colsum8(dcv), dg + _colsum8(dgr))

        z8 = jnp.zeros((8, LANES), F32)
        dw0, dw1, dw2, db, dg = lax.fori_loop(0, nr, pass1, (z8, z8, z8, z8, z8))
        dw_ref[0] = dw0
        dw_ref[1] = dw1
        dw_ref[2] = dw2
        db_ref[...] = db
        dg_ref[...] = dg

        def pass2(r, carry):
            r0 = pl.multiple_of(r * tc, tc)
            dcv = dcv_ref[pl.ds(r0, tc), :]
            nxt = dcv_ref[pl.ds(pl.multiple_of(r0 + tc, 8), 8), :]
            rid = lax.broadcasted_iota(jnp.int32, dcv.shape, 0)
            n1 = jnp.where(rid == tc - 1, nxt[0:1, :], pltpu.roll(dcv, tc - 1, axis=0))
            n2 = jnp.where(rid == tc - 1, nxt[1:2, :],
                           jnp.where(rid == tc - 2, nxt[0:1, :], pltpu.roll(dcv, tc - 2, axis=0)))
            du = dcv * w_ref[2:3, :] + n1 * w_ref[1:2, :] + n2 * w_ref[0:1, :]
            dp_ref[1, pl.ds(r0, tc), :] = (du * ch_ref[0, pl.ds(r0, tc), :].astype(F32)).astype(BF16)
            dp_ref[2, pl.ds(r0, tc), :] = (du * cc_ref[0, pl.ds(r0, tc), :].astype(F32)).astype(BF16)
            return carry

        lax.fori_loop(0, nr, pass2, 0)
        riders_end()

    def sec(k):
        return pl.BlockSpec((1, s, LANES), lambda c: (k, 0, c))

    col = lambda c: (0, c)
    r_ops, r_shapes, r_scratch, r_aliases = _riders_plumb(riders, 8, 4)
    outs = _pcall(
        body, name=name, grid=(nh,),
        in_specs=[pl.BlockSpec((s, LANES), col), sec(0), sec(1), sec(2), sec(3),
                  pl.BlockSpec((3, LANES), col), pl.BlockSpec((1, LANES), col), pl.BlockSpec((1, LANES), col)]
        + [ANY] * len(r_ops),
        out_specs=[pl.BlockSpec((4, s, LANES), lambda c: (0, 0, c)), pl.BlockSpec((3, 8, LANES), lambda c: (0, 0, c)),
                   pl.BlockSpec((8, LANES), col), pl.BlockSpec((8, LANES), col)] + [ANY] * len(r_shapes),
        out_shape=[jax.ShapeDtypeStruct((8, s, sw), BF16), jax.ShapeDtypeStruct((3, 8, sw), F32),
                   jax.ShapeDtypeStruct((8, sw), F32), jax.ShapeDtypeStruct((8, sw), F32)] + r_shapes,
        input_output_aliases=r_aliases,
        scratch_shapes=[pltpu.VMEM((s + 8, LANES), F32)] + r_scratch, compiler_params=_cp(1),
    )(dy, proj, proj, proj, proj, conv_w, conv_b, bg, *r_ops)
    return tuple(outs[:4]) + (list(outs[4:]),)


def _attn_bwd(proj, dy, ya, tl, walked, kept, bg, buf, name, riders=()):
    _, s, sw = proj.shape
    nhp = sw // LANES
    tk, t, nd, rows_c = _attn_tiles(s)
    nq = s // t
    scale = 1.0 / math.sqrt(HEAD)

    def body(*refs):
        step = pl.program_id(1)
        i = nq - 1 - step
        own, riders_end = _riders_run(riders, refs, 12, 2, 3, pl.program_id(0) * nq + step, nhp * nq)
        (q_ref, k_ref, v_ref, az_ref, dy_ref, ya_ref, tl_ref, nw_ref, g_ref, buf_ref, sa_ref, sb_ref, out_ref, dg_ref,
         dka_ref, dva_ref, dqa_ref) = own

        @pl.when(step == 0)
        def _():
            dka_ref[...] = jnp.zeros_like(dka_ref)
            dva_ref[...] = jnp.zeros_like(dva_ref)
            dg_ref[...] = jnp.zeros_like(dg_ref)

        dyv, dzg, dgr = _norm_gate_bwd(dy_ref[...].astype(F32), ya_ref[...].astype(F32), az_ref[0].astype(F32),
                                       g_ref[...], _group_mat())
        out_ref[3] = dzg.astype(BF16)
        dg_ref[...] += _colsum8(dgr)

        tri = (lax.broadcasted_iota(jnp.int32, (tk, tk), 0) <=
               lax.broadcasted_iota(jnp.int32, (tk, tk), 1)).astype(BF16)
        lane = lax.broadcasted_iota(jnp.int32, (t, LANES), 1)
        q = q_ref[0] * jnp.asarray(scale, BF16)
        do = dyv.astype(BF16)
        qms = [jnp.where((lane // HEAD) == h, q, jnp.zeros_like(q)) for h in range(2)]
        doms = [jnp.where((lane // HEAD) == h, do, jnp.zeros_like(do)) for h in range(2)]
        dqa_ref[...] = jnp.zeros_like(dqa_ref)
        chains = [(h, r0) for h in range(2) for r0 in range(0, t, rows_c)]
        qparts = [qms[h][r0:r0 + rows_c] for h, r0 in chains]
        doparts = [doms[h][r0:r0 + rows_c] for h, r0 in chains]
        tots = [tl_ref[h, r0:r0 + rows_c, :] for h, r0 in chains]

        def block(carry, tiles, items):
            k0s = [pl.multiple_of(j * tk, tk) for j in tiles]
            kjs = [k_ref[0, pl.ds(k0, tk), :] for k0 in k0s]
            vjs = [v_ref[0, pl.ds(k0, tk), :] for k0 in k0s]
            zs = [_dot_nt(qparts[n], kjs[t][:kw]) for n, t, kw, _, _ in items]
            das = [_dot_nt(doparts[n], vjs[t][:kw]) for n, t, kw, _, _ in items]
            keeps = [_both(mask, gate) for _, _, _, mask, gate in items]
            lms, lss, cls = [], [], []
            for z, keep, (n, t, kw, _, _) in zip(zs, keeps, items):
                lm, ls = _softplus_parts(z)
                if keep is not None:
                    lm = jnp.where(keep, lm, 0.0)
                lms.append(lm)
                lss.append(ls)
                cls.append(_split_dot(lm, tri[:kw, :kw], 2))
            cur = list(carry)
            psums, abs_, gs, cgs = [], [], [], []
            for lm, ls, cl, da, keep, (n, t, kw, _, _) in zip(lms, lss, cls, das, keeps, items):
                psum, gsum = cur[n]
                a = jnp.exp(ls + (tots[n] - psum - cl))
                if keep is not None:
                    a = jnp.where(keep, a, 0.0)
                g = a * da
                psums.append(gsum)
                gs.append(g)
                abs_.append(a.astype(BF16))
                cgs.append(_split_dot(g, tri[:kw, :kw], 1))
                cur[n] = (psum + jnp.sum(lm, axis=1, keepdims=True), gsum + jnp.sum(g, axis=1, keepdims=True))
            dks, dvs = {}, {}
            for ls, a, g, cg, gsum, keep, (n, t, kw, _, _) in zip(lss, abs_, gs, cgs, psums, keeps, items):
                h, r0 = chains[n]
                dz = g - jnp.exp(ls) * (gsum + cg)
                if keep is not None:
                    dz = jnp.where(keep, dz, 0.0)
                dz = dz.astype(BF16)
                dqa_ref[h, r0:r0 + rows_c, :] += jnp.dot(dz, kjs[t][:kw], preferred_element_type=F32)
                dkh = _dot_tn(dz, qparts[n])
                dvh = _dot_tn(a, doparts[n])
                dks[t, kw] = dkh if (t, kw) not in dks else dks[t, kw] + dkh
                dvs[t, kw] = dvh if (t, kw) not in dvs else dvs[t, kw] + dvh
            for t, kw in dks:
                dka_ref[pl.ds(k0s[t], kw), :] += dks[t, kw]
                dva_ref[pl.ds(k0s[t], kw), :] += dvs[t, kw]
            return tuple(cur)

        def kept_block(carry, tiles, items):
            k0s = [pl.multiple_of(j * tk, tk) for j in tiles]
            kjs = [k_ref[0, pl.ds(k0, tk), :] for k0 in k0s]
            vjs = [v_ref[0, pl.ds(k0, tk), :] for k0 in k0s]
            das = [_dot_nt(doparts[n], vjs[t][:kw]) for n, t, kw, _, _, _ in items]
            cur = list(carry)
            gsums, kept_a, gs, cgs = [], [], [], []
            for da, (n, t, kw, _, _, slot) in zip(das, items):
                psum, gsum = cur[n]
                a = sa_ref[0, slot, :, 0:kw]
                g = a.astype(F32) * da
                gsums.append(gsum)
                kept_a.append(a)
                gs.append(g)
                cgs.append(_split_dot(g, tri[:kw, :kw], 1))
                cur[n] = (psum, gsum + jnp.sum(g, axis=1, keepdims=True))
            dks, dvs = {}, {}
            for a, g, cg, gsum, (n, t, kw, mask, gate, slot) in zip(kept_a, gs, cgs, gsums, items):
                h, r0 = chains[n]
                dz = g - sb_ref[0, slot, :, 0:kw].astype(F32) * (gsum + cg)
                keep = _both(mask, gate)
                if keep is not None:
                    dz = jnp.where(keep, dz, 0.0)
                dz = dz.astype(BF16)
                dqa_ref[h, r0:r0 + rows_c, :] += jnp.dot(dz, kjs[t][:kw], preferred_element_type=F32)
                dkh = _dot_tn(dz, qparts[n])
                dvh = _dot_tn(a, doparts[n])
                dks[t, kw] = dkh if (t, kw) not in dks else dks[t, kw] + dkh
                dvs[t, kw] = dvh if (t, kw) not in dvs else dvs[t, kw] + dvh
            for t, kw in dks:
                dka_ref[pl.ds(k0s[t], kw), :] += dks[t, kw]
                dva_ref[pl.ds(k0s[t], kw), :] += dvs[t, kw]
            return tuple(cur)

        z1 = jnp.zeros((rows_c, 1), F32)
        everyone = [(n, 0, tk, None, None) for n in range(len(chains))]
        dslot, lslot, _ = _static_slots(chains, nd, rows_c, tk)
        upper = [n for n, (_, r0) in enumerate(chains) if r0 >= tk]
        lower = [n for n, (_, r0) in enumerate(chains) if r0 < tk]
        left = jnp.maximum(i * nd - 1, 0)
        code = jnp.clip(jnp.max(nw_ref[0].astype(jnp.int32)), 0, 2 * left + 1)
        too = jnp.where(i > 0, code % 2, 0)
        whole = jnp.minimum(code // 2, left)
        carry = lax.fori_loop(left - whole, left, lambda j, c: block(c, [j], everyone), ((z1, z1),) * len(chains))
        if upper:
            carry = lax.cond(too > 0, lambda c: block(c, [left], [(n, 0, tk, None, None) for n in upper]),
                             lambda c: c, carry)
        carry = kept_block(carry, [left, i * nd],
                           [(n, 0, tk, None, i > 0, lslot[n]) for n in lower]
                           + [(n, 1, kw, m, None, dslot[0, n]) for n, kw, m in _diag_work(chains, 0, rows_c, tk)])
        for d in range(1, nd):
            carry = kept_block(carry, [i * nd + d],
                               [(n, 0, kw, m, None, dslot[d, n]) for n, kw, m in _diag_work(chains, d, rows_c, tk)])
        out_ref[0] = (jnp.where(lane < HEAD, dqa_ref[0], dqa_ref[1]) * scale).astype(BF16)
        own = pl.multiple_of(i * t, t)
        out_ref[1] = dka_ref[pl.ds(own, t), :].astype(BF16)
        out_ref[2] = dva_ref[pl.ds(own, t), :].astype(BF16)
        riders_end()

    def rows(sec):
        return pl.BlockSpec((1, t, LANES), lambda hp, st: (sec, nq - 1 - st, hp))

    def whole(sec):
        return pl.BlockSpec((1, s, LANES), lambda hp, st: (sec, 0, hp))

    r_ops, r_shapes, r_scratch, r_aliases = _riders_plumb(riders, 12, 2)
    kept_spec = pl.BlockSpec((1,) + kept[0].shape[1:], lambda hp, st: (hp * nq + nq - 1 - st, 0, 0, 0))
    outs = _pcall(
        body, name=name, grid=(nhp, nq),
        in_specs=[rows(4), whole(5), whole(6), rows(7),
                  pl.BlockSpec((t, LANES), lambda hp, st: (nq - 1 - st, hp + nhp)),
                  pl.BlockSpec((t, LANES), lambda hp, st: (nq - 1 - st, hp)),
                  pl.BlockSpec((2, t, 1), lambda hp, st: (hp, nq - 1 - st, 0)),
                  pl.BlockSpec((1, 8, LANES), lambda hp, st: (hp * nq + nq - 1 - st, 0, 0)),
                  pl.BlockSpec((1, LANES), lambda hp, st: (0, hp + nhp)), ANY, kept_spec, kept_spec]
        + [ANY] * len(r_ops),
        out_specs=[pl.BlockSpec((4, t, LANES), lambda hp, st: (1, nq - 1 - st, hp)),
                   pl.BlockSpec((8, LANES), lambda hp, st: (0, hp))] + [ANY] * len(r_shapes),
        out_shape=[jax.ShapeDtypeStruct(buf.shape, buf.dtype), jax.ShapeDtypeStruct((8, sw), F32)] + r_shapes,
        input_output_aliases={9: 0, **r_aliases},
        scratch_shapes=[pltpu.VMEM((s, LANES), F32), pltpu.VMEM((s, LANES), F32), pltpu.VMEM((2, t, LANES), F32)]
        + r_scratch,
        compiler_params=_cp(2))(proj, proj, proj, proj, dy, ya, tl, walked, bg, buf, kept[0], kept[1], *r_ops)
    return outs[0], outs[1], list(outs[2:])


def _grad_w_in(h, dproj, name):
    s, d = h.shape
    ns, _, sw = dproj.shape

    def body(h_ref, b_ref, o_ref, ht_ref):
        @pl.when(pl.program_id(0) == 0)
        def _():
            ht_ref[...] = h_ref[...].T

        o_ref[...] = jnp.dot(ht_ref[...], b_ref[0], preferred_element_type=F32)

    return _pcall(body, name=name, grid=(ns,),
                  in_specs=[pl.BlockSpec((s, d), lambda j: (0, 0)), pl.BlockSpec((1, s, sw), lambda j: (j, 0, 0))],
                  out_specs=pl.BlockSpec((d, sw), lambda j: (0, j)),
                  out_shape=jax.ShapeDtypeStruct((d, ns * sw), F32),
                  scratch_shapes=[pltpu.VMEM((d, s), BF16)], compiler_params=_cp(1))(h, dproj)


def _inproj_bwd(dproj, w, layer, x, g, dx1, name, riders=()):
    ns, s, sw = dproj.shape
    d = x.shape[1]
    tm = _tile(s, 512)

    def body(*refs):
        own, riders_end = _riders_run(riders, refs, 5, 2, 0, pl.program_id(0), s // tm)
        dp_ref, w_ref, x_ref, g_ref, dx1_ref, dx_ref, dg_ref = own

        @pl.when(pl.program_id(0) == 0)
        def _():
            dg_ref[...] = jnp.zeros_like(dg_ref)

        dh = _dot_nt(dp_ref[0], w_ref[0, :, 0:sw])
        for k in range(1, ns):
            dh = dh + _dot_nt(dp_ref[k], w_ref[0, :, k * sw:(k + 1) * sw])
        dxr, dgr = _rms_bwd_rows(dh, x_ref[...], g_ref[...])
        dx_ref[...] = dx1_ref[...] + dxr
        dg_ref[...] += _colsum8(dgr)
        riders_end()

    row = lambda m: (m, 0)
    fix = lambda m: (0, 0)
    r_ops, r_shapes, r_scratch, r_aliases = _riders_plumb(riders, 5, 2)
    outs = _pcall(body, name=name, grid=(s // tm,),
                  in_specs=[pl.BlockSpec((ns, tm, sw), lambda m: (0, m, 0)),
                            pl.BlockSpec((1, d, ns * sw), lambda m: (layer, 0, 0)),
                            pl.BlockSpec((tm, d), row), pl.BlockSpec((1, d), fix), pl.BlockSpec((tm, d), row)]
                  + [ANY] * len(r_ops),
                  out_specs=[pl.BlockSpec((tm, d), row), pl.BlockSpec((8, d), fix)] + [ANY] * len(r_shapes),
                  out_shape=[jax.ShapeDtypeStruct((s, d), F32), jax.ShapeDtypeStruct((8, d), F32)] + r_shapes,
                  input_output_aliases=r_aliases, scratch_shapes=r_scratch,
                  compiler_params=_cp(1))(dproj, w, x, g, dx1, *r_ops)
    return outs[0], outs[1], list(outs[2:])


def _adamw(w, g, m, v, name):
    r, c = w.shape
    tr = _tile(r, 256)
    c1 = 1.0 - ADAM_B1 ** ADAM_STEP
    c2 = 1.0 - ADAM_B2 ** ADAM_STEP

    def body(w_ref, g_ref, m_ref, v_ref, go_ref, d_ref, mo_ref, vo_ref):
        gv = g_ref[...]
        go_ref[...] = gv
        mn = ADAM_B1 * m_ref[...] + (1.0 - ADAM_B1) * gv
        vn = ADAM_B2 * v_ref[...] + (1.0 - ADAM_B2) * (gv * gv)
        d_ref[...] = -ADAM_LR * ((mn / c1) / (jnp.sqrt(vn / c2) + ADAM_EPS) + ADAM_WD * w_ref[...])
        mo_ref[...] = mn
        vo_ref[...] = vn

    t = pl.BlockSpec((tr, c), lambda i: (i, 0))
    return _pcall(body, name=name, grid=(r // tr,), in_specs=[t] * 4, out_specs=[t] * 4,
                  out_shape=[jax.ShapeDtypeStruct((r, c), F32)] * 4, compiler_params=_cp(1))(w, g, m, v)


def _add_half(grad, other, core, a, name):
    hr, hc = other.shape
    tr = _tile(hr, 256)
    nb = hr // tr

    def body(c_ref, g_ref, o_ref, out_ref, outb_ref):
        v = g_ref[...] + o_ref[...]
        out_ref[...] = v
        outb_ref[...] = v.astype(BF16)

    t = pl.BlockSpec((tr, hc), lambda i, c: (i, 0))
    own = (lambda i, c: (c[0] * nb + i, 0)) if HALF_AXES[a] == 0 else (lambda i, c: (i, c[0]))
    grid_spec = pltpu.PrefetchScalarGridSpec(
        num_scalar_prefetch=1, grid=(nb,), in_specs=[pl.BlockSpec((tr, hc), own), t], out_specs=[t, t])
    return _pcall(body, name=name, grid_spec=grid_spec,
                  out_shape=[jax.ShapeDtypeStruct((hr, hc), F32), jax.ShapeDtypeStruct((hr, hc), BF16)],
                  compiler_params=_cp(1))(core.reshape(1).astype(jnp.int32), grad, other)


def _sum_half(wide, parts, chip, core, layer, a, stack, name):
    _, sr, sc = parts.shape
    tr = _tile(sr, 256)
    nbs = sr // tr

    def body(k_ref, f_ref, p_ref, *rest):
        rest[-1][0] = ((f_ref[...] + p_ref[0].astype(F32)) + p_ref[1].astype(F32)) + p_ref[2].astype(F32)

    f_map = (lambda i, k: (i, k[0])) if SHARD_AXES[a] == 1 else (lambda i, k: (k[0] * nbs + i, 0))
    if HALF_AXES[a] == 0:
        shape, o_map = (DEPTH, 2 * sr, sc), (lambda i, k: (layer, k[1] * nbs + i, 0))
    else:
        shape, o_map = (DEPTH, sr, 2 * sc), (lambda i, k: (layer, i, k[1]))
    in_specs = [pl.BlockSpec((tr, sc), f_map), pl.BlockSpec((3, tr, sc), lambda i, k: (0, i, 0))]
    args = [wide, parts]
    aliases = {}
    if stack is not None:
        in_specs.append(ANY)
        args.append(stack)
        aliases = {3: 0}
    grid_spec = pltpu.PrefetchScalarGridSpec(
        num_scalar_prefetch=1, grid=(nbs,), in_specs=in_specs, out_specs=pl.BlockSpec((1, tr, sc), o_map))
    return _pcall(body, name=name, grid_spec=grid_spec, out_shape=jax.ShapeDtypeStruct(shape, F32),
                  input_output_aliases=aliases,
                  compiler_params=_cp(1))(jnp.stack([chip, core]).astype(jnp.int32), *args)


def _sum_slots(slots, name):
    n = slots.shape[0]

    def body(s_ref, o_ref):
        acc = s_ref[0]
        for i in range(1, n):
            acc = acc + s_ref[i]
        o_ref[...] = acc

    return _pcall(body, name=name, out_shape=jax.ShapeDtypeStruct(slots.shape[1:], F32))(slots)


def _place():
    return lax.axis_index("x"), lax.axis_index("y"), lax.axis_index("c")


def _shard_view(ref, axis, chip, size):
    if axis == 0:
        return ref.at[pl.ds(chip * size, size), :]
    return ref.at[:, pl.ds(chip * size, size)]


SHARD_AXES = (1, 0, 0, 1)
HALF_AXES = tuple(1 - ax for ax in SHARD_AXES)


class _Rider:
    def __init__(self, operands, out_shape, sems, phases, aliased=False):
        self.operands, self.out_shape, self.sems = list(operands), list(out_shape), list(sems)
        self.phases, self.aliased = phases, aliased


def _riders_plumb(riders, n_in, n_out):
    ops, out_shape, scratch, aliases = [], [], [], {}
    for r in riders:
        if r.aliased:
            for k in range(len(r.operands)):
                aliases[n_in + len(ops) + k] = n_out + len(out_shape) + k
        ops += r.operands
        out_shape += r.out_shape
        scratch += r.sems
    return ops, out_shape, scratch, aliases


def _riders_run(riders, refs, n_in, n_out, n_scr, step, nsteps):
    n_rin = sum(len(r.operands) for r in riders)
    n_rout = sum(len(r.out_shape) for r in riders)
    rin = refs[n_in:n_in + n_rin]
    o0 = n_in + n_rin
    rout = refs[o0 + n_out:o0 + n_out + n_rout]
    s0 = o0 + n_out + n_rout
    rsem = refs[s0 + n_scr:]
    own = list(refs[:n_in]) + list(refs[o0:o0 + n_out]) + list(refs[s0:s0 + n_scr])
    lasts = []
    for r in riders:
        ph = r.phases(rin[:len(r.operands)], rout[:len(r.out_shape)], rsem[:len(r.sems)])
        rin, rout, rsem = rin[len(r.operands):], rout[len(r.out_shape):], rsem[len(r.sems):]
        pl.when(step == 0)(ph[0])
        for mid in ph[1:-1]:
            pl.when(step == (3 * nsteps) // 4)(mid)
        lasts.append(ph[-1])

    def finish():
        for last in lasts:
            pl.when(step == nsteps - 1)(last)

    return own, finish


def _gather_phases(ins, outs, ssem, rsem, layer, which):
    n = len(ins)
    x, y, c = _place()
    me = 2 * x + y
    chips = [(1 - x, y), (x, 1 - y), (1 - x, 1 - y)]

    def piece(a, chip, half, of):
        ax = SHARD_AXES[which[a]]
        block = _shard_view(of[a].at[layer], ax, chip, of[a].shape[1 + ax] // 4)
        r = block.shape[0] // 2
        return block.at[pl.ds(half * r, r), :]

    def over_ici(a, j):
        cx, cy = chips[j]
        return pltpu.make_async_remote_copy(
            src_ref=piece(a, me, c, ins), dst_ref=piece(a, me, c, outs), send_sem=ssem.at[a, j],
            recv_sem=rsem.at[a, j], device_id=(cx, cy, c), device_id_type=MESH)

    def landed(a, j, half):
        cx, cy = chips[j]
        return piece(a, 2 * cx + cy, half, outs)

    def to_sibling(a, j):
        got = landed(a, j, c)
        return pltpu.make_async_remote_copy(
            src_ref=got, dst_ref=got, send_sem=ssem.at[a, 3 + j], recv_sem=rsem.at[a, 3 + j],
            device_id=(x, y, 1 - c), device_id_type=MESH)

    def wait_arrival(a, k, place):
        pltpu.make_async_remote_copy(src_ref=place, dst_ref=place, send_sem=ssem.at[a, k], recv_sem=rsem.at[a, k],
                                     device_id=(x, y, c), device_id_type=MESH).wait_recv()

    def start():
        for a in range(n):
            for j in range(3):
                over_ici(a, j).start()

    def pass_on():
        for a in range(n):
            for j in range(3):
                wait_arrival(a, j, landed(a, j, c))
                to_sibling(a, j).start()

    def finish():
        for a in range(n):
            for j in range(3):
                wait_arrival(a, 3 + j, landed(a, j, 1 - c))
        for a in range(n):
            for j in range(3):
                over_ici(a, j).wait_send()
                to_sibling(a, j).wait_send()

    return start, pass_on, finish


def _gather_rider(fulls, layer, which):
    n = len(fulls)
    return _Rider(fulls, [jax.ShapeDtypeStruct(f.shape, f.dtype) for f in fulls],
                  [pltpu.SemaphoreType.DMA((n, 6)), pltpu.SemaphoreType.DMA((n, 6))],
                  lambda ins, outs, sems: _gather_phases(ins, outs, sems[0], sems[1], layer, which), aliased=True)


def _ride_alone(rider, name):
    n = len(rider.operands)

    def body(*refs):
        for phase in rider.phases(refs[:n], refs[n:n + len(rider.out_shape)], refs[n + len(rider.out_shape):]):
            phase()

    return _pcall(body, name=name, in_specs=[ANY] * n, out_specs=[ANY] * len(rider.out_shape),
                  out_shape=rider.out_shape, scratch_shapes=rider.sems,
                  input_output_aliases={a: a for a in range(n)} if rider.aliased else {})(*rider.operands)


def _half_view(ref, a, half):
    n = ref.shape[HALF_AXES[a]] // 2
    if HALF_AXES[a] == 0:
        return ref.at[pl.ds(half * n, n), :]
    return ref.at[:, pl.ds(half * n, n)]


def _swap_rider(grads, which):
    n = len(grads)
    halves = []
    for g, w in zip(grads, which):
        sh = list(g.shape)
        sh[HALF_AXES[w]] //= 2
        halves.append(jax.ShapeDtypeStruct(tuple(sh), g.dtype))

    def phases(srcs, outs, sems):
        x, y, c = _place()

        def copy(a):
            return pltpu.make_async_remote_copy(
                src_ref=_half_view(srcs[a], which[a], 1 - c), dst_ref=outs[a], send_sem=sems[0].at[a],
                recv_sem=sems[1].at[a], device_id=(x, y, 1 - c), device_id_type=MESH)

        def start():
            for a in range(n):
                copy(a).start()

        def finish():
            for a in range(n):
                copy(a).wait()

        return start, finish

    return _Rider(grads, halves, [pltpu.SemaphoreType.DMA((n,)), pltpu.SemaphoreType.DMA((n,))], phases)


def _scatter_rider(sums, which):
    n = len(sums)
    shapes = []
    for f, w in zip(sums, which):
        sh = list(f.shape)
        sh[SHARD_AXES[w]] //= 4
        shapes.append(jax.ShapeDtypeStruct((3,) + tuple(sh), f.dtype))

    def phases(srcs, outs, sems):
        x, y, c = _place()
        chips = [(1 - x, y), (x, 1 - y), (1 - x, 1 - y)]

        def copy(a, j):
            cx, cy = chips[j]
            ax = SHARD_AXES[which[a]]
            src = _shard_view(srcs[a], ax, 2 * cx + cy, srcs[a].shape[ax] // 4)
            return pltpu.make_async_remote_copy(src_ref=src, dst_ref=outs[a].at[j], send_sem=sems[0].at[a, j],
                                                recv_sem=sems[1].at[a, j], device_id=(cx, cy, c), device_id_type=MESH)

        def start():
            for a in range(n):
                for j in range(3):
                    copy(a, j).start()

        def finish():
            for a in range(n):
                for j in range(3):
                    copy(a, j).wait()

        return start, finish

    return _Rider(sums, shapes, [pltpu.SemaphoreType.DMA((n, 3)), pltpu.SemaphoreType.DMA((n, 3))], phases)


def _pair_halves(stacks):
    n = len(stacks)

    def body(*refs):
        ins, outs = refs[:n], refs[n:2 * n]
        ssem, rsem = refs[2 * n:]
        x, y, c = _place()
        cps = [pltpu.make_async_remote_copy(
            src_ref=_half_view(ins[a].at[l], a, c), dst_ref=_half_view(outs[a].at[l], a, c), send_sem=ssem.at[a, l],
            recv_sem=rsem.at[a, l], device_id=(x, y, 1 - c), device_id_type=MESH)
            for a in range(n) for l in range(DEPTH)]
        for cp in cps:
            cp.start()
        for a in range(n):
            for l in range(DEPTH):
                got = _half_view(outs[a].at[l], a, 1 - c)
                pltpu.make_async_remote_copy(src_ref=got, dst_ref=got, send_sem=ssem.at[a, l], recv_sem=rsem.at[a, l],
                                             device_id=(x, y, 1 - c), device_id_type=MESH).wait_recv()
        for cp in cps:
            cp.wait_send()

    return _pcall(body, name="pair_halves", in_specs=[ANY] * n, out_specs=[ANY] * n,
                  out_shape=[jax.ShapeDtypeStruct(st.shape, st.dtype) for st in stacks],
                  input_output_aliases={a: a for a in range(n)},
                  scratch_shapes=[pltpu.SemaphoreType.DMA((n, DEPTH)), pltpu.SemaphoreType.DMA((n, DEPTH))])(*stacks)


class _GradReduce:
    def __init__(self, chip, core):
        self.chip, self.core = chip, core
        self.stacks = [None] * len(SHARD_AXES)

    def add(self, layer, grads, which, got):
        return [(layer, w) + tuple(_add_half(g, o, self.core, w, f"add_half_{layer}_{w}"))
                for g, o, w in zip(grads, got, which)]

    def finish(self, sums, partials):
        for (layer, w, wide, _), pr in zip(sums, partials):
            self.stacks[w] = _sum_half(wide, pr, self.chip, self.core, layer, w, self.stacks[w], f"sum_half_{layer}_{w}")

    def result(self):
        return _pair_halves(self.stacks)


def _exchange_small(pack, name, riders=()):
    nd = 8

    def body(*refs):
        own, riders_end = _riders_run(riders, refs, 1, 1, 2, jnp.int32(0), 1)
        p_ref, o_ref, ssem, rsem = own
        x, y, c = _place()
        me = 4 * x + 2 * y + c
        o_ref[me] = p_ref[...]
        cps = []
        for j in range(1, nd):
            px, py, pc = x ^ (j >> 2), y ^ ((j >> 1) & 1), c ^ (j & 1)
            cps.append(pltpu.make_async_remote_copy(
                src_ref=p_ref, dst_ref=o_ref.at[me], send_sem=ssem.at[j - 1], recv_sem=rsem.at[j - 1],
                device_id=(px, py, pc), device_id_type=MESH))
        for cp in cps:
            cp.start()
        for j in range(1, nd):
            peer = me ^ j
            got = o_ref.at[peer]
            pltpu.make_async_remote_copy(src_ref=got, dst_ref=got, send_sem=ssem.at[j - 1], recv_sem=rsem.at[j - 1],
                                         device_id=(x, y, c), device_id_type=MESH).wait_recv()
        for cp in cps:
            cp.wait_send()
        riders_end()

    vm = pl.BlockSpec(memory_space=pltpu.VMEM)
    r_ops, r_shapes, r_scratch, r_aliases = _riders_plumb(riders, 1, 1)
    outs = _pcall(body, name=name, in_specs=[vm] + [ANY] * len(r_ops), out_specs=[vm] + [ANY] * len(r_shapes),
                  out_shape=[jax.ShapeDtypeStruct((nd,) + pack.shape, pack.dtype)] + r_shapes,
                  input_output_aliases=r_aliases,
                  scratch_shapes=[pltpu.SemaphoreType.DMA((nd - 1,)), pltpu.SemaphoreType.DMA((nd - 1,))]
                  + r_scratch)(pack, *r_ops)
    return (outs[0], list(outs[1:])) if riders else outs[0]


def _row(v):
    return v.reshape(1, -1)


def _local_step(x, p, tgt, norm_g, conv_w, conv_b, branch_g, ple_norm_g, b_pg, final_g, w_in, w_out, w_pg, w_pe,
                gather=False, reduce=None):
    saved = []
    xl = x
    for l in range(DEPTH):
        riders = [_gather_rider([w_out, w_pg, w_pe], 0, [1, 2, 3])] if gather and l == 0 else []
        h, proj, got = _inproj(xl, _row(norm_g[l]), w_in, l, f"inproj_{l}", riders)
        if riders:
            w_out, w_pg, w_pe = got
        later = gather and l + 1 < DEPTH
        riders = [_gather_rider([w_in, w_pg, w_pe], l + 1, [0, 2, 3])] if later else []
        ya, tl, walked, kept, got = _attn_fwd(proj, f"attn_fwd_{l}", riders)
        if riders:
            w_in, w_pg, w_pe = got
        riders = [_gather_rider([w_out], l + 1, [1])] if later else []
        y, got = _mix_fwd(proj, ya, conv_w[l], _row(conv_b[l]), _row(branch_g[l]), f"mix_fwd_{l}", riders)
        if riders:
            w_out, = got
        x1, hn = _outproj(y, w_out, l, xl, _row(ple_norm_g[l]), f"outproj_{l}")
        x2, gate, e = _ple_fwd(hn, w_pg, _row(b_pg[l]), p, w_pe, l, x1, f"ple_fwd_{l}")
        saved.append((xl, h, proj, ya, tl, walked, kept, y, x1, hn, gate, e))
        xl = x2

    sq, dx, d_final = _loss_head(xl, tgt, _row(final_g), "loss_head")

    big = [None] * DEPTH
    carried = []
    small = {k: [None] * DEPTH for k in ("norm_g", "conv_w", "conv_b", "branch_g", "ple_norm_g", "b_pg")}
    for l in reversed(range(DEPTH)):
        xl, h, proj, ya, tl, walked, kept, y, x1, hn, gate, e = saved[l]
        du, de, dx1, dy, db_pg, d_ple, _ = _ple_bwd(dx, gate, e, x1, w_pg, _row(ple_norm_g[l]), w_out, l,
                                                    f"ple_bwd_{l}")
        sums = carried
        g_pg = _mm_tn(hn, du, f"grad_w_pg_{l}")
        g_pe = _mm_tn(p, de, f"grad_w_pe_{l}", a_layer=l)
        g_out = _mm_tn(y, dx1, f"grad_w_out_{l}")
        others = [g_out, g_pg, g_pe]
        riders = [_swap_rider(others, [1, 2, 3])] if reduce is not None else []
        dpc, d_cw, d_cb, d_bg_c, got = _convmix_bwd(dy, proj, conv_w[l], _row(conv_b[l]), _row(branch_g[l]),
                                                    f"convmix_bwd_{l}", riders)
        if reduce is not None:
            sums += reduce.add(l, others, [1, 2, 3], got)
        riders = [_scatter_rider([sm[3] for sm in sums], [sm[1] for sm in sums])] if sums else []
        dproj, d_bg_a, got = _attn_bwd(proj, dy, ya, tl, walked, kept, _row(branch_g[l]), dpc, f"attn_bwd_{l}", riders)
        if sums:
            reduce.finish(sums, got)
        g_in = _grad_w_in(h, dproj, f"grad_w_in_{l}")
        if reduce is None:
            dx, d_norm, _ = _inproj_bwd(dproj, w_in, l, xl, _row(norm_g[l]), dx1, f"inproj_bwd_{l}")
        elif l > 0:
            dx, d_norm, got = _inproj_bwd(dproj, w_in, l, xl, _row(norm_g[l]), dx1, f"inproj_bwd_{l}",
                                          [_swap_rider([g_in], [0])])
            carried = reduce.add(l, [g_in], [0], got)
        else:
            sums = reduce.add(l, [g_in], [0], _ride_alone(_swap_rider([g_in], [0]), "swap_halves_last"))
            dx, d_norm, got = _inproj_bwd(dproj, w_in, l, xl, _row(norm_g[l]), dx1, f"inproj_bwd_{l}",
                                          [_scatter_rider([sm[3] for sm in sums], [0])])
            reduce.finish(sums, got)
        big[l] = (g_in, g_out, g_pg, g_pe)
        small["norm_g"][l] = jnp.sum(d_norm, axis=0)
        small["conv_w"][l] = jnp.sum(d_cw, axis=1)
        small["conv_b"][l] = jnp.sum(d_cb, axis=0)
        small["branch_g"][l] = jnp.concatenate([jnp.sum(d_bg_c, axis=0), jnp.sum(d_bg_a, axis=0)])
        small["ple_norm_g"][l] = jnp.sum(d_ple, axis=0)
        small["b_pg"][l] = jnp.sum(db_pg, axis=0)
    small = {k: jnp.stack(v) for k, v in small.items()}
    small["final_g"] = jnp.sum(d_final, axis=0)
    return sq[0, 0], dx, big, small


SMALL_ORDER = ("norm_g", "conv_w", "conv_b", "branch_g", "ple_norm_g", "b_pg", "final_g")


def _pack(parts, width):
    flat = jnp.concatenate([v.reshape(-1) for v in parts])
    rows = -(-flat.shape[0] // width)
    rows = -(-rows // 8) * 8
    return jnp.pad(flat, (0, rows * width - flat.shape[0])).reshape(rows, width)


def _unpack(packed, like):
    flat = packed.reshape(-1)
    out, off = [], 0
    for v in like:
        out.append(flat[off:off + v.size].reshape(v.shape))
        off += v.size
    return out


def kernel(x, p, norm_g, w_in, conv_w, conv_b, branch_g, w_out, ple_norm_g, w_pg, b_pg, w_pe, final_g, loss_target, m_norm_g, m_w_in, m_conv_w, m_conv_b, m_branch_g, m_w_out, m_ple_norm_g, m_w_pg, m_b_pg, m_w_pe, m_final_g, v_norm_g, v_w_in, v_conv_w, v_conv_b, v_branch_g, v_w_out, v_ple_norm_g, v_w_pg, v_b_pg, v_w_pe, v_final_g):
    ix, iy, ic = _place()
    chip = 2 * ix + iy
    d = x.shape[-1]

    big_w = (w_in, w_out, w_pg, w_pe)
    own = [_cast_into_full(w, chip, ax, f"cast_{i}") for i, (w, ax) in enumerate(zip(big_w, SHARD_AXES))]
    full_in, = _ride_alone(_gather_rider([own[0]], 0, [0]), "gather_w_in_0")
    full_out, full_pg, full_pe = own[1:]
    cw_shard = conv_w.shape[-1]
    cw_slots = _exchange_small(_pack([conv_w], LANES), "exchange_conv_w")
    conv_full = jnp.concatenate([_unpack(cw_slots[2 * k], [conv_w])[0] for k in range(4)], axis=-1)

    reduce = _GradReduce(chip, ic)
    sq, dx, _, small_g = _local_step(
        x[0], p[:, 0], loss_target[0], norm_g, conv_full, conv_b, branch_g, ple_norm_g, b_pg, final_g,
        full_in, full_out, full_pg, full_pe, gather=True, reduce=reduce)

    g_big = reduce.result()

    parts = [small_g[k] for k in SMALL_ORDER] + [sq.reshape(1)]
    slots = _exchange_small(_pack(parts, d), "exchange_small_grads")
    total = _unpack(_sum_slots(slots, "sum_small"), parts)
    g_small = dict(zip(SMALL_ORDER, total[:-1]))
    loss = 0.5 * total[-1][0] / d
    g_small["conv_w"] = lax.dynamic_slice_in_dim(g_small["conv_w"], chip * cw_shard, cw_shard, axis=2)

    grads = dict(g_small)
    grads.update(w_in=g_big[0], w_out=g_big[1], w_pg=g_big[2], w_pe=g_big[3])
    weights = dict(norm_g=norm_g, w_in=w_in, conv_w=conv_w, conv_b=conv_b, branch_g=branch_g, w_out=w_out,
                   ple_norm_g=ple_norm_g, w_pg=w_pg, b_pg=b_pg, w_pe=w_pe, final_g=final_g)
    ms = dict(norm_g=m_norm_g, w_in=m_w_in, conv_w=m_conv_w, conv_b=m_conv_b, branch_g=m_branch_g, w_out=m_w_out,
              ple_norm_g=m_ple_norm_g, w_pg=m_w_pg, b_pg=m_b_pg, w_pe=m_w_pe, final_g=m_final_g)
    vs = dict(norm_g=v_norm_g, w_in=v_w_in, conv_w=v_conv_w, conv_b=v_conv_b, branch_g=v_branch_g, w_out=v_w_out,
              ple_norm_g=v_ple_norm_g, w_pg=v_w_pg, b_pg=v_b_pg, w_pe=v_w_pe, final_g=v_final_g)
    names = ("norm_g", "w_in", "conv_w", "conv_b", "branch_g", "w_out", "ple_norm_g", "w_pg", "b_pg", "w_pe", "final_g")
    delta, new_m, new_v = {}, {}, {}
    for k in ("w_in", "w_out", "w_pg", "w_pe"):
        shp = weights[k].shape
        two = lambda a: a.reshape(-1, shp[-1])
        gr, dl, mn, vn = _adamw(two(weights[k]), two(grads[k]), two(ms[k]), two(vs[k]), f"adamw_{k}")
        delta[k], new_m[k], new_v[k] = dl.reshape(shp), mn.reshape(shp), vn.reshape(shp)
        grads[k] = gr.reshape(shp)
    like = [weights[k] for k in SMALL_ORDER]
    packs = [_pack([src[k] for k in SMALL_ORDER], d) for src in (weights, grads, ms, vs)]
    outs = _adamw(*packs, "adamw_small")
    for res, o in zip((delta, new_m, new_v), outs[1:]):
        res.update(dict(zip(SMALL_ORDER, _unpack(o, like))))

    return (loss, dx[None], *[grads[k] for k in names], *[delta[k] for k in names],
            *[new_m[k] for k in names], *[new_v[k] for k in names])
```
